```python
import jax, jax.numpy as jnp
from jax import lax
import numpy as np

D_MODEL = 1024
BATCH = 8
SEQ = 8192
DEPTH = 2

CHUNK = 64
SB_HEADS = 8
SB_HEAD_DIM = D_MODEL // 16
SB_WIDTH = SB_HEADS * SB_HEAD_DIM
SB_BLOCK = 128
HG_HEAD_DIM = D_MODEL // 8
HG_HEADS = 4
HG_WIDTH = HG_HEADS * HG_HEAD_DIM
HG_CHUNK = CHUNK // 4
MIX_WIDTH = SB_WIDTH + HG_WIDTH
IN_WIDTH = 3 * SB_WIDTH + 4 * HG_WIDTH
SPLIT_POINTS = (SB_WIDTH, 2 * SB_WIDTH, 3 * SB_WIDTH,
                3 * SB_WIDTH + HG_WIDTH, 3 * SB_WIDTH + 2 * HG_WIDTH,
                3 * SB_WIDTH + 3 * HG_WIDTH)
CONV_WIDTH = 31
D_FF = 4 * D_MODEL
N_AB = (DEPTH + 1) // 2
N_CONV = DEPTH // 2
RMS_EPS = 1e-6
LN_EPS = 1e-5

kernel_name = "hybrid_stickbreak_hgrn2_conformer_trunk"


def rmsnorm(x, g):
    xf = x.astype(jnp.float32)
    y = xf * lax.rsqrt(jnp.mean(xf * xf, axis=-1, keepdims=True) + RMS_EPS)
    return (y * g).astype(x.dtype)


def stick_breaking_attention(q, k, v):
    b, s, h, dh = q.shape
    nb = s // SB_BLOCK
    scale = dh ** -0.5
    kh = k.transpose(0, 2, 1, 3)
    vh = v.transpose(0, 2, 1, 3)
    q_blocks = jnp.moveaxis(q.transpose(0, 2, 1, 3).reshape(b, h, nb, SB_BLOCK, dh), 2, 0)
    key_pos = jnp.arange(s)

    def one_block(args):
        qb, blk = args
        z = jnp.einsum('bhqd,bhkd->bhqk', qb, kh).astype(jnp.float32) * scale
        q_pos = blk * SB_BLOCK + jnp.arange(SB_BLOCK)
        mask = key_pos[None, :] < q_pos[:, None]
        log_beta = jax.nn.log_sigmoid(z)
        log_keep = jnp.where(mask, log_beta - z, 0.0)
        log_keep_after = lax.cumsum(log_keep, axis=3, reverse=True) - log_keep
        w = jnp.where(mask, jnp.exp(log_beta + log_keep_after), 0.0)
        return jnp.einsum('bhqk,bhkd->bhqd', w.astype(vh.dtype), vh)

    out = lax.map(one_block, (q_blocks, jnp.arange(nb)))
    out = jnp.moveaxis(out, 0, 2).reshape(b, h, s, dh).transpose(0, 2, 1, 3)
    return out.reshape(b, s, h * dh)


def hgrn2(q, f_raw, i, gate, lb, norm_g):
    b, s, h, d = q.shape
    n = s // HG_CHUNK
    f32 = jnp.float32
    f = lb + (1.0 - lb) * jax.nn.sigmoid(f_raw.astype(f32))
    g = jnp.log(f)
    kk = 1.0 - f

    def chunked(t):
        return t.transpose(0, 2, 1, 3).reshape(b, h, n, HG_CHUNK, d)

    qc, gc, kc, vc = chunked(q.astype(f32)), chunked(g), chunked(kk), chunked(i.astype(f32))
    G = jnp.cumsum(gc, axis=3)
    G_last = G[:, :, :, -1:, :]
    q_dec = qc * jnp.exp(G)
    k_intra = kc * jnp.exp(-G)
    k_state = kc * jnp.exp(G_last - G)
    decay = jnp.exp(G_last[:, :, :, 0, :])
    causal = jnp.tril(jnp.ones((HG_CHUNK, HG_CHUNK), dtype=bool))
    scores = jnp.where(causal, jnp.einsum('bhncd,bhnsd->bhncs', q_dec, k_intra), 0.0)
    o_intra = jnp.einsum('bhncs,bhnsv->bhncv', scores, vc)

    def step(state, xs):
        qd, ks, v, dec = xs
        o = jnp.einsum('bhcd,bhdv->bhcv', qd, state)
        state = state * dec[..., None] + jnp.einsum('bhcd,bhcv->bhdv', ks, v)
        return state, o

    xs = (jnp.moveaxis(q_dec, 2, 0), jnp.moveaxis(k_state, 2, 0),
          jnp.moveaxis(vc, 2, 0), jnp.moveaxis(decay, 2, 0))
    _, o_inter = lax.scan(step, jnp.zeros((b, h, d, d), f32), xs)
    o = o_intra + jnp.moveaxis(o_inter, 0, 2)
    o = o.reshape(b, h, s, d).transpose(0, 2, 1, 3)
    o = o * lax.rsqrt(jnp.mean(o * o, axis=-1, keepdims=True) + RMS_EPS) * norm_g
    o = o * jax.nn.silu(gate.astype(f32))
    return o.reshape(b, s, h * d).astype(q.dtype)


def parallel_ab_mixer(u, w_in, w_out, lb, hg_norm_g):
    b, s, _ = u.shape
    proj = u @ w_in
    sb_q, sb_k, sb_v, hg_q, hg_f, hg_i, hg_g = jnp.split(proj, SPLIT_POINTS, axis=-1)
    sb_shape = (b, s, SB_HEADS, SB_HEAD_DIM)
    hg_shape = (b, s, HG_HEADS, HG_HEAD_DIM)
    o_sb = stick_breaking_attention(sb_q.reshape(sb_shape), sb_k.reshape(sb_shape), sb_v.reshape(sb_shape))
    o_hg = hgrn2(hg_q.reshape(hg_shape), hg_f.reshape(hg_shape), hg_i.reshape(hg_shape),
                 hg_g.reshape(hg_shape), lb.reshape(HG_HEADS, HG_HEAD_DIM), hg_norm_g)
    return jnp.concatenate([o_sb, o_hg], axis=-1) @ w_out


def conformer_conv(u, w_glu, b_glu, w_dw, b_dw, ln_g, ln_b, w_pw, b_pw):
    a = jax.nn.glu(u @ w_glu + b_glu, axis=-1)
    y = lax.conv_general_dilated(a, w_dw[:, None, :], window_strides=(1,),
                                 padding=[(CONV_WIDTH - 1, 0)],
                                 dimension_numbers=('NWC', 'WIO', 'NWC'),
                                 feature_group_count=D_MODEL) + b_dw
    yf = y.astype(jnp.float32)
    mu = jnp.mean(yf, axis=-1, keepdims=True)
    var = jnp.mean(jnp.square(yf - mu), axis=-1, keepdims=True)
    y = jax.nn.silu((yf - mu) * lax.rsqrt(var + LN_EPS) * ln_g + ln_b).astype(u.dtype)
    return y @ w_pw + b_pw


def squared_relu_mlp(u, w1, w2):
    return jnp.square(jax.nn.relu(u @ w1)) @ w2


def _fwd_setup_inputs(seed: int = 0) -> dict:
    key = jax.random.key(seed)
    ks = jax.random.split(key, 20)
    nrm = jax.random.normal
    f32 = jnp.float32
    return {
        "x": nrm(ks[0], (BATCH, SEQ, D_MODEL), f32),
        "norm_mix_g": 1.0 + 0.02 * nrm(ks[1], (DEPTH, D_MODEL), f32),
        "norm_ffn_g": 1.0 + 0.02 * nrm(ks[2], (DEPTH, D_MODEL), f32),
        "w_in_ab": nrm(ks[3], (N_AB, D_MODEL, IN_WIDTH), f32) * D_MODEL ** -0.5,
        "w_out_ab": nrm(ks[4], (N_AB, MIX_WIDTH, D_MODEL), f32) * MIX_WIDTH ** -0.5,
        "hg_lb_logits": 0.1 * nrm(ks[5], (DEPTH + 1, HG_WIDTH), f32),
        "hg_norm_g": 1.0 + 0.02 * nrm(ks[6], (N_AB, HG_HEADS, HG_HEAD_DIM), f32),
        "conv_w_glu": nrm(ks[7], (N_CONV, D_MODEL, 2 * D_MODEL), f32) * D_MODEL ** -0.5,
        "conv_b_glu": 0.02 * nrm(ks[8], (N_CONV, 2 * D_MODEL), f32),
        "conv_w_dw": nrm(ks[9], (N_CONV, CONV_WIDTH, D_MODEL), f32) * CONV_WIDTH ** -0.5,
        "conv_b_dw": 0.02 * nrm(ks[10], (N_CONV, D_MODEL), f32),
        "conv_ln_g": 1.0 + 0.02 * nrm(ks[11], (N_CONV, D_MODEL), f32),
        "conv_ln_b": 0.02 * nrm(ks[12], (N_CONV, D_MODEL), f32),
        "conv_w_pw": nrm(ks[13], (N_CONV, D_MODEL, D_MODEL), f32) * D_MODEL ** -0.5,
        "conv_b_pw": 0.02 * nrm(ks[14], (N_CONV, D_MODEL), f32),
        "w_ff1": nrm(ks[15], (DEPTH, D_MODEL, D_FF), f32) * D_MODEL ** -0.5,
        "w_ff2": nrm(ks[16], (DEPTH, D_FF, D_MODEL), f32) * D_FF ** -0.5,
        "final_norm_g": 1.0 + 0.02 * nrm(ks[17], (D_MODEL,), f32),
    }


def _fwd_reference(x, norm_mix_g, norm_ffn_g, w_in_ab, w_out_ab, hg_lb_logits, hg_norm_g,
              conv_w_glu, conv_b_glu, conv_w_dw, conv_b_dw, conv_ln_g, conv_ln_b,
              conv_w_pw, conv_b_pw, w_ff1, w_ff2, final_norm_g):
    lower_bounds = jnp.cumsum(jax.nn.softmax(hg_lb_logits.astype(jnp.float32), axis=0), axis=0)
    h = x
    for layer in range(DEPTH):
        j = layer // 2
        u = rmsnorm(h, norm_mix_g[layer])
        if layer % 2 == 0:
            h = h + parallel_ab_mixer(u, w_in_ab[j], w_out_ab[j], lower_bounds[layer], hg_norm_g[j])
        else:
            h = h + conformer_conv(u, conv_w_glu[j], conv_b_glu[j], conv_w_dw[j], conv_b_dw[j],
                                   conv_ln_g[j], conv_ln_b[j], conv_w_pw[j], conv_b_pw[j])
        u = rmsnorm(h, norm_ffn_g[layer])
        h = h + squared_relu_mlp(u, w_ff1[layer], w_ff2[layer])
    return rmsnorm(h, final_norm_g)


import jax as _jax
import jax.numpy as _jnp

TWIN_FORMAT = 'train_step'
FWD_PARAMS = ['x', 'norm_mix_g', 'norm_ffn_g', 'w_in_ab', 'w_out_ab', 'hg_lb_logits', 'hg_norm_g', 'conv_w_glu', 'conv_b_glu', 'conv_w_dw', 'conv_b_dw', 'conv_ln_g', 'conv_ln_b', 'conv_w_pw', 'conv_b_pw', 'w_ff1', 'w_ff2', 'final_norm_g']
TWIN_WEIGHTS = ['norm_mix_g', 'norm_ffn_g', 'w_in_ab', 'w_out_ab', 'hg_lb_logits', 'hg_norm_g', 'conv_w_glu', 'conv_b_glu', 'conv_w_dw', 'conv_b_dw', 'conv_ln_g', 'conv_ln_b', 'conv_w_pw', 'conv_b_pw', 'w_ff1', 'w_ff2', 'final_norm_g']
TWIN_DIFF_INPUT = 'x'
TWIN_INPUTS = ['x', 'norm_mix_g', 'norm_ffn_g', 'w_in_ab', 'w_out_ab', 'hg_lb_logits', 'hg_norm_g', 'conv_w_glu', 'conv_b_glu', 'conv_w_dw', 'conv_b_dw', 'conv_ln_g', 'conv_ln_b', 'conv_w_pw', 'conv_b_pw', 'w_ff1', 'w_ff2', 'final_norm_g', 'loss_target', 'm_norm_mix_g', 'm_norm_ffn_g', 'm_w_in_ab', 'm_w_out_ab', 'm_hg_lb_logits', 'm_hg_norm_g', 'm_conv_w_glu', 'm_conv_b_glu', 'm_conv_w_dw', 'm_conv_b_dw', 'm_conv_ln_g', 'm_conv_ln_b', 'm_conv_w_pw', 'm_conv_b_pw', 'm_w_ff1', 'm_w_ff2', 'm_final_norm_g', 'v_norm_mix_g', 'v_norm_ffn_g', 'v_w_in_ab', 'v_w_out_ab', 'v_hg_lb_logits', 'v_hg_norm_g', 'v_conv_w_glu', 'v_conv_b_glu', 'v_conv_w_dw', 'v_conv_b_dw', 'v_conv_ln_g', 'v_conv_ln_b', 'v_conv_w_pw', 'v_conv_b_pw', 'v_w_ff1', 'v_w_ff2', 'v_final_norm_g']
TWIN_OUTPUTS = ['loss', 'grad_x', 'grad_norm_mix_g', 'grad_norm_ffn_g', 'grad_w_in_ab', 'grad_w_out_ab', 'grad_hg_lb_logits', 'grad_hg_norm_g', 'grad_conv_w_glu', 'grad_conv_b_glu', 'grad_conv_w_dw', 'grad_conv_b_dw', 'grad_conv_ln_g', 'grad_conv_ln_b', 'grad_conv_w_pw', 'grad_conv_b_pw', 'grad_w_ff1', 'grad_w_ff2', 'grad_final_norm_g', 'delta_norm_mix_g', 'delta_norm_ffn_g', 'delta_w_in_ab', 'delta_w_out_ab', 'delta_hg_lb_logits', 'delta_hg_norm_g', 'delta_conv_w_glu', 'delta_conv_b_glu', 'delta_conv_w_dw', 'delta_conv_b_dw', 'delta_conv_ln_g', 'delta_conv_ln_b', 'delta_conv_w_pw', 'delta_conv_b_pw', 'delta_w_ff1', 'delta_w_ff2', 'delta_final_norm_g', 'new_m_norm_mix_g', 'new_m_norm_ffn_g', 'new_m_w_in_ab', 'new_m_w_out_ab', 'new_m_hg_lb_logits', 'new_m_hg_norm_g', 'new_m_conv_w_glu', 'new_m_conv_b_glu', 'new_m_conv_w_dw', 'new_m_conv_b_dw', 'new_m_conv_ln_g', 'new_m_conv_ln_b', 'new_m_conv_w_pw', 'new_m_conv_b_pw', 'new_m_w_ff1', 'new_m_w_ff2', 'new_m_final_norm_g', 'new_v_norm_mix_g', 'new_v_norm_ffn_g', 'new_v_w_in_ab', 'new_v_w_out_ab', 'new_v_hg_lb_logits', 'new_v_hg_norm_g', 'new_v_conv_w_glu', 'new_v_conv_b_glu', 'new_v_conv_w_dw', 'new_v_conv_b_dw', 'new_v_conv_ln_g', 'new_v_conv_ln_b', 'new_v_conv_w_pw', 'new_v_conv_b_pw', 'new_v_w_ff1', 'new_v_w_ff2', 'new_v_final_norm_g']
TWIN_LEAF_KINDS = {'loss': 'loss', 'grad_x': 'grad_x', 'grad_norm_mix_g': 'grad_w', 'grad_norm_ffn_g': 'grad_w', 'grad_w_in_ab': 'grad_w', 'grad_w_out_ab': 'grad_w', 'grad_hg_lb_logits': 'grad_w', 'grad_hg_norm_g': 'grad_w', 'grad_conv_w_glu': 'grad_w', 'grad_conv_b_glu': 'grad_w', 'grad_conv_w_dw': 'grad_w', 'grad_conv_b_dw': 'grad_w', 'grad_conv_ln_g': 'grad_w', 'grad_conv_ln_b': 'grad_w', 'grad_conv_w_pw': 'grad_w', 'grad_conv_b_pw': 'grad_w', 'grad_w_ff1': 'grad_w', 'grad_w_ff2': 'grad_w', 'grad_final_norm_g': 'grad_w', 'delta_norm_mix_g': 'delta_w', 'delta_norm_ffn_g': 'delta_w', 'delta_w_in_ab': 'delta_w', 'delta_w_out_ab': 'delta_w', 'delta_hg_lb_logits': 'delta_w', 'delta_hg_norm_g': 'delta_w', 'delta_conv_w_glu': 'delta_w', 'delta_conv_b_glu': 'delta_w', 'delta_conv_w_dw': 'delta_w', 'delta_conv_b_dw': 'delta_w', 'delta_conv_ln_g': 'delta_w', 'delta_conv_ln_b': 'delta_w', 'delta_conv_w_pw': 'delta_w', 'delta_conv_b_pw': 'delta_w', 'delta_w_ff1': 'delta_w', 'delta_w_ff2': 'delta_w', 'delta_final_norm_g': 'delta_w', 'new_m_norm_mix_g': 'new_m', 'new_m_norm_ffn_g': 'new_m', 'new_m_w_in_ab': 'new_m', 'new_m_w_out_ab': 'new_m', 'new_m_hg_lb_logits': 'new_m', 'new_m_hg_norm_g': 'new_m', 'new_m_conv_w_glu': 'new_m', 'new_m_conv_b_glu': 'new_m', 'new_m_conv_w_dw': 'new_m', 'new_m_conv_b_dw': 'new_m', 'new_m_conv_ln_g': 'new_m', 'new_m_conv_ln_b': 'new_m', 'new_m_conv_w_pw': 'new_m', 'new_m_conv_b_pw': 'new_m', 'new_m_w_ff1': 'new_m', 'new_m_w_ff2': 'new_m', 'new_m_final_norm_g': 'new_m', 'new_v_norm_mix_g': 'new_v', 'new_v_norm_ffn_g': 'new_v', 'new_v_w_in_ab': 'new_v', 'new_v_w_out_ab': 'new_v', 'new_v_hg_lb_logits': 'new_v', 'new_v_hg_norm_g': 'new_v', 'new_v_conv_w_glu': 'new_v', 'new_v_conv_b_glu': 'new_v', 'new_v_conv_w_dw': 'new_v', 'new_v_conv_b_dw': 'new_v', 'new_v_conv_ln_g': 'new_v', 'new_v_conv_ln_b': 'new_v', 'new_v_conv_w_pw': 'new_v', 'new_v_conv_b_pw': 'new_v', 'new_v_w_ff1': 'new_v', 'new_v_w_ff2': 'new_v', 'new_v_final_norm_g': 'new_v'}


def _forward(args):
    return _fwd_reference(*[args[k] for k in FWD_PARAMS])


def _output_shape():
    def fwd():
        inp = _fwd_setup_inputs(0)
        return _fwd_reference(*[inp[k] for k in FWD_PARAMS])
    out = _jax.eval_shape(fwd)
    return out.shape, out.dtype

N_MICROBATCH = 1
ADAM_LR = 0.001
ADAM_B1 = 0.9
ADAM_B2 = 0.999
ADAM_EPS = 1e-08
ADAM_WD = 0.01
ADAM_STEP = 10
PER_EXAMPLE_BATCH_AXIS = {'x': 0, 'loss_target': 0}
SHARED_INPUTS = []
_WEIGHT_DTYPES = {'norm_mix_g': _jnp.float32, 'norm_ffn_g': _jnp.float32, 'w_in_ab': _jnp.float32, 'w_out_ab': _jnp.float32, 'hg_lb_logits': _jnp.float32, 'hg_norm_g': _jnp.float32, 'conv_w_glu': _jnp.float32, 'conv_b_glu': _jnp.float32, 'conv_w_dw': _jnp.float32, 'conv_b_dw': _jnp.float32, 'conv_ln_g': _jnp.float32, 'conv_ln_b': _jnp.float32, 'conv_w_pw': _jnp.float32, 'conv_b_pw': _jnp.float32, 'w_ff1': _jnp.float32, 'w_ff2': _jnp.float32, 'final_norm_g': _jnp.float32}
MOMENT_SCALE = {'norm_mix_g': 2.132056e-01, 'norm_ffn_g': 2.010145e-01, 'w_in_ab': 1.435870e-01, 'w_out_ab': 1.527441e-01, 'hg_lb_logits': 5.208901e-02, 'hg_norm_g': 1.542136e-01, 'conv_w_glu': 7.402406e-02, 'conv_b_glu': 1.189552e-01, 'conv_w_dw': 9.709608e-02, 'conv_b_dw': 2.871812e-01, 'conv_ln_g': 1.375322e-01, 'conv_ln_b': 1.732936e-01, 'conv_w_pw': 1.126227e-01, 'conv_b_pw': 3.641252e-01, 'w_ff1': 1.002181e-01, 'w_ff2': 2.200657e-01, 'final_norm_g': 6.502912e+01}


def _to_microbatches(a, axis):
    t = _jnp.moveaxis(a, axis, 0)
    t = t.reshape((N_MICROBATCH, t.shape[0] // N_MICROBATCH) + t.shape[1:])
    return _jnp.moveaxis(t, 1, axis + 1)


def setup_inputs(seed: int = 0) -> dict:
    inp = _fwd_setup_inputs(seed)
    key = _jax.random.fold_in(_jax.random.key(seed), 7919)
    shape, _ = _output_shape()
    out = dict(inp)
    out["loss_target"] = _jax.random.normal(_jax.random.fold_in(key, 0), shape, _jnp.float32)
    for i, name in enumerate(TWIN_WEIGHTS):
        w = inp[name].astype(_jnp.float32)
        if MOMENT_SCALE is None:
            s = _jnp.sqrt(_jnp.mean(_jnp.square(w)) + 1e-30)
        else:
            s = MOMENT_SCALE[name]
        km, kv = _jax.random.split(_jax.random.fold_in(key, i + 1))
        out[name] = w
        out["m_" + name] = s * _jax.random.normal(km, w.shape, _jnp.float32)
        out["v_" + name] = (s * s) * _jax.random.uniform(kv, w.shape, _jnp.float32, 0.5, 1.5)
    if N_MICROBATCH > 1:
        for name, axis in PER_EXAMPLE_BATCH_AXIS.items():
            out[name] = _to_microbatches(out[name], axis)
    return {'x': out['x'], 'norm_mix_g': out['norm_mix_g'], 'norm_ffn_g': out['norm_ffn_g'], 'w_in_ab': out['w_in_ab'], 'w_out_ab': out['w_out_ab'], 'hg_lb_logits': out['hg_lb_logits'], 'hg_norm_g': out['hg_norm_g'], 'conv_w_glu': out['conv_w_glu'], 'conv_b_glu': out['conv_b_glu'], 'conv_w_dw': out['conv_w_dw'], 'conv_b_dw': out['conv_b_dw'], 'conv_ln_g': out['conv_ln_g'], 'conv_ln_b': out['conv_ln_b'], 'conv_w_pw': out['conv_w_pw'], 'conv_b_pw': out['conv_b_pw'], 'w_ff1': out['w_ff1'], 'w_ff2': out['w_ff2'], 'final_norm_g': out['final_norm_g'], 'loss_target': out['loss_target'], 'm_norm_mix_g': out['m_norm_mix_g'], 'm_norm_ffn_g': out['m_norm_ffn_g'], 'm_w_in_ab': out['m_w_in_ab'], 'm_w_out_ab': out['m_w_out_ab'], 'm_hg_lb_logits': out['m_hg_lb_logits'], 'm_hg_norm_g': out['m_hg_norm_g'], 'm_conv_w_glu': out['m_conv_w_glu'], 'm_conv_b_glu': out['m_conv_b_glu'], 'm_conv_w_dw': out['m_conv_w_dw'], 'm_conv_b_dw': out['m_conv_b_dw'], 'm_conv_ln_g': out['m_conv_ln_g'], 'm_conv_ln_b': out['m_conv_ln_b'], 'm_conv_w_pw': out['m_conv_w_pw'], 'm_conv_b_pw': out['m_conv_b_pw'], 'm_w_ff1': out['m_w_ff1'], 'm_w_ff2': out['m_w_ff2'], 'm_final_norm_g': out['m_final_norm_g'], 'v_norm_mix_g': out['v_norm_mix_g'], 'v_norm_ffn_g': out['v_norm_ffn_g'], 'v_w_in_ab': out['v_w_in_ab'], 'v_w_out_ab': out['v_w_out_ab'], 'v_hg_lb_logits': out['v_hg_lb_logits'], 'v_hg_norm_g': out['v_hg_norm_g'], 'v_conv_w_glu': out['v_conv_w_glu'], 'v_conv_b_glu': out['v_conv_b_glu'], 'v_conv_w_dw': out['v_conv_w_dw'], 'v_conv_b_dw': out['v_conv_b_dw'], 'v_conv_ln_g': out['v_conv_ln_g'], 'v_conv_ln_b': out['v_conv_ln_b'], 'v_conv_w_pw': out['v_conv_w_pw'], 'v_conv_b_pw': out['v_conv_b_pw'], 'v_w_ff1': out['v_w_ff1'], 'v_w_ff2': out['v_w_ff2'], 'v_final_norm_g': out['v_final_norm_g']}


def _loss(weights, diff, rest, loss_target):
    with _jax.named_scope("forward"):
        args = {**rest, TWIN_DIFF_INPUT: diff, **{k: w.astype(_WEIGHT_DTYPES[k]) for k, w in weights.items()}}
        y = _forward(args)
    with _jax.named_scope("loss_head"):
        err = _jnp.square(y.astype(_jnp.float32) - loss_target)
        return 0.5 * _jnp.sum(_jnp.mean(err, axis=-1)) if err.ndim else 0.5 * err


def _adamw(w, g, m, v):
    m = ADAM_B1 * m + (1.0 - ADAM_B1) * g
    v = ADAM_B2 * v + (1.0 - ADAM_B2) * _jnp.square(g)
    m_hat = m / (1.0 - ADAM_B1 ** ADAM_STEP)
    v_hat = v / (1.0 - ADAM_B2 ** ADAM_STEP)
    delta = -ADAM_LR * (m_hat / (_jnp.sqrt(v_hat) + ADAM_EPS) + ADAM_WD * w)
    return delta, m, v


def reference(x, norm_mix_g, norm_ffn_g, w_in_ab, w_out_ab, hg_lb_logits, hg_norm_g, conv_w_glu, conv_b_glu, conv_w_dw, conv_b_dw, conv_ln_g, conv_ln_b, conv_w_pw, conv_b_pw, w_ff1, w_ff2, final_norm_g, loss_target, m_norm_mix_g, m_norm_ffn_g, m_w_in_ab, m_w_out_ab, m_hg_lb_logits, m_hg_norm_g, m_conv_w_glu, m_conv_b_glu, m_conv_w_dw, m_conv_b_dw, m_conv_ln_g, m_conv_ln_b, m_conv_w_pw, m_conv_b_pw, m_w_ff1, m_w_ff2, m_final_norm_g, v_norm_mix_g, v_norm_ffn_g, v_w_in_ab, v_w_out_ab, v_hg_lb_logits, v_hg_norm_g, v_conv_w_glu, v_conv_b_glu, v_conv_w_dw, v_conv_b_dw, v_conv_ln_g, v_conv_ln_b, v_conv_w_pw, v_conv_b_pw, v_w_ff1, v_w_ff2, v_final_norm_g):
    given = dict(x=x, norm_mix_g=norm_mix_g, norm_ffn_g=norm_ffn_g, w_in_ab=w_in_ab, w_out_ab=w_out_ab, hg_lb_logits=hg_lb_logits, hg_norm_g=hg_norm_g, conv_w_glu=conv_w_glu, conv_b_glu=conv_b_glu, conv_w_dw=conv_w_dw, conv_b_dw=conv_b_dw, conv_ln_g=conv_ln_g, conv_ln_b=conv_ln_b, conv_w_pw=conv_w_pw, conv_b_pw=conv_b_pw, w_ff1=w_ff1, w_ff2=w_ff2, final_norm_g=final_norm_g, loss_target=loss_target, m_norm_mix_g=m_norm_mix_g, m_norm_ffn_g=m_norm_ffn_g, m_w_in_ab=m_w_in_ab, m_w_out_ab=m_w_out_ab, m_hg_lb_logits=m_hg_lb_logits, m_hg_norm_g=m_hg_norm_g, m_conv_w_glu=m_conv_w_glu, m_conv_b_glu=m_conv_b_glu, m_conv_w_dw=m_conv_w_dw, m_conv_b_dw=m_conv_b_dw, m_conv_ln_g=m_conv_ln_g, m_conv_ln_b=m_conv_ln_b, m_conv_w_pw=m_conv_w_pw, m_conv_b_pw=m_conv_b_pw, m_w_ff1=m_w_ff1, m_w_ff2=m_w_ff2, m_final_norm_g=m_final_norm_g, v_norm_mix_g=v_norm_mix_g, v_norm_ffn_g=v_norm_ffn_g, v_w_in_ab=v_w_in_ab, v_w_out_ab=v_w_out_ab, v_hg_lb_logits=v_hg_lb_logits, v_hg_norm_g=v_hg_norm_g, v_conv_w_glu=v_conv_w_glu, v_conv_b_glu=v_conv_b_glu, v_conv_w_dw=v_conv_w_dw, v_conv_b_dw=v_conv_b_dw, v_conv_ln_g=v_conv_ln_g, v_conv_ln_b=v_conv_ln_b, v_conv_w_pw=v_conv_w_pw, v_conv_b_pw=v_conv_b_pw, v_w_ff1=v_w_ff1, v_w_ff2=v_w_ff2, v_final_norm_g=v_final_norm_g)
    weights = {n: given[n] for n in TWIN_WEIGHTS}
    shared = {n: given[n] for n in SHARED_INPUTS}
    per_example = {n: given[n] for n in ['x']}
    grad_fn = _jax.value_and_grad(_loss, argnums=(0, 1))

    def one_microbatch(ex, loss_target):
        ex = dict(ex)
        diff = ex.pop(TWIN_DIFF_INPUT)
        return grad_fn(weights, diff, {**shared, **ex}, loss_target)

    if N_MICROBATCH == 1:
        loss, (grad_w, grad_x) = one_microbatch(per_example, given["loss_target"])
    else:
        def body(carry, xs):
            loss_sum, grad_sum = carry
            l_k, (gw_k, gx_k) = one_microbatch(xs[0], xs[1])
            with _jax.named_scope("update"):
                return (loss_sum + l_k, _jax.tree.map(_jnp.add, grad_sum, gw_k)), gx_k

        init = (_jnp.zeros((), _jnp.float32), _jax.tree.map(_jnp.zeros_like, weights))
        (loss, grad_w), grad_x = _jax.lax.scan(body, init, (per_example, given["loss_target"]))
    with _jax.named_scope("update"):
        delta_w, new_m, new_v = {}, {}, {}
        for n in TWIN_WEIGHTS:
            delta_w[n], new_m[n], new_v[n] = _adamw(weights[n], grad_w[n], given["m_" + n], given["v_" + n])
    return (loss, grad_x, *[grad_w[n] for n in TWIN_WEIGHTS], *[delta_w[n] for n in TWIN_WEIGHTS],
            *[new_m[n] for n in TWIN_WEIGHTS], *[new_v[n] for n in TWIN_WEIGHTS])
```

```python
import functools

import jax
import jax.numpy as jnp
from jax import lax
from jax.experimental import pallas as pl
from jax.experimental.pallas import tpu as pltpu

F32 = jnp.float32
MXU_DTYPE = jnp.bfloat16
MESH = pl.DeviceIdType.MESH

D_MODEL = 1024
SB_HEADS, SB_DH, SB_WIDTH = 8, 64, 512
SB_BLOCK = 128
HG_HEADS, HG_DH, HG_WIDTH = 4, 128, 512
HG_CHUNK = 16
HG_TOKENS = 256
IN_WIDTH = 3 * SB_WIDTH + 4 * HG_WIDTH
CONV_WIDTH = 31
CONV_HALO = 32
D_FF = 4096
RMS_EPS = 1e-6
LN_EPS = 1e-5
N_CHIPS = 4
N_DEV = 8
SMALL_ROWS = 48
VMEM_LIMIT = 56 * 1024 * 1024

ADAM_LR, ADAM_B1, ADAM_B2, ADAM_EPS, ADAM_WD, ADAM_STEP = 0.001, 0.9, 0.999, 1e-08, 0.01, 10


def _params(*sem):
    return pltpu.CompilerParams(dimension_semantics=sem, vmem_limit_bytes=VMEM_LIMIT)


def _mx(v):
    return v.astype(MXU_DTYPE)


def _dot(a, b):
    return jnp.dot(_mx(a), _mx(b), preferred_element_type=F32)


def _dot_nt(a, b):
    return lax.dot_general(_mx(a), _mx(b), (((1,), (1,)), ((), ())), preferred_element_type=F32)


def _dot_tn(a, b):
    return lax.dot_general(_mx(a), _mx(b), (((0,), (0,)), ((), ())), preferred_element_type=F32)


def _split_dot(v, u):
    hi = _mx(v)
    lo = _mx(v - hi.astype(F32))
    return (jnp.dot(hi, u, preferred_element_type=F32) + jnp.dot(lo, u, preferred_element_type=F32))


def _matmul(a, b, *, mode, out_dtypes, epilogue=None, tiles=(), rows=(), tm=512, tn=1024, tk=1024, name):
    if mode == "nn":
        (M, K), N = a.shape, b.shape[1]
    elif mode == "nt":
        (M, K), N = a.shape, b.shape[0]
    else:
        (K, M), N = a.shape, b.shape[1]
    tm, tn, tk = min(tm, M), min(tn, N), min(tk, K)
    assert M % tm == 0 and N % tn == 0 and K % tk == 0, (name, M, N, K)
    nk = K // tk
    a_spec = pl.BlockSpec((tk, tm), lambda i, j, k: (k, i)) if mode == "tn" else pl.BlockSpec((tm, tk), lambda i, j, k: (i, k))
    b_spec = pl.BlockSpec((tn, tk), lambda i, j, k: (j, k)) if mode == "nt" else pl.BlockSpec((tk, tn), lambda i, j, k: (k, j))
    dims = {"nn": ((1,), (0,)), "nt": ((1,), (1,)), "tn": ((0,), (0,))}[mode]
    n_t, n_r, n_o = len(tiles), len(rows), len(out_dtypes)
    if epilogue is None:
        epilogue = lambda acc: (acc,)

    def body(a_ref, b_ref, *rest):
        extra, outs, acc_ref = rest[:n_t + n_r], rest[n_t + n_r:n_t + n_r + n_o], rest[-1]
        k = pl.program_id(2)

        @pl.when(k == 0)
        def _():
            acc_ref[...] = jnp.zeros_like(acc_ref)

        acc_ref[...] += lax.dot_general(_mx(a_ref[...]), _mx(b_ref[...]), (dims, ((), ())), preferred_element_type=F32)

        @pl.when(k == nk - 1)
        def _():
            res = epilogue(acc_ref[...], *[e[...] for e in extra])
            for o_ref, r in zip(outs, res):
                o_ref[...] = r.astype(o_ref.dtype)

    tile_spec = pl.BlockSpec((tm, tn), lambda i, j, k: (i, j))
    row_spec = pl.BlockSpec((1, tn), lambda i, j, k: (0, j))
    outs = pl.pallas_call(
        body, grid=(M // tm, N // tn, nk),
        in_specs=[a_spec, b_spec] + [tile_spec] * n_t + [row_spec] * n_r,
        out_specs=[tile_spec] * n_o,
        out_shape=[jax.ShapeDtypeStruct((M, N), dt) for dt in out_dtypes],
        scratch_shapes=[pltpu.VMEM((tm, tn), F32)],
        compiler_params=_params("parallel", "parallel", "arbitrary"), name=name,
    )(a, b, *tiles, *rows)
    return outs[0] if n_o == 1 else outs


def _token_block(T):
    return min(512, T)


def _rmsnorm_fwd(h, g, *, name):
    T, D = h.shape
    tb = _token_block(T)

    def body(h_ref, g_ref, u_ref):
        x = h_ref[...]
        r = lax.rsqrt(jnp.mean(x * x, axis=-1, keepdims=True) + RMS_EPS)
        u_ref[...] = (x * r * g_ref[...]).astype(u_ref.dtype)

    blk = pl.BlockSpec((tb, D), lambda i: (i, 0))
    return pl.pallas_call(
        body, grid=(T // tb,), in_specs=[blk, pl.BlockSpec((1, D), lambda i: (0, 0))], out_specs=blk,
        out_shape=jax.ShapeDtypeStruct((T, D), MXU_DTYPE), compiler_params=_params("parallel"), name=name,
    )(h, g)


def _rms_bwd_math(x, g, du):
    r = lax.rsqrt(jnp.mean(x * x, axis=-1, keepdims=True) + RMS_EPS)
    gd = g * du
    dx = r * gd - x * (r * r * r) * jnp.mean(gd * x, axis=-1, keepdims=True)
    return dx, du * x * r


def _rmsnorm_bwd(du, h, g, dres, *, name):
    T, D = h.shape
    tb = _token_block(T)

    def body(du_ref, h_ref, g_ref, dres_ref, dh_ref, dg_ref, cs_ref):
        @pl.when(pl.program_id(0) == 0)
        def _():
            dg_ref[...] = jnp.zeros_like(dg_ref)
            cs_ref[...] = jnp.zeros_like(cs_ref)

        dx, dg_terms = _rms_bwd_math(h_ref[...], g_ref[...], du_ref[...])
        dh = dres_ref[...] + dx
        dh_ref[...] = dh
        dg_ref[...] += jnp.sum(dg_terms, axis=0, keepdims=True)
        cs_ref[...] += jnp.sum(dh, axis=0, keepdims=True)

    blk = pl.BlockSpec((tb, D), lambda i: (i, 0))
    row = pl.BlockSpec((1, D), lambda i: (0, 0))
    return pl.pallas_call(
        body, grid=(T // tb,), in_specs=[blk, blk, row, blk], out_specs=[blk, row, row],
        out_shape=[jax.ShapeDtypeStruct((T, D), F32), jax.ShapeDtypeStruct((1, D), F32), jax.ShapeDtypeStruct((1, D), F32)],
        compiler_params=_params("arbitrary"), name=name,
    )(du, h, g, dres)


def _loss_head(h, g, target, *, name):
    T, D = h.shape
    tb = _token_block(T)

    def body(h_ref, g_ref, t_ref, dh_ref, dg_ref, loss_ref):
        @pl.when(pl.program_id(0) == 0)
        def _():
            dg_ref[...] = jnp.zeros_like(dg_ref)
            loss_ref[...] = jnp.zeros_like(loss_ref)

        x, gg = h_ref[...], g_ref[...]
        r = lax.rsqrt(jnp.mean(x * x, axis=-1, keepdims=True) + RMS_EPS)
        diff = x * r * gg - t_ref[...]
        per_token = jnp.mean(diff * diff, axis=-1, keepdims=True)
        loss_ref[...] += 0.5 * jnp.sum(per_token, axis=0, keepdims=True)
        dx, dg_terms = _rms_bwd_math(x, gg, diff / D)
        dh_ref[...] = dx
        dg_ref[...] += jnp.sum(dg_terms, axis=0, keepdims=True)

    blk = pl.BlockSpec((tb, D), lambda i: (i, 0))
    row = pl.BlockSpec((1, D), lambda i: (0, 0))
    return pl.pallas_call(
        body, grid=(T // tb,), in_specs=[blk, row, blk], out_specs=[blk, row, pl.BlockSpec((1, 1), lambda i: (0, 0))],
        out_shape=[jax.ShapeDtypeStruct((T, D), F32), jax.ShapeDtypeStruct((1, D), F32), jax.ShapeDtypeStruct((1, 1), F32)],
        compiler_params=_params("arbitrary"), name=name,
    )(h, g, target)


def _sb_tile(qs, ks, c, tri, mask):
    z = _dot_nt(qs, ks)
    sp = jnp.maximum(z, 0.0) + jnp.log1p(jnp.exp(-jnp.abs(z)))
    lk, lb = -sp, z - sp
    if mask is not None:
        lk = jnp.where(mask, lk, 0.0)
    w = jnp.exp(lb + _split_dot(lk, tri) + c)
    if mask is not None:
        w = jnp.where(mask, w, 0.0)
    return lk, lb, w


def _sb_consts():
    r = lax.broadcasted_iota(jnp.int32, (SB_BLOCK, SB_BLOCK), 0)
    s = lax.broadcasted_iota(jnp.int32, (SB_BLOCK, SB_BLOCK), 1)
    return s < r, _mx(r > s), _mx(r < s)


def _sb_fwd(q, k, v, *, name):
    H, T, dh = q.shape
    scale = dh ** -0.5

    def body(q_ref, k_ref, v_ref, o_ref):
        i = pl.program_id(1)
        qs = q_ref[0] * scale
        mask, tri, _ = _sb_consts()

        def tile(jb, m, c, acc):
            sl = pl.ds(pl.multiple_of(jb * SB_BLOCK, SB_BLOCK), SB_BLOCK)
            lk, _, w = _sb_tile(qs, k_ref[0, sl, :], c, tri, m)
            return c + jnp.sum(lk, axis=1, keepdims=True), acc + _dot(w, v_ref[0, sl, :])

        carry = tile(i, mask, jnp.zeros((SB_BLOCK, 1), F32), jnp.zeros((SB_BLOCK, dh), F32))
        _, acc = lax.fori_loop(0, i, lambda it, cr: tile(i - 1 - it, None, *cr), carry)
        o_ref[0] = acc

    qblk = pl.BlockSpec((1, SB_BLOCK, dh), lambda h, i: (h, i, 0))
    full = pl.BlockSpec((1, T, dh), lambda h, i: (h, 0, 0))
    return pl.pallas_call(
        body, grid=(H, T // SB_BLOCK), in_specs=[qblk, full, full], out_specs=qblk,
        out_shape=jax.ShapeDtypeStruct((H, T, dh), F32), compiler_params=_params("parallel", "parallel"), name=name,
    )(q, k, v)


def _sb_bwd(q, k, v, do, *, name):
    H, T, dh = q.shape
    scale = dh ** -0.5
    nkb = T // SB_BLOCK

    def body(q_ref, k_ref, v_ref, do_ref, dq_ref, dk_ref, dv_ref, da_ref, beta_ref):
        i = pl.program_id(1)

        @pl.when(i == 0)
        def _():
            dk_ref[...] = jnp.zeros_like(dk_ref)
            dv_ref[...] = jnp.zeros_like(dv_ref)

        q_raw = q_ref[0]
        qs = q_raw * scale
        do_m = _mx(do_ref[0])
        mask, tri, tri_before = _sb_consts()
        keys = lambda jb: pl.ds(pl.multiple_of(jb * SB_BLOCK, SB_BLOCK), SB_BLOCK)

        def weights(jb, m, c):
            sl = keys(jb)
            lk, lb, w = _sb_tile(qs, k_ref[0, sl, :], c, tri, m)
            da_ref[jb] = _dot_nt(do_m, v_ref[0, sl, :]) * w
            beta_ref[jb] = jnp.exp(lb)
            dv_ref[0, sl, :] += _dot_tn(w, do_m)
            return c + jnp.sum(lk, axis=1, keepdims=True)

        c = weights(i, mask, jnp.zeros((SB_BLOCK, 1), F32))
        lax.fori_loop(0, i, lambda it, c: weights(i - 1 - it, None, c), c)

        def logits(jb, m, before, dq):
            sl = keys(jb)
            da = da_ref[jb]
            dz = da - beta_ref[jb] * (da + _split_dot(da, tri_before) + before)
            if m is not None:
                dz = jnp.where(m, dz, 0.0)
            dzs = _mx(dz * scale)
            dk_ref[0, sl, :] += _dot_tn(dzs, q_raw)
            return before + jnp.sum(da, axis=1, keepdims=True), dq + _dot(dzs, k_ref[0, sl, :])

        carry = (jnp.zeros((SB_BLOCK, 1), F32), jnp.zeros((SB_BLOCK, dh), F32))
        carry = lax.fori_loop(0, i, lambda jb, cr: logits(jb, None, *cr), carry)
        dq_ref[0] = logits(i, mask, *carry)[1]

    qblk = pl.BlockSpec((1, SB_BLOCK, dh), lambda h, i: (h, i, 0))
    full = pl.BlockSpec((1, T, dh), lambda h, i: (h, 0, 0))
    shp = jax.ShapeDtypeStruct((H, T, dh), F32)
    return pl.pallas_call(
        body, grid=(H, T // SB_BLOCK), in_specs=[qblk, full, full, qblk], out_specs=[qblk, full, full],
        out_shape=[shp, shp, shp],
        scratch_shapes=[pltpu.VMEM((nkb, SB_BLOCK, SB_BLOCK), F32), pltpu.VMEM((nkb, SB_BLOCK, SB_BLOCK), F32)],
        compiler_params=_params("parallel", "arbitrary"), name=name,
    )(q, k, v, do)


def _chunk_iota():
    return lax.broadcasted_iota(jnp.int32, (HG_CHUNK, HG_DH), 0)


def _chunk_cumsum(x, reverse=False):
    row = _chunk_iota()
    for sh in (1, 2, 4, 8):
        if reverse:
            x = x + jnp.where(row < HG_CHUNK - sh, pltpu.roll(x, HG_CHUNK - sh, 0), 0.0)
        else:
            x = x + jnp.where(row >= sh, pltpu.roll(x, sh, 0), 0.0)
    return x


def _hg_lower_bound(logits_ref):
    lg = logits_ref[...]
    e = jnp.exp(lg - jnp.max(lg, axis=0, keepdims=True))
    return e[0:1, :] / jnp.sum(e, axis=0, keepdims=True)


def _hg_chunk_terms(fr, q, lb):
    sig = jax.nn.sigmoid(fr)
    f = lb + (1.0 - lb) * sig
    kk = 1.0 - f
    G = _chunk_cumsum(jnp.log(f))
    g_last = G[HG_CHUNK - 1:HG_CHUNK, :]
    e_g, e_ng, e_lg = jnp.exp(G), jnp.exp(-G), jnp.exp(g_last - G)
    return dict(sig=sig, f=f, kk=kk, e_g=e_g, e_ng=e_ng, e_lg=e_lg, q_dec=q * e_g, k_intra=kk * e_ng,
                k_state=kk * e_lg, decay=jnp.exp(g_last))


def _hg_causal():
    c = lax.broadcasted_iota(jnp.int32, (HG_CHUNK, HG_CHUNK), 0)
    s = lax.broadcasted_iota(jnp.int32, (HG_CHUNK, HG_CHUNK), 1)
    return s <= c


def _hg_specs(T, tb, col0, order):
    return [pl.BlockSpec((tb, HG_WIDTH), functools.partial(lambda i, j: (order(i), j), j=col0 + j)) for j in range(4)]


def _hg_fwd(proj, logits, norm_g, *, name):
    T = proj.shape[0]
    tb = min(HG_TOKENS, T)
    nch = tb // HG_CHUNK

    def body(q_ref, f_ref, i_ref, gate_ref, lg_ref, ng_ref, out_ref, o_ref, s_ref, st_ref):
        @pl.when(pl.program_id(0) == 0)
        def _():
            st_ref[...] = jnp.zeros_like(st_ref)

        lb_all = _hg_lower_bound(lg_ref)
        causal = _hg_causal()

        def chunk(ci, _):
            rows = pl.ds(pl.multiple_of(ci * HG_CHUNK, HG_CHUNK), HG_CHUNK)
            for hh in range(HG_HEADS):
                cols = slice(hh * HG_DH, (hh + 1) * HG_DH)
                t = _hg_chunk_terms(f_ref[rows, cols], q_ref[rows, cols], lb_all[:, cols])
                v = i_ref[rows, cols]
                st = st_ref[hh]
                scores = jnp.where(causal, _dot_nt(t["q_dec"], t["k_intra"]), 0.0)
                o_ref[rows, cols] = _dot(scores, v) + _dot_nt(t["q_dec"], st)
                s_ref[ci, hh] = st
                st_ref[hh] = st * t["decay"] + _dot_tn(v, t["k_state"])
            return 0

        lax.fori_loop(0, nch, chunk, 0)
        for hh in range(HG_HEADS):
            cols = slice(hh * HG_DH, (hh + 1) * HG_DH)
            o = o_ref[:, cols]
            gate = gate_ref[:, cols]
            on = o * lax.rsqrt(jnp.mean(o * o, axis=-1, keepdims=True) + RMS_EPS) * ng_ref[:, cols]
            out_ref[:, cols] = (on * (gate * jax.nn.sigmoid(gate))).astype(out_ref.dtype)

    blk = pl.BlockSpec((tb, HG_WIDTH), lambda i: (i, 0))
    return pl.pallas_call(
        body, grid=(T // tb,),
        in_specs=_hg_specs(T, tb, 3, lambda i: i) + [pl.BlockSpec((3, HG_WIDTH), lambda i: (0, 0)), pl.BlockSpec((1, HG_WIDTH), lambda i: (0, 0))],
        out_specs=[blk, blk, pl.BlockSpec((nch, HG_HEADS, HG_DH, HG_DH), lambda i: (i, 0, 0, 0))],
        out_shape=[jax.ShapeDtypeStruct((T, HG_WIDTH), MXU_DTYPE), jax.ShapeDtypeStruct((T, HG_WIDTH), F32),
                   jax.ShapeDtypeStruct((T // HG_CHUNK, HG_HEADS, HG_DH, HG_DH), F32)],
        scratch_shapes=[pltpu.VMEM((HG_HEADS, HG_DH, HG_DH), F32)],
        compiler_params=_params("arbitrary"), name=name,
    )(proj, proj, proj, proj, logits, norm_g)


def _hg_bwd(proj, o_raw, states, dmix, logits, norm_g, *, name):
    T = proj.shape[0]
    tb = min(HG_TOKENS, T)
    nch = tb // HG_CHUNK
    nb = T // tb
    rev = lambda i: nb - 1 - i

    def body(q_ref, f_ref, i_ref, gate_ref, o_ref, s_ref, dout_ref, lg_ref, ng_ref, dp_ref, dlb_ref, dng_ref, do_ref, dst_ref):
        @pl.when(pl.program_id(0) == 0)
        def _():
            dst_ref[...] = jnp.zeros_like(dst_ref)
            dlb_ref[...] = jnp.zeros_like(dlb_ref)
            dng_ref[...] = jnp.zeros_like(dng_ref)

        lb_all = _hg_lower_bound(lg_ref)
        causal = _hg_causal()
        row = _chunk_iota()
        for hh in range(HG_HEADS):
            cols = slice(hh * HG_DH, (hh + 1) * HG_DH)
            o, gate, dout, ng = o_ref[:, cols], gate_ref[:, cols], dout_ref[:, cols], ng_ref[:, cols]
            sg = jax.nn.sigmoid(gate)
            r = lax.rsqrt(jnp.mean(o * o, axis=-1, keepdims=True) + RMS_EPS)
            oh = o * r
            dp_ref[:, 3 * HG_WIDTH + hh * HG_DH:3 * HG_WIDTH + (hh + 1) * HG_DH] = dout * (oh * ng) * (sg * (1.0 + gate * (1.0 - sg)))
            don = dout * (gate * sg)
            dng_ref[:, cols] += jnp.sum(don * oh, axis=0, keepdims=True)
            doh = don * ng
            do_ref[:, cols] = r * (doh - oh * jnp.mean(doh * oh, axis=-1, keepdims=True))

        def chunk(it, _):
            ci = nch - 1 - it
            rows = pl.ds(pl.multiple_of(ci * HG_CHUNK, HG_CHUNK), HG_CHUNK)
            for hh in range(HG_HEADS):
                cols = slice(hh * HG_DH, (hh + 1) * HG_DH)
                lb = lb_all[:, cols]
                t = _hg_chunk_terms(f_ref[rows, cols], q_ref[rows, cols], lb)
                v, do_c, st, dst = i_ref[rows, cols], do_ref[rows, cols], s_ref[ci, hh], dst_ref[hh]
                scores = jnp.where(causal, _dot_nt(t["q_dec"], t["k_intra"]), 0.0)
                dscores = jnp.where(causal, _dot_nt(do_c, v), 0.0)
                dqd = _dot(dscores, t["k_intra"]) + _dot(do_c, st)
                dki = _dot_tn(dscores, t["q_dec"])
                dks = _dot(v, dst)
                dp_ref[rows, 2 * HG_WIDTH + hh * HG_DH:2 * HG_WIDTH + (hh + 1) * HG_DH] = _dot_tn(scores, do_c) + _dot_nt(t["k_state"], dst)
                ddecay = jnp.sum(st * dst, axis=0, keepdims=True)
                dst_ref[hh] = dst * t["decay"] + _dot_tn(do_c, t["q_dec"])
                dks_ks = dks * t["k_state"]
                d_glast = jnp.sum(dks_ks, axis=0, keepdims=True) + ddecay * t["decay"]
                d_g = dqd * t["q_dec"] - dki * t["k_intra"] - dks_ks + jnp.where(row == HG_CHUNK - 1, d_glast, 0.0)
                df = _chunk_cumsum(d_g, reverse=True) / t["f"] - (dki * t["e_ng"] + dks * t["e_lg"])
                dp_ref[rows, hh * HG_DH:(hh + 1) * HG_DH] = dqd * t["e_g"]
                dp_ref[rows, HG_WIDTH + hh * HG_DH:HG_WIDTH + (hh + 1) * HG_DH] = df * (1.0 - lb) * t["sig"] * (1.0 - t["sig"])
                dlb_ref[:, cols] += jnp.sum(df * (1.0 - t["sig"]), axis=0, keepdims=True)
            return 0

        lax.fori_loop(0, nch, chunk, 0)

    blk = pl.BlockSpec((tb, HG_WIDTH), lambda i: (rev(i), 0))
    row_spec = pl.BlockSpec((1, HG_WIDTH), lambda i: (0, 0))
    return pl.pallas_call(
        body, grid=(nb,),
        in_specs=_hg_specs(T, tb, 3, rev) + [
            blk, pl.BlockSpec((nch, HG_HEADS, HG_DH, HG_DH), lambda i: (rev(i), 0, 0, 0)),
            pl.BlockSpec((tb, HG_WIDTH), lambda i: (rev(i), 1)), pl.BlockSpec((3, HG_WIDTH), lambda i: (0, 0)), row_spec],
        out_specs=[pl.BlockSpec((tb, 4 * HG_WIDTH), lambda i: (rev(i), 0)), row_spec, row_spec],
        out_shape=[jax.ShapeDtypeStruct((T, 4 * HG_WIDTH), F32), jax.ShapeDtypeStruct((1, HG_WIDTH), F32), jax.ShapeDtypeStruct((1, HG_WIDTH), F32)],
        scratch_shapes=[pltpu.VMEM((tb, HG_WIDTH), F32), pltpu.VMEM((HG_HEADS, HG_DH, HG_DH), F32)],
        compiler_params=_params("arbitrary"), name=name,
    )(proj, proj, proj, proj, o_raw, states, dmix, logits, norm_g)


def _conv_fwd(p, w_dw, b_dw, ln_g, ln_b, *, name):
    T, D = p.shape[0], p.shape[1] // 2
    tb = _token_block(T)
    hpb = tb // CONV_HALO

    def body(p1_ref, p2_ref, q1_ref, q2_ref, w_ref, bdw_ref, g_ref, b_ref, a_ref, y_ref, act_ref, buf):
        i = pl.program_id(0)
        a = p1_ref[...] * jax.nn.sigmoid(p2_ref[...])
        buf[0:CONV_HALO, :] = jnp.where(i > 0, q1_ref[...] * jax.nn.sigmoid(q2_ref[...]), 0.0)
        buf[CONV_HALO:, :] = a
        a_ref[...] = a
        y = jnp.zeros((tb, D), F32) + bdw_ref[...]
        for k in range(CONV_WIDTH):
            y = y + buf[pl.ds(CONV_HALO - CONV_WIDTH + 1 + k, tb), :] * w_ref[k:k + 1, :]
        y_ref[...] = y
        mu = jnp.mean(y, axis=-1, keepdims=True)
        yc = y - mu
        s = yc * lax.rsqrt(jnp.mean(yc * yc, axis=-1, keepdims=True) + LN_EPS) * g_ref[...] + b_ref[...]
        act_ref[...] = (s * jax.nn.sigmoid(s)).astype(act_ref.dtype)

    prev = lambda i: jnp.maximum(i * hpb - 1, 0)
    blk = pl.BlockSpec((tb, D), lambda i: (i, 0))
    row = pl.BlockSpec((1, D), lambda i: (0, 0))
    return pl.pallas_call(
        body, grid=(T // tb,),
        in_specs=[blk, pl.BlockSpec((tb, D), lambda i: (i, 1)), pl.BlockSpec((CONV_HALO, D), lambda i: (prev(i), 0)),
                  pl.BlockSpec((CONV_HALO, D), lambda i: (prev(i), 1)), pl.BlockSpec((CONV_HALO, D), lambda i: (0, 0)), row, row, row],
        out_specs=[blk, blk, blk],
        out_shape=[jax.ShapeDtypeStruct((T, D), F32), jax.ShapeDtypeStruct((T, D), F32), jax.ShapeDtypeStruct((T, D), MXU_DTYPE)],
        scratch_shapes=[pltpu.VMEM((tb + CONV_HALO, D), F32)],
        compiler_params=_params("parallel"), name=name,
    )(p, p, p, p, w_dw, b_dw, ln_g, ln_b)


def _conv_bwd_norm(dact, y, ln_g, ln_b, *, name):
    T, D = y.shape
    tb = _token_block(T)

    def body(da_ref, y_ref, g_ref, b_ref, dy_ref, dg_ref, db_ref, cs_ref):
        @pl.when(pl.program_id(0) == 0)
        def _():
            dg_ref[...] = jnp.zeros_like(dg_ref)
            db_ref[...] = jnp.zeros_like(db_ref)
            cs_ref[...] = jnp.zeros_like(cs_ref)

        y, g = y_ref[...], g_ref[...]
        yc = y - jnp.mean(y, axis=-1, keepdims=True)
        rs = lax.rsqrt(jnp.mean(yc * yc, axis=-1, keepdims=True) + LN_EPS)
        yn = yc * rs
        s = yn * g + b_ref[...]
        sg = jax.nn.sigmoid(s)
        ds = da_ref[...] * (sg * (1.0 + s * (1.0 - sg)))
        dg_ref[...] += jnp.sum(ds * yn, axis=0, keepdims=True)
        db_ref[...] += jnp.sum(ds, axis=0, keepdims=True)
        dyn = ds * g
        dy = rs * (dyn - jnp.mean(dyn, axis=-1, keepdims=True) - yn * jnp.mean(dyn * yn, axis=-1, keepdims=True))
        dy_ref[...] = dy
        cs_ref[...] += jnp.sum(dy, axis=0, keepdims=True)

    blk = pl.BlockSpec((tb, D), lambda i: (i, 0))
    row = pl.BlockSpec((1, D), lambda i: (0, 0))
    rs_ = jax.ShapeDtypeStruct((1, D), F32)
    return pl.pallas_call(
        body, grid=(T // tb,), in_specs=[blk, blk, row, row], out_specs=[blk, row, row, row],
        out_shape=[jax.ShapeDtypeStruct((T, D), F32), rs_, rs_, rs_], compiler_params=_params("arbitrary"), name=name,
    )(dact, y, ln_g, ln_b)


def _conv_bwd_taps(dy, a, p, w_dw, *, name):
    T, D = dy.shape
    tb = _token_block(T)
    hpb = tb // CONV_HALO
    last = T // CONV_HALO - 1
    nb = T // tb

    def body(dy_ref, dyn_ref, a_ref, ap_ref, p1_ref, p2_ref, w_ref, dp_ref, dw_ref, cs_ref, dbuf, abuf):
        i = pl.program_id(0)

        @pl.when(i == 0)
        def _():
            dw_ref[...] = jnp.zeros_like(dw_ref)
            cs_ref[...] = jnp.zeros_like(cs_ref)

        dy = dy_ref[...]
        dbuf[0:tb, :] = dy
        dbuf[tb:, :] = jnp.where(i < nb - 1, dyn_ref[...], 0.0)
        abuf[0:CONV_HALO, :] = jnp.where(i > 0, ap_ref[...], 0.0)
        abuf[CONV_HALO:, :] = a_ref[...]
        da = jnp.zeros((tb, D), F32)
        for k in range(CONV_WIDTH):
            da = da + dbuf[pl.ds(CONV_WIDTH - 1 - k, tb), :] * w_ref[k:k + 1, :]
            dw_ref[k:k + 1, :] += jnp.sum(dy * abuf[pl.ds(CONV_HALO - CONV_WIDTH + 1 + k, tb), :], axis=0, keepdims=True)
        p1 = p1_ref[...]
        sg = jax.nn.sigmoid(p2_ref[...])
        dp1 = da * sg
        dp2 = da * p1 * (sg * (1.0 - sg))
        dp_ref[:, 0:D] = dp1
        dp_ref[:, D:] = dp2
        cs_ref[:, 0:D] += jnp.sum(dp1, axis=0, keepdims=True)
        cs_ref[:, D:] += jnp.sum(dp2, axis=0, keepdims=True)

    blk = pl.BlockSpec((tb, D), lambda i: (i, 0))
    return pl.pallas_call(
        body, grid=(nb,),
        in_specs=[blk, pl.BlockSpec((CONV_HALO, D), lambda i: (jnp.minimum((i + 1) * hpb, last), 0)), blk,
                  pl.BlockSpec((CONV_HALO, D), lambda i: (jnp.maximum(i * hpb - 1, 0), 0)), blk,
                  pl.BlockSpec((tb, D), lambda i: (i, 1)), pl.BlockSpec((CONV_HALO, D), lambda i: (0, 0))],
        out_specs=[pl.BlockSpec((tb, 2 * D), lambda i: (i, 0)), pl.BlockSpec((CONV_HALO, D), lambda i: (0, 0)), pl.BlockSpec((1, 2 * D), lambda i: (0, 0))],
        out_shape=[jax.ShapeDtypeStruct((T, 2 * D), F32), jax.ShapeDtypeStruct((CONV_HALO, D), F32), jax.ShapeDtypeStruct((1, 2 * D), F32)],
        scratch_shapes=[pltpu.VMEM((tb + CONV_HALO, D), F32), pltpu.VMEM((tb + CONV_HALO, D), F32)],
        compiler_params=_params("arbitrary"), name=name,
    )(dy, dy, a, a, p, p, w_dw)


def _row_block(rows):
    for tr in (512, 256, 128, 64, 32, 16, 8):
        if rows % tr == 0:
            return tr
    return rows


def _sum_leading(x, *, name):
    n, R, C = x.shape
    tr = _row_block(R)

    def body(x_ref, o_ref):
        acc = x_ref[0]
        for j in range(1, n):
            acc = acc + x_ref[j]
        o_ref[...] = acc

    return pl.pallas_call(
        body, grid=(R // tr,), in_specs=[pl.BlockSpec((n, tr, C), lambda i: (0, i, 0))], out_specs=pl.BlockSpec((tr, C), lambda i: (i, 0)),
        out_shape=jax.ShapeDtypeStruct((R, C), x.dtype), compiler_params=_params("parallel"), name=name,
    )(x)


def _add_pair(x, y, *, name):
    n, R, C = x.shape
    tr = _row_block(R)

    def body(x_ref, y_ref, o_ref):
        o_ref[...] = x_ref[...] + y_ref[...]

    blk = pl.BlockSpec((1, tr, C), lambda j, i: (j, i, 0))
    return pl.pallas_call(
        body, grid=(n, R // tr), in_specs=[blk, blk], out_specs=blk,
        out_shape=jax.ShapeDtypeStruct((n, R, C), x.dtype), compiler_params=_params("parallel", "parallel"), name=name,
    )(x, y)


def _adamw(w, g, m, v, *, name):
    R, C = w.shape
    tr = _row_block(R)
    c1, c2 = 1.0 - ADAM_B1 ** ADAM_STEP, 1.0 - ADAM_B2 ** ADAM_STEP

    def body(w_ref, g_ref, m_ref, v_ref, d_ref, nm_ref, nv_ref):
        g_ = g_ref[...]
        nm = ADAM_B1 * m_ref[...] + (1.0 - ADAM_B1) * g_
        nv = ADAM_B2 * v_ref[...] + (1.0 - ADAM_B2) * (g_ * g_)
        d_ref[...] = -ADAM_LR * ((nm / c1) / (jnp.sqrt(nv / c2) + ADAM_EPS) + ADAM_WD * w_ref[...])
        nm_ref[...] = nm
        nv_ref[...] = nv

    blk = pl.BlockSpec((tr, C), lambda i: (i, 0))
    shp = jax.ShapeDtypeStruct((R, C), F32)
    return pl.pallas_call(
        body, grid=(R // tr,), in_specs=[blk] * 4, out_specs=[blk] * 3, out_shape=[shp] * 3,
        compiler_params=_params("parallel"), name=name,
    )(w, g, m, v)


def _small_reduce(packs, logits, *, name):
    n, R, C = packs.shape

    def body(p_ref, lg_ref, s_ref, dlg_ref):
        acc = p_ref[0]
        for j in range(1, n):
            acc = acc + p_ref[j]
        s_ref[...] = acc
        lg = lg_ref[...]
        e = jnp.exp(lg - jnp.max(lg, axis=0, keepdims=True))
        sm = e / jnp.sum(e, axis=0, keepdims=True)
        dlb = acc[5:6, HG_WIDTH:2 * HG_WIDTH]
        first = lax.broadcasted_iota(jnp.int32, sm.shape, 0) == 0
        dlg_ref[...] = sm[0:1, :] * (jnp.where(first, 1.0, 0.0) - sm) * dlb

    whole = lambda shape: pl.BlockSpec(shape, lambda: (0,) * len(shape))
    return pl.pallas_call(
        body, in_specs=[whole((n, R, C)), whole(logits.shape)], out_specs=[whole((R, C)), whole(logits.shape)],
        out_shape=[jax.ShapeDtypeStruct((R, C), F32), jax.ShapeDtypeStruct(logits.shape, F32)],
        compiler_params=pltpu.CompilerParams(vmem_limit_bytes=VMEM_LIMIT), name=name,
    )(packs, logits)


HBM_SPEC = pl.BlockSpec(memory_space=pl.ANY)


def _place():
    return lax.axis_index("x"), lax.axis_index("y"), lax.axis_index("c")


def _exchange(arrays, peers_of, slot_of, n_slots, *, name):
    n = len(arrays)
    n_peers = len(peers_of(0, 0, 0))

    def body(*refs):
        ins, outs = refs[:n], refs[n:2 * n]
        send_sems, recv_sems, local_sems = refs[2 * n:]
        x, y, c = _place()
        slot = slot_of(x, y, c)
        peers = peers_of(x, y, c)
        copies = []
        for a in range(n):
            mine = pltpu.make_async_copy(ins[a], outs[a].at[slot], local_sems.at[a])
            mine.start()
            copies.append(mine)
            for k, peer in enumerate(peers):
                cp = pltpu.make_async_remote_copy(
                    src_ref=ins[a], dst_ref=outs[a].at[slot], send_sem=send_sems.at[a * n_peers + k],
                    recv_sem=recv_sems.at[a * n_peers + k], device_id=peer, device_id_type=MESH)
                cp.start()
                copies.append(cp)
        for cp in copies:
            cp.wait()

    return pl.pallas_call(
        body, in_specs=[HBM_SPEC] * n, out_specs=[HBM_SPEC] * n,
        out_shape=[jax.ShapeDtypeStruct((n_slots,) + a.shape, a.dtype) for a in arrays],
        scratch_shapes=[pltpu.SemaphoreType.DMA((n * n_peers,)), pltpu.SemaphoreType.DMA((n * n_peers,)), pltpu.SemaphoreType.DMA((n,))],
        name=name,
    )(*arrays)


def _same_core_peers(x, y, c):
    return [(1 - x, y, c), (x, 1 - y, c), (1 - x, 1 - y, c)]


def _all_peers(x, y, c):
    flip = lambda v, b: 1 - v if b else v
    return [(flip(x, r & 4), flip(y, r & 2), flip(c, r & 1)) for r in range(1, 8)]


def _gather_chips(arrays, *, name):
    return _exchange(arrays, _same_core_peers, lambda x, y, c: 2 * x + y, N_CHIPS, name=name)


def _gather_all(arrays, *, name):
    return _exchange(arrays, _all_peers, lambda x, y, c: 4 * x + 2 * y + c, N_DEV, name=name)


def _pair_swap(arrays, *, name):
    n = len(arrays)

    def body(*refs):
        ins, outs = refs[:n], refs[n:2 * n]
        send_sems, recv_sems = refs[2 * n:]
        x, y, c = _place()
        copies = []
        for a in range(n):
            cp = pltpu.make_async_remote_copy(
                src_ref=ins[a].at[1 - c], dst_ref=outs[a], send_sem=send_sems.at[a], recv_sem=recv_sems.at[a],
                device_id=(x, y, 1 - c), device_id_type=MESH)
            cp.start()
            copies.append(cp)
        for cp in copies:
            cp.wait()

    return pl.pallas_call(
        body, in_specs=[HBM_SPEC] * n, out_specs=[HBM_SPEC] * n,
        out_shape=[jax.ShapeDtypeStruct(a.shape[1:], a.dtype) for a in arrays],
        scratch_shapes=[pltpu.SemaphoreType.DMA((n,)), pltpu.SemaphoreType.DMA((n,))], name=name,
    )(*arrays)


def _chip_scatter(p, *, name):
    def body(p_ref, o_ref, send_sems, recv_sems, local_sem):
        x, y, c = _place()
        me = 2 * x + y
        mine = pltpu.make_async_copy(p_ref.at[me], o_ref.at[me], local_sem)
        mine.start()
        copies = [mine]
        for k, (px, py, pc) in enumerate(_same_core_peers(x, y, c)):
            cp = pltpu.make_async_remote_copy(
                src_ref=p_ref.at[2 * px + py], dst_ref=o_ref.at[me], send_sem=send_sems.at[k], recv_sem=recv_sems.at[k],
                device_id=(px, py, pc), device_id_type=MESH)
            cp.start()
            copies.append(cp)
        for cp in copies:
            cp.wait()

    return pl.pallas_call(
        body, in_specs=[HBM_SPEC], out_specs=HBM_SPEC, out_shape=jax.ShapeDtypeStruct(p.shape, p.dtype),
        scratch_shapes=[pltpu.SemaphoreType.DMA((3,)), pltpu.SemaphoreType.DMA((3,)), pltpu.SemaphoreType.DMA], name=name,
    )(p)


def _pair_gather(q, *, name):
    def body(q_ref, o_ref, send_sem, recv_sem, local_sem):
        x, y, c = _place()
        mine = pltpu.make_async_copy(q_ref, o_ref.at[c], local_sem)
        mine.start()
        cp = pltpu.make_async_remote_copy(src_ref=q_ref, dst_ref=o_ref.at[c], send_sem=send_sem, recv_sem=recv_sem,
                                          device_id=(x, y, 1 - c), device_id_type=MESH)
        cp.start()
        cp.wait()
        mine.wait()

    return pl.pallas_call(
        body, in_specs=[HBM_SPEC], out_specs=HBM_SPEC, out_shape=jax.ShapeDtypeStruct((2,) + q.shape, q.dtype),
        scratch_shapes=[pltpu.SemaphoreType.DMA, pltpu.SemaphoreType.DMA, pltpu.SemaphoreType.DMA], name=name,
    )(q)


def _cols_from_chips(g):
    g = jnp.moveaxis(g, 0, -2)
    return g.reshape(g.shape[:-2] + (g.shape[-2] * g.shape[-1],))


def _rows_from_chips(g):
    g = jnp.moveaxis(g, 0, -3)
    return g.reshape(g.shape[:-3] + (g.shape[-3] * g.shape[-2], g.shape[-1]))


def _grad_blocks(dw, kind):
    if kind == "cols2d":
        K, N = dw.shape
        b = dw.reshape(2, K // 2, N_CHIPS, N // N_CHIPS).transpose(2, 0, 1, 3)
    elif kind == "rows2d":
        b = dw.reshape(N_CHIPS, 2, dw.shape[0] // 8, dw.shape[1])
    elif kind == "cols3d":
        L, K, N = dw.shape
        b = dw.reshape(L, K, N_CHIPS, N // N_CHIPS).transpose(2, 0, 1, 3)
    else:
        L, K, N = dw.shape
        b = dw.reshape(L, N_CHIPS, K // N_CHIPS, N).transpose(1, 0, 2, 3)
    return b.reshape(N_CHIPS, 2, -1, D_MODEL)


def _pad_rows(a, rows):
    return jnp.concatenate([a, jnp.zeros((rows - a.shape[0],) + a.shape[1:], a.dtype)], axis=0)


def _forward_backward(x, target, W):
    row = lambda a: a.reshape(1, -1)
    relu2 = lambda acc: (acc, jnp.square(jnp.maximum(acc, 0.0)))
    residual = lambda acc, res: (res + acc,)
    G = {}

    u0 = _rmsnorm_fwd(x, row(W["norm_mix_g"][0]), name="norm_mix0")
    proj = _matmul(u0, W["w_in"], mode="nn", out_dtypes=[F32], tn=896, name="in_proj")
    heads = lambda a: a.reshape(a.shape[0], SB_HEADS, SB_DH).transpose(1, 0, 2)
    unheads = lambda a: a.transpose(1, 0, 2).reshape(a.shape[1], SB_WIDTH)
    sq, sk, sv = (heads(proj[:, j * SB_WIDTH:(j + 1) * SB_WIDTH]).astype(MXU_DTYPE) for j in range(3))
    o_sb = _sb_fwd(sq, sk, sv, name="sb_fwd")
    hg_out, hg_o, hg_states = _hg_fwd(proj, W["hg_lb_logits"], row(W["hg_norm_g"]), name="hg_fwd")
    mix = jnp.concatenate([unheads(o_sb).astype(MXU_DTYPE), hg_out], axis=-1)
    h1 = _matmul(mix, W["w_out"], mode="nn", out_dtypes=[F32], epilogue=residual, tiles=[x], name="out_proj")
    u1 = _rmsnorm_fwd(h1, row(W["norm_ffn_g"][0]), name="norm_ffn0")
    a0, r0 = _matmul(u1, W["w_ff1"][0], mode="nn", out_dtypes=[F32, MXU_DTYPE], epilogue=relu2, name="ff1_0")
    h2 = _matmul(r0, W["w_ff2"][0], mode="nn", out_dtypes=[F32], epilogue=residual, tiles=[h1], name="ff2_0")
    u2 = _rmsnorm_fwd(h2, row(W["norm_mix_g"][1]), name="norm_mix1")
    p = _matmul(u2, W["w_glu"], mode="nn", out_dtypes=[F32], epilogue=lambda acc, b: (acc + b,), rows=[row(W["b_glu"])], name="glu_proj")
    w_dw = _pad_rows(W["w_dw"], CONV_HALO)
    ca, cy, cact = _conv_fwd(p, w_dw, row(W["b_dw"]), row(W["ln_g"]), row(W["ln_b"]), name="conv_fwd")
    h3 = _matmul(cact, W["w_pw"], mode="nn", out_dtypes=[F32], epilogue=lambda acc, res, b: (res + acc + b,),
                 tiles=[h2], rows=[row(W["b_pw"])], name="pw_proj")
    u3 = _rmsnorm_fwd(h3, row(W["norm_ffn_g"][1]), name="norm_ffn1")
    a1, r1 = _matmul(u3, W["w_ff1"][1], mode="nn", out_dtypes=[F32, MXU_DTYPE], epilogue=relu2, name="ff1_1")
    h4 = _matmul(r1, W["w_ff2"][1], mode="nn", out_dtypes=[F32], epilogue=residual, tiles=[h3], name="ff2_1")

    dh4, G["final_norm_g"], loss = _loss_head(h4, row(W["final_norm_g"]), target, name="loss_head")

    def mlp_bwd(dh, h_in, u, a, r, layer, tag):
        d_relu2 = lambda acc, a_blk: (acc * (2.0 * jnp.maximum(a_blk, 0.0)),)
        da = _matmul(dh, W["w_ff2"][layer], mode="nt", out_dtypes=[MXU_DTYPE], epilogue=d_relu2, tiles=[a], name="d_ff2_act" + tag)
        dw2 = _matmul(r, dh, mode="tn", out_dtypes=[F32], name="d_ff2_w" + tag)
        dw1 = _matmul(u, da, mode="tn", out_dtypes=[F32], name="d_ff1_w" + tag)
        du = _matmul(da, W["w_ff1"][layer], mode="nt", out_dtypes=[F32], name="d_ff1_act" + tag)
        dh_in, dg, cs = _rmsnorm_bwd(du, h_in, row(W["norm_ffn_g"][layer]), dh, name="d_norm_ffn" + tag)
        return dh_in, dg, cs, dw1, dw2

    dh3, dg_ffn1, cs_h3, dw1_1, dw2_1 = mlp_bwd(dh4, h3, u3, a1, r1, 1, "1")
    G["b_pw"] = cs_h3
    dact = _matmul(dh3, W["w_pw"], mode="nt", out_dtypes=[F32], name="d_pw_act")
    G["w_pw"] = _matmul(cact, dh3, mode="tn", out_dtypes=[F32], name="d_pw_w")
    dy, G["ln_g"], G["ln_b"], G["b_dw"] = _conv_bwd_norm(dact, cy, row(W["ln_g"]), row(W["ln_b"]), name="d_conv_norm")
    dp, G["w_dw"], G["b_glu"] = _conv_bwd_taps(dy, ca, p, w_dw, name="d_conv_taps")
    G["w_glu"] = _matmul(u2, dp, mode="tn", out_dtypes=[F32], name="d_glu_w")
    du2 = _matmul(dp, W["w_glu"], mode="nt", out_dtypes=[F32], name="d_glu_act")
    dh2, dg_mix1, _ = _rmsnorm_bwd(du2, h2, row(W["norm_mix_g"][1]), dh3, name="d_norm_mix1")
    dh1, dg_ffn0, _, dw1_0, dw2_0 = mlp_bwd(dh2, h1, u1, a0, r0, 0, "0")
    G["w_ff1"], G["w_ff2"] = jnp.stack([dw1_0, dw1_1]), jnp.stack([dw2_0, dw2_1])
    G["norm_ffn_g"] = jnp.concatenate([dg_ffn0, dg_ffn1], axis=0)
    dmix = _matmul(dh1, W["w_out"], mode="nt", out_dtypes=[F32], name="d_out_act")
    G["w_out"] = _matmul(mix, dh1, mode="tn", out_dtypes=[F32], name="d_out_w")
    dsq, dsk, dsv = _sb_bwd(sq, sk, sv, heads(dmix[:, :SB_WIDTH]), name="sb_bwd")
    d_hg, G["hg_lb"], G["hg_norm_g"] = _hg_bwd(proj, hg_o, hg_states, dmix, W["hg_lb_logits"], row(W["hg_norm_g"]), name="hg_bwd")
    dproj = jnp.concatenate([unheads(dsq), unheads(dsk), unheads(dsv), d_hg], axis=-1).astype(MXU_DTYPE)
    G["w_in"] = _matmul(u0, dproj, mode="tn", out_dtypes=[F32], tn=896, name="d_in_w")
    du0 = _matmul(dproj, W["w_in"], mode="nt", out_dtypes=[F32], tk=896, name="d_in_act")
    dx, dg_mix0, _ = _rmsnorm_bwd(du0, x, row(W["norm_mix_g"][0]), dh1, name="d_norm_mix0")
    G["norm_mix_g"] = jnp.concatenate([dg_mix0, dg_mix1], axis=0)
    return loss, dx, G


BIG = (("w_in_ab", "w_in", "cols2d"), ("w_out_ab", "w_out", "rows2d"), ("conv_w_glu", "w_glu", "cols2d"),
       ("conv_w_pw", "w_pw", "rows2d"), ("w_ff1", "w_ff1", "cols3d"), ("w_ff2", "w_ff2", "rows3d"))
SMALL_SHARDED = ("conv_b_glu", "conv_w_dw", "conv_b_dw", "conv_ln_g", "conv_ln_b", "conv_b_pw")
REPLICATED = ("norm_mix_g", "norm_ffn_g", "hg_lb_logits", "hg_norm_g", "final_norm_g")
ORDER = ("norm_mix_g", "norm_ffn_g", "w_in_ab", "w_out_ab", "hg_lb_logits", "hg_norm_g", "conv_w_glu", "conv_b_glu",
         "conv_w_dw", "conv_b_dw", "conv_ln_g", "conv_ln_b", "conv_w_pw", "conv_b_pw", "w_ff1", "w_ff2", "final_norm_g")


def _step(x, loss_target, w, m, v):
    D = D_MODEL
    x2, t2 = x.reshape(-1, D), loss_target.reshape(-1, D)
    chip = 2 * lax.axis_index("x") + lax.axis_index("y")

    small_in = jnp.concatenate([w["conv_b_glu"].reshape(2, 256), w["conv_w_dw"].reshape(CONV_WIDTH, 256)] +
                               [w[n].reshape(1, 256) for n in ("conv_b_dw", "conv_ln_g", "conv_ln_b", "conv_b_pw")], axis=0)
    gathered = _gather_chips([w[n].astype(MXU_DTYPE) for n, _, _ in BIG] + [_pad_rows(small_in, 40)], name="gather_weights")
    gw = dict(zip([s for _, s, _ in BIG], gathered[:-1]))
    gs = gathered[-1]
    vec = lambda r0, r1: gs[:, r0:r1].transpose(1, 0, 2).reshape(r1 - r0, N_CHIPS * 256)
    W = {
        "w_in": _cols_from_chips(gw["w_in"][:, 0]), "w_out": _rows_from_chips(gw["w_out"][:, 0]),
        "w_glu": _cols_from_chips(gw["w_glu"][:, 0]), "w_pw": _rows_from_chips(gw["w_pw"][:, 0]),
        "w_ff1": _cols_from_chips(gw["w_ff1"]), "w_ff2": _rows_from_chips(gw["w_ff2"]),
        "b_glu": gs[:, 0:2].reshape(2 * D), "w_dw": vec(2, 33), "b_dw": vec(33, 34)[0], "ln_g": vec(34, 35)[0],
        "ln_b": vec(35, 36)[0], "b_pw": vec(36, 37)[0],
        "norm_mix_g": w["norm_mix_g"], "norm_ffn_g": w["norm_ffn_g"], "hg_lb_logits": w["hg_lb_logits"],
        "hg_norm_g": w["hg_norm_g"], "final_norm_g": w["final_norm_g"],
    }

    loss, dx, G = _forward_backward(x2, t2, W)

    pack = jnp.concatenate([
        G["norm_mix_g"], G["norm_ffn_g"], G["final_norm_g"], jnp.concatenate([G["hg_norm_g"], G["hg_lb"]], axis=1),
        _pad_rows(jnp.broadcast_to(loss, (1, D)), 2), G["b_glu"].reshape(2, D), G["w_dw"], G["b_dw"], G["ln_g"], G["ln_b"], G["b_pw"],
    ], axis=0)
    pack = _pad_rows(pack, SMALL_ROWS)
    (packs,) = _gather_all([pack], name="gather_small_grads")
    ssum, d_logits = _small_reduce(packs, w["hg_lb_logits"], name="reduce_small_grads")
    cut = lambda r0, r1: lax.dynamic_slice(ssum, (r0, chip * 256), (r1 - r0, 256))
    grads = {
        "norm_mix_g": ssum[0:2], "norm_ffn_g": ssum[2:4], "final_norm_g": ssum[4], "hg_norm_g": ssum[5, :HG_WIDTH].reshape(1, HG_HEADS, HG_DH),
        "hg_lb_logits": d_logits,
        "conv_b_glu": lax.dynamic_slice(ssum[8:10].reshape(1, 2 * D), (0, chip * 512), (1, 512)),
        "conv_w_dw": cut(10, 10 + CONV_WIDTH).reshape(1, CONV_WIDTH, 256),
        "conv_b_dw": cut(42, 43), "conv_ln_g": cut(43, 44), "conv_ln_b": cut(44, 45), "conv_b_pw": cut(45, 46),
    }
    loss_out = ssum[6, 0]

    blocks = jnp.concatenate([_grad_blocks(G[s], kind) for _, s, kind in BIG], axis=2)
    c = lax.axis_index("c")
    (from_pair,) = _pair_swap([blocks.transpose(1, 0, 2, 3)], name="grads_pair_swap")
    mine = lax.dynamic_index_in_dim(blocks, c, axis=1, keepdims=False)
    chip_sum = _add_pair(mine, from_pair, name="grads_pair_add")
    per_chip = _chip_scatter(chip_sum, name="grads_chip_scatter")
    half = _sum_leading(per_chip, name="grads_chip_add")
    full = _pair_gather(half, name="grads_pair_gather")
    off = 0
    for n, s, kind in BIG:
        shard = w[n].shape
        rows = w[n].size // (2 * D)
        grads[n] = full[:, off:off + rows].reshape(shard)
        off += rows

    delta, new_m, new_v = {}, {}, {}
    for n, _, _ in BIG:
        view = lambda a: a.reshape(-1, a.shape[-1])
        outs = _adamw(view(w[n]), view(grads[n]), view(m[n]), view(v[n]), name="adamw_" + n)
        delta[n], new_m[n], new_v[n] = (o.reshape(w[n].shape) for o in outs)
    small = SMALL_SHARDED + REPLICATED
    sizes = [w[n].size for n in small]
    total = sum(sizes)
    rows = -(-total // (8 * D)) * 8
    packed = lambda d: _pad_rows(jnp.concatenate([d[n].reshape(-1) for n in small]).reshape(-1, 128), rows * 8).reshape(rows, D)
    outs = _adamw(packed(w), packed(grads), packed(m), packed(v), name="adamw_small")
    off = 0
    for n, size in zip(small, sizes):
        delta[n], new_m[n], new_v[n] = (o.reshape(-1)[off:off + size].reshape(w[n].shape) for o in outs)
        off += size
    grads = {n: grads[n].reshape(w[n].shape) for n in ORDER}
    return (loss_out, dx.reshape(x.shape), *[grads[n] for n in ORDER], *[delta[n] for n in ORDER],
            *[new_m[n] for n in ORDER], *[new_v[n] for n in ORDER])


def kernel(x, norm_mix_g, norm_ffn_g, w_in_ab, w_out_ab, hg_lb_logits, hg_norm_g, conv_w_glu, conv_b_glu, conv_w_dw, conv_b_dw, conv_ln_g, conv_ln_b, conv_w_pw, conv_b_pw, w_ff1, w_ff2, final_norm_g, loss_target, m_norm_mix_g, m_norm_ffn_g, m_w_in_ab, m_w_out_ab, m_hg_lb_logits, m_hg_norm_g, m_conv_w_glu, m_conv_b_glu, m_conv_w_dw, m_conv_b_dw, m_conv_ln_g, m_conv_ln_b, m_conv_w_pw, m_conv_b_pw, m_w_ff1, m_w_ff2, m_final_norm_g, v_norm_mix_g, v_norm_ffn_g, v_w_in_ab, v_w_out_ab, v_hg_lb_logits, v_hg_norm_g, v_conv_w_glu, v_conv_b_glu, v_conv_w_dw, v_conv_b_dw, v_conv_ln_g, v_conv_ln_b, v_conv_w_pw, v_conv_b_pw, v_w_ff1, v_w_ff2, v_final_norm_g):
    args = locals()
    w = {n: args[n] for n in ORDER}
    m = {n: args["m_" + n] for n in ORDER}
    v = {n: args["v_" + n] for n in ORDER}
    return _step(x, loss_target, w, m, v)
```

```python
import functools

import jax
import jax.numpy as jnp
from jax import lax
from jax.experimental import pallas as pl
from jax.experimental.pallas import tpu as pltpu

F32 = jnp.float32
MXU_DTYPE = jnp.bfloat16
MESH = pl.DeviceIdType.MESH

D_MODEL = 1024
SB_HEADS, SB_DH, SB_WIDTH = 8, 64, 512
SB_KEYS = 512
SB_ROWS_FWD, SB_ROWS_BWD = 512, 256
HG_HEADS, HG_DH, HG_WIDTH = 4, 128, 512
HG_CHUNK = 16
HG_TOKENS = 256
IN_WIDTH = 3 * SB_WIDTH + 4 * HG_WIDTH
CONV_WIDTH = 31
CONV_HALO = 32
D_FF = 4096
RMS_EPS = 1e-6
LN_EPS = 1e-5
N_CHIPS = 4
N_DEV = 8
SMALL_ROWS = 48
VMEM_LIMIT = 56 * 1024 * 1024

ADAM_LR, ADAM_B1, ADAM_B2, ADAM_EPS, ADAM_WD, ADAM_STEP = 0.001, 0.9, 0.999, 1e-08, 0.01, 10


def _params(*sem):
    return pltpu.CompilerParams(dimension_semantics=sem, vmem_limit_bytes=VMEM_LIMIT)


def _mx(v):
    return v.astype(MXU_DTYPE)


def _dot(a, b):
    return jnp.dot(_mx(a), _mx(b), preferred_element_type=F32)


def _dot_nt(a, b):
    return lax.dot_general(_mx(a), _mx(b), (((1,), (1,)), ((), ())), preferred_element_type=F32)


def _dot_tn(a, b):
    return lax.dot_general(_mx(a), _mx(b), (((0,), (0,)), ((), ())), preferred_element_type=F32)


def _split_dot(v, u):
    hi = _mx(v)
    lo = _mx(v - hi.astype(F32))
    return (jnp.dot(hi, u, preferred_element_type=F32) + jnp.dot(lo, u, preferred_element_type=F32))


def _matmul(a, b, *, mode, out_dtypes, epilogue=None, tiles=(), rows=(), tm=512, tn=1024, tk=1024, name):
    if mode == "nn":
        (M, K), N = a.shape, b.shape[1]
    elif mode == "nt":
        (M, K), N = a.shape, b.shape[0]
    else:
        (K, M), N = a.shape, b.shape[1]
    tm, tn, tk = min(tm, M), min(tn, N), min(tk, K)
    assert M % tm == 0 and N % tn == 0 and K % tk == 0, (name, M, N, K)
    nk = K // tk
    a_spec = pl.BlockSpec((tk, tm), lambda i, j, k: (k, i)) if mode == "tn" else pl.BlockSpec((tm, tk), lambda i, j, k: (i, k))
    b_spec = pl.BlockSpec((tn, tk), lambda i, j, k: (j, k)) if mode == "nt" else pl.BlockSpec((tk, tn), lambda i, j, k: (k, j))
    dims = {"nn": ((1,), (0,)), "nt": ((1,), (1,)), "tn": ((0,), (0,))}[mode]
    n_t, n_r, n_o = len(tiles), len(rows), len(out_dtypes)
    if epilogue is None:
        epilogue = lambda acc: (acc,)

    def body(a_ref, b_ref, *rest):
        extra, outs, acc_ref = rest[:n_t + n_r], rest[n_t + n_r:n_t + n_r + n_o], rest[-1]
        k = pl.program_id(2)

        @pl.when(k == 0)
        def _():
            acc_ref[...] = jnp.zeros_like(acc_ref)

        acc_ref[...] += lax.dot_general(_mx(a_ref[...]), _mx(b_ref[...]), (dims, ((), ())), preferred_element_type=F32)

        @pl.when(k == nk - 1)
        def _():
            res = epilogue(acc_ref[...], *[e[...] for e in extra])
            for o_ref, r in zip(outs, res):
                o_ref[...] = r.astype(o_ref.dtype)

    tile_spec = pl.BlockSpec((tm, tn), lambda i, j, k: (i, j))
    row_spec = pl.BlockSpec((1, tn), lambda i, j, k: (0, j))
    outs = pl.pallas_call(
        body, grid=(M // tm, N // tn, nk),
        in_specs=[a_spec, b_spec] + [tile_spec] * n_t + [row_spec] * n_r,
        out_specs=[tile_spec] * n_o,
        out_shape=[jax.ShapeDtypeStruct((M, N), dt) for dt in out_dtypes],
        scratch_shapes=[pltpu.VMEM((tm, tn), F32)],
        compiler_params=_params("parallel", "parallel", "arbitrary"), name=name,
    )(a, b, *tiles, *rows)
    return outs[0] if n_o == 1 else outs


def _token_block(T):
    return min(512, T)


def _rmsnorm_fwd(h, g, *, name):
    T, D = h.shape
    tb = _token_block(T)

    def body(h_ref, g_ref, u_ref):
        x = h_ref[...]
        r = lax.rsqrt(jnp.mean(x * x, axis=-1, keepdims=True) + RMS_EPS)
        u_ref[...] = (x * r * g_ref[...]).astype(u_ref.dtype)

    blk = pl.BlockSpec((tb, D), lambda i: (i, 0))
    return pl.pallas_call(
        body, grid=(T // tb,), in_specs=[blk, pl.BlockSpec((1, D), lambda i: (0, 0))], out_specs=blk,
        out_shape=jax.ShapeDtypeStruct((T, D), MXU_DTYPE), compiler_params=_params("parallel"), name=name,
    )(h, g)


def _rms_bwd_math(x, g, du):
    r = lax.rsqrt(jnp.mean(x * x, axis=-1, keepdims=True) + RMS_EPS)
    gd = g * du
    dx = r * gd - x * (r * r * r) * jnp.mean(gd * x, axis=-1, keepdims=True)
    return dx, du * x * r


def _rmsnorm_bwd(du, h, g, dres, *, name):
    T, D = h.shape
    tb = _token_block(T)

    def body(du_ref, h_ref, g_ref, dres_ref, dh_ref, dg_ref, cs_ref):
        @pl.when(pl.program_id(0) == 0)
        def _():
            dg_ref[...] = jnp.zeros_like(dg_ref)
            cs_ref[...] = jnp.zeros_like(cs_ref)

        dx, dg_terms = _rms_bwd_math(h_ref[...], g_ref[...], du_ref[...])
        dh = dres_ref[...] + dx
        dh_ref[...] = dh
        dg_ref[...] += jnp.sum(dg_terms, axis=0, keepdims=True)
        cs_ref[...] += jnp.sum(dh, axis=0, keepdims=True)

    blk = pl.BlockSpec((tb, D), lambda i: (i, 0))
    row = pl.BlockSpec((1, D), lambda i: (0, 0))
    return pl.pallas_call(
        body, grid=(T // tb,), in_specs=[blk, blk, row, blk], out_specs=[blk, row, row],
        out_shape=[jax.ShapeDtypeStruct((T, D), F32), jax.ShapeDtypeStruct((1, D), F32), jax.ShapeDtypeStruct((1, D), F32)],
        compiler_params=_params("arbitrary"), name=name,
    )(du, h, g, dres)


def _loss_head(h, g, target, *, name):
    T, D = h.shape
    tb = _token_block(T)

    def body(h_ref, g_ref, t_ref, dh_ref, dg_ref, loss_ref):
        @pl.when(pl.program_id(0) == 0)
        def _():
            dg_ref[...] = jnp.zeros_like(dg_ref)
            loss_ref[...] = jnp.zeros_like(loss_ref)

        x, gg = h_ref[...], g_ref[...]
        r = lax.rsqrt(jnp.mean(x * x, axis=-1, keepdims=True) + RMS_EPS)
        diff = x * r * gg - t_ref[...]
        per_token = jnp.mean(diff * diff, axis=-1, keepdims=True)
        loss_ref[...] += 0.5 * jnp.sum(per_token, axis=0, keepdims=True)
        dx, dg_terms = _rms_bwd_math(x, gg, diff / D)
        dh_ref[...] = dx
        dg_ref[...] += jnp.sum(dg_terms, axis=0, keepdims=True)

    blk = pl.BlockSpec((tb, D), lambda i: (i, 0))
    row = pl.BlockSpec((1, D), lambda i: (0, 0))
    return pl.pallas_call(
        body, grid=(T // tb,), in_specs=[blk, row, blk], out_specs=[blk, row, pl.BlockSpec((1, 1), lambda i: (0, 0))],
        out_shape=[jax.ShapeDtypeStruct((T, D), F32), jax.ShapeDtypeStruct((1, D), F32), jax.ShapeDtypeStruct((1, 1), F32)],
        compiler_params=_params("arbitrary"), name=name,
    )(h, g, target)


def _sb_scores(qm, ks, later, tri, mask):
    z = _dot_nt(qm, ks)
    sp = jnp.maximum(z, 0.0) + jnp.log(1.0 + jnp.exp(-jnp.abs(z)))
    lb = z - sp
    if mask is not None:
        sp = jnp.where(mask, sp, 0.0)
    w = jnp.exp(lb - (_split_dot(sp, tri) + later))
    if mask is not None:
        w = jnp.where(mask, w, 0.0)
    return sp, lb, w


def _sb_setup(q_ref, rows):
    i, hsel = pl.program_id(1), pl.program_id(2)
    lane = lax.broadcasted_iota(jnp.int32, (rows, 2 * SB_DH), 1)
    mine = (lane >= SB_DH) == (hsel == 1)
    diag = (i * rows) // SB_KEYS
    t = i * rows + lax.broadcasted_iota(jnp.int32, (rows, SB_KEYS), 0)
    s = diag * SB_KEYS + lax.broadcasted_iota(jnp.int32, (rows, SB_KEYS), 1)
    a = lax.broadcasted_iota(jnp.int32, (SB_KEYS, SB_KEYS), 0)
    b = lax.broadcasted_iota(jnp.int32, (SB_KEYS, SB_KEYS), 1)
    return i, hsel, mine, diag, s < t, _mx(a > b), _mx(a < b)


def _sb_keys(j):
    return pl.ds(pl.multiple_of(j * SB_KEYS, SB_KEYS), SB_KEYS)


def _sb_specs(T, rows):
    pair = lambda col0: pl.BlockSpec((rows, 2 * SB_DH), lambda p, i, h: (i, col0 + p))
    whole = lambda col0: pl.BlockSpec((T, 2 * SB_DH), lambda p, i, h: (0, col0 + p))
    return pair, whole


def _sb_fwd(qkv, *, name):
    T = qkv.shape[0]
    rows = min(SB_ROWS_FWD, T)
    scale = SB_DH ** -0.5
    n_pairs = SB_HEADS // 2

    def body(q_ref, k_ref, v_ref, o_ref):
        i, hsel, mine, diag, mask, tri, _ = _sb_setup(q_ref, rows)
        qm = jnp.where(mine, q_ref[...], 0) * scale

        def tile(j, m, later, acc):
            sp, _, w = _sb_scores(qm, k_ref[_sb_keys(j), :], later, tri, m)
            return later + jnp.sum(sp, axis=1, keepdims=True), acc + _dot(w, v_ref[_sb_keys(j), :])

        carry = tile(diag, mask, jnp.zeros((rows, 1), F32), jnp.zeros((rows, 2 * SB_DH), F32))
        _, acc = lax.fori_loop(0, diag, lambda it, cr: tile(diag - 1 - it, None, *cr), carry)
        res = jnp.where(mine, acc, 0.0)

        @pl.when(hsel == 0)
        def _():
            o_ref[...] = res

        @pl.when(hsel == 1)
        def _():
            o_ref[...] += res

    pair, whole = _sb_specs(T, rows)
    return pl.pallas_call(
        body, grid=(n_pairs, T // rows, 2), in_specs=[pair(0), whole(n_pairs), whole(2 * n_pairs)], out_specs=pair(0),
        out_shape=jax.ShapeDtypeStruct((T, SB_WIDTH), F32), compiler_params=_params("parallel", "arbitrary", "arbitrary"), name=name,
    )(qkv, qkv, qkv)


def _sb_bwd(qkv, dmix, *, name):
    T = qkv.shape[0]
    rows = min(SB_ROWS_BWD, T)
    scale = SB_DH ** -0.5
    n_pairs = SB_HEADS // 2

    def body(q_ref, k_ref, v_ref, do_ref, dq_ref, dk_ref, dv_ref, da_ref, beta_ref):
        i, hsel, mine, diag, mask, tri, tri_before = _sb_setup(q_ref, rows)

        @pl.when((i == 0) & (hsel == 0))
        def _():
            dk_ref[...] = jnp.zeros_like(dk_ref)
            dv_ref[...] = jnp.zeros_like(dv_ref)

        q_raw = jnp.where(mine, q_ref[...], 0)
        qm = q_raw * scale
        do_m = _mx(jnp.where(mine, do_ref[...], 0.0))

        def weights(j, m, later):
            sp, lb, w = _sb_scores(qm, k_ref[_sb_keys(j), :], later, tri, m)
            da_ref[j] = _dot_nt(do_m, v_ref[_sb_keys(j), :]) * w
            beta_ref[j] = jnp.exp(lb)
            dv_ref[_sb_keys(j), :] += _dot_tn(w, do_m)
            return later + jnp.sum(sp, axis=1, keepdims=True)

        later = weights(diag, mask, jnp.zeros((rows, 1), F32))
        lax.fori_loop(0, diag, lambda it, c: weights(diag - 1 - it, None, c), later)

        def logits(j, m, before, dq):
            da = da_ref[j]
            dz = da - beta_ref[j] * (da + _split_dot(da, tri_before) + before)
            if m is not None:
                dz = jnp.where(m, dz, 0.0)
            dzs = _mx(dz * scale)
            dk_ref[_sb_keys(j), :] += _dot_tn(dzs, q_raw)
            return before + jnp.sum(da, axis=1, keepdims=True), dq + _dot(dzs, k_ref[_sb_keys(j), :])

        carry = (jnp.zeros((rows, 1), F32), jnp.zeros((rows, 2 * SB_DH), F32))
        carry = lax.fori_loop(0, diag, lambda j, cr: logits(j, None, *cr), carry)
        res = jnp.where(mine, logits(diag, mask, *carry)[1], 0.0)

        @pl.when(hsel == 0)
        def _():
            dq_ref[...] = res

        @pl.when(hsel == 1)
        def _():
            dq_ref[...] += res

    pair, whole = _sb_specs(T, rows)
    shp = jax.ShapeDtypeStruct((T, SB_WIDTH), F32)
    n_tiles = T // SB_KEYS
    return pl.pallas_call(
        body, grid=(n_pairs, T // rows, 2), in_specs=[pair(0), whole(n_pairs), whole(2 * n_pairs), pair(0)],
        out_specs=[pair(0), whole(0), whole(0)], out_shape=[shp, shp, shp],
        scratch_shapes=[pltpu.VMEM((n_tiles, rows, SB_KEYS), F32), pltpu.VMEM((n_tiles, rows, SB_KEYS), F32)],
        compiler_params=_params("parallel", "arbitrary", "arbitrary"), name=name,
    )(qkv, qkv, qkv, dmix)


def _chunk_iota():
    return lax.broadcasted_iota(jnp.int32, (HG_CHUNK, HG_DH), 0)


def _chunk_cumsum(x, reverse=False):
    row = _chunk_iota()
    for sh in (1, 2, 4, 8):
        if reverse:
            x = x + jnp.where(row < HG_CHUNK - sh, pltpu.roll(x, HG_CHUNK - sh, 0), 0.0)
        else:
            x = x + jnp.where(row >= sh, pltpu.roll(x, sh, 0), 0.0)
    return x


def _hg_lower_bound(logits_ref):
    lg = logits_ref[...]
    e = jnp.exp(lg - jnp.max(lg, axis=0, keepdims=True))
    return e[0:1, :] / jnp.sum(e, axis=0, keepdims=True)


def _hg_chunk_terms(fr, q, lb):
    sig = jax.nn.sigmoid(fr)
    f = lb + (1.0 - lb) * sig
    kk = 1.0 - f
    G = _chunk_cumsum(jnp.log(f))
    g_last = G[HG_CHUNK - 1:HG_CHUNK, :]
    e_g, e_ng, e_lg = jnp.exp(G), jnp.exp(-G), jnp.exp(g_last - G)
    return dict(sig=sig, f=f, kk=kk, e_g=e_g, e_ng=e_ng, e_lg=e_lg, q_dec=q * e_g, k_intra=kk * e_ng,
                k_state=kk * e_lg, decay=jnp.exp(g_last))


def _hg_causal():
    c = lax.broadcasted_iota(jnp.int32, (HG_CHUNK, HG_CHUNK), 0)
    s = lax.broadcasted_iota(jnp.int32, (HG_CHUNK, HG_CHUNK), 1)
    return s <= c


def _hg_specs(T, tb, col0, order):
    return [pl.BlockSpec((tb, HG_WIDTH), functools.partial(lambda i, j: (order(i), j), j=col0 + j)) for j in range(4)]


def _hg_fwd(proj, logits, norm_g, *, name):
    T = proj.shape[0]
    tb = min(HG_TOKENS, T)
    nch = tb // HG_CHUNK

    def body(q_ref, f_ref, i_ref, gate_ref, lg_ref, ng_ref, out_ref, o_ref, s_ref, st_ref):
        @pl.when(pl.program_id(0) == 0)
        def _():
            st_ref[...] = jnp.zeros_like(st_ref)

        lb_all = _hg_lower_bound(lg_ref)
        causal = _hg_causal()

        def chunk(ci, _):
            rows = pl.ds(pl.multiple_of(ci * HG_CHUNK, HG_CHUNK), HG_CHUNK)
            for hh in range(HG_HEADS):
                cols = slice(hh * HG_DH, (hh + 1) * HG_DH)
                t = _hg_chunk_terms(f_ref[rows, cols], q_ref[rows, cols], lb_all[:, cols])
                v = i_ref[rows, cols]
                st = st_ref[hh]
                scores = jnp.where(causal, _dot_nt(t["q_dec"], t["k_intra"]), 0.0)
                o_ref[rows, cols] = _dot(scores, v) + _dot_nt(t["q_dec"], st)
                s_ref[ci, hh] = st
                st_ref[hh] = st * t["decay"] + _dot_tn(v, t["k_state"])
            return 0

        lax.fori_loop(0, nch, chunk, 0)
        for hh in range(HG_HEADS):
            cols = slice(hh * HG_DH, (hh + 1) * HG_DH)
            o = o_ref[:, cols]
            gate = gate_ref[:, cols]
            on = o * lax.rsqrt(jnp.mean(o * o, axis=-1, keepdims=True) + RMS_EPS) * ng_ref[:, cols]
            out_ref[:, cols] = (on * (gate * jax.nn.sigmoid(gate))).astype(out_ref.dtype)

    blk = pl.BlockSpec((tb, HG_WIDTH), lambda i: (i, 0))
    return pl.pallas_call(
        body, grid=(T // tb,),
        in_specs=_hg_specs(T, tb, 3, lambda i: i) + [pl.BlockSpec((3, HG_WIDTH), lambda i: (0, 0)), pl.BlockSpec((1, HG_WIDTH), lambda i: (0, 0))],
        out_specs=[blk, blk, pl.BlockSpec((nch, HG_HEADS, HG_DH, HG_DH), lambda i: (i, 0, 0, 0))],
        out_shape=[jax.ShapeDtypeStruct((T, HG_WIDTH), MXU_DTYPE), jax.ShapeDtypeStruct((T, HG_WIDTH), F32),
                   jax.ShapeDtypeStruct((T // HG_CHUNK, HG_HEADS, HG_DH, HG_DH), F32)],
        scratch_shapes=[pltpu.VMEM((HG_HEADS, HG_DH, HG_DH), F32)],
        compiler_params=_params("arbitrary"), name=name,
    )(proj, proj, proj, proj, logits, norm_g)


def _hg_bwd(proj, o_raw, states, dmix, logits, norm_g, *, name):
    T = proj.shape[0]
    tb = min(HG_TOKENS, T)
    nch = tb // HG_CHUNK
    nb = T // tb
    rev = lambda i: nb - 1 - i

    def body(q_ref, f_ref, i_ref, gate_ref, o_ref, s_ref, dout_ref, lg_ref, ng_ref, dp_ref, dlb_ref, dng_ref, do_ref, dst_ref):
        @pl.when(pl.program_id(0) == 0)
        def _():
            dst_ref[...] = jnp.zeros_like(dst_ref)
            dlb_ref[...] = jnp.zeros_like(dlb_ref)
            dng_ref[...] = jnp.zeros_like(dng_ref)

        lb_all = _hg_lower_bound(lg_ref)
        causal = _hg_causal()
        row = _chunk_iota()
        for hh in range(HG_HEADS):
            cols = slice(hh * HG_DH, (hh + 1) * HG_DH)
            o, gate, dout, ng = o_ref[:, cols], gate_ref[:, cols], dout_ref[:, cols], ng_ref[:, cols]
            sg = jax.nn.sigmoid(gate)
            r = lax.rsqrt(jnp.mean(o * o, axis=-1, keepdims=True) + RMS_EPS)
            oh = o * r
            dp_ref[:, 3 * HG_WIDTH + hh * HG_DH:3 * HG_WIDTH + (hh + 1) * HG_DH] = dout * (oh * ng) * (sg * (1.0 + gate * (1.0 - sg)))
            don = dout * (gate * sg)
            dng_ref[:, cols] += jnp.sum(don * oh, axis=0, keepdims=True)
            doh = don * ng
            do_ref[:, cols] = r * (doh - oh * jnp.mean(doh * oh, axis=-1, keepdims=True))

        def chunk(it, _):
            ci = nch - 1 - it
            rows = pl.ds(pl.multiple_of(ci * HG_CHUNK, HG_CHUNK), HG_CHUNK)
            for hh in range(HG_HEADS):
                cols = slice(hh * HG_DH, (hh + 1) * HG_DH)
                lb = lb_all[:, cols]
                t = _hg_chunk_terms(f_ref[rows, cols], q_ref[rows, cols], lb)
                v, do_c, st, dst = i_ref[rows, cols], do_ref[rows, cols], s_ref[ci, hh], dst_ref[hh]
                scores = jnp.where(causal, _dot_nt(t["q_dec"], t["k_intra"]), 0.0)
                dscores = jnp.where(causal, _dot_nt(do_c, v), 0.0)
                dqd = _dot(dscores, t["k_intra"]) + _dot(do_c, st)
                dki = _dot_tn(dscores, t["q_dec"])
                dks = _dot(v, dst)
                dp_ref[rows, 2 * HG_WIDTH + hh * HG_DH:2 * HG_WIDTH + (hh + 1) * HG_DH] = _dot_tn(scores, do_c) + _dot_nt(t["k_state"], dst)
                ddecay = jnp.sum(st * dst, axis=0, keepdims=True)
                dst_ref[hh] = dst * t["decay"] + _dot_tn(do_c, t["q_dec"])
                dks_ks = dks * t["k_state"]
                d_glast = jnp.sum(dks_ks, axis=0, keepdims=True) + ddecay * t["decay"]
                d_g = dqd * t["q_dec"] - dki * t["k_intra"] - dks_ks + jnp.where(row == HG_CHUNK - 1, d_glast, 0.0)
                df = _chunk_cumsum(d_g, reverse=True) / t["f"] - (dki * t["e_ng"] + dks * t["e_lg"])
                dp_ref[rows, hh * HG_DH:(hh + 1) * HG_DH] = dqd * t["e_g"]
                dp_ref[rows, HG_WIDTH + hh * HG_DH:HG_WIDTH + (hh + 1) * HG_DH] = df * (1.0 - lb) * t["sig"] * (1.0 - t["sig"])
                dlb_ref[:, cols] += jnp.sum(df * (1.0 - t["sig"]), axis=0, keepdims=True)
            return 0

        lax.fori_loop(0, nch, chunk, 0)

    blk = pl.BlockSpec((tb, HG_WIDTH), lambda i: (rev(i), 0))
    row_spec = pl.BlockSpec((1, HG_WIDTH), lambda i: (0, 0))
    return pl.pallas_call(
        body, grid=(nb,),
        in_specs=_hg_specs(T, tb, 3, rev) + [
            blk, pl.BlockSpec((nch, HG_HEADS, HG_DH, HG_DH), lambda i: (rev(i), 0, 0, 0)),
            pl.BlockSpec((tb, HG_WIDTH), lambda i: (rev(i), 1)), pl.BlockSpec((3, HG_WIDTH), lambda i: (0, 0)), row_spec],
        out_specs=[pl.BlockSpec((tb, 4 * HG_WIDTH), lambda i: (rev(i), 0)), row_spec, row_spec],
        out_shape=[jax.ShapeDtypeStruct((T, 4 * HG_WIDTH), F32), jax.ShapeDtypeStruct((1, HG_WIDTH), F32), jax.ShapeDtypeStruct((1, HG_WIDTH), F32)],
        scratch_shapes=[pltpu.VMEM((tb, HG_WIDTH), F32), pltpu.VMEM((HG_HEADS, HG_DH, HG_DH), F32)],
        compiler_params=_params("arbitrary"), name=name,
    )(proj, proj, proj, proj, o_raw, states, dmix, logits, norm_g)


def _conv_fwd(p, w_dw, b_dw, ln_g, ln_b, *, name):
    T, D = p.shape[0], p.shape[1] // 2
    tb = _token_block(T)
    hpb = tb // CONV_HALO

    def body(p1_ref, p2_ref, q1_ref, q2_ref, w_ref, bdw_ref, g_ref, b_ref, a_ref, y_ref, act_ref, buf):
        i = pl.program_id(0)
        a = p1_ref[...] * jax.nn.sigmoid(p2_ref[...])
        buf[0:CONV_HALO, :] = jnp.where(i > 0, q1_ref[...] * jax.nn.sigmoid(q2_ref[...]), 0.0)
        buf[CONV_HALO:, :] = a
        a_ref[...] = a
        y = jnp.zeros((tb, D), F32) + bdw_ref[...]
        for k in range(CONV_WIDTH):
            y = y + buf[pl.ds(CONV_HALO - CONV_WIDTH + 1 + k, tb), :] * w_ref[k:k + 1, :]
        y_ref[...] = y
        mu = jnp.mean(y, axis=-1, keepdims=True)
        yc = y - mu
        s = yc * lax.rsqrt(jnp.mean(yc * yc, axis=-1, keepdims=True) + LN_EPS) * g_ref[...] + b_ref[...]
        act_ref[...] = (s * jax.nn.sigmoid(s)).astype(act_ref.dtype)

    prev = lambda i: jnp.maximum(i * hpb - 1, 0)
    blk = pl.BlockSpec((tb, D), lambda i: (i, 0))
    row = pl.BlockSpec((1, D), lambda i: (0, 0))
    return pl.pallas_call(
        body, grid=(T // tb,),
        in_specs=[blk, pl.BlockSpec((tb, D), lambda i: (i, 1)), pl.BlockSpec((CONV_HALO, D), lambda i: (prev(i), 0)),
                  pl.BlockSpec((CONV_HALO, D), lambda i: (prev(i), 1)), pl.BlockSpec((CONV_HALO, D), lambda i: (0, 0)), row, row, row],
        out_specs=[blk, blk, blk],
        out_shape=[jax.ShapeDtypeStruct((T, D), F32), jax.ShapeDtypeStruct((T, D), F32), jax.ShapeDtypeStruct((T, D), MXU_DTYPE)],
        scratch_shapes=[pltpu.VMEM((tb + CONV_HALO, D), F32)],
        compiler_params=_params("parallel"), name=name,
    )(p, p, p, p, w_dw, b_dw, ln_g, ln_b)


def _conv_bwd_norm(dact, y, ln_g, ln_b, *, name):
    T, D = y.shape
    tb = _token_block(T)

    def body(da_ref, y_ref, g_ref, b_ref, dy_ref, dg_ref, db_ref, cs_ref):
        @pl.when(pl.program_id(0) == 0)
        def _():
            dg_ref[...] = jnp.zeros_like(dg_ref)
            db_ref[...] = jnp.zeros_like(db_ref)
            cs_ref[...] = jnp.zeros_like(cs_ref)

        y, g = y_ref[...], g_ref[...]
        yc = y - jnp.mean(y, axis=-1, keepdims=True)
        rs = lax.rsqrt(jnp.mean(yc * yc, axis=-1, keepdims=True) + LN_EPS)
        yn = yc * rs
        s = yn * g + b_ref[...]
        sg = jax.nn.sigmoid(s)
        ds = da_ref[...] * (sg * (1.0 + s * (1.0 - sg)))
        dg_ref[...] += jnp.sum(ds * yn, axis=0, keepdims=True)
        db_ref[...] += jnp.sum(ds, axis=0, keepdims=True)
        dyn = ds * g
        dy = rs * (dyn - jnp.mean(dyn, axis=-1, keepdims=True) - yn * jnp.mean(dyn * yn, axis=-1, keepdims=True))
        dy_ref[...] = dy
        cs_ref[...] += jnp.sum(dy, axis=0, keepdims=True)

    blk = pl.BlockSpec((tb, D), lambda i: (i, 0))
    row = pl.BlockSpec((1, D), lambda i: (0, 0))
    rs_ = jax.ShapeDtypeStruct((1, D), F32)
    return pl.pallas_call(
        body, grid=(T // tb,), in_specs=[blk, blk, row, row], out_specs=[blk, row, row, row],
        out_shape=[jax.ShapeDtypeStruct((T, D), F32), rs_, rs_, rs_], compiler_params=_params("arbitrary"), name=name,
    )(dact, y, ln_g, ln_b)


def _conv_bwd_taps(dy, a, p, w_dw, *, name):
    T, D = dy.shape
    tb = _token_block(T)
    hpb = tb // CONV_HALO
    last = T // CONV_HALO - 1
    nb = T // tb

    def body(dy_ref, dyn_ref, a_ref, ap_ref, p1_ref, p2_ref, w_ref, dp_ref, dw_ref, cs_ref, dbuf, abuf):
        i = pl.program_id(0)

        @pl.when(i == 0)
        def _():
            dw_ref[...] = jnp.zeros_like(dw_ref)
            cs_ref[...] = jnp.zeros_like(cs_ref)

        dy = dy_ref[...]
        dbuf[0:tb, :] = dy
        dbuf[tb:, :] = jnp.where(i < nb - 1, dyn_ref[...], 0.0)
        abuf[0:CONV_HALO, :] = jnp.where(i > 0, ap_ref[...], 0.0)
        abuf[CONV_HALO:, :] = a_ref[...]
        da = jnp.zeros((tb, D), F32)
        for k in range(CONV_WIDTH):
            da = da + dbuf[pl.ds(CONV_WIDTH - 1 - k, tb), :] * w_ref[k:k + 1, :]
            dw_ref[k:k + 1, :] += jnp.sum(dy * abuf[pl.ds(CONV_HALO - CONV_WIDTH + 1 + k, tb), :], axis=0, keepdims=True)
        p1 = p1_ref[...]
        sg = jax.nn.sigmoid(p2_ref[...])
        dp1 = da * sg
        dp2 = da * p1 * (sg * (1.0 - sg))
        dp_ref[:, 0:D] = dp1
        dp_ref[:, D:] = dp2
        cs_ref[:, 0:D] += jnp.sum(dp1, axis=0, keepdims=True)
        cs_ref[:, D:] += jnp.sum(dp2, axis=0, keepdims=True)

    blk = pl.BlockSpec((tb, D), lambda i: (i, 0))
    return pl.pallas_call(
        body, grid=(nb,),
        in_specs=[blk, pl.BlockSpec((CONV_HALO, D), lambda i: (jnp.minimum((i + 1) * hpb, last), 0)), blk,
                  pl.BlockSpec((CONV_HALO, D), lambda i: (jnp.maximum(i * hpb - 1, 0), 0)), blk,
                  pl.BlockSpec((tb, D), lambda i: (i, 1)), pl.BlockSpec((CONV_HALO, D), lambda i: (0, 0))],
        out_specs=[pl.BlockSpec((tb, 2 * D), lambda i: (i, 0)), pl.BlockSpec((CONV_HALO, D), lambda i: (0, 0)), pl.BlockSpec((1, 2 * D), lambda i: (0, 0))],
        out_shape=[jax.ShapeDtypeStruct((T, 2 * D), F32), jax.ShapeDtypeStruct((CONV_HALO, D), F32), jax.ShapeDtypeStruct((1, 2 * D), F32)],
        scratch_shapes=[pltpu.VMEM((tb + CONV_HALO, D), F32), pltpu.VMEM((tb + CONV_HALO, D), F32)],
        compiler_params=_params("arbitrary"), name=name,
    )(dy, dy, a, a, p, p, w_dw)


def _row_block(rows):
    for tr in (512, 256, 128, 64, 32, 16, 8):
        if rows % tr == 0:
            return tr
    return rows


def _sum_leading(x, *, name):
    n, R, C = x.shape
    tr = _row_block(R)

    def body(x_ref, o_ref):
        acc = x_ref[0]
        for j in range(1, n):
            acc = acc + x_ref[j]
        o_ref[...] = acc

    return pl.pallas_call(
        body, grid=(R // tr,), in_specs=[pl.BlockSpec((n, tr, C), lambda i: (0, i, 0))], out_specs=pl.BlockSpec((tr, C), lambda i: (i, 0)),
        out_shape=jax.ShapeDtypeStruct((R, C), x.dtype), compiler_params=_params("parallel"), name=name,
    )(x)


def _add_pair(x, y, *, name):
    n, R, C = x.shape
    tr = _row_block(R)

    def body(x_ref, y_ref, o_ref):
        o_ref[...] = x_ref[...] + y_ref[...]

    blk = pl.BlockSpec((1, tr, C), lambda j, i: (j, i, 0))
    return pl.pallas_call(
        body, grid=(n, R // tr), in_specs=[blk, blk], out_specs=blk,
        out_shape=jax.ShapeDtypeStruct((n, R, C), x.dtype), compiler_params=_params("parallel", "parallel"), name=name,
    )(x, y)


def _adamw(w, g, m, v, *, name):
    R, C = w.shape
    tr = _row_block(R)
    c1, c2 = 1.0 - ADAM_B1 ** ADAM_STEP, 1.0 - ADAM_B2 ** ADAM_STEP

    def body(w_ref, g_ref, m_ref, v_ref, d_ref, nm_ref, nv_ref):
        g_ = g_ref[...]
        nm = ADAM_B1 * m_ref[...] + (1.0 - ADAM_B1) * g_
        nv = ADAM_B2 * v_ref[...] + (1.0 - ADAM_B2) * (g_ * g_)
        d_ref[...] = -ADAM_LR * ((nm / c1) / (jnp.sqrt(nv / c2) + ADAM_EPS) + ADAM_WD * w_ref[...])
        nm_ref[...] = nm
        nv_ref[...] = nv

    blk = pl.BlockSpec((tr, C), lambda i: (i, 0))
    shp = jax.ShapeDtypeStruct((R, C), F32)
    return pl.pallas_call(
        body, grid=(R // tr,), in_specs=[blk] * 4, out_specs=[blk] * 3, out_shape=[shp] * 3,
        compiler_params=_params("parallel"), name=name,
    )(w, g, m, v)


def _small_reduce(packs, logits, *, name):
    n, R, C = packs.shape

    def body(p_ref, lg_ref, s_ref, dlg_ref):
        acc = p_ref[0]
        for j in range(1, n):
            acc = acc + p_ref[j]
        s_ref[...] = acc
        lg = lg_ref[...]
        e = jnp.exp(lg - jnp.max(lg, axis=0, keepdims=True))
        sm = e / jnp.sum(e, axis=0, keepdims=True)
        dlb = acc[5:6, HG_WIDTH:2 * HG_WIDTH]
        first = lax.broadcasted_iota(jnp.int32, sm.shape, 0) == 0
        dlg_ref[...] = sm[0:1, :] * (jnp.where(first, 1.0, 0.0) - sm) * dlb

    whole = lambda shape: pl.BlockSpec(shape, lambda: (0,) * len(shape))
    return pl.pallas_call(
        body, in_specs=[whole((n, R, C)), whole(logits.shape)], out_specs=[whole((R, C)), whole(logits.shape)],
        out_shape=[jax.ShapeDtypeStruct((R, C), F32), jax.ShapeDtypeStruct(logits.shape, F32)],
        compiler_params=pltpu.CompilerParams(vmem_limit_bytes=VMEM_LIMIT), name=name,
    )(packs, logits)


HBM_SPEC = pl.BlockSpec(memory_space=pl.ANY)


def _place():
    return lax.axis_index("x"), lax.axis_index("y"), lax.axis_index("c")


def _exchange(arrays, peers_of, slot_of, n_slots, *, name):
    n = len(arrays)
    n_peers = len(peers_of(0, 0, 0))

    def body(*refs):
        ins, outs = refs[:n], refs[n:2 * n]
        send_sems, recv_sems, local_sems = refs[2 * n:]
        x, y, c = _place()
        slot = slot_of(x, y, c)
        peers = peers_of(x, y, c)
        copies = []
        for a in range(n):
            mine = pltpu.make_async_copy(ins[a], outs[a].at[slot], local_sems.at[a])
            mine.start()
            copies.append(mine)
            for k, peer in enumerate(peers):
                cp = pltpu.make_async_remote_copy(
                    src_ref=ins[a], dst_ref=outs[a].at[slot], send_sem=send_sems.at[a * n_peers + k],
                    recv_sem=recv_sems.at[a * n_peers + k], device_id=peer, device_id_type=MESH)
                cp.start()
                copies.append(cp)
        for cp in copies:
            cp.wait()

    return pl.pallas_call(
        body, in_specs=[HBM_SPEC] * n, out_specs=[HBM_SPEC] * n,
        out_shape=[jax.ShapeDtypeStruct((n_slots,) + a.shape, a.dtype) for a in arrays],
        scratch_shapes=[pltpu.SemaphoreType.DMA((n * n_peers,)), pltpu.SemaphoreType.DMA((n * n_peers,)), pltpu.SemaphoreType.DMA((n,))],
        name=name,
    )(*arrays)


def _same_core_peers(x, y, c):
    return [(1 - x, y, c), (x, 1 - y, c), (1 - x, 1 - y, c)]


def _all_peers(x, y, c):
    flip = lambda v, b: 1 - v if b else v
    return [(flip(x, r & 4), flip(y, r & 2), flip(c, r & 1)) for r in range(1, 8)]


def _gather_chips(arrays, *, name):
    return _exchange(arrays, _same_core_peers, lambda x, y, c: 2 * x + y, N_CHIPS, name=name)


def _gather_all(arrays, *, name):
    return _exchange(arrays, _all_peers, lambda x, y, c: 4 * x + 2 * y + c, N_DEV, name=name)


def _pair_swap(arrays, *, name):
    n = len(arrays)

    def body(*refs):
        ins, outs = refs[:n], refs[n:2 * n]
        send_sems, recv_sems = refs[2 * n:]
        x, y, c = _place()
        copies = []
        for a in range(n):
            cp = pltpu.make_async_remote_copy(
                src_ref=ins[a].at[1 - c], dst_ref=outs[a], send_sem=send_sems.at[a], recv_sem=recv_sems.at[a],
                device_id=(x, y, 1 - c), device_id_type=MESH)
            cp.start()
            copies.append(cp)
        for cp in copies:
            cp.wait()

    return pl.pallas_call(
        body, in_specs=[HBM_SPEC] * n, out_specs=[HBM_SPEC] * n,
        out_shape=[jax.ShapeDtypeStruct(a.shape[1:], a.dtype) for a in arrays],
        scratch_shapes=[pltpu.SemaphoreType.DMA((n,)), pltpu.SemaphoreType.DMA((n,))], name=name,
    )(*arrays)


def _chip_scatter(p, *, name):
    def body(p_ref, o_ref, send_sems, recv_sems, local_sem):
        x, y, c = _place()
        me = 2 * x + y
        mine = pltpu.make_async_copy(p_ref.at[me], o_ref.at[me], local_sem)
        mine.start()
        copies = [mine]
        for k, (px, py, pc) in enumerate(_same_core_peers(x, y, c)):
            cp = pltpu.make_async_remote_copy(
                src_ref=p_ref.at[2 * px + py], dst_ref=o_ref.at[me], send_sem=send_sems.at[k], recv_sem=recv_sems.at[k],
                device_id=(px, py, pc), device_id_type=MESH)
            cp.start()
            copies.append(cp)
        for cp in copies:
            cp.wait()

    return pl.pallas_call(
        body, in_specs=[HBM_SPEC], out_specs=HBM_SPEC, out_shape=jax.ShapeDtypeStruct(p.shape, p.dtype),
        scratch_shapes=[pltpu.SemaphoreType.DMA((3,)), pltpu.SemaphoreType.DMA((3,)), pltpu.SemaphoreType.DMA], name=name,
    )(p)


def _pair_gather(q, *, name):
    def body(q_ref, o_ref, send_sem, recv_sem, local_sem):
        x, y, c = _place()
        mine = pltpu.make_async_copy(q_ref, o_ref.at[c], local_sem)
        mine.start()
        cp = pltpu.make_async_remote_copy(src_ref=q_ref, dst_ref=o_ref.at[c], send_sem=send_sem, recv_sem=recv_sem,
                                          device_id=(x, y, 1 - c), device_id_type=MESH)
        cp.start()
        cp.wait()
        mine.wait()

    return pl.pallas_call(
        body, in_specs=[HBM_SPEC], out_specs=HBM_SPEC, out_shape=jax.ShapeDtypeStruct((2,) + q.shape, q.dtype),
        scratch_shapes=[pltpu.SemaphoreType.DMA, pltpu.SemaphoreType.DMA, pltpu.SemaphoreType.DMA], name=name,
    )(q)


def _cols_from_chips(g):
    g = jnp.moveaxis(g, 0, -2)
    return g.reshape(g.shape[:-2] + (g.shape[-2] * g.shape[-1],))


def _rows_from_chips(g):
    g = jnp.moveaxis(g, 0, -3)
    return g.reshape(g.shape[:-3] + (g.shape[-3] * g.shape[-2], g.shape[-1]))


def _grad_blocks(dw, kind):
    if kind == "cols2d":
        K, N = dw.shape
        b = dw.reshape(2, K // 2, N_CHIPS, N // N_CHIPS).transpose(2, 0, 1, 3)
    elif kind == "rows2d":
        b = dw.reshape(N_CHIPS, 2, dw.shape[0] // 8, dw.shape[1])
    elif kind == "cols3d":
        L, K, N = dw.shape
        b = dw.reshape(L, K, N_CHIPS, N // N_CHIPS).transpose(2, 0, 1, 3)
    else:
        L, K, N = dw.shape
        b = dw.reshape(L, N_CHIPS, K // N_CHIPS, N).transpose(1, 0, 2, 3)
    return b.reshape(N_CHIPS, 2, -1, D_MODEL)


def _pad_rows(a, rows):
    return jnp.concatenate([a, jnp.zeros((rows - a.shape[0],) + a.shape[1:], a.dtype)], axis=0)


def _forward_backward(x, target, W):
    row = lambda a: a.reshape(1, -1)
    relu2 = lambda acc: (acc, jnp.square(jnp.maximum(acc, 0.0)))
    residual = lambda acc, res: (res + acc,)
    G = {}

    u0 = _rmsnorm_fwd(x, row(W["norm_mix_g"][0]), name="norm_mix0")
    proj = _matmul(u0, W["w_in"], mode="nn", out_dtypes=[F32], tn=896, name="in_proj")
    qkv = proj[:, :3 * SB_WIDTH].astype(MXU_DTYPE)
    o_sb = _sb_fwd(qkv, name="sb_fwd")
    hg_out, hg_o, hg_states = _hg_fwd(proj, W["hg_lb_logits"], row(W["hg_norm_g"]), name="hg_fwd")
    mix = jnp.concatenate([o_sb.astype(MXU_DTYPE), hg_out], axis=-1)
    h1 = _matmul(mix, W["w_out"], mode="nn", out_dtypes=[F32], epilogue=residual, tiles=[x], name="out_proj")
    u1 = _rmsnorm_fwd(h1, row(W["norm_ffn_g"][0]), name="norm_ffn0")
    a0, r0 = _matmul(u1, W["w_ff1"][0], mode="nn", out_dtypes=[F32, MXU_DTYPE], epilogue=relu2, name="ff1_0")
    h2 = _matmul(r0, W["w_ff2"][0], mode="nn", out_dtypes=[F32], epilogue=residual, tiles=[h1], name="ff2_0")
    u2 = _rmsnorm_fwd(h2, row(W["norm_mix_g"][1]), name="norm_mix1")
    p = _matmul(u2, W["w_glu"], mode="nn", out_dtypes=[F32], epilogue=lambda acc, b: (acc + b,), rows=[row(W["b_glu"])], name="glu_proj")
    w_dw = _pad_rows(W["w_dw"], CONV_HALO)
    ca, cy, cact = _conv_fwd(p, w_dw, row(W["b_dw"]), row(W["ln_g"]), row(W["ln_b"]), name="conv_fwd")
    h3 = _matmul(cact, W["w_pw"], mode="nn", out_dtypes=[F32], epilogue=lambda acc, res, b: (res + acc + b,),
                 tiles=[h2], rows=[row(W["b_pw"])], name="pw_proj")
    u3 = _rmsnorm_fwd(h3, row(W["norm_ffn_g"][1]), name="norm_ffn1")
    a1, r1 = _matmul(u3, W["w_ff1"][1], mode="nn", out_dtypes=[F32, MXU_DTYPE], epilogue=relu2, name="ff1_1")
    h4 = _matmul(r1, W["w_ff2"][1], mode="nn", out_dtypes=[F32], epilogue=residual, tiles=[h3], name="ff2_1")

    dh4, G["final_norm_g"], loss = _loss_head(h4, row(W["final_norm_g"]), target, name="loss_head")

    def mlp_bwd(dh, h_in, u, a, r, layer, tag):
        d_relu2 = lambda acc, a_blk: (acc * (2.0 * jnp.maximum(a_blk, 0.0)),)
        da = _matmul(dh, W["w_ff2"][layer], mode="nt", out_dtypes=[MXU_DTYPE], epilogue=d_relu2, tiles=[a], name="d_ff2_act" + tag)
        dw2 = _matmul(r, dh, mode="tn", out_dtypes=[F32], name="d_ff2_w" + tag)
        dw1 = _matmul(u, da, mode="tn", out_dtypes=[F32], name="d_ff1_w" + tag)
        du = _matmul(da, W["w_ff1"][layer], mode="nt", out_dtypes=[F32], name="d_ff1_act" + tag)
        dh_in, dg, cs = _rmsnorm_bwd(du, h_in, row(W["norm_ffn_g"][layer]), dh, name="d_norm_ffn" + tag)
        return dh_in, dg, cs, dw1, dw2

    dh3, dg_ffn1, cs_h3, dw1_1, dw2_1 = mlp_bwd(dh4, h3, u3, a1, r1, 1, "1")
    G["b_pw"] = cs_h3
    dact = _matmul(dh3, W["w_pw"], mode="nt", out_dtypes=[F32], name="d_pw_act")
    G["w_pw"] = _matmul(cact, dh3, mode="tn", out_dtypes=[F32], name="d_pw_w")
    dy, G["ln_g"], G["ln_b"], G["b_dw"] = _conv_bwd_norm(dact, cy, row(W["ln_g"]), row(W["ln_b"]), name="d_conv_norm")
    dp, G["w_dw"], G["b_glu"] = _conv_bwd_taps(dy, ca, p, w_dw, name="d_conv_taps")
    G["w_glu"] = _matmul(u2, dp, mode="tn", out_dtypes=[F32], name="d_glu_w")
    du2 = _matmul(dp, W["w_glu"], mode="nt", out_dtypes=[F32], name="d_glu_act")
    dh2, dg_mix1, _ = _rmsnorm_bwd(du2, h2, row(W["norm_mix_g"][1]), dh3, name="d_norm_mix1")
    dh1, dg_ffn0, _, dw1_0, dw2_0 = mlp_bwd(dh2, h1, u1, a0, r0, 0, "0")
    G["w_ff1"], G["w_ff2"] = jnp.stack([dw1_0, dw1_1]), jnp.stack([dw2_0, dw2_1])
    G["norm_ffn_g"] = jnp.concatenate([dg_ffn0, dg_ffn1], axis=0)
    dmix = _matmul(dh1, W["w_out"], mode="nt", out_dtypes=[F32], name="d_out_act")
    G["w_out"] = _matmul(mix, dh1, mode="tn", out_dtypes=[F32], name="d_out_w")
    dsq, dsk, dsv = _sb_bwd(qkv, dmix, name="sb_bwd")
    d_hg, G["hg_lb"], G["hg_norm_g"] = _hg_bwd(proj, hg_o, hg_states, dmix, W["hg_lb_logits"], row(W["hg_norm_g"]), name="hg_bwd")
    dproj = jnp.concatenate([dsq, dsk, dsv, d_hg], axis=-1).astype(MXU_DTYPE)
    G["w_in"] = _matmul(u0, dproj, mode="tn", out_dtypes=[F32], tn=896, name="d_in_w")
    du0 = _matmul(dproj, W["w_in"], mode="nt", out_dtypes=[F32], tk=896, name="d_in_act")
    dx, dg_mix0, _ = _rmsnorm_bwd(du0, x, row(W["norm_mix_g"][0]), dh1, name="d_norm_mix0")
    G["norm_mix_g"] = jnp.concatenate([dg_mix0, dg_mix1], axis=0)
    return loss, dx, G


BIG = (("w_in_ab", "w_in", "cols2d"), ("w_out_ab", "w_out", "rows2d"), ("conv_w_glu", "w_glu", "cols2d"),
       ("conv_w_pw", "w_pw", "rows2d"), ("w_ff1", "w_ff1", "cols3d"), ("w_ff2", "w_ff2", "rows3d"))
SMALL_SHARDED = ("conv_b_glu", "conv_w_dw", "conv_b_dw", "conv_ln_g", "conv_ln_b", "conv_b_pw")
REPLICATED = ("norm_mix_g", "norm_ffn_g", "hg_lb_logits", "hg_norm_g", "final_norm_g")
ORDER = ("norm_mix_g", "norm_ffn_g", "w_in_ab", "w_out_ab", "hg_lb_logits", "hg_norm_g", "conv_w_glu", "conv_b_glu",
         "conv_w_dw", "conv_b_dw", "conv_ln_g", "conv_ln_b", "conv_w_pw", "conv_b_pw", "w_ff1", "w_ff2", "final_norm_g")


def _step(x, loss_target, w, m, v):
    D = D_MODEL
    x2, t2 = x.reshape(-1, D), loss_target.reshape(-1, D)
    chip = 2 * lax.axis_index("x") + lax.axis_index("y")

    small_in = jnp.concatenate([w["conv_b_glu"].reshape(2, 256), w["conv_w_dw"].reshape(CONV_WIDTH, 256)] +
                               [w[n].reshape(1, 256) for n in ("conv_b_dw", "conv_ln_g", "conv_ln_b", "conv_b_pw")], axis=0)
    gathered = _gather_chips([w[n].astype(MXU_DTYPE) for n, _, _ in BIG] + [_pad_rows(small_in, 40)], name="gather_weights")
    gw = dict(zip([s for _, s, _ in BIG], gathered[:-1]))
    gs = gathered[-1]
    vec = lambda r0, r1: gs[:, r0:r1].transpose(1, 0, 2).reshape(r1 - r0, N_CHIPS * 256)
    W = {
        "w_in": _cols_from_chips(gw["w_in"][:, 0]), "w_out": _rows_from_chips(gw["w_out"][:, 0]),
        "w_glu": _cols_from_chips(gw["w_glu"][:, 0]), "w_pw": _rows_from_chips(gw["w_pw"][:, 0]),
        "w_ff1": _cols_from_chips(gw["w_ff1"]), "w_ff2": _rows_from_chips(gw["w_ff2"]),
        "b_glu": gs[:, 0:2].reshape(2 * D), "w_dw": vec(2, 33), "b_dw": vec(33, 34)[0], "ln_g": vec(34, 35)[0],
        "ln_b": vec(35, 36)[0], "b_pw": vec(36, 37)[0],
        "norm_mix_g": w["norm_mix_g"], "norm_ffn_g": w["norm_ffn_g"], "hg_lb_logits": w["hg_lb_logits"],
        "hg_norm_g": w["hg_norm_g"], "final_norm_g": w["final_norm_g"],
    }

    loss, dx, G = _forward_backward(x2, t2, W)

    pack = jnp.concatenate([
        G["norm_mix_g"], G["norm_ffn_g"], G["final_norm_g"], jnp.concatenate([G["hg_norm_g"], G["hg_lb"]], axis=1),
        _pad_rows(jnp.broadcast_to(loss, (1, D)), 2), G["b_glu"].reshape(2, D), G["w_dw"], G["b_dw"], G["ln_g"], G["ln_b"], G["b_pw"],
    ], axis=0)
    pack = _pad_rows(pack, SMALL_ROWS)
    (packs,) = _gather_all([pack], name="gather_small_grads")
    ssum, d_logits = _small_reduce(packs, w["hg_lb_logits"], name="reduce_small_grads")
    cut = lambda r0, r1: lax.dynamic_slice(ssum, (r0, chip * 256), (r1 - r0, 256))
    grads = {
        "norm_mix_g": ssum[0:2], "norm_ffn_g": ssum[2:4], "final_norm_g": ssum[4], "hg_norm_g": ssum[5, :HG_WIDTH].reshape(1, HG_HEADS, HG_DH),
        "hg_lb_logits": d_logits,
        "conv_b_glu": lax.dynamic_slice(ssum[8:10].reshape(1, 2 * D), (0, chip * 512), (1, 512)),
        "conv_w_dw": cut(10, 10 + CONV_WIDTH).reshape(1, CONV_WIDTH, 256),
        "conv_b_dw": cut(42, 43), "conv_ln_g": cut(43, 44), "conv_ln_b": cut(44, 45), "conv_b_pw": cut(45, 46),
    }
    loss_out = ssum[6, 0]

    blocks = jnp.concatenate([_grad_blocks(G[s], kind) for _, s, kind in BIG], axis=2)
    c = lax.axis_index("c")
    (from_pair,) = _pair_swap([blocks.transpose(1, 0, 2, 3)], name="grads_pair_swap")
    mine = lax.dynamic_index_in_dim(blocks, c, axis=1, keepdims=False)
    chip_sum = _add_pair(mine, from_pair, name="grads_pair_add")
    per_chip = _chip_scatter(chip_sum, name="grads_chip_scatter")
    half = _sum_leading(per_chip, name="grads_chip_add")
    full = _pair_gather(half, name="grads_pair_gather")
    off = 0
    for n, s, kind in BIG:
        shard = w[n].shape
        rows = w[n].size // (2 * D)
        grads[n] = full[:, off:off + rows].reshape(shard)
        off += rows

    delta, new_m, new_v = {}, {}, {}
    for n, _, _ in BIG:
        view = lambda a: a.reshape(-1, a.shape[-1])
        outs = _adamw(view(w[n]), view(grads[n]), view(m[n]), view(v[n]), name="adamw_" + n)
        delta[n], new_m[n], new_v[n] = (o.reshape(w[n].shape) for o in outs)
    small = SMALL_SHARDED + REPLICATED
    sizes = [w[n].size for n in small]
    total = sum(sizes)
    rows = -(-total // (8 * D)) * 8
    packed = lambda d: _pad_rows(jnp.concatenate([d[n].reshape(-1) for n in small]).reshape(-1, 128), rows * 8).reshape(rows, D)
    outs = _adamw(packed(w), packed(grads), packed(m), packed(v), name="adamw_small")
    off = 0
    for n, size in zip(small, sizes):
        delta[n], new_m[n], new_v[n] = (o.reshape(-1)[off:off + size].reshape(w[n].shape) for o in outs)
        off += size
    grads = {n: grads[n].reshape(w[n].shape) for n in ORDER}
    return (loss_out, dx.reshape(x.shape), *[grads[n] for n in ORDER], *[delta[n] for n in ORDER],
            *[new_m[n] for n in ORDER], *[new_v[n] for n in ORDER])


def kernel(x, norm_mix_g, norm_ffn_g, w_in_ab, w_out_ab, hg_lb_logits, hg_norm_g, conv_w_glu, conv_b_glu, conv_w_dw, conv_b_dw, conv_ln_g, conv_ln_b, conv_w_pw, conv_b_pw, w_ff1, w_ff2, final_norm_g, loss_target, m_norm_mix_g, m_norm_ffn_g, m_w_in_ab, m_w_out_ab, m_hg_lb_logits, m_hg_norm_g, m_conv_w_glu, m_conv_b_glu, m_conv_w_dw, m_conv_b_dw, m_conv_ln_g, m_conv_ln_b, m_conv_w_pw, m_conv_b_pw, m_w_ff1, m_w_ff2, m_final_norm_g, v_norm_mix_g, v_norm_ffn_g, v_w_in_ab, v_w_out_ab, v_hg_lb_logits, v_hg_norm_g, v_conv_w_glu, v_conv_b_glu, v_conv_w_dw, v_conv_b_dw, v_conv_ln_g, v_conv_ln_b, v_conv_w_pw, v_conv_b_pw, v_w_ff1, v_w_ff2, v_final_norm_g):
    args = locals()
    w = {n: args[n] for n in ORDER}
    m = {n: args["m_" + n] for n in ORDER}
    v = {n: args["v_" + n] for n in ORDER}
    return _step(x, loss_target, w, m, v)
```

```python
import functools

import jax
import jax.numpy as jnp
from jax import lax
from jax.experimental import pallas as pl
from jax.experimental.pallas import tpu as pltpu

F32 = jnp.float32
MXU_DTYPE = jnp.bfloat16
MESH = pl.DeviceIdType.MESH

D_MODEL = 1024
SB_HEADS, SB_DH, SB_WIDTH = 8, 64, 512
SB_KEYS = 512
SB_SUB = 256
SB_ROWS_FWD, SB_ROWS_BWD = 512, 256
HG_HEADS, HG_DH, HG_WIDTH = 4, 128, 512
HG_CHUNK = 16
HG_TOKENS = 256
IN_WIDTH = 3 * SB_WIDTH + 4 * HG_WIDTH
CONV_WIDTH = 31
CONV_HALO = 32
CONV_ROWS = 32
D_FF = 4096
RMS_EPS = 1e-6
LN_EPS = 1e-5
N_CHIPS = 4
N_DEV = 8
SMALL_ROWS = 48
VMEM_LIMIT = 56 * 1024 * 1024

ADAM_LR, ADAM_B1, ADAM_B2, ADAM_EPS, ADAM_WD, ADAM_STEP = 0.001, 0.9, 0.999, 1e-08, 0.01, 10


def _params(*sem):
    return pltpu.CompilerParams(dimension_semantics=sem, vmem_limit_bytes=VMEM_LIMIT)


def _mx(v):
    return v.astype(MXU_DTYPE)


def _dot(a, b):
    return jnp.dot(_mx(a), _mx(b), preferred_element_type=F32)


def _dot_nt(a, b):
    return lax.dot_general(_mx(a), _mx(b), (((1,), (1,)), ((), ())), preferred_element_type=F32)


def _dot_tn(a, b):
    return lax.dot_general(_mx(a), _mx(b), (((0,), (0,)), ((), ())), preferred_element_type=F32)


def _neg_abs(x):
    bits = lax.bitcast_convert_type(x, jnp.uint32) | jnp.uint32(0x80000000)
    return lax.bitcast_convert_type(bits, F32)


def _key_order_sums(v, tri2, later):
    hi = _mx(v)
    lo = _mx(v - hi.astype(F32))
    n = SB_KEYS // SB_SUB
    blocks = [slice(b * SB_SUB, (b + 1) * SB_SUB) for b in range(n)]
    totals = [jnp.sum(v[:, sl], axis=1, keepdims=True) for sl in blocks]
    sums = []
    for b, sl in enumerate(blocks):
        s = jnp.dot(jnp.concatenate([hi[:, sl], lo[:, sl]], axis=1), tri2, preferred_element_type=F32)
        for o in (range(b + 1, n) if later else range(b)):
            s = s + totals[o]
        sums.append(s)
    return jnp.concatenate(sums, axis=1), functools.reduce(lambda a, b: a + b, totals)


def _matmul(a, b, *, mode, out_dtypes, epilogue=None, tiles=(), rows=(), tm=512, tn=1024, tk=1024, name):
    if mode == "nn":
        (M, K), N = a.shape, b.shape[1]
    elif mode == "nt":
        (M, K), N = a.shape, b.shape[0]
    else:
        (K, M), N = a.shape, b.shape[1]
    tm, tn, tk = min(tm, M), min(tn, N), min(tk, K)
    assert M % tm == 0 and N % tn == 0 and K % tk == 0, (name, M, N, K)
    nk = K // tk
    a_spec = pl.BlockSpec((tk, tm), lambda i, j, k: (k, i)) if mode == "tn" else pl.BlockSpec((tm, tk), lambda i, j, k: (i, k))
    b_spec = pl.BlockSpec((tn, tk), lambda i, j, k: (j, k)) if mode == "nt" else pl.BlockSpec((tk, tn), lambda i, j, k: (k, j))
    dims = {"nn": ((1,), (0,)), "nt": ((1,), (1,)), "tn": ((0,), (0,))}[mode]
    n_t, n_r, n_o = len(tiles), len(rows), len(out_dtypes)
    if epilogue is None:
        epilogue = lambda acc: (acc,)

    def body(a_ref, b_ref, *rest):
        extra, outs, acc_ref = rest[:n_t + n_r], rest[n_t + n_r:n_t + n_r + n_o], rest[-1]
        k = pl.program_id(2)

        @pl.when(k == 0)
        def _():
            acc_ref[...] = jnp.zeros_like(acc_ref)

        acc_ref[...] += lax.dot_general(_mx(a_ref[...]), _mx(b_ref[...]), (dims, ((), ())), preferred_element_type=F32)

        @pl.when(k == nk - 1)
        def _():
            res = epilogue(acc_ref[...], *[e[...] for e in extra])
            for o_ref, r in zip(outs, res):
                o_ref[...] = r.astype(o_ref.dtype)

    tile_spec = pl.BlockSpec((tm, tn), lambda i, j, k: (i, j))
    row_spec = pl.BlockSpec((1, tn), lambda i, j, k: (0, j))
    outs = pl.pallas_call(
        body, grid=(M // tm, N // tn, nk),
        in_specs=[a_spec, b_spec] + [tile_spec] * n_t + [row_spec] * n_r,
        out_specs=[tile_spec] * n_o,
        out_shape=[jax.ShapeDtypeStruct((M, N), dt) for dt in out_dtypes],
        scratch_shapes=[pltpu.VMEM((tm, tn), F32)],
        compiler_params=_params("parallel", "parallel", "arbitrary"), name=name,
    )(a, b, *tiles, *rows)
    return outs[0] if n_o == 1 else outs


def _token_block(T):
    return min(512, T)


def _rmsnorm_fwd(h, g, *, name):
    T, D = h.shape
    tb = _token_block(T)

    def body(h_ref, g_ref, u_ref):
        x = h_ref[...]
        r = lax.rsqrt(jnp.mean(x * x, axis=-1, keepdims=True) + RMS_EPS)
        u_ref[...] = (x * r * g_ref[...]).astype(u_ref.dtype)

    blk = pl.BlockSpec((tb, D), lambda i: (i, 0))
    return pl.pallas_call(
        body, grid=(T // tb,), in_specs=[blk, pl.BlockSpec((1, D), lambda i: (0, 0))], out_specs=blk,
        out_shape=jax.ShapeDtypeStruct((T, D), MXU_DTYPE), compiler_params=_params("parallel"), name=name,
    )(h, g)


def _rms_bwd_math(x, g, du):
    r = lax.rsqrt(jnp.mean(x * x, axis=-1, keepdims=True) + RMS_EPS)
    gd = g * du
    dx = r * gd - x * (r * r * r) * jnp.mean(gd * x, axis=-1, keepdims=True)
    return dx, du * x * r


def _rmsnorm_bwd(du, h, g, dres, *, name):
    T, D = h.shape
    tb = _token_block(T)

    def body(du_ref, h_ref, g_ref, dres_ref, dh_ref, dg_ref, cs_ref):
        @pl.when(pl.program_id(0) == 0)
        def _():
            dg_ref[...] = jnp.zeros_like(dg_ref)
            cs_ref[...] = jnp.zeros_like(cs_ref)

        dx, dg_terms = _rms_bwd_math(h_ref[...], g_ref[...], du_ref[...])
        dh = dres_ref[...] + dx
        dh_ref[...] = dh
        dg_ref[...] += jnp.sum(dg_terms, axis=0, keepdims=True)
        cs_ref[...] += jnp.sum(dh, axis=0, keepdims=True)

    blk = pl.BlockSpec((tb, D), lambda i: (i, 0))
    row = pl.BlockSpec((1, D), lambda i: (0, 0))
    return pl.pallas_call(
        body, grid=(T // tb,), in_specs=[blk, blk, row, blk], out_specs=[blk, row, row],
        out_shape=[jax.ShapeDtypeStruct((T, D), F32), jax.ShapeDtypeStruct((1, D), F32), jax.ShapeDtypeStruct((1, D), F32)],
        compiler_params=_params("arbitrary"), name=name,
    )(du, h, g, dres)


def _loss_head(h, g, target, *, name):
    T, D = h.shape
    tb = _token_block(T)

    def body(h_ref, g_ref, t_ref, dh_ref, dg_ref, loss_ref):
        @pl.when(pl.program_id(0) == 0)
        def _():
            dg_ref[...] = jnp.zeros_like(dg_ref)
            loss_ref[...] = jnp.zeros_like(loss_ref)

        x, gg = h_ref[...], g_ref[...]
        r = lax.rsqrt(jnp.mean(x * x, axis=-1, keepdims=True) + RMS_EPS)
        diff = x * r * gg - t_ref[...]
        per_token = jnp.mean(diff * diff, axis=-1, keepdims=True)
        loss_ref[...] += 0.5 * jnp.sum(per_token, axis=0, keepdims=True)
        dx, dg_terms = _rms_bwd_math(x, gg, diff / D)
        dh_ref[...] = dx
        dg_ref[...] += jnp.sum(dg_terms, axis=0, keepdims=True)

    blk = pl.BlockSpec((tb, D), lambda i: (i, 0))
    row = pl.BlockSpec((1, D), lambda i: (0, 0))
    return pl.pallas_call(
        body, grid=(T // tb,), in_specs=[blk, row, blk], out_specs=[blk, row, pl.BlockSpec((1, 1), lambda i: (0, 0))],
        out_shape=[jax.ShapeDtypeStruct((T, D), F32), jax.ShapeDtypeStruct((1, D), F32), jax.ShapeDtypeStruct((1, 1), F32)],
        compiler_params=_params("arbitrary"), name=name,
    )(h, g, target)


def _sb_scores(qm, ks, later, tri, mask):
    z = _dot_nt(qm, ks)
    sp = jnp.maximum(z, 0.0) + jnp.log(1.0 + jnp.exp(_neg_abs(z)))
    lb = z - sp
    if mask is not None:
        sp = jnp.where(mask, sp, 0.0)
    after, total = _key_order_sums(sp, tri, later=True)
    w = jnp.exp(lb - (after + later))
    if mask is not None:
        w = jnp.where(mask, w, 0.0)
    return total, lb, w


def _sb_setup(q_ref, rows):
    i, hsel = pl.program_id(1), pl.program_id(2)
    lane = lax.broadcasted_iota(jnp.int32, (rows, 2 * SB_DH), 1)
    mine = (lane >= SB_DH) == (hsel == 1)
    diag = (i * rows) // SB_KEYS
    t = i * rows + lax.broadcasted_iota(jnp.int32, (rows, SB_KEYS), 0)
    s = diag * SB_KEYS + lax.broadcasted_iota(jnp.int32, (rows, SB_KEYS), 1)
    a = lax.broadcasted_iota(jnp.int32, (2 * SB_SUB, SB_SUB), 0) % SB_SUB
    b = lax.broadcasted_iota(jnp.int32, (2 * SB_SUB, SB_SUB), 1)
    return i, hsel, mine, diag, s < t, _mx(a > b), _mx(a < b)


def _sb_keys(j):
    return pl.ds(pl.multiple_of(j * SB_KEYS, SB_KEYS), SB_KEYS)


def _sb_specs(T, rows):
    pair = lambda col0: pl.BlockSpec((rows, 2 * SB_DH), lambda p, i, h: (i, col0 + p))
    whole = lambda col0: pl.BlockSpec((T, 2 * SB_DH), lambda p, i, h: (0, col0 + p))
    return pair, whole


def _sb_fwd(qkv, *, name):
    T = qkv.shape[0]
    rows = min(SB_ROWS_FWD, T)
    scale = SB_DH ** -0.5
    n_pairs = SB_HEADS // 2

    def body(q_ref, k_ref, v_ref, o_ref):
        i, hsel, mine, diag, mask, tri, _ = _sb_setup(q_ref, rows)
        qm = jnp.where(mine, q_ref[...], 0) * scale

        def tile(j, m, later, acc):
            total, _, w = _sb_scores(qm, k_ref[_sb_keys(j), :], later, tri, m)
            return later + total, acc + _dot(w, v_ref[_sb_keys(j), :])

        carry = tile(diag, mask, jnp.zeros((rows, 1), F32), jnp.zeros((rows, 2 * SB_DH), F32))
        _, acc = lax.fori_loop(0, diag, lambda it, cr: tile(diag - 1 - it, None, *cr), carry)
        res = jnp.where(mine, acc, 0.0)

        @pl.when(hsel == 0)
        def _():
            o_ref[...] = res

        @pl.when(hsel == 1)
        def _():
            o_ref[...] += res

    pair, whole = _sb_specs(T, rows)
    return pl.pallas_call(
        body, grid=(n_pairs, T // rows, 2), in_specs=[pair(0), whole(n_pairs), whole(2 * n_pairs)], out_specs=pair(0),
        out_shape=jax.ShapeDtypeStruct((T, SB_WIDTH), F32), compiler_params=_params("parallel", "arbitrary", "arbitrary"), name=name,
    )(qkv, qkv, qkv)


def _sb_bwd(qkv, dmix, *, name):
    T = qkv.shape[0]
    rows = min(SB_ROWS_BWD, T)
    scale = SB_DH ** -0.5
    n_pairs = SB_HEADS // 2

    def body(q_ref, k_ref, v_ref, do_ref, dq_ref, dk_ref, dv_ref, da_ref, beta_ref):
        i, hsel, mine, diag, mask, tri, tri_before = _sb_setup(q_ref, rows)

        @pl.when((i == 0) & (hsel == 0))
        def _():
            dk_ref[...] = jnp.zeros_like(dk_ref)
            dv_ref[...] = jnp.zeros_like(dv_ref)

        qm = jnp.where(mine, q_ref[...], 0) * scale
        do_m = _mx(jnp.where(mine, do_ref[...], 0.0))

        def weights(j, m, later):
            total, lb, w = _sb_scores(qm, k_ref[_sb_keys(j), :], later, tri, m)
            da_ref[j] = _dot_nt(do_m, v_ref[_sb_keys(j), :]) * w
            beta_ref[j] = jnp.exp(lb)
            dv_ref[_sb_keys(j), :] += _dot_tn(w, do_m)
            return later + total

        later = weights(diag, mask, jnp.zeros((rows, 1), F32))
        lax.fori_loop(0, diag, lambda it, c: weights(diag - 1 - it, None, c), later)

        def logits(j, m, before, dq):
            da = da_ref[j]
            earlier, total = _key_order_sums(da, tri_before, later=False)
            dz = da - beta_ref[j] * (da + earlier + before)
            if m is not None:
                dz = jnp.where(m, dz, 0.0)
            dz = _mx(dz)
            dk_ref[_sb_keys(j), :] += _dot_tn(dz, qm)
            return before + total, dq + _dot(dz, k_ref[_sb_keys(j), :])

        carry = (jnp.zeros((rows, 1), F32), jnp.zeros((rows, 2 * SB_DH), F32))
        carry = lax.fori_loop(0, diag, lambda j, cr: logits(j, None, *cr), carry)
        res = jnp.where(mine, logits(diag, mask, *carry)[1] * scale, 0.0)

        @pl.when(hsel == 0)
        def _():
            dq_ref[...] = res

        @pl.when(hsel == 1)
        def _():
            dq_ref[...] += res

    pair, whole = _sb_specs(T, rows)
    shp = jax.ShapeDtypeStruct((T, SB_WIDTH), F32)
    n_tiles = T // SB_KEYS
    return pl.pallas_call(
        body, grid=(n_pairs, T // rows, 2), in_specs=[pair(0), whole(n_pairs), whole(2 * n_pairs), pair(0)],
        out_specs=[pair(0), whole(0), whole(0)], out_shape=[shp, shp, shp],
        scratch_shapes=[pltpu.VMEM((n_tiles, rows, SB_KEYS), F32), pltpu.VMEM((n_tiles, rows, SB_KEYS), F32)],
        compiler_params=_params("parallel", "arbitrary", "arbitrary"), name=name,
    )(qkv, qkv, qkv, dmix)


def _chunk_iota():
    return lax.broadcasted_iota(jnp.int32, (HG_CHUNK, HG_DH), 0)


def _chunk_cumsum(x, reverse=False):
    row = _chunk_iota()
    for sh in (1, 2, 4, 8):
        if reverse:
            x = x + jnp.where(row < HG_CHUNK - sh, pltpu.roll(x, HG_CHUNK - sh, 0), 0.0)
        else:
            x = x + jnp.where(row >= sh, pltpu.roll(x, sh, 0), 0.0)
    return x


def _hg_lower_bound(logits_ref):
    lg = logits_ref[...]
    e = jnp.exp(lg - jnp.max(lg, axis=0, keepdims=True))
    return e[0:1, :] / jnp.sum(e, axis=0, keepdims=True)


def _hg_chunk_terms(fr, q, lb):
    sig = jax.nn.sigmoid(fr)
    f = lb + (1.0 - lb) * sig
    kk = 1.0 - f
    G = _chunk_cumsum(jnp.log(f))
    g_last = G[HG_CHUNK - 1:HG_CHUNK, :]
    e_g, e_ng, e_lg = jnp.exp(G), jnp.exp(-G), jnp.exp(g_last - G)
    return dict(sig=sig, f=f, kk=kk, e_g=e_g, e_ng=e_ng, e_lg=e_lg, q_dec=q * e_g, k_intra=kk * e_ng,
                k_state=kk * e_lg, decay=jnp.exp(g_last))


def _hg_causal():
    c = lax.broadcasted_iota(jnp.int32, (HG_CHUNK, HG_CHUNK), 0)
    s = lax.broadcasted_iota(jnp.int32, (HG_CHUNK, HG_CHUNK), 1)
    return s <= c


def _hg_specs(T, tb, col0, order):
    return [pl.BlockSpec((tb, HG_WIDTH), functools.partial(lambda i, j: (order(i), j), j=col0 + j)) for j in range(4)]


def _hg_fwd(proj, logits, norm_g, *, name):
    T = proj.shape[0]
    tb = min(HG_TOKENS, T)
    nch = tb // HG_CHUNK

    def body(q_ref, f_ref, i_ref, gate_ref, lg_ref, ng_ref, out_ref, o_ref, s_ref, st_ref):
        @pl.when(pl.program_id(0) == 0)
        def _():
            st_ref[...] = jnp.zeros_like(st_ref)

        lb_all = _hg_lower_bound(lg_ref)
        causal = _hg_causal()

        def chunk(ci, _):
            rows = pl.ds(pl.multiple_of(ci * HG_CHUNK, HG_CHUNK), HG_CHUNK)
            for hh in range(HG_HEADS):
                cols = slice(hh * HG_DH, (hh + 1) * HG_DH)
                t = _hg_chunk_terms(f_ref[rows, cols], q_ref[rows, cols], lb_all[:, cols])
                v = i_ref[rows, cols]
                st = st_ref[hh]
                scores = jnp.where(causal, _dot_nt(t["q_dec"], t["k_intra"]), 0.0)
                o_ref[rows, cols] = _dot(scores, v) + _dot_nt(t["q_dec"], st)
                s_ref[ci, hh] = st
                st_ref[hh] = st * t["decay"] + _dot_tn(v, t["k_state"])
            return 0

        lax.fori_loop(0, nch, chunk, 0, unroll=2)
        for hh in range(HG_HEADS):
            cols = slice(hh * HG_DH, (hh + 1) * HG_DH)
            o = o_ref[:, cols]
            gate = gate_ref[:, cols]
            on = o * lax.rsqrt(jnp.mean(o * o, axis=-1, keepdims=True) + RMS_EPS) * ng_ref[:, cols]
            out_ref[:, cols] = (on * (gate * jax.nn.sigmoid(gate))).astype(out_ref.dtype)

    blk = pl.BlockSpec((tb, HG_WIDTH), lambda i: (i, 0))
    return pl.pallas_call(
        body, grid=(T // tb,),
        in_specs=_hg_specs(T, tb, 3, lambda i: i) + [pl.BlockSpec((3, HG_WIDTH), lambda i: (0, 0)), pl.BlockSpec((1, HG_WIDTH), lambda i: (0, 0))],
        out_specs=[blk, blk, pl.BlockSpec((nch, HG_HEADS, HG_DH, HG_DH), lambda i: (i, 0, 0, 0))],
        out_shape=[jax.ShapeDtypeStruct((T, HG_WIDTH), MXU_DTYPE), jax.ShapeDtypeStruct((T, HG_WIDTH), F32),
                   jax.ShapeDtypeStruct((T // HG_CHUNK, HG_HEADS, HG_DH, HG_DH), F32)],
        scratch_shapes=[pltpu.VMEM((HG_HEADS, HG_DH, HG_DH), F32)],
        compiler_params=_params("arbitrary"), name=name,
    )(proj, proj, proj, proj, logits, norm_g)


def _hg_bwd(proj, o_raw, states, dmix, logits, norm_g, *, name):
    T = proj.shape[0]
    tb = min(HG_TOKENS, T)
    nch = tb // HG_CHUNK
    nb = T // tb
    rev = lambda i: nb - 1 - i

    def body(q_ref, f_ref, i_ref, gate_ref, o_ref, s_ref, dout_ref, lg_ref, ng_ref, dp_ref, dlb_ref, dng_ref, do_ref, dst_ref):
        @pl.when(pl.program_id(0) == 0)
        def _():
            dst_ref[...] = jnp.zeros_like(dst_ref)
            dlb_ref[...] = jnp.zeros_like(dlb_ref)
            dng_ref[...] = jnp.zeros_like(dng_ref)

        lb_all = _hg_lower_bound(lg_ref)
        causal = _hg_causal()
        row = _chunk_iota()
        for hh in range(HG_HEADS):
            cols = slice(hh * HG_DH, (hh + 1) * HG_DH)
            o, gate, dout, ng = o_ref[:, cols], gate_ref[:, cols], dout_ref[:, cols], ng_ref[:, cols]
            sg = jax.nn.sigmoid(gate)
            r = lax.rsqrt(jnp.mean(o * o, axis=-1, keepdims=True) + RMS_EPS)
            oh = o * r
            dp_ref[:, 3 * HG_WIDTH + hh * HG_DH:3 * HG_WIDTH + (hh + 1) * HG_DH] = dout * (oh * ng) * (sg * (1.0 + gate * (1.0 - sg)))
            don = dout * (gate * sg)
            dng_ref[:, cols] += jnp.sum(don * oh, axis=0, keepdims=True)
            doh = don * ng
            do_ref[:, cols] = r * (doh - oh * jnp.mean(doh * oh, axis=-1, keepdims=True))

        def chunk(it, _):
            ci = nch - 1 - it
            rows = pl.ds(pl.multiple_of(ci * HG_CHUNK, HG_CHUNK), HG_CHUNK)
            for hh in range(HG_HEADS):
                cols = slice(hh * HG_DH, (hh + 1) * HG_DH)
                lb = lb_all[:, cols]
                t = _hg_chunk_terms(f_ref[rows, cols], q_ref[rows, cols], lb)
                v, do_c, st, dst = i_ref[rows, cols], do_ref[rows, cols], s_ref[ci, hh], dst_ref[hh]
                scores = jnp.where(causal, _dot_nt(t["q_dec"], t["k_intra"]), 0.0)
                dscores = jnp.where(causal, _dot_nt(do_c, v), 0.0)
                dqd = _dot(dscores, t["k_intra"]) + _dot(do_c, st)
                dki = _dot_tn(dscores, t["q_dec"])
                dks = _dot(v, dst)
                dp_ref[rows, 2 * HG_WIDTH + hh * HG_DH:2 * HG_WIDTH + (hh + 1) * HG_DH] = _dot_tn(scores, do_c) + _dot_nt(t["k_state"], dst)
                ddecay = jnp.sum(st * dst, axis=0, keepdims=True)
                dst_ref[hh] = dst * t["decay"] + _dot_tn(do_c, t["q_dec"])
                dks_ks = dks * t["k_state"]
                d_glast = jnp.sum(dks_ks, axis=0, keepdims=True) + ddecay * t["decay"]
                d_g = dqd * t["q_dec"] - dki * t["k_intra"] - dks_ks + jnp.where(row == HG_CHUNK - 1, d_glast, 0.0)
                df = _chunk_cumsum(d_g, reverse=True) / t["f"] - (dki * t["e_ng"] + dks * t["e_lg"])
                dp_ref[rows, hh * HG_DH:(hh + 1) * HG_DH] = dqd * t["e_g"]
                dp_ref[rows, HG_WIDTH + hh * HG_DH:HG_WIDTH + (hh + 1) * HG_DH] = df * (1.0 - lb) * t["sig"] * (1.0 - t["sig"])
                dlb_ref[:, cols] += jnp.sum(df * (1.0 - t["sig"]), axis=0, keepdims=True)
            return 0

        lax.fori_loop(0, nch, chunk, 0, unroll=2)

    blk = pl.BlockSpec((tb, HG_WIDTH), lambda i: (rev(i), 0))
    row_spec = pl.BlockSpec((1, HG_WIDTH), lambda i: (0, 0))
    return pl.pallas_call(
        body, grid=(nb,),
        in_specs=_hg_specs(T, tb, 3, rev) + [
            blk, pl.BlockSpec((nch, HG_HEADS, HG_DH, HG_DH), lambda i: (rev(i), 0, 0, 0)),
            pl.BlockSpec((tb, HG_WIDTH), lambda i: (rev(i), 1)), pl.BlockSpec((3, HG_WIDTH), lambda i: (0, 0)), row_spec],
        out_specs=[pl.BlockSpec((tb, 4 * HG_WIDTH), lambda i: (rev(i), 0)), row_spec, row_spec],
        out_shape=[jax.ShapeDtypeStruct((T, 4 * HG_WIDTH), F32), jax.ShapeDtypeStruct((1, HG_WIDTH), F32), jax.ShapeDtypeStruct((1, HG_WIDTH), F32)],
        scratch_shapes=[pltpu.VMEM((tb, HG_WIDTH), F32), pltpu.VMEM((HG_HEADS, HG_DH, HG_DH), F32)],
        compiler_params=_params("arbitrary"), name=name,
    )(proj, proj, proj, proj, o_raw, states, dmix, logits, norm_g)


def _shifted_copies(sh_ref, n_rows):
    keep = n_rows + CONV_HALO - 8
    for b in range(1, 8):
        sh_ref[b, 0:keep, :] = sh_ref[0, b:b + keep, :]


def _tap_rows(sh_ref, offset, r0, lanes):
    start = pl.multiple_of(r0 + (offset - offset % 8), 8)
    return sh_ref[offset % 8, pl.ds(start, CONV_ROWS), lanes]


def _conv_fwd(p, w_dw, b_dw, ln_g, ln_b, *, name):
    T, D = p.shape[0], p.shape[1] // 2
    tb = _token_block(T)
    hpb = tb // CONV_HALO
    lane_step = 512

    def body(p1_ref, p2_ref, q1_ref, q2_ref, w_ref, bdw_ref, g_ref, b_ref, a_ref, y_ref, act_ref, sh_ref):
        i = pl.program_id(0)
        a = p1_ref[...] * jax.nn.sigmoid(p2_ref[...])
        sh_ref[0, 0:CONV_HALO, :] = jnp.where(i > 0, q1_ref[...] * jax.nn.sigmoid(q2_ref[...]), 0.0)
        sh_ref[0, CONV_HALO:, :] = a
        a_ref[...] = a
        _shifted_copies(sh_ref, tb)

        def chunk(ci, _):
            r0 = pl.multiple_of(ci * CONV_ROWS, CONV_ROWS)
            for l0 in range(0, D, lane_step):
                lanes = slice(l0, l0 + lane_step)
                acc = jnp.broadcast_to(bdw_ref[:, lanes], (CONV_ROWS, lane_step))
                for k in range(CONV_WIDTH):
                    acc = acc + _tap_rows(sh_ref, CONV_HALO - CONV_WIDTH + 1 + k, r0, lanes) * w_ref[k:k + 1, lanes]
                y_ref[pl.ds(r0, CONV_ROWS), lanes] = acc
            return 0

        lax.fori_loop(0, tb // CONV_ROWS, chunk, 0)
        y = y_ref[...]
        mu = jnp.mean(y, axis=-1, keepdims=True)
        yc = y - mu
        s = yc * lax.rsqrt(jnp.mean(yc * yc, axis=-1, keepdims=True) + LN_EPS) * g_ref[...] + b_ref[...]
        act_ref[...] = (s * jax.nn.sigmoid(s)).astype(act_ref.dtype)

    prev = lambda i: jnp.maximum(i * hpb - 1, 0)
    blk = pl.BlockSpec((tb, D), lambda i: (i, 0))
    row = pl.BlockSpec((1, D), lambda i: (0, 0))
    return pl.pallas_call(
        body, grid=(T // tb,),
        in_specs=[blk, pl.BlockSpec((tb, D), lambda i: (i, 1)), pl.BlockSpec((CONV_HALO, D), lambda i: (prev(i), 0)),
                  pl.BlockSpec((CONV_HALO, D), lambda i: (prev(i), 1)), pl.BlockSpec((CONV_HALO, D), lambda i: (0, 0)), row, row, row],
        out_specs=[blk, blk, blk],
        out_shape=[jax.ShapeDtypeStruct((T, D), F32), jax.ShapeDtypeStruct((T, D), F32), jax.ShapeDtypeStruct((T, D), MXU_DTYPE)],
        scratch_shapes=[pltpu.VMEM((8, tb + CONV_HALO, D), F32)],
        compiler_params=_params("parallel"), name=name,
    )(p, p, p, p, w_dw, b_dw, ln_g, ln_b)


def _conv_bwd_norm(dact, y, ln_g, ln_b, *, name):
    T, D = y.shape
    tb = _token_block(T)

    def body(da_ref, y_ref, g_ref, b_ref, dy_ref, dg_ref, db_ref, cs_ref):
        @pl.when(pl.program_id(0) == 0)
        def _():
            dg_ref[...] = jnp.zeros_like(dg_ref)
            db_ref[...] = jnp.zeros_like(db_ref)
            cs_ref[...] = jnp.zeros_like(cs_ref)

        y, g = y_ref[...], g_ref[...]
        yc = y - jnp.mean(y, axis=-1, keepdims=True)
        rs = lax.rsqrt(jnp.mean(yc * yc, axis=-1, keepdims=True) + LN_EPS)
        yn = yc * rs
        s = yn * g + b_ref[...]
        sg = jax.nn.sigmoid(s)
        ds = da_ref[...] * (sg * (1.0 + s * (1.0 - sg)))
        dg_ref[...] += jnp.sum(ds * yn, axis=0, keepdims=True)
        db_ref[...] += jnp.sum(ds, axis=0, keepdims=True)
        dyn = ds * g
        dy = rs * (dyn - jnp.mean(dyn, axis=-1, keepdims=True) - yn * jnp.mean(dyn * yn, axis=-1, keepdims=True))
        dy_ref[...] = dy
        cs_ref[...] += jnp.sum(dy, axis=0, keepdims=True)

    blk = pl.BlockSpec((tb, D), lambda i: (i, 0))
    row = pl.BlockSpec((1, D), lambda i: (0, 0))
    rs_ = jax.ShapeDtypeStruct((1, D), F32)
    return pl.pallas_call(
        body, grid=(T // tb,), in_specs=[blk, blk, row, row], out_specs=[blk, row, row, row],
        out_shape=[jax.ShapeDtypeStruct((T, D), F32), rs_, rs_, rs_], compiler_params=_params("arbitrary"), name=name,
    )(dact, y, ln_g, ln_b)


def _conv_bwd_taps(dy, a, p, w_dw, *, name):
    T, D = dy.shape
    tb = _token_block(T)
    hpb = tb // CONV_HALO
    last = T // CONV_HALO - 1
    nb = T // tb
    lane_step = 128
    groups = CONV_ROWS // 8

    def body(dy_ref, dyn_ref, a_ref, p1_ref, p2_ref, w_ref, dp_ref, dw_ref, cs_ref, sh_ref, da_ref):
        i = pl.program_id(0)

        @pl.when(i == 0)
        def _():
            dw_ref[...] = jnp.zeros_like(dw_ref)
            cs_ref[...] = jnp.zeros_like(cs_ref)

        sh_ref[0, 0:tb, :] = dy_ref[...]
        sh_ref[0, tb:, :] = jnp.where(i < nb - 1, dyn_ref[...], 0.0)
        _shifted_copies(sh_ref, tb)
        for l0 in range(0, D, lane_step):
            lanes = slice(l0, l0 + lane_step)

            def chunk(ci, sums):
                r0 = pl.multiple_of(ci * CONV_ROWS, CONV_ROWS)
                a_c = a_ref[pl.ds(r0, CONV_ROWS), lanes]
                da = jnp.zeros((CONV_ROWS, lane_step), F32)
                new = []
                for k in range(CONV_WIDTH):
                    s_k = _tap_rows(sh_ref, CONV_WIDTH - 1 - k, r0, lanes)
                    da = da + s_k * w_ref[k:k + 1, lanes]
                    new.append(sums[k] + jnp.sum((s_k * a_c).reshape(groups, 8, lane_step), axis=0))
                da_ref[pl.ds(r0, CONV_ROWS), lanes] = da
                return tuple(new)

            sums = lax.fori_loop(0, tb // CONV_ROWS, chunk, tuple(jnp.zeros((8, lane_step), F32) for _ in range(CONV_WIDTH)))
            for k in range(CONV_WIDTH):
                dw_ref[k:k + 1, lanes] += jnp.sum(sums[k], axis=0, keepdims=True)
        da = da_ref[...]
        p1 = p1_ref[...]
        sg = jax.nn.sigmoid(p2_ref[...])
        dp1 = da * sg
        dp2 = da * p1 * (sg * (1.0 - sg))
        dp_ref[:, 0:D] = dp1
        dp_ref[:, D:] = dp2
        cs_ref[:, 0:D] += jnp.sum(dp1, axis=0, keepdims=True)
        cs_ref[:, D:] += jnp.sum(dp2, axis=0, keepdims=True)

    blk = pl.BlockSpec((tb, D), lambda i: (i, 0))
    return pl.pallas_call(
        body, grid=(nb,),
        in_specs=[blk, pl.BlockSpec((CONV_HALO, D), lambda i: (jnp.minimum((i + 1) * hpb, last), 0)), blk, blk,
                  pl.BlockSpec((tb, D), lambda i: (i, 1)), pl.BlockSpec((CONV_HALO, D), lambda i: (0, 0))],
        out_specs=[pl.BlockSpec((tb, 2 * D), lambda i: (i, 0)), pl.BlockSpec((CONV_HALO, D), lambda i: (0, 0)), pl.BlockSpec((1, 2 * D), lambda i: (0, 0))],
        out_shape=[jax.ShapeDtypeStruct((T, 2 * D), F32), jax.ShapeDtypeStruct((CONV_HALO, D), F32), jax.ShapeDtypeStruct((1, 2 * D), F32)],
        scratch_shapes=[pltpu.VMEM((8, tb + CONV_HALO, D), F32), pltpu.VMEM((tb, D), F32)],
        compiler_params=_params("arbitrary"), name=name,
    )(dy, dy, a, p, p, w_dw)


def _row_block(rows):
    for tr in (512, 256, 128, 64, 32, 16, 8):
        if rows % tr == 0:
            return tr
    return rows


def _sum_leading(x, *, name):
    n, R, C = x.shape
    tr = _row_block(R)

    def body(x_ref, o_ref):
        acc = x_ref[0]
        for j in range(1, n):
            acc = acc + x_ref[j]
        o_ref[...] = acc

    return pl.pallas_call(
        body, grid=(R // tr,), in_specs=[pl.BlockSpec((n, tr, C), lambda i: (0, i, 0))], out_specs=pl.BlockSpec((tr, C), lambda i: (i, 0)),
        out_shape=jax.ShapeDtypeStruct((R, C), x.dtype), compiler_params=_params("parallel"), name=name,
    )(x)


def _add_pair(x, y, *, name):
    n, R, C = x.shape
    tr = _row_block(R)

    def body(x_ref, y_ref, o_ref):
        o_ref[...] = x_ref[...] + y_ref[...]

    blk = pl.BlockSpec((1, tr, C), lambda j, i: (j, i, 0))
    return pl.pallas_call(
        body, grid=(n, R // tr), in_specs=[blk, blk], out_specs=blk,
        out_shape=jax.ShapeDtypeStruct((n, R, C), x.dtype), compiler_params=_params("parallel", "parallel"), name=name,
    )(x, y)


def _adamw(w, g, m, v, *, name):
    R, C = w.shape
    tr = _row_block(R)
    c1, c2 = 1.0 - ADAM_B1 ** ADAM_STEP, 1.0 - ADAM_B2 ** ADAM_STEP

    def body(w_ref, g_ref, m_ref, v_ref, d_ref, nm_ref, nv_ref):
        g_ = g_ref[...]
        nm = ADAM_B1 * m_ref[...] + (1.0 - ADAM_B1) * g_
        nv = ADAM_B2 * v_ref[...] + (1.0 - ADAM_B2) * (g_ * g_)
        d_ref[...] = -ADAM_LR * ((nm / c1) / (jnp.sqrt(nv / c2) + ADAM_EPS) + ADAM_WD * w_ref[...])
        nm_ref[...] = nm
        nv_ref[...] = nv

    blk = pl.BlockSpec((tr, C), lambda i: (i, 0))
    shp = jax.ShapeDtypeStruct((R, C), F32)
    return pl.pallas_call(
        body, grid=(R // tr,), in_specs=[blk] * 4, out_specs=[blk] * 3, out_shape=[shp] * 3,
        compiler_params=_params("parallel"), name=name,
    )(w, g, m, v)


def _small_reduce(packs, logits, *, name):
    n, R, C = packs.shape

    def body(p_ref, lg_ref, s_ref, dlg_ref):
        acc = p_ref[0]
        for j in range(1, n):
            acc = acc + p_ref[j]
        s_ref[...] = acc
        lg = lg_ref[...]
        e = jnp.exp(lg - jnp.max(lg, axis=0, keepdims=True))
        sm = e / jnp.sum(e, axis=0, keepdims=True)
        dlb = acc[5:6, HG_WIDTH:2 * HG_WIDTH]
        first = lax.broadcasted_iota(jnp.int32, sm.shape, 0) == 0
        dlg_ref[...] = sm[0:1, :] * (jnp.where(first, 1.0, 0.0) - sm) * dlb

    whole = lambda shape: pl.BlockSpec(shape, lambda: (0,) * len(shape))
    return pl.pallas_call(
        body, in_specs=[whole((n, R, C)), whole(logits.shape)], out_specs=[whole((R, C)), whole(logits.shape)],
        out_shape=[jax.ShapeDtypeStruct((R, C), F32), jax.ShapeDtypeStruct(logits.shape, F32)],
        compiler_params=pltpu.CompilerParams(vmem_limit_bytes=VMEM_LIMIT), name=name,
    )(packs, logits)


HBM_SPEC = pl.BlockSpec(memory_space=pl.ANY)


def _place():
    return lax.axis_index("x"), lax.axis_index("y"), lax.axis_index("c")


def _exchange(arrays, peers_of, slot_of, n_slots, *, name):
    n = len(arrays)
    n_peers = len(peers_of(0, 0, 0))

    def body(*refs):
        ins, outs = refs[:n], refs[n:2 * n]
        send_sems, recv_sems = refs[2 * n:]
        x, y, c = _place()
        slot = slot_of(x, y, c)
        copies = []
        for a in range(n):
            for k, peer in enumerate(peers_of(x, y, c)):
                cp = pltpu.make_async_remote_copy(
                    src_ref=ins[a], dst_ref=outs[a].at[slot], send_sem=send_sems.at[a * n_peers + k],
                    recv_sem=recv_sems.at[a * n_peers + k], device_id=peer, device_id_type=MESH)
                cp.start()
                copies.append(cp)
        for cp in copies:
            cp.wait()

    outs = pl.pallas_call(
        body, in_specs=[HBM_SPEC] * n, out_specs=[HBM_SPEC] * n,
        out_shape=[jax.ShapeDtypeStruct((n_slots,) + a.shape, a.dtype) for a in arrays],
        scratch_shapes=[pltpu.SemaphoreType.DMA((n * n_peers,)), pltpu.SemaphoreType.DMA((n * n_peers,))],
        name=name,
    )(*arrays)
    slot = slot_of(*_place())
    return [lax.dynamic_update_index_in_dim(o, a, slot, 0) for o, a in zip(outs, arrays)]


def _same_core_peers(x, y, c):
    return [(1 - x, y, c), (x, 1 - y, c), (1 - x, 1 - y, c)]


def _all_peers(x, y, c):
    flip = lambda v, b: 1 - v if b else v
    return [(flip(x, r & 4), flip(y, r & 2), flip(c, r & 1)) for r in range(1, 8)]


def _gather_chips(arrays, *, name):
    return _exchange(arrays, _same_core_peers, lambda x, y, c: 2 * x + y, N_CHIPS, name=name)


def _gather_all(arrays, *, name):
    return _exchange(arrays, _all_peers, lambda x, y, c: 4 * x + 2 * y + c, N_DEV, name=name)


def _pair_swap(arrays, *, name):
    n = len(arrays)

    def body(*refs):
        ins, outs = refs[:n], refs[n:2 * n]
        send_sems, recv_sems = refs[2 * n:]
        x, y, c = _place()
        copies = []
        for a in range(n):
            cp = pltpu.make_async_remote_copy(
                src_ref=ins[a].at[1 - c], dst_ref=outs[a], send_sem=send_sems.at[a], recv_sem=recv_sems.at[a],
                device_id=(x, y, 1 - c), device_id_type=MESH)
            cp.start()
            copies.append(cp)
        for cp in copies:
            cp.wait()

    return pl.pallas_call(
        body, in_specs=[HBM_SPEC] * n, out_specs=[HBM_SPEC] * n,
        out_shape=[jax.ShapeDtypeStruct(a.shape[1:], a.dtype) for a in arrays],
        scratch_shapes=[pltpu.SemaphoreType.DMA((n,)), pltpu.SemaphoreType.DMA((n,))], name=name,
    )(*arrays)


def _chip_scatter(p, *, name):
    def body(p_ref, o_ref, send_sems, recv_sems):
        x, y, c = _place()
        me = 2 * x + y
        copies = []
        for k, (px, py, pc) in enumerate(_same_core_peers(x, y, c)):
            cp = pltpu.make_async_remote_copy(
                src_ref=p_ref.at[2 * px + py], dst_ref=o_ref.at[me], send_sem=send_sems.at[k], recv_sem=recv_sems.at[k],
                device_id=(px, py, pc), device_id_type=MESH)
            cp.start()
            copies.append(cp)
        for cp in copies:
            cp.wait()

    out = pl.pallas_call(
        body, in_specs=[HBM_SPEC], out_specs=HBM_SPEC, out_shape=jax.ShapeDtypeStruct(p.shape, p.dtype),
        scratch_shapes=[pltpu.SemaphoreType.DMA((3,)), pltpu.SemaphoreType.DMA((3,))], name=name,
    )(p)
    x, y, _ = _place()
    me = 2 * x + y
    return lax.dynamic_update_index_in_dim(out, lax.dynamic_index_in_dim(p, me, 0, keepdims=False), me, 0)


def _pair_gather(q, *, name):
    def body(q_ref, o_ref, send_sem, recv_sem):
        x, y, c = _place()
        cp = pltpu.make_async_remote_copy(src_ref=q_ref, dst_ref=o_ref.at[c], send_sem=send_sem, recv_sem=recv_sem,
                                          device_id=(x, y, 1 - c), device_id_type=MESH)
        cp.start()
        cp.wait()

    out = pl.pallas_call(
        body, in_specs=[HBM_SPEC], out_specs=HBM_SPEC, out_shape=jax.ShapeDtypeStruct((2,) + q.shape, q.dtype),
        scratch_shapes=[pltpu.SemaphoreType.DMA, pltpu.SemaphoreType.DMA], name=name,
    )(q)
    return lax.dynamic_update_index_in_dim(out, q, _place()[2], 0)


def _cols_from_chips(g):
    g = jnp.moveaxis(g, 0, -2)
    return g.reshape(g.shape[:-2] + (g.shape[-2] * g.shape[-1],))


def _rows_from_chips(g):
    g = jnp.moveaxis(g, 0, -3)
    return g.reshape(g.shape[:-3] + (g.shape[-3] * g.shape[-2], g.shape[-1]))


def _grad_blocks(dw, kind):
    if kind == "cols2d":
        K, N = dw.shape
        b = dw.reshape(2, K // 2, N_CHIPS, N // N_CHIPS).transpose(2, 0, 1, 3)
    elif kind == "rows2d":
        b = dw.reshape(N_CHIPS, 2, dw.shape[0] // 8, dw.shape[1])
    elif kind == "cols3d":
        L, K, N = dw.shape
        b = dw.reshape(L, K, N_CHIPS, N // N_CHIPS).transpose(2, 0, 1, 3)
    else:
        L, K, N = dw.shape
        b = dw.reshape(L, N_CHIPS, K // N_CHIPS, N).transpose(1, 0, 2, 3)
    return b.reshape(N_CHIPS, 2, -1, D_MODEL)


def _pad_rows(a, rows):
    return jnp.concatenate([a, jnp.zeros((rows - a.shape[0],) + a.shape[1:], a.dtype)], axis=0)


def _forward_backward(x, target, W):
    row = lambda a: a.reshape(1, -1)
    relu2 = lambda acc: (acc, jnp.square(jnp.maximum(acc, 0.0)))
    residual = lambda acc, res: (res + acc,)
    G = {}

    u0 = _rmsnorm_fwd(x, row(W["norm_mix_g"][0]), name="norm_mix0")
    proj = _matmul(u0, W["w_in"], mode="nn", out_dtypes=[F32], tn=896, name="in_proj")
    qkv = proj[:, :3 * SB_WIDTH].astype(MXU_DTYPE)
    o_sb = _sb_fwd(qkv, name="sb_fwd")
    hg_out, hg_o, hg_states = _hg_fwd(proj, W["hg_lb_logits"], row(W["hg_norm_g"]), name="hg_fwd")
    mix = jnp.concatenate([o_sb.astype(MXU_DTYPE), hg_out], axis=-1)
    h1 = _matmul(mix, W["w_out"], mode="nn", out_dtypes=[F32], epilogue=residual, tiles=[x], name="out_proj")
    u1 = _rmsnorm_fwd(h1, row(W["norm_ffn_g"][0]), name="norm_ffn0")
    a0, r0 = _matmul(u1, W["w_ff1"][0], mode="nn", out_dtypes=[F32, MXU_DTYPE], epilogue=relu2, name="ff1_0")
    h2 = _matmul(r0, W["w_ff2"][0], mode="nn", out_dtypes=[F32], epilogue=residual, tiles=[h1], name="ff2_0")
    u2 = _rmsnorm_fwd(h2, row(W["norm_mix_g"][1]), name="norm_mix1")
    p = _matmul(u2, W["w_glu"], mode="nn", out_dtypes=[F32], epilogue=lambda acc, b: (acc + b,), rows=[row(W["b_glu"])], name="glu_proj")
    w_dw = _pad_rows(W["w_dw"], CONV_HALO)
    ca, cy, cact = _conv_fwd(p, w_dw, row(W["b_dw"]), row(W["ln_g"]), row(W["ln_b"]), name="conv_fwd")
    h3 = _matmul(cact, W["w_pw"], mode="nn", out_dtypes=[F32], epilogue=lambda acc, res, b: (res + acc + b,),
                 tiles=[h2], rows=[row(W["b_pw"])], name="pw_proj")
    u3 = _rmsnorm_fwd(h3, row(W["norm_ffn_g"][1]), name="norm_ffn1")
    a1, r1 = _matmul(u3, W["w_ff1"][1], mode="nn", out_dtypes=[F32, MXU_DTYPE], epilogue=relu2, name="ff1_1")
    h4 = _matmul(r1, W["w_ff2"][1], mode="nn", out_dtypes=[F32], epilogue=residual, tiles=[h3], name="ff2_1")

    dh4, G["final_norm_g"], loss = _loss_head(h4, row(W["final_norm_g"]), target, name="loss_head")

    def mlp_bwd(dh, h_in, u, a, r, layer, tag):
        d_relu2 = lambda acc, a_blk: (acc * (2.0 * jnp.maximum(a_blk, 0.0)),)
        da = _matmul(dh, W["w_ff2"][layer], mode="nt", out_dtypes=[MXU_DTYPE], epilogue=d_relu2, tiles=[a], name="d_ff2_act" + tag)
        dw2 = _matmul(r, dh, mode="tn", out_dtypes=[F32], name="d_ff2_w" + tag)
        dw1 = _matmul(u, da, mode="tn", out_dtypes=[F32], name="d_ff1_w" + tag)
        du = _matmul(da, W["w_ff1"][layer], mode="nt", out_dtypes=[F32], name="d_ff1_act" + tag)
        dh_in, dg, cs = _rmsnorm_bwd(du, h_in, row(W["norm_ffn_g"][layer]), dh, name="d_norm_ffn" + tag)
        return dh_in, dg, cs, dw1, dw2

    dh3, dg_ffn1, cs_h3, dw1_1, dw2_1 = mlp_bwd(dh4, h3, u3, a1, r1, 1, "1")
    G["b_pw"] = cs_h3
    dact = _matmul(dh3, W["w_pw"], mode="nt", out_dtypes=[F32], name="d_pw_act")
    G["w_pw"] = _matmul(cact, dh3, mode="tn", out_dtypes=[F32], name="d_pw_w")
    dy, G["ln_g"], G["ln_b"], G["b_dw"] = _conv_bwd_norm(dact, cy, row(W["ln_g"]), row(W["ln_b"]), name="d_conv_norm")
    dp, G["w_dw"], G["b_glu"] = _conv_bwd_taps(dy, ca, p, w_dw, name="d_conv_taps")
    G["w_glu"] = _matmul(u2, dp, mode="tn", out_dtypes=[F32], name="d_glu_w")
    du2 = _matmul(dp, W["w_glu"], mode="nt", out_dtypes=[F32], name="d_glu_act")
    dh2, dg_mix1, _ = _rmsnorm_bwd(du2, h2, row(W["norm_mix_g"][1]), dh3, name="d_norm_mix1")
    dh1, dg_ffn0, _, dw1_0, dw2_0 = mlp_bwd(dh2, h1, u1, a0, r0, 0, "0")
    G["w_ff1"], G["w_ff2"] = jnp.stack([dw1_0, dw1_1]), jnp.stack([dw2_0, dw2_1])
    G["norm_ffn_g"] = jnp.concatenate([dg_ffn0, dg_ffn1], axis=0)
    dmix = _matmul(dh1, W["w_out"], mode="nt", out_dtypes=[F32], name="d_out_act")
    G["w_out"] = _matmul(mix, dh1, mode="tn", out_dtypes=[F32], name="d_out_w")
    dsq, dsk, dsv = _sb_bwd(qkv, dmix, name="sb_bwd")
    d_hg, G["hg_lb"], G["hg_norm_g"] = _hg_bwd(proj, hg_o, hg_states, dmix, W["hg_lb_logits"], row(W["hg_norm_g"]), name="hg_bwd")
    dproj = jnp.concatenate([dsq, dsk, dsv, d_hg], axis=-1).astype(MXU_DTYPE)
    G["w_in"] = _matmul(u0, dproj, mode="tn", out_dtypes=[F32], tn=896, name="d_in_w")
    du0 = _matmul(dproj, W["w_in"], mode="nt", out_dtypes=[F32], tk=896, name="d_in_act")
    dx, dg_mix0, _ = _rmsnorm_bwd(du0, x, row(W["norm_mix_g"][0]), dh1, name="d_norm_mix0")
    G["norm_mix_g"] = jnp.concatenate([dg_mix0, dg_mix1], axis=0)
    return loss, dx, G


BIG = (("w_in_ab", "w_in", "cols2d"), ("w_out_ab", "w_out", "rows2d"), ("conv_w_glu", "w_glu", "cols2d"),
       ("conv_w_pw", "w_pw", "rows2d"), ("w_ff1", "w_ff1", "cols3d"), ("w_ff2", "w_ff2", "rows3d"))
SMALL_SHARDED = ("conv_b_glu", "conv_w_dw", "conv_b_dw", "conv_ln_g", "conv_ln_b", "conv_b_pw")
REPLICATED = ("norm_mix_g", "norm_ffn_g", "hg_lb_logits", "hg_norm_g", "final_norm_g")
ORDER = ("norm_mix_g", "norm_ffn_g", "w_in_ab", "w_out_ab", "hg_lb_logits", "hg_norm_g", "conv_w_glu", "conv_b_glu",
         "conv_w_dw", "conv_b_dw", "conv_ln_g", "conv_ln_b", "conv_w_pw", "conv_b_pw", "w_ff1", "w_ff2", "final_norm_g")


def _step(x, loss_target, w, m, v):
    D = D_MODEL
    x2, t2 = x.reshape(-1, D), loss_target.reshape(-1, D)
    chip = 2 * lax.axis_index("x") + lax.axis_index("y")

    small_in = jnp.concatenate([w["conv_b_glu"].reshape(2, 256), w["conv_w_dw"].reshape(CONV_WIDTH, 256)] +
                               [w[n].reshape(1, 256) for n in ("conv_b_dw", "conv_ln_g", "conv_ln_b", "conv_b_pw")], axis=0)
    gathered = _gather_chips([w[n].astype(MXU_DTYPE) for n, _, _ in BIG] + [_pad_rows(small_in, 40)], name="gather_weights")
    gw = dict(zip([s for _, s, _ in BIG], gathered[:-1]))
    gs = gathered[-1]
    vec = lambda r0, r1: gs[:, r0:r1].transpose(1, 0, 2).reshape(r1 - r0, N_CHIPS * 256)
    W = {
        "w_in": _cols_from_chips(gw["w_in"][:, 0]), "w_out": _rows_from_chips(gw["w_out"][:, 0]),
        "w_glu": _cols_from_chips(gw["w_glu"][:, 0]), "w_pw": _rows_from_chips(gw["w_pw"][:, 0]),
        "w_ff1": _cols_from_chips(gw["w_ff1"]), "w_ff2": _rows_from_chips(gw["w_ff2"]),
        "b_glu": gs[:, 0:2].reshape(2 * D), "w_dw": vec(2, 33), "b_dw": vec(33, 34)[0], "ln_g": vec(34, 35)[0],
        "ln_b": vec(35, 36)[0], "b_pw": vec(36, 37)[0],
        "norm_mix_g": w["norm_mix_g"], "norm_ffn_g": w["norm_ffn_g"], "hg_lb_logits": w["hg_lb_logits"],
        "hg_norm_g": w["hg_norm_g"], "final_norm_g": w["final_norm_g"],
    }

    loss, dx, G = _forward_backward(x2, t2, W)

    pack = jnp.concatenate([
        G["norm_mix_g"], G["norm_ffn_g"], G["final_norm_g"], jnp.concatenate([G["hg_norm_g"], G["hg_lb"]], axis=1),
        _pad_rows(jnp.broadcast_to(loss, (1, D)), 2), G["b_glu"].reshape(2, D), G["w_dw"], G["b_dw"], G["ln_g"], G["ln_b"], G["b_pw"],
    ], axis=0)
    pack = _pad_rows(pack, SMALL_ROWS)
    (packs,) = _gather_all([pack], name="gather_small_grads")
    ssum, d_logits = _small_reduce(packs, w["hg_lb_logits"], name="reduce_small_grads")
    cut = lambda r0, r1: lax.dynamic_slice(ssum, (r0, chip * 256), (r1 - r0, 256))
    grads = {
        "norm_mix_g": ssum[0:2], "norm_ffn_g": ssum[2:4], "final_norm_g": ssum[4], "hg_norm_g": ssum[5, :HG_WIDTH].reshape(1, HG_HEADS, HG_DH),
        "hg_lb_logits": d_logits,
        "conv_b_glu": lax.dynamic_slice(ssum[8:10].reshape(1, 2 * D), (0, chip * 512), (1, 512)),
        "conv_w_dw": cut(10, 10 + CONV_WIDTH).reshape(1, CONV_WIDTH, 256),
        "conv_b_dw": cut(42, 43), "conv_ln_g": cut(43, 44), "conv_ln_b": cut(44, 45), "conv_b_pw": cut(45, 46),
    }
    loss_out = ssum[6, 0]

    blocks = jnp.concatenate([_grad_blocks(G[s], kind) for _, s, kind in BIG], axis=2)
    c = lax.axis_index("c")
    (from_pair,) = _pair_swap([blocks.transpose(1, 0, 2, 3)], name="grads_pair_swap")
    mine = lax.dynamic_index_in_dim(blocks, c, axis=1, keepdims=False)
    chip_sum = _add_pair(mine, from_pair, name="grads_pair_add")
    per_chip = _chip_scatter(chip_sum, name="grads_chip_scatter")
    half = _sum_leading(per_chip, name="grads_chip_add")
    full = _pair_gather(half, name="grads_pair_gather")
    off = 0
    for n, s, kind in BIG:
        shard = w[n].shape
        rows = w[n].size // (2 * D)
        grads[n] = full[:, off:off + rows].reshape(shard)
        off += rows

    delta, new_m, new_v = {}, {}, {}
    for n, _, _ in BIG:
        view = lambda a: a.reshape(-1, a.shape[-1])
        outs = _adamw(view(w[n]), view(grads[n]), view(m[n]), view(v[n]), name="adamw_" + n)
        delta[n], new_m[n], new_v[n] = (o.reshape(w[n].shape) for o in outs)
    small = SMALL_SHARDED + REPLICATED
    sizes = [w[n].size for n in small]
    total = sum(sizes)
    rows = -(-total // (8 * D)) * 8
    packed = lambda d: _pad_rows(jnp.concatenate([d[n].reshape(-1) for n in small]).reshape(-1, 128), rows * 8).reshape(rows, D)
    outs = _adamw(packed(w), packed(grads), packed(m), packed(v), name="adamw_small")
    off = 0
    for n, size in zip(small, sizes):
        delta[n], new_m[n], new_v[n] = (o.reshape(-1)[off:off + size].reshape(w[n].shape) for o in outs)
        off += size
    grads = {n: grads[n].reshape(w[n].shape) for n in ORDER}
    return (loss_out, dx.reshape(x.shape), *[grads[n] for n in ORDER], *[delta[n] for n in ORDER],
            *[new_m[n] for n in ORDER], *[new_v[n] for n in ORDER])


def kernel(x, norm_mix_g, norm_ffn_g, w_in_ab, w_out_ab, hg_lb_logits, hg_norm_g, conv_w_glu, conv_b_glu, conv_w_dw, conv_b_dw, conv_ln_g, conv_ln_b, conv_w_pw, conv_b_pw, w_ff1, w_ff2, final_norm_g, loss_target, m_norm_mix_g, m_norm_ffn_g, m_w_in_ab, m_w_out_ab, m_hg_lb_logits, m_hg_norm_g, m_conv_w_glu, m_conv_b_glu, m_conv_w_dw, m_conv_b_dw, m_conv_ln_g, m_conv_ln_b, m_conv_w_pw, m_conv_b_pw, m_w_ff1, m_w_ff2, m_final_norm_g, v_norm_mix_g, v_norm_ffn_g, v_w_in_ab, v_w_out_ab, v_hg_lb_logits, v_hg_norm_g, v_conv_w_glu, v_conv_b_glu, v_conv_w_dw, v_conv_b_dw, v_conv_ln_g, v_conv_ln_b, v_conv_w_pw, v_conv_b_pw, v_w_ff1, v_w_ff2, v_final_norm_g):
    args = locals()
    w = {n: args[n] for n in ORDER}
    m = {n: args["m_" + n] for n in ORDER}
    v = {n: args["v_" + n] for n in ORDER}
    return _step(x, loss_target, w, m, v)
```

```python
import functools

import jax
import jax.numpy as jnp
from jax import lax
from jax.experimental import pallas as pl
from jax.experimental.pallas import tpu as pltpu

F32 = jnp.float32
MXU_DTYPE = jnp.bfloat16
MESH = pl.DeviceIdType.MESH

D_MODEL = 1024
SB_HEADS, SB_DH, SB_WIDTH = 8, 64, 512
SB_KEYS = 512
SB_SUB = 256
SB_ROWS_FWD, SB_ROWS_BWD = 512, 256
HG_HEADS, HG_DH, HG_WIDTH = 4, 128, 512
HG_CHUNK = 16
HG_TOKENS = 256
IN_WIDTH = 3 * SB_WIDTH + 4 * HG_WIDTH
CONV_WIDTH = 31
CONV_HALO = 32
CONV_ROWS = 32
D_FF = 4096
RMS_EPS = 1e-6
LN_EPS = 1e-5
N_CHIPS = 4
N_DEV = 8
SMALL_ROWS = 48
VMEM_LIMIT = 56 * 1024 * 1024

ADAM_LR, ADAM_B1, ADAM_B2, ADAM_EPS, ADAM_WD, ADAM_STEP = 0.001, 0.9, 0.999, 1e-08, 0.01, 10


def _params(*sem):
    return pltpu.CompilerParams(dimension_semantics=sem, vmem_limit_bytes=VMEM_LIMIT)


def _mx(v):
    return v.astype(MXU_DTYPE)


def _dot(a, b):
    return jnp.dot(_mx(a), _mx(b), preferred_element_type=F32)


def _dot_nt(a, b):
    return lax.dot_general(_mx(a), _mx(b), (((1,), (1,)), ((), ())), preferred_element_type=F32)


def _dot_tn(a, b):
    return lax.dot_general(_mx(a), _mx(b), (((0,), (0,)), ((), ())), preferred_element_type=F32)


def _neg_abs(x):
    bits = lax.bitcast_convert_type(x, jnp.uint32) | jnp.uint32(0x80000000)
    return lax.bitcast_convert_type(bits, F32)


def _key_order_sums(v, tri2, later):
    hi = _mx(v)
    lo = _mx(v - hi.astype(F32))
    n = SB_KEYS // SB_SUB
    blocks = [slice(b * SB_SUB, (b + 1) * SB_SUB) for b in range(n)]
    totals = [jnp.sum(v[:, sl], axis=1, keepdims=True) for sl in blocks]
    sums = []
    for b, sl in enumerate(blocks):
        s = jnp.dot(jnp.concatenate([hi[:, sl], lo[:, sl]], axis=1), tri2, preferred_element_type=F32)
        for o in (range(b + 1, n) if later else range(b)):
            s = s + totals[o]
        sums.append(s)
    return jnp.concatenate(sums, axis=1), functools.reduce(lambda a, b: a + b, totals)


def _matmul(a, b, *, mode, out_dtypes, epilogue=None, tiles=(), rows=(), tm=512, tn=1024, tk=1024, name):
    if mode == "nn":
        (M, K), N = a.shape, b.shape[1]
    elif mode == "nt":
        (M, K), N = a.shape, b.shape[0]
    else:
        (K, M), N = a.shape, b.shape[1]
    tm, tn, tk = min(tm, M), min(tn, N), min(tk, K)
    assert M % tm == 0 and N % tn == 0 and K % tk == 0, (name, M, N, K)
    nk = K // tk
    a_spec = pl.BlockSpec((tk, tm), lambda i, j, k: (k, i)) if mode == "tn" else pl.BlockSpec((tm, tk), lambda i, j, k: (i, k))
    b_spec = pl.BlockSpec((tn, tk), lambda i, j, k: (j, k)) if mode == "nt" else pl.BlockSpec((tk, tn), lambda i, j, k: (k, j))
    dims = {"nn": ((1,), (0,)), "nt": ((1,), (1,)), "tn": ((0,), (0,))}[mode]
    n_t, n_r, n_o = len(tiles), len(rows), len(out_dtypes)
    if epilogue is None:
        epilogue = lambda acc: (acc,)

    def body(a_ref, b_ref, *rest):
        extra, outs, acc_ref = rest[:n_t + n_r], rest[n_t + n_r:n_t + n_r + n_o], rest[-1]
        k = pl.program_id(2)

        @pl.when(k == 0)
        def _():
            acc_ref[...] = jnp.zeros_like(acc_ref)

        acc_ref[...] += lax.dot_general(_mx(a_ref[...]), _mx(b_ref[...]), (dims, ((), ())), preferred_element_type=F32)

        @pl.when(k == nk - 1)
        def _():
            res = epilogue(acc_ref[...], *[e[...] for e in extra])
            for o_ref, r in zip(outs, res):
                o_ref[...] = r.astype(o_ref.dtype)

    tile_spec = pl.BlockSpec((tm, tn), lambda i, j, k: (i, j))
    row_spec = pl.BlockSpec((1, tn), lambda i, j, k: (0, j))
    outs = pl.pallas_call(
        body, grid=(M // tm, N // tn, nk),
        in_specs=[a_spec, b_spec] + [tile_spec] * n_t + [row_spec] * n_r,
        out_specs=[tile_spec] * n_o,
        out_shape=[jax.ShapeDtypeStruct((M, N), dt) for dt in out_dtypes],
        scratch_shapes=[pltpu.VMEM((tm, tn), F32)],
        compiler_params=_params("parallel", "parallel", "arbitrary"), name=name,
    )(a, b, *tiles, *rows)
    return outs[0] if n_o == 1 else outs


def _token_block(T):
    return min(512, T)


def _rmsnorm_fwd(h, g, *, name):
    T, D = h.shape
    tb = _token_block(T)

    def body(h_ref, g_ref, u_ref):
        x = h_ref[...]
        r = lax.rsqrt(jnp.mean(x * x, axis=-1, keepdims=True) + RMS_EPS)
        u_ref[...] = (x * r * g_ref[...]).astype(u_ref.dtype)

    blk = pl.BlockSpec((tb, D), lambda i: (i, 0))
    return pl.pallas_call(
        body, grid=(T // tb,), in_specs=[blk, pl.BlockSpec((1, D), lambda i: (0, 0))], out_specs=blk,
        out_shape=jax.ShapeDtypeStruct((T, D), MXU_DTYPE), compiler_params=_params("parallel"), name=name,
    )(h, g)


def _rms_bwd_math(x, g, du):
    r = lax.rsqrt(jnp.mean(x * x, axis=-1, keepdims=True) + RMS_EPS)
    gd = g * du
    dx = r * gd - x * (r * r * r) * jnp.mean(gd * x, axis=-1, keepdims=True)
    return dx, du * x * r


def _rmsnorm_bwd(du, h, g, dres, *, name):
    T, D = h.shape
    tb = _token_block(T)

    def body(du_ref, h_ref, g_ref, dres_ref, dh_ref, dg_ref, cs_ref):
        @pl.when(pl.program_id(0) == 0)
        def _():
            dg_ref[...] = jnp.zeros_like(dg_ref)
            cs_ref[...] = jnp.zeros_like(cs_ref)

        dx, dg_terms = _rms_bwd_math(h_ref[...], g_ref[...], du_ref[...])
        dh = dres_ref[...] + dx
        dh_ref[...] = dh
        dg_ref[...] += jnp.sum(dg_terms, axis=0, keepdims=True)
        cs_ref[...] += jnp.sum(dh, axis=0, keepdims=True)

    blk = pl.BlockSpec((tb, D), lambda i: (i, 0))
    row = pl.BlockSpec((1, D), lambda i: (0, 0))
    return pl.pallas_call(
        body, grid=(T // tb,), in_specs=[blk, blk, row, blk], out_specs=[blk, row, row],
        out_shape=[jax.ShapeDtypeStruct((T, D), F32), jax.ShapeDtypeStruct((1, D), F32), jax.ShapeDtypeStruct((1, D), F32)],
        compiler_params=_params("arbitrary"), name=name,
    )(du, h, g, dres)


def _loss_head(h, g, target, *, name):
    T, D = h.shape
    tb = _token_block(T)

    def body(h_ref, g_ref, t_ref, dh_ref, dg_ref, loss_ref):
        @pl.when(pl.program_id(0) == 0)
        def _():
            dg_ref[...] = jnp.zeros_like(dg_ref)
            loss_ref[...] = jnp.zeros_like(loss_ref)

        x, gg = h_ref[...], g_ref[...]
        r = lax.rsqrt(jnp.mean(x * x, axis=-1, keepdims=True) + RMS_EPS)
        diff = x * r * gg - t_ref[...]
        per_token = jnp.mean(diff * diff, axis=-1, keepdims=True)
        loss_ref[...] += 0.5 * jnp.sum(per_token, axis=0, keepdims=True)
        dx, dg_terms = _rms_bwd_math(x, gg, diff / D)
        dh_ref[...] = dx
        dg_ref[...] += jnp.sum(dg_terms, axis=0, keepdims=True)

    blk = pl.BlockSpec((tb, D), lambda i: (i, 0))
    row = pl.BlockSpec((1, D), lambda i: (0, 0))
    return pl.pallas_call(
        body, grid=(T // tb,), in_specs=[blk, row, blk], out_specs=[blk, row, pl.BlockSpec((1, 1), lambda i: (0, 0))],
        out_shape=[jax.ShapeDtypeStruct((T, D), F32), jax.ShapeDtypeStruct((1, D), F32), jax.ShapeDtypeStruct((1, 1), F32)],
        compiler_params=_params("arbitrary"), name=name,
    )(h, g, target)


def _sb_scores(qm, ks, later, tri, mask):
    z = _dot_nt(qm, ks)
    sp = jnp.maximum(z, 0.0) + jnp.log(1.0 + jnp.exp(_neg_abs(z)))
    lb = z - sp
    if mask is not None:
        sp = jnp.where(mask, sp, 0.0)
    after, total = _key_order_sums(sp, tri, later=True)
    w = jnp.exp(lb - (after + later))
    if mask is not None:
        w = jnp.where(mask, w, 0.0)
    return total, lb, w


def _sb_setup(q_ref, rows):
    i, hsel = pl.program_id(1), pl.program_id(2)
    lane = lax.broadcasted_iota(jnp.int32, (rows, 2 * SB_DH), 1)
    mine = (lane >= SB_DH) == (hsel == 1)
    diag = (i * rows) // SB_KEYS
    t = i * rows + lax.broadcasted_iota(jnp.int32, (rows, SB_KEYS), 0)
    s = diag * SB_KEYS + lax.broadcasted_iota(jnp.int32, (rows, SB_KEYS), 1)
    a = lax.broadcasted_iota(jnp.int32, (2 * SB_SUB, SB_SUB), 0) % SB_SUB
    b = lax.broadcasted_iota(jnp.int32, (2 * SB_SUB, SB_SUB), 1)
    return i, hsel, mine, diag, s < t, _mx(a > b), _mx(a < b)


def _sb_keys(j):
    return pl.ds(pl.multiple_of(j * SB_KEYS, SB_KEYS), SB_KEYS)


def _sb_descend(n, step, carry):
    carry = lax.fori_loop(0, n // 2, lambda it, cr: step(n - 2 - 2 * it, step(n - 1 - 2 * it, cr)), carry)
    return lax.cond(n % 2 == 1, lambda cr: step(0, cr), lambda cr: cr, carry)


def _sb_ascend(n, step, carry):
    odd = n % 2
    carry = lax.cond(odd == 1, lambda cr: step(0, cr), lambda cr: cr, carry)
    return lax.fori_loop(0, n // 2, lambda it, cr: step(odd + 2 * it + 1, step(odd + 2 * it, cr)), carry)


def _sb_call(body, qkv, extra_in, out_blocks, scratch, rider, rows, *, name):
    T = qkv.shape[0]
    n_pairs = SB_HEADS // 2
    grid = (n_pairs, T // rows, 2)
    pair = lambda col0: pl.BlockSpec((rows, 2 * SB_DH), lambda p, i, h: (i, col0 + p))
    whole = lambda col0: pl.BlockSpec((T, 2 * SB_DH), lambda p, i, h: (0, col0 + p))
    shape = jax.ShapeDtypeStruct((T, SB_WIDTH), F32)
    in_specs = [pair(0), whole(n_pairs), whole(2 * n_pairs)] + [pair(0)] * len(extra_in)
    out_specs = [pair(0) if kind == "pair" else whole(0) for kind in out_blocks]
    n_in, n_out, n_r = len(in_specs), len(out_specs), 0 if rider is None else len(rider.arrays)

    def kernel_body(*refs):
        ins, r_in = refs[:n_in], refs[n_in:n_in + n_r]
        outs, r_out = refs[n_in + n_r:n_in + n_r + n_out], refs[n_in + n_r + n_out:n_in + 2 * n_r + n_out]
        rest = refs[n_in + 2 * n_r + n_out:]
        ids = [pl.program_id(a) for a in range(3)]
        if rider is not None:
            @pl.when((ids[0] == 0) & (ids[1] == 0) & (ids[2] == 0))
            def _():
                for cp in rider.make(r_in, r_out, *rest[len(scratch):]):
                    cp.start()

        body(ins, outs, rest[:len(scratch)])
        if rider is not None:
            @pl.when((ids[0] == grid[0] - 1) & (ids[1] == grid[1] - 1) & (ids[2] == grid[2] - 1))
            def _():
                for cp in rider.make(r_in, r_out, *rest[len(scratch):]):
                    cp.wait()

    res = pl.pallas_call(
        kernel_body, grid=grid, in_specs=in_specs + [HBM_SPEC] * n_r, out_specs=out_specs + [HBM_SPEC] * n_r,
        out_shape=[shape] * n_out + ([] if rider is None else rider.out_shapes),
        scratch_shapes=list(scratch) + ([] if rider is None else rider.scratch()),
        compiler_params=_params("arbitrary", "arbitrary", "arbitrary"), name=name,
    )(qkv, qkv, qkv, *extra_in, *([] if rider is None else rider.arrays))
    return res[:n_out], res[n_out:]


def _sb_fwd(qkv, rider=None, *, name):
    rows = min(SB_ROWS_FWD, qkv.shape[0])
    scale = SB_DH ** -0.5

    def body(ins, outs, _):
        (q_ref, k_ref, v_ref), (o_ref,) = ins, outs
        i, hsel, mine, diag, mask, tri, _ = _sb_setup(q_ref, rows)
        qm = jnp.where(mine, q_ref[...], 0) * scale

        def tile(j, m, later, acc):
            total, _, w = _sb_scores(qm, k_ref[_sb_keys(j), :], later, tri, m)
            return later + total, acc + _dot(w, v_ref[_sb_keys(j), :])

        carry = tile(diag, mask, jnp.zeros((rows, 1), F32), jnp.zeros((rows, 2 * SB_DH), F32))
        _, acc = _sb_descend(diag, lambda j, cr: tile(j, None, *cr), carry)
        res = jnp.where(mine, acc, 0.0)

        @pl.when(hsel == 0)
        def _():
            o_ref[...] = res

        @pl.when(hsel == 1)
        def _():
            o_ref[...] += res

    (o,), got = _sb_call(body, qkv, [], ["pair"], [], rider, rows, name=name)
    return o, got


def _sb_bwd(qkv, dmix, rider=None, *, name):
    T = qkv.shape[0]
    rows = min(SB_ROWS_BWD, T)
    scale = SB_DH ** -0.5

    def body(ins, outs, scratch):
        (q_ref, k_ref, v_ref, do_ref), (dq_ref, dk_ref, dv_ref), (da_ref, beta_ref) = ins, outs, scratch
        i, hsel, mine, diag, mask, tri, tri_before = _sb_setup(q_ref, rows)

        @pl.when((i == 0) & (hsel == 0))
        def _():
            dk_ref[...] = jnp.zeros_like(dk_ref)
            dv_ref[...] = jnp.zeros_like(dv_ref)

        qm = jnp.where(mine, q_ref[...], 0) * scale
        do_m = _mx(jnp.where(mine, do_ref[...], 0.0))

        def weights(j, m, later):
            total, lb, w = _sb_scores(qm, k_ref[_sb_keys(j), :], later, tri, m)
            da_ref[j] = _dot_nt(do_m, v_ref[_sb_keys(j), :]) * w
            beta_ref[j] = jnp.exp(lb)
            dv_ref[_sb_keys(j), :] += _dot_tn(w, do_m)
            return later + total

        later = weights(diag, mask, jnp.zeros((rows, 1), F32))
        _sb_descend(diag, lambda j, c: weights(j, None, c), later)

        def logits(j, m, before, dq):
            da = da_ref[j]
            earlier, total = _key_order_sums(da, tri_before, later=False)
            dz = da - beta_ref[j] * (da + earlier + before)
            if m is not None:
                dz = jnp.where(m, dz, 0.0)
            dz = _mx(dz)
            dk_ref[_sb_keys(j), :] += _dot_tn(dz, qm)
            return before + total, dq + _dot(dz, k_ref[_sb_keys(j), :])

        carry = (jnp.zeros((rows, 1), F32), jnp.zeros((rows, 2 * SB_DH), F32))
        carry = _sb_ascend(diag, lambda j, cr: logits(j, None, *cr), carry)
        res = jnp.where(mine, logits(diag, mask, *carry)[1] * scale, 0.0)

        @pl.when(hsel == 0)
        def _():
            dq_ref[...] = res

        @pl.when(hsel == 1)
        def _():
            dq_ref[...] += res

    n_tiles = T // SB_KEYS
    scratch = [pltpu.VMEM((n_tiles, rows, SB_KEYS), F32), pltpu.VMEM((n_tiles, rows, SB_KEYS), F32)]
    return _sb_call(body, qkv, [dmix], ["pair", "whole", "whole"], scratch, rider, rows, name=name)


def _chunk_iota():
    return lax.broadcasted_iota(jnp.int32, (HG_CHUNK, HG_DH), 0)


def _chunk_cumsum(x, reverse=False):
    row = _chunk_iota()
    for sh in (1, 2, 4, 8):
        if reverse:
            x = x + jnp.where(row < HG_CHUNK - sh, pltpu.roll(x, HG_CHUNK - sh, 0), 0.0)
        else:
            x = x + jnp.where(row >= sh, pltpu.roll(x, sh, 0), 0.0)
    return x


def _hg_lower_bound(logits_ref):
    lg = logits_ref[...]
    e = jnp.exp(lg - jnp.max(lg, axis=0, keepdims=True))
    return e[0:1, :] / jnp.sum(e, axis=0, keepdims=True)


def _hg_chunk_terms(fr, q, lb):
    sig = jax.nn.sigmoid(fr)
    f = lb + (1.0 - lb) * sig
    kk = 1.0 - f
    G = _chunk_cumsum(jnp.log(f))
    g_last = G[HG_CHUNK - 1:HG_CHUNK, :]
    e_g, e_ng, e_lg = jnp.exp(G), jnp.exp(-G), jnp.exp(g_last - G)
    return dict(sig=sig, f=f, kk=kk, e_g=e_g, e_ng=e_ng, e_lg=e_lg, q_dec=q * e_g, k_intra=kk * e_ng,
                k_state=kk * e_lg, decay=jnp.exp(g_last))


def _hg_causal():
    c = lax.broadcasted_iota(jnp.int32, (HG_CHUNK, HG_CHUNK), 0)
    s = lax.broadcasted_iota(jnp.int32, (HG_CHUNK, HG_CHUNK), 1)
    return s <= c


def _hg_specs(T, tb, col0, order):
    return [pl.BlockSpec((tb, HG_WIDTH), functools.partial(lambda i, j: (order(i), j), j=col0 + j)) for j in range(4)]


def _hg_fwd(proj, logits, norm_g, *, name):
    T = proj.shape[0]
    tb = min(HG_TOKENS, T)
    nch = tb // HG_CHUNK

    def body(q_ref, f_ref, i_ref, gate_ref, lg_ref, ng_ref, out_ref, o_ref, s_ref, st_ref):
        @pl.when(pl.program_id(0) == 0)
        def _():
            st_ref[...] = jnp.zeros_like(st_ref)

        lb_all = _hg_lower_bound(lg_ref)
        causal = _hg_causal()

        def chunk(ci, _):
            rows = pl.ds(pl.multiple_of(ci * HG_CHUNK, HG_CHUNK), HG_CHUNK)
            for hh in range(HG_HEADS):
                cols = slice(hh * HG_DH, (hh + 1) * HG_DH)
                t = _hg_chunk_terms(f_ref[rows, cols], q_ref[rows, cols], lb_all[:, cols])
                v = i_ref[rows, cols]
                st = st_ref[hh]
                scores = jnp.where(causal, _dot_nt(t["q_dec"], t["k_intra"]), 0.0)
                o_ref[rows, cols] = _dot(scores, v) + _dot_nt(t["q_dec"], st)
                s_ref[ci, hh] = st
                st_ref[hh] = st * t["decay"] + _dot_tn(v, t["k_state"])
            return 0

        lax.fori_loop(0, nch, chunk, 0, unroll=2)
        for hh in range(HG_HEADS):
            cols = slice(hh * HG_DH, (hh + 1) * HG_DH)
            o = o_ref[:, cols]
            gate = gate_ref[:, cols]
            on = o * lax.rsqrt(jnp.mean(o * o, axis=-1, keepdims=True) + RMS_EPS) * ng_ref[:, cols]
            out_ref[:, cols] = (on * (gate * jax.nn.sigmoid(gate))).astype(out_ref.dtype)

    blk = pl.BlockSpec((tb, HG_WIDTH), lambda i: (i, 0))
    return pl.pallas_call(
        body, grid=(T // tb,),
        in_specs=_hg_specs(T, tb, 3, lambda i: i) + [pl.BlockSpec((3, HG_WIDTH), lambda i: (0, 0)), pl.BlockSpec((1, HG_WIDTH), lambda i: (0, 0))],
        out_specs=[blk, blk, pl.BlockSpec((nch, HG_HEADS, HG_DH, HG_DH), lambda i: (i, 0, 0, 0))],
        out_shape=[jax.ShapeDtypeStruct((T, HG_WIDTH), MXU_DTYPE), jax.ShapeDtypeStruct((T, HG_WIDTH), F32),
                   jax.ShapeDtypeStruct((T // HG_CHUNK, HG_HEADS, HG_DH, HG_DH), F32)],
        scratch_shapes=[pltpu.VMEM((HG_HEADS, HG_DH, HG_DH), F32)],
        compiler_params=_params("arbitrary"), name=name,
    )(proj, proj, proj, proj, logits, norm_g)


def _hg_bwd(proj, o_raw, states, dmix, logits, norm_g, *, name):
    T = proj.shape[0]
    tb = min(HG_TOKENS, T)
    nch = tb // HG_CHUNK
    nb = T // tb
    rev = lambda i: nb - 1 - i

    def body(q_ref, f_ref, i_ref, gate_ref, o_ref, s_ref, dout_ref, lg_ref, ng_ref, dp_ref, dlb_ref, dng_ref, do_ref, dst_ref):
        @pl.when(pl.program_id(0) == 0)
        def _():
            dst_ref[...] = jnp.zeros_like(dst_ref)
            dlb_ref[...] = jnp.zeros_like(dlb_ref)
            dng_ref[...] = jnp.zeros_like(dng_ref)

        lb_all = _hg_lower_bound(lg_ref)
        causal = _hg_causal()
        row = _chunk_iota()
        for hh in range(HG_HEADS):
            cols = slice(hh * HG_DH, (hh + 1) * HG_DH)
            o, gate, dout, ng = o_ref[:, cols], gate_ref[:, cols], dout_ref[:, cols], ng_ref[:, cols]
            sg = jax.nn.sigmoid(gate)
            r = lax.rsqrt(jnp.mean(o * o, axis=-1, keepdims=True) + RMS_EPS)
            oh = o * r
            dp_ref[:, 3 * HG_WIDTH + hh * HG_DH:3 * HG_WIDTH + (hh + 1) * HG_DH] = dout * (oh * ng) * (sg * (1.0 + gate * (1.0 - sg)))
            don = dout * (gate * sg)
            dng_ref[:, cols] += jnp.sum(don * oh, axis=0, keepdims=True)
            doh = don * ng
            do_ref[:, cols] = r * (doh - oh * jnp.mean(doh * oh, axis=-1, keepdims=True))

        def chunk(it, _):
            ci = nch - 1 - it
            rows = pl.ds(pl.multiple_of(ci * HG_CHUNK, HG_CHUNK), HG_CHUNK)
            for hh in range(HG_HEADS):
                cols = slice(hh * HG_DH, (hh + 1) * HG_DH)
                lb = lb_all[:, cols]
                t = _hg_chunk_terms(f_ref[rows, cols], q_ref[rows, cols], lb)
                v, do_c, st, dst = i_ref[rows, cols], do_ref[rows, cols], s_ref[ci, hh], dst_ref[hh]
                scores = jnp.where(causal, _dot_nt(t["q_dec"], t["k_intra"]), 0.0)
                dscores = jnp.where(causal, _dot_nt(do_c, v), 0.0)
                dqd = _dot(dscores, t["k_intra"]) + _dot(do_c, st)
                dki = _dot_tn(dscores, t["q_dec"])
                dks = _dot(v, dst)
                dp_ref[rows, 2 * HG_WIDTH + hh * HG_DH:2 * HG_WIDTH + (hh + 1) * HG_DH] = _dot_tn(scores, do_c) + _dot_nt(t["k_state"], dst)
                ddecay = jnp.sum(st * dst, axis=0, keepdims=True)
                dst_ref[hh] = dst * t["decay"] + _dot_tn(do_c, t["q_dec"])
                dks_ks = dks * t["k_state"]
                d_glast = jnp.sum(dks_ks, axis=0, keepdims=True) + ddecay * t["decay"]
                d_g = dqd * t["q_dec"] - dki * t["k_intra"] - dks_ks + jnp.where(row == HG_CHUNK - 1, d_glast, 0.0)
                df = _chunk_cumsum(d_g, reverse=True) / t["f"] - (dki * t["e_ng"] + dks * t["e_lg"])
                dp_ref[rows, hh * HG_DH:(hh + 1) * HG_DH] = dqd * t["e_g"]
                dp_ref[rows, HG_WIDTH + hh * HG_DH:HG_WIDTH + (hh + 1) * HG_DH] = df * (1.0 - lb) * t["sig"] * (1.0 - t["sig"])
                dlb_ref[:, cols] += jnp.sum(df * (1.0 - t["sig"]), axis=0, keepdims=True)
            return 0

        lax.fori_loop(0, nch, chunk, 0, unroll=2)

    blk = pl.BlockSpec((tb, HG_WIDTH), lambda i: (rev(i), 0))
    row_spec = pl.BlockSpec((1, HG_WIDTH), lambda i: (0, 0))
    return pl.pallas_call(
        body, grid=(nb,),
        in_specs=_hg_specs(T, tb, 3, rev) + [
            blk, pl.BlockSpec((nch, HG_HEADS, HG_DH, HG_DH), lambda i: (rev(i), 0, 0, 0)),
            pl.BlockSpec((tb, HG_WIDTH), lambda i: (rev(i), 1)), pl.BlockSpec((3, HG_WIDTH), lambda i: (0, 0)), row_spec],
        out_specs=[pl.BlockSpec((tb, 4 * HG_WIDTH), lambda i: (rev(i), 0)), row_spec, row_spec],
        out_shape=[jax.ShapeDtypeStruct((T, 4 * HG_WIDTH), F32), jax.ShapeDtypeStruct((1, HG_WIDTH), F32), jax.ShapeDtypeStruct((1, HG_WIDTH), F32)],
        scratch_shapes=[pltpu.VMEM((tb, HG_WIDTH), F32), pltpu.VMEM((HG_HEADS, HG_DH, HG_DH), F32)],
        compiler_params=_params("arbitrary"), name=name,
    )(proj, proj, proj, proj, o_raw, states, dmix, logits, norm_g)


def _shifted_copies(sh_ref, n_rows):
    keep = n_rows + CONV_HALO - 8
    for b in range(1, 8):
        sh_ref[b, 0:keep, :] = sh_ref[0, b:b + keep, :]


def _tap_rows(sh_ref, offset, r0, lanes):
    start = pl.multiple_of(r0 + (offset - offset % 8), 8)
    return sh_ref[offset % 8, pl.ds(start, CONV_ROWS), lanes]


def _conv_fwd(p, w_dw, b_dw, ln_g, ln_b, *, name):
    T, D = p.shape[0], p.shape[1] // 2
    tb = _token_block(T)
    hpb = tb // CONV_HALO
    lane_step = 512

    def body(p1_ref, p2_ref, q1_ref, q2_ref, w_ref, bdw_ref, g_ref, b_ref, a_ref, y_ref, act_ref, sh_ref):
        i = pl.program_id(0)
        a = p1_ref[...] * jax.nn.sigmoid(p2_ref[...])
        sh_ref[0, 0:CONV_HALO, :] = jnp.where(i > 0, q1_ref[...] * jax.nn.sigmoid(q2_ref[...]), 0.0)
        sh_ref[0, CONV_HALO:, :] = a
        a_ref[...] = a
        _shifted_copies(sh_ref, tb)

        def chunk(ci, _):
            r0 = pl.multiple_of(ci * CONV_ROWS, CONV_ROWS)
            for l0 in range(0, D, lane_step):
                lanes = slice(l0, l0 + lane_step)
                acc = jnp.broadcast_to(bdw_ref[:, lanes], (CONV_ROWS, lane_step))
                for k in range(CONV_WIDTH):
                    acc = acc + _tap_rows(sh_ref, CONV_HALO - CONV_WIDTH + 1 + k, r0, lanes) * w_ref[k:k + 1, lanes]
                y_ref[pl.ds(r0, CONV_ROWS), lanes] = acc
            return 0

        lax.fori_loop(0, tb // CONV_ROWS, chunk, 0)
        y = y_ref[...]
        mu = jnp.mean(y, axis=-1, keepdims=True)
        yc = y - mu
        s = yc * lax.rsqrt(jnp.mean(yc * yc, axis=-1, keepdims=True) + LN_EPS) * g_ref[...] + b_ref[...]
        act_ref[...] = (s * jax.nn.sigmoid(s)).astype(act_ref.dtype)

    prev = lambda i: jnp.maximum(i * hpb - 1, 0)
    blk = pl.BlockSpec((tb, D), lambda i: (i, 0))
    row = pl.BlockSpec((1, D), lambda i: (0, 0))
    return pl.pallas_call(
        body, grid=(T // tb,),
        in_specs=[blk, pl.BlockSpec((tb, D), lambda i: (i, 1)), pl.BlockSpec((CONV_HALO, D), lambda i: (prev(i), 0)),
                  pl.BlockSpec((CONV_HALO, D), lambda i: (prev(i), 1)), pl.BlockSpec((CONV_HALO, D), lambda i: (0, 0)), row, row, row],
        out_specs=[blk, blk, blk],
        out_shape=[jax.ShapeDtypeStruct((T, D), F32), jax.ShapeDtypeStruct((T, D), F32), jax.ShapeDtypeStruct((T, D), MXU_DTYPE)],
        scratch_shapes=[pltpu.VMEM((8, tb + CONV_HALO, D), F32)],
        compiler_params=_params("parallel"), name=name,
    )(p, p, p, p, w_dw, b_dw, ln_g, ln_b)


def _conv_bwd_norm(dact, y, ln_g, ln_b, *, name):
    T, D = y.shape
    tb = _token_block(T)

    def body(da_ref, y_ref, g_ref, b_ref, dy_ref, dg_ref, db_ref, cs_ref):
        @pl.when(pl.program_id(0) == 0)
        def _():
            dg_ref[...] = jnp.zeros_like(dg_ref)
            db_ref[...] = jnp.zeros_like(db_ref)
            cs_ref[...] = jnp.zeros_like(cs_ref)

        y, g = y_ref[...], g_ref[...]
        yc = y - jnp.mean(y, axis=-1, keepdims=True)
        rs = lax.rsqrt(jnp.mean(yc * yc, axis=-1, keepdims=True) + LN_EPS)
        yn = yc * rs
        s = yn * g + b_ref[...]
        sg = jax.nn.sigmoid(s)
        ds = da_ref[...] * (sg * (1.0 + s * (1.0 - sg)))
        dg_ref[...] += jnp.sum(ds * yn, axis=0, keepdims=True)
        db_ref[...] += jnp.sum(ds, axis=0, keepdims=True)
        dyn = ds * g
        dy = rs * (dyn - jnp.mean(dyn, axis=-1, keepdims=True) - yn * jnp.mean(dyn * yn, axis=-1, keepdims=True))
        dy_ref[...] = dy
        cs_ref[...] += jnp.sum(dy, axis=0, keepdims=True)

    blk = pl.BlockSpec((tb, D), lambda i: (i, 0))
    row = pl.BlockSpec((1, D), lambda i: (0, 0))
    rs_ = jax.ShapeDtypeStruct((1, D), F32)
    return pl.pallas_call(
        body, grid=(T // tb,), in_specs=[blk, blk, row, row], out_specs=[blk, row, row, row],
        out_shape=[jax.ShapeDtypeStruct((T, D), F32), rs_, rs_, rs_], compiler_params=_params("arbitrary"), name=name,
    )(dact, y, ln_g, ln_b)


def _conv_bwd_taps(dy, a, p, w_dw, *, name):
    T, D = dy.shape
    tb = _token_block(T)
    hpb = tb // CONV_HALO
    last = T // CONV_HALO - 1
    nb = T // tb
    lane_step = 128
    groups = CONV_ROWS // 8

    def body(dy_ref, dyn_ref, a_ref, p1_ref, p2_ref, w_ref, dp_ref, dw_ref, cs_ref, sh_ref, da_ref):
        i = pl.program_id(0)

        @pl.when(i == 0)
        def _():
            dw_ref[...] = jnp.zeros_like(dw_ref)
            cs_ref[...] = jnp.zeros_like(cs_ref)

        sh_ref[0, 0:tb, :] = dy_ref[...]
        sh_ref[0, tb:, :] = jnp.where(i < nb - 1, dyn_ref[...], 0.0)
        _shifted_copies(sh_ref, tb)
        for l0 in range(0, D, lane_step):
            lanes = slice(l0, l0 + lane_step)

            def chunk(ci, sums):
                r0 = pl.multiple_of(ci * CONV_ROWS, CONV_ROWS)
                a_c = a_ref[pl.ds(r0, CONV_ROWS), lanes]
                da = jnp.zeros((CONV_ROWS, lane_step), F32)
                new = []
                for k in range(CONV_WIDTH):
                    s_k = _tap_rows(sh_ref, CONV_WIDTH - 1 - k, r0, lanes)
                    da = da + s_k * w_ref[k:k + 1, lanes]
                    new.append(sums[k] + jnp.sum((s_k * a_c).reshape(groups, 8, lane_step), axis=0))
                da_ref[pl.ds(r0, CONV_ROWS), lanes] = da
                return tuple(new)

            sums = lax.fori_loop(0, tb // CONV_ROWS, chunk, tuple(jnp.zeros((8, lane_step), F32) for _ in range(CONV_WIDTH)))
            for k in range(CONV_WIDTH):
                dw_ref[k:k + 1, lanes] += jnp.sum(sums[k], axis=0, keepdims=True)
        da = da_ref[...]
        p1 = p1_ref[...]
        sg = jax.nn.sigmoid(p2_ref[...])
        dp1 = da * sg
        dp2 = da * p1 * (sg * (1.0 - sg))
        dp_ref[:, 0:D] = dp1
        dp_ref[:, D:] = dp2
        cs_ref[:, 0:D] += jnp.sum(dp1, axis=0, keepdims=True)
        cs_ref[:, D:] += jnp.sum(dp2, axis=0, keepdims=True)

    blk = pl.BlockSpec((tb, D), lambda i: (i, 0))
    return pl.pallas_call(
        body, grid=(nb,),
        in_specs=[blk, pl.BlockSpec((CONV_HALO, D), lambda i: (jnp.minimum((i + 1) * hpb, last), 0)), blk, blk,
                  pl.BlockSpec((tb, D), lambda i: (i, 1)), pl.BlockSpec((CONV_HALO, D), lambda i: (0, 0))],
        out_specs=[pl.BlockSpec((tb, 2 * D), lambda i: (i, 0)), pl.BlockSpec((CONV_HALO, D), lambda i: (0, 0)), pl.BlockSpec((1, 2 * D), lambda i: (0, 0))],
        out_shape=[jax.ShapeDtypeStruct((T, 2 * D), F32), jax.ShapeDtypeStruct((CONV_HALO, D), F32), jax.ShapeDtypeStruct((1, 2 * D), F32)],
        scratch_shapes=[pltpu.VMEM((8, tb + CONV_HALO, D), F32), pltpu.VMEM((tb, D), F32)],
        compiler_params=_params("arbitrary"), name=name,
    )(dy, dy, a, p, p, w_dw)


def _row_block(rows):
    for tr in (512, 256, 128, 64, 32, 16, 8):
        if rows % tr == 0:
            return tr
    return rows


def _sum_leading(x, *, name):
    n, R, C = x.shape
    tr = _row_block(R)

    def body(x_ref, o_ref):
        acc = x_ref[0]
        for j in range(1, n):
            acc = acc + x_ref[j]
        o_ref[...] = acc

    return pl.pallas_call(
        body, grid=(R // tr,), in_specs=[pl.BlockSpec((n, tr, C), lambda i: (0, i, 0))], out_specs=pl.BlockSpec((tr, C), lambda i: (i, 0)),
        out_shape=jax.ShapeDtypeStruct((R, C), x.dtype), compiler_params=_params("parallel"), name=name,
    )(x)


def _add_pair(x, y, *, name):
    n, R, C = x.shape
    tr = _row_block(R)

    def body(x_ref, y_ref, o_ref):
        o_ref[...] = x_ref[...] + y_ref[...]

    blk = pl.BlockSpec((1, tr, C), lambda j, i: (j, i, 0))
    return pl.pallas_call(
        body, grid=(n, R // tr), in_specs=[blk, blk], out_specs=blk,
        out_shape=jax.ShapeDtypeStruct((n, R, C), x.dtype), compiler_params=_params("parallel", "parallel"), name=name,
    )(x, y)


def _adamw(w, g, m, v, *, name):
    R, C = w.shape
    tr = _row_block(R)
    c1, c2 = 1.0 - ADAM_B1 ** ADAM_STEP, 1.0 - ADAM_B2 ** ADAM_STEP

    def body(w_ref, g_ref, m_ref, v_ref, d_ref, nm_ref, nv_ref):
        g_ = g_ref[...]
        nm = ADAM_B1 * m_ref[...] + (1.0 - ADAM_B1) * g_
        nv = ADAM_B2 * v_ref[...] + (1.0 - ADAM_B2) * (g_ * g_)
        d_ref[...] = -ADAM_LR * ((nm / c1) / (jnp.sqrt(nv / c2) + ADAM_EPS) + ADAM_WD * w_ref[...])
        nm_ref[...] = nm
        nv_ref[...] = nv

    blk = pl.BlockSpec((tr, C), lambda i: (i, 0))
    shp = jax.ShapeDtypeStruct((R, C), F32)
    return pl.pallas_call(
        body, grid=(R // tr,), in_specs=[blk] * 4, out_specs=[blk] * 3, out_shape=[shp] * 3,
        compiler_params=_params("parallel"), name=name,
    )(w, g, m, v)


def _small_reduce(packs, logits, *, name):
    n, R, C = packs.shape

    def body(p_ref, lg_ref, s_ref, dlg_ref):
        acc = p_ref[0]
        for j in range(1, n):
            acc = acc + p_ref[j]
        s_ref[...] = acc
        lg = lg_ref[...]
        e = jnp.exp(lg - jnp.max(lg, axis=0, keepdims=True))
        sm = e / jnp.sum(e, axis=0, keepdims=True)
        dlb = acc[5:6, HG_WIDTH:2 * HG_WIDTH]
        first = lax.broadcasted_iota(jnp.int32, sm.shape, 0) == 0
        dlg_ref[...] = sm[0:1, :] * (jnp.where(first, 1.0, 0.0) - sm) * dlb

    whole = lambda shape: pl.BlockSpec(shape, lambda: (0,) * len(shape))
    return pl.pallas_call(
        body, in_specs=[whole((n, R, C)), whole(logits.shape)], out_specs=[whole((R, C)), whole(logits.shape)],
        out_shape=[jax.ShapeDtypeStruct((R, C), F32), jax.ShapeDtypeStruct(logits.shape, F32)],
        compiler_params=pltpu.CompilerParams(vmem_limit_bytes=VMEM_LIMIT), name=name,
    )(packs, logits)


HBM_SPEC = pl.BlockSpec(memory_space=pl.ANY)


def _place():
    return lax.axis_index("x"), lax.axis_index("y"), lax.axis_index("c")


class _Copies:
    def __init__(self, arrays, out_shapes, n_copies, make, finish):
        self.arrays, self.out_shapes, self.n_copies, self.make, self.finish = list(arrays), list(out_shapes), n_copies, make, finish

    def scratch(self):
        return [pltpu.SemaphoreType.DMA((self.n_copies,)), pltpu.SemaphoreType.DMA((self.n_copies,))]

    def run(self, name):
        n = len(self.arrays)

        def body(*refs):
            copies = self.make(refs[:n], refs[n:2 * n], *refs[2 * n:])
            for cp in copies:
                cp.start()
            for cp in copies:
                cp.wait()

        outs = pl.pallas_call(body, in_specs=[HBM_SPEC] * n, out_specs=[HBM_SPEC] * n, out_shape=self.out_shapes,
                              scratch_shapes=self.scratch(), name=name)(*self.arrays)
        return self.finish(outs)


def _remote(src, dst, send_sems, recv_sems, k, peer):
    return pltpu.make_async_remote_copy(src_ref=src, dst_ref=dst, send_sem=send_sems.at[k], recv_sem=recv_sems.at[k],
                                        device_id=peer, device_id_type=MESH)


def _same_core_peers(x, y, c):
    return [(1 - x, y, c), (x, 1 - y, c), (1 - x, 1 - y, c)]


def _all_peers(x, y, c):
    flip = lambda v, b: 1 - v if b else v
    return [(flip(x, r & 4), flip(y, r & 2), flip(c, r & 1)) for r in range(1, 8)]


def _gather(arrays, peers_of, slot_of, n_slots):
    n_peers = len(peers_of(0, 0, 0))

    def make(ins, outs, send_sems, recv_sems):
        x, y, c = _place()
        slot = slot_of(x, y, c)
        return [_remote(ins[a], outs[a].at[slot], send_sems, recv_sems, a * n_peers + k, peer)
                for a in range(len(arrays)) for k, peer in enumerate(peers_of(x, y, c))]

    def finish(outs):
        slot = slot_of(*_place())
        return [lax.dynamic_update_index_in_dim(o, a, slot, 0) for o, a in zip(outs, arrays)]

    shapes = [jax.ShapeDtypeStruct((n_slots,) + a.shape, a.dtype) for a in arrays]
    return _Copies(arrays, shapes, len(arrays) * n_peers, make, finish)


def _gather_chips(arrays):
    return _gather(arrays, _same_core_peers, lambda x, y, c: 2 * x + y, N_CHIPS)


def _gather_all(arrays):
    return _gather(arrays, _all_peers, lambda x, y, c: 4 * x + 2 * y + c, N_DEV)


def _pair_swap(a):
    def make(ins, outs, send_sems, recv_sems):
        x, y, c = _place()
        return [_remote(ins[0].at[1 - c], outs[0], send_sems, recv_sems, 0, (x, y, 1 - c))]

    return _Copies([a], [jax.ShapeDtypeStruct(a.shape[1:], a.dtype)], 1, make, lambda outs: outs[0])


def _chip_scatter(p):
    def make(ins, outs, send_sems, recv_sems):
        x, y, c = _place()
        return [_remote(ins[0].at[2 * px + py], outs[0].at[2 * x + y], send_sems, recv_sems, k, (px, py, pc))
                for k, (px, py, pc) in enumerate(_same_core_peers(x, y, c))]

    def finish(outs):
        x, y, _ = _place()
        me = 2 * x + y
        return lax.dynamic_update_index_in_dim(outs[0], lax.dynamic_index_in_dim(p, me, 0, keepdims=False), me, 0)

    return _Copies([p], [jax.ShapeDtypeStruct(p.shape, p.dtype)], 3, make, finish)


def _pair_gather(q):
    def make(ins, outs, send_sems, recv_sems):
        x, y, c = _place()
        return [_remote(ins[0], outs[0].at[c], send_sems, recv_sems, 0, (x, y, 1 - c))]

    return _Copies([q], [jax.ShapeDtypeStruct((2,) + q.shape, q.dtype)], 1, make,
                   lambda outs: lax.dynamic_update_index_in_dim(outs[0], q, _place()[2], 0))


def _cols_from_chips(g):
    g = jnp.moveaxis(g, 0, -2)
    return g.reshape(g.shape[:-2] + (g.shape[-2] * g.shape[-1],))


def _rows_from_chips(g):
    g = jnp.moveaxis(g, 0, -3)
    return g.reshape(g.shape[:-3] + (g.shape[-3] * g.shape[-2], g.shape[-1]))


def _grad_blocks(dw, kind):
    if kind == "cols2d":
        K, N = dw.shape
        b = dw.reshape(2, K // 2, N_CHIPS, N // N_CHIPS).transpose(2, 0, 1, 3)
    elif kind == "rows2d":
        b = dw.reshape(N_CHIPS, 2, dw.shape[0] // 8, dw.shape[1])
    elif kind == "cols3d":
        L, K, N = dw.shape
        b = dw.reshape(L, K, N_CHIPS, N // N_CHIPS).transpose(2, 0, 1, 3)
    else:
        L, K, N = dw.shape
        b = dw.reshape(L, N_CHIPS, K // N_CHIPS, N).transpose(1, 0, 2, 3)
    return b.reshape(N_CHIPS, 2, -1, D_MODEL)


def _pad_rows(a, rows):
    return jnp.concatenate([a, jnp.zeros((rows - a.shape[0],) + a.shape[1:], a.dtype)], axis=0)


def _forward_backward(x, target, W, late_weights=None, early_grads=None):
    row = lambda a: a.reshape(1, -1)
    relu2 = lambda acc: (acc, jnp.square(jnp.maximum(acc, 0.0)))
    residual = lambda acc, res: (res + acc,)
    G = {}

    u0 = _rmsnorm_fwd(x, row(W["norm_mix_g"][0]), name="norm_mix0")
    proj = _matmul(u0, W["w_in"], mode="nn", out_dtypes=[F32], tn=896, name="in_proj")
    qkv = proj[:, :3 * SB_WIDTH].astype(MXU_DTYPE)
    o_sb, got = _sb_fwd(qkv, late_weights and late_weights[0], name="sb_fwd")
    if late_weights:
        W = {**W, **late_weights[1](got)}
    hg_out, hg_o, hg_states = _hg_fwd(proj, W["hg_lb_logits"], row(W["hg_norm_g"]), name="hg_fwd")
    mix = jnp.concatenate([o_sb.astype(MXU_DTYPE), hg_out], axis=-1)
    h1 = _matmul(mix, W["w_out"], mode="nn", out_dtypes=[F32], epilogue=residual, tiles=[x], name="out_proj")
    u1 = _rmsnorm_fwd(h1, row(W["norm_ffn_g"][0]), name="norm_ffn0")
    a0, r0 = _matmul(u1, W["w_ff1"][0], mode="nn", out_dtypes=[F32, MXU_DTYPE], epilogue=relu2, name="ff1_0")
    h2 = _matmul(r0, W["w_ff2"][0], mode="nn", out_dtypes=[F32], epilogue=residual, tiles=[h1], name="ff2_0")
    u2 = _rmsnorm_fwd(h2, row(W["norm_mix_g"][1]), name="norm_mix1")
    p = _matmul(u2, W["w_glu"], mode="nn", out_dtypes=[F32], epilogue=lambda acc, b: (acc + b,), rows=[row(W["b_glu"])], name="glu_proj")
    w_dw = _pad_rows(W["w_dw"], CONV_HALO)
    ca, cy, cact = _conv_fwd(p, w_dw, row(W["b_dw"]), row(W["ln_g"]), row(W["ln_b"]), name="conv_fwd")
    h3 = _matmul(cact, W["w_pw"], mode="nn", out_dtypes=[F32], epilogue=lambda acc, res, b: (res + acc + b,),
                 tiles=[h2], rows=[row(W["b_pw"])], name="pw_proj")
    u3 = _rmsnorm_fwd(h3, row(W["norm_ffn_g"][1]), name="norm_ffn1")
    a1, r1 = _matmul(u3, W["w_ff1"][1], mode="nn", out_dtypes=[F32, MXU_DTYPE], epilogue=relu2, name="ff1_1")
    h4 = _matmul(r1, W["w_ff2"][1], mode="nn", out_dtypes=[F32], epilogue=residual, tiles=[h3], name="ff2_1")

    dh4, G["final_norm_g"], loss = _loss_head(h4, row(W["final_norm_g"]), target, name="loss_head")

    def mlp_bwd(dh, h_in, u, a, r, layer, tag):
        d_relu2 = lambda acc, a_blk: (acc * (2.0 * jnp.maximum(a_blk, 0.0)),)
        da = _matmul(dh, W["w_ff2"][layer], mode="nt", out_dtypes=[MXU_DTYPE], epilogue=d_relu2, tiles=[a], name="d_ff2_act" + tag)
        dw2 = _matmul(r, dh, mode="tn", out_dtypes=[F32], name="d_ff2_w" + tag)
        dw1 = _matmul(u, da, mode="tn", out_dtypes=[F32], name="d_ff1_w" + tag)
        du = _matmul(da, W["w_ff1"][layer], mode="nt", out_dtypes=[F32], name="d_ff1_act" + tag)
        dh_in, dg, cs = _rmsnorm_bwd(du, h_in, row(W["norm_ffn_g"][layer]), dh, name="d_norm_ffn" + tag)
        return dh_in, dg, cs, dw1, dw2

    dh3, dg_ffn1, cs_h3, dw1_1, dw2_1 = mlp_bwd(dh4, h3, u3, a1, r1, 1, "1")
    G["b_pw"] = cs_h3
    dact = _matmul(dh3, W["w_pw"], mode="nt", out_dtypes=[F32], name="d_pw_act")
    G["w_pw"] = _matmul(cact, dh3, mode="tn", out_dtypes=[F32], name="d_pw_w")
    dy, G["ln_g"], G["ln_b"], G["b_dw"] = _conv_bwd_norm(dact, cy, row(W["ln_g"]), row(W["ln_b"]), name="d_conv_norm")
    dp, G["w_dw"], G["b_glu"] = _conv_bwd_taps(dy, ca, p, w_dw, name="d_conv_taps")
    G["w_glu"] = _matmul(u2, dp, mode="tn", out_dtypes=[F32], name="d_glu_w")
    du2 = _matmul(dp, W["w_glu"], mode="nt", out_dtypes=[F32], name="d_glu_act")
    dh2, dg_mix1, _ = _rmsnorm_bwd(du2, h2, row(W["norm_mix_g"][1]), dh3, name="d_norm_mix1")
    dh1, dg_ffn0, _, dw1_0, dw2_0 = mlp_bwd(dh2, h1, u1, a0, r0, 0, "0")
    G["w_ff1"], G["w_ff2"] = jnp.stack([dw1_0, dw1_1]), jnp.stack([dw2_0, dw2_1])
    G["norm_ffn_g"] = jnp.concatenate([dg_ffn0, dg_ffn1], axis=0)
    dmix = _matmul(dh1, W["w_out"], mode="nt", out_dtypes=[F32], name="d_out_act")
    G["w_out"] = _matmul(mix, dh1, mode="tn", out_dtypes=[F32], name="d_out_w")
    riding = early_grads(G) if early_grads else None
    (dsq, dsk, dsv), got = _sb_bwd(qkv, dmix, riding, name="sb_bwd")
    d_hg, G["hg_lb"], G["hg_norm_g"] = _hg_bwd(proj, hg_o, hg_states, dmix, W["hg_lb_logits"], row(W["hg_norm_g"]), name="hg_bwd")
    dproj = jnp.concatenate([dsq, dsk, dsv, d_hg], axis=-1).astype(MXU_DTYPE)
    G["w_in"] = _matmul(u0, dproj, mode="tn", out_dtypes=[F32], tn=896, name="d_in_w")
    du0 = _matmul(dproj, W["w_in"], mode="nt", out_dtypes=[F32], tk=896, name="d_in_act")
    dx, dg_mix0, _ = _rmsnorm_bwd(du0, x, row(W["norm_mix_g"][0]), dh1, name="d_norm_mix0")
    G["norm_mix_g"] = jnp.concatenate([dg_mix0, dg_mix1], axis=0)
    return loss, dx, G, (riding, got)


BIG = (("w_out_ab", "w_out", "rows2d"), ("conv_w_glu", "w_glu", "cols2d"), ("conv_w_pw", "w_pw", "rows2d"),
       ("w_ff1", "w_ff1", "cols3d"), ("w_ff2", "w_ff2", "rows3d"), ("w_in_ab", "w_in", "cols2d"))
LATE = BIG[:-1]
SMALL_SHARDED = ("conv_b_glu", "conv_w_dw", "conv_b_dw", "conv_ln_g", "conv_ln_b", "conv_b_pw")
REPLICATED = ("norm_mix_g", "norm_ffn_g", "hg_lb_logits", "hg_norm_g", "final_norm_g")
ORDER = ("norm_mix_g", "norm_ffn_g", "w_in_ab", "w_out_ab", "hg_lb_logits", "hg_norm_g", "conv_w_glu", "conv_b_glu",
         "conv_w_dw", "conv_b_dw", "conv_ln_g", "conv_ln_b", "conv_w_pw", "conv_b_pw", "w_ff1", "w_ff2", "final_norm_g")


def _step(x, loss_target, w, m, v):
    D = D_MODEL
    x2, t2 = x.reshape(-1, D), loss_target.reshape(-1, D)
    chip = 2 * lax.axis_index("x") + lax.axis_index("y")
    c = lax.axis_index("c")

    small_in = jnp.concatenate([w["conv_b_glu"].reshape(2, 256), w["conv_w_dw"].reshape(CONV_WIDTH, 256)] +
                               [w[n].reshape(1, 256) for n in ("conv_b_dw", "conv_ln_g", "conv_ln_b", "conv_b_pw")], axis=0)
    g_in, gs = _gather_chips([w["w_in_ab"].astype(MXU_DTYPE), _pad_rows(small_in, 40)]).run("gather_first_weights")
    vec = lambda r0, r1: gs[:, r0:r1].transpose(1, 0, 2).reshape(r1 - r0, N_CHIPS * 256)
    W = {
        "w_in": _cols_from_chips(g_in[:, 0]),
        "b_glu": gs[:, 0:2].reshape(2 * D), "w_dw": vec(2, 33), "b_dw": vec(33, 34)[0], "ln_g": vec(34, 35)[0],
        "ln_b": vec(35, 36)[0], "b_pw": vec(36, 37)[0],
        "norm_mix_g": w["norm_mix_g"], "norm_ffn_g": w["norm_ffn_g"], "hg_lb_logits": w["hg_lb_logits"],
        "hg_norm_g": w["hg_norm_g"], "final_norm_g": w["final_norm_g"],
    }
    late = _gather_chips([w[n].astype(MXU_DTYPE) for n, _, _ in LATE])

    def assemble(got):
        gw = dict(zip([s for _, s, _ in LATE], late.finish(got)))
        return {"w_out": _rows_from_chips(gw["w_out"][:, 0]), "w_glu": _cols_from_chips(gw["w_glu"][:, 0]),
                "w_pw": _rows_from_chips(gw["w_pw"][:, 0]), "w_ff1": _cols_from_chips(gw["w_ff1"]), "w_ff2": _rows_from_chips(gw["w_ff2"])}

    def pair_sum(blocks, tag):
        from_pair = _pair_swap(blocks.transpose(1, 0, 2, 3)).run("grads_pair_swap_" + tag)
        mine = lax.dynamic_index_in_dim(blocks, c, axis=1, keepdims=False)
        return _add_pair(mine, from_pair, name="grads_pair_add_" + tag)

    def early_grads(G):
        blocks = jnp.concatenate([_grad_blocks(G[s], kind) for _, s, kind in LATE], axis=2)
        return _chip_scatter(pair_sum(blocks, "late"))

    loss, dx, G, (riding, got) = _forward_backward(x2, t2, W, (late, assemble), early_grads)
    half_late = _sum_leading(riding.finish(got), name="grads_chip_add_late")
    per_chip = _chip_scatter(pair_sum(_grad_blocks(G["w_in"], "cols2d"), "in")).run("grads_chip_scatter_in")
    half = jnp.concatenate([half_late, _sum_leading(per_chip, name="grads_chip_add_in")], axis=0)
    full = _pair_gather(half).run("grads_pair_gather")

    pack = jnp.concatenate([
        G["norm_mix_g"], G["norm_ffn_g"], G["final_norm_g"], jnp.concatenate([G["hg_norm_g"], G["hg_lb"]], axis=1),
        _pad_rows(jnp.broadcast_to(loss, (1, D)), 2), G["b_glu"].reshape(2, D), G["w_dw"], G["b_dw"], G["ln_g"], G["ln_b"], G["b_pw"],
    ], axis=0)
    pack = _pad_rows(pack, SMALL_ROWS)
    (packs,) = _gather_all([pack]).run("gather_small_grads")
    ssum, d_logits = _small_reduce(packs, w["hg_lb_logits"], name="reduce_small_grads")
    cut = lambda r0, r1: lax.dynamic_slice(ssum, (r0, chip * 256), (r1 - r0, 256))
    grads = {
        "norm_mix_g": ssum[0:2], "norm_ffn_g": ssum[2:4], "final_norm_g": ssum[4], "hg_norm_g": ssum[5, :HG_WIDTH].reshape(1, HG_HEADS, HG_DH),
        "hg_lb_logits": d_logits,
        "conv_b_glu": lax.dynamic_slice(ssum[8:10].reshape(1, 2 * D), (0, chip * 512), (1, 512)),
        "conv_w_dw": cut(10, 10 + CONV_WIDTH).reshape(1, CONV_WIDTH, 256),
        "conv_b_dw": cut(42, 43), "conv_ln_g": cut(43, 44), "conv_ln_b": cut(44, 45), "conv_b_pw": cut(45, 46),
    }
    loss_out = ssum[6, 0]

    off = 0
    for n, s, kind in BIG:
        shard = w[n].shape
        rows = w[n].size // (2 * D)
        grads[n] = full[:, off:off + rows].reshape(shard)
        off += rows

    delta, new_m, new_v = {}, {}, {}
    for n, _, _ in BIG:
        view = lambda a: a.reshape(-1, a.shape[-1])
        outs = _adamw(view(w[n]), view(grads[n]), view(m[n]), view(v[n]), name="adamw_" + n)
        delta[n], new_m[n], new_v[n] = (o.reshape(w[n].shape) for o in outs)
    small = SMALL_SHARDED + REPLICATED
    sizes = [w[n].size for n in small]
    total = sum(sizes)
    rows = -(-total // (8 * D)) * 8
    packed = lambda d: _pad_rows(jnp.concatenate([d[n].reshape(-1) for n in small]).reshape(-1, 128), rows * 8).reshape(rows, D)
    outs = _adamw(packed(w), packed(grads), packed(m), packed(v), name="adamw_small")
    off = 0
    for n, size in zip(small, sizes):
        delta[n], new_m[n], new_v[n] = (o.reshape(-1)[off:off + size].reshape(w[n].shape) for o in outs)
        off += size
    grads = {n: grads[n].reshape(w[n].shape) for n in ORDER}
    return (loss_out, dx.reshape(x.shape), *[grads[n] for n in ORDER], *[delta[n] for n in ORDER],
            *[new_m[n] for n in ORDER], *[new_v[n] for n in ORDER])


def kernel(x, norm_mix_g, norm_ffn_g, w_in_ab, w_out_ab, hg_lb_logits, hg_norm_g, conv_w_glu, conv_b_glu, conv_w_dw, conv_b_dw, conv_ln_g, conv_ln_b, conv_w_pw, conv_b_pw, w_ff1, w_ff2, final_norm_g, loss_target, m_norm_mix_g, m_norm_ffn_g, m_w_in_ab, m_w_out_ab, m_hg_lb_logits, m_hg_norm_g, m_conv_w_glu, m_conv_b_glu, m_conv_w_dw, m_conv_b_dw, m_conv_ln_g, m_conv_ln_b, m_conv_w_pw, m_conv_b_pw, m_w_ff1, m_w_ff2, m_final_norm_g, v_norm_mix_g, v_norm_ffn_g, v_w_in_ab, v_w_out_ab, v_hg_lb_logits, v_hg_norm_g, v_conv_w_glu, v_conv_b_glu, v_conv_w_dw, v_conv_b_dw, v_conv_ln_g, v_conv_ln_b, v_conv_w_pw, v_conv_b_pw, v_w_ff1, v_w_ff2, v_final_norm_g):
    args = locals()
    w = {n: args[n] for n in ORDER}
    m = {n: args["m_" + n] for n in ORDER}
    v = {n: args["v_" + n] for n in ORDER}
    return _step(x, loss_target, w, m, v)
```

```python
import functools

import jax
import jax.numpy as jnp
from jax import lax
from jax.experimental import pallas as pl
from jax.experimental.pallas import tpu as pltpu

F32 = jnp.float32
MXU_DTYPE = jnp.bfloat16
MESH = pl.DeviceIdType.MESH

D_MODEL = 1024
SB_HEADS, SB_DH, SB_WIDTH = 8, 64, 512
SB_KEYS = 512
SB_SUB = 256
SB_ROWS_FWD, SB_ROWS_BWD = 512, 256
HG_HEADS, HG_DH, HG_WIDTH = 4, 128, 512
HG_CHUNK = 16
HG_TOKENS = 256
IN_WIDTH = 3 * SB_WIDTH + 4 * HG_WIDTH
CONV_WIDTH = 31
CONV_HALO = 32
CONV_ROWS = 32
D_FF = 4096
RMS_EPS = 1e-6
LN_EPS = 1e-5
N_CHIPS = 4
N_DEV = 8
SMALL_ROWS = 48
VMEM_LIMIT = 56 * 1024 * 1024

ADAM_LR, ADAM_B1, ADAM_B2, ADAM_EPS, ADAM_WD, ADAM_STEP = 0.001, 0.9, 0.999, 1e-08, 0.01, 10


def _params(*sem):
    return pltpu.CompilerParams(dimension_semantics=sem, vmem_limit_bytes=VMEM_LIMIT)


def _mx(v):
    return v.astype(MXU_DTYPE)


def _dot(a, b):
    return jnp.dot(_mx(a), _mx(b), preferred_element_type=F32)


def _dot_nt(a, b):
    return lax.dot_general(_mx(a), _mx(b), (((1,), (1,)), ((), ())), preferred_element_type=F32)


def _dot_tn(a, b):
    return lax.dot_general(_mx(a), _mx(b), (((0,), (0,)), ((), ())), preferred_element_type=F32)


def _neg_abs(x):
    bits = lax.bitcast_convert_type(x, jnp.uint32) | jnp.uint32(0x80000000)
    return lax.bitcast_convert_type(bits, F32)


def _key_order_sums(v, tri2, later):
    hi = _mx(v)
    lo = _mx(v - hi.astype(F32))
    n = SB_KEYS // SB_SUB
    blocks = [slice(b * SB_SUB, (b + 1) * SB_SUB) for b in range(n)]
    totals = [jnp.sum(v[:, sl], axis=1, keepdims=True) for sl in blocks]
    sums = []
    for b, sl in enumerate(blocks):
        s = jnp.dot(jnp.concatenate([hi[:, sl], lo[:, sl]], axis=1), tri2, preferred_element_type=F32)
        for o in (range(b + 1, n) if later else range(b)):
            s = s + totals[o]
        sums.append(s)
    return jnp.concatenate(sums, axis=1), functools.reduce(lambda a, b: a + b, totals)


def _matmul(a, b, *, mode, out_dtypes, epilogue=None, tiles=(), rows=(), tm=1024, tn=1024, tk=1024, name):
    if mode == "nn":
        (M, K), N = a.shape, b.shape[1]
    elif mode == "nt":
        (M, K), N = a.shape, b.shape[0]
    else:
        (K, M), N = a.shape, b.shape[1]
    tm, tn, tk = min(tm, M), min(tn, N), min(tk, K)
    assert M % tm == 0 and N % tn == 0 and K % tk == 0, (name, M, N, K)
    nk = K // tk
    a_spec = pl.BlockSpec((tk, tm), lambda i, j, k: (k, i)) if mode == "tn" else pl.BlockSpec((tm, tk), lambda i, j, k: (i, k))
    b_spec = pl.BlockSpec((tn, tk), lambda i, j, k: (j, k)) if mode == "nt" else pl.BlockSpec((tk, tn), lambda i, j, k: (k, j))
    dims = {"nn": ((1,), (0,)), "nt": ((1,), (1,)), "tn": ((0,), (0,))}[mode]
    n_t, n_r, n_o = len(tiles), len(rows), len(out_dtypes)
    if epilogue is None:
        epilogue = lambda acc: (acc,)

    def body(a_ref, b_ref, *rest):
        extra, outs, acc_ref = rest[:n_t + n_r], rest[n_t + n_r:n_t + n_r + n_o], rest[-1]
        k = pl.program_id(2)

        @pl.when(k == 0)
        def _():
            acc_ref[...] = jnp.zeros_like(acc_ref)

        acc_ref[...] += lax.dot_general(_mx(a_ref[...]), _mx(b_ref[...]), (dims, ((), ())), preferred_element_type=F32)

        @pl.when(k == nk - 1)
        def _():
            res = epilogue(acc_ref[...], *[e[...] for e in extra])
            for o_ref, r in zip(outs, res):
                o_ref[...] = r.astype(o_ref.dtype)

    tile_spec = pl.BlockSpec((tm, tn), lambda i, j, k: (i, j))
    row_spec = pl.BlockSpec((1, tn), lambda i, j, k: (0, j))
    outs = pl.pallas_call(
        body, grid=(M // tm, N // tn, nk),
        in_specs=[a_spec, b_spec] + [tile_spec] * n_t + [row_spec] * n_r,
        out_specs=[tile_spec] * n_o,
        out_shape=[jax.ShapeDtypeStruct((M, N), dt) for dt in out_dtypes],
        scratch_shapes=[pltpu.VMEM((tm, tn), F32)],
        compiler_params=_params("parallel", "parallel", "arbitrary"), name=name,
    )(a, b, *tiles, *rows)
    return outs[0] if n_o == 1 else outs


def _token_block(T):
    return min(512, T)


def _rmsnorm_fwd(h, g, *, name):
    T, D = h.shape
    tb = _token_block(T)

    def body(h_ref, g_ref, u_ref):
        x = h_ref[...]
        r = lax.rsqrt(jnp.mean(x * x, axis=-1, keepdims=True) + RMS_EPS)
        u_ref[...] = (x * r * g_ref[...]).astype(u_ref.dtype)

    blk = pl.BlockSpec((tb, D), lambda i: (i, 0))
    return pl.pallas_call(
        body, grid=(T // tb,), in_specs=[blk, pl.BlockSpec((1, D), lambda i: (0, 0))], out_specs=blk,
        out_shape=jax.ShapeDtypeStruct((T, D), MXU_DTYPE), compiler_params=_params("parallel"), name=name,
    )(h, g)


def _rms_bwd_math(x, g, du):
    r = lax.rsqrt(jnp.mean(x * x, axis=-1, keepdims=True) + RMS_EPS)
    gd = g * du
    dx = r * gd - x * (r * r * r) * jnp.mean(gd * x, axis=-1, keepdims=True)
    return dx, du * x * r


def _rmsnorm_bwd(du, h, g, dres, *, name):
    T, D = h.shape
    tb = _token_block(T)

    def body(du_ref, h_ref, g_ref, dres_ref, dh_ref, dhm_ref, dg_ref, cs_ref):
        @pl.when(pl.program_id(0) == 0)
        def _():
            dg_ref[...] = jnp.zeros_like(dg_ref)
            cs_ref[...] = jnp.zeros_like(cs_ref)

        dx, dg_terms = _rms_bwd_math(h_ref[...], g_ref[...], du_ref[...])
        dh = dres_ref[...] + dx
        dh_ref[...] = dh
        dhm_ref[...] = dh.astype(dhm_ref.dtype)
        dg_ref[...] += jnp.sum(dg_terms, axis=0, keepdims=True)
        cs_ref[...] += jnp.sum(dh, axis=0, keepdims=True)

    blk = pl.BlockSpec((tb, D), lambda i: (i, 0))
    row = pl.BlockSpec((1, D), lambda i: (0, 0))
    return pl.pallas_call(
        body, grid=(T // tb,), in_specs=[blk, blk, row, blk], out_specs=[blk, blk, row, row],
        out_shape=[jax.ShapeDtypeStruct((T, D), F32), jax.ShapeDtypeStruct((T, D), MXU_DTYPE), jax.ShapeDtypeStruct((1, D), F32),
                   jax.ShapeDtypeStruct((1, D), F32)],
        compiler_params=_params("arbitrary"), name=name,
    )(du, h, g, dres)


def _loss_head(h, g, target, *, name):
    T, D = h.shape
    tb = _token_block(T)

    def body(h_ref, g_ref, t_ref, dh_ref, dhm_ref, dg_ref, loss_ref):
        @pl.when(pl.program_id(0) == 0)
        def _():
            dg_ref[...] = jnp.zeros_like(dg_ref)
            loss_ref[...] = jnp.zeros_like(loss_ref)

        x, gg = h_ref[...], g_ref[...]
        r = lax.rsqrt(jnp.mean(x * x, axis=-1, keepdims=True) + RMS_EPS)
        diff = x * r * gg - t_ref[...]
        per_token = jnp.mean(diff * diff, axis=-1, keepdims=True)
        loss_ref[...] += 0.5 * jnp.sum(per_token, axis=0, keepdims=True)
        dx, dg_terms = _rms_bwd_math(x, gg, diff / D)
        dh_ref[...] = dx
        dhm_ref[...] = dx.astype(dhm_ref.dtype)
        dg_ref[...] += jnp.sum(dg_terms, axis=0, keepdims=True)

    blk = pl.BlockSpec((tb, D), lambda i: (i, 0))
    row = pl.BlockSpec((1, D), lambda i: (0, 0))
    return pl.pallas_call(
        body, grid=(T // tb,), in_specs=[blk, row, blk], out_specs=[blk, blk, row, pl.BlockSpec((1, 1), lambda i: (0, 0))],
        out_shape=[jax.ShapeDtypeStruct((T, D), F32), jax.ShapeDtypeStruct((T, D), MXU_DTYPE), jax.ShapeDtypeStruct((1, D), F32),
                   jax.ShapeDtypeStruct((1, 1), F32)],
        compiler_params=_params("arbitrary"), name=name,
    )(h, g, target)


def _sb_scores(qm, ks, later, tri, mask):
    z = _dot_nt(qm, ks)
    sp = jnp.maximum(z, 0.0) + jnp.log(1.0 + jnp.exp(_neg_abs(z)))
    lb = z - sp
    if mask is not None:
        sp = jnp.where(mask, sp, 0.0)
    after, total = _key_order_sums(sp, tri, later=True)
    w = jnp.exp(lb - (after + later))
    if mask is not None:
        w = jnp.where(mask, w, 0.0)
    return total, lb, w


def _sb_setup(q_ref, rows):
    i, hsel = pl.program_id(1), pl.program_id(2)
    lane = lax.broadcasted_iota(jnp.int32, (rows, 2 * SB_DH), 1)
    mine = (lane >= SB_DH) == (hsel == 1)
    diag = (i * rows) // SB_KEYS
    t = i * rows + lax.broadcasted_iota(jnp.int32, (rows, SB_KEYS), 0)
    s = diag * SB_KEYS + lax.broadcasted_iota(jnp.int32, (rows, SB_KEYS), 1)
    a = lax.broadcasted_iota(jnp.int32, (2 * SB_SUB, SB_SUB), 0) % SB_SUB
    b = lax.broadcasted_iota(jnp.int32, (2 * SB_SUB, SB_SUB), 1)
    return i, hsel, mine, diag, s < t, _mx(a > b), _mx(a < b)


def _sb_keys(j):
    return pl.ds(pl.multiple_of(j * SB_KEYS, SB_KEYS), SB_KEYS)


def _sb_descend(n, step, carry):
    carry = lax.fori_loop(0, n // 2, lambda it, cr: step(n - 2 - 2 * it, step(n - 1 - 2 * it, cr)), carry)
    return lax.cond(n % 2 == 1, lambda cr: step(0, cr), lambda cr: cr, carry)


def _sb_ascend(n, step, carry):
    odd = n % 2
    carry = lax.cond(odd == 1, lambda cr: step(0, cr), lambda cr: cr, carry)
    return lax.fori_loop(0, n // 2, lambda it, cr: step(odd + 2 * it + 1, step(odd + 2 * it, cr)), carry)


def _sb_call(body, qkv, extra_in, out_blocks, scratch, rider, rows, *, name):
    T = qkv.shape[0]
    n_pairs = SB_HEADS // 2
    grid = (n_pairs, T // rows, 2)
    pair = lambda col0: pl.BlockSpec((rows, 2 * SB_DH), lambda p, i, h: (i, col0 + p))
    whole = lambda col0: pl.BlockSpec((T, 2 * SB_DH), lambda p, i, h: (0, col0 + p))
    shape = jax.ShapeDtypeStruct((T, SB_WIDTH), F32)
    in_specs = [pair(0), whole(n_pairs), whole(2 * n_pairs)] + [pair(0)] * len(extra_in)
    out_specs = [pair(0) if kind == "pair" else whole(0) for kind in out_blocks]
    n_in, n_out, n_r = len(in_specs), len(out_specs), 0 if rider is None else len(rider.arrays)

    def kernel_body(*refs):
        ins, r_in = refs[:n_in], refs[n_in:n_in + n_r]
        outs, r_out = refs[n_in + n_r:n_in + n_r + n_out], refs[n_in + n_r + n_out:n_in + 2 * n_r + n_out]
        rest = refs[n_in + 2 * n_r + n_out:]
        ids = [pl.program_id(a) for a in range(3)]
        if rider is not None:
            @pl.when((ids[0] == 0) & (ids[1] == 0) & (ids[2] == 0))
            def _():
                for cp in rider.make(r_in, r_out, *rest[len(scratch):]):
                    cp.start()

        body(ins, outs, rest[:len(scratch)])
        if rider is not None:
            @pl.when((ids[0] == grid[0] - 1) & (ids[1] == grid[1] - 1) & (ids[2] == grid[2] - 1))
            def _():
                for cp in rider.make(r_in, r_out, *rest[len(scratch):]):
                    cp.wait()

    res = pl.pallas_call(
        kernel_body, grid=grid, in_specs=in_specs + [HBM_SPEC] * n_r, out_specs=out_specs + [HBM_SPEC] * n_r,
        out_shape=[shape] * n_out + ([] if rider is None else rider.out_shapes),
        scratch_shapes=list(scratch) + ([] if rider is None else rider.scratch()),
        compiler_params=_params("arbitrary", "arbitrary", "arbitrary"), name=name,
    )(qkv, qkv, qkv, *extra_in, *([] if rider is None else rider.arrays))
    return res[:n_out], res[n_out:]


def _sb_fwd(qkv, rider=None, *, name):
    rows = min(SB_ROWS_FWD, qkv.shape[0])
    scale = SB_DH ** -0.5

    def body(ins, outs, _):
        (q_ref, k_ref, v_ref), (o_ref,) = ins, outs
        i, hsel, mine, diag, mask, tri, _ = _sb_setup(q_ref, rows)
        qm = jnp.where(mine, q_ref[...], 0) * scale

        def tile(j, m, later, acc):
            total, _, w = _sb_scores(qm, k_ref[_sb_keys(j), :], later, tri, m)
            return later + total, acc + _dot(w, v_ref[_sb_keys(j), :])

        carry = tile(diag, mask, jnp.zeros((rows, 1), F32), jnp.zeros((rows, 2 * SB_DH), F32))
        _, acc = _sb_descend(diag, lambda j, cr: tile(j, None, *cr), carry)
        res = jnp.where(mine, acc, 0.0)

        @pl.when(hsel == 0)
        def _():
            o_ref[...] = res

        @pl.when(hsel == 1)
        def _():
            o_ref[...] += res

    (o,), got = _sb_call(body, qkv, [], ["pair"], [], rider, rows, name=name)
    return o, got


def _sb_bwd(qkv, dmix, rider=None, *, name):
    T = qkv.shape[0]
    rows = min(SB_ROWS_BWD, T)
    scale = SB_DH ** -0.5

    def body(ins, outs, scratch):
        (q_ref, k_ref, v_ref, do_ref), (dq_ref, dk_ref, dv_ref), (da_ref, beta_ref) = ins, outs, scratch
        i, hsel, mine, diag, mask, tri, tri_before = _sb_setup(q_ref, rows)

        @pl.when((i == 0) & (hsel == 0))
        def _():
            dk_ref[...] = jnp.zeros_like(dk_ref)
            dv_ref[...] = jnp.zeros_like(dv_ref)

        qm = jnp.where(mine, q_ref[...], 0) * scale
        do_m = _mx(jnp.where(mine, do_ref[...], 0.0))

        def weights(j, m, later):
            total, lb, w = _sb_scores(qm, k_ref[_sb_keys(j), :], later, tri, m)
            da_ref[j] = _dot_nt(do_m, v_ref[_sb_keys(j), :]) * w
            beta_ref[j] = jnp.exp(lb)
            dv_ref[_sb_keys(j), :] += _dot_tn(w, do_m)
            return later + total

        later = weights(diag, mask, jnp.zeros((rows, 1), F32))
        _sb_descend(diag, lambda j, c: weights(j, None, c), later)

        def logits(j, m, before, dq):
            da = da_ref[j]
            earlier, total = _key_order_sums(da, tri_before, later=False)
            dz = da - beta_ref[j] * (da + earlier + before)
            if m is not None:
                dz = jnp.where(m, dz, 0.0)
            dz = _mx(dz)
            dk_ref[_sb_keys(j), :] += _dot_tn(dz, qm)
            return before + total, dq + _dot(dz, k_ref[_sb_keys(j), :])

        carry = (jnp.zeros((rows, 1), F32), jnp.zeros((rows, 2 * SB_DH), F32))
        carry = _sb_ascend(diag, lambda j, cr: logits(j, None, *cr), carry)
        res = jnp.where(mine, logits(diag, mask, *carry)[1] * scale, 0.0)

        @pl.when(hsel == 0)
        def _():
            dq_ref[...] = res

        @pl.when(hsel == 1)
        def _():
            dq_ref[...] += res

    n_tiles = T // SB_KEYS
    scratch = [pltpu.VMEM((n_tiles, rows, SB_KEYS), F32), pltpu.VMEM((n_tiles, rows, SB_KEYS), F32)]
    return _sb_call(body, qkv, [dmix], ["pair", "whole", "whole"], scratch, rider, rows, name=name)


def _chunk_iota():
    return lax.broadcasted_iota(jnp.int32, (HG_CHUNK, HG_DH), 0)


def _chunk_cumsum(x, reverse=False):
    row = _chunk_iota()
    for sh in (1, 2, 4, 8):
        if reverse:
            x = x + jnp.where(row < HG_CHUNK - sh, pltpu.roll(x, HG_CHUNK - sh, 0), 0.0)
        else:
            x = x + jnp.where(row >= sh, pltpu.roll(x, sh, 0), 0.0)
    return x


def _hg_lower_bound(logits_ref):
    lg = logits_ref[...]
    e = jnp.exp(lg - jnp.max(lg, axis=0, keepdims=True))
    return e[0:1, :] / jnp.sum(e, axis=0, keepdims=True)


def _hg_chunk_terms(fr, q, lb):
    sig = jax.nn.sigmoid(fr)
    f = lb + (1.0 - lb) * sig
    kk = 1.0 - f
    G = _chunk_cumsum(jnp.log(f))
    g_last = G[HG_CHUNK - 1:HG_CHUNK, :]
    e_g, e_ng, e_lg = jnp.exp(G), jnp.exp(-G), jnp.exp(g_last - G)
    return dict(sig=sig, f=f, kk=kk, e_g=e_g, e_ng=e_ng, e_lg=e_lg, q_dec=q * e_g, k_intra=kk * e_ng,
                k_state=kk * e_lg, decay=jnp.exp(g_last))


def _hg_causal():
    c = lax.broadcasted_iota(jnp.int32, (HG_CHUNK, HG_CHUNK), 0)
    s = lax.broadcasted_iota(jnp.int32, (HG_CHUNK, HG_CHUNK), 1)
    return s <= c


def _hg_specs(T, tb, col0, order):
    return [pl.BlockSpec((tb, HG_WIDTH), functools.partial(lambda i, j: (order(i), j), j=col0 + j)) for j in range(4)]


def _hg_fwd(proj, logits, norm_g, *, name):
    T = proj.shape[0]
    tb = min(HG_TOKENS, T)
    nch = tb // HG_CHUNK

    def body(q_ref, f_ref, i_ref, gate_ref, lg_ref, ng_ref, out_ref, o_ref, s_ref, st_ref):
        @pl.when(pl.program_id(0) == 0)
        def _():
            st_ref[...] = jnp.zeros_like(st_ref)

        lb_all = _hg_lower_bound(lg_ref)
        causal = _hg_causal()

        def chunk(ci, _):
            rows = pl.ds(pl.multiple_of(ci * HG_CHUNK, HG_CHUNK), HG_CHUNK)
            for hh in range(HG_HEADS):
                cols = slice(hh * HG_DH, (hh + 1) * HG_DH)
                t = _hg_chunk_terms(f_ref[rows, cols], q_ref[rows, cols], lb_all[:, cols])
                v = i_ref[rows, cols]
                st = st_ref[hh]
                scores = jnp.where(causal, _dot_nt(t["q_dec"], t["k_intra"]), 0.0)
                o_ref[rows, cols] = _dot(scores, v) + _dot_nt(t["q_dec"], st)
                s_ref[ci, hh] = st
                st_ref[hh] = st * t["decay"] + _dot_tn(v, t["k_state"])
            return 0

        lax.fori_loop(0, nch, chunk, 0, unroll=2)
        for hh in range(HG_HEADS):
            cols = slice(hh * HG_DH, (hh + 1) * HG_DH)
            o = o_ref[:, cols]
            gate = gate_ref[:, cols]
            on = o * lax.rsqrt(jnp.mean(o * o, axis=-1, keepdims=True) + RMS_EPS) * ng_ref[:, cols]
            out_ref[:, cols] = (on * (gate * jax.nn.sigmoid(gate))).astype(out_ref.dtype)

    blk = pl.BlockSpec((tb, HG_WIDTH), lambda i: (i, 0))
    return pl.pallas_call(
        body, grid=(T // tb,),
        in_specs=_hg_specs(T, tb, 3, lambda i: i) + [pl.BlockSpec((3, HG_WIDTH), lambda i: (0, 0)), pl.BlockSpec((1, HG_WIDTH), lambda i: (0, 0))],
        out_specs=[blk, blk, pl.BlockSpec((nch, HG_HEADS, HG_DH, HG_DH), lambda i: (i, 0, 0, 0))],
        out_shape=[jax.ShapeDtypeStruct((T, HG_WIDTH), MXU_DTYPE), jax.ShapeDtypeStruct((T, HG_WIDTH), F32),
                   jax.ShapeDtypeStruct((T // HG_CHUNK, HG_HEADS, HG_DH, HG_DH), F32)],
        scratch_shapes=[pltpu.VMEM((HG_HEADS, HG_DH, HG_DH), F32)],
        compiler_params=_params("arbitrary"), name=name,
    )(proj, proj, proj, proj, logits, norm_g)


def _hg_bwd(proj, o_raw, states, dmix, logits, norm_g, *, name):
    T = proj.shape[0]
    tb = min(HG_TOKENS, T)
    nch = tb // HG_CHUNK
    nb = T // tb
    rev = lambda i: nb - 1 - i

    def body(q_ref, f_ref, i_ref, gate_ref, o_ref, s_ref, dout_ref, lg_ref, ng_ref, dp_ref, dlb_ref, dng_ref, do_ref, dst_ref):
        @pl.when(pl.program_id(0) == 0)
        def _():
            dst_ref[...] = jnp.zeros_like(dst_ref)
            dlb_ref[...] = jnp.zeros_like(dlb_ref)
            dng_ref[...] = jnp.zeros_like(dng_ref)

        lb_all = _hg_lower_bound(lg_ref)
        causal = _hg_causal()
        row = _chunk_iota()
        for hh in range(HG_HEADS):
            cols = slice(hh * HG_DH, (hh + 1) * HG_DH)
            o, gate, dout, ng = o_ref[:, cols], gate_ref[:, cols], dout_ref[:, cols], ng_ref[:, cols]
            sg = jax.nn.sigmoid(gate)
            r = lax.rsqrt(jnp.mean(o * o, axis=-1, keepdims=True) + RMS_EPS)
            oh = o * r
            dp_ref[:, 3 * HG_WIDTH + hh * HG_DH:3 * HG_WIDTH + (hh + 1) * HG_DH] = dout * (oh * ng) * (sg * (1.0 + gate * (1.0 - sg)))
            don = dout * (gate * sg)
            dng_ref[:, cols] += jnp.sum(don * oh, axis=0, keepdims=True)
            doh = don * ng
            do_ref[:, cols] = r * (doh - oh * jnp.mean(doh * oh, axis=-1, keepdims=True))

        def chunk(it, _):
            ci = nch - 1 - it
            rows = pl.ds(pl.multiple_of(ci * HG_CHUNK, HG_CHUNK), HG_CHUNK)
            for hh in range(HG_HEADS):
                cols = slice(hh * HG_DH, (hh + 1) * HG_DH)
                lb = lb_all[:, cols]
                t = _hg_chunk_terms(f_ref[rows, cols], q_ref[rows, cols], lb)
                v, do_c, st, dst = i_ref[rows, cols], do_ref[rows, cols], s_ref[ci, hh], dst_ref[hh]
                scores = jnp.where(causal, _dot_nt(t["q_dec"], t["k_intra"]), 0.0)
                dscores = jnp.where(causal, _dot_nt(do_c, v), 0.0)
                dqd = _dot(dscores, t["k_intra"]) + _dot(do_c, st)
                dki = _dot_tn(dscores, t["q_dec"])
                dks = _dot(v, dst)
                dp_ref[rows, 2 * HG_WIDTH + hh * HG_DH:2 * HG_WIDTH + (hh + 1) * HG_DH] = _dot_tn(scores, do_c) + _dot_nt(t["k_state"], dst)
                ddecay = jnp.sum(st * dst, axis=0, keepdims=True)
                dst_ref[hh] = dst * t["decay"] + _dot_tn(do_c, t["q_dec"])
                dks_ks = dks * t["k_state"]
                d_glast = jnp.sum(dks_ks, axis=0, keepdims=True) + ddecay * t["decay"]
                d_g = dqd * t["q_dec"] - dki * t["k_intra"] - dks_ks + jnp.where(row == HG_CHUNK - 1, d_glast, 0.0)
                df = _chunk_cumsum(d_g, reverse=True) / t["f"] - (dki * t["e_ng"] + dks * t["e_lg"])
                dp_ref[rows, hh * HG_DH:(hh + 1) * HG_DH] = dqd * t["e_g"]
                dp_ref[rows, HG_WIDTH + hh * HG_DH:HG_WIDTH + (hh + 1) * HG_DH] = df * (1.0 - lb) * t["sig"] * (1.0 - t["sig"])
                dlb_ref[:, cols] += jnp.sum(df * (1.0 - t["sig"]), axis=0, keepdims=True)
            return 0

        lax.fori_loop(0, nch, chunk, 0, unroll=2)

    blk = pl.BlockSpec((tb, HG_WIDTH), lambda i: (rev(i), 0))
    row_spec = pl.BlockSpec((1, HG_WIDTH), lambda i: (0, 0))
    return pl.pallas_call(
        body, grid=(nb,),
        in_specs=_hg_specs(T, tb, 3, rev) + [
            blk, pl.BlockSpec((nch, HG_HEADS, HG_DH, HG_DH), lambda i: (rev(i), 0, 0, 0)),
            pl.BlockSpec((tb, HG_WIDTH), lambda i: (rev(i), 1)), pl.BlockSpec((3, HG_WIDTH), lambda i: (0, 0)), row_spec],
        out_specs=[pl.BlockSpec((tb, 4 * HG_WIDTH), lambda i: (rev(i), 0)), row_spec, row_spec],
        out_shape=[jax.ShapeDtypeStruct((T, 4 * HG_WIDTH), F32), jax.ShapeDtypeStruct((1, HG_WIDTH), F32), jax.ShapeDtypeStruct((1, HG_WIDTH), F32)],
        scratch_shapes=[pltpu.VMEM((tb, HG_WIDTH), F32), pltpu.VMEM((HG_HEADS, HG_DH, HG_DH), F32)],
        compiler_params=_params("arbitrary"), name=name,
    )(proj, proj, proj, proj, o_raw, states, dmix, logits, norm_g)


def _shifted_copies(sh_ref, n_rows):
    keep = n_rows + CONV_HALO - 8
    for b in range(1, 8):
        sh_ref[b, 0:keep, :] = sh_ref[0, b:b + keep, :]


def _tap_rows(sh_ref, offset, r0, lanes):
    start = pl.multiple_of(r0 + (offset - offset % 8), 8)
    return sh_ref[offset % 8, pl.ds(start, CONV_ROWS), lanes]


def _conv_fwd(p, w_dw, b_dw, ln_g, ln_b, *, name):
    T, D = p.shape[0], p.shape[1] // 2
    tb = _token_block(T)
    hpb = tb // CONV_HALO
    lane_step = 512

    def body(p1_ref, p2_ref, q1_ref, q2_ref, w_ref, bdw_ref, g_ref, b_ref, a_ref, y_ref, act_ref, sh_ref):
        i = pl.program_id(0)
        a = p1_ref[...] * jax.nn.sigmoid(p2_ref[...])
        sh_ref[0, 0:CONV_HALO, :] = jnp.where(i > 0, q1_ref[...] * jax.nn.sigmoid(q2_ref[...]), 0.0)
        sh_ref[0, CONV_HALO:, :] = a
        a_ref[...] = a
        _shifted_copies(sh_ref, tb)

        def chunk(ci, _):
            r0 = pl.multiple_of(ci * CONV_ROWS, CONV_ROWS)
            for l0 in range(0, D, lane_step):
                lanes = slice(l0, l0 + lane_step)
                acc = jnp.broadcast_to(bdw_ref[:, lanes], (CONV_ROWS, lane_step))
                for k in range(CONV_WIDTH):
                    acc = acc + _tap_rows(sh_ref, CONV_HALO - CONV_WIDTH + 1 + k, r0, lanes) * w_ref[k:k + 1, lanes]
                y_ref[pl.ds(r0, CONV_ROWS), lanes] = acc
            return 0

        lax.fori_loop(0, tb // CONV_ROWS, chunk, 0)
        y = y_ref[...]
        mu = jnp.mean(y, axis=-1, keepdims=True)
        yc = y - mu
        s = yc * lax.rsqrt(jnp.mean(yc * yc, axis=-1, keepdims=True) + LN_EPS) * g_ref[...] + b_ref[...]
        act_ref[...] = (s * jax.nn.sigmoid(s)).astype(act_ref.dtype)

    prev = lambda i: jnp.maximum(i * hpb - 1, 0)
    blk = pl.BlockSpec((tb, D), lambda i: (i, 0))
    row = pl.BlockSpec((1, D), lambda i: (0, 0))
    return pl.pallas_call(
        body, grid=(T // tb,),
        in_specs=[blk, pl.BlockSpec((tb, D), lambda i: (i, 1)), pl.BlockSpec((CONV_HALO, D), lambda i: (prev(i), 0)),
                  pl.BlockSpec((CONV_HALO, D), lambda i: (prev(i), 1)), pl.BlockSpec((CONV_HALO, D), lambda i: (0, 0)), row, row, row],
        out_specs=[blk, blk, blk],
        out_shape=[jax.ShapeDtypeStruct((T, D), F32), jax.ShapeDtypeStruct((T, D), F32), jax.ShapeDtypeStruct((T, D), MXU_DTYPE)],
        scratch_shapes=[pltpu.VMEM((8, tb + CONV_HALO, D), F32)],
        compiler_params=_params("parallel"), name=name,
    )(p, p, p, p, w_dw, b_dw, ln_g, ln_b)


def _conv_bwd_norm(dact, y, ln_g, ln_b, *, name):
    T, D = y.shape
    tb = _token_block(T)

    def body(da_ref, y_ref, g_ref, b_ref, dy_ref, dg_ref, db_ref, cs_ref):
        @pl.when(pl.program_id(0) == 0)
        def _():
            dg_ref[...] = jnp.zeros_like(dg_ref)
            db_ref[...] = jnp.zeros_like(db_ref)
            cs_ref[...] = jnp.zeros_like(cs_ref)

        y, g = y_ref[...], g_ref[...]
        yc = y - jnp.mean(y, axis=-1, keepdims=True)
        rs = lax.rsqrt(jnp.mean(yc * yc, axis=-1, keepdims=True) + LN_EPS)
        yn = yc * rs
        s = yn * g + b_ref[...]
        sg = jax.nn.sigmoid(s)
        ds = da_ref[...] * (sg * (1.0 + s * (1.0 - sg)))
        dg_ref[...] += jnp.sum(ds * yn, axis=0, keepdims=True)
        db_ref[...] += jnp.sum(ds, axis=0, keepdims=True)
        dyn = ds * g
        dy = rs * (dyn - jnp.mean(dyn, axis=-1, keepdims=True) - yn * jnp.mean(dyn * yn, axis=-1, keepdims=True))
        dy_ref[...] = dy
        cs_ref[...] += jnp.sum(dy, axis=0, keepdims=True)

    blk = pl.BlockSpec((tb, D), lambda i: (i, 0))
    row = pl.BlockSpec((1, D), lambda i: (0, 0))
    rs_ = jax.ShapeDtypeStruct((1, D), F32)
    return pl.pallas_call(
        body, grid=(T // tb,), in_specs=[blk, blk, row, row], out_specs=[blk, row, row, row],
        out_shape=[jax.ShapeDtypeStruct((T, D), F32), rs_, rs_, rs_], compiler_params=_params("arbitrary"), name=name,
    )(dact, y, ln_g, ln_b)


def _conv_bwd_taps(dy, a, p, w_dw, *, name):
    T, D = dy.shape
    tb = _token_block(T)
    hpb = tb // CONV_HALO
    last = T // CONV_HALO - 1
    nb = T // tb
    lane_step = 128
    groups = CONV_ROWS // 8

    def body(dy_ref, dyn_ref, a_ref, p1_ref, p2_ref, w_ref, dp_ref, dw_ref, cs_ref, sh_ref, da_ref):
        i = pl.program_id(0)

        @pl.when(i == 0)
        def _():
            dw_ref[...] = jnp.zeros_like(dw_ref)
            cs_ref[...] = jnp.zeros_like(cs_ref)

        sh_ref[0, 0:tb, :] = dy_ref[...]
        sh_ref[0, tb:, :] = jnp.where(i < nb - 1, dyn_ref[...], 0.0)
        _shifted_copies(sh_ref, tb)
        for l0 in range(0, D, lane_step):
            lanes = slice(l0, l0 + lane_step)

            def chunk(ci, sums):
                r0 = pl.multiple_of(ci * CONV_ROWS, CONV_ROWS)
                a_c = a_ref[pl.ds(r0, CONV_ROWS), lanes]
                da = jnp.zeros((CONV_ROWS, lane_step), F32)
                new = []
                for k in range(CONV_WIDTH):
                    s_k = _tap_rows(sh_ref, CONV_WIDTH - 1 - k, r0, lanes)
                    da = da + s_k * w_ref[k:k + 1, lanes]
                    new.append(sums[k] + jnp.sum((s_k * a_c).reshape(groups, 8, lane_step), axis=0))
                da_ref[pl.ds(r0, CONV_ROWS), lanes] = da
                return tuple(new)

            sums = lax.fori_loop(0, tb // CONV_ROWS, chunk, tuple(jnp.zeros((8, lane_step), F32) for _ in range(CONV_WIDTH)))
            for k in range(CONV_WIDTH):
                dw_ref[k:k + 1, lanes] += jnp.sum(sums[k], axis=0, keepdims=True)
        da = da_ref[...]
        p1 = p1_ref[...]
        sg = jax.nn.sigmoid(p2_ref[...])
        dp1 = da * sg
        dp2 = da * p1 * (sg * (1.0 - sg))
        dp_ref[:, 0:D] = dp1.astype(dp_ref.dtype)
        dp_ref[:, D:] = dp2.astype(dp_ref.dtype)
        cs_ref[:, 0:D] += jnp.sum(dp1, axis=0, keepdims=True)
        cs_ref[:, D:] += jnp.sum(dp2, axis=0, keepdims=True)

    blk = pl.BlockSpec((tb, D), lambda i: (i, 0))
    return pl.pallas_call(
        body, grid=(nb,),
        in_specs=[blk, pl.BlockSpec((CONV_HALO, D), lambda i: (jnp.minimum((i + 1) * hpb, last), 0)), blk, blk,
                  pl.BlockSpec((tb, D), lambda i: (i, 1)), pl.BlockSpec((CONV_HALO, D), lambda i: (0, 0))],
        out_specs=[pl.BlockSpec((tb, 2 * D), lambda i: (i, 0)), pl.BlockSpec((CONV_HALO, D), lambda i: (0, 0)), pl.BlockSpec((1, 2 * D), lambda i: (0, 0))],
        out_shape=[jax.ShapeDtypeStruct((T, 2 * D), MXU_DTYPE), jax.ShapeDtypeStruct((CONV_HALO, D), F32), jax.ShapeDtypeStruct((1, 2 * D), F32)],
        scratch_shapes=[pltpu.VMEM((8, tb + CONV_HALO, D), F32), pltpu.VMEM((tb, D), F32)],
        compiler_params=_params("arbitrary"), name=name,
    )(dy, dy, a, p, p, w_dw)


def _row_block(rows):
    for tr in (512, 256, 128, 64, 32, 16, 8):
        if rows % tr == 0:
            return tr
    return rows


def _sum_leading(x, *, name):
    n, R, C = x.shape
    tr = _row_block(R)

    def body(x_ref, o_ref):
        acc = x_ref[0]
        for j in range(1, n):
            acc = acc + x_ref[j]
        o_ref[...] = acc

    return pl.pallas_call(
        body, grid=(R // tr,), in_specs=[pl.BlockSpec((n, tr, C), lambda i: (0, i, 0))], out_specs=pl.BlockSpec((tr, C), lambda i: (i, 0)),
        out_shape=jax.ShapeDtypeStruct((R, C), x.dtype), compiler_params=_params("parallel"), name=name,
    )(x)


def _add_pair(x, y, *, name):
    n, R, C = x.shape
    tr = _row_block(R)

    def body(x_ref, y_ref, o_ref):
        o_ref[...] = x_ref[...] + y_ref[...]

    blk = pl.BlockSpec((1, tr, C), lambda j, i: (j, i, 0))
    return pl.pallas_call(
        body, grid=(n, R // tr), in_specs=[blk, blk], out_specs=blk,
        out_shape=jax.ShapeDtypeStruct((n, R, C), x.dtype), compiler_params=_params("parallel", "parallel"), name=name,
    )(x, y)


def _adamw(w, g, m, v, *, name):
    R, C = w.shape
    tr = _row_block(R)
    c1, c2 = 1.0 - ADAM_B1 ** ADAM_STEP, 1.0 - ADAM_B2 ** ADAM_STEP

    def body(w_ref, g_ref, m_ref, v_ref, d_ref, nm_ref, nv_ref):
        g_ = g_ref[...]
        nm = ADAM_B1 * m_ref[...] + (1.0 - ADAM_B1) * g_
        nv = ADAM_B2 * v_ref[...] + (1.0 - ADAM_B2) * (g_ * g_)
        d_ref[...] = -ADAM_LR * ((nm / c1) / (jnp.sqrt(nv / c2) + ADAM_EPS) + ADAM_WD * w_ref[...])
        nm_ref[...] = nm
        nv_ref[...] = nv

    blk = pl.BlockSpec((tr, C), lambda i: (i, 0))
    shp = jax.ShapeDtypeStruct((R, C), F32)
    return pl.pallas_call(
        body, grid=(R // tr,), in_specs=[blk] * 4, out_specs=[blk] * 3, out_shape=[shp] * 3,
        compiler_params=_params("parallel"), name=name,
    )(w, g, m, v)


def _small_reduce(packs, logits, *, name):
    n, R, C = packs.shape

    def body(p_ref, lg_ref, s_ref, dlg_ref):
        acc = p_ref[0]
        for j in range(1, n):
            acc = acc + p_ref[j]
        s_ref[...] = acc
        lg = lg_ref[...]
        e = jnp.exp(lg - jnp.max(lg, axis=0, keepdims=True))
        sm = e / jnp.sum(e, axis=0, keepdims=True)
        dlb = acc[5:6, HG_WIDTH:2 * HG_WIDTH]
        first = lax.broadcasted_iota(jnp.int32, sm.shape, 0) == 0
        dlg_ref[...] = sm[0:1, :] * (jnp.where(first, 1.0, 0.0) - sm) * dlb

    whole = lambda shape: pl.BlockSpec(shape, lambda: (0,) * len(shape))
    return pl.pallas_call(
        body, in_specs=[whole((n, R, C)), whole(logits.shape)], out_specs=[whole((R, C)), whole(logits.shape)],
        out_shape=[jax.ShapeDtypeStruct((R, C), F32), jax.ShapeDtypeStruct(logits.shape, F32)],
        compiler_params=pltpu.CompilerParams(vmem_limit_bytes=VMEM_LIMIT), name=name,
    )(packs, logits)


HBM_SPEC = pl.BlockSpec(memory_space=pl.ANY)


def _place():
    return lax.axis_index("x"), lax.axis_index("y"), lax.axis_index("c")


class _Copies:
    def __init__(self, arrays, out_shapes, n_copies, make, finish):
        self.arrays, self.out_shapes, self.n_copies, self.make, self.finish = list(arrays), list(out_shapes), n_copies, make, finish

    def scratch(self):
        return [pltpu.SemaphoreType.DMA((self.n_copies,)), pltpu.SemaphoreType.DMA((self.n_copies,))]

    def run(self, name):
        n = len(self.arrays)

        def body(*refs):
            copies = self.make(refs[:n], refs[n:2 * n], *refs[2 * n:])
            for cp in copies:
                cp.start()
            for cp in copies:
                cp.wait()

        outs = pl.pallas_call(body, in_specs=[HBM_SPEC] * n, out_specs=[HBM_SPEC] * n, out_shape=self.out_shapes,
                              scratch_shapes=self.scratch(), name=name)(*self.arrays)
        return self.finish(outs)


def _remote(src, dst, send_sems, recv_sems, k, peer):
    return pltpu.make_async_remote_copy(src_ref=src, dst_ref=dst, send_sem=send_sems.at[k], recv_sem=recv_sems.at[k],
                                        device_id=peer, device_id_type=MESH)


def _same_core_peers(x, y, c):
    return [(1 - x, y, c), (x, 1 - y, c), (1 - x, 1 - y, c)]


def _all_peers(x, y, c):
    flip = lambda v, b: 1 - v if b else v
    return [(flip(x, r & 4), flip(y, r & 2), flip(c, r & 1)) for r in range(1, 8)]


def _gather(arrays, peers_of, slot_of, n_slots):
    n_peers = len(peers_of(0, 0, 0))

    def make(ins, outs, send_sems, recv_sems):
        x, y, c = _place()
        slot = slot_of(x, y, c)
        return [_remote(ins[a], outs[a].at[slot], send_sems, recv_sems, a * n_peers + k, peer)
                for a in range(len(arrays)) for k, peer in enumerate(peers_of(x, y, c))]

    def finish(outs):
        slot = slot_of(*_place())
        return [lax.dynamic_update_index_in_dim(o, a, slot, 0) for o, a in zip(outs, arrays)]

    shapes = [jax.ShapeDtypeStruct((n_slots,) + a.shape, a.dtype) for a in arrays]
    return _Copies(arrays, shapes, len(arrays) * n_peers, make, finish)


def _gather_chips(arrays):
    return _gather(arrays, _same_core_peers, lambda x, y, c: 2 * x + y, N_CHIPS)


def _gather_all(arrays):
    return _gather(arrays, _all_peers, lambda x, y, c: 4 * x + 2 * y + c, N_DEV)


def _pair_swap(a):
    def make(ins, outs, send_sems, recv_sems):
        x, y, c = _place()
        return [_remote(ins[0].at[1 - c], outs[0], send_sems, recv_sems, 0, (x, y, 1 - c))]

    return _Copies([a], [jax.ShapeDtypeStruct(a.shape[1:], a.dtype)], 1, make, lambda outs: outs[0])


def _chip_scatter(p):
    def make(ins, outs, send_sems, recv_sems):
        x, y, c = _place()
        return [_remote(ins[0].at[2 * px + py], outs[0].at[2 * x + y], send_sems, recv_sems, k, (px, py, pc))
                for k, (px, py, pc) in enumerate(_same_core_peers(x, y, c))]

    def finish(outs):
        x, y, _ = _place()
        me = 2 * x + y
        return lax.dynamic_update_index_in_dim(outs[0], lax.dynamic_index_in_dim(p, me, 0, keepdims=False), me, 0)

    return _Copies([p], [jax.ShapeDtypeStruct(p.shape, p.dtype)], 3, make, finish)


def _pair_gather(q):
    def make(ins, outs, send_sems, recv_sems):
        x, y, c = _place()
        return [_remote(ins[0], outs[0].at[c], send_sems, recv_sems, 0, (x, y, 1 - c))]

    return _Copies([q], [jax.ShapeDtypeStruct((2,) + q.shape, q.dtype)], 1, make,
                   lambda outs: lax.dynamic_update_index_in_dim(outs[0], q, _place()[2], 0))


def _cols_from_chips(g):
    g = jnp.moveaxis(g, 0, -2)
    return g.reshape(g.shape[:-2] + (g.shape[-2] * g.shape[-1],))


def _rows_from_chips(g):
    g = jnp.moveaxis(g, 0, -3)
    return g.reshape(g.shape[:-3] + (g.shape[-3] * g.shape[-2], g.shape[-1]))


def _grad_blocks(dw, kind):
    if kind == "cols2d":
        K, N = dw.shape
        b = dw.reshape(2, K // 2, N_CHIPS, N // N_CHIPS).transpose(2, 0, 1, 3)
    elif kind == "rows2d":
        b = dw.reshape(N_CHIPS, 2, dw.shape[0] // 8, dw.shape[1])
    elif kind == "cols3d":
        L, K, N = dw.shape
        b = dw.reshape(L, K, N_CHIPS, N // N_CHIPS).transpose(2, 0, 1, 3)
    else:
        L, K, N = dw.shape
        b = dw.reshape(L, N_CHIPS, K // N_CHIPS, N).transpose(1, 0, 2, 3)
    return b.reshape(N_CHIPS, 2, -1, D_MODEL)


def _pad_rows(a, rows):
    return jnp.concatenate([a, jnp.zeros((rows - a.shape[0],) + a.shape[1:], a.dtype)], axis=0)


def _forward_backward(x, target, W, late_weights=None, early_grads=None):
    row = lambda a: a.reshape(1, -1)
    relu2 = lambda acc: (jnp.square(jnp.maximum(acc, 0.0)),)
    residual = lambda acc, res: (res + acc,)
    G = {}

    u0 = _rmsnorm_fwd(x, row(W["norm_mix_g"][0]), name="norm_mix0")
    proj = _matmul(u0, W["w_in"], mode="nn", out_dtypes=[F32], tn=896, name="in_proj")
    qkv = proj[:, :3 * SB_WIDTH].astype(MXU_DTYPE)
    o_sb, got = _sb_fwd(qkv, late_weights and late_weights[0], name="sb_fwd")
    if late_weights:
        W = {**W, **late_weights[1](got)}
    hg_out, hg_o, hg_states = _hg_fwd(proj, W["hg_lb_logits"], row(W["hg_norm_g"]), name="hg_fwd")
    mix = jnp.concatenate([o_sb.astype(MXU_DTYPE), hg_out], axis=-1)
    h1 = _matmul(mix, W["w_out"], mode="nn", out_dtypes=[F32], epilogue=residual, tiles=[x], name="out_proj")
    u1 = _rmsnorm_fwd(h1, row(W["norm_ffn_g"][0]), name="norm_ffn0")
    r0 = _matmul(u1, W["w_ff1"][0], mode="nn", out_dtypes=[MXU_DTYPE], epilogue=relu2, name="ff1_0")
    h2 = _matmul(r0, W["w_ff2"][0], mode="nn", out_dtypes=[F32], epilogue=residual, tiles=[h1], name="ff2_0")
    u2 = _rmsnorm_fwd(h2, row(W["norm_mix_g"][1]), name="norm_mix1")
    p = _matmul(u2, W["w_glu"], mode="nn", out_dtypes=[F32], epilogue=lambda acc, b: (acc + b,), rows=[row(W["b_glu"])], name="glu_proj")
    w_dw = _pad_rows(W["w_dw"], CONV_HALO)
    ca, cy, cact = _conv_fwd(p, w_dw, row(W["b_dw"]), row(W["ln_g"]), row(W["ln_b"]), name="conv_fwd")
    h3 = _matmul(cact, W["w_pw"], mode="nn", out_dtypes=[F32], epilogue=lambda acc, res, b: (res + acc + b,),
                 tiles=[h2], rows=[row(W["b_pw"])], name="pw_proj")
    u3 = _rmsnorm_fwd(h3, row(W["norm_ffn_g"][1]), name="norm_ffn1")
    r1 = _matmul(u3, W["w_ff1"][1], mode="nn", out_dtypes=[MXU_DTYPE], epilogue=relu2, name="ff1_1")
    h4 = _matmul(r1, W["w_ff2"][1], mode="nn", out_dtypes=[F32], epilogue=residual, tiles=[h3], name="ff2_1")

    dh4, dh4_m, G["final_norm_g"], loss = _loss_head(h4, row(W["final_norm_g"]), target, name="loss_head")

    def mlp_bwd(dh, dh_m, h_in, u, r, layer, tag):
        d_relu2 = lambda acc, r_blk: (acc * (2.0 * jnp.sqrt(r_blk.astype(F32))),)
        da = _matmul(dh_m, W["w_ff2"][layer], mode="nt", out_dtypes=[MXU_DTYPE], epilogue=d_relu2, tiles=[r], name="d_ff2_act" + tag)
        dw2 = _matmul(r, dh_m, mode="tn", out_dtypes=[F32], name="d_ff2_w" + tag)
        dw1 = _matmul(u, da, mode="tn", out_dtypes=[F32], name="d_ff1_w" + tag)
        du = _matmul(da, W["w_ff1"][layer], mode="nt", out_dtypes=[F32], name="d_ff1_act" + tag)
        dh_in, dh_in_m, dg, cs = _rmsnorm_bwd(du, h_in, row(W["norm_ffn_g"][layer]), dh, name="d_norm_ffn" + tag)
        return dh_in, dh_in_m, dg, cs, dw1, dw2

    dh3, dh3_m, dg_ffn1, cs_h3, dw1_1, dw2_1 = mlp_bwd(dh4, dh4_m, h3, u3, r1, 1, "1")
    G["b_pw"] = cs_h3
    dact = _matmul(dh3_m, W["w_pw"], mode="nt", out_dtypes=[F32], name="d_pw_act")
    G["w_pw"] = _matmul(cact, dh3_m, mode="tn", out_dtypes=[F32], name="d_pw_w")
    dy, G["ln_g"], G["ln_b"], G["b_dw"] = _conv_bwd_norm(dact, cy, row(W["ln_g"]), row(W["ln_b"]), name="d_conv_norm")
    dp, G["w_dw"], G["b_glu"] = _conv_bwd_taps(dy, ca, p, w_dw, name="d_conv_taps")
    G["w_glu"] = _matmul(u2, dp, mode="tn", out_dtypes=[F32], name="d_glu_w")
    du2 = _matmul(dp, W["w_glu"], mode="nt", out_dtypes=[F32], name="d_glu_act")
    dh2, dh2_m, dg_mix1, _ = _rmsnorm_bwd(du2, h2, row(W["norm_mix_g"][1]), dh3, name="d_norm_mix1")
    dh1, dh1_m, dg_ffn0, _, dw1_0, dw2_0 = mlp_bwd(dh2, dh2_m, h1, u1, r0, 0, "0")
    G["w_ff1"], G["w_ff2"] = jnp.stack([dw1_0, dw1_1]), jnp.stack([dw2_0, dw2_1])
    G["norm_ffn_g"] = jnp.concatenate([dg_ffn0, dg_ffn1], axis=0)
    dmix = _matmul(dh1_m, W["w_out"], mode="nt", out_dtypes=[F32], name="d_out_act")
    G["w_out"] = _matmul(mix, dh1_m, mode="tn", out_dtypes=[F32], name="d_out_w")
    riding = early_grads(G) if early_grads else None
    (dsq, dsk, dsv), got = _sb_bwd(qkv, dmix, riding, name="sb_bwd")
    d_hg, G["hg_lb"], G["hg_norm_g"] = _hg_bwd(proj, hg_o, hg_states, dmix, W["hg_lb_logits"], row(W["hg_norm_g"]), name="hg_bwd")
    dproj = jnp.concatenate([dsq, dsk, dsv, d_hg], axis=-1).astype(MXU_DTYPE)
    G["w_in"] = _matmul(u0, dproj, mode="tn", out_dtypes=[F32], tn=896, name="d_in_w")
    du0 = _matmul(dproj, W["w_in"], mode="nt", out_dtypes=[F32], tk=896, name="d_in_act")
    dx, _, dg_mix0, _ = _rmsnorm_bwd(du0, x, row(W["norm_mix_g"][0]), dh1, name="d_norm_mix0")
    G["norm_mix_g"] = jnp.concatenate([dg_mix0, dg_mix1], axis=0)
    return loss, dx, G, (riding, got)


BIG = (("w_out_ab", "w_out", "rows2d"), ("conv_w_glu", "w_glu", "cols2d"), ("conv_w_pw", "w_pw", "rows2d"),
       ("w_ff1", "w_ff1", "cols3d"), ("w_ff2", "w_ff2", "rows3d"), ("w_in_ab", "w_in", "cols2d"))
LATE = BIG[:-1]
SMALL_SHARDED = ("conv_b_glu", "conv_w_dw", "conv_b_dw", "conv_ln_g", "conv_ln_b", "conv_b_pw")
REPLICATED = ("norm_mix_g", "norm_ffn_g", "hg_lb_logits", "hg_norm_g", "final_norm_g")
ORDER = ("norm_mix_g", "norm_ffn_g", "w_in_ab", "w_out_ab", "hg_lb_logits", "hg_norm_g", "conv_w_glu", "conv_b_glu",
         "conv_w_dw", "conv_b_dw", "conv_ln_g", "conv_ln_b", "conv_w_pw", "conv_b_pw", "w_ff1", "w_ff2", "final_norm_g")


def _step(x, loss_target, w, m, v):
    D = D_MODEL
    x2, t2 = x.reshape(-1, D), loss_target.reshape(-1, D)
    chip = 2 * lax.axis_index("x") + lax.axis_index("y")
    c = lax.axis_index("c")

    small_in = jnp.concatenate([w["conv_b_glu"].reshape(2, 256), w["conv_w_dw"].reshape(CONV_WIDTH, 256)] +
                               [w[n].reshape(1, 256) for n in ("conv_b_dw", "conv_ln_g", "conv_ln_b", "conv_b_pw")], axis=0)
    g_in, gs = _gather_chips([w["w_in_ab"].astype(MXU_DTYPE), _pad_rows(small_in, 40)]).run("gather_first_weights")
    vec = lambda r0, r1: gs[:, r0:r1].transpose(1, 0, 2).reshape(r1 - r0, N_CHIPS * 256)
    W = {
        "w_in": _cols_from_chips(g_in[:, 0]),
        "b_glu": gs[:, 0:2].reshape(2 * D), "w_dw": vec(2, 33), "b_dw": vec(33, 34)[0], "ln_g": vec(34, 35)[0],
        "ln_b": vec(35, 36)[0], "b_pw": vec(36, 37)[0],
        "norm_mix_g": w["norm_mix_g"], "norm_ffn_g": w["norm_ffn_g"], "hg_lb_logits": w["hg_lb_logits"],
        "hg_norm_g": w["hg_norm_g"], "final_norm_g": w["final_norm_g"],
    }
    late = _gather_chips([w[n].astype(MXU_DTYPE) for n, _, _ in LATE])

    def assemble(got):
        gw = dict(zip([s for _, s, _ in LATE], late.finish(got)))
        return {"w_out": _rows_from_chips(gw["w_out"][:, 0]), "w_glu": _cols_from_chips(gw["w_glu"][:, 0]),
                "w_pw": _rows_from_chips(gw["w_pw"][:, 0]), "w_ff1": _cols_from_chips(gw["w_ff1"]), "w_ff2": _rows_from_chips(gw["w_ff2"])}

    def pair_sum(blocks, tag):
        from_pair = _pair_swap(blocks.transpose(1, 0, 2, 3)).run("grads_pair_swap_" + tag)
        mine = lax.dynamic_index_in_dim(blocks, c, axis=1, keepdims=False)
        return _add_pair(mine, from_pair, name="grads_pair_add_" + tag)

    def early_grads(G):
        blocks = jnp.concatenate([_grad_blocks(G[s], kind) for _, s, kind in LATE], axis=2)
        return _chip_scatter(pair_sum(blocks, "late"))

    loss, dx, G, (riding, got) = _forward_backward(x2, t2, W, (late, assemble), early_grads)
    half_late = _sum_leading(riding.finish(got), name="grads_chip_add_late")
    per_chip = _chip_scatter(pair_sum(_grad_blocks(G["w_in"], "cols2d"), "in")).run("grads_chip_scatter_in")
    half = jnp.concatenate([half_late, _sum_leading(per_chip, name="grads_chip_add_in")], axis=0)
    full = _pair_gather(half).run("grads_pair_gather")

    pack = jnp.concatenate([
        G["norm_mix_g"], G["norm_ffn_g"], G["final_norm_g"], jnp.concatenate([G["hg_norm_g"], G["hg_lb"]], axis=1),
        _pad_rows(jnp.broadcast_to(loss, (1, D)), 2), G["b_glu"].reshape(2, D), G["w_dw"], G["b_dw"], G["ln_g"], G["ln_b"], G["b_pw"],
    ], axis=0)
    pack = _pad_rows(pack, SMALL_ROWS)
    (packs,) = _gather_all([pack]).run("gather_small_grads")
    ssum, d_logits = _small_reduce(packs, w["hg_lb_logits"], name="reduce_small_grads")
    cut = lambda r0, r1: lax.dynamic_slice(ssum, (r0, chip * 256), (r1 - r0, 256))
    grads = {
        "norm_mix_g": ssum[0:2], "norm_ffn_g": ssum[2:4], "final_norm_g": ssum[4], "hg_norm_g": ssum[5, :HG_WIDTH].reshape(1, HG_HEADS, HG_DH),
        "hg_lb_logits": d_logits,
        "conv_b_glu": lax.dynamic_slice(ssum[8:10].reshape(1, 2 * D), (0, chip * 512), (1, 512)),
        "conv_w_dw": cut(10, 10 + CONV_WIDTH).reshape(1, CONV_WIDTH, 256),
        "conv_b_dw": cut(42, 43), "conv_ln_g": cut(43, 44), "conv_ln_b": cut(44, 45), "conv_b_pw": cut(45, 46),
    }
    loss_out = ssum[6, 0]

    off = 0
    for n, s, kind in BIG:
        shard = w[n].shape
        rows = w[n].size // (2 * D)
        grads[n] = full[:, off:off + rows].reshape(shard)
        off += rows

    delta, new_m, new_v = {}, {}, {}
    for n, _, _ in BIG:
        view = lambda a: a.reshape(-1, a.shape[-1])
        outs = _adamw(view(w[n]), view(grads[n]), view(m[n]), view(v[n]), name="adamw_" + n)
        delta[n], new_m[n], new_v[n] = (o.reshape(w[n].shape) for o in outs)
    small = SMALL_SHARDED + REPLICATED
    sizes = [w[n].size for n in small]
    total = sum(sizes)
    rows = -(-total // (8 * D)) * 8
    packed = lambda d: _pad_rows(jnp.concatenate([d[n].reshape(-1) for n in small]).reshape(-1, 128), rows * 8).reshape(rows, D)
    outs = _adamw(packed(w), packed(grads), packed(m), packed(v), name="adamw_small")
    off = 0
    for n, size in zip(small, sizes):
        delta[n], new_m[n], new_v[n] = (o.reshape(-1)[off:off + size].reshape(w[n].shape) for o in outs)
        off += size
    grads = {n: grads[n].reshape(w[n].shape) for n in ORDER}
    return (loss_out, dx.reshape(x.shape), *[grads[n] for n in ORDER], *[delta[n] for n in ORDER],
            *[new_m[n] for n in ORDER], *[new_v[n] for n in ORDER])


def kernel(x, norm_mix_g, norm_ffn_g, w_in_ab, w_out_ab, hg_lb_logits, hg_norm_g, conv_w_glu, conv_b_glu, conv_w_dw, conv_b_dw, conv_ln_g, conv_ln_b, conv_w_pw, conv_b_pw, w_ff1, w_ff2, final_norm_g, loss_target, m_norm_mix_g, m_norm_ffn_g, m_w_in_ab, m_w_out_ab, m_hg_lb_logits, m_hg_norm_g, m_conv_w_glu, m_conv_b_glu, m_conv_w_dw, m_conv_b_dw, m_conv_ln_g, m_conv_ln_b, m_conv_w_pw, m_conv_b_pw, m_w_ff1, m_w_ff2, m_final_norm_g, v_norm_mix_g, v_norm_ffn_g, v_w_in_ab, v_w_out_ab, v_hg_lb_logits, v_hg_norm_g, v_conv_w_glu, v_conv_b_glu, v_conv_w_dw, v_conv_b_dw, v_conv_ln_g, v_conv_ln_b, v_conv_w_pw, v_conv_b_pw, v_w_ff1, v_w_ff2, v_final_norm_g):
    args = locals()
    w = {n: args[n] for n in ORDER}
    m = {n: args["m_" + n] for n in ORDER}
    v = {n: args["v_" + n] for n in ORDER}
    return _step(x, loss_target, w, m, v)
```

```python
import functools

import jax
import jax.numpy as jnp
from jax import lax
from jax.experimental import pallas as pl
from jax.experimental.pallas import tpu as pltpu

F32 = jnp.float32
MXU_DTYPE = jnp.bfloat16
MESH = pl.DeviceIdType.MESH

D_MODEL = 1024
SB_HEADS, SB_DH, SB_WIDTH = 8, 64, 512
SB_KEYS = 512
SB_SUB = 256
SB_ROWS_FWD, SB_ROWS_BWD = 512, 256
HG_HEADS, HG_DH, HG_WIDTH = 4, 128, 512
HG_CHUNK = 16
HG_TOKENS = 256
IN_WIDTH = 3 * SB_WIDTH + 4 * HG_WIDTH
CONV_WIDTH = 31
CONV_HALO = 32
CONV_ROWS = 32
D_FF = 4096
RMS_EPS = 1e-6
LN_EPS = 1e-5
N_CHIPS = 4
N_DEV = 8
SMALL_ROWS = 48
VMEM_LIMIT = 56 * 1024 * 1024

ADAM_LR, ADAM_B1, ADAM_B2, ADAM_EPS, ADAM_WD, ADAM_STEP = 0.001, 0.9, 0.999, 1e-08, 0.01, 10


def _params(*sem):
    return pltpu.CompilerParams(dimension_semantics=sem, vmem_limit_bytes=VMEM_LIMIT)


def _mx(v):
    return v.astype(MXU_DTYPE)


def _dot(a, b):
    return jnp.dot(_mx(a), _mx(b), preferred_element_type=F32)


def _dot_nt(a, b):
    return lax.dot_general(_mx(a), _mx(b), (((1,), (1,)), ((), ())), preferred_element_type=F32)


def _dot_tn(a, b):
    return lax.dot_general(_mx(a), _mx(b), (((0,), (0,)), ((), ())), preferred_element_type=F32)


def _neg_abs(x):
    bits = lax.bitcast_convert_type(x, jnp.uint32) | jnp.uint32(0x80000000)
    return lax.bitcast_convert_type(bits, F32)


def _key_order_sums(v, tri2, later):
    hi = _mx(v)
    lo = _mx(v - hi.astype(F32))
    n = SB_KEYS // SB_SUB
    blocks = [slice(b * SB_SUB, (b + 1) * SB_SUB) for b in range(n)]
    totals = [jnp.sum(v[:, sl], axis=1, keepdims=True) for sl in blocks]
    sums = []
    for b, sl in enumerate(blocks):
        s = jnp.dot(jnp.concatenate([hi[:, sl], lo[:, sl]], axis=1), tri2, preferred_element_type=F32)
        for o in (range(b + 1, n) if later else range(b)):
            s = s + totals[o]
        sums.append(s)
    return jnp.concatenate(sums, axis=1), functools.reduce(lambda a, b: a + b, totals)


def _matmul(a, b, *, mode, out_dtypes, epilogue=None, tiles=(), rows=(), tm=1024, tn=1024, tk=1024, name):
    if mode == "nn":
        (M, K), N = a.shape, b.shape[1]
    elif mode == "nt":
        (M, K), N = a.shape, b.shape[0]
    else:
        (K, M), N = a.shape, b.shape[1]
    tm, tn, tk = min(tm, M), min(tn, N), min(tk, K)
    assert M % tm == 0 and N % tn == 0 and K % tk == 0, (name, M, N, K)
    nk = K // tk
    a_spec = pl.BlockSpec((tk, tm), lambda i, j, k: (k, i)) if mode == "tn" else pl.BlockSpec((tm, tk), lambda i, j, k: (i, k))
    b_spec = pl.BlockSpec((tn, tk), lambda i, j, k: (j, k)) if mode == "nt" else pl.BlockSpec((tk, tn), lambda i, j, k: (k, j))
    dims = {"nn": ((1,), (0,)), "nt": ((1,), (1,)), "tn": ((0,), (0,))}[mode]
    n_t, n_r, n_o = len(tiles), len(rows), len(out_dtypes)
    if epilogue is None:
        epilogue = lambda acc: (acc,)

    def body(a_ref, b_ref, *rest):
        extra, outs, acc_ref = rest[:n_t + n_r], rest[n_t + n_r:n_t + n_r + n_o], rest[-1]
        k = pl.program_id(2)

        @pl.when(k == 0)
        def _():
            acc_ref[...] = jnp.zeros_like(acc_ref)

        acc_ref[...] += lax.dot_general(_mx(a_ref[...]), _mx(b_ref[...]), (dims, ((), ())), preferred_element_type=F32)

        @pl.when(k == nk - 1)
        def _():
            res = epilogue(acc_ref[...], *[e[...] for e in extra])
            for o_ref, r in zip(outs, res):
                o_ref[...] = r.astype(o_ref.dtype)

    tile_spec = pl.BlockSpec((tm, tn), lambda i, j, k: (i, j))
    row_spec = pl.BlockSpec((1, tn), lambda i, j, k: (0, j))
    outs = pl.pallas_call(
        body, grid=(M // tm, N // tn, nk),
        in_specs=[a_spec, b_spec] + [tile_spec] * n_t + [row_spec] * n_r,
        out_specs=[tile_spec] * n_o,
        out_shape=[jax.ShapeDtypeStruct((M, N), dt) for dt in out_dtypes],
        scratch_shapes=[pltpu.VMEM((tm, tn), F32)],
        compiler_params=_params("parallel", "parallel", "arbitrary"), name=name,
    )(a, b, *tiles, *rows)
    return outs[0] if n_o == 1 else outs


def _token_block(T):
    return min(512, T)


def _rmsnorm_fwd(h, g, *, name):
    T, D = h.shape
    tb = _token_block(T)

    def body(h_ref, g_ref, u_ref):
        x = h_ref[...]
        r = lax.rsqrt(jnp.mean(x * x, axis=-1, keepdims=True) + RMS_EPS)
        u_ref[...] = (x * r * g_ref[...]).astype(u_ref.dtype)

    blk = pl.BlockSpec((tb, D), lambda i: (i, 0))
    return pl.pallas_call(
        body, grid=(T // tb,), in_specs=[blk, pl.BlockSpec((1, D), lambda i: (0, 0))], out_specs=blk,
        out_shape=jax.ShapeDtypeStruct((T, D), MXU_DTYPE), compiler_params=_params("parallel"), name=name,
    )(h, g)


def _rms_bwd_math(x, g, du):
    r = lax.rsqrt(jnp.mean(x * x, axis=-1, keepdims=True) + RMS_EPS)
    gd = g * du
    dx = r * gd - x * (r * r * r) * jnp.mean(gd * x, axis=-1, keepdims=True)
    return dx, du * x * r


def _rmsnorm_bwd(du, h, g, dres, *, name):
    T, D = h.shape
    tb = _token_block(T)

    def body(du_ref, h_ref, g_ref, dres_ref, dh_ref, dhm_ref, dg_ref, cs_ref):
        @pl.when(pl.program_id(0) == 0)
        def _():
            dg_ref[...] = jnp.zeros_like(dg_ref)
            cs_ref[...] = jnp.zeros_like(cs_ref)

        dx, dg_terms = _rms_bwd_math(h_ref[...], g_ref[...], du_ref[...])
        dh = dres_ref[...] + dx
        dh_ref[...] = dh
        dhm_ref[...] = dh.astype(dhm_ref.dtype)
        dg_ref[...] += jnp.sum(dg_terms, axis=0, keepdims=True)
        cs_ref[...] += jnp.sum(dh, axis=0, keepdims=True)

    blk = pl.BlockSpec((tb, D), lambda i: (i, 0))
    row = pl.BlockSpec((1, D), lambda i: (0, 0))
    return pl.pallas_call(
        body, grid=(T // tb,), in_specs=[blk, blk, row, blk], out_specs=[blk, blk, row, row],
        out_shape=[jax.ShapeDtypeStruct((T, D), F32), jax.ShapeDtypeStruct((T, D), MXU_DTYPE), jax.ShapeDtypeStruct((1, D), F32),
                   jax.ShapeDtypeStruct((1, D), F32)],
        compiler_params=_params("arbitrary"), name=name,
    )(du, h, g, dres)


def _loss_head(h, g, target, *, name):
    T, D = h.shape
    tb = _token_block(T)

    def body(h_ref, g_ref, t_ref, dh_ref, dhm_ref, dg_ref, loss_ref):
        @pl.when(pl.program_id(0) == 0)
        def _():
            dg_ref[...] = jnp.zeros_like(dg_ref)
            loss_ref[...] = jnp.zeros_like(loss_ref)

        x, gg = h_ref[...], g_ref[...]
        r = lax.rsqrt(jnp.mean(x * x, axis=-1, keepdims=True) + RMS_EPS)
        diff = x * r * gg - t_ref[...]
        per_token = jnp.mean(diff * diff, axis=-1, keepdims=True)
        loss_ref[...] += 0.5 * jnp.sum(per_token, axis=0, keepdims=True)
        dx, dg_terms = _rms_bwd_math(x, gg, diff / D)
        dh_ref[...] = dx
        dhm_ref[...] = dx.astype(dhm_ref.dtype)
        dg_ref[...] += jnp.sum(dg_terms, axis=0, keepdims=True)

    blk = pl.BlockSpec((tb, D), lambda i: (i, 0))
    row = pl.BlockSpec((1, D), lambda i: (0, 0))
    return pl.pallas_call(
        body, grid=(T // tb,), in_specs=[blk, row, blk], out_specs=[blk, blk, row, pl.BlockSpec((1, 1), lambda i: (0, 0))],
        out_shape=[jax.ShapeDtypeStruct((T, D), F32), jax.ShapeDtypeStruct((T, D), MXU_DTYPE), jax.ShapeDtypeStruct((1, D), F32),
                   jax.ShapeDtypeStruct((1, 1), F32)],
        compiler_params=_params("arbitrary"), name=name,
    )(h, g, target)


def _sb_scores(qm, ks, later, tri, mask):
    z = _dot_nt(qm, ks)
    sp = jnp.maximum(z, 0.0) + jnp.log(1.0 + jnp.exp(_neg_abs(z)))
    lb = z - sp
    if mask is not None:
        sp = jnp.where(mask, sp, 0.0)
    after, total = _key_order_sums(sp, tri, later=True)
    w = jnp.exp(lb - (after + later))
    if mask is not None:
        w = jnp.where(mask, w, 0.0)
    return total, lb, w


def _sb_setup(q_ref, rows):
    i, hsel = pl.program_id(1), pl.program_id(2)
    lane = lax.broadcasted_iota(jnp.int32, (rows, 2 * SB_DH), 1)
    mine = (lane >= SB_DH) == (hsel == 1)
    diag = (i * rows) // SB_KEYS
    t = i * rows + lax.broadcasted_iota(jnp.int32, (rows, SB_KEYS), 0)
    s = diag * SB_KEYS + lax.broadcasted_iota(jnp.int32, (rows, SB_KEYS), 1)
    a = lax.broadcasted_iota(jnp.int32, (2 * SB_SUB, SB_SUB), 0) % SB_SUB
    b = lax.broadcasted_iota(jnp.int32, (2 * SB_SUB, SB_SUB), 1)
    return i, hsel, mine, diag, s < t, _mx(a > b), _mx(a < b)


def _sb_keys(j):
    return pl.ds(pl.multiple_of(j * SB_KEYS, SB_KEYS), SB_KEYS)


def _sb_descend(n, step, carry):
    carry = lax.fori_loop(0, n // 2, lambda it, cr: step(n - 2 - 2 * it, step(n - 1 - 2 * it, cr)), carry)
    return lax.cond(n % 2 == 1, lambda cr: step(0, cr), lambda cr: cr, carry)


def _sb_ascend(n, step, carry):
    odd = n % 2
    carry = lax.cond(odd == 1, lambda cr: step(0, cr), lambda cr: cr, carry)
    return lax.fori_loop(0, n // 2, lambda it, cr: step(odd + 2 * it + 1, step(odd + 2 * it, cr)), carry)


def _sb_call(body, qkv, extra_in, out_blocks, scratch, rider, rows, *, name):
    T = qkv.shape[0]
    n_pairs = SB_HEADS // 2
    grid = (n_pairs, T // rows, 2)
    pair = lambda col0: pl.BlockSpec((rows, 2 * SB_DH), lambda p, i, h: (i, col0 + p))
    whole = lambda col0: pl.BlockSpec((T, 2 * SB_DH), lambda p, i, h: (0, col0 + p))
    shape = jax.ShapeDtypeStruct((T, SB_WIDTH), F32)
    in_specs = [pair(0), whole(n_pairs), whole(2 * n_pairs)] + [pair(0)] * len(extra_in)
    out_specs = [pair(0) if kind == "pair" else whole(0) for kind in out_blocks]
    n_in, n_out, n_r = len(in_specs), len(out_specs), 0 if rider is None else len(rider.arrays)

    def kernel_body(*refs):
        ins, r_in = refs[:n_in], refs[n_in:n_in + n_r]
        outs, r_out = refs[n_in + n_r:n_in + n_r + n_out], refs[n_in + n_r + n_out:n_in + 2 * n_r + n_out]
        rest = refs[n_in + 2 * n_r + n_out:]
        ids = [pl.program_id(a) for a in range(3)]
        if rider is not None:
            @pl.when((ids[0] == 0) & (ids[1] == 0) & (ids[2] == 0))
            def _():
                for cp in rider.make(r_in, r_out, *rest[len(scratch):]):
                    cp.start()

        body(ins, outs, rest[:len(scratch)])
        if rider is not None:
            @pl.when((ids[0] == grid[0] - 1) & (ids[1] == grid[1] - 1) & (ids[2] == grid[2] - 1))
            def _():
                for cp in rider.make(r_in, r_out, *rest[len(scratch):]):
                    cp.wait()

    res = pl.pallas_call(
        kernel_body, grid=grid, in_specs=in_specs + [HBM_SPEC] * n_r, out_specs=out_specs + [HBM_SPEC] * n_r,
        out_shape=[shape] * n_out + ([] if rider is None else rider.out_shapes),
        scratch_shapes=list(scratch) + ([] if rider is None else rider.scratch()),
        compiler_params=_params("arbitrary", "arbitrary", "arbitrary"), name=name,
    )(qkv, qkv, qkv, *extra_in, *([] if rider is None else rider.arrays))
    return res[:n_out], res[n_out:]


def _sb_fwd(qkv, rider=None, *, name):
    rows = min(SB_ROWS_FWD, qkv.shape[0])
    scale = SB_DH ** -0.5

    def body(ins, outs, _):
        (q_ref, k_ref, v_ref), (o_ref,) = ins, outs
        i, hsel, mine, diag, mask, tri, _ = _sb_setup(q_ref, rows)
        qm = jnp.where(mine, q_ref[...], 0) * scale

        def tile(j, m, later, acc):
            total, _, w = _sb_scores(qm, k_ref[_sb_keys(j), :], later, tri, m)
            return later + total, acc + _dot(w, v_ref[_sb_keys(j), :])

        carry = tile(diag, mask, jnp.zeros((rows, 1), F32), jnp.zeros((rows, 2 * SB_DH), F32))
        _, acc = _sb_descend(diag, lambda j, cr: tile(j, None, *cr), carry)
        res = jnp.where(mine, acc, 0.0)

        @pl.when(hsel == 0)
        def _():
            o_ref[...] = res

        @pl.when(hsel == 1)
        def _():
            o_ref[...] += res

    (o,), got = _sb_call(body, qkv, [], ["pair"], [], rider, rows, name=name)
    return o, got


def _sb_bwd(qkv, dmix, rider=None, *, name):
    T = qkv.shape[0]
    rows = min(SB_ROWS_BWD, T)
    scale = SB_DH ** -0.5

    def body(ins, outs, scratch):
        (q_ref, k_ref, v_ref, do_ref), (dq_ref, dk_ref, dv_ref), (da_ref, beta_ref) = ins, outs, scratch
        i, hsel, mine, diag, mask, tri, tri_before = _sb_setup(q_ref, rows)

        @pl.when((i == 0) & (hsel == 0))
        def _():
            dk_ref[...] = jnp.zeros_like(dk_ref)
            dv_ref[...] = jnp.zeros_like(dv_ref)

        qm = jnp.where(mine, q_ref[...], 0) * scale
        do_m = _mx(jnp.where(mine, do_ref[...], 0.0))

        def weights(j, m, later):
            total, lb, w = _sb_scores(qm, k_ref[_sb_keys(j), :], later, tri, m)
            da_ref[j] = _dot_nt(do_m, v_ref[_sb_keys(j), :]) * w
            beta_ref[j] = jnp.exp(lb)
            dv_ref[_sb_keys(j), :] += _dot_tn(w, do_m)
            return later + total

        later = weights(diag, mask, jnp.zeros((rows, 1), F32))
        _sb_descend(diag, lambda j, c: weights(j, None, c), later)

        def logits(j, m, before, dq):
            da = da_ref[j]
            earlier, total = _key_order_sums(da, tri_before, later=False)
            dz = da - beta_ref[j] * (da + earlier + before)
            if m is not None:
                dz = jnp.where(m, dz, 0.0)
            dz = _mx(dz)
            dk_ref[_sb_keys(j), :] += _dot_tn(dz, qm)
            return before + total, dq + _dot(dz, k_ref[_sb_keys(j), :])

        carry = (jnp.zeros((rows, 1), F32), jnp.zeros((rows, 2 * SB_DH), F32))
        carry = _sb_ascend(diag, lambda j, cr: logits(j, None, *cr), carry)
        res = jnp.where(mine, logits(diag, mask, *carry)[1] * scale, 0.0)

        @pl.when(hsel == 0)
        def _():
            dq_ref[...] = res

        @pl.when(hsel == 1)
        def _():
            dq_ref[...] += res

    n_tiles = T // SB_KEYS
    scratch = [pltpu.VMEM((n_tiles, rows, SB_KEYS), F32), pltpu.VMEM((n_tiles, rows, SB_KEYS), F32)]
    return _sb_call(body, qkv, [dmix], ["pair", "whole", "whole"], scratch, rider, rows, name=name)


def _chunk_row(n):
    return lax.broadcasted_iota(jnp.int32, (n, HG_DH), 0) % HG_CHUNK


def _chunk_cumsum(x, row, reverse=False):
    n = x.shape[0]
    for sh in (1, 2, 4, 8):
        if reverse:
            x = x + jnp.where(row < HG_CHUNK - sh, pltpu.roll(x, n - sh, 0), 0.0)
        else:
            x = x + jnp.where(row >= sh, pltpu.roll(x, sh, 0), 0.0)
    return x


def _hg_lower_bound(logits_ref):
    lg = logits_ref[...]
    e = jnp.exp(lg - jnp.max(lg, axis=0, keepdims=True))
    return e[0:1, :] / jnp.sum(e, axis=0, keepdims=True)


def _hg_terms(fr, q, lb, row):
    sig = jax.nn.sigmoid(fr)
    f = lb + (1.0 - lb) * sig
    kk = 1.0 - f
    g = jnp.log(f)
    G = _chunk_cumsum(g, row)
    g_last = G + (_chunk_cumsum(g, row, reverse=True) - g)
    e_g, e_ng, e_lg = jnp.exp(G), jnp.exp(-G), jnp.exp(g_last - G)
    return dict(sig=sig, f=f, kk=kk, e_g=e_g, e_ng=e_ng, e_lg=e_lg, q_dec=q * e_g, k_intra=kk * e_ng,
                k_state=kk * e_lg, decay=jnp.exp(g_last))


def _hg_causal(n):
    t = lax.broadcasted_iota(jnp.int32, (n, n), 0)
    s = lax.broadcasted_iota(jnp.int32, (n, n), 1)
    return (s <= t) & (s // HG_CHUNK == t // HG_CHUNK)


def _chunks(a):
    return a.reshape(a.shape[0] // HG_CHUNK, HG_CHUNK, a.shape[1])


def _per_chunk(lhs, rhs, contract):
    return lax.dot_general(_mx(lhs), _mx(rhs), ((contract[:1], contract[1:]), ((0,), (0,))), preferred_element_type=F32)


def _hg_specs(T, tb, col0, order):
    return [pl.BlockSpec((tb, HG_WIDTH), functools.partial(lambda i, j: (order(i), j), j=col0 + j)) for j in range(4)]


def _hg_fwd(proj, logits, norm_g, *, name):
    T = proj.shape[0]
    tb = min(HG_TOKENS, T)
    nch = tb // HG_CHUNK

    def body(q_ref, f_ref, i_ref, gate_ref, lg_ref, ng_ref, out_ref, o_ref, s_ref, st_ref, inc_ref, dec_ref):
        @pl.when(pl.program_id(0) == 0)
        def _():
            st_ref[...] = jnp.zeros_like(st_ref)

        lb_all = _hg_lower_bound(lg_ref)
        row = _chunk_row(tb)
        causal = _hg_causal(tb)
        for hh in range(HG_HEADS):
            cols = slice(hh * HG_DH, (hh + 1) * HG_DH)
            t = _hg_terms(f_ref[:, cols], q_ref[:, cols], lb_all[:, cols], row)
            v = i_ref[:, cols]
            scores = jnp.where(causal, _dot_nt(t["q_dec"], t["k_intra"]), 0.0)
            o_intra = _dot(scores, v)
            inc_ref[...] = _per_chunk(_chunks(v), _chunks(t["k_state"]), (1, 1))
            dec_ref[...] = _chunks(t["decay"])

            def step(ci, st):
                s_ref[ci, hh] = st
                return st * dec_ref[ci][0:1, :] + inc_ref[ci]

            st_ref[hh] = lax.fori_loop(0, nch, step, st_ref[hh], unroll=4)
            o_inter = _per_chunk(_chunks(t["q_dec"]), s_ref[:, hh], (2, 2))
            o = o_intra + o_inter.reshape(tb, HG_DH)
            o_ref[:, cols] = o
            gate = gate_ref[:, cols]
            on = o * lax.rsqrt(jnp.mean(o * o, axis=-1, keepdims=True) + RMS_EPS) * ng_ref[:, cols]
            out_ref[:, cols] = (on * (gate * jax.nn.sigmoid(gate))).astype(out_ref.dtype)

    blk = pl.BlockSpec((tb, HG_WIDTH), lambda i: (i, 0))
    return pl.pallas_call(
        body, grid=(T // tb,),
        in_specs=_hg_specs(T, tb, 3, lambda i: i) + [pl.BlockSpec((3, HG_WIDTH), lambda i: (0, 0)), pl.BlockSpec((1, HG_WIDTH), lambda i: (0, 0))],
        out_specs=[blk, blk, pl.BlockSpec((nch, HG_HEADS, HG_DH, HG_DH), lambda i: (i, 0, 0, 0))],
        out_shape=[jax.ShapeDtypeStruct((T, HG_WIDTH), MXU_DTYPE), jax.ShapeDtypeStruct((T, HG_WIDTH), F32),
                   jax.ShapeDtypeStruct((T // HG_CHUNK, HG_HEADS, HG_DH, HG_DH), F32)],
        scratch_shapes=[pltpu.VMEM((HG_HEADS, HG_DH, HG_DH), F32), pltpu.VMEM((nch, HG_DH, HG_DH), F32),
                        pltpu.VMEM((nch, HG_CHUNK, HG_DH), F32)],
        compiler_params=_params("arbitrary"), name=name,
    )(proj, proj, proj, proj, logits, norm_g)


def _hg_bwd(proj, o_raw, states, dmix, logits, norm_g, *, name):
    T = proj.shape[0]
    tb = min(HG_TOKENS, T)
    nch = tb // HG_CHUNK
    nb = T // tb
    rev = lambda i: nb - 1 - i

    def body(q_ref, f_ref, i_ref, gate_ref, o_ref, s_ref, dout_ref, lg_ref, ng_ref, dp_ref, dlb_ref, dng_ref,
             dst_ref, inc_ref, dec_ref, after_ref):
        @pl.when(pl.program_id(0) == 0)
        def _():
            dst_ref[...] = jnp.zeros_like(dst_ref)
            dlb_ref[...] = jnp.zeros_like(dlb_ref)
            dng_ref[...] = jnp.zeros_like(dng_ref)

        lb_all = _hg_lower_bound(lg_ref)
        row = _chunk_row(tb)
        causal = _hg_causal(tb)
        for hh in range(HG_HEADS):
            cols = slice(hh * HG_DH, (hh + 1) * HG_DH)
            out_cols = lambda part: slice(part * HG_WIDTH + hh * HG_DH, part * HG_WIDTH + (hh + 1) * HG_DH)
            o, gate, dout, ng, lb = o_ref[:, cols], gate_ref[:, cols], dout_ref[:, cols], ng_ref[:, cols], lb_all[:, cols]
            sg = jax.nn.sigmoid(gate)
            r = lax.rsqrt(jnp.mean(o * o, axis=-1, keepdims=True) + RMS_EPS)
            oh = o * r
            dp_ref[:, out_cols(3)] = dout * (oh * ng) * (sg * (1.0 + gate * (1.0 - sg)))
            don = dout * (gate * sg)
            dng_ref[:, cols] += jnp.sum(don * oh, axis=0, keepdims=True)
            doh = don * ng
            do = r * (doh - oh * jnp.mean(doh * oh, axis=-1, keepdims=True))

            t = _hg_terms(f_ref[:, cols], q_ref[:, cols], lb, row)
            v = i_ref[:, cols]
            scores = jnp.where(causal, _dot_nt(t["q_dec"], t["k_intra"]), 0.0)
            dscores = jnp.where(causal, _dot_nt(do, v), 0.0)
            inc_ref[...] = _per_chunk(_chunks(do), _chunks(t["q_dec"]), (1, 1))
            dec_ref[...] = _chunks(t["decay"])

            def step(it, dst):
                ci = nch - 1 - it
                after_ref[ci] = dst
                return dst * dec_ref[ci][0:1, :] + inc_ref[ci]

            dst_ref[hh] = lax.fori_loop(0, nch, step, dst_ref[hh], unroll=4)
            st, dst = s_ref[:, hh], after_ref[...]
            dqd = _dot(dscores, t["k_intra"]) + _per_chunk(_chunks(do), st, (2, 1)).reshape(tb, HG_DH)
            dki = _dot_tn(dscores, t["q_dec"])
            dks = _per_chunk(_chunks(v), dst, (2, 1)).reshape(tb, HG_DH)
            dp_ref[:, out_cols(2)] = _dot_tn(scores, do) + _per_chunk(_chunks(t["k_state"]), dst, (2, 2)).reshape(tb, HG_DH)
            ddecay = jnp.broadcast_to(jnp.sum(st * dst, axis=1, keepdims=True), (nch, HG_CHUNK, HG_DH)).reshape(tb, HG_DH)
            dks_ks = dks * t["k_state"]
            d_glast = _chunk_cumsum(dks_ks, row) + ddecay * t["decay"]
            d_g = dqd * t["q_dec"] - dki * t["k_intra"] - dks_ks + jnp.where(row == HG_CHUNK - 1, d_glast, 0.0)
            df = _chunk_cumsum(d_g, row, reverse=True) / t["f"] - (dki * t["e_ng"] + dks * t["e_lg"])
            dp_ref[:, out_cols(0)] = dqd * t["e_g"]
            dp_ref[:, out_cols(1)] = df * (1.0 - lb) * t["sig"] * (1.0 - t["sig"])
            dlb_ref[:, cols] += jnp.sum(df * (1.0 - t["sig"]), axis=0, keepdims=True)

    blk = pl.BlockSpec((tb, HG_WIDTH), lambda i: (rev(i), 0))
    row_spec = pl.BlockSpec((1, HG_WIDTH), lambda i: (0, 0))
    return pl.pallas_call(
        body, grid=(nb,),
        in_specs=_hg_specs(T, tb, 3, rev) + [
            blk, pl.BlockSpec((nch, HG_HEADS, HG_DH, HG_DH), lambda i: (rev(i), 0, 0, 0)),
            pl.BlockSpec((tb, HG_WIDTH), lambda i: (rev(i), 1)), pl.BlockSpec((3, HG_WIDTH), lambda i: (0, 0)), row_spec],
        out_specs=[pl.BlockSpec((tb, 4 * HG_WIDTH), lambda i: (rev(i), 0)), row_spec, row_spec],
        out_shape=[jax.ShapeDtypeStruct((T, 4 * HG_WIDTH), F32), jax.ShapeDtypeStruct((1, HG_WIDTH), F32), jax.ShapeDtypeStruct((1, HG_WIDTH), F32)],
        scratch_shapes=[pltpu.VMEM((HG_HEADS, HG_DH, HG_DH), F32), pltpu.VMEM((nch, HG_DH, HG_DH), F32),
                        pltpu.VMEM((nch, HG_CHUNK, HG_DH), F32), pltpu.VMEM((nch, HG_DH, HG_DH), F32)],
        compiler_params=_params("arbitrary"), name=name,
    )(proj, proj, proj, proj, o_raw, states, dmix, logits, norm_g)


def _shifted_copies(sh_ref, n_rows):
    keep = n_rows + CONV_HALO - 8
    for b in range(1, 8):
        sh_ref[b, 0:keep, :] = sh_ref[0, b:b + keep, :]


def _tap_rows(sh_ref, offset, r0, lanes):
    start = pl.multiple_of(r0 + (offset - offset % 8), 8)
    return sh_ref[offset % 8, pl.ds(start, CONV_ROWS), lanes]


def _conv_fwd(p, w_dw, b_dw, ln_g, ln_b, *, name):
    T, D = p.shape[0], p.shape[1] // 2
    tb = _token_block(T)
    hpb = tb // CONV_HALO
    lane_step = 512

    def body(p1_ref, p2_ref, q1_ref, q2_ref, w_ref, bdw_ref, g_ref, b_ref, a_ref, y_ref, act_ref, sh_ref):
        i = pl.program_id(0)
        a = p1_ref[...] * jax.nn.sigmoid(p2_ref[...])
        sh_ref[0, 0:CONV_HALO, :] = jnp.where(i > 0, q1_ref[...] * jax.nn.sigmoid(q2_ref[...]), 0.0)
        sh_ref[0, CONV_HALO:, :] = a
        a_ref[...] = a
        _shifted_copies(sh_ref, tb)

        def chunk(ci, _):
            r0 = pl.multiple_of(ci * CONV_ROWS, CONV_ROWS)
            for l0 in range(0, D, lane_step):
                lanes = slice(l0, l0 + lane_step)
                acc = jnp.broadcast_to(bdw_ref[:, lanes], (CONV_ROWS, lane_step))
                for k in range(CONV_WIDTH):
                    acc = acc + _tap_rows(sh_ref, CONV_HALO - CONV_WIDTH + 1 + k, r0, lanes) * w_ref[k:k + 1, lanes]
                y_ref[pl.ds(r0, CONV_ROWS), lanes] = acc
            return 0

        lax.fori_loop(0, tb // CONV_ROWS, chunk, 0)
        y = y_ref[...]
        mu = jnp.mean(y, axis=-1, keepdims=True)
        yc = y - mu
        s = yc * lax.rsqrt(jnp.mean(yc * yc, axis=-1, keepdims=True) + LN_EPS) * g_ref[...] + b_ref[...]
        act_ref[...] = (s * jax.nn.sigmoid(s)).astype(act_ref.dtype)

    prev = lambda i: jnp.maximum(i * hpb - 1, 0)
    blk = pl.BlockSpec((tb, D), lambda i: (i, 0))
    row = pl.BlockSpec((1, D), lambda i: (0, 0))
    return pl.pallas_call(
        body, grid=(T // tb,),
        in_specs=[blk, pl.BlockSpec((tb, D), lambda i: (i, 1)), pl.BlockSpec((CONV_HALO, D), lambda i: (prev(i), 0)),
                  pl.BlockSpec((CONV_HALO, D), lambda i: (prev(i), 1)), pl.BlockSpec((CONV_HALO, D), lambda i: (0, 0)), row, row, row],
        out_specs=[blk, blk, blk],
        out_shape=[jax.ShapeDtypeStruct((T, D), F32), jax.ShapeDtypeStruct((T, D), F32), jax.ShapeDtypeStruct((T, D), MXU_DTYPE)],
        scratch_shapes=[pltpu.VMEM((8, tb + CONV_HALO, D), F32)],
        compiler_params=_params("parallel"), name=name,
    )(p, p, p, p, w_dw, b_dw, ln_g, ln_b)


def _conv_bwd_norm(dact, y, ln_g, ln_b, *, name):
    T, D = y.shape
    tb = _token_block(T)

    def body(da_ref, y_ref, g_ref, b_ref, dy_ref, dg_ref, db_ref, cs_ref):
        @pl.when(pl.program_id(0) == 0)
        def _():
            dg_ref[...] = jnp.zeros_like(dg_ref)
            db_ref[...] = jnp.zeros_like(db_ref)
            cs_ref[...] = jnp.zeros_like(cs_ref)

        y, g = y_ref[...], g_ref[...]
        yc = y - jnp.mean(y, axis=-1, keepdims=True)
        rs = lax.rsqrt(jnp.mean(yc * yc, axis=-1, keepdims=True) + LN_EPS)
        yn = yc * rs
        s = yn * g + b_ref[...]
        sg = jax.nn.sigmoid(s)
        ds = da_ref[...] * (sg * (1.0 + s * (1.0 - sg)))
        dg_ref[...] += jnp.sum(ds * yn, axis=0, keepdims=True)
        db_ref[...] += jnp.sum(ds, axis=0, keepdims=True)
        dyn = ds * g
        dy = rs * (dyn - jnp.mean(dyn, axis=-1, keepdims=True) - yn * jnp.mean(dyn * yn, axis=-1, keepdims=True))
        dy_ref[...] = dy
        cs_ref[...] += jnp.sum(dy, axis=0, keepdims=True)

    blk = pl.BlockSpec((tb, D), lambda i: (i, 0))
    row = pl.BlockSpec((1, D), lambda i: (0, 0))
    rs_ = jax.ShapeDtypeStruct((1, D), F32)
    return pl.pallas_call(
        body, grid=(T // tb,), in_specs=[blk, blk, row, row], out_specs=[blk, row, row, row],
        out_shape=[jax.ShapeDtypeStruct((T, D), F32), rs_, rs_, rs_], compiler_params=_params("arbitrary"), name=name,
    )(dact, y, ln_g, ln_b)


def _conv_bwd_taps(dy, a, p, w_dw, *, name):
    T, D = dy.shape
    tb = _token_block(T)
    hpb = tb // CONV_HALO
    last = T // CONV_HALO - 1
    nb = T // tb
    lane_step = 128
    groups = CONV_ROWS // 8

    def body(dy_ref, dyn_ref, a_ref, p1_ref, p2_ref, w_ref, dp_ref, dw_ref, cs_ref, sh_ref, da_ref):
        i = pl.program_id(0)

        @pl.when(i == 0)
        def _():
            dw_ref[...] = jnp.zeros_like(dw_ref)
            cs_ref[...] = jnp.zeros_like(cs_ref)

        sh_ref[0, 0:tb, :] = dy_ref[...]
        sh_ref[0, tb:, :] = jnp.where(i < nb - 1, dyn_ref[...], 0.0)
        _shifted_copies(sh_ref, tb)
        for l0 in range(0, D, lane_step):
            lanes = slice(l0, l0 + lane_step)

            def chunk(ci, sums):
                r0 = pl.multiple_of(ci * CONV_ROWS, CONV_ROWS)
                a_c = a_ref[pl.ds(r0, CONV_ROWS), lanes]
                da = jnp.zeros((CONV_ROWS, lane_step), F32)
                new = []
                for k in range(CONV_WIDTH):
                    s_k = _tap_rows(sh_ref, CONV_WIDTH - 1 - k, r0, lanes)
                    da = da + s_k * w_ref[k:k + 1, lanes]
                    new.append(sums[k] + jnp.sum((s_k * a_c).reshape(groups, 8, lane_step), axis=0))
                da_ref[pl.ds(r0, CONV_ROWS), lanes] = da
                return tuple(new)

            sums = lax.fori_loop(0, tb // CONV_ROWS, chunk, tuple(jnp.zeros((8, lane_step), F32) for _ in range(CONV_WIDTH)))
            for k in range(CONV_WIDTH):
                dw_ref[k:k + 1, lanes] += jnp.sum(sums[k], axis=0, keepdims=True)
        da = da_ref[...]
        p1 = p1_ref[...]
        sg = jax.nn.sigmoid(p2_ref[...])
        dp1 = da * sg
        dp2 = da * p1 * (sg * (1.0 - sg))
        dp_ref[:, 0:D] = dp1.astype(dp_ref.dtype)
        dp_ref[:, D:] = dp2.astype(dp_ref.dtype)
        cs_ref[:, 0:D] += jnp.sum(dp1, axis=0, keepdims=True)
        cs_ref[:, D:] += jnp.sum(dp2, axis=0, keepdims=True)

    blk = pl.BlockSpec((tb, D), lambda i: (i, 0))
    return pl.pallas_call(
        body, grid=(nb,),
        in_specs=[blk, pl.BlockSpec((CONV_HALO, D), lambda i: (jnp.minimum((i + 1) * hpb, last), 0)), blk, blk,
                  pl.BlockSpec((tb, D), lambda i: (i, 1)), pl.BlockSpec((CONV_HALO, D), lambda i: (0, 0))],
        out_specs=[pl.BlockSpec((tb, 2 * D), lambda i: (i, 0)), pl.BlockSpec((CONV_HALO, D), lambda i: (0, 0)), pl.BlockSpec((1, 2 * D), lambda i: (0, 0))],
        out_shape=[jax.ShapeDtypeStruct((T, 2 * D), MXU_DTYPE), jax.ShapeDtypeStruct((CONV_HALO, D), F32), jax.ShapeDtypeStruct((1, 2 * D), F32)],
        scratch_shapes=[pltpu.VMEM((8, tb + CONV_HALO, D), F32), pltpu.VMEM((tb, D), F32)],
        compiler_params=_params("arbitrary"), name=name,
    )(dy, dy, a, p, p, w_dw)


def _row_block(rows):
    for tr in (512, 256, 128, 64, 32, 16, 8):
        if rows % tr == 0:
            return tr
    return rows


def _sum_leading(x, *, name):
    n, R, C = x.shape
    tr = _row_block(R)

    def body(x_ref, o_ref):
        acc = x_ref[0]
        for j in range(1, n):
            acc = acc + x_ref[j]
        o_ref[...] = acc

    return pl.pallas_call(
        body, grid=(R // tr,), in_specs=[pl.BlockSpec((n, tr, C), lambda i: (0, i, 0))], out_specs=pl.BlockSpec((tr, C), lambda i: (i, 0)),
        out_shape=jax.ShapeDtypeStruct((R, C), x.dtype), compiler_params=_params("parallel"), name=name,
    )(x)


def _add_pair(x, y, *, name):
    n, R, C = x.shape
    tr = _row_block(R)

    def body(x_ref, y_ref, o_ref):
        o_ref[...] = x_ref[...] + y_ref[...]

    blk = pl.BlockSpec((1, tr, C), lambda j, i: (j, i, 0))
    return pl.pallas_call(
        body, grid=(n, R // tr), in_specs=[blk, blk], out_specs=blk,
        out_shape=jax.ShapeDtypeStruct((n, R, C), x.dtype), compiler_params=_params("parallel", "parallel"), name=name,
    )(x, y)


def _adamw(w, g, m, v, *, name):
    R, C = w.shape
    tr = _row_block(R)
    c1, c2 = 1.0 - ADAM_B1 ** ADAM_STEP, 1.0 - ADAM_B2 ** ADAM_STEP

    def body(w_ref, g_ref, m_ref, v_ref, d_ref, nm_ref, nv_ref):
        g_ = g_ref[...]
        nm = ADAM_B1 * m_ref[...] + (1.0 - ADAM_B1) * g_
        nv = ADAM_B2 * v_ref[...] + (1.0 - ADAM_B2) * (g_ * g_)
        d_ref[...] = -ADAM_LR * ((nm / c1) / (jnp.sqrt(nv / c2) + ADAM_EPS) + ADAM_WD * w_ref[...])
        nm_ref[...] = nm
        nv_ref[...] = nv

    blk = pl.BlockSpec((tr, C), lambda i: (i, 0))
    shp = jax.ShapeDtypeStruct((R, C), F32)
    return pl.pallas_call(
        body, grid=(R // tr,), in_specs=[blk] * 4, out_specs=[blk] * 3, out_shape=[shp] * 3,
        compiler_params=_params("parallel"), name=name,
    )(w, g, m, v)


def _small_reduce(packs, logits, *, name):
    n, R, C = packs.shape

    def body(p_ref, lg_ref, s_ref, dlg_ref):
        acc = p_ref[0]
        for j in range(1, n):
            acc = acc + p_ref[j]
        s_ref[...] = acc
        lg = lg_ref[...]
        e = jnp.exp(lg - jnp.max(lg, axis=0, keepdims=True))
        sm = e / jnp.sum(e, axis=0, keepdims=True)
        dlb = acc[5:6, HG_WIDTH:2 * HG_WIDTH]
        first = lax.broadcasted_iota(jnp.int32, sm.shape, 0) == 0
        dlg_ref[...] = sm[0:1, :] * (jnp.where(first, 1.0, 0.0) - sm) * dlb

    whole = lambda shape: pl.BlockSpec(shape, lambda: (0,) * len(shape))
    return pl.pallas_call(
        body, in_specs=[whole((n, R, C)), whole(logits.shape)], out_specs=[whole((R, C)), whole(logits.shape)],
        out_shape=[jax.ShapeDtypeStruct((R, C), F32), jax.ShapeDtypeStruct(logits.shape, F32)],
        compiler_params=pltpu.CompilerParams(vmem_limit_bytes=VMEM_LIMIT), name=name,
    )(packs, logits)


HBM_SPEC = pl.BlockSpec(memory_space=pl.ANY)


def _place():
    return lax.axis_index("x"), lax.axis_index("y"), lax.axis_index("c")


class _Copies:
    def __init__(self, arrays, out_shapes, n_copies, make, finish):
        self.arrays, self.out_shapes, self.n_copies, self.make, self.finish = list(arrays), list(out_shapes), n_copies, make, finish

    def scratch(self):
        return [pltpu.SemaphoreType.DMA((self.n_copies,)), pltpu.SemaphoreType.DMA((self.n_copies,))]

    def run(self, name):
        n = len(self.arrays)

        def body(*refs):
            copies = self.make(refs[:n], refs[n:2 * n], *refs[2 * n:])
            for cp in copies:
                cp.start()
            for cp in copies:
                cp.wait()

        outs = pl.pallas_call(body, in_specs=[HBM_SPEC] * n, out_specs=[HBM_SPEC] * n, out_shape=self.out_shapes,
                              scratch_shapes=self.scratch(), name=name)(*self.arrays)
        return self.finish(outs)


def _remote(src, dst, send_sems, recv_sems, k, peer):
    return pltpu.make_async_remote_copy(src_ref=src, dst_ref=dst, send_sem=send_sems.at[k], recv_sem=recv_sems.at[k],
                                        device_id=peer, device_id_type=MESH)


def _same_core_peers(x, y, c):
    return [(1 - x, y, c), (x, 1 - y, c), (1 - x, 1 - y, c)]


def _all_peers(x, y, c):
    flip = lambda v, b: 1 - v if b else v
    return [(flip(x, r & 4), flip(y, r & 2), flip(c, r & 1)) for r in range(1, 8)]


def _gather(arrays, peers_of, slot_of, n_slots):
    n_peers = len(peers_of(0, 0, 0))

    def make(ins, outs, send_sems, recv_sems):
        x, y, c = _place()
        slot = slot_of(x, y, c)
        return [_remote(ins[a], outs[a].at[slot], send_sems, recv_sems, a * n_peers + k, peer)
                for a in range(len(arrays)) for k, peer in enumerate(peers_of(x, y, c))]

    def finish(outs):
        slot = slot_of(*_place())
        return [lax.dynamic_update_index_in_dim(o, a, slot, 0) for o, a in zip(outs, arrays)]

    shapes = [jax.ShapeDtypeStruct((n_slots,) + a.shape, a.dtype) for a in arrays]
    return _Copies(arrays, shapes, len(arrays) * n_peers, make, finish)


def _gather_chips(arrays):
    return _gather(arrays, _same_core_peers, lambda x, y, c: 2 * x + y, N_CHIPS)


def _gather_all(arrays):
    return _gather(arrays, _all_peers, lambda x, y, c: 4 * x + 2 * y + c, N_DEV)


def _pair_swap(a):
    def make(ins, outs, send_sems, recv_sems):
        x, y, c = _place()
        return [_remote(ins[0].at[1 - c], outs[0], send_sems, recv_sems, 0, (x, y, 1 - c))]

    return _Copies([a], [jax.ShapeDtypeStruct(a.shape[1:], a.dtype)], 1, make, lambda outs: outs[0])


def _chip_scatter(p):
    def make(ins, outs, send_sems, recv_sems):
        x, y, c = _place()
        return [_remote(ins[0].at[2 * px + py], outs[0].at[2 * x + y], send_sems, recv_sems, k, (px, py, pc))
                for k, (px, py, pc) in enumerate(_same_core_peers(x, y, c))]

    def finish(outs):
        x, y, _ = _place()
        me = 2 * x + y
        return lax.dynamic_update_index_in_dim(outs[0], lax.dynamic_index_in_dim(p, me, 0, keepdims=False), me, 0)

    return _Copies([p], [jax.ShapeDtypeStruct(p.shape, p.dtype)], 3, make, finish)


def _pair_gather(q):
    def make(ins, outs, send_sems, recv_sems):
        x, y, c = _place()
        return [_remote(ins[0], outs[0].at[c], send_sems, recv_sems, 0, (x, y, 1 - c))]

    return _Copies([q], [jax.ShapeDtypeStruct((2,) + q.shape, q.dtype)], 1, make,
                   lambda outs: lax.dynamic_update_index_in_dim(outs[0], q, _place()[2], 0))


def _cols_from_chips(g):
    g = jnp.moveaxis(g, 0, -2)
    return g.reshape(g.shape[:-2] + (g.shape[-2] * g.shape[-1],))


def _rows_from_chips(g):
    g = jnp.moveaxis(g, 0, -3)
    return g.reshape(g.shape[:-3] + (g.shape[-3] * g.shape[-2], g.shape[-1]))


def _grad_blocks(dw, kind):
    if kind == "cols2d":
        K, N = dw.shape
        b = dw.reshape(2, K // 2, N_CHIPS, N // N_CHIPS).transpose(2, 0, 1, 3)
    elif kind == "rows2d":
        b = dw.reshape(N_CHIPS, 2, dw.shape[0] // 8, dw.shape[1])
    elif kind == "cols3d":
        L, K, N = dw.shape
        b = dw.reshape(L, K, N_CHIPS, N // N_CHIPS).transpose(2, 0, 1, 3)
    else:
        L, K, N = dw.shape
        b = dw.reshape(L, N_CHIPS, K // N_CHIPS, N).transpose(1, 0, 2, 3)
    return b.reshape(N_CHIPS, 2, -1, D_MODEL)


def _pad_rows(a, rows):
    return jnp.concatenate([a, jnp.zeros((rows - a.shape[0],) + a.shape[1:], a.dtype)], axis=0)


def _forward_backward(x, target, W, late_weights=None, early_grads=None):
    row = lambda a: a.reshape(1, -1)
    relu2 = lambda acc: (jnp.square(jnp.maximum(acc, 0.0)),)
    residual = lambda acc, res: (res + acc,)
    G = {}

    u0 = _rmsnorm_fwd(x, row(W["norm_mix_g"][0]), name="norm_mix0")
    proj = _matmul(u0, W["w_in"], mode="nn", out_dtypes=[F32], tn=896, name="in_proj")
    qkv = proj[:, :3 * SB_WIDTH].astype(MXU_DTYPE)
    o_sb, got = _sb_fwd(qkv, late_weights and late_weights[0], name="sb_fwd")
    if late_weights:
        W = {**W, **late_weights[1](got)}
    hg_out, hg_o, hg_states = _hg_fwd(proj, W["hg_lb_logits"], row(W["hg_norm_g"]), name="hg_fwd")
    mix = jnp.concatenate([o_sb.astype(MXU_DTYPE), hg_out], axis=-1)
    h1 = _matmul(mix, W["w_out"], mode="nn", out_dtypes=[F32], epilogue=residual, tiles=[x], name="out_proj")
    u1 = _rmsnorm_fwd(h1, row(W["norm_ffn_g"][0]), name="norm_ffn0")
    r0 = _matmul(u1, W["w_ff1"][0], mode="nn", out_dtypes=[MXU_DTYPE], epilogue=relu2, name="ff1_0")
    h2 = _matmul(r0, W["w_ff2"][0], mode="nn", out_dtypes=[F32], epilogue=residual, tiles=[h1], name="ff2_0")
    u2 = _rmsnorm_fwd(h2, row(W["norm_mix_g"][1]), name="norm_mix1")
    p = _matmul(u2, W["w_glu"], mode="nn", out_dtypes=[F32], epilogue=lambda acc, b: (acc + b,), rows=[row(W["b_glu"])], name="glu_proj")
    w_dw = _pad_rows(W["w_dw"], CONV_HALO)
    ca, cy, cact = _conv_fwd(p, w_dw, row(W["b_dw"]), row(W["ln_g"]), row(W["ln_b"]), name="conv_fwd")
    h3 = _matmul(cact, W["w_pw"], mode="nn", out_dtypes=[F32], epilogue=lambda acc, res, b: (res + acc + b,),
                 tiles=[h2], rows=[row(W["b_pw"])], name="pw_proj")
    u3 = _rmsnorm_fwd(h3, row(W["norm_ffn_g"][1]), name="norm_ffn1")
    r1 = _matmul(u3, W["w_ff1"][1], mode="nn", out_dtypes=[MXU_DTYPE], epilogue=relu2, name="ff1_1")
    h4 = _matmul(r1, W["w_ff2"][1], mode="nn", out_dtypes=[F32], epilogue=residual, tiles=[h3], name="ff2_1")

    dh4, dh4_m, G["final_norm_g"], loss = _loss_head(h4, row(W["final_norm_g"]), target, name="loss_head")

    def mlp_bwd(dh, dh_m, h_in, u, r, layer, tag):
        d_relu2 = lambda acc, r_blk: (acc * (2.0 * jnp.sqrt(r_blk.astype(F32))),)
        da = _matmul(dh_m, W["w_ff2"][layer], mode="nt", out_dtypes=[MXU_DTYPE], epilogue=d_relu2, tiles=[r], name="d_ff2_act" + tag)
        dw2 = _matmul(r, dh_m, mode="tn", out_dtypes=[F32], name="d_ff2_w" + tag)
        dw1 = _matmul(u, da, mode="tn", out_dtypes=[F32], name="d_ff1_w" + tag)
        du = _matmul(da, W["w_ff1"][layer], mode="nt", out_dtypes=[F32], name="d_ff1_act" + tag)
        dh_in, dh_in_m, dg, cs = _rmsnorm_bwd(du, h_in, row(W["norm_ffn_g"][layer]), dh, name="d_norm_ffn" + tag)
        return dh_in, dh_in_m, dg, cs, dw1, dw2

    dh3, dh3_m, dg_ffn1, cs_h3, dw1_1, dw2_1 = mlp_bwd(dh4, dh4_m, h3, u3, r1, 1, "1")
    G["b_pw"] = cs_h3
    dact = _matmul(dh3_m, W["w_pw"], mode="nt", out_dtypes=[F32], name="d_pw_act")
    G["w_pw"] = _matmul(cact, dh3_m, mode="tn", out_dtypes=[F32], name="d_pw_w")
    dy, G["ln_g"], G["ln_b"], G["b_dw"] = _conv_bwd_norm(dact, cy, row(W["ln_g"]), row(W["ln_b"]), name="d_conv_norm")
    dp, G["w_dw"], G["b_glu"] = _conv_bwd_taps(dy, ca, p, w_dw, name="d_conv_taps")
    G["w_glu"] = _matmul(u2, dp, mode="tn", out_dtypes=[F32], name="d_glu_w")
    du2 = _matmul(dp, W["w_glu"], mode="nt", out_dtypes=[F32], name="d_glu_act")
    dh2, dh2_m, dg_mix1, _ = _rmsnorm_bwd(du2, h2, row(W["norm_mix_g"][1]), dh3, name="d_norm_mix1")
    dh1, dh1_m, dg_ffn0, _, dw1_0, dw2_0 = mlp_bwd(dh2, dh2_m, h1, u1, r0, 0, "0")
    G["w_ff1"], G["w_ff2"] = jnp.stack([dw1_0, dw1_1]), jnp.stack([dw2_0, dw2_1])
    G["norm_ffn_g"] = jnp.concatenate([dg_ffn0, dg_ffn1], axis=0)
    dmix = _matmul(dh1_m, W["w_out"], mode="nt", out_dtypes=[F32], name="d_out_act")
    G["w_out"] = _matmul(mix, dh1_m, mode="tn", out_dtypes=[F32], name="d_out_w")
    riding = early_grads(G) if early_grads else None
    (dsq, dsk, dsv), got = _sb_bwd(qkv, dmix, riding, name="sb_bwd")
    d_hg, G["hg_lb"], G["hg_norm_g"] = _hg_bwd(proj, hg_o, hg_states, dmix, W["hg_lb_logits"], row(W["hg_norm_g"]), name="hg_bwd")
    dproj = jnp.concatenate([dsq, dsk, dsv, d_hg], axis=-1).astype(MXU_DTYPE)
    G["w_in"] = _matmul(u0, dproj, mode="tn", out_dtypes=[F32], tn=896, name="d_in_w")
    du0 = _matmul(dproj, W["w_in"], mode="nt", out_dtypes=[F32], tk=896, name="d_in_act")
    dx, _, dg_mix0, _ = _rmsnorm_bwd(du0, x, row(W["norm_mix_g"][0]), dh1, name="d_norm_mix0")
    G["norm_mix_g"] = jnp.concatenate([dg_mix0, dg_mix1], axis=0)
    return loss, dx, G, (riding, got)


BIG = (("w_out_ab", "w_out", "rows2d"), ("conv_w_glu", "w_glu", "cols2d"), ("conv_w_pw", "w_pw", "rows2d"),
       ("w_ff1", "w_ff1", "cols3d"), ("w_ff2", "w_ff2", "rows3d"), ("w_in_ab", "w_in", "cols2d"))
LATE = BIG[:-1]
SMALL_SHARDED = ("conv_b_glu", "conv_w_dw", "conv_b_dw", "conv_ln_g", "conv_ln_b", "conv_b_pw")
REPLICATED = ("norm_mix_g", "norm_ffn_g", "hg_lb_logits", "hg_norm_g", "final_norm_g")
ORDER = ("norm_mix_g", "norm_ffn_g", "w_in_ab", "w_out_ab", "hg_lb_logits", "hg_norm_g", "conv_w_glu", "conv_b_glu",
         "conv_w_dw", "conv_b_dw", "conv_ln_g", "conv_ln_b", "conv_w_pw", "conv_b_pw", "w_ff1", "w_ff2", "final_norm_g")


def _step(x, loss_target, w, m, v):
    D = D_MODEL
    x2, t2 = x.reshape(-1, D), loss_target.reshape(-1, D)
    chip = 2 * lax.axis_index("x") + lax.axis_index("y")
    c = lax.axis_index("c")

    small_in = jnp.concatenate([w["conv_b_glu"].reshape(2, 256), w["conv_w_dw"].reshape(CONV_WIDTH, 256)] +
                               [w[n].reshape(1, 256) for n in ("conv_b_dw", "conv_ln_g", "conv_ln_b", "conv_b_pw")], axis=0)
    g_in, gs = _gather_chips([w["w_in_ab"].astype(MXU_DTYPE), _pad_rows(small_in, 40)]).run("gather_first_weights")
    vec = lambda r0, r1: gs[:, r0:r1].transpose(1, 0, 2).reshape(r1 - r0, N_CHIPS * 256)
    W = {
        "w_in": _cols_from_chips(g_in[:, 0]),
        "b_glu": gs[:, 0:2].reshape(2 * D), "w_dw": vec(2, 33), "b_dw": vec(33, 34)[0], "ln_g": vec(34, 35)[0],
        "ln_b": vec(35, 36)[0], "b_pw": vec(36, 37)[0],
        "norm_mix_g": w["norm_mix_g"], "norm_ffn_g": w["norm_ffn_g"], "hg_lb_logits": w["hg_lb_logits"],
        "hg_norm_g": w["hg_norm_g"], "final_norm_g": w["final_norm_g"],
    }
    late = _gather_chips([w[n].astype(MXU_DTYPE) for n, _, _ in LATE])

    def assemble(got):
        gw = dict(zip([s for _, s, _ in LATE], late.finish(got)))
        return {"w_out": _rows_from_chips(gw["w_out"][:, 0]), "w_glu": _cols_from_chips(gw["w_glu"][:, 0]),
                "w_pw": _rows_from_chips(gw["w_pw"][:, 0]), "w_ff1": _cols_from_chips(gw["w_ff1"]), "w_ff2": _rows_from_chips(gw["w_ff2"])}

    def pair_sum(blocks, tag):
        from_pair = _pair_swap(blocks.transpose(1, 0, 2, 3)).run("grads_pair_swap_" + tag)
        mine = lax.dynamic_index_in_dim(blocks, c, axis=1, keepdims=False)
        return _add_pair(mine, from_pair, name="grads_pair_add_" + tag)

    def early_grads(G):
        blocks = jnp.concatenate([_grad_blocks(G[s], kind) for _, s, kind in LATE], axis=2)
        return _chip_scatter(pair_sum(blocks, "late"))

    loss, dx, G, (riding, got) = _forward_backward(x2, t2, W, (late, assemble), early_grads)
    half_late = _sum_leading(riding.finish(got), name="grads_chip_add_late")
    per_chip = _chip_scatter(pair_sum(_grad_blocks(G["w_in"], "cols2d"), "in")).run("grads_chip_scatter_in")
    half = jnp.concatenate([half_late, _sum_leading(per_chip, name="grads_chip_add_in")], axis=0)
    full = _pair_gather(half).run("grads_pair_gather")

    pack = jnp.concatenate([
        G["norm_mix_g"], G["norm_ffn_g"], G["final_norm_g"], jnp.concatenate([G["hg_norm_g"], G["hg_lb"]], axis=1),
        _pad_rows(jnp.broadcast_to(loss, (1, D)), 2), G["b_glu"].reshape(2, D), G["w_dw"], G["b_dw"], G["ln_g"], G["ln_b"], G["b_pw"],
    ], axis=0)
    pack = _pad_rows(pack, SMALL_ROWS)
    (packs,) = _gather_all([pack]).run("gather_small_grads")
    ssum, d_logits = _small_reduce(packs, w["hg_lb_logits"], name="reduce_small_grads")
    cut = lambda r0, r1: lax.dynamic_slice(ssum, (r0, chip * 256), (r1 - r0, 256))
    grads = {
        "norm_mix_g": ssum[0:2], "norm_ffn_g": ssum[2:4], "final_norm_g": ssum[4], "hg_norm_g": ssum[5, :HG_WIDTH].reshape(1, HG_HEADS, HG_DH),
        "hg_lb_logits": d_logits,
        "conv_b_glu": lax.dynamic_slice(ssum[8:10].reshape(1, 2 * D), (0, chip * 512), (1, 512)),
        "conv_w_dw": cut(10, 10 + CONV_WIDTH).reshape(1, CONV_WIDTH, 256),
        "conv_b_dw": cut(42, 43), "conv_ln_g": cut(43, 44), "conv_ln_b": cut(44, 45), "conv_b_pw": cut(45, 46),
    }
    loss_out = ssum[6, 0]

    off = 0
    for n, s, kind in BIG:
        shard = w[n].shape
        rows = w[n].size // (2 * D)
        grads[n] = full[:, off:off + rows].reshape(shard)
        off += rows

    delta, new_m, new_v = {}, {}, {}
    for n, _, _ in BIG:
        view = lambda a: a.reshape(-1, a.shape[-1])
        outs = _adamw(view(w[n]), view(grads[n]), view(m[n]), view(v[n]), name="adamw_" + n)
        delta[n], new_m[n], new_v[n] = (o.reshape(w[n].shape) for o in outs)
    small = SMALL_SHARDED + REPLICATED
    sizes = [w[n].size for n in small]
    total = sum(sizes)
    rows = -(-total // (8 * D)) * 8
    packed = lambda d: _pad_rows(jnp.concatenate([d[n].reshape(-1) for n in small]).reshape(-1, 128), rows * 8).reshape(rows, D)
    outs = _adamw(packed(w), packed(grads), packed(m), packed(v), name="adamw_small")
    off = 0
    for n, size in zip(small, sizes):
        delta[n], new_m[n], new_v[n] = (o.reshape(-1)[off:off + size].reshape(w[n].shape) for o in outs)
        off += size
    grads = {n: grads[n].reshape(w[n].shape) for n in ORDER}
    return (loss_out, dx.reshape(x.shape), *[grads[n] for n in ORDER], *[delta[n] for n in ORDER],
            *[new_m[n] for n in ORDER], *[new_v[n] for n in ORDER])


def kernel(x, norm_mix_g, norm_ffn_g, w_in_ab, w_out_ab, hg_lb_logits, hg_norm_g, conv_w_glu, conv_b_glu, conv_w_dw, conv_b_dw, conv_ln_g, conv_ln_b, conv_w_pw, conv_b_pw, w_ff1, w_ff2, final_norm_g, loss_target, m_norm_mix_g, m_norm_ffn_g, m_w_in_ab, m_w_out_ab, m_hg_lb_logits, m_hg_norm_g, m_conv_w_glu, m_conv_b_glu, m_conv_w_dw, m_conv_b_dw, m_conv_ln_g, m_conv_ln_b, m_conv_w_pw, m_conv_b_pw, m_w_ff1, m_w_ff2, m_final_norm_g, v_norm_mix_g, v_norm_ffn_g, v_w_in_ab, v_w_out_ab, v_hg_lb_logits, v_hg_norm_g, v_conv_w_glu, v_conv_b_glu, v_conv_w_dw, v_conv_b_dw, v_conv_ln_g, v_conv_ln_b, v_conv_w_pw, v_conv_b_pw, v_w_ff1, v_w_ff2, v_final_norm_g):
    args = locals()
    w = {n: args[n] for n in ORDER}
    m = {n: args["m_" + n] for n in ORDER}
    v = {n: args["v_" + n] for n in ORDER}
    return _step(x, loss_target, w, m, v)
```

```python
import functools

import jax
import jax.numpy as jnp
from jax import lax
from jax.experimental import pallas as pl
from jax.experimental.pallas import tpu as pltpu

F32 = jnp.float32
MXU_DTYPE = jnp.bfloat16
MESH = pl.DeviceIdType.MESH

D_MODEL = 1024
SB_HEADS, SB_DH, SB_WIDTH = 8, 64, 512
SB_KEYS = 512
SB_SUB = 256
SB_ROWS_FWD, SB_ROWS_BWD = 512, 256
HG_HEADS, HG_DH, HG_WIDTH = 4, 128, 512
HG_CHUNK = 16
HG_TOKENS = 256
IN_WIDTH = 3 * SB_WIDTH + 4 * HG_WIDTH
CONV_WIDTH = 31
CONV_HALO = 32
CONV_ROWS = 32
D_FF = 4096
RMS_EPS = 1e-6
LN_EPS = 1e-5
N_CHIPS = 4
N_DEV = 8
SMALL_ROWS = 48
VMEM_LIMIT = 56 * 1024 * 1024

ADAM_LR, ADAM_B1, ADAM_B2, ADAM_EPS, ADAM_WD, ADAM_STEP = 0.001, 0.9, 0.999, 1e-08, 0.01, 10


def _params(*sem):
    return pltpu.CompilerParams(dimension_semantics=sem, vmem_limit_bytes=VMEM_LIMIT)


def _mx(v):
    return v.astype(MXU_DTYPE)


def _dot(a, b):
    return jnp.dot(_mx(a), _mx(b), preferred_element_type=F32)


def _dot_nt(a, b):
    return lax.dot_general(_mx(a), _mx(b), (((1,), (1,)), ((), ())), preferred_element_type=F32)


def _dot_tn(a, b):
    return lax.dot_general(_mx(a), _mx(b), (((0,), (0,)), ((), ())), preferred_element_type=F32)


def _neg_abs(x):
    bits = lax.bitcast_convert_type(x, jnp.uint32) | jnp.uint32(0x80000000)
    return lax.bitcast_convert_type(bits, F32)


def _key_order_sums(v, tri2, later):
    hi = _mx(v)
    lo = _mx(v - hi.astype(F32))
    n = SB_KEYS // SB_SUB
    blocks = [slice(b * SB_SUB, (b + 1) * SB_SUB) for b in range(n)]
    totals = [jnp.sum(v[:, sl], axis=1, keepdims=True) for sl in blocks]
    sums = []
    for b, sl in enumerate(blocks):
        s = jnp.dot(jnp.concatenate([hi[:, sl], lo[:, sl]], axis=1), tri2, preferred_element_type=F32)
        for o in (range(b + 1, n) if later else range(b)):
            s = s + totals[o]
        sums.append(s)
    return jnp.concatenate(sums, axis=1), functools.reduce(lambda a, b: a + b, totals)


class _ChipWeight:
    def __init__(self, parts, along, lead=()):
        self.parts, self.along, self.lead = parts, along, tuple(lead)
        r, c = parts.shape[-2:]
        self.shape = (r, N_CHIPS * c) if along == "cols" else (N_CHIPS * r, c)

    def _gathered_is_n(self, mode):
        return (self.along == "cols") == (mode in ("nn", "tn"))

    def tile(self, mode, tn, tk):
        r, c = self.parts.shape[-2:]
        part = c if self.along == "cols" else r
        return (part, tk) if self._gathered_is_n(mode) else (tn, part)

    def spec(self, mode, tn, tk):
        squeezed = (None,) * (1 + len(self.lead))
        lead, cols, by_n = self.lead, self.along == "cols", self._gathered_is_n(mode)
        block = (tn, tk) if mode == "nt" else (tk, tn)

        def index(i, j, k):
            chip, other = (j, k) if by_n else (k, j)
            return (chip,) + lead + ((other, 0) if cols else (0, other))

        return pl.BlockSpec(squeezed + block, index)


def _matmul(a, b, *, mode, out_dtypes, epilogue=None, tiles=(), rows=(), tm=1024, tn=1024, tk=1024, name):
    b_shape = b.shape
    if mode == "nn":
        (M, K), N = a.shape, b_shape[1]
    elif mode == "nt":
        (M, K), N = a.shape, b_shape[0]
    else:
        (K, M), N = a.shape, b_shape[1]
    if isinstance(b, _ChipWeight):
        tn, tk = b.tile(mode, tn, tk)
    tm, tn, tk = min(tm, M), min(tn, N), min(tk, K)
    assert M % tm == 0 and N % tn == 0 and K % tk == 0, (name, M, N, K)
    nk = K // tk
    a_spec = pl.BlockSpec((tk, tm), lambda i, j, k: (k, i)) if mode == "tn" else pl.BlockSpec((tm, tk), lambda i, j, k: (i, k))
    if isinstance(b, _ChipWeight):
        b_spec, b = b.spec(mode, tn, tk), b.parts
    else:
        b_spec = pl.BlockSpec((tn, tk), lambda i, j, k: (j, k)) if mode == "nt" else pl.BlockSpec((tk, tn), lambda i, j, k: (k, j))
    dims = {"nn": ((1,), (0,)), "nt": ((1,), (1,)), "tn": ((0,), (0,))}[mode]
    n_t, n_r, n_o = len(tiles), len(rows), len(out_dtypes)
    if epilogue is None:
        epilogue = lambda acc: (acc,)

    def body(a_ref, b_ref, *rest):
        extra, outs, acc_ref = rest[:n_t + n_r], rest[n_t + n_r:n_t + n_r + n_o], rest[-1]
        k = pl.program_id(2)

        @pl.when(k == 0)
        def _():
            acc_ref[...] = jnp.zeros_like(acc_ref)

        acc_ref[...] += lax.dot_general(_mx(a_ref[...]), _mx(b_ref[...]), (dims, ((), ())), preferred_element_type=F32)

        @pl.when(k == nk - 1)
        def _():
            res = epilogue(acc_ref[...], *[e[...] for e in extra])
            for o_ref, r in zip(outs, res):
                o_ref[...] = r.astype(o_ref.dtype)

    tile_spec = pl.BlockSpec((tm, tn), lambda i, j, k: (i, j))
    row_spec = pl.BlockSpec((1, tn), lambda i, j, k: (0, j))
    outs = pl.pallas_call(
        body, grid=(M // tm, N // tn, nk),
        in_specs=[a_spec, b_spec] + [tile_spec] * n_t + [row_spec] * n_r,
        out_specs=[tile_spec] * n_o,
        out_shape=[jax.ShapeDtypeStruct((M, N), dt) for dt in out_dtypes],
        scratch_shapes=[pltpu.VMEM((tm, tn), F32)],
        compiler_params=_params("parallel", "parallel", "arbitrary"), name=name,
    )(a, b, *tiles, *rows)
    return outs[0] if n_o == 1 else outs


def _token_block(T):
    return min(512, T)


def _rmsnorm_fwd(h, g, *, name):
    T, D = h.shape
    tb = _token_block(T)

    def body(h_ref, g_ref, u_ref):
        x = h_ref[...]
        r = lax.rsqrt(jnp.mean(x * x, axis=-1, keepdims=True) + RMS_EPS)
        u_ref[...] = (x * r * g_ref[...]).astype(u_ref.dtype)

    blk = pl.BlockSpec((tb, D), lambda i: (i, 0))
    return pl.pallas_call(
        body, grid=(T // tb,), in_specs=[blk, pl.BlockSpec((1, D), lambda i: (0, 0))], out_specs=blk,
        out_shape=jax.ShapeDtypeStruct((T, D), MXU_DTYPE), compiler_params=_params("parallel"), name=name,
    )(h, g)


def _rms_bwd_math(x, g, du):
    r = lax.rsqrt(jnp.mean(x * x, axis=-1, keepdims=True) + RMS_EPS)
    gd = g * du
    dx = r * gd - x * (r * r * r) * jnp.mean(gd * x, axis=-1, keepdims=True)
    return dx, du * x * r


def _rmsnorm_bwd(du, h, g, dres, *, name):
    T, D = h.shape
    tb = _token_block(T)

    def body(du_ref, h_ref, g_ref, dres_ref, dh_ref, dhm_ref, dg_ref, cs_ref):
        @pl.when(pl.program_id(0) == 0)
        def _():
            dg_ref[...] = jnp.zeros_like(dg_ref)
            cs_ref[...] = jnp.zeros_like(cs_ref)

        dx, dg_terms = _rms_bwd_math(h_ref[...], g_ref[...], du_ref[...])
        dh = dres_ref[...] + dx
        dh_ref[...] = dh
        dhm_ref[...] = dh.astype(dhm_ref.dtype)
        dg_ref[...] += jnp.sum(dg_terms, axis=0, keepdims=True)
        cs_ref[...] += jnp.sum(dh, axis=0, keepdims=True)

    blk = pl.BlockSpec((tb, D), lambda i: (i, 0))
    row = pl.BlockSpec((1, D), lambda i: (0, 0))
    return pl.pallas_call(
        body, grid=(T // tb,), in_specs=[blk, blk, row, blk], out_specs=[blk, blk, row, row],
        out_shape=[jax.ShapeDtypeStruct((T, D), F32), jax.ShapeDtypeStruct((T, D), MXU_DTYPE), jax.ShapeDtypeStruct((1, D), F32),
                   jax.ShapeDtypeStruct((1, D), F32)],
        compiler_params=_params("arbitrary"), name=name,
    )(du, h, g, dres)


def _loss_head(h, g, target, *, name):
    T, D = h.shape
    tb = _token_block(T)

    def body(h_ref, g_ref, t_ref, dh_ref, dhm_ref, dg_ref, loss_ref):
        @pl.when(pl.program_id(0) == 0)
        def _():
            dg_ref[...] = jnp.zeros_like(dg_ref)
            loss_ref[...] = jnp.zeros_like(loss_ref)

        x, gg = h_ref[...], g_ref[...]
        r = lax.rsqrt(jnp.mean(x * x, axis=-1, keepdims=True) + RMS_EPS)
        diff = x * r * gg - t_ref[...]
        per_token = jnp.mean(diff * diff, axis=-1, keepdims=True)
        loss_ref[...] += 0.5 * jnp.sum(per_token, axis=0, keepdims=True)
        dx, dg_terms = _rms_bwd_math(x, gg, diff / D)
        dh_ref[...] = dx
        dhm_ref[...] = dx.astype(dhm_ref.dtype)
        dg_ref[...] += jnp.sum(dg_terms, axis=0, keepdims=True)

    blk = pl.BlockSpec((tb, D), lambda i: (i, 0))
    row = pl.BlockSpec((1, D), lambda i: (0, 0))
    return pl.pallas_call(
        body, grid=(T // tb,), in_specs=[blk, row, blk], out_specs=[blk, blk, row, pl.BlockSpec((1, 1), lambda i: (0, 0))],
        out_shape=[jax.ShapeDtypeStruct((T, D), F32), jax.ShapeDtypeStruct((T, D), MXU_DTYPE), jax.ShapeDtypeStruct((1, D), F32),
                   jax.ShapeDtypeStruct((1, 1), F32)],
        compiler_params=_params("arbitrary"), name=name,
    )(h, g, target)


def _sb_scores(qm, ks, later, tri, mask):
    z = _dot_nt(qm, ks)
    sp = jnp.maximum(z, 0.0) + jnp.log(1.0 + jnp.exp(_neg_abs(z)))
    lb = z - sp
    if mask is not None:
        sp = jnp.where(mask, sp, 0.0)
    after, total = _key_order_sums(sp, tri, later=True)
    w = jnp.exp(lb - (after + later))
    if mask is not None:
        w = jnp.where(mask, w, 0.0)
    return total, lb, w


def _sb_setup(q_ref, rows):
    i, hsel = pl.program_id(1), pl.program_id(2)
    lane = lax.broadcasted_iota(jnp.int32, (rows, 2 * SB_DH), 1)
    mine = (lane >= SB_DH) == (hsel == 1)
    diag = (i * rows) // SB_KEYS
    t = i * rows + lax.broadcasted_iota(jnp.int32, (rows, SB_KEYS), 0)
    s = diag * SB_KEYS + lax.broadcasted_iota(jnp.int32, (rows, SB_KEYS), 1)
    a = lax.broadcasted_iota(jnp.int32, (2 * SB_SUB, SB_SUB), 0) % SB_SUB
    b = lax.broadcasted_iota(jnp.int32, (2 * SB_SUB, SB_SUB), 1)
    return i, hsel, mine, diag, s < t, _mx(a > b), _mx(a < b)


def _sb_keys(j):
    return pl.ds(pl.multiple_of(j * SB_KEYS, SB_KEYS), SB_KEYS)


def _sb_descend(n, step, carry):
    carry = lax.fori_loop(0, n // 2, lambda it, cr: step(n - 2 - 2 * it, step(n - 1 - 2 * it, cr)), carry)
    return lax.cond(n % 2 == 1, lambda cr: step(0, cr), lambda cr: cr, carry)


def _sb_ascend(n, step, carry):
    odd = n % 2
    carry = lax.cond(odd == 1, lambda cr: step(0, cr), lambda cr: cr, carry)
    return lax.fori_loop(0, n // 2, lambda it, cr: step(odd + 2 * it + 1, step(odd + 2 * it, cr)), carry)


def _sb_call(body, qkv, extra_in, out_blocks, scratch, rider, rows, *, name):
    T = qkv.shape[0]
    n_pairs = SB_HEADS // 2
    grid = (n_pairs, T // rows, 2)
    pair = lambda col0: pl.BlockSpec((rows, 2 * SB_DH), lambda p, i, h: (i, col0 + p))
    whole = lambda col0: pl.BlockSpec((T, 2 * SB_DH), lambda p, i, h: (0, col0 + p))
    shape = jax.ShapeDtypeStruct((T, SB_WIDTH), F32)
    in_specs = [pair(0), whole(n_pairs), whole(2 * n_pairs)] + [pair(0)] * len(extra_in)
    out_specs = [pair(0) if kind == "pair" else whole(0) for kind in out_blocks]
    n_in, n_out, n_r = len(in_specs), len(out_specs), 0 if rider is None else len(rider.arrays)

    def kernel_body(*refs):
        ins, r_in = refs[:n_in], refs[n_in:n_in + n_r]
        outs, r_out = refs[n_in + n_r:n_in + n_r + n_out], refs[n_in + n_r + n_out:n_in + 2 * n_r + n_out]
        rest = refs[n_in + 2 * n_r + n_out:]
        ids = [pl.program_id(a) for a in range(3)]
        if rider is not None:
            @pl.when((ids[0] == 0) & (ids[1] == 0) & (ids[2] == 0))
            def _():
                for cp in rider.make(r_in, r_out, *rest[len(scratch):]):
                    cp.start()

        body(ins, outs, rest[:len(scratch)])
        if rider is not None:
            @pl.when((ids[0] == grid[0] - 1) & (ids[1] == grid[1] - 1) & (ids[2] == grid[2] - 1))
            def _():
                for cp in rider.make(r_in, r_out, *rest[len(scratch):]):
                    cp.wait()

    res = pl.pallas_call(
        kernel_body, grid=grid, in_specs=in_specs + [HBM_SPEC] * n_r, out_specs=out_specs + [HBM_SPEC] * n_r,
        out_shape=[shape] * n_out + ([] if rider is None else rider.out_shapes),
        scratch_shapes=list(scratch) + ([] if rider is None else rider.scratch()),
        compiler_params=_params("arbitrary", "arbitrary", "arbitrary"), name=name,
    )(qkv, qkv, qkv, *extra_in, *([] if rider is None else rider.arrays))
    return res[:n_out], res[n_out:]


def _sb_fwd(qkv, rider=None, *, name):
    rows = min(SB_ROWS_FWD, qkv.shape[0])
    scale = SB_DH ** -0.5

    def body(ins, outs, _):
        (q_ref, k_ref, v_ref), (o_ref,) = ins, outs
        i, hsel, mine, diag, mask, tri, _ = _sb_setup(q_ref, rows)
        qm = jnp.where(mine, q_ref[...], 0) * scale

        def tile(j, m, later, acc):
            total, _, w = _sb_scores(qm, k_ref[_sb_keys(j), :], later, tri, m)
            return later + total, acc + _dot(w, v_ref[_sb_keys(j), :])

        carry = tile(diag, mask, jnp.zeros((rows, 1), F32), jnp.zeros((rows, 2 * SB_DH), F32))
        _, acc = _sb_descend(diag, lambda j, cr: tile(j, None, *cr), carry)
        res = jnp.where(mine, acc, 0.0)

        @pl.when(hsel == 0)
        def _():
            o_ref[...] = res

        @pl.when(hsel == 1)
        def _():
            o_ref[...] += res

    (o,), got = _sb_call(body, qkv, [], ["pair"], [], rider, rows, name=name)
    return o, got


def _sb_bwd(qkv, dmix, rider=None, *, name):
    T = qkv.shape[0]
    rows = min(SB_ROWS_BWD, T)
    scale = SB_DH ** -0.5

    def body(ins, outs, scratch):
        (q_ref, k_ref, v_ref, do_ref), (dq_ref, dk_ref, dv_ref), (da_ref, beta_ref) = ins, outs, scratch
        i, hsel, mine, diag, mask, tri, tri_before = _sb_setup(q_ref, rows)

        @pl.when((i == 0) & (hsel == 0))
        def _():
            dk_ref[...] = jnp.zeros_like(dk_ref)
            dv_ref[...] = jnp.zeros_like(dv_ref)

        qm = jnp.where(mine, q_ref[...], 0) * scale
        do_m = _mx(jnp.where(mine, do_ref[...], 0.0))

        def weights(j, m, later):
            total, lb, w = _sb_scores(qm, k_ref[_sb_keys(j), :], later, tri, m)
            da_ref[j] = _dot_nt(do_m, v_ref[_sb_keys(j), :]) * w
            beta_ref[j] = jnp.exp(lb)
            dv_ref[_sb_keys(j), :] += _dot_tn(w, do_m)
            return later + total

        later = weights(diag, mask, jnp.zeros((rows, 1), F32))
        _sb_descend(diag, lambda j, c: weights(j, None, c), later)

        def logits(j, m, before, dq):
            da = da_ref[j]
            earlier, total = _key_order_sums(da, tri_before, later=False)
            dz = da - beta_ref[j] * (da + earlier + before)
            if m is not None:
                dz = jnp.where(m, dz, 0.0)
            dz = _mx(dz)
            dk_ref[_sb_keys(j), :] += _dot_tn(dz, qm)
            return before + total, dq + _dot(dz, k_ref[_sb_keys(j), :])

        carry = (jnp.zeros((rows, 1), F32), jnp.zeros((rows, 2 * SB_DH), F32))
        carry = _sb_ascend(diag, lambda j, cr: logits(j, None, *cr), carry)
        res = jnp.where(mine, logits(diag, mask, *carry)[1] * scale, 0.0)

        @pl.when(hsel == 0)
        def _():
            dq_ref[...] = res

        @pl.when(hsel == 1)
        def _():
            dq_ref[...] += res

    n_tiles = T // SB_KEYS
    scratch = [pltpu.VMEM((n_tiles, rows, SB_KEYS), F32), pltpu.VMEM((n_tiles, rows, SB_KEYS), F32)]
    return _sb_call(body, qkv, [dmix], ["pair", "whole", "whole"], scratch, rider, rows, name=name)


def _chunk_row(n):
    return lax.broadcasted_iota(jnp.int32, (n, HG_DH), 0) % HG_CHUNK


def _chunk_cumsum(x, row, reverse=False):
    n = x.shape[0]
    for sh in (1, 2, 4, 8):
        if reverse:
            x = x + jnp.where(row < HG_CHUNK - sh, pltpu.roll(x, n - sh, 0), 0.0)
        else:
            x = x + jnp.where(row >= sh, pltpu.roll(x, sh, 0), 0.0)
    return x


def _hg_lower_bound(logits_ref):
    lg = logits_ref[...]
    e = jnp.exp(lg - jnp.max(lg, axis=0, keepdims=True))
    return e[0:1, :] / jnp.sum(e, axis=0, keepdims=True)


def _hg_terms(fr, q, lb, row):
    sig = jax.nn.sigmoid(fr)
    f = lb + (1.0 - lb) * sig
    kk = 1.0 - f
    g = jnp.log(f)
    G = _chunk_cumsum(g, row)
    g_last = G + (_chunk_cumsum(g, row, reverse=True) - g)
    e_g, e_ng, e_lg = jnp.exp(G), jnp.exp(-G), jnp.exp(g_last - G)
    return dict(sig=sig, f=f, kk=kk, e_g=e_g, e_ng=e_ng, e_lg=e_lg, q_dec=q * e_g, k_intra=kk * e_ng,
                k_state=kk * e_lg, decay=jnp.exp(g_last))


def _hg_causal(n):
    t = lax.broadcasted_iota(jnp.int32, (n, n), 0)
    s = lax.broadcasted_iota(jnp.int32, (n, n), 1)
    return (s <= t) & (s // HG_CHUNK == t // HG_CHUNK)


def _chunks(a):
    return a.reshape(a.shape[0] // HG_CHUNK, HG_CHUNK, a.shape[1])


def _per_chunk(lhs, rhs, contract):
    return lax.dot_general(_mx(lhs), _mx(rhs), ((contract[:1], contract[1:]), ((0,), (0,))), preferred_element_type=F32)


def _hg_specs(T, tb, col0, order):
    return [pl.BlockSpec((tb, HG_WIDTH), functools.partial(lambda i, j: (order(i), j), j=col0 + j)) for j in range(4)]


def _hg_fwd(proj, logits, norm_g, *, name):
    T = proj.shape[0]
    tb = min(HG_TOKENS, T)
    nch = tb // HG_CHUNK

    def body(q_ref, f_ref, i_ref, gate_ref, lg_ref, ng_ref, out_ref, o_ref, s_ref, st_ref, inc_ref, dec_ref):
        @pl.when(pl.program_id(0) == 0)
        def _():
            st_ref[...] = jnp.zeros_like(st_ref)

        lb_all = _hg_lower_bound(lg_ref)
        row = _chunk_row(tb)
        causal = _hg_causal(tb)
        for hh in range(HG_HEADS):
            cols = slice(hh * HG_DH, (hh + 1) * HG_DH)
            t = _hg_terms(f_ref[:, cols], q_ref[:, cols], lb_all[:, cols], row)
            v = i_ref[:, cols]
            scores = jnp.where(causal, _dot_nt(t["q_dec"], t["k_intra"]), 0.0)
            o_intra = _dot(scores, v)
            inc_ref[...] = _per_chunk(_chunks(v), _chunks(t["k_state"]), (1, 1))
            dec_ref[...] = _chunks(t["decay"])

            def step(ci, st):
                s_ref[ci, hh] = st
                return st * dec_ref[ci][0:1, :] + inc_ref[ci]

            st_ref[hh] = lax.fori_loop(0, nch, step, st_ref[hh], unroll=4)
            o_inter = _per_chunk(_chunks(t["q_dec"]), s_ref[:, hh], (2, 2))
            o = o_intra + o_inter.reshape(tb, HG_DH)
            o_ref[:, cols] = o
            gate = gate_ref[:, cols]
            on = o * lax.rsqrt(jnp.mean(o * o, axis=-1, keepdims=True) + RMS_EPS) * ng_ref[:, cols]
            out_ref[:, cols] = (on * (gate * jax.nn.sigmoid(gate))).astype(out_ref.dtype)

    blk = pl.BlockSpec((tb, HG_WIDTH), lambda i: (i, 0))
    return pl.pallas_call(
        body, grid=(T // tb,),
        in_specs=_hg_specs(T, tb, 3, lambda i: i) + [pl.BlockSpec((3, HG_WIDTH), lambda i: (0, 0)), pl.BlockSpec((1, HG_WIDTH), lambda i: (0, 0))],
        out_specs=[blk, blk, pl.BlockSpec((nch, HG_HEADS, HG_DH, HG_DH), lambda i: (i, 0, 0, 0))],
        out_shape=[jax.ShapeDtypeStruct((T, HG_WIDTH), MXU_DTYPE), jax.ShapeDtypeStruct((T, HG_WIDTH), F32),
                   jax.ShapeDtypeStruct((T // HG_CHUNK, HG_HEADS, HG_DH, HG_DH), F32)],
        scratch_shapes=[pltpu.VMEM((HG_HEADS, HG_DH, HG_DH), F32), pltpu.VMEM((nch, HG_DH, HG_DH), F32),
                        pltpu.VMEM((nch, HG_CHUNK, HG_DH), F32)],
        compiler_params=_params("arbitrary"), name=name,
    )(proj, proj, proj, proj, logits, norm_g)


def _hg_bwd(proj, o_raw, states, dmix, logits, norm_g, *, name):
    T = proj.shape[0]
    tb = min(HG_TOKENS, T)
    nch = tb // HG_CHUNK
    nb = T // tb
    rev = lambda i: nb - 1 - i

    def body(q_ref, f_ref, i_ref, gate_ref, o_ref, s_ref, dout_ref, lg_ref, ng_ref, dp_ref, dlb_ref, dng_ref,
             dst_ref, inc_ref, dec_ref, after_ref):
        @pl.when(pl.program_id(0) == 0)
        def _():
            dst_ref[...] = jnp.zeros_like(dst_ref)
            dlb_ref[...] = jnp.zeros_like(dlb_ref)
            dng_ref[...] = jnp.zeros_like(dng_ref)

        lb_all = _hg_lower_bound(lg_ref)
        row = _chunk_row(tb)
        causal = _hg_causal(tb)
        for hh in range(HG_HEADS):
            cols = slice(hh * HG_DH, (hh + 1) * HG_DH)
            out_cols = lambda part: slice(part * HG_WIDTH + hh * HG_DH, part * HG_WIDTH + (hh + 1) * HG_DH)
            o, gate, dout, ng, lb = o_ref[:, cols], gate_ref[:, cols], dout_ref[:, cols], ng_ref[:, cols], lb_all[:, cols]
            sg = jax.nn.sigmoid(gate)
            r = lax.rsqrt(jnp.mean(o * o, axis=-1, keepdims=True) + RMS_EPS)
            oh = o * r
            dp_ref[:, out_cols(3)] = dout * (oh * ng) * (sg * (1.0 + gate * (1.0 - sg)))
            don = dout * (gate * sg)
            dng_ref[:, cols] += jnp.sum(don * oh, axis=0, keepdims=True)
            doh = don * ng
            do = r * (doh - oh * jnp.mean(doh * oh, axis=-1, keepdims=True))

            t = _hg_terms(f_ref[:, cols], q_ref[:, cols], lb, row)
            v = i_ref[:, cols]
            scores = jnp.where(causal, _dot_nt(t["q_dec"], t["k_intra"]), 0.0)
            dscores = jnp.where(causal, _dot_nt(do, v), 0.0)
            inc_ref[...] = _per_chunk(_chunks(do), _chunks(t["q_dec"]), (1, 1))
            dec_ref[...] = _chunks(t["decay"])

            def step(it, dst):
                ci = nch - 1 - it
                after_ref[ci] = dst
                return dst * dec_ref[ci][0:1, :] + inc_ref[ci]

            dst_ref[hh] = lax.fori_loop(0, nch, step, dst_ref[hh], unroll=4)
            st, dst = s_ref[:, hh], after_ref[...]
            dqd = _dot(dscores, t["k_intra"]) + _per_chunk(_chunks(do), st, (2, 1)).reshape(tb, HG_DH)
            dki = _dot_tn(dscores, t["q_dec"])
            dks = _per_chunk(_chunks(v), dst, (2, 1)).reshape(tb, HG_DH)
            dp_ref[:, out_cols(2)] = _dot_tn(scores, do) + _per_chunk(_chunks(t["k_state"]), dst, (2, 2)).reshape(tb, HG_DH)
            ddecay = jnp.broadcast_to(jnp.sum(st * dst, axis=1, keepdims=True), (nch, HG_CHUNK, HG_DH)).reshape(tb, HG_DH)
            dks_ks = dks * t["k_state"]
            d_glast = _chunk_cumsum(dks_ks, row) + ddecay * t["decay"]
            d_g = dqd * t["q_dec"] - dki * t["k_intra"] - dks_ks + jnp.where(row == HG_CHUNK - 1, d_glast, 0.0)
            df = _chunk_cumsum(d_g, row, reverse=True) / t["f"] - (dki * t["e_ng"] + dks * t["e_lg"])
            dp_ref[:, out_cols(0)] = dqd * t["e_g"]
            dp_ref[:, out_cols(1)] = df * (1.0 - lb) * t["sig"] * (1.0 - t["sig"])
            dlb_ref[:, cols] += jnp.sum(df * (1.0 - t["sig"]), axis=0, keepdims=True)

    blk = pl.BlockSpec((tb, HG_WIDTH), lambda i: (rev(i), 0))
    row_spec = pl.BlockSpec((1, HG_WIDTH), lambda i: (0, 0))
    return pl.pallas_call(
        body, grid=(nb,),
        in_specs=_hg_specs(T, tb, 3, rev) + [
            blk, pl.BlockSpec((nch, HG_HEADS, HG_DH, HG_DH), lambda i: (rev(i), 0, 0, 0)),
            pl.BlockSpec((tb, HG_WIDTH), lambda i: (rev(i), 1)), pl.BlockSpec((3, HG_WIDTH), lambda i: (0, 0)), row_spec],
        out_specs=[pl.BlockSpec((tb, 4 * HG_WIDTH), lambda i: (rev(i), 0)), row_spec, row_spec],
        out_shape=[jax.ShapeDtypeStruct((T, 4 * HG_WIDTH), F32), jax.ShapeDtypeStruct((1, HG_WIDTH), F32), jax.ShapeDtypeStruct((1, HG_WIDTH), F32)],
        scratch_shapes=[pltpu.VMEM((HG_HEADS, HG_DH, HG_DH), F32), pltpu.VMEM((nch, HG_DH, HG_DH), F32),
                        pltpu.VMEM((nch, HG_CHUNK, HG_DH), F32), pltpu.VMEM((nch, HG_DH, HG_DH), F32)],
        compiler_params=_params("arbitrary"), name=name,
    )(proj, proj, proj, proj, o_raw, states, dmix, logits, norm_g)


def _shifted_copies(sh_ref, n_rows):
    keep = n_rows + CONV_HALO - 8
    for b in range(1, 8):
        sh_ref[b, 0:keep, :] = sh_ref[0, b:b + keep, :]


def _tap_rows(sh_ref, offset, r0, lanes):
    start = pl.multiple_of(r0 + (offset - offset % 8), 8)
    return sh_ref[offset % 8, pl.ds(start, CONV_ROWS), lanes]


def _conv_fwd(p, w_dw, b_dw, ln_g, ln_b, *, name):
    T, D = p.shape[0], p.shape[1] // 2
    tb = _token_block(T)
    hpb = tb // CONV_HALO
    lane_step = 512

    def body(p1_ref, p2_ref, q1_ref, q2_ref, w_ref, bdw_ref, g_ref, b_ref, a_ref, y_ref, act_ref, sh_ref):
        i = pl.program_id(0)
        a = p1_ref[...] * jax.nn.sigmoid(p2_ref[...])
        sh_ref[0, 0:CONV_HALO, :] = jnp.where(i > 0, q1_ref[...] * jax.nn.sigmoid(q2_ref[...]), 0.0)
        sh_ref[0, CONV_HALO:, :] = a
        a_ref[...] = a
        _shifted_copies(sh_ref, tb)

        def chunk(ci, _):
            r0 = pl.multiple_of(ci * CONV_ROWS, CONV_ROWS)
            for l0 in range(0, D, lane_step):
                lanes = slice(l0, l0 + lane_step)
                acc = jnp.broadcast_to(bdw_ref[:, lanes], (CONV_ROWS, lane_step))
                for k in range(CONV_WIDTH):
                    acc = acc + _tap_rows(sh_ref, CONV_HALO - CONV_WIDTH + 1 + k, r0, lanes) * w_ref[k:k + 1, lanes]
                y_ref[pl.ds(r0, CONV_ROWS), lanes] = acc
            return 0

        lax.fori_loop(0, tb // CONV_ROWS, chunk, 0)
        y = y_ref[...]
        mu = jnp.mean(y, axis=-1, keepdims=True)
        yc = y - mu
        s = yc * lax.rsqrt(jnp.mean(yc * yc, axis=-1, keepdims=True) + LN_EPS) * g_ref[...] + b_ref[...]
        act_ref[...] = (s * jax.nn.sigmoid(s)).astype(act_ref.dtype)

    prev = lambda i: jnp.maximum(i * hpb - 1, 0)
    blk = pl.BlockSpec((tb, D), lambda i: (i, 0))
    row = pl.BlockSpec((1, D), lambda i: (0, 0))
    return pl.pallas_call(
        body, grid=(T // tb,),
        in_specs=[blk, pl.BlockSpec((tb, D), lambda i: (i, 1)), pl.BlockSpec((CONV_HALO, D), lambda i: (prev(i), 0)),
                  pl.BlockSpec((CONV_HALO, D), lambda i: (prev(i), 1)), pl.BlockSpec((CONV_HALO, D), lambda i: (0, 0)), row, row, row],
        out_specs=[blk, blk, blk],
        out_shape=[jax.ShapeDtypeStruct((T, D), F32), jax.ShapeDtypeStruct((T, D), F32), jax.ShapeDtypeStruct((T, D), MXU_DTYPE)],
        scratch_shapes=[pltpu.VMEM((8, tb + CONV_HALO, D), F32)],
        compiler_params=_params("parallel"), name=name,
    )(p, p, p, p, w_dw, b_dw, ln_g, ln_b)


def _conv_bwd_norm(dact, y, ln_g, ln_b, *, name):
    T, D = y.shape
    tb = _token_block(T)

    def body(da_ref, y_ref, g_ref, b_ref, dy_ref, dg_ref, db_ref, cs_ref):
        @pl.when(pl.program_id(0) == 0)
        def _():
            dg_ref[...] = jnp.zeros_like(dg_ref)
            db_ref[...] = jnp.zeros_like(db_ref)
            cs_ref[...] = jnp.zeros_like(cs_ref)

        y, g = y_ref[...], g_ref[...]
        yc = y - jnp.mean(y, axis=-1, keepdims=True)
        rs = lax.rsqrt(jnp.mean(yc * yc, axis=-1, keepdims=True) + LN_EPS)
        yn = yc * rs
        s = yn * g + b_ref[...]
        sg = jax.nn.sigmoid(s)
        ds = da_ref[...] * (sg * (1.0 + s * (1.0 - sg)))
        dg_ref[...] += jnp.sum(ds * yn, axis=0, keepdims=True)
        db_ref[...] += jnp.sum(ds, axis=0, keepdims=True)
        dyn = ds * g
        dy = rs * (dyn - jnp.mean(dyn, axis=-1, keepdims=True) - yn * jnp.mean(dyn * yn, axis=-1, keepdims=True))
        dy_ref[...] = dy
        cs_ref[...] += jnp.sum(dy, axis=0, keepdims=True)

    blk = pl.BlockSpec((tb, D), lambda i: (i, 0))
    row = pl.BlockSpec((1, D), lambda i: (0, 0))
    rs_ = jax.ShapeDtypeStruct((1, D), F32)
    return pl.pallas_call(
        body, grid=(T // tb,), in_specs=[blk, blk, row, row], out_specs=[blk, row, row, row],
        out_shape=[jax.ShapeDtypeStruct((T, D), F32), rs_, rs_, rs_], compiler_params=_params("arbitrary"), name=name,
    )(dact, y, ln_g, ln_b)


def _conv_bwd_taps(dy, a, p, w_dw, *, name):
    T, D = dy.shape
    tb = _token_block(T)
    hpb = tb // CONV_HALO
    last = T // CONV_HALO - 1
    nb = T // tb
    lane_step = 128
    groups = CONV_ROWS // 8

    def body(dy_ref, dyn_ref, a_ref, p1_ref, p2_ref, w_ref, dp_ref, dw_ref, cs_ref, sh_ref, da_ref):
        i = pl.program_id(0)

        @pl.when(i == 0)
        def _():
            dw_ref[...] = jnp.zeros_like(dw_ref)
            cs_ref[...] = jnp.zeros_like(cs_ref)

        sh_ref[0, 0:tb, :] = dy_ref[...]
        sh_ref[0, tb:, :] = jnp.where(i < nb - 1, dyn_ref[...], 0.0)
        _shifted_copies(sh_ref, tb)
        for l0 in range(0, D, lane_step):
            lanes = slice(l0, l0 + lane_step)

            def chunk(ci, sums):
                r0 = pl.multiple_of(ci * CONV_ROWS, CONV_ROWS)
                a_c = a_ref[pl.ds(r0, CONV_ROWS), lanes]
                da = jnp.zeros((CONV_ROWS, lane_step), F32)
                new = []
                for k in range(CONV_WIDTH):
                    s_k = _tap_rows(sh_ref, CONV_WIDTH - 1 - k, r0, lanes)
                    da = da + s_k * w_ref[k:k + 1, lanes]
                    new.append(sums[k] + jnp.sum((s_k * a_c).reshape(groups, 8, lane_step), axis=0))
                da_ref[pl.ds(r0, CONV_ROWS), lanes] = da
                return tuple(new)

            sums = lax.fori_loop(0, tb // CONV_ROWS, chunk, tuple(jnp.zeros((8, lane_step), F32) for _ in range(CONV_WIDTH)))
            for k in range(CONV_WIDTH):
                dw_ref[k:k + 1, lanes] += jnp.sum(sums[k], axis=0, keepdims=True)
        da = da_ref[...]
        p1 = p1_ref[...]
        sg = jax.nn.sigmoid(p2_ref[...])
        dp1 = da * sg
        dp2 = da * p1 * (sg * (1.0 - sg))
        dp_ref[:, 0:D] = dp1.astype(dp_ref.dtype)
        dp_ref[:, D:] = dp2.astype(dp_ref.dtype)
        cs_ref[:, 0:D] += jnp.sum(dp1, axis=0, keepdims=True)
        cs_ref[:, D:] += jnp.sum(dp2, axis=0, keepdims=True)

    blk = pl.BlockSpec((tb, D), lambda i: (i, 0))
    return pl.pallas_call(
        body, grid=(nb,),
        in_specs=[blk, pl.BlockSpec((CONV_HALO, D), lambda i: (jnp.minimum((i + 1) * hpb, last), 0)), blk, blk,
                  pl.BlockSpec((tb, D), lambda i: (i, 1)), pl.BlockSpec((CONV_HALO, D), lambda i: (0, 0))],
        out_specs=[pl.BlockSpec((tb, 2 * D), lambda i: (i, 0)), pl.BlockSpec((CONV_HALO, D), lambda i: (0, 0)), pl.BlockSpec((1, 2 * D), lambda i: (0, 0))],
        out_shape=[jax.ShapeDtypeStruct((T, 2 * D), MXU_DTYPE), jax.ShapeDtypeStruct((CONV_HALO, D), F32), jax.ShapeDtypeStruct((1, 2 * D), F32)],
        scratch_shapes=[pltpu.VMEM((8, tb + CONV_HALO, D), F32), pltpu.VMEM((tb, D), F32)],
        compiler_params=_params("arbitrary"), name=name,
    )(dy, dy, a, p, p, w_dw)


def _row_block(rows):
    for tr in (512, 256, 128, 64, 32, 16, 8):
        if rows % tr == 0:
            return tr
    return rows


def _sum_leading(x, *, name):
    n, R, C = x.shape
    tr = _row_block(R)

    def body(x_ref, o_ref):
        acc = x_ref[0]
        for j in range(1, n):
            acc = acc + x_ref[j]
        o_ref[...] = acc

    return pl.pallas_call(
        body, grid=(R // tr,), in_specs=[pl.BlockSpec((n, tr, C), lambda i: (0, i, 0))], out_specs=pl.BlockSpec((tr, C), lambda i: (i, 0)),
        out_shape=jax.ShapeDtypeStruct((R, C), x.dtype), compiler_params=_params("parallel"), name=name,
    )(x)


def _add_pair(x, y, *, name):
    n, R, C = x.shape
    tr = _row_block(R)

    def body(x_ref, y_ref, o_ref):
        o_ref[...] = x_ref[...] + y_ref[...]

    blk = pl.BlockSpec((1, tr, C), lambda j, i: (j, i, 0))
    return pl.pallas_call(
        body, grid=(n, R // tr), in_specs=[blk, blk], out_specs=blk,
        out_shape=jax.ShapeDtypeStruct((n, R, C), x.dtype), compiler_params=_params("parallel", "parallel"), name=name,
    )(x, y)


def _adamw(w, g, m, v, *, name):
    R, C = w.shape
    tr = _row_block(R)
    c1, c2 = 1.0 - ADAM_B1 ** ADAM_STEP, 1.0 - ADAM_B2 ** ADAM_STEP

    def body(w_ref, g_ref, m_ref, v_ref, d_ref, nm_ref, nv_ref):
        g_ = g_ref[...]
        nm = ADAM_B1 * m_ref[...] + (1.0 - ADAM_B1) * g_
        nv = ADAM_B2 * v_ref[...] + (1.0 - ADAM_B2) * (g_ * g_)
        d_ref[...] = -ADAM_LR * ((nm / c1) / (jnp.sqrt(nv / c2) + ADAM_EPS) + ADAM_WD * w_ref[...])
        nm_ref[...] = nm
        nv_ref[...] = nv

    blk = pl.BlockSpec((tr, C), lambda i: (i, 0))
    shp = jax.ShapeDtypeStruct((R, C), F32)
    return pl.pallas_call(
        body, grid=(R // tr,), in_specs=[blk] * 4, out_specs=[blk] * 3, out_shape=[shp] * 3,
        compiler_params=_params("parallel"), name=name,
    )(w, g, m, v)


def _small_reduce(packs, logits, *, name):
    n, R, C = packs.shape

    def body(p_ref, lg_ref, s_ref, dlg_ref):
        acc = p_ref[0]
        for j in range(1, n):
            acc = acc + p_ref[j]
        s_ref[...] = acc
        lg = lg_ref[...]
        e = jnp.exp(lg - jnp.max(lg, axis=0, keepdims=True))
        sm = e / jnp.sum(e, axis=0, keepdims=True)
        dlb = acc[5:6, HG_WIDTH:2 * HG_WIDTH]
        first = lax.broadcasted_iota(jnp.int32, sm.shape, 0) == 0
        dlg_ref[...] = sm[0:1, :] * (jnp.where(first, 1.0, 0.0) - sm) * dlb

    whole = lambda shape: pl.BlockSpec(shape, lambda: (0,) * len(shape))
    return pl.pallas_call(
        body, in_specs=[whole((n, R, C)), whole(logits.shape)], out_specs=[whole((R, C)), whole(logits.shape)],
        out_shape=[jax.ShapeDtypeStruct((R, C), F32), jax.ShapeDtypeStruct(logits.shape, F32)],
        compiler_params=pltpu.CompilerParams(vmem_limit_bytes=VMEM_LIMIT), name=name,
    )(packs, logits)


HBM_SPEC = pl.BlockSpec(memory_space=pl.ANY)


def _place():
    return lax.axis_index("x"), lax.axis_index("y"), lax.axis_index("c")


class _Copies:
    def __init__(self, arrays, out_shapes, n_copies, make, finish):
        self.arrays, self.out_shapes, self.n_copies, self.make, self.finish = list(arrays), list(out_shapes), n_copies, make, finish

    def scratch(self):
        return [pltpu.SemaphoreType.DMA((self.n_copies,)), pltpu.SemaphoreType.DMA((self.n_copies,))]

    def run(self, name):
        n = len(self.arrays)

        def body(*refs):
            copies = self.make(refs[:n], refs[n:2 * n], *refs[2 * n:])
            for cp in copies:
                cp.start()
            for cp in copies:
                cp.wait()

        outs = pl.pallas_call(body, in_specs=[HBM_SPEC] * n, out_specs=[HBM_SPEC] * n, out_shape=self.out_shapes,
                              scratch_shapes=self.scratch(), name=name)(*self.arrays)
        return self.finish(outs)


def _remote(src, dst, send_sems, recv_sems, k, peer):
    return pltpu.make_async_remote_copy(src_ref=src, dst_ref=dst, send_sem=send_sems.at[k], recv_sem=recv_sems.at[k],
                                        device_id=peer, device_id_type=MESH)


def _same_core_peers(x, y, c):
    return [(1 - x, y, c), (x, 1 - y, c), (1 - x, 1 - y, c)]


def _all_peers(x, y, c):
    flip = lambda v, b: 1 - v if b else v
    return [(flip(x, r & 4), flip(y, r & 2), flip(c, r & 1)) for r in range(1, 8)]


def _gather(arrays, peers_of, slot_of, n_slots):
    n_peers = len(peers_of(0, 0, 0))

    def make(ins, outs, send_sems, recv_sems):
        x, y, c = _place()
        slot = slot_of(x, y, c)
        return [_remote(ins[a], outs[a].at[slot], send_sems, recv_sems, a * n_peers + k, peer)
                for a in range(len(arrays)) for k, peer in enumerate(peers_of(x, y, c))]

    def finish(outs):
        slot = slot_of(*_place())
        return [lax.dynamic_update_index_in_dim(o, a, slot, 0) for o, a in zip(outs, arrays)]

    shapes = [jax.ShapeDtypeStruct((n_slots,) + a.shape, a.dtype) for a in arrays]
    return _Copies(arrays, shapes, len(arrays) * n_peers, make, finish)


def _gather_chips(arrays):
    return _gather(arrays, _same_core_peers, lambda x, y, c: 2 * x + y, N_CHIPS)


def _gather_all(arrays):
    return _gather(arrays, _all_peers, lambda x, y, c: 4 * x + 2 * y + c, N_DEV)


def _pair_swap(a):
    def make(ins, outs, send_sems, recv_sems):
        x, y, c = _place()
        return [_remote(ins[0].at[1 - c], outs[0], send_sems, recv_sems, 0, (x, y, 1 - c))]

    return _Copies([a], [jax.ShapeDtypeStruct(a.shape[1:], a.dtype)], 1, make, lambda outs: outs[0])


def _chip_scatter(p):
    def make(ins, outs, send_sems, recv_sems):
        x, y, c = _place()
        return [_remote(ins[0].at[2 * px + py], outs[0].at[2 * x + y], send_sems, recv_sems, k, (px, py, pc))
                for k, (px, py, pc) in enumerate(_same_core_peers(x, y, c))]

    def finish(outs):
        x, y, _ = _place()
        me = 2 * x + y
        return lax.dynamic_update_index_in_dim(outs[0], lax.dynamic_index_in_dim(p, me, 0, keepdims=False), me, 0)

    return _Copies([p], [jax.ShapeDtypeStruct(p.shape, p.dtype)], 3, make, finish)


def _owner_scatter(blocks):
    def make(ins, outs, send_sems, recv_sems):
        x, y, c = _place()
        return [_remote(ins[0].at[2 * px + py, pc], outs[0].at[4 * x + 2 * y + c], send_sems, recv_sems, k, (px, py, pc))
                for k, (px, py, pc) in enumerate(_all_peers(x, y, c))]

    def finish(outs):
        x, y, c = _place()
        mine = lax.dynamic_index_in_dim(lax.dynamic_index_in_dim(blocks, 2 * x + y, 0, keepdims=False), c, 0, keepdims=False)
        return lax.dynamic_update_index_in_dim(outs[0], mine, 4 * x + 2 * y + c, 0)

    return _Copies([blocks], [jax.ShapeDtypeStruct((N_DEV,) + blocks.shape[2:], blocks.dtype)], N_DEV - 1, make, finish)


def _pair_gather(q):
    def make(ins, outs, send_sems, recv_sems):
        x, y, c = _place()
        return [_remote(ins[0], outs[0].at[c], send_sems, recv_sems, 0, (x, y, 1 - c))]

    return _Copies([q], [jax.ShapeDtypeStruct((2,) + q.shape, q.dtype)], 1, make,
                   lambda outs: lax.dynamic_update_index_in_dim(outs[0], q, _place()[2], 0))


def _grad_blocks(dw, kind):
    if kind == "cols2d":
        K, N = dw.shape
        b = dw.reshape(2, K // 2, N_CHIPS, N // N_CHIPS).transpose(2, 0, 1, 3)
    elif kind == "rows2d":
        b = dw.reshape(N_CHIPS, 2, dw.shape[0] // 8, dw.shape[1])
    elif kind == "cols3d":
        L, K, N = dw.shape
        b = dw.reshape(L, K, N_CHIPS, N // N_CHIPS).transpose(2, 0, 1, 3)
    else:
        L, K, N = dw.shape
        b = dw.reshape(L, N_CHIPS, K // N_CHIPS, N).transpose(1, 0, 2, 3)
    return b.reshape(N_CHIPS, 2, -1, D_MODEL)


def _pad_rows(a, rows):
    return jnp.concatenate([a, jnp.zeros((rows - a.shape[0],) + a.shape[1:], a.dtype)], axis=0)


def _forward_backward(x, target, W, late_weights=None, early_grads=None):
    row = lambda a: a.reshape(1, -1)
    relu2 = lambda acc: (jnp.square(jnp.maximum(acc, 0.0)),)
    normed = lambda h, g: h * lax.rsqrt(jnp.mean(h * h, axis=-1, keepdims=True) + RMS_EPS) * g

    def residual_norm(acc, res, g):
        h = res + acc
        return h, normed(h, g)

    G = {}

    u0 = _rmsnorm_fwd(x, row(W["norm_mix_g"][0]), name="norm_mix0")
    proj, qkv = _matmul(u0, W["w_in"], mode="nn", out_dtypes=[F32, MXU_DTYPE], epilogue=lambda acc: (acc, acc), tn=896, name="in_proj")
    o_sb, got = _sb_fwd(qkv, late_weights and late_weights[0], name="sb_fwd")
    if late_weights:
        W = {**W, **late_weights[1](got)}
    hg_out, hg_o, hg_states = _hg_fwd(proj, W["hg_lb_logits"], row(W["hg_norm_g"]), name="hg_fwd")
    mix = jnp.concatenate([o_sb.astype(MXU_DTYPE), hg_out], axis=-1)
    h1, u1 = _matmul(mix, W["w_out"], mode="nn", out_dtypes=[F32, MXU_DTYPE], epilogue=residual_norm, tiles=[x],
                     rows=[row(W["norm_ffn_g"][0])], name="out_proj")
    r0 = _matmul(u1, W["w_ff1"][0], mode="nn", out_dtypes=[MXU_DTYPE], epilogue=relu2, name="ff1_0")
    h2, u2 = _matmul(r0, W["w_ff2"][0], mode="nn", out_dtypes=[F32, MXU_DTYPE], epilogue=residual_norm, tiles=[h1],
                     rows=[row(W["norm_mix_g"][1])], name="ff2_0")
    p = _matmul(u2, W["w_glu"], mode="nn", out_dtypes=[F32], epilogue=lambda acc, b: (acc + b,), rows=[row(W["b_glu"])], name="glu_proj")
    w_dw = _pad_rows(W["w_dw"], CONV_HALO)
    ca, cy, cact = _conv_fwd(p, w_dw, row(W["b_dw"]), row(W["ln_g"]), row(W["ln_b"]), name="conv_fwd")
    h3, u3 = _matmul(cact, W["w_pw"], mode="nn", out_dtypes=[F32, MXU_DTYPE], epilogue=lambda acc, res, b, g: residual_norm(acc + b, res, g),
                     tiles=[h2], rows=[row(W["b_pw"]), row(W["norm_ffn_g"][1])], name="pw_proj")
    r1 = _matmul(u3, W["w_ff1"][1], mode="nn", out_dtypes=[MXU_DTYPE], epilogue=relu2, name="ff1_1")
    h4 = _matmul(r1, W["w_ff2"][1], mode="nn", out_dtypes=[F32], epilogue=lambda acc, res: (res + acc,), tiles=[h3], name="ff2_1")

    dh4, dh4_m, G["final_norm_g"], loss = _loss_head(h4, row(W["final_norm_g"]), target, name="loss_head")

    def mlp_bwd(dh, dh_m, h_in, u, r, layer, tag):
        d_relu2 = lambda acc, r_blk: (acc * (2.0 * jnp.sqrt(r_blk.astype(F32))),)
        da = _matmul(dh_m, W["w_ff2"][layer], mode="nt", out_dtypes=[MXU_DTYPE], epilogue=d_relu2, tiles=[r], name="d_ff2_act" + tag)
        dw2 = _matmul(r, dh_m, mode="tn", out_dtypes=[F32], name="d_ff2_w" + tag)
        dw1 = _matmul(u, da, mode="tn", out_dtypes=[F32], name="d_ff1_w" + tag)
        du = _matmul(da, W["w_ff1"][layer], mode="nt", out_dtypes=[F32], name="d_ff1_act" + tag)
        dh_in, dh_in_m, dg, cs = _rmsnorm_bwd(du, h_in, row(W["norm_ffn_g"][layer]), dh, name="d_norm_ffn" + tag)
        return dh_in, dh_in_m, dg, cs, dw1, dw2

    dh3, dh3_m, dg_ffn1, cs_h3, dw1_1, dw2_1 = mlp_bwd(dh4, dh4_m, h3, u3, r1, 1, "1")
    G["b_pw"] = cs_h3
    dact = _matmul(dh3_m, W["w_pw"], mode="nt", out_dtypes=[F32], name="d_pw_act")
    G["w_pw"] = _matmul(cact, dh3_m, mode="tn", out_dtypes=[F32], name="d_pw_w")
    dy, G["ln_g"], G["ln_b"], G["b_dw"] = _conv_bwd_norm(dact, cy, row(W["ln_g"]), row(W["ln_b"]), name="d_conv_norm")
    dp, G["w_dw"], G["b_glu"] = _conv_bwd_taps(dy, ca, p, w_dw, name="d_conv_taps")
    G["w_glu"] = _matmul(u2, dp, mode="tn", out_dtypes=[F32], name="d_glu_w")
    du2 = _matmul(dp, W["w_glu"], mode="nt", out_dtypes=[F32], name="d_glu_act")
    dh2, dh2_m, dg_mix1, _ = _rmsnorm_bwd(du2, h2, row(W["norm_mix_g"][1]), dh3, name="d_norm_mix1")
    dh1, dh1_m, dg_ffn0, _, dw1_0, dw2_0 = mlp_bwd(dh2, dh2_m, h1, u1, r0, 0, "0")
    G["w_ff1"], G["w_ff2"] = jnp.stack([dw1_0, dw1_1]), jnp.stack([dw2_0, dw2_1])
    G["norm_ffn_g"] = jnp.concatenate([dg_ffn0, dg_ffn1], axis=0)
    dmix = _matmul(dh1_m, W["w_out"], mode="nt", out_dtypes=[F32], name="d_out_act")
    G["w_out"] = _matmul(mix, dh1_m, mode="tn", out_dtypes=[F32], name="d_out_w")
    riding = early_grads(G) if early_grads else None
    (dsq, dsk, dsv), got = _sb_bwd(qkv, dmix, riding, name="sb_bwd")
    d_hg, G["hg_lb"], G["hg_norm_g"] = _hg_bwd(proj, hg_o, hg_states, dmix, W["hg_lb_logits"], row(W["hg_norm_g"]), name="hg_bwd")
    dproj = jnp.concatenate([dsq, dsk, dsv, d_hg], axis=-1).astype(MXU_DTYPE)
    G["w_in"] = _matmul(u0, dproj, mode="tn", out_dtypes=[F32], tn=896, name="d_in_w")
    du0 = _matmul(dproj, W["w_in"], mode="nt", out_dtypes=[F32], tk=896, name="d_in_act")
    dx, _, dg_mix0, _ = _rmsnorm_bwd(du0, x, row(W["norm_mix_g"][0]), dh1, name="d_norm_mix0")
    G["norm_mix_g"] = jnp.concatenate([dg_mix0, dg_mix1], axis=0)
    return loss, dx, G, (riding, got)


BIG = (("w_out_ab", "w_out", "rows2d"), ("conv_w_glu", "w_glu", "cols2d"), ("conv_w_pw", "w_pw", "rows2d"),
       ("w_ff1", "w_ff1", "cols3d"), ("w_ff2", "w_ff2", "rows3d"), ("w_in_ab", "w_in", "cols2d"))
LATE = BIG[:-1]
SMALL_SHARDED = ("conv_b_glu", "conv_w_dw", "conv_b_dw", "conv_ln_g", "conv_ln_b", "conv_b_pw")
REPLICATED = ("norm_mix_g", "norm_ffn_g", "hg_lb_logits", "hg_norm_g", "final_norm_g")
ORDER = ("norm_mix_g", "norm_ffn_g", "w_in_ab", "w_out_ab", "hg_lb_logits", "hg_norm_g", "conv_w_glu", "conv_b_glu",
         "conv_w_dw", "conv_b_dw", "conv_ln_g", "conv_ln_b", "conv_w_pw", "conv_b_pw", "w_ff1", "w_ff2", "final_norm_g")


def _step(x, loss_target, w, m, v):
    D = D_MODEL
    x2, t2 = x.reshape(-1, D), loss_target.reshape(-1, D)
    chip = 2 * lax.axis_index("x") + lax.axis_index("y")
    c = lax.axis_index("c")

    small_in = jnp.concatenate([w["conv_b_glu"].reshape(2, 256), w["conv_w_dw"].reshape(CONV_WIDTH, 256)] +
                               [w[n].reshape(1, 256) for n in ("conv_b_dw", "conv_ln_g", "conv_ln_b", "conv_b_pw")], axis=0)
    g_in, gs = _gather_chips([w["w_in_ab"].astype(MXU_DTYPE), _pad_rows(small_in, 40)]).run("gather_first_weights")
    vec = lambda r0, r1: gs[:, r0:r1].transpose(1, 0, 2).reshape(r1 - r0, N_CHIPS * 256)
    W = {
        "w_in": _ChipWeight(g_in[:, 0], "cols"),
        "b_glu": gs[:, 0:2].reshape(2 * D), "w_dw": vec(2, 33), "b_dw": vec(33, 34)[0], "ln_g": vec(34, 35)[0],
        "ln_b": vec(35, 36)[0], "b_pw": vec(36, 37)[0],
        "norm_mix_g": w["norm_mix_g"], "norm_ffn_g": w["norm_ffn_g"], "hg_lb_logits": w["hg_lb_logits"],
        "hg_norm_g": w["hg_norm_g"], "final_norm_g": w["final_norm_g"],
    }
    late = _gather_chips([w[n].astype(MXU_DTYPE) for n, _, _ in LATE])

    def assemble(got):
        gw = dict(zip([s for _, s, _ in LATE], late.finish(got)))
        layers = lambda g, along: [_ChipWeight(g, along, (layer,)) for layer in range(2)]
        return {"w_out": gw["w_out"].reshape(D, D), "w_glu": _ChipWeight(gw["w_glu"][:, 0], "cols"), "w_pw": gw["w_pw"].reshape(D, D),
                "w_ff1": layers(gw["w_ff1"], "cols"), "w_ff2": layers(gw["w_ff2"], "rows")}

    def early_grads(G):
        return _owner_scatter(jnp.concatenate([_grad_blocks(G[s], kind) for _, s, kind in LATE], axis=2))

    loss, dx, G, (riding, got) = _forward_backward(x2, t2, W, (late, assemble), early_grads)
    half_late = _sum_leading(riding.finish(got), name="grads_owner_add_late")
    blocks = _grad_blocks(G["w_in"], "cols2d")
    from_pair = _pair_swap(blocks.transpose(1, 0, 2, 3)).run("grads_pair_swap_in")
    chip_sum = _add_pair(lax.dynamic_index_in_dim(blocks, c, axis=1, keepdims=False), from_pair, name="grads_pair_add_in")
    per_chip = _chip_scatter(chip_sum).run("grads_chip_scatter_in")
    half = jnp.concatenate([half_late, _sum_leading(per_chip, name="grads_chip_add_in")], axis=0)
    full = _pair_gather(half).run("grads_pair_gather")

    pack = jnp.concatenate([
        G["norm_mix_g"], G["norm_ffn_g"], G["final_norm_g"], jnp.concatenate([G["hg_norm_g"], G["hg_lb"]], axis=1),
        _pad_rows(jnp.broadcast_to(loss, (1, D)), 2), G["b_glu"].reshape(2, D), G["w_dw"], G["b_dw"], G["ln_g"], G["ln_b"], G["b_pw"],
    ], axis=0)
    pack = _pad_rows(pack, SMALL_ROWS)
    (packs,) = _gather_all([pack]).run("gather_small_grads")
    ssum, d_logits = _small_reduce(packs, w["hg_lb_logits"], name="reduce_small_grads")
    cut = lambda r0, r1: lax.dynamic_slice(ssum, (r0, chip * 256), (r1 - r0, 256))
    grads = {
        "norm_mix_g": ssum[0:2], "norm_ffn_g": ssum[2:4], "final_norm_g": ssum[4], "hg_norm_g": ssum[5, :HG_WIDTH].reshape(1, HG_HEADS, HG_DH),
        "hg_lb_logits": d_logits,
        "conv_b_glu": lax.dynamic_slice(ssum[8:10].reshape(1, 2 * D), (0, chip * 512), (1, 512)),
        "conv_w_dw": cut(10, 10 + CONV_WIDTH).reshape(1, CONV_WIDTH, 256),
        "conv_b_dw": cut(42, 43), "conv_ln_g": cut(43, 44), "conv_ln_b": cut(44, 45), "conv_b_pw": cut(45, 46),
    }
    loss_out = ssum[6, 0]

    off = 0
    for n, s, kind in BIG:
        shard = w[n].shape
        rows = w[n].size // (2 * D)
        grads[n] = full[:, off:off + rows].reshape(shard)
        off += rows

    delta, new_m, new_v = {}, {}, {}
    for n, _, _ in BIG:
        view = lambda a: a.reshape(-1, a.shape[-1])
        outs = _adamw(view(w[n]), view(grads[n]), view(m[n]), view(v[n]), name="adamw_" + n)
        delta[n], new_m[n], new_v[n] = (o.reshape(w[n].shape) for o in outs)
    small = SMALL_SHARDED + REPLICATED
    sizes = [w[n].size for n in small]
    total = sum(sizes)
    rows = -(-total // (8 * D)) * 8
    packed = lambda d: _pad_rows(jnp.concatenate([d[n].reshape(-1) for n in small]).reshape(-1, 128), rows * 8).reshape(rows, D)
    outs = _adamw(packed(w), packed(grads), packed(m), packed(v), name="adamw_small")
    off = 0
    for n, size in zip(small, sizes):
        delta[n], new_m[n], new_v[n] = (o.reshape(-1)[off:off + size].reshape(w[n].shape) for o in outs)
        off += size
    grads = {n: grads[n].reshape(w[n].shape) for n in ORDER}
    return (loss_out, dx.reshape(x.shape), *[grads[n] for n in ORDER], *[delta[n] for n in ORDER],
            *[new_m[n] for n in ORDER], *[new_v[n] for n in ORDER])


def kernel(x, norm_mix_g, norm_ffn_g, w_in_ab, w_out_ab, hg_lb_logits, hg_norm_g, conv_w_glu, conv_b_glu, conv_w_dw, conv_b_dw, conv_ln_g, conv_ln_b, conv_w_pw, conv_b_pw, w_ff1, w_ff2, final_norm_g, loss_target, m_norm_mix_g, m_norm_ffn_g, m_w_in_ab, m_w_out_ab, m_hg_lb_logits, m_hg_norm_g, m_conv_w_glu, m_conv_b_glu, m_conv_w_dw, m_conv_b_dw, m_conv_ln_g, m_conv_ln_b, m_conv_w_pw, m_conv_b_pw, m_w_ff1, m_w_ff2, m_final_norm_g, v_norm_mix_g, v_norm_ffn_g, v_w_in_ab, v_w_out_ab, v_hg_lb_logits, v_hg_norm_g, v_conv_w_glu, v_conv_b_glu, v_conv_w_dw, v_conv_b_dw, v_conv_ln_g, v_conv_ln_b, v_conv_w_pw, v_conv_b_pw, v_w_ff1, v_w_ff2, v_final_norm_g):
    args = locals()
    w = {n: args[n] for n in ORDER}
    m = {n: args["m_" + n] for n in ORDER}
    v = {n: args["v_" + n] for n in ORDER}
    return _step(x, loss_target, w, m, v)
```

```python
import functools

import jax
import jax.numpy as jnp
from jax import lax
from jax.experimental import pallas as pl
from jax.experimental.pallas import tpu as pltpu

F32 = jnp.float32
MXU_DTYPE = jnp.bfloat16
MESH = pl.DeviceIdType.MESH

D_MODEL = 1024
SB_HEADS, SB_DH, SB_WIDTH = 8, 64, 512
SB_KEYS = 512
SB_SUB = 256
SB_ROWS_FWD, SB_ROWS_BWD = 512, 256
HG_HEADS, HG_DH, HG_WIDTH = 4, 128, 512
HG_CHUNK = 16
HG_TOKENS = 256
IN_WIDTH = 3 * SB_WIDTH + 4 * HG_WIDTH
CONV_WIDTH = 31
CONV_HALO = 32
CONV_ROWS = 32
D_FF = 4096
RMS_EPS = 1e-6
LN_EPS = 1e-5
N_CHIPS = 4
N_DEV = 8
SMALL_ROWS = 48
VMEM_LIMIT = 56 * 1024 * 1024

ADAM_LR, ADAM_B1, ADAM_B2, ADAM_EPS, ADAM_WD, ADAM_STEP = 0.001, 0.9, 0.999, 1e-08, 0.01, 10


def _params(*sem):
    return pltpu.CompilerParams(dimension_semantics=sem, vmem_limit_bytes=VMEM_LIMIT)


def _mx(v):
    return v.astype(MXU_DTYPE)


def _dot(a, b):
    return jnp.dot(_mx(a), _mx(b), preferred_element_type=F32)


def _dot_nt(a, b):
    return lax.dot_general(_mx(a), _mx(b), (((1,), (1,)), ((), ())), preferred_element_type=F32)


def _dot_tn(a, b):
    return lax.dot_general(_mx(a), _mx(b), (((0,), (0,)), ((), ())), preferred_element_type=F32)


def _neg_abs(x):
    bits = lax.bitcast_convert_type(x, jnp.uint32) | jnp.uint32(0x80000000)
    return lax.bitcast_convert_type(bits, F32)


def _key_order_sums(v, tri2, later):
    hi = _mx(v)
    lo = _mx(v - hi.astype(F32))
    n = SB_KEYS // SB_SUB
    blocks = [slice(b * SB_SUB, (b + 1) * SB_SUB) for b in range(n)]
    totals = [jnp.sum(v[:, sl], axis=1, keepdims=True) for sl in blocks]
    sums = []
    for b, sl in enumerate(blocks):
        s = jnp.dot(jnp.concatenate([hi[:, sl], lo[:, sl]], axis=1), tri2, preferred_element_type=F32)
        for o in (range(b + 1, n) if later else range(b)):
            s = s + totals[o]
        sums.append(s)
    return jnp.concatenate(sums, axis=1), functools.reduce(lambda a, b: a + b, totals)


class _ChipWeight:
    def __init__(self, parts, along, lead=()):
        self.parts, self.along, self.lead = parts, along, tuple(lead)
        r, c = parts.shape[-2:]
        self.shape = (r, N_CHIPS * c) if along == "cols" else (N_CHIPS * r, c)

    def _gathered_is_n(self, mode):
        return (self.along == "cols") == (mode in ("nn", "tn"))

    def tile(self, mode, tn, tk):
        r, c = self.parts.shape[-2:]
        part = c if self.along == "cols" else r
        return (part, tk) if self._gathered_is_n(mode) else (tn, part)

    def spec(self, mode, tn, tk):
        squeezed = (None,) * (1 + len(self.lead))
        lead, cols, by_n = self.lead, self.along == "cols", self._gathered_is_n(mode)
        block = (tn, tk) if mode == "nt" else (tk, tn)

        def index(i, j, k):
            chip, other = (j, k) if by_n else (k, j)
            return (chip,) + lead + ((other, 0) if cols else (0, other))

        return pl.BlockSpec(squeezed + block, index)


def _matmul(a, b, *, mode, out_dtypes, epilogue=None, tiles=(), rows=(), n_sums=0, rider=None, tm=1024, tn=1024, tk=1024, name):
    b_shape = b.shape
    if mode == "nn":
        (M, K), N = a.shape, b_shape[1]
    elif mode == "nt":
        (M, K), N = a.shape, b_shape[0]
    else:
        (K, M), N = a.shape, b_shape[1]
    if isinstance(b, _ChipWeight):
        tn, tk = b.tile(mode, tn, tk)
    tm, tn, tk = min(tm, M), min(tn, N), min(tk, K)
    assert M % tm == 0 and N % tn == 0 and K % tk == 0, (name, M, N, K)
    nk = K // tk
    a_spec = pl.BlockSpec((tk, tm), lambda i, j, k: (k, i)) if mode == "tn" else pl.BlockSpec((tm, tk), lambda i, j, k: (i, k))
    if isinstance(b, _ChipWeight):
        b_spec, b = b.spec(mode, tn, tk), b.parts
    else:
        b_spec = pl.BlockSpec((tn, tk), lambda i, j, k: (j, k)) if mode == "nt" else pl.BlockSpec((tk, tn), lambda i, j, k: (k, j))
    dims = {"nn": ((1,), (0,)), "nt": ((1,), (1,)), "tn": ((0,), (0,))}[mode]
    n_t, n_r, n_o = len(tiles), len(rows), len(out_dtypes)
    n_x = 0 if rider is None else len(rider.arrays)
    grid = (M // tm, N // tn, nk)
    if epilogue is None:
        epilogue = lambda acc: (acc,)

    def body(a_ref, b_ref, *rest):
        extra, x_in, rest = rest[:n_t + n_r], rest[n_t + n_r:n_t + n_r + n_x], rest[n_t + n_r + n_x:]
        outs, sums, x_out, acc_ref, sems = rest[:n_o], rest[n_o:n_o + n_sums], rest[n_o + n_sums:n_o + n_sums + n_x], rest[n_o + n_sums + n_x], rest[n_o + n_sums + n_x + 1:]
        i, j, k = (pl.program_id(d) for d in range(3))
        if rider is not None:
            @pl.when((i == 0) & (j == 0) & (k == 0))
            def _():
                for cp in rider.make(x_in, x_out, *sems):
                    cp.start()

        @pl.when(k == 0)
        def _():
            acc_ref[...] = jnp.zeros_like(acc_ref)

        acc_ref[...] += lax.dot_general(_mx(a_ref[...]), _mx(b_ref[...]), (dims, ((), ())), preferred_element_type=F32)

        @pl.when(k == nk - 1)
        def _():
            res = epilogue(acc_ref[...], *[e[...] for e in extra])
            for o_ref, r in zip(outs, res[:n_o]):
                o_ref[...] = r.astype(o_ref.dtype)
            for s_ref, r in zip(sums, res[n_o:]):
                @pl.when(i == 0)
                def _():
                    s_ref[...] = jnp.zeros_like(s_ref)

                s_ref[...] += r

        if rider is not None:
            @pl.when((i == grid[0] - 1) & (j == grid[1] - 1) & (k == grid[2] - 1))
            def _():
                for cp in rider.make(x_in, x_out, *sems):
                    cp.wait()

    tile_spec = pl.BlockSpec((tm, tn), lambda i, j, k: (i, j))
    row_spec = pl.BlockSpec((1, tn), lambda i, j, k: (0, j))
    ordered = n_sums > 0 or rider is not None
    outs = pl.pallas_call(
        body, grid=grid,
        in_specs=[a_spec, b_spec] + [tile_spec] * n_t + [row_spec] * n_r + [HBM_SPEC] * n_x,
        out_specs=[tile_spec] * n_o + [row_spec] * n_sums + [HBM_SPEC] * n_x,
        out_shape=[jax.ShapeDtypeStruct((M, N), dt) for dt in out_dtypes] + [jax.ShapeDtypeStruct((1, N), F32)] * n_sums
        + ([] if rider is None else rider.out_shapes),
        scratch_shapes=[pltpu.VMEM((tm, tn), F32)] + ([] if rider is None else rider.scratch()),
        compiler_params=_params(*(("arbitrary",) * 3 if ordered else ("parallel", "parallel", "arbitrary"))), name=name,
    )(a, b, *tiles, *rows, *([] if rider is None else rider.arrays))
    res = outs[0] if n_o + n_sums == 1 else outs[:n_o + n_sums]
    return res if rider is None else (res, outs[n_o + n_sums:])


def _token_block(T):
    return min(512, T)


def _rmsnorm_fwd(h, g, *, name):
    T, D = h.shape
    tb = _token_block(T)

    def body(h_ref, g_ref, u_ref):
        x = h_ref[...]
        r = lax.rsqrt(jnp.mean(x * x, axis=-1, keepdims=True) + RMS_EPS)
        u_ref[...] = (x * r * g_ref[...]).astype(u_ref.dtype)

    blk = pl.BlockSpec((tb, D), lambda i: (i, 0))
    return pl.pallas_call(
        body, grid=(T // tb,), in_specs=[blk, pl.BlockSpec((1, D), lambda i: (0, 0))], out_specs=blk,
        out_shape=jax.ShapeDtypeStruct((T, D), MXU_DTYPE), compiler_params=_params("parallel"), name=name,
    )(h, g)


def _rms_bwd_math(x, g, du):
    r = lax.rsqrt(jnp.mean(x * x, axis=-1, keepdims=True) + RMS_EPS)
    gd = g * du
    dx = r * gd - x * (r * r * r) * jnp.mean(gd * x, axis=-1, keepdims=True)
    return dx, du * x * r


def _loss_head(h, g, target, *, name):
    T, D = h.shape
    tb = _token_block(T)

    def body(h_ref, g_ref, t_ref, dh_ref, dhm_ref, dg_ref, loss_ref):
        @pl.when(pl.program_id(0) == 0)
        def _():
            dg_ref[...] = jnp.zeros_like(dg_ref)
            loss_ref[...] = jnp.zeros_like(loss_ref)

        x, gg = h_ref[...], g_ref[...]
        r = lax.rsqrt(jnp.mean(x * x, axis=-1, keepdims=True) + RMS_EPS)
        diff = x * r * gg - t_ref[...]
        per_token = jnp.mean(diff * diff, axis=-1, keepdims=True)
        loss_ref[...] += 0.5 * jnp.sum(per_token, axis=0, keepdims=True)
        dx, dg_terms = _rms_bwd_math(x, gg, diff / D)
        dh_ref[...] = dx
        dhm_ref[...] = dx.astype(dhm_ref.dtype)
        dg_ref[...] += jnp.sum(dg_terms, axis=0, keepdims=True)

    blk = pl.BlockSpec((tb, D), lambda i: (i, 0))
    row = pl.BlockSpec((1, D), lambda i: (0, 0))
    return pl.pallas_call(
        body, grid=(T // tb,), in_specs=[blk, row, blk], out_specs=[blk, blk, row, pl.BlockSpec((1, 1), lambda i: (0, 0))],
        out_shape=[jax.ShapeDtypeStruct((T, D), F32), jax.ShapeDtypeStruct((T, D), MXU_DTYPE), jax.ShapeDtypeStruct((1, D), F32),
                   jax.ShapeDtypeStruct((1, 1), F32)],
        compiler_params=_params("arbitrary"), name=name,
    )(h, g, target)


def _sb_scores(qm, ks, later, tri, mask):
    z = _dot_nt(qm, ks)
    sp = jnp.maximum(z, 0.0) + jnp.log(1.0 + jnp.exp(_neg_abs(z)))
    lb = z - sp
    if mask is not None:
        sp = jnp.where(mask, sp, 0.0)
    after, total = _key_order_sums(sp, tri, later=True)
    w = jnp.exp(lb - (after + later))
    if mask is not None:
        w = jnp.where(mask, w, 0.0)
    return total, lb, w


def _sb_setup(q_ref, rows):
    i, hsel = pl.program_id(1), pl.program_id(2)
    lane = lax.broadcasted_iota(jnp.int32, (rows, 2 * SB_DH), 1)
    mine = (lane >= SB_DH) == (hsel == 1)
    diag = (i * rows) // SB_KEYS
    t = i * rows + lax.broadcasted_iota(jnp.int32, (rows, SB_KEYS), 0)
    s = diag * SB_KEYS + lax.broadcasted_iota(jnp.int32, (rows, SB_KEYS), 1)
    a = lax.broadcasted_iota(jnp.int32, (2 * SB_SUB, SB_SUB), 0) % SB_SUB
    b = lax.broadcasted_iota(jnp.int32, (2 * SB_SUB, SB_SUB), 1)
    return i, hsel, mine, diag, s < t, _mx(a > b), _mx(a < b)


def _sb_keys(j):
    return pl.ds(pl.multiple_of(j * SB_KEYS, SB_KEYS), SB_KEYS)


def _sb_descend(n, step, carry):
    carry = lax.fori_loop(0, n // 2, lambda it, cr: step(n - 2 - 2 * it, step(n - 1 - 2 * it, cr)), carry)
    return lax.cond(n % 2 == 1, lambda cr: step(0, cr), lambda cr: cr, carry)


def _sb_ascend(n, step, carry):
    odd = n % 2
    carry = lax.cond(odd == 1, lambda cr: step(0, cr), lambda cr: cr, carry)
    return lax.fori_loop(0, n // 2, lambda it, cr: step(odd + 2 * it + 1, step(odd + 2 * it, cr)), carry)


def _sb_call(body, qkv, extra_in, out_blocks, out_dtype, scratch, rider, rows, *, name):
    T = qkv.shape[0]
    n_pairs = SB_HEADS // 2
    grid = (n_pairs, T // rows, 2)
    pair = lambda col0: pl.BlockSpec((rows, 2 * SB_DH), lambda p, i, h: (i, col0 + p))
    whole = lambda col0: pl.BlockSpec((T, 2 * SB_DH), lambda p, i, h: (0, col0 + p))
    in_specs = [pair(0), whole(n_pairs), whole(2 * n_pairs)] + [pair(0)] * len(extra_in)
    out_specs = [pair(0) if kind == "pair" else whole(0) for kind in out_blocks]
    n_in, n_out, n_r = len(in_specs), len(out_specs), 0 if rider is None else len(rider.arrays)

    def kernel_body(*refs):
        ins, r_in = refs[:n_in], refs[n_in:n_in + n_r]
        outs, r_out = refs[n_in + n_r:n_in + n_r + n_out], refs[n_in + n_r + n_out:n_in + 2 * n_r + n_out]
        rest = refs[n_in + 2 * n_r + n_out:]
        ids = [pl.program_id(a) for a in range(3)]
        if rider is not None:
            @pl.when((ids[0] == 0) & (ids[1] == 0) & (ids[2] == 0))
            def _():
                for cp in rider.make(r_in, r_out, *rest[len(scratch):]):
                    cp.start()

        body(ins, outs, rest[:len(scratch)])
        if rider is not None:
            @pl.when((ids[0] == grid[0] - 1) & (ids[1] == grid[1] - 1) & (ids[2] == grid[2] - 1))
            def _():
                for cp in rider.make(r_in, r_out, *rest[len(scratch):]):
                    cp.wait()

    res = pl.pallas_call(
        kernel_body, grid=grid, in_specs=in_specs + [HBM_SPEC] * n_r, out_specs=out_specs + [HBM_SPEC] * n_r,
        out_shape=[jax.ShapeDtypeStruct((T, SB_WIDTH), out_dtype)] * n_out + ([] if rider is None else rider.out_shapes),
        scratch_shapes=list(scratch) + ([] if rider is None else rider.scratch()),
        compiler_params=_params("arbitrary", "arbitrary", "arbitrary"), name=name,
    )(qkv, qkv, qkv, *extra_in, *([] if rider is None else rider.arrays))
    return res[:n_out], res[n_out:]


def _sb_fwd(qkv, rider=None, *, name):
    rows = min(SB_ROWS_FWD, qkv.shape[0])
    scale = SB_DH ** -0.5

    def body(ins, outs, _):
        (q_ref, k_ref, v_ref), (o_ref,) = ins, outs
        i, hsel, mine, diag, mask, tri, _ = _sb_setup(q_ref, rows)
        qm = jnp.where(mine, q_ref[...], 0) * scale

        def tile(j, m, later, acc):
            total, _, w = _sb_scores(qm, k_ref[_sb_keys(j), :], later, tri, m)
            return later + total, acc + _dot(w, v_ref[_sb_keys(j), :])

        carry = tile(diag, mask, jnp.zeros((rows, 1), F32), jnp.zeros((rows, 2 * SB_DH), F32))
        _, acc = _sb_descend(diag, lambda j, cr: tile(j, None, *cr), carry)
        res = jnp.where(mine, acc, 0.0).astype(o_ref.dtype)

        @pl.when(hsel == 0)
        def _():
            o_ref[...] = res

        @pl.when(hsel == 1)
        def _():
            o_ref[...] += res

    (o,), got = _sb_call(body, qkv, [], ["pair"], MXU_DTYPE, [], rider, rows, name=name)
    return o, got


def _sb_bwd(qkv, dmix, rider=None, *, name):
    T = qkv.shape[0]
    rows = min(SB_ROWS_BWD, T)
    scale = SB_DH ** -0.5

    def body(ins, outs, scratch):
        (q_ref, k_ref, v_ref, do_ref), (dq_ref, dk_ref, dv_ref), (da_ref, beta_ref) = ins, outs, scratch
        i, hsel, mine, diag, mask, tri, tri_before = _sb_setup(q_ref, rows)

        @pl.when((i == 0) & (hsel == 0))
        def _():
            dk_ref[...] = jnp.zeros_like(dk_ref)
            dv_ref[...] = jnp.zeros_like(dv_ref)

        qm = jnp.where(mine, q_ref[...], 0) * scale
        do_m = _mx(jnp.where(mine, do_ref[...], 0.0))

        def weights(j, m, later):
            total, lb, w = _sb_scores(qm, k_ref[_sb_keys(j), :], later, tri, m)
            da_ref[j] = _dot_nt(do_m, v_ref[_sb_keys(j), :]) * w
            beta_ref[j] = jnp.exp(lb)
            dv_ref[_sb_keys(j), :] += _dot_tn(w, do_m)
            return later + total

        later = weights(diag, mask, jnp.zeros((rows, 1), F32))
        _sb_descend(diag, lambda j, c: weights(j, None, c), later)

        def logits(j, m, before, dq):
            da = da_ref[j]
            earlier, total = _key_order_sums(da, tri_before, later=False)
            dz = da - beta_ref[j] * (da + earlier + before)
            if m is not None:
                dz = jnp.where(m, dz, 0.0)
            dz = _mx(dz)
            dk_ref[_sb_keys(j), :] += _dot_tn(dz, qm)
            return before + total, dq + _dot(dz, k_ref[_sb_keys(j), :])

        carry = (jnp.zeros((rows, 1), F32), jnp.zeros((rows, 2 * SB_DH), F32))
        carry = _sb_ascend(diag, lambda j, cr: logits(j, None, *cr), carry)
        res = jnp.where(mine, logits(diag, mask, *carry)[1] * scale, 0.0)

        @pl.when(hsel == 0)
        def _():
            dq_ref[...] = res

        @pl.when(hsel == 1)
        def _():
            dq_ref[...] += res

    n_tiles = T // SB_KEYS
    scratch = [pltpu.VMEM((n_tiles, rows, SB_KEYS), F32), pltpu.VMEM((n_tiles, rows, SB_KEYS), F32)]
    return _sb_call(body, qkv, [dmix], ["pair", "whole", "whole"], F32, scratch, rider, rows, name=name)


def _chunk_row(n):
    return lax.broadcasted_iota(jnp.int32, (n, HG_DH), 0) % HG_CHUNK


def _chunk_cumsum(x, row, reverse=False):
    n = x.shape[0]
    for sh in (1, 2, 4, 8):
        if reverse:
            x = x + jnp.where(row < HG_CHUNK - sh, pltpu.roll(x, n - sh, 0), 0.0)
        else:
            x = x + jnp.where(row >= sh, pltpu.roll(x, sh, 0), 0.0)
    return x


def _hg_lower_bound(logits_ref):
    lg = logits_ref[...]
    e = jnp.exp(lg - jnp.max(lg, axis=0, keepdims=True))
    return e[0:1, :] / jnp.sum(e, axis=0, keepdims=True)


def _hg_terms(fr, q, lb, row):
    sig = jax.nn.sigmoid(fr)
    f = lb + (1.0 - lb) * sig
    kk = 1.0 - f
    g = jnp.log(f)
    G = _chunk_cumsum(g, row)
    g_last = G + (_chunk_cumsum(g, row, reverse=True) - g)
    e_g, e_ng, e_lg = jnp.exp(G), jnp.exp(-G), jnp.exp(g_last - G)
    return dict(sig=sig, f=f, kk=kk, e_g=e_g, e_ng=e_ng, e_lg=e_lg, q_dec=q * e_g, k_intra=kk * e_ng,
                k_state=kk * e_lg, decay=jnp.exp(g_last))


def _hg_causal(n):
    t = lax.broadcasted_iota(jnp.int32, (n, n), 0)
    s = lax.broadcasted_iota(jnp.int32, (n, n), 1)
    return (s <= t) & (s // HG_CHUNK == t // HG_CHUNK)


def _chunks(a):
    return a.reshape(a.shape[0] // HG_CHUNK, HG_CHUNK, a.shape[1])


def _per_chunk(lhs, rhs, contract):
    return lax.dot_general(_mx(lhs), _mx(rhs), ((contract[:1], contract[1:]), ((0,), (0,))), preferred_element_type=F32)


def _hg_specs(T, tb, col0, order):
    return [pl.BlockSpec((tb, HG_WIDTH), functools.partial(lambda i, j: (order(i), j), j=col0 + j)) for j in range(4)]


def _hg_fwd(proj, logits, norm_g, *, name):
    T = proj.shape[0]
    tb = min(HG_TOKENS, T)
    nch = tb // HG_CHUNK

    def body(q_ref, f_ref, i_ref, gate_ref, lg_ref, ng_ref, out_ref, o_ref, s_ref, st_ref, inc_ref, dec_ref):
        @pl.when(pl.program_id(0) == 0)
        def _():
            st_ref[...] = jnp.zeros_like(st_ref)

        lb_all = _hg_lower_bound(lg_ref)
        row = _chunk_row(tb)
        causal = _hg_causal(tb)
        for hh in range(HG_HEADS):
            cols = slice(hh * HG_DH, (hh + 1) * HG_DH)
            t = _hg_terms(f_ref[:, cols], q_ref[:, cols], lb_all[:, cols], row)
            v = i_ref[:, cols]
            scores = jnp.where(causal, _dot_nt(t["q_dec"], t["k_intra"]), 0.0)
            o_intra = _dot(scores, v)
            inc_ref[...] = _per_chunk(_chunks(v), _chunks(t["k_state"]), (1, 1))
            dec_ref[...] = _chunks(t["decay"])

            def step(ci, st):
                s_ref[ci, hh] = st
                return st * dec_ref[ci][0:1, :] + inc_ref[ci]

            st_ref[hh] = lax.fori_loop(0, nch, step, st_ref[hh], unroll=4)
            o_inter = _per_chunk(_chunks(t["q_dec"]), s_ref[:, hh], (2, 2))
            o = o_intra + o_inter.reshape(tb, HG_DH)
            o_ref[:, cols] = o
            gate = gate_ref[:, cols]
            on = o * lax.rsqrt(jnp.mean(o * o, axis=-1, keepdims=True) + RMS_EPS) * ng_ref[:, cols]
            out_ref[:, cols] = (on * (gate * jax.nn.sigmoid(gate))).astype(out_ref.dtype)

    blk = pl.BlockSpec((tb, HG_WIDTH), lambda i: (i, 0))
    return pl.pallas_call(
        body, grid=(T // tb,),
        in_specs=_hg_specs(T, tb, 3, lambda i: i) + [pl.BlockSpec((3, HG_WIDTH), lambda i: (0, 0)), pl.BlockSpec((1, HG_WIDTH), lambda i: (0, 0))],
        out_specs=[blk, blk, pl.BlockSpec((nch, HG_HEADS, HG_DH, HG_DH), lambda i: (i, 0, 0, 0))],
        out_shape=[jax.ShapeDtypeStruct((T, HG_WIDTH), MXU_DTYPE), jax.ShapeDtypeStruct((T, HG_WIDTH), F32),
                   jax.ShapeDtypeStruct((T // HG_CHUNK, HG_HEADS, HG_DH, HG_DH), F32)],
        scratch_shapes=[pltpu.VMEM((HG_HEADS, HG_DH, HG_DH), F32), pltpu.VMEM((nch, HG_DH, HG_DH), F32),
                        pltpu.VMEM((nch, HG_CHUNK, HG_DH), F32)],
        compiler_params=_params("arbitrary"), name=name,
    )(proj, proj, proj, proj, logits, norm_g)


def _hg_bwd(proj, o_raw, states, dmix, logits, norm_g, *, name):
    T = proj.shape[0]
    tb = min(HG_TOKENS, T)
    nch = tb // HG_CHUNK
    nb = T // tb
    rev = lambda i: nb - 1 - i

    def body(q_ref, f_ref, i_ref, gate_ref, o_ref, s_ref, dout_ref, lg_ref, ng_ref, dp_ref, dlb_ref, dng_ref,
             dst_ref, inc_ref, dec_ref, after_ref):
        @pl.when(pl.program_id(0) == 0)
        def _():
            dst_ref[...] = jnp.zeros_like(dst_ref)
            dlb_ref[...] = jnp.zeros_like(dlb_ref)
            dng_ref[...] = jnp.zeros_like(dng_ref)

        lb_all = _hg_lower_bound(lg_ref)
        row = _chunk_row(tb)
        causal = _hg_causal(tb)
        for hh in range(HG_HEADS):
            cols = slice(hh * HG_DH, (hh + 1) * HG_DH)
            out_cols = lambda part: slice(part * HG_WIDTH + hh * HG_DH, part * HG_WIDTH + (hh + 1) * HG_DH)
            o, gate, dout, ng, lb = o_ref[:, cols], gate_ref[:, cols], dout_ref[:, cols], ng_ref[:, cols], lb_all[:, cols]
            sg = jax.nn.sigmoid(gate)
            r = lax.rsqrt(jnp.mean(o * o, axis=-1, keepdims=True) + RMS_EPS)
            oh = o * r
            dp_ref[:, out_cols(3)] = dout * (oh * ng) * (sg * (1.0 + gate * (1.0 - sg)))
            don = dout * (gate * sg)
            dng_ref[:, cols] += jnp.sum(don * oh, axis=0, keepdims=True)
            doh = don * ng
            do = r * (doh - oh * jnp.mean(doh * oh, axis=-1, keepdims=True))

            t = _hg_terms(f_ref[:, cols], q_ref[:, cols], lb, row)
            v = i_ref[:, cols]
            scores = jnp.where(causal, _dot_nt(t["q_dec"], t["k_intra"]), 0.0)
            dscores = jnp.where(causal, _dot_nt(do, v), 0.0)
            inc_ref[...] = _per_chunk(_chunks(do), _chunks(t["q_dec"]), (1, 1))
            dec_ref[...] = _chunks(t["decay"])

            def step(it, dst):
                ci = nch - 1 - it
                after_ref[ci] = dst
                return dst * dec_ref[ci][0:1, :] + inc_ref[ci]

            dst_ref[hh] = lax.fori_loop(0, nch, step, dst_ref[hh], unroll=4)
            st, dst = s_ref[:, hh], after_ref[...]
            dqd = _dot(dscores, t["k_intra"]) + _per_chunk(_chunks(do), st, (2, 1)).reshape(tb, HG_DH)
            dki = _dot_tn(dscores, t["q_dec"])
            dks = _per_chunk(_chunks(v), dst, (2, 1)).reshape(tb, HG_DH)
            dp_ref[:, out_cols(2)] = _dot_tn(scores, do) + _per_chunk(_chunks(t["k_state"]), dst, (2, 2)).reshape(tb, HG_DH)
            ddecay = jnp.broadcast_to(jnp.sum(st * dst, axis=1, keepdims=True), (nch, HG_CHUNK, HG_DH)).reshape(tb, HG_DH)
            dks_ks = dks * t["k_state"]
            d_glast = _chunk_cumsum(dks_ks, row) + ddecay * t["decay"]
            d_g = dqd * t["q_dec"] - dki * t["k_intra"] - dks_ks + jnp.where(row == HG_CHUNK - 1, d_glast, 0.0)
            df = _chunk_cumsum(d_g, row, reverse=True) / t["f"] - (dki * t["e_ng"] + dks * t["e_lg"])
            dp_ref[:, out_cols(0)] = dqd * t["e_g"]
            dp_ref[:, out_cols(1)] = df * (1.0 - lb) * t["sig"] * (1.0 - t["sig"])
            dlb_ref[:, cols] += jnp.sum(df * (1.0 - t["sig"]), axis=0, keepdims=True)

    blk = pl.BlockSpec((tb, HG_WIDTH), lambda i: (rev(i), 0))
    row_spec = pl.BlockSpec((1, HG_WIDTH), lambda i: (0, 0))
    return pl.pallas_call(
        body, grid=(nb,),
        in_specs=_hg_specs(T, tb, 3, rev) + [
            blk, pl.BlockSpec((nch, HG_HEADS, HG_DH, HG_DH), lambda i: (rev(i), 0, 0, 0)),
            pl.BlockSpec((tb, HG_WIDTH), lambda i: (rev(i), 1)), pl.BlockSpec((3, HG_WIDTH), lambda i: (0, 0)), row_spec],
        out_specs=[pl.BlockSpec((tb, 4 * HG_WIDTH), lambda i: (rev(i), 0)), row_spec, row_spec],
        out_shape=[jax.ShapeDtypeStruct((T, 4 * HG_WIDTH), F32), jax.ShapeDtypeStruct((1, HG_WIDTH), F32), jax.ShapeDtypeStruct((1, HG_WIDTH), F32)],
        scratch_shapes=[pltpu.VMEM((HG_HEADS, HG_DH, HG_DH), F32), pltpu.VMEM((nch, HG_DH, HG_DH), F32),
                        pltpu.VMEM((nch, HG_CHUNK, HG_DH), F32), pltpu.VMEM((nch, HG_DH, HG_DH), F32)],
        compiler_params=_params("arbitrary"), name=name,
    )(proj, proj, proj, proj, o_raw, states, dmix, logits, norm_g)


def _shifted_copies(sh_ref, n_rows):
    keep = n_rows + CONV_HALO - 8
    for b in range(1, 8):
        sh_ref[b, 0:keep, :] = sh_ref[0, b:b + keep, :]


def _tap_rows(sh_ref, offset, r0, lanes):
    start = pl.multiple_of(r0 + (offset - offset % 8), 8)
    return sh_ref[offset % 8, pl.ds(start, CONV_ROWS), lanes]


def _conv_fwd(p, w_dw, b_dw, ln_g, ln_b, *, name):
    T, D = p.shape[0], p.shape[1] // 2
    tb = _token_block(T)
    hpb = tb // CONV_HALO
    lane_step = 512

    def body(p1_ref, p2_ref, q1_ref, q2_ref, w_ref, bdw_ref, g_ref, b_ref, a_ref, y_ref, act_ref, sh_ref):
        i = pl.program_id(0)
        a = p1_ref[...] * jax.nn.sigmoid(p2_ref[...])
        sh_ref[0, 0:CONV_HALO, :] = jnp.where(i > 0, q1_ref[...] * jax.nn.sigmoid(q2_ref[...]), 0.0)
        sh_ref[0, CONV_HALO:, :] = a
        a_ref[...] = a
        _shifted_copies(sh_ref, tb)

        def chunk(ci, _):
            r0 = pl.multiple_of(ci * CONV_ROWS, CONV_ROWS)
            for l0 in range(0, D, lane_step):
                lanes = slice(l0, l0 + lane_step)
                acc = jnp.broadcast_to(bdw_ref[:, lanes], (CONV_ROWS, lane_step))
                for k in range(CONV_WIDTH):
                    acc = acc + _tap_rows(sh_ref, CONV_HALO - CONV_WIDTH + 1 + k, r0, lanes) * w_ref[k:k + 1, lanes]
                y_ref[pl.ds(r0, CONV_ROWS), lanes] = acc
            return 0

        lax.fori_loop(0, tb // CONV_ROWS, chunk, 0)
        y = y_ref[...]
        mu = jnp.mean(y, axis=-1, keepdims=True)
        yc = y - mu
        s = yc * lax.rsqrt(jnp.mean(yc * yc, axis=-1, keepdims=True) + LN_EPS) * g_ref[...] + b_ref[...]
        act_ref[...] = (s * jax.nn.sigmoid(s)).astype(act_ref.dtype)

    prev = lambda i: jnp.maximum(i * hpb - 1, 0)
    blk = pl.BlockSpec((tb, D), lambda i: (i, 0))
    row = pl.BlockSpec((1, D), lambda i: (0, 0))
    return pl.pallas_call(
        body, grid=(T // tb,),
        in_specs=[blk, pl.BlockSpec((tb, D), lambda i: (i, 1)), pl.BlockSpec((CONV_HALO, D), lambda i: (prev(i), 0)),
                  pl.BlockSpec((CONV_HALO, D), lambda i: (prev(i), 1)), pl.BlockSpec((CONV_HALO, D), lambda i: (0, 0)), row, row, row],
        out_specs=[blk, blk, blk],
        out_shape=[jax.ShapeDtypeStruct((T, D), F32), jax.ShapeDtypeStruct((T, D), F32), jax.ShapeDtypeStruct((T, D), MXU_DTYPE)],
        scratch_shapes=[pltpu.VMEM((8, tb + CONV_HALO, D), F32)],
        compiler_params=_params("parallel"), name=name,
    )(p, p, p, p, w_dw, b_dw, ln_g, ln_b)


def _conv_bwd_norm(dact, y, ln_g, ln_b, *, name):
    T, D = y.shape
    tb = _token_block(T)

    def body(da_ref, y_ref, g_ref, b_ref, dy_ref, dg_ref, db_ref, cs_ref):
        @pl.when(pl.program_id(0) == 0)
        def _():
            dg_ref[...] = jnp.zeros_like(dg_ref)
            db_ref[...] = jnp.zeros_like(db_ref)
            cs_ref[...] = jnp.zeros_like(cs_ref)

        y, g = y_ref[...], g_ref[...]
        yc = y - jnp.mean(y, axis=-1, keepdims=True)
        rs = lax.rsqrt(jnp.mean(yc * yc, axis=-1, keepdims=True) + LN_EPS)
        yn = yc * rs
        s = yn * g + b_ref[...]
        sg = jax.nn.sigmoid(s)
        ds = da_ref[...] * (sg * (1.0 + s * (1.0 - sg)))
        dg_ref[...] += jnp.sum(ds * yn, axis=0, keepdims=True)
        db_ref[...] += jnp.sum(ds, axis=0, keepdims=True)
        dyn = ds * g
        dy = rs * (dyn - jnp.mean(dyn, axis=-1, keepdims=True) - yn * jnp.mean(dyn * yn, axis=-1, keepdims=True))
        dy_ref[...] = dy
        cs_ref[...] += jnp.sum(dy, axis=0, keepdims=True)

    blk = pl.BlockSpec((tb, D), lambda i: (i, 0))
    row = pl.BlockSpec((1, D), lambda i: (0, 0))
    rs_ = jax.ShapeDtypeStruct((1, D), F32)
    return pl.pallas_call(
        body, grid=(T // tb,), in_specs=[blk, blk, row, row], out_specs=[blk, row, row, row],
        out_shape=[jax.ShapeDtypeStruct((T, D), F32), rs_, rs_, rs_], compiler_params=_params("arbitrary"), name=name,
    )(dact, y, ln_g, ln_b)


def _conv_bwd_taps(dy, a, p, w_dw, *, name):
    T, D = dy.shape
    tb = _token_block(T)
    hpb = tb // CONV_HALO
    last = T // CONV_HALO - 1
    nb = T // tb
    lane_step = 128
    groups = CONV_ROWS // 8

    def body(dy_ref, dyn_ref, a_ref, p1_ref, p2_ref, w_ref, dp_ref, dw_ref, cs_ref, sh_ref, da_ref):
        i = pl.program_id(0)

        @pl.when(i == 0)
        def _():
            dw_ref[...] = jnp.zeros_like(dw_ref)
            cs_ref[...] = jnp.zeros_like(cs_ref)

        sh_ref[0, 0:tb, :] = dy_ref[...]
        sh_ref[0, tb:, :] = jnp.where(i < nb - 1, dyn_ref[...], 0.0)
        _shifted_copies(sh_ref, tb)
        for l0 in range(0, D, lane_step):
            lanes = slice(l0, l0 + lane_step)

            def chunk(ci, sums):
                r0 = pl.multiple_of(ci * CONV_ROWS, CONV_ROWS)
                a_c = a_ref[pl.ds(r0, CONV_ROWS), lanes]
                da = jnp.zeros((CONV_ROWS, lane_step), F32)
                new = []
                for k in range(CONV_WIDTH):
                    s_k = _tap_rows(sh_ref, CONV_WIDTH - 1 - k, r0, lanes)
                    da = da + s_k * w_ref[k:k + 1, lanes]
                    new.append(sums[k] + jnp.sum((s_k * a_c).reshape(groups, 8, lane_step), axis=0))
                da_ref[pl.ds(r0, CONV_ROWS), lanes] = da
                return tuple(new)

            sums = lax.fori_loop(0, tb // CONV_ROWS, chunk, tuple(jnp.zeros((8, lane_step), F32) for _ in range(CONV_WIDTH)))
            for k in range(CONV_WIDTH):
                dw_ref[k:k + 1, lanes] += jnp.sum(sums[k], axis=0, keepdims=True)
        da = da_ref[...]
        p1 = p1_ref[...]
        sg = jax.nn.sigmoid(p2_ref[...])
        dp1 = da * sg
        dp2 = da * p1 * (sg * (1.0 - sg))
        dp_ref[:, 0:D] = dp1.astype(dp_ref.dtype)
        dp_ref[:, D:] = dp2.astype(dp_ref.dtype)
        cs_ref[:, 0:D] += jnp.sum(dp1, axis=0, keepdims=True)
        cs_ref[:, D:] += jnp.sum(dp2, axis=0, keepdims=True)

    blk = pl.BlockSpec((tb, D), lambda i: (i, 0))
    return pl.pallas_call(
        body, grid=(nb,),
        in_specs=[blk, pl.BlockSpec((CONV_HALO, D), lambda i: (jnp.minimum((i + 1) * hpb, last), 0)), blk, blk,
                  pl.BlockSpec((tb, D), lambda i: (i, 1)), pl.BlockSpec((CONV_HALO, D), lambda i: (0, 0))],
        out_specs=[pl.BlockSpec((tb, 2 * D), lambda i: (i, 0)), pl.BlockSpec((CONV_HALO, D), lambda i: (0, 0)), pl.BlockSpec((1, 2 * D), lambda i: (0, 0))],
        out_shape=[jax.ShapeDtypeStruct((T, 2 * D), MXU_DTYPE), jax.ShapeDtypeStruct((CONV_HALO, D), F32), jax.ShapeDtypeStruct((1, 2 * D), F32)],
        scratch_shapes=[pltpu.VMEM((8, tb + CONV_HALO, D), F32), pltpu.VMEM((tb, D), F32)],
        compiler_params=_params("arbitrary"), name=name,
    )(dy, dy, a, p, p, w_dw)


def _row_block(rows):
    for tr in (512, 256, 128, 64, 32, 16, 8):
        if rows % tr == 0:
            return tr
    return rows


def _sum_leading(x, *, name):
    n, R, C = x.shape
    tr = _row_block(R)

    def body(x_ref, o_ref):
        acc = x_ref[0]
        for j in range(1, n):
            acc = acc + x_ref[j]
        o_ref[...] = acc

    return pl.pallas_call(
        body, grid=(R // tr,), in_specs=[pl.BlockSpec((n, tr, C), lambda i: (0, i, 0))], out_specs=pl.BlockSpec((tr, C), lambda i: (i, 0)),
        out_shape=jax.ShapeDtypeStruct((R, C), x.dtype), compiler_params=_params("parallel"), name=name,
    )(x)


def _add_pair(x, y, *, name):
    n, R, C = x.shape
    tr = _row_block(R)

    def body(x_ref, y_ref, o_ref):
        o_ref[...] = x_ref[...] + y_ref[...]

    blk = pl.BlockSpec((1, tr, C), lambda j, i: (j, i, 0))
    return pl.pallas_call(
        body, grid=(n, R // tr), in_specs=[blk, blk], out_specs=blk,
        out_shape=jax.ShapeDtypeStruct((n, R, C), x.dtype), compiler_params=_params("parallel", "parallel"), name=name,
    )(x, y)


def _adamw(w, g, m, v, *, name):
    R, C = w.shape
    tr = _row_block(R)
    c1, c2 = 1.0 - ADAM_B1 ** ADAM_STEP, 1.0 - ADAM_B2 ** ADAM_STEP

    def body(w_ref, g_ref, m_ref, v_ref, d_ref, nm_ref, nv_ref):
        g_ = g_ref[...]
        nm = ADAM_B1 * m_ref[...] + (1.0 - ADAM_B1) * g_
        nv = ADAM_B2 * v_ref[...] + (1.0 - ADAM_B2) * (g_ * g_)
        d_ref[...] = -ADAM_LR * ((nm / c1) / (jnp.sqrt(nv / c2) + ADAM_EPS) + ADAM_WD * w_ref[...])
        nm_ref[...] = nm
        nv_ref[...] = nv

    blk = pl.BlockSpec((tr, C), lambda i: (i, 0))
    shp = jax.ShapeDtypeStruct((R, C), F32)
    return pl.pallas_call(
        body, grid=(R // tr,), in_specs=[blk] * 4, out_specs=[blk] * 3, out_shape=[shp] * 3,
        compiler_params=_params("parallel"), name=name,
    )(w, g, m, v)


def _small_reduce(packs, logits, *, name):
    n, R, C = packs.shape

    def body(p_ref, lg_ref, s_ref, dlg_ref):
        acc = p_ref[0]
        for j in range(1, n):
            acc = acc + p_ref[j]
        s_ref[...] = acc
        lg = lg_ref[...]
        e = jnp.exp(lg - jnp.max(lg, axis=0, keepdims=True))
        sm = e / jnp.sum(e, axis=0, keepdims=True)
        dlb = acc[5:6, HG_WIDTH:2 * HG_WIDTH]
        first = lax.broadcasted_iota(jnp.int32, sm.shape, 0) == 0
        dlg_ref[...] = sm[0:1, :] * (jnp.where(first, 1.0, 0.0) - sm) * dlb

    whole = lambda shape: pl.BlockSpec(shape, lambda: (0,) * len(shape))
    return pl.pallas_call(
        body, in_specs=[whole((n, R, C)), whole(logits.shape)], out_specs=[whole((R, C)), whole(logits.shape)],
        out_shape=[jax.ShapeDtypeStruct((R, C), F32), jax.ShapeDtypeStruct(logits.shape, F32)],
        compiler_params=pltpu.CompilerParams(vmem_limit_bytes=VMEM_LIMIT), name=name,
    )(packs, logits)


HBM_SPEC = pl.BlockSpec(memory_space=pl.ANY)


def _place():
    return lax.axis_index("x"), lax.axis_index("y"), lax.axis_index("c")


class _Copies:
    def __init__(self, arrays, out_shapes, n_copies, make, finish):
        self.arrays, self.out_shapes, self.n_copies, self.make, self.finish = list(arrays), list(out_shapes), n_copies, make, finish

    def scratch(self):
        return [pltpu.SemaphoreType.DMA((self.n_copies,)), pltpu.SemaphoreType.DMA((self.n_copies,))]

    def run(self, name):
        n = len(self.arrays)

        def body(*refs):
            copies = self.make(refs[:n], refs[n:2 * n], *refs[2 * n:])
            for cp in copies:
                cp.start()
            for cp in copies:
                cp.wait()

        outs = pl.pallas_call(body, in_specs=[HBM_SPEC] * n, out_specs=[HBM_SPEC] * n, out_shape=self.out_shapes,
                              scratch_shapes=self.scratch(), name=name)(*self.arrays)
        return self.finish(outs)


def _remote(src, dst, send_sems, recv_sems, k, peer):
    return pltpu.make_async_remote_copy(src_ref=src, dst_ref=dst, send_sem=send_sems.at[k], recv_sem=recv_sems.at[k],
                                        device_id=peer, device_id_type=MESH)


def _same_core_peers(x, y, c):
    return [(1 - x, y, c), (x, 1 - y, c), (1 - x, 1 - y, c)]


def _all_peers(x, y, c):
    flip = lambda v, b: 1 - v if b else v
    return [(flip(x, r & 4), flip(y, r & 2), flip(c, r & 1)) for r in range(1, 8)]


def _gather(arrays, peers_of, slot_of, n_slots):
    n_peers = len(peers_of(0, 0, 0))

    def make(ins, outs, send_sems, recv_sems):
        x, y, c = _place()
        slot = slot_of(x, y, c)
        return [_remote(ins[a], outs[a].at[slot], send_sems, recv_sems, a * n_peers + k, peer)
                for a in range(len(arrays)) for k, peer in enumerate(peers_of(x, y, c))]

    def finish(outs):
        slot = slot_of(*_place())
        return [lax.dynamic_update_index_in_dim(o, a, slot, 0) for o, a in zip(outs, arrays)]

    shapes = [jax.ShapeDtypeStruct((n_slots,) + a.shape, a.dtype) for a in arrays]
    return _Copies(arrays, shapes, len(arrays) * n_peers, make, finish)


def _gather_chips(arrays):
    return _gather(arrays, _same_core_peers, lambda x, y, c: 2 * x + y, N_CHIPS)


def _gather_all(arrays):
    return _gather(arrays, _all_peers, lambda x, y, c: 4 * x + 2 * y + c, N_DEV)


def _pair_swap(a):
    def make(ins, outs, send_sems, recv_sems):
        x, y, c = _place()
        return [_remote(ins[0].at[1 - c], outs[0], send_sems, recv_sems, 0, (x, y, 1 - c))]

    return _Copies([a], [jax.ShapeDtypeStruct(a.shape[1:], a.dtype)], 1, make, lambda outs: outs[0])


def _chip_scatter(p):
    def make(ins, outs, send_sems, recv_sems):
        x, y, c = _place()
        return [_remote(ins[0].at[2 * px + py], outs[0].at[2 * x + y], send_sems, recv_sems, k, (px, py, pc))
                for k, (px, py, pc) in enumerate(_same_core_peers(x, y, c))]

    def finish(outs):
        x, y, _ = _place()
        me = 2 * x + y
        return lax.dynamic_update_index_in_dim(outs[0], lax.dynamic_index_in_dim(p, me, 0, keepdims=False), me, 0)

    return _Copies([p], [jax.ShapeDtypeStruct(p.shape, p.dtype)], 3, make, finish)


def _owner_scatter(blocks):
    def make(ins, outs, send_sems, recv_sems):
        x, y, c = _place()
        return [_remote(ins[0].at[2 * px + py, pc], outs[0].at[4 * x + 2 * y + c], send_sems, recv_sems, k, (px, py, pc))
                for k, (px, py, pc) in enumerate(_all_peers(x, y, c))]

    def finish(outs):
        x, y, c = _place()
        mine = lax.dynamic_index_in_dim(lax.dynamic_index_in_dim(blocks, 2 * x + y, 0, keepdims=False), c, 0, keepdims=False)
        return lax.dynamic_update_index_in_dim(outs[0], mine, 4 * x + 2 * y + c, 0)

    return _Copies([blocks], [jax.ShapeDtypeStruct((N_DEV,) + blocks.shape[2:], blocks.dtype)], N_DEV - 1, make, finish)


def _pair_gather(q):
    def make(ins, outs, send_sems, recv_sems):
        x, y, c = _place()
        return [_remote(ins[0], outs[0].at[c], send_sems, recv_sems, 0, (x, y, 1 - c))]

    return _Copies([q], [jax.ShapeDtypeStruct((2,) + q.shape, q.dtype)], 1, make,
                   lambda outs: lax.dynamic_update_index_in_dim(outs[0], q, _place()[2], 0))


def _grad_blocks(dw, kind):
    if kind == "cols2d":
        K, N = dw.shape
        b = dw.reshape(2, K // 2, N_CHIPS, N // N_CHIPS).transpose(2, 0, 1, 3)
    elif kind == "rows2d":
        b = dw.reshape(N_CHIPS, 2, dw.shape[0] // 8, dw.shape[1])
    elif kind == "cols3d":
        L, K, N = dw.shape
        b = dw.reshape(L, K, N_CHIPS, N // N_CHIPS).transpose(2, 0, 1, 3)
    else:
        L, K, N = dw.shape
        b = dw.reshape(L, N_CHIPS, K // N_CHIPS, N).transpose(1, 0, 2, 3)
    return b.reshape(N_CHIPS, 2, -1, D_MODEL)


def _pad_rows(a, rows):
    return jnp.concatenate([a, jnp.zeros((rows - a.shape[0],) + a.shape[1:], a.dtype)], axis=0)


def _forward_backward(x, target, W, late_weights=None, early_grads=None, last_grads=None):
    row = lambda a: a.reshape(1, -1)
    relu2 = lambda acc: (jnp.square(jnp.maximum(acc, 0.0)),)
    normed = lambda h, g: h * lax.rsqrt(jnp.mean(h * h, axis=-1, keepdims=True) + RMS_EPS) * g

    def residual_norm(acc, res, g):
        h = res + acc
        return h, normed(h, g)

    def norm_bwd(du, h_blk, dres_blk, g):
        dx, dg_terms = _rms_bwd_math(h_blk, g, du)
        dh = dres_blk + dx
        return dh, dh, jnp.sum(dg_terms, axis=0, keepdims=True), jnp.sum(dh, axis=0, keepdims=True)

    def matmul_norm_bwd(dy, w, h_in, g, dres, rider=None, *, tk=1024, name):
        return _matmul(dy, w, mode="nt", out_dtypes=[F32, MXU_DTYPE], epilogue=norm_bwd, tiles=[h_in, dres], rows=[row(g)],
                       n_sums=2, rider=rider, tm=512, tk=tk, name=name)

    G = {}

    u0 = _rmsnorm_fwd(x, row(W["norm_mix_g"][0]), name="norm_mix0")
    proj, qkv = _matmul(u0, W["w_in"], mode="nn", out_dtypes=[F32, MXU_DTYPE], epilogue=lambda acc: (acc, acc), tn=896, name="in_proj")
    o_sb, got = _sb_fwd(qkv, late_weights and late_weights[0], name="sb_fwd")
    if late_weights:
        W = {**W, **late_weights[1](got)}
    hg_out, hg_o, hg_states = _hg_fwd(proj, W["hg_lb_logits"], row(W["hg_norm_g"]), name="hg_fwd")
    mix = jnp.concatenate([o_sb, hg_out], axis=-1)
    h1, u1 = _matmul(mix, W["w_out"], mode="nn", out_dtypes=[F32, MXU_DTYPE], epilogue=residual_norm, tiles=[x],
                     rows=[row(W["norm_ffn_g"][0])], name="out_proj")
    r0 = _matmul(u1, W["w_ff1"][0], mode="nn", out_dtypes=[MXU_DTYPE], epilogue=relu2, name="ff1_0")
    h2, u2 = _matmul(r0, W["w_ff2"][0], mode="nn", out_dtypes=[F32, MXU_DTYPE], epilogue=residual_norm, tiles=[h1],
                     rows=[row(W["norm_mix_g"][1])], name="ff2_0")
    p = _matmul(u2, W["w_glu"], mode="nn", out_dtypes=[F32], epilogue=lambda acc, b: (acc + b,), rows=[row(W["b_glu"])], name="glu_proj")
    w_dw = _pad_rows(W["w_dw"], CONV_HALO)
    ca, cy, cact = _conv_fwd(p, w_dw, row(W["b_dw"]), row(W["ln_g"]), row(W["ln_b"]), name="conv_fwd")
    h3, u3 = _matmul(cact, W["w_pw"], mode="nn", out_dtypes=[F32, MXU_DTYPE], epilogue=lambda acc, res, b, g: residual_norm(acc + b, res, g),
                     tiles=[h2], rows=[row(W["b_pw"]), row(W["norm_ffn_g"][1])], name="pw_proj")
    r1 = _matmul(u3, W["w_ff1"][1], mode="nn", out_dtypes=[MXU_DTYPE], epilogue=relu2, name="ff1_1")
    h4 = _matmul(r1, W["w_ff2"][1], mode="nn", out_dtypes=[F32], epilogue=lambda acc, res: (res + acc,), tiles=[h3], name="ff2_1")

    dh4, dh4_m, G["final_norm_g"], loss = _loss_head(h4, row(W["final_norm_g"]), target, name="loss_head")

    def mlp_bwd(dh, dh_m, h_in, u, r, layer, tag):
        d_relu2 = lambda acc, r_blk: (acc * (2.0 * jnp.sqrt(r_blk.astype(F32))),)
        da = _matmul(dh_m, W["w_ff2"][layer], mode="nt", out_dtypes=[MXU_DTYPE], epilogue=d_relu2, tiles=[r], name="d_ff2_act" + tag)
        dw2 = _matmul(r, dh_m, mode="tn", out_dtypes=[F32], name="d_ff2_w" + tag)
        dw1 = _matmul(u, da, mode="tn", out_dtypes=[F32], name="d_ff1_w" + tag)
        dh_in, dh_in_m, dg, cs = matmul_norm_bwd(da, W["w_ff1"][layer], h_in, W["norm_ffn_g"][layer], dh, name="d_ff1_act" + tag)
        return dh_in, dh_in_m, dg, cs, dw1, dw2

    dh3, dh3_m, dg_ffn1, cs_h3, dw1_1, dw2_1 = mlp_bwd(dh4, dh4_m, h3, u3, r1, 1, "1")
    G["b_pw"] = cs_h3
    dact = _matmul(dh3_m, W["w_pw"], mode="nt", out_dtypes=[F32], name="d_pw_act")
    G["w_pw"] = _matmul(cact, dh3_m, mode="tn", out_dtypes=[F32], name="d_pw_w")
    dy, G["ln_g"], G["ln_b"], G["b_dw"] = _conv_bwd_norm(dact, cy, row(W["ln_g"]), row(W["ln_b"]), name="d_conv_norm")
    dp, G["w_dw"], G["b_glu"] = _conv_bwd_taps(dy, ca, p, w_dw, name="d_conv_taps")
    G["w_glu"] = _matmul(u2, dp, mode="tn", out_dtypes=[F32], name="d_glu_w")
    dh2, dh2_m, dg_mix1, _ = matmul_norm_bwd(dp, W["w_glu"], h2, W["norm_mix_g"][1], dh3, name="d_glu_act")
    dh1, dh1_m, dg_ffn0, _, dw1_0, dw2_0 = mlp_bwd(dh2, dh2_m, h1, u1, r0, 0, "0")
    G["w_ff1"], G["w_ff2"] = jnp.stack([dw1_0, dw1_1]), jnp.stack([dw2_0, dw2_1])
    G["norm_ffn_g"] = jnp.concatenate([dg_ffn0, dg_ffn1], axis=0)
    dmix = _matmul(dh1_m, W["w_out"], mode="nt", out_dtypes=[F32], name="d_out_act")
    G["w_out"] = _matmul(mix, dh1_m, mode="tn", out_dtypes=[F32], name="d_out_w")
    riding = early_grads(G) if early_grads else None
    (dsq, dsk, dsv), got = _sb_bwd(qkv, dmix, riding, name="sb_bwd")
    d_hg, G["hg_lb"], G["hg_norm_g"] = _hg_bwd(proj, hg_o, hg_states, dmix, W["hg_lb_logits"], row(W["hg_norm_g"]), name="hg_bwd")
    dproj = jnp.concatenate([dsq, dsk, dsv, d_hg], axis=-1).astype(MXU_DTYPE)
    G["w_in"] = _matmul(u0, dproj, mode="tn", out_dtypes=[F32], tn=896, name="d_in_w")
    last = last_grads(G) if last_grads else None
    res = matmul_norm_bwd(dproj, W["w_in"], x, W["norm_mix_g"][0], dh1, last, tk=896, name="d_in_act")
    (dx, _, dg_mix0, _), got_last = res if last_grads else (res, [])
    G["norm_mix_g"] = jnp.concatenate([dg_mix0, dg_mix1], axis=0)
    return loss, dx, G, [(riding, got), (last, got_last)]


BIG = (("w_out_ab", "w_out", "rows2d"), ("conv_w_glu", "w_glu", "cols2d"), ("conv_w_pw", "w_pw", "rows2d"),
       ("w_ff1", "w_ff1", "cols3d"), ("w_ff2", "w_ff2", "rows3d"), ("w_in_ab", "w_in", "cols2d"))
LATE = BIG[:-1]
SMALL_SHARDED = ("conv_b_glu", "conv_w_dw", "conv_b_dw", "conv_ln_g", "conv_ln_b", "conv_b_pw")
REPLICATED = ("norm_mix_g", "norm_ffn_g", "hg_lb_logits", "hg_norm_g", "final_norm_g")
ORDER = ("norm_mix_g", "norm_ffn_g", "w_in_ab", "w_out_ab", "hg_lb_logits", "hg_norm_g", "conv_w_glu", "conv_b_glu",
         "conv_w_dw", "conv_b_dw", "conv_ln_g", "conv_ln_b", "conv_w_pw", "conv_b_pw", "w_ff1", "w_ff2", "final_norm_g")


def _step(x, loss_target, w, m, v):
    D = D_MODEL
    x2, t2 = x.reshape(-1, D), loss_target.reshape(-1, D)
    chip = 2 * lax.axis_index("x") + lax.axis_index("y")
    c = lax.axis_index("c")

    small_in = jnp.concatenate([w["conv_b_glu"].reshape(2, 256), w["conv_w_dw"].reshape(CONV_WIDTH, 256)] +
                               [w[n].reshape(1, 256) for n in ("conv_b_dw", "conv_ln_g", "conv_ln_b", "conv_b_pw")], axis=0)
    g_in, gs = _gather_chips([w["w_in_ab"].astype(MXU_DTYPE), _pad_rows(small_in, 40)]).run("gather_first_weights")
    vec = lambda r0, r1: gs[:, r0:r1].transpose(1, 0, 2).reshape(r1 - r0, N_CHIPS * 256)
    W = {
        "w_in": _ChipWeight(g_in[:, 0], "cols"),
        "b_glu": gs[:, 0:2].reshape(2 * D), "w_dw": vec(2, 33), "b_dw": vec(33, 34)[0], "ln_g": vec(34, 35)[0],
        "ln_b": vec(35, 36)[0], "b_pw": vec(36, 37)[0],
        "norm_mix_g": w["norm_mix_g"], "norm_ffn_g": w["norm_ffn_g"], "hg_lb_logits": w["hg_lb_logits"],
        "hg_norm_g": w["hg_norm_g"], "final_norm_g": w["final_norm_g"],
    }
    late = _gather_chips([w[n].astype(MXU_DTYPE) for n, _, _ in LATE])

    def assemble(got):
        gw = dict(zip([s for _, s, _ in LATE], late.finish(got)))
        layers = lambda g, along: [_ChipWeight(g, along, (layer,)) for layer in range(2)]
        return {"w_out": gw["w_out"].reshape(D, D), "w_glu": _ChipWeight(gw["w_glu"][:, 0], "cols"), "w_pw": gw["w_pw"].reshape(D, D),
                "w_ff1": layers(gw["w_ff1"], "cols"), "w_ff2": layers(gw["w_ff2"], "rows")}

    def early_grads(G):
        return _owner_scatter(jnp.concatenate([_grad_blocks(G[s], kind) for _, s, kind in LATE], axis=2))

    def last_grads(G):
        blocks = _grad_blocks(G["w_in"], "cols2d")
        from_pair = _pair_swap(blocks.transpose(1, 0, 2, 3)).run("grads_pair_swap_in")
        return _chip_scatter(_add_pair(lax.dynamic_index_in_dim(blocks, c, axis=1, keepdims=False), from_pair, name="grads_pair_add_in"))

    loss, dx, G, riders = _forward_backward(x2, t2, W, (late, assemble), early_grads, last_grads)
    halves = [_sum_leading(copies.finish(got), name="grads_add_" + tag) for (copies, got), tag in zip(riders, ("late", "in"))]
    half = jnp.concatenate(halves, axis=0)
    full = _pair_gather(half).run("grads_pair_gather")

    pack = jnp.concatenate([
        G["norm_mix_g"], G["norm_ffn_g"], G["final_norm_g"], jnp.concatenate([G["hg_norm_g"], G["hg_lb"]], axis=1),
        _pad_rows(jnp.broadcast_to(loss, (1, D)), 2), G["b_glu"].reshape(2, D), G["w_dw"], G["b_dw"], G["ln_g"], G["ln_b"], G["b_pw"],
    ], axis=0)
    pack = _pad_rows(pack, SMALL_ROWS)
    (packs,) = _gather_all([pack]).run("gather_small_grads")
    ssum, d_logits = _small_reduce(packs, w["hg_lb_logits"], name="reduce_small_grads")
    cut = lambda r0, r1: lax.dynamic_slice(ssum, (r0, chip * 256), (r1 - r0, 256))
    grads = {
        "norm_mix_g": ssum[0:2], "norm_ffn_g": ssum[2:4], "final_norm_g": ssum[4], "hg_norm_g": ssum[5, :HG_WIDTH].reshape(1, HG_HEADS, HG_DH),
        "hg_lb_logits": d_logits,
        "conv_b_glu": lax.dynamic_slice(ssum[8:10].reshape(1, 2 * D), (0, chip * 512), (1, 512)),
        "conv_w_dw": cut(10, 10 + CONV_WIDTH).reshape(1, CONV_WIDTH, 256),
        "conv_b_dw": cut(42, 43), "conv_ln_g": cut(43, 44), "conv_ln_b": cut(44, 45), "conv_b_pw": cut(45, 46),
    }
    loss_out = ssum[6, 0]

    off = 0
    for n, s, kind in BIG:
        shard = w[n].shape
        rows = w[n].size // (2 * D)
        grads[n] = full[:, off:off + rows].reshape(shard)
        off += rows

    delta, new_m, new_v = {}, {}, {}
    for n, _, _ in BIG:
        view = lambda a: a.reshape(-1, a.shape[-1])
        outs = _adamw(view(w[n]), view(grads[n]), view(m[n]), view(v[n]), name="adamw_" + n)
        delta[n], new_m[n], new_v[n] = (o.reshape(w[n].shape) for o in outs)
    small = SMALL_SHARDED + REPLICATED
    sizes = [w[n].size for n in small]
    total = sum(sizes)
    rows = -(-total // (8 * D)) * 8
    packed = lambda d: _pad_rows(jnp.concatenate([d[n].reshape(-1) for n in small]).reshape(-1, 128), rows * 8).reshape(rows, D)
    outs = _adamw(packed(w), packed(grads), packed(m), packed(v), name="adamw_small")
    off = 0
    for n, size in zip(small, sizes):
        delta[n], new_m[n], new_v[n] = (o.reshape(-1)[off:off + size].reshape(w[n].shape) for o in outs)
        off += size
    grads = {n: grads[n].reshape(w[n].shape) for n in ORDER}
    return (loss_out, dx.reshape(x.shape), *[grads[n] for n in ORDER], *[delta[n] for n in ORDER],
            *[new_m[n] for n in ORDER], *[new_v[n] for n in ORDER])


def kernel(x, norm_mix_g, norm_ffn_g, w_in_ab, w_out_ab, hg_lb_logits, hg_norm_g, conv_w_glu, conv_b_glu, conv_w_dw, conv_b_dw, conv_ln_g, conv_ln_b, conv_w_pw, conv_b_pw, w_ff1, w_ff2, final_norm_g, loss_target, m_norm_mix_g, m_norm_ffn_g, m_w_in_ab, m_w_out_ab, m_hg_lb_logits, m_hg_norm_g, m_conv_w_glu, m_conv_b_glu, m_conv_w_dw, m_conv_b_dw, m_conv_ln_g, m_conv_ln_b, m_conv_w_pw, m_conv_b_pw, m_w_ff1, m_w_ff2, m_final_norm_g, v_norm_mix_g, v_norm_ffn_g, v_w_in_ab, v_w_out_ab, v_hg_lb_logits, v_hg_norm_g, v_conv_w_glu, v_conv_b_glu, v_conv_w_dw, v_conv_b_dw, v_conv_ln_g, v_conv_ln_b, v_conv_w_pw, v_conv_b_pw, v_w_ff1, v_w_ff2, v_final_norm_g):
    args = locals()
    w = {n: args[n] for n in ORDER}
    m = {n: args["m_" + n] for n in ORDER}
    v = {n: args["v_" + n] for n in ORDER}
    return _step(x, loss_target, w, m, v)
```

```python
import functools

import jax
import jax.numpy as jnp
from jax import lax
from jax.experimental import pallas as pl
from jax.experimental.pallas import tpu as pltpu

F32 = jnp.float32
MXU_DTYPE = jnp.bfloat16
MESH = pl.DeviceIdType.MESH

D_MODEL = 1024
SB_HEADS, SB_DH, SB_WIDTH = 8, 64, 512
SB_KEYS = 512
SB_SUB = 256
SB_ROWS_FWD, SB_ROWS_BWD = 512, 256
HG_HEADS, HG_DH, HG_WIDTH = 4, 128, 512
HG_CHUNK = 16
HG_TOKENS = 256
CONV_WIDTH = 31
CONV_HALO = 32
CONV_ROWS = 32
RMS_EPS = 1e-6
LN_EPS = 1e-5
N_CHIPS = 4
N_DEV = 8
SHARD = D_MODEL // N_CHIPS
SMALL_IN_ROWS = 40
SMALL_ROWS = 48
VMEM_LIMIT = 56 * 1024 * 1024

ADAM_LR, ADAM_B1, ADAM_B2, ADAM_EPS, ADAM_WD, ADAM_STEP = 0.001, 0.9, 0.999, 1e-08, 0.01, 10


def _params(*sem):
    return pltpu.CompilerParams(dimension_semantics=sem, vmem_limit_bytes=VMEM_LIMIT)


def _mx(v):
    return v.astype(MXU_DTYPE)


def _dot(a, b):
    return jnp.dot(_mx(a), _mx(b), preferred_element_type=F32)


def _dot_nt(a, b):
    return lax.dot_general(_mx(a), _mx(b), (((1,), (1,)), ((), ())), preferred_element_type=F32)


def _dot_tn(a, b):
    return lax.dot_general(_mx(a), _mx(b), (((0,), (0,)), ((), ())), preferred_element_type=F32)


def _neg_abs(x):
    bits = lax.bitcast_convert_type(x, jnp.uint32) | jnp.uint32(0x80000000)
    return lax.bitcast_convert_type(bits, F32)


def _key_order_sums(v, tri2, later):
    hi = _mx(v)
    lo = _mx(v - hi.astype(F32))
    n = SB_KEYS // SB_SUB
    blocks = [slice(b * SB_SUB, (b + 1) * SB_SUB) for b in range(n)]
    totals = [jnp.sum(v[:, sl], axis=1, keepdims=True) for sl in blocks]
    sums = []
    for b, sl in enumerate(blocks):
        s = jnp.dot(jnp.concatenate([hi[:, sl], lo[:, sl]], axis=1), tri2, preferred_element_type=F32)
        for o in (range(b + 1, n) if later else range(b)):
            s = s + totals[o]
        sums.append(s)
    return jnp.concatenate(sums, axis=1), functools.reduce(lambda a, b: a + b, totals)


class _ChipWeight:
    def __init__(self, parts, along, lead=()):
        self.parts, self.along, self.lead = parts, along, tuple(lead)
        r, c = parts.shape[-2:]
        self.shape = (r, N_CHIPS * c) if along == "cols" else (N_CHIPS * r, c)

    def _gathered_is_n(self, mode):
        return (self.along == "cols") == (mode in ("nn", "tn"))

    def tile(self, mode, tn, tk):
        r, c = self.parts.shape[-2:]
        part = c if self.along == "cols" else r
        return (part, tk) if self._gathered_is_n(mode) else (tn, part)

    def spec(self, mode, tn, tk):
        squeezed = (None,) * (1 + len(self.lead))
        lead, cols, by_n = self.lead, self.along == "cols", self._gathered_is_n(mode)
        block = (tn, tk) if mode == "nt" else (tk, tn)

        def index(i, j, k):
            chip, other = (j, k) if by_n else (k, j)
            return (chip,) + lead + ((other, 0) if cols else (0, other))

        return pl.BlockSpec(squeezed + block, index)


def _matmul(a, b, *, mode, out_dtypes, epilogue=None, tiles=(), rows=(), n_sums=0, rider=None, tm=1024, tn=1024, tk=1024, name):
    b_shape = b.shape
    if mode == "nn":
        (M, K), N = a.shape, b_shape[1]
    elif mode == "nt":
        (M, K), N = a.shape, b_shape[0]
    else:
        (K, M), N = a.shape, b_shape[1]
    if isinstance(b, _ChipWeight):
        tn, tk = b.tile(mode, tn, tk)
    tm, tn, tk = min(tm, M), min(tn, N), min(tk, K)
    assert M % tm == 0 and N % tn == 0 and K % tk == 0, (name, M, N, K)
    nk = K // tk
    a_spec = pl.BlockSpec((tk, tm), lambda i, j, k: (k, i)) if mode == "tn" else pl.BlockSpec((tm, tk), lambda i, j, k: (i, k))
    if isinstance(b, _ChipWeight):
        b_spec, b = b.spec(mode, tn, tk), b.parts
    else:
        b_spec = pl.BlockSpec((tn, tk), lambda i, j, k: (j, k)) if mode == "nt" else pl.BlockSpec((tk, tn), lambda i, j, k: (k, j))
    dims = {"nn": ((1,), (0,)), "nt": ((1,), (1,)), "tn": ((0,), (0,))}[mode]
    n_t, n_r, n_o = len(tiles), len(rows), len(out_dtypes)
    n_x = 0 if rider is None else len(rider.arrays)
    grid = (M // tm, N // tn, nk)
    if epilogue is None:
        epilogue = lambda acc: (acc,)

    def body(a_ref, b_ref, *rest):
        extra, x_in, rest = rest[:n_t + n_r], rest[n_t + n_r:n_t + n_r + n_x], rest[n_t + n_r + n_x:]
        outs, sums, x_out, acc_ref, sems = rest[:n_o], rest[n_o:n_o + n_sums], rest[n_o + n_sums:n_o + n_sums + n_x], rest[n_o + n_sums + n_x], rest[n_o + n_sums + n_x + 1:]
        i, j, k = (pl.program_id(d) for d in range(3))
        if rider is not None:
            @pl.when((i == 0) & (j == 0) & (k == 0))
            def _():
                for cp in rider.make(x_in, x_out, *sems):
                    cp.start()

        @pl.when(k == 0)
        def _():
            acc_ref[...] = jnp.zeros_like(acc_ref)

        acc_ref[...] += lax.dot_general(_mx(a_ref[...]), _mx(b_ref[...]), (dims, ((), ())), preferred_element_type=F32)

        @pl.when(k == nk - 1)
        def _():
            res = epilogue(acc_ref[...], *[e[...] for e in extra])
            for o_ref, r in zip(outs, res[:n_o]):
                o_ref[...] = r.astype(o_ref.dtype)
            for s_ref, r in zip(sums, res[n_o:]):
                @pl.when(i == 0)
                def _():
                    s_ref[...] = jnp.zeros_like(s_ref)

                s_ref[...] += r

        if rider is not None:
            @pl.when((i == grid[0] - 1) & (j == grid[1] - 1) & (k == grid[2] - 1))
            def _():
                for cp in rider.make(x_in, x_out, *sems):
                    cp.wait()

    tile_spec = pl.BlockSpec((tm, tn), lambda i, j, k: (i, j))
    row_spec = pl.BlockSpec((1, tn), lambda i, j, k: (0, j))
    ordered = n_sums > 0 or rider is not None
    outs = pl.pallas_call(
        body, grid=grid,
        in_specs=[a_spec, b_spec] + [tile_spec] * n_t + [row_spec] * n_r + [HBM_SPEC] * n_x,
        out_specs=[tile_spec] * n_o + [row_spec] * n_sums + [HBM_SPEC] * n_x,
        out_shape=[jax.ShapeDtypeStruct((M, N), dt) for dt in out_dtypes] + [jax.ShapeDtypeStruct((1, N), F32)] * n_sums
        + ([] if rider is None else rider.out_shapes),
        scratch_shapes=[pltpu.VMEM((tm, tn), F32)] + ([] if rider is None else rider.scratch()),
        compiler_params=_params(*(("arbitrary",) * 3 if ordered else ("parallel", "parallel", "arbitrary"))), name=name,
    )(a, b, *tiles, *rows, *([] if rider is None else rider.arrays))
    res = outs[0] if n_o + n_sums == 1 else outs[:n_o + n_sums]
    return res if rider is None else (res, outs[n_o + n_sums:])


def _token_block(T):
    return min(512, T)


def _rmsnorm_fwd(h, g, *, name):
    T, D = h.shape
    tb = _token_block(T)

    def body(h_ref, g_ref, u_ref):
        x = h_ref[...]
        r = lax.rsqrt(jnp.mean(x * x, axis=-1, keepdims=True) + RMS_EPS)
        u_ref[...] = (x * r * g_ref[...]).astype(u_ref.dtype)

    blk = pl.BlockSpec((tb, D), lambda i: (i, 0))
    return pl.pallas_call(
        body, grid=(T // tb,), in_specs=[blk, pl.BlockSpec((1, D), lambda i: (0, 0))], out_specs=blk,
        out_shape=jax.ShapeDtypeStruct((T, D), MXU_DTYPE), compiler_params=_params("parallel"), name=name,
    )(h, g)


def _rms_bwd_math(x, g, du):
    r = lax.rsqrt(jnp.mean(x * x, axis=-1, keepdims=True) + RMS_EPS)
    gd = g * du
    dx = r * gd - x * (r * r * r) * jnp.mean(gd * x, axis=-1, keepdims=True)
    return dx, du * x * r


def _loss_head(h, g, target, *, name):
    T, D = h.shape
    tb = _token_block(T)

    def body(h_ref, g_ref, t_ref, dh_ref, dhm_ref, dg_ref, loss_ref):
        @pl.when(pl.program_id(0) == 0)
        def _():
            dg_ref[...] = jnp.zeros_like(dg_ref)
            loss_ref[...] = jnp.zeros_like(loss_ref)

        x, gg = h_ref[...], g_ref[...]
        r = lax.rsqrt(jnp.mean(x * x, axis=-1, keepdims=True) + RMS_EPS)
        diff = x * r * gg - t_ref[...]
        per_token = jnp.mean(diff * diff, axis=-1, keepdims=True)
        loss_ref[...] += 0.5 * jnp.sum(per_token, axis=0, keepdims=True)
        dx, dg_terms = _rms_bwd_math(x, gg, diff / D)
        dh_ref[...] = dx
        dhm_ref[...] = dx.astype(dhm_ref.dtype)
        dg_ref[...] += jnp.sum(dg_terms, axis=0, keepdims=True)

    blk = pl.BlockSpec((tb, D), lambda i: (i, 0))
    row = pl.BlockSpec((1, D), lambda i: (0, 0))
    return pl.pallas_call(
        body, grid=(T // tb,), in_specs=[blk, row, blk], out_specs=[blk, blk, row, pl.BlockSpec((1, 1), lambda i: (0, 0))],
        out_shape=[jax.ShapeDtypeStruct((T, D), F32), jax.ShapeDtypeStruct((T, D), MXU_DTYPE), jax.ShapeDtypeStruct((1, D), F32),
                   jax.ShapeDtypeStruct((1, 1), F32)],
        compiler_params=_params("arbitrary"), name=name,
    )(h, g, target)


def _sb_scores(qm, ks, later, tri, mask, need_log_beta=True):
    z = _dot_nt(qm, ks)
    sp = jnp.maximum(z, 0.0) + jnp.log(1.0 + jnp.exp(_neg_abs(z)))
    lb = z - sp if need_log_beta else None
    if mask is not None:
        sp = jnp.where(mask, sp, 0.0)
    after, total = _key_order_sums(sp, tri, later=True)
    w = jnp.exp((lb if need_log_beta else z) - (after + later))
    if mask is not None:
        w = jnp.where(mask, w, 0.0)
    return total, lb, w


def _sb_setup(q_ref, rows, inclusive=False):
    i, hsel = pl.program_id(1), pl.program_id(2)
    lane = lax.broadcasted_iota(jnp.int32, (rows, 2 * SB_DH), 1)
    mine = (lane >= SB_DH) == (hsel == 1)
    diag = (i * rows) // SB_KEYS
    t = i * rows + lax.broadcasted_iota(jnp.int32, (rows, SB_KEYS), 0)
    s = diag * SB_KEYS + lax.broadcasted_iota(jnp.int32, (rows, SB_KEYS), 1)
    a = lax.broadcasted_iota(jnp.int32, (2 * SB_SUB, SB_SUB), 0) % SB_SUB
    b = lax.broadcasted_iota(jnp.int32, (2 * SB_SUB, SB_SUB), 1)
    return i, hsel, mine, diag, s < t, _mx(a >= b if inclusive else a > b), _mx(a < b)


def _sb_keys(j):
    return pl.ds(pl.multiple_of(j * SB_KEYS, SB_KEYS), SB_KEYS)


def _sb_descend(n, step, carry):
    carry = lax.fori_loop(0, n // 2, lambda it, cr: step(n - 2 - 2 * it, step(n - 1 - 2 * it, cr)), carry)
    return lax.cond(n % 2 == 1, lambda cr: step(0, cr), lambda cr: cr, carry)


def _sb_ascend(n, step, carry):
    odd = n % 2
    carry = lax.cond(odd == 1, lambda cr: step(0, cr), lambda cr: cr, carry)
    return lax.fori_loop(0, n // 2, lambda it, cr: step(odd + 2 * it + 1, step(odd + 2 * it, cr)), carry)


def _sb_call(body, qkv, extra_in, out_blocks, out_dtype, scratch, rider, rows, *, name):
    T = qkv.shape[0]
    n_pairs = SB_HEADS // 2
    grid = (n_pairs, T // rows, 2)
    pair = lambda col0: pl.BlockSpec((rows, 2 * SB_DH), lambda p, i, h: (i, col0 + p))
    whole = lambda col0: pl.BlockSpec((T, 2 * SB_DH), lambda p, i, h: (0, col0 + p))
    in_specs = [pair(0), whole(n_pairs), whole(2 * n_pairs)] + [pair(0)] * len(extra_in)
    out_specs = [pair(0) if kind == "pair" else whole(0) for kind in out_blocks]
    n_in, n_out, n_r = len(in_specs), len(out_specs), 0 if rider is None else len(rider.arrays)

    def kernel_body(*refs):
        ins, r_in = refs[:n_in], refs[n_in:n_in + n_r]
        outs, r_out = refs[n_in + n_r:n_in + n_r + n_out], refs[n_in + n_r + n_out:n_in + 2 * n_r + n_out]
        rest = refs[n_in + 2 * n_r + n_out:]
        ids = [pl.program_id(a) for a in range(3)]
        if rider is not None:
            @pl.when((ids[0] == 0) & (ids[1] == 0) & (ids[2] == 0))
            def _():
                for cp in rider.make(r_in, r_out, *rest[len(scratch):]):
                    cp.start()

        body(ins, outs, rest[:len(scratch)])
        if rider is not None:
            @pl.when((ids[0] == grid[0] - 1) & (ids[1] == grid[1] - 1) & (ids[2] == grid[2] - 1))
            def _():
                for cp in rider.make(r_in, r_out, *rest[len(scratch):]):
                    cp.wait()

    res = pl.pallas_call(
        kernel_body, grid=grid, in_specs=in_specs + [HBM_SPEC] * n_r, out_specs=out_specs + [HBM_SPEC] * n_r,
        out_shape=[jax.ShapeDtypeStruct((T, SB_WIDTH), out_dtype)] * n_out + ([] if rider is None else rider.out_shapes),
        scratch_shapes=list(scratch) + ([] if rider is None else rider.scratch()),
        compiler_params=_params("arbitrary", "arbitrary", "arbitrary"), name=name,
    )(qkv, qkv, qkv, *extra_in, *([] if rider is None else rider.arrays))
    return res[:n_out], res[n_out:]


def _sb_fwd(qkv, rider=None, *, name):
    rows = min(SB_ROWS_FWD, qkv.shape[0])
    scale = SB_DH ** -0.5

    def body(ins, outs, _):
        (q_ref, k_ref, v_ref), (o_ref,) = ins, outs
        i, hsel, mine, diag, mask, tri, _ = _sb_setup(q_ref, rows, inclusive=True)
        qm = jnp.where(mine, q_ref[...], 0) * scale

        def tile(j, m, later, acc):
            total, _, w = _sb_scores(qm, k_ref[_sb_keys(j), :], later, tri, m, need_log_beta=False)
            return later + total, acc + _dot(w, v_ref[_sb_keys(j), :])

        carry = tile(diag, mask, jnp.zeros((rows, 1), F32), jnp.zeros((rows, 2 * SB_DH), F32))
        _, acc = _sb_descend(diag, lambda j, cr: tile(j, None, *cr), carry)
        res = jnp.where(mine, acc, 0.0).astype(o_ref.dtype)

        @pl.when(hsel == 0)
        def _():
            o_ref[...] = res

        @pl.when(hsel == 1)
        def _():
            o_ref[...] += res

    (o,), got = _sb_call(body, qkv, [], ["pair"], MXU_DTYPE, [], rider, rows, name=name)
    return o, got


def _sb_bwd(qkv, dmix, rider=None, *, name):
    T = qkv.shape[0]
    rows = min(SB_ROWS_BWD, T)
    scale = SB_DH ** -0.5

    def body(ins, outs, scratch):
        (q_ref, k_ref, v_ref, do_ref), (dq_ref, dk_ref, dv_ref), (da_ref, beta_ref) = ins, outs, scratch
        i, hsel, mine, diag, mask, tri, tri_before = _sb_setup(q_ref, rows)

        @pl.when((i == 0) & (hsel == 0))
        def _():
            dk_ref[...] = jnp.zeros_like(dk_ref)
            dv_ref[...] = jnp.zeros_like(dv_ref)

        qm = jnp.where(mine, q_ref[...], 0) * scale
        do_m = _mx(jnp.where(mine, do_ref[...], 0.0))

        def weights(j, m, later):
            total, lb, w = _sb_scores(qm, k_ref[_sb_keys(j), :], later, tri, m)
            da_ref[j] = _dot_nt(do_m, v_ref[_sb_keys(j), :]) * w
            beta_ref[j] = jnp.exp(lb)
            dv_ref[_sb_keys(j), :] += _dot_tn(w, do_m)
            return later + total

        later = weights(diag, mask, jnp.zeros((rows, 1), F32))
        _sb_descend(diag, lambda j, c: weights(j, None, c), later)

        def logits(j, m, before, dq):
            da = da_ref[j]
            earlier, total = _key_order_sums(da, tri_before, later=False)
            dz = da - beta_ref[j] * (da + earlier + before)
            if m is not None:
                dz = jnp.where(m, dz, 0.0)
            dz = _mx(dz)
            dk_ref[_sb_keys(j), :] += _dot_tn(dz, qm)
            return before + total, dq + _dot(dz, k_ref[_sb_keys(j), :])

        carry = (jnp.zeros((rows, 1), F32), jnp.zeros((rows, 2 * SB_DH), F32))
        carry = _sb_ascend(diag, lambda j, cr: logits(j, None, *cr), carry)
        res = jnp.where(mine, logits(diag, mask, *carry)[1] * scale, 0.0)

        @pl.when(hsel == 0)
        def _():
            dq_ref[...] = res

        @pl.when(hsel == 1)
        def _():
            dq_ref[...] += res

    n_tiles = T // SB_KEYS
    scratch = [pltpu.VMEM((n_tiles, rows, SB_KEYS), F32), pltpu.VMEM((n_tiles, rows, SB_KEYS), F32)]
    return _sb_call(body, qkv, [dmix], ["pair", "whole", "whole"], F32, scratch, rider, rows, name=name)


def _chunk_row(n):
    return lax.broadcasted_iota(jnp.int32, (n, HG_DH), 0) % HG_CHUNK


def _chunk_cumsum(x, row, reverse=False):
    n = x.shape[0]
    for sh in (1, 2, 4, 8):
        if reverse:
            x = x + jnp.where(row < HG_CHUNK - sh, pltpu.roll(x, n - sh, 0), 0.0)
        else:
            x = x + jnp.where(row >= sh, pltpu.roll(x, sh, 0), 0.0)
    return x


def _hg_lower_bound(logits_ref):
    lg = logits_ref[...]
    e = jnp.exp(lg - jnp.max(lg, axis=0, keepdims=True))
    return e[0:1, :] / jnp.sum(e, axis=0, keepdims=True)


def _hg_terms(fr, q, lb, row):
    sig = jax.nn.sigmoid(fr)
    f = lb + (1.0 - lb) * sig
    kk = 1.0 - f
    g = jnp.log(f)
    G = _chunk_cumsum(g, row)
    g_last = G + (_chunk_cumsum(g, row, reverse=True) - g)
    e_g, e_ng, e_lg = jnp.exp(G), jnp.exp(-G), jnp.exp(g_last - G)
    return dict(sig=sig, f=f, kk=kk, e_g=e_g, e_ng=e_ng, e_lg=e_lg, q_dec=q * e_g, k_intra=kk * e_ng,
                k_state=kk * e_lg, decay=jnp.exp(g_last))


def _hg_causal(n):
    t = lax.broadcasted_iota(jnp.int32, (n, n), 0)
    s = lax.broadcasted_iota(jnp.int32, (n, n), 1)
    return (s <= t) & (s // HG_CHUNK == t // HG_CHUNK)


def _chunks(a):
    return a.reshape(a.shape[0] // HG_CHUNK, HG_CHUNK, a.shape[1])


def _per_chunk(lhs, rhs, contract):
    return lax.dot_general(_mx(lhs), _mx(rhs), ((contract[:1], contract[1:]), ((0,), (0,))), preferred_element_type=F32)


def _hg_specs(T, tb, col0, order):
    return [pl.BlockSpec((tb, HG_WIDTH), functools.partial(lambda i, j: (order(i), j), j=col0 + j)) for j in range(4)]


def _hg_fwd(proj, logits, norm_g, *, name):
    T = proj.shape[0]
    tb = min(HG_TOKENS, T)
    nch = tb // HG_CHUNK

    def body(q_ref, f_ref, i_ref, gate_ref, lg_ref, ng_ref, out_ref, o_ref, s_ref, st_ref, inc_ref, dec_ref):
        @pl.when(pl.program_id(0) == 0)
        def _():
            st_ref[...] = jnp.zeros_like(st_ref)

        lb_all = _hg_lower_bound(lg_ref)
        row = _chunk_row(tb)
        causal = _hg_causal(tb)
        for hh in range(HG_HEADS):
            cols = slice(hh * HG_DH, (hh + 1) * HG_DH)
            t = _hg_terms(f_ref[:, cols], q_ref[:, cols], lb_all[:, cols], row)
            v = i_ref[:, cols]
            scores = jnp.where(causal, _dot_nt(t["q_dec"], t["k_intra"]), 0.0)
            o_intra = _dot(scores, v)
            inc_ref[...] = _per_chunk(_chunks(v), _chunks(t["k_state"]), (1, 1))
            dec_ref[...] = _chunks(t["decay"])

            def step(ci, st):
                s_ref[ci, hh] = st
                return st * dec_ref[ci][0:1, :] + inc_ref[ci]

            st_ref[hh] = lax.fori_loop(0, nch, step, st_ref[hh], unroll=4)
            o_inter = _per_chunk(_chunks(t["q_dec"]), s_ref[:, hh], (2, 2))
            o = o_intra + o_inter.reshape(tb, HG_DH)
            o_ref[:, cols] = o
            gate = gate_ref[:, cols]
            on = o * lax.rsqrt(jnp.mean(o * o, axis=-1, keepdims=True) + RMS_EPS) * ng_ref[:, cols]
            out_ref[:, cols] = (on * (gate * jax.nn.sigmoid(gate))).astype(out_ref.dtype)

    blk = pl.BlockSpec((tb, HG_WIDTH), lambda i: (i, 0))
    return pl.pallas_call(
        body, grid=(T // tb,),
        in_specs=_hg_specs(T, tb, 3, lambda i: i) + [pl.BlockSpec((3, HG_WIDTH), lambda i: (0, 0)), pl.BlockSpec((1, HG_WIDTH), lambda i: (0, 0))],
        out_specs=[blk, blk, pl.BlockSpec((nch, HG_HEADS, HG_DH, HG_DH), lambda i: (i, 0, 0, 0))],
        out_shape=[jax.ShapeDtypeStruct((T, HG_WIDTH), MXU_DTYPE), jax.ShapeDtypeStruct((T, HG_WIDTH), F32),
                   jax.ShapeDtypeStruct((T // HG_CHUNK, HG_HEADS, HG_DH, HG_DH), F32)],
        scratch_shapes=[pltpu.VMEM((HG_HEADS, HG_DH, HG_DH), F32), pltpu.VMEM((nch, HG_DH, HG_DH), F32),
                        pltpu.VMEM((nch, HG_CHUNK, HG_DH), F32)],
        compiler_params=_params("arbitrary"), name=name,
    )(proj, proj, proj, proj, logits, norm_g)


def _hg_bwd(proj, o_raw, states, dmix, logits, norm_g, *, name):
    T = proj.shape[0]
    tb = min(HG_TOKENS, T)
    nch = tb // HG_CHUNK
    nb = T // tb
    rev = lambda i: nb - 1 - i

    def body(q_ref, f_ref, i_ref, gate_ref, o_ref, s_ref, dout_ref, lg_ref, ng_ref, dp_ref, dlb_ref, dng_ref,
             dst_ref, inc_ref, dec_ref, after_ref):
        @pl.when(pl.program_id(0) == 0)
        def _():
            dst_ref[...] = jnp.zeros_like(dst_ref)
            dlb_ref[...] = jnp.zeros_like(dlb_ref)
            dng_ref[...] = jnp.zeros_like(dng_ref)

        lb_all = _hg_lower_bound(lg_ref)
        row = _chunk_row(tb)
        causal = _hg_causal(tb)
        for hh in range(HG_HEADS):
            cols = slice(hh * HG_DH, (hh + 1) * HG_DH)
            out_cols = lambda part: slice(part * HG_WIDTH + hh * HG_DH, part * HG_WIDTH + (hh + 1) * HG_DH)
            o, gate, dout, ng, lb = o_ref[:, cols], gate_ref[:, cols], dout_ref[:, cols], ng_ref[:, cols], lb_all[:, cols]
            sg = jax.nn.sigmoid(gate)
            r = lax.rsqrt(jnp.mean(o * o, axis=-1, keepdims=True) + RMS_EPS)
            oh = o * r
            dp_ref[:, out_cols(3)] = dout * (oh * ng) * (sg * (1.0 + gate * (1.0 - sg)))
            don = dout * (gate * sg)
            dng_ref[:, cols] += jnp.sum(don * oh, axis=0, keepdims=True)
            doh = don * ng
            do = r * (doh - oh * jnp.mean(doh * oh, axis=-1, keepdims=True))

            t = _hg_terms(f_ref[:, cols], q_ref[:, cols], lb, row)
            v = i_ref[:, cols]
            scores = jnp.where(causal, _dot_nt(t["q_dec"], t["k_intra"]), 0.0)
            dscores = jnp.where(causal, _dot_nt(do, v), 0.0)
            inc_ref[...] = _per_chunk(_chunks(do), _chunks(t["q_dec"]), (1, 1))
            dec_ref[...] = _chunks(t["decay"])

            def step(it, dst):
                ci = nch - 1 - it
                after_ref[ci] = dst
                return dst * dec_ref[ci][0:1, :] + inc_ref[ci]

            dst_ref[hh] = lax.fori_loop(0, nch, step, dst_ref[hh], unroll=4)
            st, dst = s_ref[:, hh], after_ref[...]
            dqd = _dot(dscores, t["k_intra"]) + _per_chunk(_chunks(do), st, (2, 1)).reshape(tb, HG_DH)
            dki = _dot_tn(dscores, t["q_dec"])
            dks = _per_chunk(_chunks(v), dst, (2, 1)).reshape(tb, HG_DH)
            dp_ref[:, out_cols(2)] = _dot_tn(scores, do) + _per_chunk(_chunks(t["k_state"]), dst, (2, 2)).reshape(tb, HG_DH)
            ddecay = jnp.broadcast_to(jnp.sum(st * dst, axis=1, keepdims=True), (nch, HG_CHUNK, HG_DH)).reshape(tb, HG_DH)
            dks_ks = dks * t["k_state"]
            d_glast = _chunk_cumsum(dks_ks, row) + ddecay * t["decay"]
            d_g = dqd * t["q_dec"] - dki * t["k_intra"] - dks_ks + jnp.where(row == HG_CHUNK - 1, d_glast, 0.0)
            df = _chunk_cumsum(d_g, row, reverse=True) / t["f"] - (dki * t["e_ng"] + dks * t["e_lg"])
            dp_ref[:, out_cols(0)] = dqd * t["e_g"]
            dp_ref[:, out_cols(1)] = df * (1.0 - lb) * t["sig"] * (1.0 - t["sig"])
            dlb_ref[:, cols] += jnp.sum(df * (1.0 - t["sig"]), axis=0, keepdims=True)

    blk = pl.BlockSpec((tb, HG_WIDTH), lambda i: (rev(i), 0))
    row_spec = pl.BlockSpec((1, HG_WIDTH), lambda i: (0, 0))
    return pl.pallas_call(
        body, grid=(nb,),
        in_specs=_hg_specs(T, tb, 3, rev) + [
            blk, pl.BlockSpec((nch, HG_HEADS, HG_DH, HG_DH), lambda i: (rev(i), 0, 0, 0)),
            pl.BlockSpec((tb, HG_WIDTH), lambda i: (rev(i), 1)), pl.BlockSpec((3, HG_WIDTH), lambda i: (0, 0)), row_spec],
        out_specs=[pl.BlockSpec((tb, 4 * HG_WIDTH), lambda i: (rev(i), 0)), row_spec, row_spec],
        out_shape=[jax.ShapeDtypeStruct((T, 4 * HG_WIDTH), F32), jax.ShapeDtypeStruct((1, HG_WIDTH), F32), jax.ShapeDtypeStruct((1, HG_WIDTH), F32)],
        scratch_shapes=[pltpu.VMEM((HG_HEADS, HG_DH, HG_DH), F32), pltpu.VMEM((nch, HG_DH, HG_DH), F32),
                        pltpu.VMEM((nch, HG_CHUNK, HG_DH), F32), pltpu.VMEM((nch, HG_DH, HG_DH), F32)],
        compiler_params=_params("arbitrary"), name=name,
    )(proj, proj, proj, proj, o_raw, states, dmix, logits, norm_g)


def _shifted_copies(sh_ref, n_rows):
    keep = n_rows + CONV_HALO - 8
    for b in range(1, 8):
        sh_ref[b, 0:keep, :] = sh_ref[0, b:b + keep, :]


def _tap_rows(sh_ref, offset, r0, lanes):
    start = pl.multiple_of(r0 + (offset - offset % 8), 8)
    return sh_ref[offset % 8, pl.ds(start, CONV_ROWS), lanes]


def _conv_fwd(p, w_dw, b_dw, ln_g, ln_b, *, name):
    T, D = p.shape[0], p.shape[1] // 2
    tb = _token_block(T)
    hpb = tb // CONV_HALO
    lane_step = 512

    def body(p1_ref, p2_ref, q1_ref, q2_ref, w_ref, bdw_ref, g_ref, b_ref, a_ref, y_ref, act_ref, sh_ref):
        i = pl.program_id(0)
        a = p1_ref[...] * jax.nn.sigmoid(p2_ref[...])
        sh_ref[0, 0:CONV_HALO, :] = jnp.where(i > 0, q1_ref[...] * jax.nn.sigmoid(q2_ref[...]), 0.0)
        sh_ref[0, CONV_HALO:, :] = a
        a_ref[...] = a
        _shifted_copies(sh_ref, tb)

        def chunk(ci, _):
            r0 = pl.multiple_of(ci * CONV_ROWS, CONV_ROWS)
            for l0 in range(0, D, lane_step):
                lanes = slice(l0, l0 + lane_step)
                acc = jnp.broadcast_to(bdw_ref[:, lanes], (CONV_ROWS, lane_step))
                for k in range(CONV_WIDTH):
                    acc = acc + _tap_rows(sh_ref, CONV_HALO - CONV_WIDTH + 1 + k, r0, lanes) * w_ref[k:k + 1, lanes]
                y_ref[pl.ds(r0, CONV_ROWS), lanes] = acc
            return 0

        lax.fori_loop(0, tb // CONV_ROWS, chunk, 0)
        y = y_ref[...]
        mu = jnp.mean(y, axis=-1, keepdims=True)
        yc = y - mu
        s = yc * lax.rsqrt(jnp.mean(yc * yc, axis=-1, keepdims=True) + LN_EPS) * g_ref[...] + b_ref[...]
        act_ref[...] = (s * jax.nn.sigmoid(s)).astype(act_ref.dtype)

    prev = lambda i: jnp.maximum(i * hpb - 1, 0)
    blk = pl.BlockSpec((tb, D), lambda i: (i, 0))
    row = pl.BlockSpec((1, D), lambda i: (0, 0))
    return pl.pallas_call(
        body, grid=(T // tb,),
        in_specs=[blk, pl.BlockSpec((tb, D), lambda i: (i, 1)), pl.BlockSpec((CONV_HALO, D), lambda i: (prev(i), 0)),
                  pl.BlockSpec((CONV_HALO, D), lambda i: (prev(i), 1)), pl.BlockSpec((CONV_HALO, D), lambda i: (0, 0)), row, row, row],
        out_specs=[blk, blk, blk],
        out_shape=[jax.ShapeDtypeStruct((T, D), F32), jax.ShapeDtypeStruct((T, D), F32), jax.ShapeDtypeStruct((T, D), MXU_DTYPE)],
        scratch_shapes=[pltpu.VMEM((8, tb + CONV_HALO, D), F32)],
        compiler_params=_params("parallel"), name=name,
    )(p, p, p, p, w_dw, b_dw, ln_g, ln_b)


def _conv_bwd_norm(dact, y, ln_g, ln_b, *, name):
    T, D = y.shape
    tb = _token_block(T)

    def body(da_ref, y_ref, g_ref, b_ref, dy_ref, dg_ref, db_ref, cs_ref):
        @pl.when(pl.program_id(0) == 0)
        def _():
            dg_ref[...] = jnp.zeros_like(dg_ref)
            db_ref[...] = jnp.zeros_like(db_ref)
            cs_ref[...] = jnp.zeros_like(cs_ref)

        y, g = y_ref[...], g_ref[...]
        yc = y - jnp.mean(y, axis=-1, keepdims=True)
        rs = lax.rsqrt(jnp.mean(yc * yc, axis=-1, keepdims=True) + LN_EPS)
        yn = yc * rs
        s = yn * g + b_ref[...]
        sg = jax.nn.sigmoid(s)
        ds = da_ref[...] * (sg * (1.0 + s * (1.0 - sg)))
        dg_ref[...] += jnp.sum(ds * yn, axis=0, keepdims=True)
        db_ref[...] += jnp.sum(ds, axis=0, keepdims=True)
        dyn = ds * g
        dy = rs * (dyn - jnp.mean(dyn, axis=-1, keepdims=True) - yn * jnp.mean(dyn * yn, axis=-1, keepdims=True))
        dy_ref[...] = dy
        cs_ref[...] += jnp.sum(dy, axis=0, keepdims=True)

    blk = pl.BlockSpec((tb, D), lambda i: (i, 0))
    row = pl.BlockSpec((1, D), lambda i: (0, 0))
    rs_ = jax.ShapeDtypeStruct((1, D), F32)
    return pl.pallas_call(
        body, grid=(T // tb,), in_specs=[blk, blk, row, row], out_specs=[blk, row, row, row],
        out_shape=[jax.ShapeDtypeStruct((T, D), F32), rs_, rs_, rs_], compiler_params=_params("arbitrary"), name=name,
    )(dact, y, ln_g, ln_b)


def _conv_bwd_taps(dy, a, p, w_dw, *, name):
    T, D = dy.shape
    tb = _token_block(T)
    hpb = tb // CONV_HALO
    last = T // CONV_HALO - 1
    nb = T // tb
    lane_step = 128
    groups = CONV_ROWS // 8

    def body(dy_ref, dyn_ref, a_ref, p1_ref, p2_ref, w_ref, dp_ref, dw_ref, cs_ref, sh_ref, da_ref):
        i = pl.program_id(0)

        @pl.when(i == 0)
        def _():
            dw_ref[...] = jnp.zeros_like(dw_ref)
            cs_ref[...] = jnp.zeros_like(cs_ref)

        sh_ref[0, 0:tb, :] = dy_ref[...]
        sh_ref[0, tb:, :] = jnp.where(i < nb - 1, dyn_ref[...], 0.0)
        _shifted_copies(sh_ref, tb)
        for l0 in range(0, D, lane_step):
            lanes = slice(l0, l0 + lane_step)

            def chunk(ci, sums):
                r0 = pl.multiple_of(ci * CONV_ROWS, CONV_ROWS)
                a_c = a_ref[pl.ds(r0, CONV_ROWS), lanes]
                da = jnp.zeros((CONV_ROWS, lane_step), F32)
                new = []
                for k in range(CONV_WIDTH):
                    s_k = _tap_rows(sh_ref, CONV_WIDTH - 1 - k, r0, lanes)
                    da = da + s_k * w_ref[k:k + 1, lanes]
                    new.append(sums[k] + jnp.sum((s_k * a_c).reshape(groups, 8, lane_step), axis=0))
                da_ref[pl.ds(r0, CONV_ROWS), lanes] = da
                return tuple(new)

            sums = lax.fori_loop(0, tb // CONV_ROWS, chunk, tuple(jnp.zeros((8, lane_step), F32) for _ in range(CONV_WIDTH)))
            for k in range(CONV_WIDTH):
                dw_ref[k:k + 1, lanes] += jnp.sum(sums[k], axis=0, keepdims=True)
        da = da_ref[...]
        p1 = p1_ref[...]
        sg = jax.nn.sigmoid(p2_ref[...])
        dp1 = da * sg
        dp2 = da * p1 * (sg * (1.0 - sg))
        dp_ref[:, 0:D] = dp1.astype(dp_ref.dtype)
        dp_ref[:, D:] = dp2.astype(dp_ref.dtype)
        cs_ref[:, 0:D] += jnp.sum(dp1, axis=0, keepdims=True)
        cs_ref[:, D:] += jnp.sum(dp2, axis=0, keepdims=True)

    blk = pl.BlockSpec((tb, D), lambda i: (i, 0))
    return pl.pallas_call(
        body, grid=(nb,),
        in_specs=[blk, pl.BlockSpec((CONV_HALO, D), lambda i: (jnp.minimum((i + 1) * hpb, last), 0)), blk, blk,
                  pl.BlockSpec((tb, D), lambda i: (i, 1)), pl.BlockSpec((CONV_HALO, D), lambda i: (0, 0))],
        out_specs=[pl.BlockSpec((tb, 2 * D), lambda i: (i, 0)), pl.BlockSpec((CONV_HALO, D), lambda i: (0, 0)), pl.BlockSpec((1, 2 * D), lambda i: (0, 0))],
        out_shape=[jax.ShapeDtypeStruct((T, 2 * D), MXU_DTYPE), jax.ShapeDtypeStruct((CONV_HALO, D), F32), jax.ShapeDtypeStruct((1, 2 * D), F32)],
        scratch_shapes=[pltpu.VMEM((8, tb + CONV_HALO, D), F32), pltpu.VMEM((tb, D), F32)],
        compiler_params=_params("arbitrary"), name=name,
    )(dy, dy, a, p, p, w_dw)


def _row_block(rows):
    for tr in (512, 256, 128, 64, 32, 16, 8):
        if rows % tr == 0:
            return tr
    return rows


def _sum_leading(x, *, name):
    n, R, C = x.shape
    tr = _row_block(R)

    def body(x_ref, o_ref):
        acc = x_ref[0]
        for j in range(1, n):
            acc = acc + x_ref[j]
        o_ref[...] = acc

    return pl.pallas_call(
        body, grid=(R // tr,), in_specs=[pl.BlockSpec((n, tr, C), lambda i: (0, i, 0))], out_specs=pl.BlockSpec((tr, C), lambda i: (i, 0)),
        out_shape=jax.ShapeDtypeStruct((R, C), x.dtype), compiler_params=_params("parallel"), name=name,
    )(x)


def _add_pair(x, y, *, name):
    n, R, C = x.shape
    tr = _row_block(R)

    def body(x_ref, y_ref, o_ref):
        o_ref[...] = x_ref[...] + y_ref[...]

    blk = pl.BlockSpec((1, tr, C), lambda j, i: (j, i, 0))
    return pl.pallas_call(
        body, grid=(n, R // tr), in_specs=[blk, blk], out_specs=blk,
        out_shape=jax.ShapeDtypeStruct((n, R, C), x.dtype), compiler_params=_params("parallel", "parallel"), name=name,
    )(x, y)


def _adamw(w, g, m, v, *, name):
    R, C = w.shape
    tr = _row_block(R)
    c1, c2 = 1.0 - ADAM_B1 ** ADAM_STEP, 1.0 - ADAM_B2 ** ADAM_STEP

    def body(w_ref, g_ref, m_ref, v_ref, d_ref, nm_ref, nv_ref):
        g_ = g_ref[...]
        nm = ADAM_B1 * m_ref[...] + (1.0 - ADAM_B1) * g_
        nv = ADAM_B2 * v_ref[...] + (1.0 - ADAM_B2) * (g_ * g_)
        d_ref[...] = -ADAM_LR * ((nm / c1) / (jnp.sqrt(nv / c2) + ADAM_EPS) + ADAM_WD * w_ref[...])
        nm_ref[...] = nm
        nv_ref[...] = nv

    blk = pl.BlockSpec((tr, C), lambda i: (i, 0))
    shp = jax.ShapeDtypeStruct((R, C), F32)
    return pl.pallas_call(
        body, grid=(R // tr,), in_specs=[blk] * 4, out_specs=[blk] * 3, out_shape=[shp] * 3,
        compiler_params=_params("parallel"), name=name,
    )(w, g, m, v)


def _small_reduce(packs, logits, *, name):
    n, R, C = packs.shape

    def body(p_ref, lg_ref, s_ref, dlg_ref):
        acc = p_ref[0]
        for j in range(1, n):
            acc = acc + p_ref[j]
        s_ref[...] = acc
        lg = lg_ref[...]
        e = jnp.exp(lg - jnp.max(lg, axis=0, keepdims=True))
        sm = e / jnp.sum(e, axis=0, keepdims=True)
        dlb = acc[5:6, HG_WIDTH:2 * HG_WIDTH]
        first = lax.broadcasted_iota(jnp.int32, sm.shape, 0) == 0
        dlg_ref[...] = sm[0:1, :] * (jnp.where(first, 1.0, 0.0) - sm) * dlb

    whole = lambda shape: pl.BlockSpec(shape, lambda: (0,) * len(shape))
    return pl.pallas_call(
        body, in_specs=[whole((n, R, C)), whole(logits.shape)], out_specs=[whole((R, C)), whole(logits.shape)],
        out_shape=[jax.ShapeDtypeStruct((R, C), F32), jax.ShapeDtypeStruct(logits.shape, F32)],
        compiler_params=pltpu.CompilerParams(vmem_limit_bytes=VMEM_LIMIT), name=name,
    )(packs, logits)


HBM_SPEC = pl.BlockSpec(memory_space=pl.ANY)


def _place():
    return lax.axis_index("x"), lax.axis_index("y"), lax.axis_index("c")


class _Copies:
    def __init__(self, arrays, out_shapes, n_copies, make, finish):
        self.arrays, self.out_shapes, self.n_copies, self.make, self.finish = list(arrays), list(out_shapes), n_copies, make, finish

    def scratch(self):
        return [pltpu.SemaphoreType.DMA((self.n_copies,)), pltpu.SemaphoreType.DMA((self.n_copies,))]

    def run(self, name):
        n = len(self.arrays)

        def body(*refs):
            copies = self.make(refs[:n], refs[n:2 * n], *refs[2 * n:])
            for cp in copies:
                cp.start()
            for cp in copies:
                cp.wait()

        outs = pl.pallas_call(body, in_specs=[HBM_SPEC] * n, out_specs=[HBM_SPEC] * n, out_shape=self.out_shapes,
                              scratch_shapes=self.scratch(), name=name)(*self.arrays)
        return self.finish(outs)


def _remote(src, dst, send_sems, recv_sems, k, peer):
    return pltpu.make_async_remote_copy(src_ref=src, dst_ref=dst, send_sem=send_sems.at[k], recv_sem=recv_sems.at[k],
                                        device_id=peer, device_id_type=MESH)


def _same_core_peers(x, y, c):
    return [(1 - x, y, c), (x, 1 - y, c), (1 - x, 1 - y, c)]


def _all_peers(x, y, c):
    flip = lambda v, b: 1 - v if b else v
    return [(flip(x, r & 4), flip(y, r & 2), flip(c, r & 1)) for r in range(1, 8)]


def _gather(arrays, peers_of, slot_of, n_slots):
    n_peers = len(peers_of(0, 0, 0))

    def make(ins, outs, send_sems, recv_sems):
        x, y, c = _place()
        slot = slot_of(x, y, c)
        return [_remote(ins[a], outs[a].at[slot], send_sems, recv_sems, a * n_peers + k, peer)
                for a in range(len(arrays)) for k, peer in enumerate(peers_of(x, y, c))]

    def finish(outs):
        slot = slot_of(*_place())
        return [lax.dynamic_update_index_in_dim(o, a, slot, 0) for o, a in zip(outs, arrays)]

    shapes = [jax.ShapeDtypeStruct((n_slots,) + a.shape, a.dtype) for a in arrays]
    return _Copies(arrays, shapes, len(arrays) * n_peers, make, finish)


def _gather_chips(arrays):
    return _gather(arrays, _same_core_peers, lambda x, y, c: 2 * x + y, N_CHIPS)


def _gather_all(arrays):
    return _gather(arrays, _all_peers, lambda x, y, c: 4 * x + 2 * y + c, N_DEV)


def _pair_swap(a):
    def make(ins, outs, send_sems, recv_sems):
        x, y, c = _place()
        return [_remote(ins[0].at[1 - c], outs[0], send_sems, recv_sems, 0, (x, y, 1 - c))]

    return _Copies([a], [jax.ShapeDtypeStruct(a.shape[1:], a.dtype)], 1, make, lambda outs: outs[0])


def _chip_scatter(p):
    def make(ins, outs, send_sems, recv_sems):
        x, y, c = _place()
        return [_remote(ins[0].at[2 * px + py], outs[0].at[2 * x + y], send_sems, recv_sems, k, (px, py, pc))
                for k, (px, py, pc) in enumerate(_same_core_peers(x, y, c))]

    def finish(outs):
        x, y, _ = _place()
        me = 2 * x + y
        return lax.dynamic_update_index_in_dim(outs[0], lax.dynamic_index_in_dim(p, me, 0, keepdims=False), me, 0)

    return _Copies([p], [jax.ShapeDtypeStruct(p.shape, p.dtype)], 3, make, finish)


def _owner_scatter(blocks):
    def make(ins, outs, send_sems, recv_sems):
        x, y, c = _place()
        return [_remote(ins[0].at[2 * px + py, pc], outs[0].at[4 * x + 2 * y + c], send_sems, recv_sems, k, (px, py, pc))
                for k, (px, py, pc) in enumerate(_all_peers(x, y, c))]

    def finish(outs):
        x, y, c = _place()
        mine = lax.dynamic_index_in_dim(lax.dynamic_index_in_dim(blocks, 2 * x + y, 0, keepdims=False), c, 0, keepdims=False)
        return lax.dynamic_update_index_in_dim(outs[0], mine, 4 * x + 2 * y + c, 0)

    return _Copies([blocks], [jax.ShapeDtypeStruct((N_DEV,) + blocks.shape[2:], blocks.dtype)], N_DEV - 1, make, finish)


def _pair_gather(q):
    def make(ins, outs, send_sems, recv_sems):
        x, y, c = _place()
        return [_remote(ins[0], outs[0].at[c], send_sems, recv_sems, 0, (x, y, 1 - c))]

    return _Copies([q], [jax.ShapeDtypeStruct((2,) + q.shape, q.dtype)], 1, make,
                   lambda outs: lax.dynamic_update_index_in_dim(outs[0], q, _place()[2], 0))


def _grad_blocks(dw, kind):
    if kind == "cols2d":
        K, N = dw.shape
        b = dw.reshape(2, K // 2, N_CHIPS, N // N_CHIPS).transpose(2, 0, 1, 3)
    elif kind == "rows2d":
        b = dw.reshape(N_CHIPS, 2, dw.shape[0] // 8, dw.shape[1])
    elif kind == "cols3d":
        L, K, N = dw.shape
        b = dw.reshape(L, K, N_CHIPS, N // N_CHIPS).transpose(2, 0, 1, 3)
    else:
        L, K, N = dw.shape
        b = dw.reshape(L, N_CHIPS, K // N_CHIPS, N).transpose(1, 0, 2, 3)
    return b.reshape(N_CHIPS, 2, -1, D_MODEL)


def _pad_rows(a, rows):
    return jnp.concatenate([a, jnp.zeros((rows - a.shape[0],) + a.shape[1:], a.dtype)], axis=0)


def _forward_backward(x, target, W, late_weights=None, early_grads=None, last_grads=None):
    row = lambda a: a.reshape(1, -1)
    relu2 = lambda acc: (jnp.square(jnp.maximum(acc, 0.0)),)
    normed = lambda h, g: h * lax.rsqrt(jnp.mean(h * h, axis=-1, keepdims=True) + RMS_EPS) * g

    def residual_norm(acc, res, g):
        h = res + acc
        return h, normed(h, g)

    def norm_bwd(du, h_blk, dres_blk, g):
        dx, dg_terms = _rms_bwd_math(h_blk, g, du)
        dh = dres_blk + dx
        return dh, dh, jnp.sum(dg_terms, axis=0, keepdims=True), jnp.sum(dh, axis=0, keepdims=True)

    def matmul_norm_bwd(dy, w, h_in, g, dres, rider=None, *, tk=1024, name):
        return _matmul(dy, w, mode="nt", out_dtypes=[F32, MXU_DTYPE], epilogue=norm_bwd, tiles=[h_in, dres], rows=[row(g)],
                       n_sums=2, rider=rider, tk=tk, name=name)

    G = {}

    u0 = _rmsnorm_fwd(x, row(W["norm_mix_g"][0]), name="norm_mix0")
    proj, qkv = _matmul(u0, W["w_in"], mode="nn", out_dtypes=[F32, MXU_DTYPE], epilogue=lambda acc: (acc, acc), tn=896, name="in_proj")
    o_sb, got = _sb_fwd(qkv, late_weights and late_weights[0], name="sb_fwd")
    if late_weights:
        W = {**W, **late_weights[1](got)}
    hg_out, hg_o, hg_states = _hg_fwd(proj, W["hg_lb_logits"], row(W["hg_norm_g"]), name="hg_fwd")
    mix = jnp.concatenate([o_sb, hg_out], axis=-1)
    h1, u1 = _matmul(mix, W["w_out"], mode="nn", out_dtypes=[F32, MXU_DTYPE], epilogue=residual_norm, tiles=[x],
                     rows=[row(W["norm_ffn_g"][0])], name="out_proj")
    r0 = _matmul(u1, W["w_ff1"][0], mode="nn", out_dtypes=[MXU_DTYPE], epilogue=relu2, name="ff1_0")
    h2, u2 = _matmul(r0, W["w_ff2"][0], mode="nn", out_dtypes=[F32, MXU_DTYPE], epilogue=residual_norm, tiles=[h1],
                     rows=[row(W["norm_mix_g"][1])], name="ff2_0")
    p = _matmul(u2, W["w_glu"], mode="nn", out_dtypes=[F32], epilogue=lambda acc, b: (acc + b,), rows=[row(W["b_glu"])], name="glu_proj")
    w_dw = _pad_rows(W["w_dw"], CONV_HALO)
    ca, cy, cact = _conv_fwd(p, w_dw, row(W["b_dw"]), row(W["ln_g"]), row(W["ln_b"]), name="conv_fwd")
    h3, u3 = _matmul(cact, W["w_pw"], mode="nn", out_dtypes=[F32, MXU_DTYPE], epilogue=lambda acc, res, b, g: residual_norm(acc + b, res, g),
                     tiles=[h2], rows=[row(W["b_pw"]), row(W["norm_ffn_g"][1])], name="pw_proj")
    r1 = _matmul(u3, W["w_ff1"][1], mode="nn", out_dtypes=[MXU_DTYPE], epilogue=relu2, name="ff1_1")
    h4 = _matmul(r1, W["w_ff2"][1], mode="nn", out_dtypes=[F32], epilogue=lambda acc, res: (res + acc,), tiles=[h3], name="ff2_1")

    dh4, dh4_m, G["final_norm_g"], loss = _loss_head(h4, row(W["final_norm_g"]), target, name="loss_head")

    def mlp_bwd(dh, dh_m, h_in, u, r, layer, tag):
        d_relu2 = lambda acc, r_blk: (acc * (2.0 * jnp.sqrt(r_blk.astype(F32))),)
        da = _matmul(dh_m, W["w_ff2"][layer], mode="nt", out_dtypes=[MXU_DTYPE], epilogue=d_relu2, tiles=[r], name="d_ff2_act" + tag)
        dw2 = _matmul(r, dh_m, mode="tn", out_dtypes=[F32], name="d_ff2_w" + tag)
        dw1 = _matmul(u, da, mode="tn", out_dtypes=[F32], name="d_ff1_w" + tag)
        dh_in, dh_in_m, dg, cs = matmul_norm_bwd(da, W["w_ff1"][layer], h_in, W["norm_ffn_g"][layer], dh, name="d_ff1_act" + tag)
        return dh_in, dh_in_m, dg, cs, dw1, dw2

    dh3, dh3_m, dg_ffn1, cs_h3, dw1_1, dw2_1 = mlp_bwd(dh4, dh4_m, h3, u3, r1, 1, "1")
    G["b_pw"] = cs_h3
    dact = _matmul(dh3_m, W["w_pw"], mode="nt", out_dtypes=[F32], name="d_pw_act")
    G["w_pw"] = _matmul(cact, dh3_m, mode="tn", out_dtypes=[F32], name="d_pw_w")
    dy, G["ln_g"], G["ln_b"], G["b_dw"] = _conv_bwd_norm(dact, cy, row(W["ln_g"]), row(W["ln_b"]), name="d_conv_norm")
    dp, G["w_dw"], G["b_glu"] = _conv_bwd_taps(dy, ca, p, w_dw, name="d_conv_taps")
    G["w_glu"] = _matmul(u2, dp, mode="tn", out_dtypes=[F32], name="d_glu_w")
    dh2, dh2_m, dg_mix1, _ = matmul_norm_bwd(dp, W["w_glu"], h2, W["norm_mix_g"][1], dh3, name="d_glu_act")
    dh1, dh1_m, dg_ffn0, _, dw1_0, dw2_0 = mlp_bwd(dh2, dh2_m, h1, u1, r0, 0, "0")
    G["w_ff1"], G["w_ff2"] = jnp.stack([dw1_0, dw1_1]), jnp.stack([dw2_0, dw2_1])
    G["norm_ffn_g"] = jnp.concatenate([dg_ffn0, dg_ffn1], axis=0)
    dmix = _matmul(dh1_m, W["w_out"], mode="nt", out_dtypes=[F32], name="d_out_act")
    G["w_out"] = _matmul(mix, dh1_m, mode="tn", out_dtypes=[F32], name="d_out_w")
    riding = early_grads(G) if early_grads else None
    (dsq, dsk, dsv), got = _sb_bwd(qkv, dmix, riding, name="sb_bwd")
    d_hg, G["hg_lb"], G["hg_norm_g"] = _hg_bwd(proj, hg_o, hg_states, dmix, W["hg_lb_logits"], row(W["hg_norm_g"]), name="hg_bwd")
    dproj = jnp.concatenate([dsq, dsk, dsv, d_hg], axis=-1).astype(MXU_DTYPE)
    G["w_in"] = _matmul(u0, dproj, mode="tn", out_dtypes=[F32], tn=896, name="d_in_w")
    last = last_grads(G) if last_grads else None
    res = matmul_norm_bwd(dproj, W["w_in"], x, W["norm_mix_g"][0], dh1, last, tk=896, name="d_in_act")
    (dx, _, dg_mix0, _), got_last = res if last_grads else (res, [])
    G["norm_mix_g"] = jnp.concatenate([dg_mix0, dg_mix1], axis=0)
    return loss, dx, G, [(riding, got), (last, got_last)]


BIG = (("w_out_ab", "w_out", "rows2d"), ("conv_w_glu", "w_glu", "cols2d"), ("conv_w_pw", "w_pw", "rows2d"),
       ("w_ff1", "w_ff1", "cols3d"), ("w_ff2", "w_ff2", "rows3d"), ("w_in_ab", "w_in", "cols2d"))
LATE = BIG[:-1]
SMALL_SHARDED = ("conv_b_glu", "conv_w_dw", "conv_b_dw", "conv_ln_g", "conv_ln_b", "conv_b_pw")
REPLICATED = ("norm_mix_g", "norm_ffn_g", "hg_lb_logits", "hg_norm_g", "final_norm_g")
ORDER = ("norm_mix_g", "norm_ffn_g", "w_in_ab", "w_out_ab", "hg_lb_logits", "hg_norm_g", "conv_w_glu", "conv_b_glu",
         "conv_w_dw", "conv_b_dw", "conv_ln_g", "conv_ln_b", "conv_w_pw", "conv_b_pw", "w_ff1", "w_ff2", "final_norm_g")


def _step(x, loss_target, w, m, v):
    D = D_MODEL
    x2, t2 = x.reshape(-1, D), loss_target.reshape(-1, D)
    chip = 2 * lax.axis_index("x") + lax.axis_index("y")
    c = lax.axis_index("c")

    small_in = jnp.concatenate([w["conv_b_glu"].reshape(2, SHARD), w["conv_w_dw"].reshape(CONV_WIDTH, SHARD)] +
                               [w[n].reshape(1, SHARD) for n in ("conv_b_dw", "conv_ln_g", "conv_ln_b", "conv_b_pw")], axis=0)
    g_in, gs = _gather_chips([w["w_in_ab"].astype(MXU_DTYPE), _pad_rows(small_in, SMALL_IN_ROWS)]).run("gather_first_weights")
    vec = lambda r0, r1: gs[:, r0:r1].transpose(1, 0, 2).reshape(r1 - r0, D)
    W = {
        "w_in": _ChipWeight(g_in[:, 0], "cols"),
        "b_glu": gs[:, 0:2].reshape(2 * D), "w_dw": vec(2, 33), "b_dw": vec(33, 34)[0], "ln_g": vec(34, 35)[0],
        "ln_b": vec(35, 36)[0], "b_pw": vec(36, 37)[0],
        "norm_mix_g": w["norm_mix_g"], "norm_ffn_g": w["norm_ffn_g"], "hg_lb_logits": w["hg_lb_logits"],
        "hg_norm_g": w["hg_norm_g"], "final_norm_g": w["final_norm_g"],
    }
    late = _gather_chips([w[n].astype(MXU_DTYPE) for n, _, _ in LATE])

    def assemble(got):
        gw = dict(zip([s for _, s, _ in LATE], late.finish(got)))
        layers = lambda g, along: [_ChipWeight(g, along, (layer,)) for layer in range(2)]
        return {"w_out": gw["w_out"].reshape(D, D), "w_glu": _ChipWeight(gw["w_glu"][:, 0], "cols"), "w_pw": gw["w_pw"].reshape(D, D),
                "w_ff1": layers(gw["w_ff1"], "cols"), "w_ff2": layers(gw["w_ff2"], "rows")}

    def early_grads(G):
        return _owner_scatter(jnp.concatenate([_grad_blocks(G[s], kind) for _, s, kind in LATE], axis=2))

    def last_grads(G):
        blocks = _grad_blocks(G["w_in"], "cols2d")
        from_pair = _pair_swap(blocks.transpose(1, 0, 2, 3)).run("grads_pair_swap_in")
        return _chip_scatter(_add_pair(lax.dynamic_index_in_dim(blocks, c, axis=1, keepdims=False), from_pair, name="grads_pair_add_in"))

    loss, dx, G, riders = _forward_backward(x2, t2, W, (late, assemble), early_grads, last_grads)
    halves = [_sum_leading(copies.finish(got), name="grads_add_" + tag) for (copies, got), tag in zip(riders, ("late", "in"))]
    half = jnp.concatenate(halves, axis=0)
    full = _pair_gather(half).run("grads_pair_gather")

    pack = jnp.concatenate([
        G["norm_mix_g"], G["norm_ffn_g"], G["final_norm_g"], jnp.concatenate([G["hg_norm_g"], G["hg_lb"]], axis=1),
        _pad_rows(jnp.broadcast_to(loss, (1, D)), 2), G["b_glu"].reshape(2, D), G["w_dw"], G["b_dw"], G["ln_g"], G["ln_b"], G["b_pw"],
    ], axis=0)
    pack = _pad_rows(pack, SMALL_ROWS)
    (packs,) = _gather_all([pack]).run("gather_small_grads")
    ssum, d_logits = _small_reduce(packs, w["hg_lb_logits"], name="reduce_small_grads")
    cut = lambda r0, r1: lax.dynamic_slice(ssum, (r0, chip * SHARD), (r1 - r0, SHARD))
    grads = {
        "norm_mix_g": ssum[0:2], "norm_ffn_g": ssum[2:4], "final_norm_g": ssum[4], "hg_norm_g": ssum[5, :HG_WIDTH].reshape(1, HG_HEADS, HG_DH),
        "hg_lb_logits": d_logits,
        "conv_b_glu": lax.dynamic_slice(ssum[8:10].reshape(1, 2 * D), (0, chip * 2 * SHARD), (1, 2 * SHARD)),
        "conv_w_dw": cut(10, 10 + CONV_WIDTH).reshape(1, CONV_WIDTH, SHARD),
        "conv_b_dw": cut(42, 43), "conv_ln_g": cut(43, 44), "conv_ln_b": cut(44, 45), "conv_b_pw": cut(45, 46),
    }
    loss_out = ssum[6, 0]

    off = 0
    for n, s, kind in BIG:
        shard = w[n].shape
        rows = w[n].size // (2 * D)
        grads[n] = full[:, off:off + rows].reshape(shard)
        off += rows

    delta, new_m, new_v = {}, {}, {}
    for n, _, _ in BIG:
        view = lambda a: a.reshape(-1, a.shape[-1])
        outs = _adamw(view(w[n]), view(grads[n]), view(m[n]), view(v[n]), name="adamw_" + n)
        delta[n], new_m[n], new_v[n] = (o.reshape(w[n].shape) for o in outs)
    small = SMALL_SHARDED + REPLICATED
    sizes = [w[n].size for n in small]
    total = sum(sizes)
    rows = -(-total // (8 * D)) * 8
    packed = lambda d: _pad_rows(jnp.concatenate([d[n].reshape(-1) for n in small]).reshape(-1, 128), rows * 8).reshape(rows, D)
    outs = _adamw(packed(w), packed(grads), packed(m), packed(v), name="adamw_small")
    off = 0
    for n, size in zip(small, sizes):
        delta[n], new_m[n], new_v[n] = (o.reshape(-1)[off:off + size].reshape(w[n].shape) for o in outs)
        off += size
    grads = {n: grads[n].reshape(w[n].shape) for n in ORDER}
    return (loss_out, dx.reshape(x.shape), *[grads[n] for n in ORDER], *[delta[n] for n in ORDER],
            *[new_m[n] for n in ORDER], *[new_v[n] for n in ORDER])


def kernel(x, norm_mix_g, norm_ffn_g, w_in_ab, w_out_ab, hg_lb_logits, hg_norm_g, conv_w_glu, conv_b_glu, conv_w_dw, conv_b_dw, conv_ln_g, conv_ln_b, conv_w_pw, conv_b_pw, w_ff1, w_ff2, final_norm_g, loss_target, m_norm_mix_g, m_norm_ffn_g, m_w_in_ab, m_w_out_ab, m_hg_lb_logits, m_hg_norm_g, m_conv_w_glu, m_conv_b_glu, m_conv_w_dw, m_conv_b_dw, m_conv_ln_g, m_conv_ln_b, m_conv_w_pw, m_conv_b_pw, m_w_ff1, m_w_ff2, m_final_norm_g, v_norm_mix_g, v_norm_ffn_g, v_w_in_ab, v_w_out_ab, v_hg_lb_logits, v_hg_norm_g, v_conv_w_glu, v_conv_b_glu, v_conv_w_dw, v_conv_b_dw, v_conv_ln_g, v_conv_ln_b, v_conv_w_pw, v_conv_b_pw, v_w_ff1, v_w_ff2, v_final_norm_g):
    args = locals()
    w = {n: args[n] for n in ORDER}
    m = {n: args["m_" + n] for n in ORDER}
    v = {n: args["v_" + n] for n in ORDER}
    return _step(x, loss_target, w, m, v)
```

```python
import functools

import jax
import jax.numpy as jnp
from jax import lax
from jax.experimental import pallas as pl
from jax.experimental.pallas import tpu as pltpu

F32 = jnp.float32
MXU_DTYPE = jnp.bfloat16
MESH = pl.DeviceIdType.MESH

D_MODEL = 1024
SB_HEADS, SB_DH, SB_WIDTH = 8, 64, 512
SB_KEYS = 512
SB_SUB = 256
SB_ROWS_FWD, SB_ROWS_BWD = 512, 256
HG_HEADS, HG_DH, HG_WIDTH = 4, 128, 512
HG_CHUNK = 16
HG_TOKENS = 256
CONV_WIDTH = 31
CONV_HALO = 32
CONV_ROWS = 32
RMS_EPS = 1e-6
LN_EPS = 1e-5
N_CHIPS = 4
N_DEV = 8
SHARD = D_MODEL // N_CHIPS
SMALL_IN_ROWS = 40
SMALL_ROWS = 48
VMEM_LIMIT = 56 * 1024 * 1024

ADAM_LR, ADAM_B1, ADAM_B2, ADAM_EPS, ADAM_WD, ADAM_STEP = 0.001, 0.9, 0.999, 1e-08, 0.01, 10


def _params(*sem):
    return pltpu.CompilerParams(dimension_semantics=sem, vmem_limit_bytes=VMEM_LIMIT)


def _mx(v):
    return v.astype(MXU_DTYPE)


def _dot(a, b):
    return jnp.dot(_mx(a), _mx(b), preferred_element_type=F32)


def _dot_nt(a, b):
    return lax.dot_general(_mx(a), _mx(b), (((1,), (1,)), ((), ())), preferred_element_type=F32)


def _dot_tn(a, b):
    return lax.dot_general(_mx(a), _mx(b), (((0,), (0,)), ((), ())), preferred_element_type=F32)


def _neg_abs(x):
    bits = lax.bitcast_convert_type(x, jnp.uint32) | jnp.uint32(0x80000000)
    return lax.bitcast_convert_type(bits, F32)


def _key_order_sums(v, tri2, later):
    hi = _mx(v)
    lo = _mx(v - hi.astype(F32))
    n = v.shape[1] // SB_SUB
    sums, far = [None] * n, None
    for b in (reversed(range(n)) if later else range(n)):
        sl = slice(b * SB_SUB, (b + 1) * SB_SUB)
        inside = jnp.dot(jnp.concatenate([hi[:, sl], lo[:, sl]], axis=1), tri2, preferred_element_type=F32)
        sums[b] = inside if far is None else inside + far
        total = jnp.sum(v[:, sl], axis=1, keepdims=True)
        far = total if far is None else far + total
    return jnp.concatenate(sums, axis=1), far


class _ChipWeight:
    def __init__(self, parts, along, lead=()):
        self.parts, self.along, self.lead = parts, along, tuple(lead)
        r, c = parts.shape[-2:]
        self.shape = (r, N_CHIPS * c) if along == "cols" else (N_CHIPS * r, c)

    def _gathered_is_n(self, mode):
        return (self.along == "cols") == (mode in ("nn", "tn"))

    def tile(self, mode, tn, tk):
        r, c = self.parts.shape[-2:]
        part = c if self.along == "cols" else r
        return (part, tk) if self._gathered_is_n(mode) else (tn, part)

    def spec(self, mode, tn, tk):
        squeezed = (None,) * (1 + len(self.lead))
        lead, cols, by_n = self.lead, self.along == "cols", self._gathered_is_n(mode)
        block = (tn, tk) if mode == "nt" else (tk, tn)

        def index(i, j, k):
            chip, other = (j, k) if by_n else (k, j)
            return (chip,) + lead + ((other, 0) if cols else (0, other))

        return pl.BlockSpec(squeezed + block, index)


def _matmul(a, b, *, mode, out_dtypes, epilogue=None, tiles=(), rows=(), n_sums=0, rider=None, tm=1024, tn=1024, tk=1024, name):
    b_shape = b.shape
    if mode == "nn":
        (M, K), N = a.shape, b_shape[1]
    elif mode == "nt":
        (M, K), N = a.shape, b_shape[0]
    else:
        (K, M), N = a.shape, b_shape[1]
    if isinstance(b, _ChipWeight):
        tn, tk = b.tile(mode, tn, tk)
    tm, tn, tk = min(tm, M), min(tn, N), min(tk, K)
    assert M % tm == 0 and N % tn == 0 and K % tk == 0, (name, M, N, K)
    nk = K // tk
    a_spec = pl.BlockSpec((tk, tm), lambda i, j, k: (k, i)) if mode == "tn" else pl.BlockSpec((tm, tk), lambda i, j, k: (i, k))
    if isinstance(b, _ChipWeight):
        b_spec, b = b.spec(mode, tn, tk), b.parts
    else:
        b_spec = pl.BlockSpec((tn, tk), lambda i, j, k: (j, k)) if mode == "nt" else pl.BlockSpec((tk, tn), lambda i, j, k: (k, j))
    dims = {"nn": ((1,), (0,)), "nt": ((1,), (1,)), "tn": ((0,), (0,))}[mode]
    n_t, n_r, n_o = len(tiles), len(rows), len(out_dtypes)
    n_x = 0 if rider is None else len(rider.arrays)
    grid = (M // tm, N // tn, nk)
    if epilogue is None:
        epilogue = lambda acc: (acc,)

    def body(a_ref, b_ref, *rest):
        extra, x_in, rest = rest[:n_t + n_r], rest[n_t + n_r:n_t + n_r + n_x], rest[n_t + n_r + n_x:]
        outs, sums, x_out, acc_ref, sems = rest[:n_o], rest[n_o:n_o + n_sums], rest[n_o + n_sums:n_o + n_sums + n_x], rest[n_o + n_sums + n_x], rest[n_o + n_sums + n_x + 1:]
        i, j, k = (pl.program_id(d) for d in range(3))
        if rider is not None:
            @pl.when((i == 0) & (j == 0) & (k == 0))
            def _():
                for cp in rider.make(x_in, x_out, *sems):
                    cp.start()

        @pl.when(k == 0)
        def _():
            acc_ref[...] = jnp.zeros_like(acc_ref)

        acc_ref[...] += lax.dot_general(_mx(a_ref[...]), _mx(b_ref[...]), (dims, ((), ())), preferred_element_type=F32)

        @pl.when(k == nk - 1)
        def _():
            res = epilogue(acc_ref[...], *[e[...] for e in extra])
            for o_ref, r in zip(outs, res[:n_o]):
                o_ref[...] = r.astype(o_ref.dtype)
            for s_ref, r in zip(sums, res[n_o:]):
                @pl.when(i == 0)
                def _():
                    s_ref[...] = jnp.zeros_like(s_ref)

                s_ref[...] += r

        if rider is not None:
            @pl.when((i == grid[0] - 1) & (j == grid[1] - 1) & (k == grid[2] - 1))
            def _():
                for cp in rider.make(x_in, x_out, *sems):
                    cp.wait()

    tile_spec = pl.BlockSpec((tm, tn), lambda i, j, k: (i, j))
    row_spec = pl.BlockSpec((1, tn), lambda i, j, k: (0, j))
    ordered = n_sums > 0 or rider is not None
    outs = pl.pallas_call(
        body, grid=grid,
        in_specs=[a_spec, b_spec] + [tile_spec] * n_t + [row_spec] * n_r + [HBM_SPEC] * n_x,
        out_specs=[tile_spec] * n_o + [row_spec] * n_sums + [HBM_SPEC] * n_x,
        out_shape=[jax.ShapeDtypeStruct((M, N), dt) for dt in out_dtypes] + [jax.ShapeDtypeStruct((1, N), F32)] * n_sums
        + ([] if rider is None else rider.out_shapes),
        scratch_shapes=[pltpu.VMEM((tm, tn), F32)] + ([] if rider is None else rider.scratch()),
        compiler_params=_params(*(("arbitrary",) * 3 if ordered else ("parallel", "parallel", "arbitrary"))), name=name,
    )(a, b, *tiles, *rows, *([] if rider is None else rider.arrays))
    res = outs[0] if n_o + n_sums == 1 else outs[:n_o + n_sums]
    return res if rider is None else (res, outs[n_o + n_sums:])


def _token_block(T):
    return min(512, T)


def _rmsnorm_fwd(h, g, *, name):
    T, D = h.shape
    tb = _token_block(T)

    def body(h_ref, g_ref, u_ref):
        x = h_ref[...]
        r = lax.rsqrt(jnp.mean(x * x, axis=-1, keepdims=True) + RMS_EPS)
        u_ref[...] = (x * r * g_ref[...]).astype(u_ref.dtype)

    blk = pl.BlockSpec((tb, D), lambda i: (i, 0))
    return pl.pallas_call(
        body, grid=(T // tb,), in_specs=[blk, pl.BlockSpec((1, D), lambda i: (0, 0))], out_specs=blk,
        out_shape=jax.ShapeDtypeStruct((T, D), MXU_DTYPE), compiler_params=_params("parallel"), name=name,
    )(h, g)


def _rms_bwd_math(x, g, du):
    r = lax.rsqrt(jnp.mean(x * x, axis=-1, keepdims=True) + RMS_EPS)
    gd = g * du
    dx = r * gd - x * (r * r * r) * jnp.mean(gd * x, axis=-1, keepdims=True)
    return dx, du * x * r


def _loss_head(h, g, target, *, name):
    T, D = h.shape
    tb = _token_block(T)

    def body(h_ref, g_ref, t_ref, dh_ref, dhm_ref, dg_ref, loss_ref):
        @pl.when(pl.program_id(0) == 0)
        def _():
            dg_ref[...] = jnp.zeros_like(dg_ref)
            loss_ref[...] = jnp.zeros_like(loss_ref)

        x, gg = h_ref[...], g_ref[...]
        r = lax.rsqrt(jnp.mean(x * x, axis=-1, keepdims=True) + RMS_EPS)
        diff = x * r * gg - t_ref[...]
        per_token = jnp.mean(diff * diff, axis=-1, keepdims=True)
        loss_ref[...] += 0.5 * jnp.sum(per_token, axis=0, keepdims=True)
        dx, dg_terms = _rms_bwd_math(x, gg, diff / D)
        dh_ref[...] = dx
        dhm_ref[...] = dx.astype(dhm_ref.dtype)
        dg_ref[...] += jnp.sum(dg_terms, axis=0, keepdims=True)

    blk = pl.BlockSpec((tb, D), lambda i: (i, 0))
    row = pl.BlockSpec((1, D), lambda i: (0, 0))
    return pl.pallas_call(
        body, grid=(T // tb,), in_specs=[blk, row, blk], out_specs=[blk, blk, row, pl.BlockSpec((1, 1), lambda i: (0, 0))],
        out_shape=[jax.ShapeDtypeStruct((T, D), F32), jax.ShapeDtypeStruct((T, D), MXU_DTYPE), jax.ShapeDtypeStruct((1, D), F32),
                   jax.ShapeDtypeStruct((1, 1), F32)],
        compiler_params=_params("arbitrary"), name=name,
    )(h, g, target)


def _sb_scores(qm, ks, later, tri, mask, need_log_beta=True):
    z = _dot_nt(qm, ks)
    sp = jnp.maximum(z, 0.0) + jnp.log(1.0 + jnp.exp(_neg_abs(z)))
    lb = z - sp if need_log_beta else None
    if mask is not None:
        sp = jnp.where(mask, sp, 0.0)
    after, total = _key_order_sums(sp, tri, later=True)
    w = jnp.exp((lb if need_log_beta else z) - (after + later))
    if mask is not None:
        w = jnp.where(mask, w, 0.0)
    return total, lb, w


def _sb_setup(q_ref, rows, inclusive=False):
    i, hsel = pl.program_id(1), pl.program_id(2)
    lane = lax.broadcasted_iota(jnp.int32, (rows, 2 * SB_DH), 1)
    mine = (lane >= SB_DH) == (hsel == 1)
    diag = (i * rows) // SB_KEYS
    t = i * rows + lax.broadcasted_iota(jnp.int32, (rows, SB_KEYS), 0)
    s = diag * SB_KEYS + lax.broadcasted_iota(jnp.int32, (rows, SB_KEYS), 1)
    a = lax.broadcasted_iota(jnp.int32, (2 * SB_SUB, SB_SUB), 0) % SB_SUB
    b = lax.broadcasted_iota(jnp.int32, (2 * SB_SUB, SB_SUB), 1)
    return i, hsel, mine, diag, s < t, _mx(a >= b if inclusive else a > b), _mx(a < b)


def _sb_keys(j, n=1):
    return pl.ds(pl.multiple_of(j * SB_KEYS, SB_KEYS), n * SB_KEYS)


def _sb_descend(n, step, carry, wide):
    pair = (lambda j, cr: step(j, 2, cr)) if wide else (lambda j, cr: step(j, 1, step(j + 1, 1, cr)))
    carry = lax.fori_loop(0, n // 2, lambda it, cr: pair(n - 2 - 2 * it, cr), carry)
    return lax.cond(n % 2 == 1, lambda cr: step(0, 1, cr), lambda cr: cr, carry)


def _sb_ascend(n, step, carry):
    odd = n % 2
    carry = lax.cond(odd == 1, lambda cr: step(0, 1, cr), lambda cr: cr, carry)
    return lax.fori_loop(0, n // 2, lambda it, cr: step(odd + 2 * it, 2, cr), carry)


def _sb_call(body, qkv, extra_in, out_blocks, out_dtype, scratch, rider, rows, *, name):
    T = qkv.shape[0]
    n_pairs = SB_HEADS // 2
    grid = (n_pairs, T // rows, 2)
    pair = lambda col0: pl.BlockSpec((rows, 2 * SB_DH), lambda p, i, h: (i, col0 + p))
    whole = lambda col0: pl.BlockSpec((T, 2 * SB_DH), lambda p, i, h: (0, col0 + p))
    in_specs = [pair(0), whole(n_pairs), whole(2 * n_pairs)] + [pair(0)] * len(extra_in)
    out_specs = [pair(0) if kind == "pair" else whole(0) for kind in out_blocks]
    n_in, n_out, n_r = len(in_specs), len(out_specs), 0 if rider is None else len(rider.arrays)

    def kernel_body(*refs):
        ins, r_in = refs[:n_in], refs[n_in:n_in + n_r]
        outs, r_out = refs[n_in + n_r:n_in + n_r + n_out], refs[n_in + n_r + n_out:n_in + 2 * n_r + n_out]
        rest = refs[n_in + 2 * n_r + n_out:]
        ids = [pl.program_id(a) for a in range(3)]
        if rider is not None:
            @pl.when((ids[0] == 0) & (ids[1] == 0) & (ids[2] == 0))
            def _():
                for cp in rider.make(r_in, r_out, *rest[len(scratch):]):
                    cp.start()

        body(ins, outs, rest[:len(scratch)])
        if rider is not None:
            @pl.when((ids[0] == grid[0] - 1) & (ids[1] == grid[1] - 1) & (ids[2] == grid[2] - 1))
            def _():
                for cp in rider.make(r_in, r_out, *rest[len(scratch):]):
                    cp.wait()

    res = pl.pallas_call(
        kernel_body, grid=grid, in_specs=in_specs + [HBM_SPEC] * n_r, out_specs=out_specs + [HBM_SPEC] * n_r,
        out_shape=[jax.ShapeDtypeStruct((T, SB_WIDTH), out_dtype)] * n_out + ([] if rider is None else rider.out_shapes),
        scratch_shapes=list(scratch) + ([] if rider is None else rider.scratch()),
        compiler_params=_params("arbitrary", "arbitrary", "arbitrary"), name=name,
    )(qkv, qkv, qkv, *extra_in, *([] if rider is None else rider.arrays))
    return res[:n_out], res[n_out:]


def _sb_fwd(qkv, rider=None, *, name):
    rows = min(SB_ROWS_FWD, qkv.shape[0])
    scale = SB_DH ** -0.5

    def body(ins, outs, _):
        (q_ref, k_ref, v_ref), (o_ref,) = ins, outs
        i, hsel, mine, diag, mask, tri, _ = _sb_setup(q_ref, rows, inclusive=True)
        qm = jnp.where(mine, q_ref[...], 0) * scale

        def tile(j, n, m, later, acc):
            total, _, w = _sb_scores(qm, k_ref[_sb_keys(j, n), :], later, tri, m, need_log_beta=False)
            return later + total, acc + _dot(w, v_ref[_sb_keys(j, n), :])

        carry = tile(diag, 1, mask, jnp.zeros((rows, 1), F32), jnp.zeros((rows, 2 * SB_DH), F32))
        _, acc = _sb_descend(diag, lambda j, n, cr: tile(j, n, None, *cr), carry, wide=False)
        res = jnp.where(mine, acc, 0.0).astype(o_ref.dtype)

        @pl.when(hsel == 0)
        def _():
            o_ref[...] = res

        @pl.when(hsel == 1)
        def _():
            o_ref[...] += res

    (o,), got = _sb_call(body, qkv, [], ["pair"], MXU_DTYPE, [], rider, rows, name=name)
    return o, got


def _sb_bwd(qkv, dmix, rider=None, *, name):
    T = qkv.shape[0]
    rows = min(SB_ROWS_BWD, T)
    scale = SB_DH ** -0.5

    def body(ins, outs, scratch):
        (q_ref, k_ref, v_ref, do_ref), (dq_ref, dk_ref, dv_ref), (da_ref, beta_ref) = ins, outs, scratch
        i, hsel, mine, diag, mask, tri, tri_before = _sb_setup(q_ref, rows)

        @pl.when((i == 0) & (hsel == 0))
        def _():
            dk_ref[...] = jnp.zeros_like(dk_ref)
            dv_ref[...] = jnp.zeros_like(dv_ref)

        qm = jnp.where(mine, q_ref[...], 0) * scale
        do_m = _mx(jnp.where(mine, do_ref[...], 0.0))

        def weights(j, n, m, later):
            total, lb, w = _sb_scores(qm, k_ref[_sb_keys(j, n), :], later, tri, m)
            da, beta = _dot_nt(do_m, v_ref[_sb_keys(j, n), :]) * w, jnp.exp(lb)
            for t in range(n):
                da_ref[j + t] = da[:, t * SB_KEYS:(t + 1) * SB_KEYS]
                beta_ref[j + t] = beta[:, t * SB_KEYS:(t + 1) * SB_KEYS]
            dv_ref[_sb_keys(j, n), :] += _dot_tn(w, do_m)
            return later + total

        later = weights(diag, 1, mask, jnp.zeros((rows, 1), F32))
        _sb_descend(diag, lambda j, n, c: weights(j, n, None, c), later, wide=True)

        def logits(j, n, m, before, dq):
            da = jnp.concatenate([da_ref[j + t] for t in range(n)], axis=1)
            beta = jnp.concatenate([beta_ref[j + t] for t in range(n)], axis=1)
            earlier, total = _key_order_sums(da, tri_before, later=False)
            dz = da - beta * (da + earlier + before)
            if m is not None:
                dz = jnp.where(m, dz, 0.0)
            dz = _mx(dz)
            dk_ref[_sb_keys(j, n), :] += _dot_tn(dz, qm)
            return before + total, dq + _dot(dz, k_ref[_sb_keys(j, n), :])

        carry = (jnp.zeros((rows, 1), F32), jnp.zeros((rows, 2 * SB_DH), F32))
        carry = _sb_ascend(diag, lambda j, n, cr: logits(j, n, None, *cr), carry)
        res = jnp.where(mine, logits(diag, 1, mask, *carry)[1] * scale, 0.0)

        @pl.when(hsel == 0)
        def _():
            dq_ref[...] = res

        @pl.when(hsel == 1)
        def _():
            dq_ref[...] += res

    n_tiles = T // SB_KEYS
    scratch = [pltpu.VMEM((n_tiles, rows, SB_KEYS), F32), pltpu.VMEM((n_tiles, rows, SB_KEYS), F32)]
    return _sb_call(body, qkv, [dmix], ["pair", "whole", "whole"], F32, scratch, rider, rows, name=name)


def _chunk_row(n):
    return lax.broadcasted_iota(jnp.int32, (n, HG_DH), 0) % HG_CHUNK


def _chunk_cumsum(x, row, reverse=False):
    n = x.shape[0]
    for sh in (1, 2, 4, 8):
        if reverse:
            x = x + jnp.where(row < HG_CHUNK - sh, pltpu.roll(x, n - sh, 0), 0.0)
        else:
            x = x + jnp.where(row >= sh, pltpu.roll(x, sh, 0), 0.0)
    return x


def _hg_lower_bound(logits_ref):
    lg = logits_ref[...]
    e = jnp.exp(lg - jnp.max(lg, axis=0, keepdims=True))
    return e[0:1, :] / jnp.sum(e, axis=0, keepdims=True)


def _hg_terms(fr, q, lb, row):
    sig = jax.nn.sigmoid(fr)
    f = lb + (1.0 - lb) * sig
    kk = 1.0 - f
    g = jnp.log(f)
    G = _chunk_cumsum(g, row)
    g_last = G + (_chunk_cumsum(g, row, reverse=True) - g)
    e_g, e_ng, e_lg = jnp.exp(G), jnp.exp(-G), jnp.exp(g_last - G)
    return dict(sig=sig, f=f, kk=kk, e_g=e_g, e_ng=e_ng, e_lg=e_lg, q_dec=q * e_g, k_intra=kk * e_ng,
                k_state=kk * e_lg, decay=jnp.exp(g_last))


def _hg_causal(n):
    t = lax.broadcasted_iota(jnp.int32, (n, n), 0)
    s = lax.broadcasted_iota(jnp.int32, (n, n), 1)
    return (s <= t) & (s // HG_CHUNK == t // HG_CHUNK)


def _chunks(a):
    return a.reshape(a.shape[0] // HG_CHUNK, HG_CHUNK, a.shape[1])


def _per_chunk(lhs, rhs, contract):
    return lax.dot_general(_mx(lhs), _mx(rhs), ((contract[:1], contract[1:]), ((0,), (0,))), preferred_element_type=F32)


def _hg_specs(T, tb, col0, order):
    return [pl.BlockSpec((tb, HG_WIDTH), functools.partial(lambda i, j: (order(i), j), j=col0 + j)) for j in range(4)]


def _hg_fwd(proj, logits, norm_g, *, name):
    T = proj.shape[0]
    tb = min(HG_TOKENS, T)
    nch = tb // HG_CHUNK

    def body(q_ref, f_ref, i_ref, gate_ref, lg_ref, ng_ref, out_ref, o_ref, s_ref, st_ref, inc_ref, dec_ref):
        @pl.when(pl.program_id(0) == 0)
        def _():
            st_ref[...] = jnp.zeros_like(st_ref)

        lb_all = _hg_lower_bound(lg_ref)
        row = _chunk_row(tb)
        causal = _hg_causal(tb)
        for hh in range(HG_HEADS):
            cols = slice(hh * HG_DH, (hh + 1) * HG_DH)
            t = _hg_terms(f_ref[:, cols], q_ref[:, cols], lb_all[:, cols], row)
            v = i_ref[:, cols]
            scores = jnp.where(causal, _dot_nt(t["q_dec"], t["k_intra"]), 0.0)
            o_intra = _dot(scores, v)
            inc_ref[...] = _per_chunk(_chunks(v), _chunks(t["k_state"]), (1, 1))
            dec_ref[...] = _chunks(t["decay"])

            def step(ci, st):
                s_ref[ci, hh] = st
                return st * dec_ref[ci][0:1, :] + inc_ref[ci]

            st_ref[hh] = lax.fori_loop(0, nch, step, st_ref[hh], unroll=4)
            o_inter = _per_chunk(_chunks(t["q_dec"]), s_ref[:, hh], (2, 2))
            o = o_intra + o_inter.reshape(tb, HG_DH)
            o_ref[:, cols] = o
            gate = gate_ref[:, cols]
            on = o * lax.rsqrt(jnp.mean(o * o, axis=-1, keepdims=True) + RMS_EPS) * ng_ref[:, cols]
            out_ref[:, cols] = (on * (gate * jax.nn.sigmoid(gate))).astype(out_ref.dtype)

    blk = pl.BlockSpec((tb, HG_WIDTH), lambda i: (i, 0))
    return pl.pallas_call(
        body, grid=(T // tb,),
        in_specs=_hg_specs(T, tb, 3, lambda i: i) + [pl.BlockSpec((3, HG_WIDTH), lambda i: (0, 0)), pl.BlockSpec((1, HG_WIDTH), lambda i: (0, 0))],
        out_specs=[blk, blk, pl.BlockSpec((nch, HG_HEADS, HG_DH, HG_DH), lambda i: (i, 0, 0, 0))],
        out_shape=[jax.ShapeDtypeStruct((T, HG_WIDTH), MXU_DTYPE), jax.ShapeDtypeStruct((T, HG_WIDTH), F32),
                   jax.ShapeDtypeStruct((T // HG_CHUNK, HG_HEADS, HG_DH, HG_DH), F32)],
        scratch_shapes=[pltpu.VMEM((HG_HEADS, HG_DH, HG_DH), F32), pltpu.VMEM((nch, HG_DH, HG_DH), F32),
                        pltpu.VMEM((nch, HG_CHUNK, HG_DH), F32)],
        compiler_params=_params("arbitrary"), name=name,
    )(proj, proj, proj, proj, logits, norm_g)


def _hg_bwd(proj, o_raw, states, dmix, logits, norm_g, *, name):
    T = proj.shape[0]
    tb = min(HG_TOKENS, T)
    nch = tb // HG_CHUNK
    nb = T // tb
    rev = lambda i: nb - 1 - i

    def body(q_ref, f_ref, i_ref, gate_ref, o_ref, s_ref, dout_ref, lg_ref, ng_ref, dp_ref, dlb_ref, dng_ref,
             dst_ref, inc_ref, dec_ref, after_ref):
        @pl.when(pl.program_id(0) == 0)
        def _():
            dst_ref[...] = jnp.zeros_like(dst_ref)
            dlb_ref[...] = jnp.zeros_like(dlb_ref)
            dng_ref[...] = jnp.zeros_like(dng_ref)

        lb_all = _hg_lower_bound(lg_ref)
        row = _chunk_row(tb)
        causal = _hg_causal(tb)
        for hh in range(HG_HEADS):
            cols = slice(hh * HG_DH, (hh + 1) * HG_DH)
            out_cols = lambda part: slice(part * HG_WIDTH + hh * HG_DH, part * HG_WIDTH + (hh + 1) * HG_DH)
            o, gate, dout, ng, lb = o_ref[:, cols], gate_ref[:, cols], dout_ref[:, cols], ng_ref[:, cols], lb_all[:, cols]
            sg = jax.nn.sigmoid(gate)
            r = lax.rsqrt(jnp.mean(o * o, axis=-1, keepdims=True) + RMS_EPS)
            oh = o * r
            dp_ref[:, out_cols(3)] = dout * (oh * ng) * (sg * (1.0 + gate * (1.0 - sg)))
            don = dout * (gate * sg)
            dng_ref[:, cols] += jnp.sum(don * oh, axis=0, keepdims=True)
            doh = don * ng
            do = r * (doh - oh * jnp.mean(doh * oh, axis=-1, keepdims=True))

            t = _hg_terms(f_ref[:, cols], q_ref[:, cols], lb, row)
            v = i_ref[:, cols]
            scores = jnp.where(causal, _dot_nt(t["q_dec"], t["k_intra"]), 0.0)
            dscores = jnp.where(causal, _dot_nt(do, v), 0.0)
            inc_ref[...] = _per_chunk(_chunks(do), _chunks(t["q_dec"]), (1, 1))
            dec_ref[...] = _chunks(t["decay"])

            def step(it, dst):
                ci = nch - 1 - it
                after_ref[ci] = dst
                return dst * dec_ref[ci][0:1, :] + inc_ref[ci]

            dst_ref[hh] = lax.fori_loop(0, nch, step, dst_ref[hh], unroll=4)
            st, dst = s_ref[:, hh], after_ref[...]
            dqd = _dot(dscores, t["k_intra"]) + _per_chunk(_chunks(do), st, (2, 1)).reshape(tb, HG_DH)
            dki = _dot_tn(dscores, t["q_dec"])
            dks = _per_chunk(_chunks(v), dst, (2, 1)).reshape(tb, HG_DH)
            dp_ref[:, out_cols(2)] = _dot_tn(scores, do) + _per_chunk(_chunks(t["k_state"]), dst, (2, 2)).reshape(tb, HG_DH)
            ddecay = jnp.broadcast_to(jnp.sum(st * dst, axis=1, keepdims=True), (nch, HG_CHUNK, HG_DH)).reshape(tb, HG_DH)
            dks_ks = dks * t["k_state"]
            d_glast = _chunk_cumsum(dks_ks, row) + ddecay * t["decay"]
            d_g = dqd * t["q_dec"] - dki * t["k_intra"] - dks_ks + jnp.where(row == HG_CHUNK - 1, d_glast, 0.0)
            df = _chunk_cumsum(d_g, row, reverse=True) / t["f"] - (dki * t["e_ng"] + dks * t["e_lg"])
            dp_ref[:, out_cols(0)] = dqd * t["e_g"]
            dp_ref[:, out_cols(1)] = df * (1.0 - lb) * t["sig"] * (1.0 - t["sig"])
            dlb_ref[:, cols] += jnp.sum(df * (1.0 - t["sig"]), axis=0, keepdims=True)

    blk = pl.BlockSpec((tb, HG_WIDTH), lambda i: (rev(i), 0))
    row_spec = pl.BlockSpec((1, HG_WIDTH), lambda i: (0, 0))
    return pl.pallas_call(
        body, grid=(nb,),
        in_specs=_hg_specs(T, tb, 3, rev) + [
            blk, pl.BlockSpec((nch, HG_HEADS, HG_DH, HG_DH), lambda i: (rev(i), 0, 0, 0)),
            pl.BlockSpec((tb, HG_WIDTH), lambda i: (rev(i), 1)), pl.BlockSpec((3, HG_WIDTH), lambda i: (0, 0)), row_spec],
        out_specs=[pl.BlockSpec((tb, 4 * HG_WIDTH), lambda i: (rev(i), 0)), row_spec, row_spec],
        out_shape=[jax.ShapeDtypeStruct((T, 4 * HG_WIDTH), F32), jax.ShapeDtypeStruct((1, HG_WIDTH), F32), jax.ShapeDtypeStruct((1, HG_WIDTH), F32)],
        scratch_shapes=[pltpu.VMEM((HG_HEADS, HG_DH, HG_DH), F32), pltpu.VMEM((nch, HG_DH, HG_DH), F32),
                        pltpu.VMEM((nch, HG_CHUNK, HG_DH), F32), pltpu.VMEM((nch, HG_DH, HG_DH), F32)],
        compiler_params=_params("arbitrary"), name=name,
    )(proj, proj, proj, proj, o_raw, states, dmix, logits, norm_g)


def _shifted_copies(sh_ref, n_rows):
    keep = n_rows + CONV_HALO - 8
    for b in range(1, 8):
        sh_ref[b, 0:keep, :] = sh_ref[0, b:b + keep, :]


def _tap_rows(sh_ref, offset, r0, lanes):
    start = pl.multiple_of(r0 + (offset - offset % 8), 8)
    return sh_ref[offset % 8, pl.ds(start, CONV_ROWS), lanes]


def _conv_fwd(p, w_dw, b_dw, ln_g, ln_b, *, name):
    T, D = p.shape[0], p.shape[1] // 2
    tb = _token_block(T)
    hpb = tb // CONV_HALO
    lane_step = 512

    def body(p1_ref, p2_ref, q1_ref, q2_ref, w_ref, bdw_ref, g_ref, b_ref, a_ref, y_ref, act_ref, sh_ref):
        i = pl.program_id(0)
        a = p1_ref[...] * jax.nn.sigmoid(p2_ref[...])
        sh_ref[0, 0:CONV_HALO, :] = jnp.where(i > 0, q1_ref[...] * jax.nn.sigmoid(q2_ref[...]), 0.0)
        sh_ref[0, CONV_HALO:, :] = a
        a_ref[...] = a
        _shifted_copies(sh_ref, tb)

        def chunk(ci, _):
            r0 = pl.multiple_of(ci * CONV_ROWS, CONV_ROWS)
            for l0 in range(0, D, lane_step):
                lanes = slice(l0, l0 + lane_step)
                acc = jnp.broadcast_to(bdw_ref[:, lanes], (CONV_ROWS, lane_step))
                for k in range(CONV_WIDTH):
                    acc = acc + _tap_rows(sh_ref, CONV_HALO - CONV_WIDTH + 1 + k, r0, lanes) * w_ref[k:k + 1, lanes]
                y_ref[pl.ds(r0, CONV_ROWS), lanes] = acc
            return 0

        lax.fori_loop(0, tb // CONV_ROWS, chunk, 0)
        y = y_ref[...]
        mu = jnp.mean(y, axis=-1, keepdims=True)
        yc = y - mu
        s = yc * lax.rsqrt(jnp.mean(yc * yc, axis=-1, keepdims=True) + LN_EPS) * g_ref[...] + b_ref[...]
        act_ref[...] = (s * jax.nn.sigmoid(s)).astype(act_ref.dtype)

    prev = lambda i: jnp.maximum(i * hpb - 1, 0)
    blk = pl.BlockSpec((tb, D), lambda i: (i, 0))
    row = pl.BlockSpec((1, D), lambda i: (0, 0))
    return pl.pallas_call(
        body, grid=(T // tb,),
        in_specs=[blk, pl.BlockSpec((tb, D), lambda i: (i, 1)), pl.BlockSpec((CONV_HALO, D), lambda i: (prev(i), 0)),
                  pl.BlockSpec((CONV_HALO, D), lambda i: (prev(i), 1)), pl.BlockSpec((CONV_HALO, D), lambda i: (0, 0)), row, row, row],
        out_specs=[blk, blk, blk],
        out_shape=[jax.ShapeDtypeStruct((T, D), F32), jax.ShapeDtypeStruct((T, D), F32), jax.ShapeDtypeStruct((T, D), MXU_DTYPE)],
        scratch_shapes=[pltpu.VMEM((8, tb + CONV_HALO, D), F32)],
        compiler_params=_params("parallel"), name=name,
    )(p, p, p, p, w_dw, b_dw, ln_g, ln_b)


def _conv_bwd_norm(dact, y, ln_g, ln_b, *, name):
    T, D = y.shape
    tb = _token_block(T)

    def body(da_ref, y_ref, g_ref, b_ref, dy_ref, dg_ref, db_ref, cs_ref):
        @pl.when(pl.program_id(0) == 0)
        def _():
            dg_ref[...] = jnp.zeros_like(dg_ref)
            db_ref[...] = jnp.zeros_like(db_ref)
            cs_ref[...] = jnp.zeros_like(cs_ref)

        y, g = y_ref[...], g_ref[...]
        yc = y - jnp.mean(y, axis=-1, keepdims=True)
        rs = lax.rsqrt(jnp.mean(yc * yc, axis=-1, keepdims=True) + LN_EPS)
        yn = yc * rs
        s = yn * g + b_ref[...]
        sg = jax.nn.sigmoid(s)
        ds = da_ref[...] * (sg * (1.0 + s * (1.0 - sg)))
        dg_ref[...] += jnp.sum(ds * yn, axis=0, keepdims=True)
        db_ref[...] += jnp.sum(ds, axis=0, keepdims=True)
        dyn = ds * g
        dy = rs * (dyn - jnp.mean(dyn, axis=-1, keepdims=True) - yn * jnp.mean(dyn * yn, axis=-1, keepdims=True))
        dy_ref[...] = dy
        cs_ref[...] += jnp.sum(dy, axis=0, keepdims=True)

    blk = pl.BlockSpec((tb, D), lambda i: (i, 0))
    row = pl.BlockSpec((1, D), lambda i: (0, 0))
    rs_ = jax.ShapeDtypeStruct((1, D), F32)
    return pl.pallas_call(
        body, grid=(T // tb,), in_specs=[blk, blk, row, row], out_specs=[blk, row, row, row],
        out_shape=[jax.ShapeDtypeStruct((T, D), F32), rs_, rs_, rs_], compiler_params=_params("arbitrary"), name=name,
    )(dact, y, ln_g, ln_b)


def _conv_bwd_taps(dy, a, p, w_dw, *, name):
    T, D = dy.shape
    tb = _token_block(T)
    hpb = tb // CONV_HALO
    last = T // CONV_HALO - 1
    nb = T // tb
    lane_step = 128
    groups = CONV_ROWS // 8

    def body(dy_ref, dyn_ref, a_ref, p1_ref, p2_ref, w_ref, dp_ref, dw_ref, cs_ref, sh_ref, da_ref):
        i = pl.program_id(0)

        @pl.when(i == 0)
        def _():
            dw_ref[...] = jnp.zeros_like(dw_ref)
            cs_ref[...] = jnp.zeros_like(cs_ref)

        sh_ref[0, 0:tb, :] = dy_ref[...]
        sh_ref[0, tb:, :] = jnp.where(i < nb - 1, dyn_ref[...], 0.0)
        _shifted_copies(sh_ref, tb)
        for l0 in range(0, D, lane_step):
            lanes = slice(l0, l0 + lane_step)

            def chunk(ci, sums):
                r0 = pl.multiple_of(ci * CONV_ROWS, CONV_ROWS)
                a_c = a_ref[pl.ds(r0, CONV_ROWS), lanes]
                da = jnp.zeros((CONV_ROWS, lane_step), F32)
                new = []
                for k in range(CONV_WIDTH):
                    s_k = _tap_rows(sh_ref, CONV_WIDTH - 1 - k, r0, lanes)
                    da = da + s_k * w_ref[k:k + 1, lanes]
                    new.append(sums[k] + jnp.sum((s_k * a_c).reshape(groups, 8, lane_step), axis=0))
                da_ref[pl.ds(r0, CONV_ROWS), lanes] = da
                return tuple(new)

            sums = lax.fori_loop(0, tb // CONV_ROWS, chunk, tuple(jnp.zeros((8, lane_step), F32) for _ in range(CONV_WIDTH)))
            for k in range(CONV_WIDTH):
                dw_ref[k:k + 1, lanes] += jnp.sum(sums[k], axis=0, keepdims=True)
        da = da_ref[...]
        p1 = p1_ref[...]
        sg = jax.nn.sigmoid(p2_ref[...])
        dp1 = da * sg
        dp2 = da * p1 * (sg * (1.0 - sg))
        dp_ref[:, 0:D] = dp1.astype(dp_ref.dtype)
        dp_ref[:, D:] = dp2.astype(dp_ref.dtype)
        cs_ref[:, 0:D] += jnp.sum(dp1, axis=0, keepdims=True)
        cs_ref[:, D:] += jnp.sum(dp2, axis=0, keepdims=True)

    blk = pl.BlockSpec((tb, D), lambda i: (i, 0))
    return pl.pallas_call(
        body, grid=(nb,),
        in_specs=[blk, pl.BlockSpec((CONV_HALO, D), lambda i: (jnp.minimum((i + 1) * hpb, last), 0)), blk, blk,
                  pl.BlockSpec((tb, D), lambda i: (i, 1)), pl.BlockSpec((CONV_HALO, D), lambda i: (0, 0))],
        out_specs=[pl.BlockSpec((tb, 2 * D), lambda i: (i, 0)), pl.BlockSpec((CONV_HALO, D), lambda i: (0, 0)), pl.BlockSpec((1, 2 * D), lambda i: (0, 0))],
        out_shape=[jax.ShapeDtypeStruct((T, 2 * D), MXU_DTYPE), jax.ShapeDtypeStruct((CONV_HALO, D), F32), jax.ShapeDtypeStruct((1, 2 * D), F32)],
        scratch_shapes=[pltpu.VMEM((8, tb + CONV_HALO, D), F32), pltpu.VMEM((tb, D), F32)],
        compiler_params=_params("arbitrary"), name=name,
    )(dy, dy, a, p, p, w_dw)


def _row_block(rows):
    for tr in (512, 256, 128, 64, 32, 16, 8):
        if rows % tr == 0:
            return tr
    return rows


def _sum_leading(x, *, name):
    n, R, C = x.shape
    tr = _row_block(R)

    def body(x_ref, o_ref):
        acc = x_ref[0]
        for j in range(1, n):
            acc = acc + x_ref[j]
        o_ref[...] = acc

    return pl.pallas_call(
        body, grid=(R // tr,), in_specs=[pl.BlockSpec((n, tr, C), lambda i: (0, i, 0))], out_specs=pl.BlockSpec((tr, C), lambda i: (i, 0)),
        out_shape=jax.ShapeDtypeStruct((R, C), x.dtype), compiler_params=_params("parallel"), name=name,
    )(x)


def _add_pair(x, y, *, name):
    n, R, C = x.shape
    tr = _row_block(R)

    def body(x_ref, y_ref, o_ref):
        o_ref[...] = x_ref[...] + y_ref[...]

    blk = pl.BlockSpec((1, tr, C), lambda j, i: (j, i, 0))
    return pl.pallas_call(
        body, grid=(n, R // tr), in_specs=[blk, blk], out_specs=blk,
        out_shape=jax.ShapeDtypeStruct((n, R, C), x.dtype), compiler_params=_params("parallel", "parallel"), name=name,
    )(x, y)


def _adamw(w, g, m, v, *, name):
    R, C = w.shape
    tr = _row_block(R)
    c1, c2 = 1.0 - ADAM_B1 ** ADAM_STEP, 1.0 - ADAM_B2 ** ADAM_STEP

    def body(w_ref, g_ref, m_ref, v_ref, d_ref, nm_ref, nv_ref):
        g_ = g_ref[...]
        nm = ADAM_B1 * m_ref[...] + (1.0 - ADAM_B1) * g_
        nv = ADAM_B2 * v_ref[...] + (1.0 - ADAM_B2) * (g_ * g_)
        d_ref[...] = -ADAM_LR * ((nm / c1) / (jnp.sqrt(nv / c2) + ADAM_EPS) + ADAM_WD * w_ref[...])
        nm_ref[...] = nm
        nv_ref[...] = nv

    blk = pl.BlockSpec((tr, C), lambda i: (i, 0))
    shp = jax.ShapeDtypeStruct((R, C), F32)
    return pl.pallas_call(
        body, grid=(R // tr,), in_specs=[blk] * 4, out_specs=[blk] * 3, out_shape=[shp] * 3,
        compiler_params=_params("parallel"), name=name,
    )(w, g, m, v)


def _small_reduce(packs, logits, *, name):
    n, R, C = packs.shape

    def body(p_ref, lg_ref, s_ref, dlg_ref):
        acc = p_ref[0]
        for j in range(1, n):
            acc = acc + p_ref[j]
        s_ref[...] = acc
        lg = lg_ref[...]
        e = jnp.exp(lg - jnp.max(lg, axis=0, keepdims=True))
        sm = e / jnp.sum(e, axis=0, keepdims=True)
        dlb = acc[5:6, HG_WIDTH:2 * HG_WIDTH]
        first = lax.broadcasted_iota(jnp.int32, sm.shape, 0) == 0
        dlg_ref[...] = sm[0:1, :] * (jnp.where(first, 1.0, 0.0) - sm) * dlb

    whole = lambda shape: pl.BlockSpec(shape, lambda: (0,) * len(shape))
    return pl.pallas_call(
        body, in_specs=[whole((n, R, C)), whole(logits.shape)], out_specs=[whole((R, C)), whole(logits.shape)],
        out_shape=[jax.ShapeDtypeStruct((R, C), F32), jax.ShapeDtypeStruct(logits.shape, F32)],
        compiler_params=pltpu.CompilerParams(vmem_limit_bytes=VMEM_LIMIT), name=name,
    )(packs, logits)


HBM_SPEC = pl.BlockSpec(memory_space=pl.ANY)


def _place():
    return lax.axis_index("x"), lax.axis_index("y"), lax.axis_index("c")


class _Copies:
    def __init__(self, arrays, out_shapes, n_copies, make, finish):
        self.arrays, self.out_shapes, self.n_copies, self.make, self.finish = list(arrays), list(out_shapes), n_copies, make, finish

    def scratch(self):
        return [pltpu.SemaphoreType.DMA((self.n_copies,)), pltpu.SemaphoreType.DMA((self.n_copies,))]

    def run(self, name):
        n = len(self.arrays)

        def body(*refs):
            copies = self.make(refs[:n], refs[n:2 * n], *refs[2 * n:])
            for cp in copies:
                cp.start()
            for cp in copies:
                cp.wait()

        outs = pl.pallas_call(body, in_specs=[HBM_SPEC] * n, out_specs=[HBM_SPEC] * n, out_shape=self.out_shapes,
                              scratch_shapes=self.scratch(), name=name)(*self.arrays)
        return self.finish(outs)


def _remote(src, dst, send_sems, recv_sems, k, peer):
    return pltpu.make_async_remote_copy(src_ref=src, dst_ref=dst, send_sem=send_sems.at[k], recv_sem=recv_sems.at[k],
                                        device_id=peer, device_id_type=MESH)


def _same_core_peers(x, y, c):
    return [(1 - x, y, c), (x, 1 - y, c), (1 - x, 1 - y, c)]


def _all_peers(x, y, c):
    flip = lambda v, b: 1 - v if b else v
    return [(flip(x, r & 4), flip(y, r & 2), flip(c, r & 1)) for r in range(1, 8)]


def _gather(arrays, peers_of, slot_of, n_slots):
    n_peers = len(peers_of(0, 0, 0))

    def make(ins, outs, send_sems, recv_sems):
        x, y, c = _place()
        slot = slot_of(x, y, c)
        return [_remote(ins[a], outs[a].at[slot], send_sems, recv_sems, a * n_peers + k, peer)
                for a in range(len(arrays)) for k, peer in enumerate(peers_of(x, y, c))]

    def finish(outs):
        slot = slot_of(*_place())
        return [lax.dynamic_update_index_in_dim(o, a, slot, 0) for o, a in zip(outs, arrays)]

    shapes = [jax.ShapeDtypeStruct((n_slots,) + a.shape, a.dtype) for a in arrays]
    return _Copies(arrays, shapes, len(arrays) * n_peers, make, finish)


def _gather_chips(arrays):
    return _gather(arrays, _same_core_peers, lambda x, y, c: 2 * x + y, N_CHIPS)


def _gather_all(arrays):
    return _gather(arrays, _all_peers, lambda x, y, c: 4 * x + 2 * y + c, N_DEV)


def _pair_swap(a):
    def make(ins, outs, send_sems, recv_sems):
        x, y, c = _place()
        return [_remote(ins[0].at[1 - c], outs[0], send_sems, recv_sems, 0, (x, y, 1 - c))]

    return _Copies([a], [jax.ShapeDtypeStruct(a.shape[1:], a.dtype)], 1, make, lambda outs: outs[0])


def _chip_scatter(p):
    def make(ins, outs, send_sems, recv_sems):
        x, y, c = _place()
        return [_remote(ins[0].at[2 * px + py], outs[0].at[2 * x + y], send_sems, recv_sems, k, (px, py, pc))
                for k, (px, py, pc) in enumerate(_same_core_peers(x, y, c))]

    def finish(outs):
        x, y, _ = _place()
        me = 2 * x + y
        return lax.dynamic_update_index_in_dim(outs[0], lax.dynamic_index_in_dim(p, me, 0, keepdims=False), me, 0)

    return _Copies([p], [jax.ShapeDtypeStruct(p.shape, p.dtype)], 3, make, finish)


def _owner_scatter(blocks):
    def make(ins, outs, send_sems, recv_sems):
        x, y, c = _place()
        return [_remote(ins[0].at[2 * px + py, pc], outs[0].at[4 * x + 2 * y + c], send_sems, recv_sems, k, (px, py, pc))
                for k, (px, py, pc) in enumerate(_all_peers(x, y, c))]

    def finish(outs):
        x, y, c = _place()
        mine = lax.dynamic_index_in_dim(lax.dynamic_index_in_dim(blocks, 2 * x + y, 0, keepdims=False), c, 0, keepdims=False)
        return lax.dynamic_update_index_in_dim(outs[0], mine, 4 * x + 2 * y + c, 0)

    return _Copies([blocks], [jax.ShapeDtypeStruct((N_DEV,) + blocks.shape[2:], blocks.dtype)], N_DEV - 1, make, finish)


def _pair_gather(q):
    def make(ins, outs, send_sems, recv_sems):
        x, y, c = _place()
        return [_remote(ins[0], outs[0].at[c], send_sems, recv_sems, 0, (x, y, 1 - c))]

    return _Copies([q], [jax.ShapeDtypeStruct((2,) + q.shape, q.dtype)], 1, make,
                   lambda outs: lax.dynamic_update_index_in_dim(outs[0], q, _place()[2], 0))


def _grad_blocks(dw, kind):
    if kind == "cols2d":
        K, N = dw.shape
        b = dw.reshape(2, K // 2, N_CHIPS, N // N_CHIPS).transpose(2, 0, 1, 3)
    elif kind == "rows2d":
        b = dw.reshape(N_CHIPS, 2, dw.shape[0] // 8, dw.shape[1])
    elif kind == "cols3d":
        L, K, N = dw.shape
        b = dw.reshape(L, K, N_CHIPS, N // N_CHIPS).transpose(2, 0, 1, 3)
    else:
        L, K, N = dw.shape
        b = dw.reshape(L, N_CHIPS, K // N_CHIPS, N).transpose(1, 0, 2, 3)
    return b.reshape(N_CHIPS, 2, -1, D_MODEL)


def _pad_rows(a, rows):
    return jnp.concatenate([a, jnp.zeros((rows - a.shape[0],) + a.shape[1:], a.dtype)], axis=0)


def _forward_backward(x, target, W, late_weights=None, early_grads=None, last_grads=None):
    row = lambda a: a.reshape(1, -1)
    relu2 = lambda acc: (jnp.square(jnp.maximum(acc, 0.0)),)
    normed = lambda h, g: h * lax.rsqrt(jnp.mean(h * h, axis=-1, keepdims=True) + RMS_EPS) * g

    def residual_norm(acc, res, g):
        h = res + acc
        return h, normed(h, g)

    def norm_bwd(du, h_blk, dres_blk, g):
        dx, dg_terms = _rms_bwd_math(h_blk, g, du)
        dh = dres_blk + dx
        return dh, dh, jnp.sum(dg_terms, axis=0, keepdims=True), jnp.sum(dh, axis=0, keepdims=True)

    def matmul_norm_bwd(dy, w, h_in, g, dres, rider=None, *, tk=1024, name):
        return _matmul(dy, w, mode="nt", out_dtypes=[F32, MXU_DTYPE], epilogue=norm_bwd, tiles=[h_in, dres], rows=[row(g)],
                       n_sums=2, rider=rider, tk=tk, name=name)

    G = {}

    u0 = _rmsnorm_fwd(x, row(W["norm_mix_g"][0]), name="norm_mix0")
    proj, qkv = _matmul(u0, W["w_in"], mode="nn", out_dtypes=[F32, MXU_DTYPE], epilogue=lambda acc: (acc, acc), tn=896, name="in_proj")
    o_sb, got = _sb_fwd(qkv, late_weights and late_weights[0], name="sb_fwd")
    if late_weights:
        W = {**W, **late_weights[1](got)}
    hg_out, hg_o, hg_states = _hg_fwd(proj, W["hg_lb_logits"], row(W["hg_norm_g"]), name="hg_fwd")
    mix = jnp.concatenate([o_sb, hg_out], axis=-1)
    h1, u1 = _matmul(mix, W["w_out"], mode="nn", out_dtypes=[F32, MXU_DTYPE], epilogue=residual_norm, tiles=[x],
                     rows=[row(W["norm_ffn_g"][0])], name="out_proj")
    r0 = _matmul(u1, W["w_ff1"][0], mode="nn", out_dtypes=[MXU_DTYPE], epilogue=relu2, name="ff1_0")
    h2, u2 = _matmul(r0, W["w_ff2"][0], mode="nn", out_dtypes=[F32, MXU_DTYPE], epilogue=residual_norm, tiles=[h1],
                     rows=[row(W["norm_mix_g"][1])], name="ff2_0")
    p = _matmul(u2, W["w_glu"], mode="nn", out_dtypes=[F32], epilogue=lambda acc, b: (acc + b,), rows=[row(W["b_glu"])], name="glu_proj")
    w_dw = _pad_rows(W["w_dw"], CONV_HALO)
    ca, cy, cact = _conv_fwd(p, w_dw, row(W["b_dw"]), row(W["ln_g"]), row(W["ln_b"]), name="conv_fwd")
    h3, u3 = _matmul(cact, W["w_pw"], mode="nn", out_dtypes=[F32, MXU_DTYPE], epilogue=lambda acc, res, b, g: residual_norm(acc + b, res, g),
                     tiles=[h2], rows=[row(W["b_pw"]), row(W["norm_ffn_g"][1])], name="pw_proj")
    r1 = _matmul(u3, W["w_ff1"][1], mode="nn", out_dtypes=[MXU_DTYPE], epilogue=relu2, name="ff1_1")
    h4 = _matmul(r1, W["w_ff2"][1], mode="nn", out_dtypes=[F32], epilogue=lambda acc, res: (res + acc,), tiles=[h3], name="ff2_1")

    dh4, dh4_m, G["final_norm_g"], loss = _loss_head(h4, row(W["final_norm_g"]), target, name="loss_head")

    def mlp_bwd(dh, dh_m, h_in, u, r, layer, tag):
        d_relu2 = lambda acc, r_blk: (acc * (2.0 * jnp.sqrt(r_blk.astype(F32))),)
        da = _matmul(dh_m, W["w_ff2"][layer], mode="nt", out_dtypes=[MXU_DTYPE], epilogue=d_relu2, tiles=[r], name="d_ff2_act" + tag)
        dw2 = _matmul(r, dh_m, mode="tn", out_dtypes=[F32], name="d_ff2_w" + tag)
        dw1 = _matmul(u, da, mode="tn", out_dtypes=[F32], name="d_ff1_w" + tag)
        dh_in, dh_in_m, dg, cs = matmul_norm_bwd(da, W["w_ff1"][layer], h_in, W["norm_ffn_g"][layer], dh, name="d_ff1_act" + tag)
        return dh_in, dh_in_m, dg, cs, dw1, dw2

    dh3, dh3_m, dg_ffn1, cs_h3, dw1_1, dw2_1 = mlp_bwd(dh4, dh4_m, h3, u3, r1, 1, "1")
    G["b_pw"] = cs_h3
    dact = _matmul(dh3_m, W["w_pw"], mode="nt", out_dtypes=[F32], name="d_pw_act")
    G["w_pw"] = _matmul(cact, dh3_m, mode="tn", out_dtypes=[F32], name="d_pw_w")
    dy, G["ln_g"], G["ln_b"], G["b_dw"] = _conv_bwd_norm(dact, cy, row(W["ln_g"]), row(W["ln_b"]), name="d_conv_norm")
    dp, G["w_dw"], G["b_glu"] = _conv_bwd_taps(dy, ca, p, w_dw, name="d_conv_taps")
    G["w_glu"] = _matmul(u2, dp, mode="tn", out_dtypes=[F32], name="d_glu_w")
    dh2, dh2_m, dg_mix1, _ = matmul_norm_bwd(dp, W["w_glu"], h2, W["norm_mix_g"][1], dh3, name="d_glu_act")
    dh1, dh1_m, dg_ffn0, _, dw1_0, dw2_0 = mlp_bwd(dh2, dh2_m, h1, u1, r0, 0, "0")
    G["w_ff1"], G["w_ff2"] = jnp.stack([dw1_0, dw1_1]), jnp.stack([dw2_0, dw2_1])
    G["norm_ffn_g"] = jnp.concatenate([dg_ffn0, dg_ffn1], axis=0)
    dmix = _matmul(dh1_m, W["w_out"], mode="nt", out_dtypes=[F32], name="d_out_act")
    G["w_out"] = _matmul(mix, dh1_m, mode="tn", out_dtypes=[F32], name="d_out_w")
    riding = early_grads(G) if early_grads else None
    (dsq, dsk, dsv), got = _sb_bwd(qkv, dmix, riding, name="sb_bwd")
    d_hg, G["hg_lb"], G["hg_norm_g"] = _hg_bwd(proj, hg_o, hg_states, dmix, W["hg_lb_logits"], row(W["hg_norm_g"]), name="hg_bwd")
    dproj = jnp.concatenate([dsq, dsk, dsv, d_hg], axis=-1).astype(MXU_DTYPE)
    G["w_in"] = _matmul(u0, dproj, mode="tn", out_dtypes=[F32], tn=896, name="d_in_w")
    last = last_grads(G) if last_grads else None
    res = matmul_norm_bwd(dproj, W["w_in"], x, W["norm_mix_g"][0], dh1, last, tk=896, name="d_in_act")
    (dx, _, dg_mix0, _), got_last = res if last_grads else (res, [])
    G["norm_mix_g"] = jnp.concatenate([dg_mix0, dg_mix1], axis=0)
    return loss, dx, G, [(riding, got), (last, got_last)]


BIG = (("w_out_ab", "w_out", "rows2d"), ("conv_w_glu", "w_glu", "cols2d"), ("conv_w_pw", "w_pw", "rows2d"),
       ("w_ff1", "w_ff1", "cols3d"), ("w_ff2", "w_ff2", "rows3d"), ("w_in_ab", "w_in", "cols2d"))
LATE = BIG[:-1]
SMALL_SHARDED = ("conv_b_glu", "conv_w_dw", "conv_b_dw", "conv_ln_g", "conv_ln_b", "conv_b_pw")
REPLICATED = ("norm_mix_g", "norm_ffn_g", "hg_lb_logits", "hg_norm_g", "final_norm_g")
ORDER = ("norm_mix_g", "norm_ffn_g", "w_in_ab", "w_out_ab", "hg_lb_logits", "hg_norm_g", "conv_w_glu", "conv_b_glu",
         "conv_w_dw", "conv_b_dw", "conv_ln_g", "conv_ln_b", "conv_w_pw", "conv_b_pw", "w_ff1", "w_ff2", "final_norm_g")


def _step(x, loss_target, w, m, v):
    D = D_MODEL
    x2, t2 = x.reshape(-1, D), loss_target.reshape(-1, D)
    chip = 2 * lax.axis_index("x") + lax.axis_index("y")
    c = lax.axis_index("c")

    small_in = jnp.concatenate([w["conv_b_glu"].reshape(2, SHARD), w["conv_w_dw"].reshape(CONV_WIDTH, SHARD)] +
                               [w[n].reshape(1, SHARD) for n in ("conv_b_dw", "conv_ln_g", "conv_ln_b", "conv_b_pw")], axis=0)
    g_in, gs = _gather_chips([w["w_in_ab"].astype(MXU_DTYPE), _pad_rows(small_in, SMALL_IN_ROWS)]).run("gather_first_weights")
    vec = lambda r0, r1: gs[:, r0:r1].transpose(1, 0, 2).reshape(r1 - r0, D)
    W = {
        "w_in": _ChipWeight(g_in[:, 0], "cols"),
        "b_glu": gs[:, 0:2].reshape(2 * D), "w_dw": vec(2, 33), "b_dw": vec(33, 34)[0], "ln_g": vec(34, 35)[0],
        "ln_b": vec(35, 36)[0], "b_pw": vec(36, 37)[0],
        "norm_mix_g": w["norm_mix_g"], "norm_ffn_g": w["norm_ffn_g"], "hg_lb_logits": w["hg_lb_logits"],
        "hg_norm_g": w["hg_norm_g"], "final_norm_g": w["final_norm_g"],
    }
    late = _gather_chips([w[n].astype(MXU_DTYPE) for n, _, _ in LATE])

    def assemble(got):
        gw = dict(zip([s for _, s, _ in LATE], late.finish(got)))
        layers = lambda g, along: [_ChipWeight(g, along, (layer,)) for layer in range(2)]
        return {"w_out": gw["w_out"].reshape(D, D), "w_glu": _ChipWeight(gw["w_glu"][:, 0], "cols"), "w_pw": gw["w_pw"].reshape(D, D),
                "w_ff1": layers(gw["w_ff1"], "cols"), "w_ff2": layers(gw["w_ff2"], "rows")}

    def early_grads(G):
        return _owner_scatter(jnp.concatenate([_grad_blocks(G[s], kind) for _, s, kind in LATE], axis=2))

    def last_grads(G):
        blocks = _grad_blocks(G["w_in"], "cols2d")
        from_pair = _pair_swap(blocks.transpose(1, 0, 2, 3)).run("grads_pair_swap_in")
        return _chip_scatter(_add_pair(lax.dynamic_index_in_dim(blocks, c, axis=1, keepdims=False), from_pair, name="grads_pair_add_in"))

    loss, dx, G, riders = _forward_backward(x2, t2, W, (late, assemble), early_grads, last_grads)
    halves = [_sum_leading(copies.finish(got), name="grads_add_" + tag) for (copies, got), tag in zip(riders, ("late", "in"))]
    half = jnp.concatenate(halves, axis=0)
    full = _pair_gather(half).run("grads_pair_gather")

    pack = jnp.concatenate([
        G["norm_mix_g"], G["norm_ffn_g"], G["final_norm_g"], jnp.concatenate([G["hg_norm_g"], G["hg_lb"]], axis=1),
        _pad_rows(jnp.broadcast_to(loss, (1, D)), 2), G["b_glu"].reshape(2, D), G["w_dw"], G["b_dw"], G["ln_g"], G["ln_b"], G["b_pw"],
    ], axis=0)
    pack = _pad_rows(pack, SMALL_ROWS)
    (packs,) = _gather_all([pack]).run("gather_small_grads")
    ssum, d_logits = _small_reduce(packs, w["hg_lb_logits"], name="reduce_small_grads")
    cut = lambda r0, r1: lax.dynamic_slice(ssum, (r0, chip * SHARD), (r1 - r0, SHARD))
    grads = {
        "norm_mix_g": ssum[0:2], "norm_ffn_g": ssum[2:4], "final_norm_g": ssum[4], "hg_norm_g": ssum[5, :HG_WIDTH].reshape(1, HG_HEADS, HG_DH),
        "hg_lb_logits": d_logits,
        "conv_b_glu": lax.dynamic_slice(ssum[8:10].reshape(1, 2 * D), (0, chip * 2 * SHARD), (1, 2 * SHARD)),
        "conv_w_dw": cut(10, 10 + CONV_WIDTH).reshape(1, CONV_WIDTH, SHARD),
        "conv_b_dw": cut(42, 43), "conv_ln_g": cut(43, 44), "conv_ln_b": cut(44, 45), "conv_b_pw": cut(45, 46),
    }
    loss_out = ssum[6, 0]

    off = 0
    for n, s, kind in BIG:
        shard = w[n].shape
        rows = w[n].size // (2 * D)
        grads[n] = full[:, off:off + rows].reshape(shard)
        off += rows

    delta, new_m, new_v = {}, {}, {}
    for n, _, _ in BIG:
        view = lambda a: a.reshape(-1, a.shape[-1])
        outs = _adamw(view(w[n]), view(grads[n]), view(m[n]), view(v[n]), name="adamw_" + n)
        delta[n], new_m[n], new_v[n] = (o.reshape(w[n].shape) for o in outs)
    small = SMALL_SHARDED + REPLICATED
    sizes = [w[n].size for n in small]
    total = sum(sizes)
    rows = -(-total // (8 * D)) * 8
    packed = lambda d: _pad_rows(jnp.concatenate([d[n].reshape(-1) for n in small]).reshape(-1, 128), rows * 8).reshape(rows, D)
    outs = _adamw(packed(w), packed(grads), packed(m), packed(v), name="adamw_small")
    off = 0
    for n, size in zip(small, sizes):
        delta[n], new_m[n], new_v[n] = (o.reshape(-1)[off:off + size].reshape(w[n].shape) for o in outs)
        off += size
    grads = {n: grads[n].reshape(w[n].shape) for n in ORDER}
    return (loss_out, dx.reshape(x.shape), *[grads[n] for n in ORDER], *[delta[n] for n in ORDER],
            *[new_m[n] for n in ORDER], *[new_v[n] for n in ORDER])


def kernel(x, norm_mix_g, norm_ffn_g, w_in_ab, w_out_ab, hg_lb_logits, hg_norm_g, conv_w_glu, conv_b_glu, conv_w_dw, conv_b_dw, conv_ln_g, conv_ln_b, conv_w_pw, conv_b_pw, w_ff1, w_ff2, final_norm_g, loss_target, m_norm_mix_g, m_norm_ffn_g, m_w_in_ab, m_w_out_ab, m_hg_lb_logits, m_hg_norm_g, m_conv_w_glu, m_conv_b_glu, m_conv_w_dw, m_conv_b_dw, m_conv_ln_g, m_conv_ln_b, m_conv_w_pw, m_conv_b_pw, m_w_ff1, m_w_ff2, m_final_norm_g, v_norm_mix_g, v_norm_ffn_g, v_w_in_ab, v_w_out_ab, v_hg_lb_logits, v_hg_norm_g, v_conv_w_glu, v_conv_b_glu, v_conv_w_dw, v_conv_b_dw, v_conv_ln_g, v_conv_ln_b, v_conv_w_pw, v_conv_b_pw, v_w_ff1, v_w_ff2, v_final_norm_g):
    args = locals()
    w = {n: args[n] for n in ORDER}
    m = {n: args["m_" + n] for n in ORDER}
    v = {n: args["v_" + n] for n in ORDER}
    return _step(x, loss_target, w, m, v)
```

```python
import functools

import jax
import jax.numpy as jnp
from jax import lax
from jax.experimental import pallas as pl
from jax.experimental.pallas import tpu as pltpu

F32 = jnp.float32
MXU_DTYPE = jnp.bfloat16
MESH = pl.DeviceIdType.MESH

D_MODEL = 1024
SB_HEADS, SB_DH, SB_WIDTH = 8, 64, 512
SB_KEYS = 512
SB_SUB = 256
SB_ROWS_FWD, SB_ROWS_BWD = 512, 256
HG_HEADS, HG_DH, HG_WIDTH = 4, 128, 512
HG_CHUNK = 16
HG_TOKENS = 256
CONV_WIDTH = 31
CONV_HALO = 32
CONV_ROWS = 32
RMS_EPS = 1e-6
LN_EPS = 1e-5
N_CHIPS = 4
N_DEV = 8
SHARD = D_MODEL // N_CHIPS
SMALL_IN_ROWS = 40
SMALL_ROWS = 48
WGRAD_TOKENS = 2048
VMEM_LIMIT = 56 * 1024 * 1024

ADAM_LR, ADAM_B1, ADAM_B2, ADAM_EPS, ADAM_WD, ADAM_STEP = 0.001, 0.9, 0.999, 1e-08, 0.01, 10


def _params(*sem):
    return pltpu.CompilerParams(dimension_semantics=sem, vmem_limit_bytes=VMEM_LIMIT)


def _mx(v):
    return v.astype(MXU_DTYPE)


def _dot(a, b):
    return jnp.dot(_mx(a), _mx(b), preferred_element_type=F32)


def _dot_nt(a, b):
    return lax.dot_general(_mx(a), _mx(b), (((1,), (1,)), ((), ())), preferred_element_type=F32)


def _dot_tn(a, b):
    return lax.dot_general(_mx(a), _mx(b), (((0,), (0,)), ((), ())), preferred_element_type=F32)


def _neg_abs(x):
    bits = lax.bitcast_convert_type(x, jnp.uint32) | jnp.uint32(0x80000000)
    return lax.bitcast_convert_type(bits, F32)


def _key_order_sums(v, tri2, later):
    hi = _mx(v)
    lo = _mx(v - hi.astype(F32))
    n = v.shape[1] // SB_SUB
    blocks = [slice(b * SB_SUB, (b + 1) * SB_SUB) for b in range(n)]
    totals = [jnp.sum(v[:, sl], axis=1, keepdims=True) for sl in blocks]
    far, running = [None] * n, None
    for b in (reversed(range(n)) if later else range(n)):
        far[b] = running
        running = totals[b] if running is None else running + totals[b]
    sums = []
    for b, sl in enumerate(blocks):
        inside = jnp.dot(jnp.concatenate([hi[:, sl], lo[:, sl]], axis=1), tri2, preferred_element_type=F32)
        sums.append(inside if far[b] is None else inside + far[b])
    return jnp.concatenate(sums, axis=1), running


class _ChipWeight:
    def __init__(self, parts, along, lead=()):
        self.parts, self.along, self.lead = parts, along, tuple(lead)
        r, c = parts.shape[-2:]
        self.shape = (r, N_CHIPS * c) if along == "cols" else (N_CHIPS * r, c)

    def _gathered_is_n(self, mode):
        return (self.along == "cols") == (mode in ("nn", "tn"))

    def tile(self, mode, tn, tk):
        r, c = self.parts.shape[-2:]
        part = c if self.along == "cols" else r
        return (part, tk) if self._gathered_is_n(mode) else (tn, part)

    def spec(self, mode, tn, tk):
        squeezed = (None,) * (1 + len(self.lead))
        lead, cols, by_n = self.lead, self.along == "cols", self._gathered_is_n(mode)
        block = (tn, tk) if mode == "nt" else (tk, tn)

        def index(i, j, k):
            chip, other = (j, k) if by_n else (k, j)
            return (chip,) + lead + ((other, 0) if cols else (0, other))

        return pl.BlockSpec(squeezed + block, index)


def _matmul(a, b, *, mode, out_dtypes, epilogue=None, tiles=(), rows=(), n_sums=0, rider=None, tm=1024, tn=1024, tk=1024, name):
    b_shape = b.shape
    if mode == "nn":
        (M, K), N = a.shape, b_shape[1]
    elif mode == "nt":
        (M, K), N = a.shape, b_shape[0]
    else:
        (K, M), N = a.shape, b_shape[1]
    if isinstance(b, _ChipWeight):
        tn, tk = b.tile(mode, tn, tk)
    tm, tn, tk = min(tm, M), min(tn, N), min(tk, K)
    assert M % tm == 0 and N % tn == 0 and K % tk == 0, (name, M, N, K)
    nk = K // tk
    a_spec = pl.BlockSpec((tk, tm), lambda i, j, k: (k, i)) if mode == "tn" else pl.BlockSpec((tm, tk), lambda i, j, k: (i, k))
    if isinstance(b, _ChipWeight):
        b_spec, b = b.spec(mode, tn, tk), b.parts
    else:
        b_spec = pl.BlockSpec((tn, tk), lambda i, j, k: (j, k)) if mode == "nt" else pl.BlockSpec((tk, tn), lambda i, j, k: (k, j))
    dims = {"nn": ((1,), (0,)), "nt": ((1,), (1,)), "tn": ((0,), (0,))}[mode]
    n_t, n_r, n_o = len(tiles), len(rows), len(out_dtypes)
    n_x = 0 if rider is None else len(rider.arrays)
    grid = (M // tm, N // tn, nk)
    if epilogue is None:
        epilogue = lambda acc: (acc,)

    def body(a_ref, b_ref, *rest):
        extra, x_in, rest = rest[:n_t + n_r], rest[n_t + n_r:n_t + n_r + n_x], rest[n_t + n_r + n_x:]
        outs, sums, x_out, acc_ref, sems = rest[:n_o], rest[n_o:n_o + n_sums], rest[n_o + n_sums:n_o + n_sums + n_x], rest[n_o + n_sums + n_x], rest[n_o + n_sums + n_x + 1:]
        i, j, k = (pl.program_id(d) for d in range(3))
        if rider is not None:
            @pl.when((i == 0) & (j == 0) & (k == 0))
            def _():
                for cp in rider.make(x_in, x_out, *sems):
                    cp.start()

        @pl.when(k == 0)
        def _():
            acc_ref[...] = jnp.zeros_like(acc_ref)

        acc_ref[...] += lax.dot_general(_mx(a_ref[...]), _mx(b_ref[...]), (dims, ((), ())), preferred_element_type=F32)

        @pl.when(k == nk - 1)
        def _():
            res = epilogue(acc_ref[...], *[e[...] for e in extra])
            for o_ref, r in zip(outs, res[:n_o]):
                o_ref[...] = r.astype(o_ref.dtype)
            for s_ref, r in zip(sums, res[n_o:]):
                @pl.when(i == 0)
                def _():
                    s_ref[...] = jnp.zeros_like(s_ref)

                s_ref[...] += r

        if rider is not None:
            @pl.when((i == grid[0] - 1) & (j == grid[1] - 1) & (k == grid[2] - 1))
            def _():
                for cp in rider.make(x_in, x_out, *sems):
                    cp.wait()

    tile_spec = pl.BlockSpec((tm, tn), lambda i, j, k: (i, j))
    row_spec = pl.BlockSpec((1, tn), lambda i, j, k: (0, j))
    ordered = n_sums > 0 or rider is not None
    outs = pl.pallas_call(
        body, grid=grid,
        in_specs=[a_spec, b_spec] + [tile_spec] * n_t + [row_spec] * n_r + [HBM_SPEC] * n_x,
        out_specs=[tile_spec] * n_o + [row_spec] * n_sums + [HBM_SPEC] * n_x,
        out_shape=[jax.ShapeDtypeStruct((M, N), dt) for dt in out_dtypes] + [jax.ShapeDtypeStruct((1, N), F32)] * n_sums
        + ([] if rider is None else rider.out_shapes),
        scratch_shapes=[pltpu.VMEM((tm, tn), F32)] + ([] if rider is None else rider.scratch()),
        compiler_params=_params(*(("arbitrary",) * 3 if ordered else ("parallel", "parallel", "arbitrary"))), name=name,
    )(a, b, *tiles, *rows, *([] if rider is None else rider.arrays))
    res = outs[0] if n_o + n_sums == 1 else outs[:n_o + n_sums]
    return res if rider is None else (res, outs[n_o + n_sums:])


def _token_block(T):
    return min(512, T)


def _rmsnorm_fwd(h, g, *, name):
    T, D = h.shape
    tb = _token_block(T)

    def body(h_ref, g_ref, u_ref):
        x = h_ref[...]
        r = lax.rsqrt(jnp.mean(x * x, axis=-1, keepdims=True) + RMS_EPS)
        u_ref[...] = (x * r * g_ref[...]).astype(u_ref.dtype)

    blk = pl.BlockSpec((tb, D), lambda i: (i, 0))
    return pl.pallas_call(
        body, grid=(T // tb,), in_specs=[blk, pl.BlockSpec((1, D), lambda i: (0, 0))], out_specs=blk,
        out_shape=jax.ShapeDtypeStruct((T, D), MXU_DTYPE), compiler_params=_params("parallel"), name=name,
    )(h, g)


def _rms_bwd_math(x, g, du):
    r = lax.rsqrt(jnp.mean(x * x, axis=-1, keepdims=True) + RMS_EPS)
    gd = g * du
    dx = r * gd - x * (r * r * r) * jnp.mean(gd * x, axis=-1, keepdims=True)
    return dx, du * x * r


def _loss_head(h, g, target, *, name):
    T, D = h.shape
    tb = _token_block(T)

    def body(h_ref, g_ref, t_ref, dh_ref, dhm_ref, dg_ref, loss_ref):
        @pl.when(pl.program_id(0) == 0)
        def _():
            dg_ref[...] = jnp.zeros_like(dg_ref)
            loss_ref[...] = jnp.zeros_like(loss_ref)

        x, gg = h_ref[...], g_ref[...]
        r = lax.rsqrt(jnp.mean(x * x, axis=-1, keepdims=True) + RMS_EPS)
        diff = x * r * gg - t_ref[...]
        per_token = jnp.mean(diff * diff, axis=-1, keepdims=True)
        loss_ref[...] += 0.5 * jnp.sum(per_token, axis=0, keepdims=True)
        dx, dg_terms = _rms_bwd_math(x, gg, diff / D)
        dh_ref[...] = dx
        dhm_ref[...] = dx.astype(dhm_ref.dtype)
        dg_ref[...] += jnp.sum(dg_terms, axis=0, keepdims=True)

    blk = pl.BlockSpec((tb, D), lambda i: (i, 0))
    row = pl.BlockSpec((1, D), lambda i: (0, 0))
    return pl.pallas_call(
        body, grid=(T // tb,), in_specs=[blk, row, blk], out_specs=[blk, blk, row, pl.BlockSpec((1, 1), lambda i: (0, 0))],
        out_shape=[jax.ShapeDtypeStruct((T, D), F32), jax.ShapeDtypeStruct((T, D), MXU_DTYPE), jax.ShapeDtypeStruct((1, D), F32),
                   jax.ShapeDtypeStruct((1, 1), F32)],
        compiler_params=_params("arbitrary"), name=name,
    )(h, g, target)


def _sb_scores(qm, ks, later, tri, mask, need_log_beta=True):
    z = _dot_nt(qm, ks)
    sp = jnp.maximum(z, 0.0) + jnp.log(1.0 + jnp.exp(_neg_abs(z)))
    lb = z - sp if need_log_beta else None
    if mask is not None:
        sp = jnp.where(mask, sp, 0.0)
    after, total = _key_order_sums(sp, tri, later=True)
    w = jnp.exp((lb if need_log_beta else z) - (after + later))
    if mask is not None:
        w = jnp.where(mask, w, 0.0)
    return total, lb, w


def _sb_setup(q_ref, rows, inclusive=False):
    i, hsel = pl.program_id(1), pl.program_id(2)
    lane = lax.broadcasted_iota(jnp.int32, (rows, 2 * SB_DH), 1)
    mine = (lane >= SB_DH) == (hsel == 1)
    diag = (i * rows) // SB_KEYS
    t = i * rows + lax.broadcasted_iota(jnp.int32, (rows, SB_KEYS), 0)
    s = diag * SB_KEYS + lax.broadcasted_iota(jnp.int32, (rows, SB_KEYS), 1)
    a = lax.broadcasted_iota(jnp.int32, (2 * SB_SUB, SB_SUB), 0) % SB_SUB
    b = lax.broadcasted_iota(jnp.int32, (2 * SB_SUB, SB_SUB), 1)
    return i, hsel, mine, diag, s < t, _mx(a >= b if inclusive else a > b), _mx(a < b)


def _sb_keys(j, n=1):
    return pl.ds(pl.multiple_of(j * SB_KEYS, SB_KEYS), n * SB_KEYS)


def _sb_descend(n, step, carry, wide):
    pair = (lambda j, cr: step(j, 2, cr)) if wide else (lambda j, cr: step(j, 1, step(j + 1, 1, cr)))
    carry = lax.fori_loop(0, n // 2, lambda it, cr: pair(n - 2 - 2 * it, cr), carry)
    return lax.cond(n % 2 == 1, lambda cr: step(0, 1, cr), lambda cr: cr, carry)


def _sb_ascend(n, step, carry):
    odd = n % 2
    carry = lax.cond(odd == 1, lambda cr: step(0, 1, cr), lambda cr: cr, carry)
    return lax.fori_loop(0, n // 2, lambda it, cr: step(odd + 2 * it, 2, cr), carry)


def _sb_call(body, qkv, extra_in, out_blocks, out_dtype, scratch, rider, rows, *, name):
    T = qkv.shape[0]
    n_pairs = SB_HEADS // 2
    grid = (n_pairs, T // rows, 2)
    pair = lambda col0: pl.BlockSpec((rows, 2 * SB_DH), lambda p, i, h: (i, col0 + p))
    whole = lambda col0: pl.BlockSpec((T, 2 * SB_DH), lambda p, i, h: (0, col0 + p))
    in_specs = [pair(0), whole(n_pairs), whole(2 * n_pairs)] + [pair(0)] * len(extra_in)
    out_specs = [pair(0) if kind == "pair" else whole(0) for kind in out_blocks]
    n_in, n_out, n_r = len(in_specs), len(out_specs), 0 if rider is None else len(rider.arrays)

    def kernel_body(*refs):
        ins, r_in = refs[:n_in], refs[n_in:n_in + n_r]
        outs, r_out = refs[n_in + n_r:n_in + n_r + n_out], refs[n_in + n_r + n_out:n_in + 2 * n_r + n_out]
        rest = refs[n_in + 2 * n_r + n_out:]
        ids = [pl.program_id(a) for a in range(3)]
        if rider is not None:
            @pl.when((ids[0] == 0) & (ids[1] == 0) & (ids[2] == 0))
            def _():
                for cp in rider.make(r_in, r_out, *rest[len(scratch):]):
                    cp.start()

        body(ins, outs, rest[:len(scratch)])
        if rider is not None:
            @pl.when((ids[0] == grid[0] - 1) & (ids[1] == grid[1] - 1) & (ids[2] == grid[2] - 1))
            def _():
                for cp in rider.make(r_in, r_out, *rest[len(scratch):]):
                    cp.wait()

    res = pl.pallas_call(
        kernel_body, grid=grid, in_specs=in_specs + [HBM_SPEC] * n_r, out_specs=out_specs + [HBM_SPEC] * n_r,
        out_shape=[jax.ShapeDtypeStruct((T, SB_WIDTH), out_dtype)] * n_out + ([] if rider is None else rider.out_shapes),
        scratch_shapes=list(scratch) + ([] if rider is None else rider.scratch()),
        compiler_params=_params("arbitrary", "arbitrary", "arbitrary"), name=name,
    )(qkv, qkv, qkv, *extra_in, *([] if rider is None else rider.arrays))
    return res[:n_out], res[n_out:]


def _sb_fwd(qkv, rider=None, *, name):
    rows = min(SB_ROWS_FWD, qkv.shape[0])
    scale = SB_DH ** -0.5

    def body(ins, outs, _):
        (q_ref, k_ref, v_ref), (o_ref,) = ins, outs
        i, hsel, mine, diag, mask, tri, _ = _sb_setup(q_ref, rows, inclusive=True)
        qm = jnp.where(mine, q_ref[...], 0) * scale

        def tile(j, n, m, later, acc):
            total, _, w = _sb_scores(qm, k_ref[_sb_keys(j, n), :], later, tri, m, need_log_beta=False)
            return later + total, acc + _dot(w, v_ref[_sb_keys(j, n), :])

        carry = tile(diag, 1, mask, jnp.zeros((rows, 1), F32), jnp.zeros((rows, 2 * SB_DH), F32))
        _, acc = _sb_descend(diag, lambda j, n, cr: tile(j, n, None, *cr), carry, wide=False)
        res = jnp.where(mine, acc, 0.0).astype(o_ref.dtype)

        @pl.when(hsel == 0)
        def _():
            o_ref[...] = res

        @pl.when(hsel == 1)
        def _():
            o_ref[...] += res

    (o,), got = _sb_call(body, qkv, [], ["pair"], MXU_DTYPE, [], rider, rows, name=name)
    return o, got


def _sb_bwd(qkv, dmix, rider=None, *, name):
    T = qkv.shape[0]
    rows = min(SB_ROWS_BWD, T)
    scale = SB_DH ** -0.5

    def body(ins, outs, scratch):
        (q_ref, k_ref, v_ref, do_ref), (dq_ref, dk_ref, dv_ref), (da_ref, beta_ref) = ins, outs, scratch
        i, hsel, mine, diag, mask, tri, tri_before = _sb_setup(q_ref, rows)

        @pl.when((i == 0) & (hsel == 0))
        def _():
            dk_ref[...] = jnp.zeros_like(dk_ref)
            dv_ref[...] = jnp.zeros_like(dv_ref)

        qm = jnp.where(mine, q_ref[...], 0) * scale
        do_m = _mx(jnp.where(mine, do_ref[...], 0.0))

        def weights(j, n, m, later):
            total, lb, w = _sb_scores(qm, k_ref[_sb_keys(j, n), :], later, tri, m)
            da, beta = _dot_nt(do_m, v_ref[_sb_keys(j, n), :]) * w, jnp.exp(lb)
            for t in range(n):
                da_ref[j + t] = da[:, t * SB_KEYS:(t + 1) * SB_KEYS]
                beta_ref[j + t] = beta[:, t * SB_KEYS:(t + 1) * SB_KEYS]
            dv_ref[_sb_keys(j, n), :] += _dot_tn(w, do_m)
            return later + total

        later = weights(diag, 1, mask, jnp.zeros((rows, 1), F32))
        _sb_descend(diag, lambda j, n, c: weights(j, n, None, c), later, wide=True)

        def logits(j, n, m, before, dq):
            da = jnp.concatenate([da_ref[j + t] for t in range(n)], axis=1)
            beta = jnp.concatenate([beta_ref[j + t] for t in range(n)], axis=1)
            earlier, total = _key_order_sums(da, tri_before, later=False)
            dz = da - beta * (da + earlier + before)
            if m is not None:
                dz = jnp.where(m, dz, 0.0)
            dz = _mx(dz)
            dk_ref[_sb_keys(j, n), :] += _dot_tn(dz, qm)
            return before + total, dq + _dot(dz, k_ref[_sb_keys(j, n), :])

        carry = (jnp.zeros((rows, 1), F32), jnp.zeros((rows, 2 * SB_DH), F32))
        carry = _sb_ascend(diag, lambda j, n, cr: logits(j, n, None, *cr), carry)
        res = jnp.where(mine, logits(diag, 1, mask, *carry)[1] * scale, 0.0)

        @pl.when(hsel == 0)
        def _():
            dq_ref[...] = res

        @pl.when(hsel == 1)
        def _():
            dq_ref[...] += res

    n_tiles = T // SB_KEYS
    scratch = [pltpu.VMEM((n_tiles, rows, SB_KEYS), F32), pltpu.VMEM((n_tiles, rows, SB_KEYS), F32)]
    return _sb_call(body, qkv, [dmix], ["pair", "whole", "whole"], F32, scratch, rider, rows, name=name)


def _chunk_row(n):
    return lax.broadcasted_iota(jnp.int32, (n, HG_DH), 0) % HG_CHUNK


def _chunk_cumsum(x, row, reverse=False):
    n = x.shape[0]
    for sh in (1, 2, 4, 8):
        if reverse:
            x = x + jnp.where(row < HG_CHUNK - sh, pltpu.roll(x, n - sh, 0), 0.0)
        else:
            x = x + jnp.where(row >= sh, pltpu.roll(x, sh, 0), 0.0)
    return x


def _hg_lower_bound(logits_ref):
    lg = logits_ref[...]
    e = jnp.exp(lg - jnp.max(lg, axis=0, keepdims=True))
    return e[0:1, :] / jnp.sum(e, axis=0, keepdims=True)


def _hg_terms(fr, q, lb, row):
    sig = jax.nn.sigmoid(fr)
    f = lb + (1.0 - lb) * sig
    kk = 1.0 - f
    g = jnp.log(f)
    G = _chunk_cumsum(g, row)
    g_last = G + (_chunk_cumsum(g, row, reverse=True) - g)
    e_g, e_ng, e_lg = jnp.exp(G), jnp.exp(-G), jnp.exp(g_last - G)
    return dict(sig=sig, f=f, kk=kk, e_g=e_g, e_ng=e_ng, e_lg=e_lg, q_dec=q * e_g, k_intra=kk * e_ng,
                k_state=kk * e_lg, decay=jnp.exp(g_last))


def _hg_causal(n):
    t = lax.broadcasted_iota(jnp.int32, (n, n), 0)
    s = lax.broadcasted_iota(jnp.int32, (n, n), 1)
    return (s <= t) & (s // HG_CHUNK == t // HG_CHUNK)


def _chunks(a):
    return a.reshape(a.shape[0] // HG_CHUNK, HG_CHUNK, a.shape[1])


def _per_chunk(lhs, rhs, contract):
    return lax.dot_general(_mx(lhs), _mx(rhs), ((contract[:1], contract[1:]), ((0,), (0,))), preferred_element_type=F32)


def _hg_specs(T, tb, col0, order):
    return [pl.BlockSpec((tb, HG_WIDTH), functools.partial(lambda i, j: (order(i), j), j=col0 + j)) for j in range(4)]


def _hg_fwd(proj, logits, norm_g, *, name):
    T = proj.shape[0]
    tb = min(HG_TOKENS, T)
    nch = tb // HG_CHUNK

    def body(q_ref, f_ref, i_ref, gate_ref, lg_ref, ng_ref, out_ref, o_ref, s_ref, st_ref, inc_ref, dec_ref):
        @pl.when(pl.program_id(0) == 0)
        def _():
            st_ref[...] = jnp.zeros_like(st_ref)

        lb_all = _hg_lower_bound(lg_ref)
        row = _chunk_row(tb)
        causal = _hg_causal(tb)
        for hh in range(HG_HEADS):
            cols = slice(hh * HG_DH, (hh + 1) * HG_DH)
            t = _hg_terms(f_ref[:, cols], q_ref[:, cols], lb_all[:, cols], row)
            v = i_ref[:, cols]
            scores = jnp.where(causal, _dot_nt(t["q_dec"], t["k_intra"]), 0.0)
            o_intra = _dot(scores, v)
            inc_ref[...] = _per_chunk(_chunks(v), _chunks(t["k_state"]), (1, 1))
            dec_ref[...] = _chunks(t["decay"])

            def step(ci, st):
                s_ref[ci, hh] = st
                return st * dec_ref[ci][0:1, :] + inc_ref[ci]

            st_ref[hh] = lax.fori_loop(0, nch, step, st_ref[hh], unroll=4)
            o_inter = _per_chunk(_chunks(t["q_dec"]), s_ref[:, hh], (2, 2))
            o = o_intra + o_inter.reshape(tb, HG_DH)
            o_ref[:, cols] = o
            gate = gate_ref[:, cols]
            on = o * lax.rsqrt(jnp.mean(o * o, axis=-1, keepdims=True) + RMS_EPS) * ng_ref[:, cols]
            out_ref[:, cols] = (on * (gate * jax.nn.sigmoid(gate))).astype(out_ref.dtype)

    blk = pl.BlockSpec((tb, HG_WIDTH), lambda i: (i, 0))
    return pl.pallas_call(
        body, grid=(T // tb,),
        in_specs=_hg_specs(T, tb, 3, lambda i: i) + [pl.BlockSpec((3, HG_WIDTH), lambda i: (0, 0)), pl.BlockSpec((1, HG_WIDTH), lambda i: (0, 0))],
        out_specs=[blk, blk, pl.BlockSpec((nch, HG_HEADS, HG_DH, HG_DH), lambda i: (i, 0, 0, 0))],
        out_shape=[jax.ShapeDtypeStruct((T, HG_WIDTH), MXU_DTYPE), jax.ShapeDtypeStruct((T, HG_WIDTH), F32),
                   jax.ShapeDtypeStruct((T // HG_CHUNK, HG_HEADS, HG_DH, HG_DH), F32)],
        scratch_shapes=[pltpu.VMEM((HG_HEADS, HG_DH, HG_DH), F32), pltpu.VMEM((nch, HG_DH, HG_DH), F32),
                        pltpu.VMEM((nch, HG_CHUNK, HG_DH), F32)],
        compiler_params=_params("arbitrary"), name=name,
    )(proj, proj, proj, proj, logits, norm_g)


def _hg_bwd(proj, o_raw, states, dmix, logits, norm_g, *, name):
    T = proj.shape[0]
    tb = min(HG_TOKENS, T)
    nch = tb // HG_CHUNK
    nb = T // tb
    rev = lambda i: nb - 1 - i

    def body(q_ref, f_ref, i_ref, gate_ref, o_ref, s_ref, dout_ref, lg_ref, ng_ref, dp_ref, dlb_ref, dng_ref,
             dst_ref, inc_ref, dec_ref, after_ref):
        @pl.when(pl.program_id(0) == 0)
        def _():
            dst_ref[...] = jnp.zeros_like(dst_ref)
            dlb_ref[...] = jnp.zeros_like(dlb_ref)
            dng_ref[...] = jnp.zeros_like(dng_ref)

        lb_all = _hg_lower_bound(lg_ref)
        row = _chunk_row(tb)
        causal = _hg_causal(tb)
        for hh in range(HG_HEADS):
            cols = slice(hh * HG_DH, (hh + 1) * HG_DH)
            out_cols = lambda part: slice(part * HG_WIDTH + hh * HG_DH, part * HG_WIDTH + (hh + 1) * HG_DH)
            o, gate, dout, ng, lb = o_ref[:, cols], gate_ref[:, cols], dout_ref[:, cols], ng_ref[:, cols], lb_all[:, cols]
            sg = jax.nn.sigmoid(gate)
            r = lax.rsqrt(jnp.mean(o * o, axis=-1, keepdims=True) + RMS_EPS)
            oh = o * r
            dp_ref[:, out_cols(3)] = dout * (oh * ng) * (sg * (1.0 + gate * (1.0 - sg)))
            don = dout * (gate * sg)
            dng_ref[:, cols] += jnp.sum(don * oh, axis=0, keepdims=True)
            doh = don * ng
            do = r * (doh - oh * jnp.mean(doh * oh, axis=-1, keepdims=True))

            t = _hg_terms(f_ref[:, cols], q_ref[:, cols], lb, row)
            v = i_ref[:, cols]
            scores = jnp.where(causal, _dot_nt(t["q_dec"], t["k_intra"]), 0.0)
            dscores = jnp.where(causal, _dot_nt(do, v), 0.0)
            inc_ref[...] = _per_chunk(_chunks(do), _chunks(t["q_dec"]), (1, 1))
            dec_ref[...] = _chunks(t["decay"])

            def step(it, dst):
                ci = nch - 1 - it
                after_ref[ci] = dst
                return dst * dec_ref[ci][0:1, :] + inc_ref[ci]

            dst_ref[hh] = lax.fori_loop(0, nch, step, dst_ref[hh], unroll=4)
            st, dst = s_ref[:, hh], after_ref[...]
            dqd = _dot(dscores, t["k_intra"]) + _per_chunk(_chunks(do), st, (2, 1)).reshape(tb, HG_DH)
            dki = _dot_tn(dscores, t["q_dec"])
            dks = _per_chunk(_chunks(v), dst, (2, 1)).reshape(tb, HG_DH)
            dp_ref[:, out_cols(2)] = _dot_tn(scores, do) + _per_chunk(_chunks(t["k_state"]), dst, (2, 2)).reshape(tb, HG_DH)
            ddecay = jnp.broadcast_to(jnp.sum(st * dst, axis=1, keepdims=True), (nch, HG_CHUNK, HG_DH)).reshape(tb, HG_DH)
            dks_ks = dks * t["k_state"]
            d_glast = _chunk_cumsum(dks_ks, row) + ddecay * t["decay"]
            d_g = dqd * t["q_dec"] - dki * t["k_intra"] - dks_ks + jnp.where(row == HG_CHUNK - 1, d_glast, 0.0)
            df = _chunk_cumsum(d_g, row, reverse=True) / t["f"] - (dki * t["e_ng"] + dks * t["e_lg"])
            dp_ref[:, out_cols(0)] = dqd * t["e_g"]
            dp_ref[:, out_cols(1)] = df * (1.0 - lb) * t["sig"] * (1.0 - t["sig"])
            dlb_ref[:, cols] += jnp.sum(df * (1.0 - t["sig"]), axis=0, keepdims=True)

    blk = pl.BlockSpec((tb, HG_WIDTH), lambda i: (rev(i), 0))
    row_spec = pl.BlockSpec((1, HG_WIDTH), lambda i: (0, 0))
    return pl.pallas_call(
        body, grid=(nb,),
        in_specs=_hg_specs(T, tb, 3, rev) + [
            blk, pl.BlockSpec((nch, HG_HEADS, HG_DH, HG_DH), lambda i: (rev(i), 0, 0, 0)),
            pl.BlockSpec((tb, HG_WIDTH), lambda i: (rev(i), 1)), pl.BlockSpec((3, HG_WIDTH), lambda i: (0, 0)), row_spec],
        out_specs=[pl.BlockSpec((tb, 4 * HG_WIDTH), lambda i: (rev(i), 0)), row_spec, row_spec],
        out_shape=[jax.ShapeDtypeStruct((T, 4 * HG_WIDTH), F32), jax.ShapeDtypeStruct((1, HG_WIDTH), F32), jax.ShapeDtypeStruct((1, HG_WIDTH), F32)],
        scratch_shapes=[pltpu.VMEM((HG_HEADS, HG_DH, HG_DH), F32), pltpu.VMEM((nch, HG_DH, HG_DH), F32),
                        pltpu.VMEM((nch, HG_CHUNK, HG_DH), F32), pltpu.VMEM((nch, HG_DH, HG_DH), F32)],
        compiler_params=_params("arbitrary"), name=name,
    )(proj, proj, proj, proj, o_raw, states, dmix, logits, norm_g)


def _shifted_copies(sh_ref, n_rows):
    keep = n_rows + CONV_HALO - 8
    for b in range(1, 8):
        sh_ref[b, 0:keep, :] = sh_ref[0, b:b + keep, :]


def _tap_rows(sh_ref, offset, r0, lanes):
    start = pl.multiple_of(r0 + (offset - offset % 8), 8)
    return sh_ref[offset % 8, pl.ds(start, CONV_ROWS), lanes]


def _conv_fwd(p, w_dw, b_dw, ln_g, ln_b, *, name):
    T, D = p.shape[0], p.shape[1] // 2
    tb = _token_block(T)
    hpb = tb // CONV_HALO
    lane_step = 512

    def body(p1_ref, p2_ref, q1_ref, q2_ref, w_ref, bdw_ref, g_ref, b_ref, a_ref, y_ref, act_ref, sh_ref):
        i = pl.program_id(0)
        a = p1_ref[...] * jax.nn.sigmoid(p2_ref[...])
        sh_ref[0, 0:CONV_HALO, :] = jnp.where(i > 0, q1_ref[...] * jax.nn.sigmoid(q2_ref[...]), 0.0)
        sh_ref[0, CONV_HALO:, :] = a
        a_ref[...] = a
        _shifted_copies(sh_ref, tb)

        def chunk(ci, _):
            r0 = pl.multiple_of(ci * CONV_ROWS, CONV_ROWS)
            for l0 in range(0, D, lane_step):
                lanes = slice(l0, l0 + lane_step)
                acc = jnp.broadcast_to(bdw_ref[:, lanes], (CONV_ROWS, lane_step))
                for k in range(CONV_WIDTH):
                    acc = acc + _tap_rows(sh_ref, CONV_HALO - CONV_WIDTH + 1 + k, r0, lanes) * w_ref[k:k + 1, lanes]
                y_ref[pl.ds(r0, CONV_ROWS), lanes] = acc
            return 0

        lax.fori_loop(0, tb // CONV_ROWS, chunk, 0)
        y = y_ref[...]
        mu = jnp.mean(y, axis=-1, keepdims=True)
        yc = y - mu
        s = yc * lax.rsqrt(jnp.mean(yc * yc, axis=-1, keepdims=True) + LN_EPS) * g_ref[...] + b_ref[...]
        act_ref[...] = (s * jax.nn.sigmoid(s)).astype(act_ref.dtype)

    prev = lambda i: jnp.maximum(i * hpb - 1, 0)
    blk = pl.BlockSpec((tb, D), lambda i: (i, 0))
    row = pl.BlockSpec((1, D), lambda i: (0, 0))
    return pl.pallas_call(
        body, grid=(T // tb,),
        in_specs=[blk, pl.BlockSpec((tb, D), lambda i: (i, 1)), pl.BlockSpec((CONV_HALO, D), lambda i: (prev(i), 0)),
                  pl.BlockSpec((CONV_HALO, D), lambda i: (prev(i), 1)), pl.BlockSpec((CONV_HALO, D), lambda i: (0, 0)), row, row, row],
        out_specs=[blk, blk, blk],
        out_shape=[jax.ShapeDtypeStruct((T, D), F32), jax.ShapeDtypeStruct((T, D), F32), jax.ShapeDtypeStruct((T, D), MXU_DTYPE)],
        scratch_shapes=[pltpu.VMEM((8, tb + CONV_HALO, D), F32)],
        compiler_params=_params("parallel"), name=name,
    )(p, p, p, p, w_dw, b_dw, ln_g, ln_b)


def _conv_bwd_norm(dact, y, ln_g, ln_b, *, name):
    T, D = y.shape
    tb = _token_block(T)

    def body(da_ref, y_ref, g_ref, b_ref, dy_ref, dg_ref, db_ref, cs_ref):
        @pl.when(pl.program_id(0) == 0)
        def _():
            dg_ref[...] = jnp.zeros_like(dg_ref)
            db_ref[...] = jnp.zeros_like(db_ref)
            cs_ref[...] = jnp.zeros_like(cs_ref)

        y, g = y_ref[...], g_ref[...]
        yc = y - jnp.mean(y, axis=-1, keepdims=True)
        rs = lax.rsqrt(jnp.mean(yc * yc, axis=-1, keepdims=True) + LN_EPS)
        yn = yc * rs
        s = yn * g + b_ref[...]
        sg = jax.nn.sigmoid(s)
        ds = da_ref[...] * (sg * (1.0 + s * (1.0 - sg)))
        dg_ref[...] += jnp.sum(ds * yn, axis=0, keepdims=True)
        db_ref[...] += jnp.sum(ds, axis=0, keepdims=True)
        dyn = ds * g
        dy = rs * (dyn - jnp.mean(dyn, axis=-1, keepdims=True) - yn * jnp.mean(dyn * yn, axis=-1, keepdims=True))
        dy_ref[...] = dy
        cs_ref[...] += jnp.sum(dy, axis=0, keepdims=True)

    blk = pl.BlockSpec((tb, D), lambda i: (i, 0))
    row = pl.BlockSpec((1, D), lambda i: (0, 0))
    rs_ = jax.ShapeDtypeStruct((1, D), F32)
    return pl.pallas_call(
        body, grid=(T // tb,), in_specs=[blk, blk, row, row], out_specs=[blk, row, row, row],
        out_shape=[jax.ShapeDtypeStruct((T, D), F32), rs_, rs_, rs_], compiler_params=_params("arbitrary"), name=name,
    )(dact, y, ln_g, ln_b)


def _conv_bwd_taps(dy, a, p, w_dw, *, name):
    T, D = dy.shape
    tb = _token_block(T)
    hpb = tb // CONV_HALO
    last = T // CONV_HALO - 1
    nb = T // tb
    lane_step = 128
    groups = CONV_ROWS // 8

    def body(dy_ref, dyn_ref, a_ref, p1_ref, p2_ref, w_ref, dp_ref, dw_ref, cs_ref, sh_ref, da_ref):
        i = pl.program_id(0)

        @pl.when(i == 0)
        def _():
            dw_ref[...] = jnp.zeros_like(dw_ref)
            cs_ref[...] = jnp.zeros_like(cs_ref)

        sh_ref[0, 0:tb, :] = dy_ref[...]
        sh_ref[0, tb:, :] = jnp.where(i < nb - 1, dyn_ref[...], 0.0)
        _shifted_copies(sh_ref, tb)
        for l0 in range(0, D, lane_step):
            lanes = slice(l0, l0 + lane_step)

            def chunk(ci, sums):
                r0 = pl.multiple_of(ci * CONV_ROWS, CONV_ROWS)
                a_c = a_ref[pl.ds(r0, CONV_ROWS), lanes]
                da = jnp.zeros((CONV_ROWS, lane_step), F32)
                new = []
                for k in range(CONV_WIDTH):
                    s_k = _tap_rows(sh_ref, CONV_WIDTH - 1 - k, r0, lanes)
                    da = da + s_k * w_ref[k:k + 1, lanes]
                    new.append(sums[k] + jnp.sum((s_k * a_c).reshape(groups, 8, lane_step), axis=0))
                da_ref[pl.ds(r0, CONV_ROWS), lanes] = da
                return tuple(new)

            sums = lax.fori_loop(0, tb // CONV_ROWS, chunk, tuple(jnp.zeros((8, lane_step), F32) for _ in range(CONV_WIDTH)))
            for k in range(CONV_WIDTH):
                dw_ref[k:k + 1, lanes] += jnp.sum(sums[k], axis=0, keepdims=True)
        da = da_ref[...]
        p1 = p1_ref[...]
        sg = jax.nn.sigmoid(p2_ref[...])
        dp1 = da * sg
        dp2 = da * p1 * (sg * (1.0 - sg))
        dp_ref[:, 0:D] = dp1.astype(dp_ref.dtype)
        dp_ref[:, D:] = dp2.astype(dp_ref.dtype)
        cs_ref[:, 0:D] += jnp.sum(dp1, axis=0, keepdims=True)
        cs_ref[:, D:] += jnp.sum(dp2, axis=0, keepdims=True)

    blk = pl.BlockSpec((tb, D), lambda i: (i, 0))
    return pl.pallas_call(
        body, grid=(nb,),
        in_specs=[blk, pl.BlockSpec((CONV_HALO, D), lambda i: (jnp.minimum((i + 1) * hpb, last), 0)), blk, blk,
                  pl.BlockSpec((tb, D), lambda i: (i, 1)), pl.BlockSpec((CONV_HALO, D), lambda i: (0, 0))],
        out_specs=[pl.BlockSpec((tb, 2 * D), lambda i: (i, 0)), pl.BlockSpec((CONV_HALO, D), lambda i: (0, 0)), pl.BlockSpec((1, 2 * D), lambda i: (0, 0))],
        out_shape=[jax.ShapeDtypeStruct((T, 2 * D), MXU_DTYPE), jax.ShapeDtypeStruct((CONV_HALO, D), F32), jax.ShapeDtypeStruct((1, 2 * D), F32)],
        scratch_shapes=[pltpu.VMEM((8, tb + CONV_HALO, D), F32), pltpu.VMEM((tb, D), F32)],
        compiler_params=_params("arbitrary"), name=name,
    )(dy, dy, a, p, p, w_dw)


def _row_block(rows):
    for tr in (512, 256, 128, 64, 32, 16, 8):
        if rows % tr == 0:
            return tr
    return rows


def _sum_leading(x, *, name):
    n, R, C = x.shape
    tr = _row_block(R)

    def body(x_ref, o_ref):
        acc = x_ref[0]
        for j in range(1, n):
            acc = acc + x_ref[j]
        o_ref[...] = acc

    return pl.pallas_call(
        body, grid=(R // tr,), in_specs=[pl.BlockSpec((n, tr, C), lambda i: (0, i, 0))], out_specs=pl.BlockSpec((tr, C), lambda i: (i, 0)),
        out_shape=jax.ShapeDtypeStruct((R, C), x.dtype), compiler_params=_params("parallel"), name=name,
    )(x)


def _add_pair(x, y, *, name):
    n, R, C = x.shape
    tr = _row_block(R)

    def body(x_ref, y_ref, o_ref):
        o_ref[...] = x_ref[...] + y_ref[...]

    blk = pl.BlockSpec((1, tr, C), lambda j, i: (j, i, 0))
    return pl.pallas_call(
        body, grid=(n, R // tr), in_specs=[blk, blk], out_specs=blk,
        out_shape=jax.ShapeDtypeStruct((n, R, C), x.dtype), compiler_params=_params("parallel", "parallel"), name=name,
    )(x, y)


def _adamw(w, g, m, v, *, name):
    R, C = w.shape
    tr = _row_block(R)
    c1, c2 = 1.0 - ADAM_B1 ** ADAM_STEP, 1.0 - ADAM_B2 ** ADAM_STEP

    def body(w_ref, g_ref, m_ref, v_ref, d_ref, nm_ref, nv_ref):
        g_ = g_ref[...]
        nm = ADAM_B1 * m_ref[...] + (1.0 - ADAM_B1) * g_
        nv = ADAM_B2 * v_ref[...] + (1.0 - ADAM_B2) * (g_ * g_)
        d_ref[...] = -ADAM_LR * ((nm / c1) / (jnp.sqrt(nv / c2) + ADAM_EPS) + ADAM_WD * w_ref[...])
        nm_ref[...] = nm
        nv_ref[...] = nv

    blk = pl.BlockSpec((tr, C), lambda i: (i, 0))
    shp = jax.ShapeDtypeStruct((R, C), F32)
    return pl.pallas_call(
        body, grid=(R // tr,), in_specs=[blk] * 4, out_specs=[blk] * 3, out_shape=[shp] * 3,
        compiler_params=_params("parallel"), name=name,
    )(w, g, m, v)


def _small_reduce(packs, logits, *, name):
    n, R, C = packs.shape

    def body(p_ref, lg_ref, s_ref, dlg_ref):
        acc = p_ref[0]
        for j in range(1, n):
            acc = acc + p_ref[j]
        s_ref[...] = acc
        lg = lg_ref[...]
        e = jnp.exp(lg - jnp.max(lg, axis=0, keepdims=True))
        sm = e / jnp.sum(e, axis=0, keepdims=True)
        dlb = acc[5:6, HG_WIDTH:2 * HG_WIDTH]
        first = lax.broadcasted_iota(jnp.int32, sm.shape, 0) == 0
        dlg_ref[...] = sm[0:1, :] * (jnp.where(first, 1.0, 0.0) - sm) * dlb

    whole = lambda shape: pl.BlockSpec(shape, lambda: (0,) * len(shape))
    return pl.pallas_call(
        body, in_specs=[whole((n, R, C)), whole(logits.shape)], out_specs=[whole((R, C)), whole(logits.shape)],
        out_shape=[jax.ShapeDtypeStruct((R, C), F32), jax.ShapeDtypeStruct(logits.shape, F32)],
        compiler_params=pltpu.CompilerParams(vmem_limit_bytes=VMEM_LIMIT), name=name,
    )(packs, logits)


HBM_SPEC = pl.BlockSpec(memory_space=pl.ANY)


def _place():
    return lax.axis_index("x"), lax.axis_index("y"), lax.axis_index("c")


class _Copies:
    def __init__(self, arrays, out_shapes, n_copies, make, finish):
        self.arrays, self.out_shapes, self.n_copies, self.make, self.finish = list(arrays), list(out_shapes), n_copies, make, finish

    def scratch(self):
        return [pltpu.SemaphoreType.DMA((self.n_copies,)), pltpu.SemaphoreType.DMA((self.n_copies,))]

    def run(self, name):
        n = len(self.arrays)

        def body(*refs):
            copies = self.make(refs[:n], refs[n:2 * n], *refs[2 * n:])
            for cp in copies:
                cp.start()
            for cp in copies:
                cp.wait()

        outs = pl.pallas_call(body, in_specs=[HBM_SPEC] * n, out_specs=[HBM_SPEC] * n, out_shape=self.out_shapes,
                              scratch_shapes=self.scratch(), name=name)(*self.arrays)
        return self.finish(outs)


def _remote(src, dst, send_sems, recv_sems, k, peer):
    return pltpu.make_async_remote_copy(src_ref=src, dst_ref=dst, send_sem=send_sems.at[k], recv_sem=recv_sems.at[k],
                                        device_id=peer, device_id_type=MESH)


def _same_core_peers(x, y, c):
    return [(1 - x, y, c), (x, 1 - y, c), (1 - x, 1 - y, c)]


def _all_peers(x, y, c):
    flip = lambda v, b: 1 - v if b else v
    return [(flip(x, r & 4), flip(y, r & 2), flip(c, r & 1)) for r in range(1, 8)]


def _gather(arrays, peers_of, slot_of, n_slots):
    n_peers = len(peers_of(0, 0, 0))

    def make(ins, outs, send_sems, recv_sems):
        x, y, c = _place()
        slot = slot_of(x, y, c)
        return [_remote(ins[a], outs[a].at[slot], send_sems, recv_sems, a * n_peers + k, peer)
                for a in range(len(arrays)) for k, peer in enumerate(peers_of(x, y, c))]

    def finish(outs):
        slot = slot_of(*_place())
        return [lax.dynamic_update_index_in_dim(o, a, slot, 0) for o, a in zip(outs, arrays)]

    shapes = [jax.ShapeDtypeStruct((n_slots,) + a.shape, a.dtype) for a in arrays]
    return _Copies(arrays, shapes, len(arrays) * n_peers, make, finish)


def _gather_chips(arrays):
    return _gather(arrays, _same_core_peers, lambda x, y, c: 2 * x + y, N_CHIPS)


def _gather_all(arrays):
    return _gather(arrays, _all_peers, lambda x, y, c: 4 * x + 2 * y + c, N_DEV)


def _pair_swap(a):
    def make(ins, outs, send_sems, recv_sems):
        x, y, c = _place()
        return [_remote(ins[0].at[1 - c], outs[0], send_sems, recv_sems, 0, (x, y, 1 - c))]

    return _Copies([a], [jax.ShapeDtypeStruct(a.shape[1:], a.dtype)], 1, make, lambda outs: outs[0])


def _chip_scatter(p):
    def make(ins, outs, send_sems, recv_sems):
        x, y, c = _place()
        return [_remote(ins[0].at[2 * px + py], outs[0].at[2 * x + y], send_sems, recv_sems, k, (px, py, pc))
                for k, (px, py, pc) in enumerate(_same_core_peers(x, y, c))]

    def finish(outs):
        x, y, _ = _place()
        me = 2 * x + y
        return lax.dynamic_update_index_in_dim(outs[0], lax.dynamic_index_in_dim(p, me, 0, keepdims=False), me, 0)

    return _Copies([p], [jax.ShapeDtypeStruct(p.shape, p.dtype)], 3, make, finish)


def _owner_scatter(blocks):
    def make(ins, outs, send_sems, recv_sems):
        x, y, c = _place()
        return [_remote(ins[0].at[2 * px + py, pc], outs[0].at[4 * x + 2 * y + c], send_sems, recv_sems, k, (px, py, pc))
                for k, (px, py, pc) in enumerate(_all_peers(x, y, c))]

    def finish(outs):
        x, y, c = _place()
        mine = lax.dynamic_index_in_dim(lax.dynamic_index_in_dim(blocks, 2 * x + y, 0, keepdims=False), c, 0, keepdims=False)
        return lax.dynamic_update_index_in_dim(outs[0], mine, 4 * x + 2 * y + c, 0)

    return _Copies([blocks], [jax.ShapeDtypeStruct((N_DEV,) + blocks.shape[2:], blocks.dtype)], N_DEV - 1, make, finish)


def _pair_gather(q):
    def make(ins, outs, send_sems, recv_sems):
        x, y, c = _place()
        return [_remote(ins[0], outs[0].at[c], send_sems, recv_sems, 0, (x, y, 1 - c))]

    return _Copies([q], [jax.ShapeDtypeStruct((2,) + q.shape, q.dtype)], 1, make,
                   lambda outs: lax.dynamic_update_index_in_dim(outs[0], q, _place()[2], 0))


def _grad_blocks(dw, kind):
    if kind == "cols2d":
        K, N = dw.shape
        b = dw.reshape(2, K // 2, N_CHIPS, N // N_CHIPS).transpose(2, 0, 1, 3)
    elif kind == "rows2d":
        b = dw.reshape(N_CHIPS, 2, dw.shape[0] // 8, dw.shape[1])
    elif kind == "cols3d":
        L, K, N = dw.shape
        b = dw.reshape(L, K, N_CHIPS, N // N_CHIPS).transpose(2, 0, 1, 3)
    else:
        L, K, N = dw.shape
        b = dw.reshape(L, N_CHIPS, K // N_CHIPS, N).transpose(1, 0, 2, 3)
    return b.reshape(N_CHIPS, 2, -1, D_MODEL)


def _pad_rows(a, rows):
    return jnp.concatenate([a, jnp.zeros((rows - a.shape[0],) + a.shape[1:], a.dtype)], axis=0)


def _forward_backward(x, target, W, late_weights=None, early_grads=None, last_grads=None):
    row = lambda a: a.reshape(1, -1)
    relu2 = lambda acc: (jnp.square(jnp.maximum(acc, 0.0)),)
    normed = lambda h, g: h * lax.rsqrt(jnp.mean(h * h, axis=-1, keepdims=True) + RMS_EPS) * g

    def residual_norm(acc, res, g):
        h = res + acc
        return h, normed(h, g)

    def norm_bwd(du, h_blk, dres_blk, g):
        dx, dg_terms = _rms_bwd_math(h_blk, g, du)
        dh = dres_blk + dx
        return dh, dh, jnp.sum(dg_terms, axis=0, keepdims=True), jnp.sum(dh, axis=0, keepdims=True)

    def matmul_norm_bwd(dy, w, h_in, g, dres, rider=None, *, tk=1024, name):
        return _matmul(dy, w, mode="nt", out_dtypes=[F32, MXU_DTYPE], epilogue=norm_bwd, tiles=[h_in, dres], rows=[row(g)],
                       n_sums=2, rider=rider, tk=tk, name=name)

    G = {}

    u0 = _rmsnorm_fwd(x, row(W["norm_mix_g"][0]), name="norm_mix0")
    proj, qkv = _matmul(u0, W["w_in"], mode="nn", out_dtypes=[F32, MXU_DTYPE], epilogue=lambda acc: (acc, acc), tn=896, name="in_proj")
    o_sb, got = _sb_fwd(qkv, late_weights and late_weights[0], name="sb_fwd")
    if late_weights:
        W = {**W, **late_weights[1](got)}
    hg_out, hg_o, hg_states = _hg_fwd(proj, W["hg_lb_logits"], row(W["hg_norm_g"]), name="hg_fwd")
    mix = jnp.concatenate([o_sb, hg_out], axis=-1)
    h1, u1 = _matmul(mix, W["w_out"], mode="nn", out_dtypes=[F32, MXU_DTYPE], epilogue=residual_norm, tiles=[x],
                     rows=[row(W["norm_ffn_g"][0])], name="out_proj")
    r0 = _matmul(u1, W["w_ff1"][0], mode="nn", out_dtypes=[MXU_DTYPE], epilogue=relu2, name="ff1_0")
    h2, u2 = _matmul(r0, W["w_ff2"][0], mode="nn", out_dtypes=[F32, MXU_DTYPE], epilogue=residual_norm, tiles=[h1],
                     rows=[row(W["norm_mix_g"][1])], name="ff2_0")
    p = _matmul(u2, W["w_glu"], mode="nn", out_dtypes=[F32], epilogue=lambda acc, b: (acc + b,), rows=[row(W["b_glu"])], name="glu_proj")
    w_dw = _pad_rows(W["w_dw"], CONV_HALO)
    ca, cy, cact = _conv_fwd(p, w_dw, row(W["b_dw"]), row(W["ln_g"]), row(W["ln_b"]), name="conv_fwd")
    h3, u3 = _matmul(cact, W["w_pw"], mode="nn", out_dtypes=[F32, MXU_DTYPE], epilogue=lambda acc, res, b, g: residual_norm(acc + b, res, g),
                     tiles=[h2], rows=[row(W["b_pw"]), row(W["norm_ffn_g"][1])], name="pw_proj")
    r1 = _matmul(u3, W["w_ff1"][1], mode="nn", out_dtypes=[MXU_DTYPE], epilogue=relu2, name="ff1_1")
    h4 = _matmul(r1, W["w_ff2"][1], mode="nn", out_dtypes=[F32], epilogue=lambda acc, res: (res + acc,), tiles=[h3], name="ff2_1")

    dh4, dh4_m, G["final_norm_g"], loss = _loss_head(h4, row(W["final_norm_g"]), target, name="loss_head")

    def mlp_bwd(dh, dh_m, h_in, u, r, layer, tag):
        d_relu2 = lambda acc, r_blk: (acc * (2.0 * jnp.sqrt(r_blk.astype(F32))),)
        da = _matmul(dh_m, W["w_ff2"][layer], mode="nt", out_dtypes=[MXU_DTYPE], epilogue=d_relu2, tiles=[r], name="d_ff2_act" + tag)
        dw2 = _matmul(r, dh_m, mode="tn", out_dtypes=[F32], tk=WGRAD_TOKENS, name="d_ff2_w" + tag)
        dw1 = _matmul(u, da, mode="tn", out_dtypes=[F32], tk=WGRAD_TOKENS, name="d_ff1_w" + tag)
        dh_in, dh_in_m, dg, cs = matmul_norm_bwd(da, W["w_ff1"][layer], h_in, W["norm_ffn_g"][layer], dh, name="d_ff1_act" + tag)
        return dh_in, dh_in_m, dg, cs, dw1, dw2

    dh3, dh3_m, dg_ffn1, cs_h3, dw1_1, dw2_1 = mlp_bwd(dh4, dh4_m, h3, u3, r1, 1, "1")
    G["b_pw"] = cs_h3
    dact = _matmul(dh3_m, W["w_pw"], mode="nt", out_dtypes=[F32], name="d_pw_act")
    G["w_pw"] = _matmul(cact, dh3_m, mode="tn", out_dtypes=[F32], tk=WGRAD_TOKENS, name="d_pw_w")
    dy, G["ln_g"], G["ln_b"], G["b_dw"] = _conv_bwd_norm(dact, cy, row(W["ln_g"]), row(W["ln_b"]), name="d_conv_norm")
    dp, G["w_dw"], G["b_glu"] = _conv_bwd_taps(dy, ca, p, w_dw, name="d_conv_taps")
    G["w_glu"] = _matmul(u2, dp, mode="tn", out_dtypes=[F32], tk=WGRAD_TOKENS, name="d_glu_w")
    dh2, dh2_m, dg_mix1, _ = matmul_norm_bwd(dp, W["w_glu"], h2, W["norm_mix_g"][1], dh3, name="d_glu_act")
    dh1, dh1_m, dg_ffn0, _, dw1_0, dw2_0 = mlp_bwd(dh2, dh2_m, h1, u1, r0, 0, "0")
    G["w_ff1"], G["w_ff2"] = jnp.stack([dw1_0, dw1_1]), jnp.stack([dw2_0, dw2_1])
    G["norm_ffn_g"] = jnp.concatenate([dg_ffn0, dg_ffn1], axis=0)
    dmix = _matmul(dh1_m, W["w_out"], mode="nt", out_dtypes=[F32], name="d_out_act")
    G["w_out"] = _matmul(mix, dh1_m, mode="tn", out_dtypes=[F32], tk=WGRAD_TOKENS, name="d_out_w")
    riding = early_grads(G) if early_grads else None
    (dsq, dsk, dsv), got = _sb_bwd(qkv, dmix, riding, name="sb_bwd")
    d_hg, G["hg_lb"], G["hg_norm_g"] = _hg_bwd(proj, hg_o, hg_states, dmix, W["hg_lb_logits"], row(W["hg_norm_g"]), name="hg_bwd")
    dproj = jnp.concatenate([dsq, dsk, dsv, d_hg], axis=-1).astype(MXU_DTYPE)
    G["w_in"] = _matmul(u0, dproj, mode="tn", out_dtypes=[F32], tn=896, tk=WGRAD_TOKENS, name="d_in_w")
    last = last_grads(G) if last_grads else None
    res = matmul_norm_bwd(dproj, W["w_in"], x, W["norm_mix_g"][0], dh1, last, tk=896, name="d_in_act")
    (dx, _, dg_mix0, _), got_last = res if last_grads else (res, [])
    G["norm_mix_g"] = jnp.concatenate([dg_mix0, dg_mix1], axis=0)
    return loss, dx, G, [(riding, got), (last, got_last)]


BIG = (("w_out_ab", "w_out", "rows2d"), ("conv_w_glu", "w_glu", "cols2d"), ("conv_w_pw", "w_pw", "rows2d"),
       ("w_ff1", "w_ff1", "cols3d"), ("w_ff2", "w_ff2", "rows3d"), ("w_in_ab", "w_in", "cols2d"))
LATE = BIG[:-1]
SMALL_SHARDED = ("conv_b_glu", "conv_w_dw", "conv_b_dw", "conv_ln_g", "conv_ln_b", "conv_b_pw")
REPLICATED = ("norm_mix_g", "norm_ffn_g", "hg_lb_logits", "hg_norm_g", "final_norm_g")
ORDER = ("norm_mix_g", "norm_ffn_g", "w_in_ab", "w_out_ab", "hg_lb_logits", "hg_norm_g", "conv_w_glu", "conv_b_glu",
         "conv_w_dw", "conv_b_dw", "conv_ln_g", "conv_ln_b", "conv_w_pw", "conv_b_pw", "w_ff1", "w_ff2", "final_norm_g")


def _step(x, loss_target, w, m, v):
    D = D_MODEL
    x2, t2 = x.reshape(-1, D), loss_target.reshape(-1, D)
    chip = 2 * lax.axis_index("x") + lax.axis_index("y")
    c = lax.axis_index("c")

    small_in = jnp.concatenate([w["conv_b_glu"].reshape(2, SHARD), w["conv_w_dw"].reshape(CONV_WIDTH, SHARD)] +
                               [w[n].reshape(1, SHARD) for n in ("conv_b_dw", "conv_ln_g", "conv_ln_b", "conv_b_pw")], axis=0)
    g_in, gs = _gather_chips([w["w_in_ab"].astype(MXU_DTYPE), _pad_rows(small_in, SMALL_IN_ROWS)]).run("gather_first_weights")
    vec = lambda r0, r1: gs[:, r0:r1].transpose(1, 0, 2).reshape(r1 - r0, D)
    W = {
        "w_in": _ChipWeight(g_in[:, 0], "cols"),
        "b_glu": gs[:, 0:2].reshape(2 * D), "w_dw": vec(2, 33), "b_dw": vec(33, 34)[0], "ln_g": vec(34, 35)[0],
        "ln_b": vec(35, 36)[0], "b_pw": vec(36, 37)[0],
        "norm_mix_g": w["norm_mix_g"], "norm_ffn_g": w["norm_ffn_g"], "hg_lb_logits": w["hg_lb_logits"],
        "hg_norm_g": w["hg_norm_g"], "final_norm_g": w["final_norm_g"],
    }
    late = _gather_chips([w[n].astype(MXU_DTYPE) for n, _, _ in LATE])

    def assemble(got):
        gw = dict(zip([s for _, s, _ in LATE], late.finish(got)))
        layers = lambda g, along: [_ChipWeight(g, along, (layer,)) for layer in range(2)]
        return {"w_out": gw["w_out"].reshape(D, D), "w_glu": _ChipWeight(gw["w_glu"][:, 0], "cols"), "w_pw": gw["w_pw"].reshape(D, D),
                "w_ff1": layers(gw["w_ff1"], "cols"), "w_ff2": layers(gw["w_ff2"], "rows")}

    def early_grads(G):
        return _owner_scatter(jnp.concatenate([_grad_blocks(G[s], kind) for _, s, kind in LATE], axis=2))

    def last_grads(G):
        blocks = _grad_blocks(G["w_in"], "cols2d")
        from_pair = _pair_swap(blocks.transpose(1, 0, 2, 3)).run("grads_pair_swap_in")
        return _chip_scatter(_add_pair(lax.dynamic_index_in_dim(blocks, c, axis=1, keepdims=False), from_pair, name="grads_pair_add_in"))

    loss, dx, G, riders = _forward_backward(x2, t2, W, (late, assemble), early_grads, last_grads)
    halves = [_sum_leading(copies.finish(got), name="grads_add_" + tag) for (copies, got), tag in zip(riders, ("late", "in"))]
    half = jnp.concatenate(halves, axis=0)
    full = _pair_gather(half).run("grads_pair_gather")

    pack = jnp.concatenate([
        G["norm_mix_g"], G["norm_ffn_g"], G["final_norm_g"], jnp.concatenate([G["hg_norm_g"], G["hg_lb"]], axis=1),
        _pad_rows(jnp.broadcast_to(loss, (1, D)), 2), G["b_glu"].reshape(2, D), G["w_dw"], G["b_dw"], G["ln_g"], G["ln_b"], G["b_pw"],
    ], axis=0)
    pack = _pad_rows(pack, SMALL_ROWS)
    (packs,) = _gather_all([pack]).run("gather_small_grads")
    ssum, d_logits = _small_reduce(packs, w["hg_lb_logits"], name="reduce_small_grads")
    cut = lambda r0, r1: lax.dynamic_slice(ssum, (r0, chip * SHARD), (r1 - r0, SHARD))
    grads = {
        "norm_mix_g": ssum[0:2], "norm_ffn_g": ssum[2:4], "final_norm_g": ssum[4], "hg_norm_g": ssum[5, :HG_WIDTH].reshape(1, HG_HEADS, HG_DH),
        "hg_lb_logits": d_logits,
        "conv_b_glu": lax.dynamic_slice(ssum[8:10].reshape(1, 2 * D), (0, chip * 2 * SHARD), (1, 2 * SHARD)),
        "conv_w_dw": cut(10, 10 + CONV_WIDTH).reshape(1, CONV_WIDTH, SHARD),
        "conv_b_dw": cut(42, 43), "conv_ln_g": cut(43, 44), "conv_ln_b": cut(44, 45), "conv_b_pw": cut(45, 46),
    }
    loss_out = ssum[6, 0]

    off = 0
    for n, s, kind in BIG:
        shard = w[n].shape
        rows = w[n].size // (2 * D)
        grads[n] = full[:, off:off + rows].reshape(shard)
        off += rows

    delta, new_m, new_v = {}, {}, {}
    for n, _, _ in BIG:
        view = lambda a: a.reshape(-1, a.shape[-1])
        outs = _adamw(view(w[n]), view(grads[n]), view(m[n]), view(v[n]), name="adamw_" + n)
        delta[n], new_m[n], new_v[n] = (o.reshape(w[n].shape) for o in outs)
    small = SMALL_SHARDED + REPLICATED
    sizes = [w[n].size for n in small]
    total = sum(sizes)
    rows = -(-total // (8 * D)) * 8
    packed = lambda d: _pad_rows(jnp.concatenate([d[n].reshape(-1) for n in small]).reshape(-1, 128), rows * 8).reshape(rows, D)
    outs = _adamw(packed(w), packed(grads), packed(m), packed(v), name="adamw_small")
    off = 0
    for n, size in zip(small, sizes):
        delta[n], new_m[n], new_v[n] = (o.reshape(-1)[off:off + size].reshape(w[n].shape) for o in outs)
        off += size
    grads = {n: grads[n].reshape(w[n].shape) for n in ORDER}
    return (loss_out, dx.reshape(x.shape), *[grads[n] for n in ORDER], *[delta[n] for n in ORDER],
            *[new_m[n] for n in ORDER], *[new_v[n] for n in ORDER])


def kernel(x, norm_mix_g, norm_ffn_g, w_in_ab, w_out_ab, hg_lb_logits, hg_norm_g, conv_w_glu, conv_b_glu, conv_w_dw, conv_b_dw, conv_ln_g, conv_ln_b, conv_w_pw, conv_b_pw, w_ff1, w_ff2, final_norm_g, loss_target, m_norm_mix_g, m_norm_ffn_g, m_w_in_ab, m_w_out_ab, m_hg_lb_logits, m_hg_norm_g, m_conv_w_glu, m_conv_b_glu, m_conv_w_dw, m_conv_b_dw, m_conv_ln_g, m_conv_ln_b, m_conv_w_pw, m_conv_b_pw, m_w_ff1, m_w_ff2, m_final_norm_g, v_norm_mix_g, v_norm_ffn_g, v_w_in_ab, v_w_out_ab, v_hg_lb_logits, v_hg_norm_g, v_conv_w_glu, v_conv_b_glu, v_conv_w_dw, v_conv_b_dw, v_conv_ln_g, v_conv_ln_b, v_conv_w_pw, v_conv_b_pw, v_w_ff1, v_w_ff2, v_final_norm_g):
    args = locals()
    w = {n: args[n] for n in ORDER}
    m = {n: args["m_" + n] for n in ORDER}
    v = {n: args["v_" + n] for n in ORDER}
    return _step(x, loss_target, w, m, v)
```

```python
import functools

import jax
import jax.numpy as jnp
from jax import lax
from jax.experimental import pallas as pl
from jax.experimental.pallas import tpu as pltpu

F32 = jnp.float32
MXU_DTYPE = jnp.bfloat16
MESH = pl.DeviceIdType.MESH

D_MODEL = 1024
SB_HEADS, SB_DH, SB_WIDTH = 8, 64, 512
SB_KEYS = 512
SB_SUB = 256
SB_ROWS_FWD, SB_ROWS_BWD = 512, 256
HG_HEADS, HG_DH, HG_WIDTH = 4, 128, 512
HG_CHUNK = 16
HG_TOKENS = 256
CONV_WIDTH = 31
CONV_HALO = 32
CONV_ROWS = 32
RMS_EPS = 1e-6
LN_EPS = 1e-5
N_CHIPS = 4
N_DEV = 8
SHARD = D_MODEL // N_CHIPS
SMALL_IN_ROWS = 40
SMALL_ROWS = 48
WGRAD_TOKENS = 2048
VMEM_LIMIT = 56 * 1024 * 1024

ADAM_LR, ADAM_B1, ADAM_B2, ADAM_EPS, ADAM_WD, ADAM_STEP = 0.001, 0.9, 0.999, 1e-08, 0.01, 10


def _params(*sem):
    return pltpu.CompilerParams(dimension_semantics=sem, vmem_limit_bytes=VMEM_LIMIT)


def _mx(v):
    return v.astype(MXU_DTYPE)


def _dot(a, b):
    return jnp.dot(_mx(a), _mx(b), preferred_element_type=F32)


def _dot_nt(a, b):
    return lax.dot_general(_mx(a), _mx(b), (((1,), (1,)), ((), ())), preferred_element_type=F32)


def _dot_tn(a, b):
    return lax.dot_general(_mx(a), _mx(b), (((0,), (0,)), ((), ())), preferred_element_type=F32)


def _neg_abs(x):
    bits = lax.bitcast_convert_type(x, jnp.uint32) | jnp.uint32(0x80000000)
    return lax.bitcast_convert_type(bits, F32)


def _key_order_sums(v, tri2, later):
    hi = _mx(v)
    lo = _mx(v - hi.astype(F32))
    n = v.shape[1] // SB_SUB
    blocks = [slice(b * SB_SUB, (b + 1) * SB_SUB) for b in range(n)]
    totals = [jnp.sum(v[:, sl], axis=1, keepdims=True) for sl in blocks]
    far, running = [None] * n, None
    for b in (reversed(range(n)) if later else range(n)):
        far[b] = running
        running = totals[b] if running is None else running + totals[b]
    sums = []
    for b, sl in enumerate(blocks):
        inside = jnp.dot(jnp.concatenate([hi[:, sl], lo[:, sl]], axis=1), tri2, preferred_element_type=F32)
        sums.append(inside if far[b] is None else inside + far[b])
    return jnp.concatenate(sums, axis=1), running


class _ChipWeight:
    def __init__(self, parts, along, lead=()):
        self.parts, self.along, self.lead = parts, along, tuple(lead)
        r, c = parts.shape[-2:]
        self.shape = (r, N_CHIPS * c) if along == "cols" else (N_CHIPS * r, c)

    def _gathered_is_n(self, mode):
        return (self.along == "cols") == (mode in ("nn", "tn"))

    def tile(self, mode, tn, tk):
        r, c = self.parts.shape[-2:]
        part = c if self.along == "cols" else r
        return (part, tk) if self._gathered_is_n(mode) else (tn, part)

    def spec(self, mode, tn, tk):
        squeezed = (None,) * (1 + len(self.lead))
        lead, cols, by_n = self.lead, self.along == "cols", self._gathered_is_n(mode)
        block = (tn, tk) if mode == "nt" else (tk, tn)

        def index(i, j, k):
            chip, other = (j, k) if by_n else (k, j)
            return (chip,) + lead + ((other, 0) if cols else (0, other))

        return pl.BlockSpec(squeezed + block, index)


def _matmul(a, b, *, mode, out_dtypes, epilogue=None, tiles=(), rows=(), n_sums=0, rider=None, tm=1024, tn=1024, tk=1024, name):
    b_shape = b.shape
    if mode == "nn":
        (M, K), N = a.shape, b_shape[1]
    elif mode == "nt":
        (M, K), N = a.shape, b_shape[0]
    else:
        (K, M), N = a.shape, b_shape[1]
    if isinstance(b, _ChipWeight):
        tn, tk = b.tile(mode, tn, tk)
    tm, tn, tk = min(tm, M), min(tn, N), min(tk, K)
    assert M % tm == 0 and N % tn == 0 and K % tk == 0, (name, M, N, K)
    nk = K // tk
    a_spec = pl.BlockSpec((tk, tm), lambda i, j, k: (k, i)) if mode == "tn" else pl.BlockSpec((tm, tk), lambda i, j, k: (i, k))
    if isinstance(b, _ChipWeight):
        b_spec, b = b.spec(mode, tn, tk), b.parts
    else:
        b_spec = pl.BlockSpec((tn, tk), lambda i, j, k: (j, k)) if mode == "nt" else pl.BlockSpec((tk, tn), lambda i, j, k: (k, j))
    dims = {"nn": ((1,), (0,)), "nt": ((1,), (1,)), "tn": ((0,), (0,))}[mode]
    n_t, n_r, n_o = len(tiles), len(rows), len(out_dtypes)
    n_x = 0 if rider is None else len(rider.arrays)
    grid = (M // tm, N // tn, nk)
    if epilogue is None:
        epilogue = lambda acc: (acc,)

    def body(a_ref, b_ref, *rest):
        extra, x_in, rest = rest[:n_t + n_r], rest[n_t + n_r:n_t + n_r + n_x], rest[n_t + n_r + n_x:]
        outs, sums, x_out, acc_ref, sems = rest[:n_o], rest[n_o:n_o + n_sums], rest[n_o + n_sums:n_o + n_sums + n_x], rest[n_o + n_sums + n_x], rest[n_o + n_sums + n_x + 1:]
        i, j, k = (pl.program_id(d) for d in range(3))
        if rider is not None:
            @pl.when((i == 0) & (j == 0) & (k == 0))
            def _():
                for cp in rider.make(x_in, x_out, *sems):
                    cp.start()

        @pl.when(k == 0)
        def _():
            acc_ref[...] = jnp.zeros_like(acc_ref)

        acc_ref[...] += lax.dot_general(_mx(a_ref[...]), _mx(b_ref[...]), (dims, ((), ())), preferred_element_type=F32)

        @pl.when(k == nk - 1)
        def _():
            res = epilogue(acc_ref[...], *[e[...] for e in extra])
            for o_ref, r in zip(outs, res[:n_o]):
                o_ref[...] = r.astype(o_ref.dtype)
            for s_ref, r in zip(sums, res[n_o:]):
                @pl.when(i == 0)
                def _():
                    s_ref[...] = jnp.zeros_like(s_ref)

                s_ref[...] += r

        if rider is not None:
            @pl.when((i == grid[0] - 1) & (j == grid[1] - 1) & (k == grid[2] - 1))
            def _():
                for cp in rider.make(x_in, x_out, *sems):
                    cp.wait()

    tile_spec = pl.BlockSpec((tm, tn), lambda i, j, k: (i, j))
    row_spec = pl.BlockSpec((1, tn), lambda i, j, k: (0, j))
    ordered = n_sums > 0 or rider is not None
    outs = pl.pallas_call(
        body, grid=grid,
        in_specs=[a_spec, b_spec] + [tile_spec] * n_t + [row_spec] * n_r + [HBM_SPEC] * n_x,
        out_specs=[tile_spec] * n_o + [row_spec] * n_sums + [HBM_SPEC] * n_x,
        out_shape=[jax.ShapeDtypeStruct((M, N), dt) for dt in out_dtypes] + [jax.ShapeDtypeStruct((1, N), F32)] * n_sums
        + ([] if rider is None else rider.out_shapes),
        scratch_shapes=[pltpu.VMEM((tm, tn), F32)] + ([] if rider is None else rider.scratch()),
        compiler_params=_params(*(("arbitrary",) * 3 if ordered else ("parallel", "parallel", "arbitrary"))), name=name,
    )(a, b, *tiles, *rows, *([] if rider is None else rider.arrays))
    res = outs[0] if n_o + n_sums == 1 else outs[:n_o + n_sums]
    return res if rider is None else (res, outs[n_o + n_sums:])


def _token_block(T):
    return min(512, T)


def _rmsnorm_fwd(h, g, *, name):
    T, D = h.shape
    tb = _token_block(T)

    def body(h_ref, g_ref, u_ref):
        x = h_ref[...]
        r = lax.rsqrt(jnp.mean(x * x, axis=-1, keepdims=True) + RMS_EPS)
        u_ref[...] = (x * r * g_ref[...]).astype(u_ref.dtype)

    blk = pl.BlockSpec((tb, D), lambda i: (i, 0))
    return pl.pallas_call(
        body, grid=(T // tb,), in_specs=[blk, pl.BlockSpec((1, D), lambda i: (0, 0))], out_specs=blk,
        out_shape=jax.ShapeDtypeStruct((T, D), MXU_DTYPE), compiler_params=_params("parallel"), name=name,
    )(h, g)


def _rms_bwd_math(x, g, du):
    r = lax.rsqrt(jnp.mean(x * x, axis=-1, keepdims=True) + RMS_EPS)
    gd = g * du
    dx = r * gd - x * (r * r * r) * jnp.mean(gd * x, axis=-1, keepdims=True)
    return dx, du * x * r


def _loss_head(h, g, target, *, name):
    T, D = h.shape
    tb = _token_block(T)

    def body(h_ref, g_ref, t_ref, dh_ref, dhm_ref, dg_ref, loss_ref):
        @pl.when(pl.program_id(0) == 0)
        def _():
            dg_ref[...] = jnp.zeros_like(dg_ref)
            loss_ref[...] = jnp.zeros_like(loss_ref)

        x, gg = h_ref[...], g_ref[...]
        r = lax.rsqrt(jnp.mean(x * x, axis=-1, keepdims=True) + RMS_EPS)
        diff = x * r * gg - t_ref[...]
        per_token = jnp.mean(diff * diff, axis=-1, keepdims=True)
        loss_ref[...] += 0.5 * jnp.sum(per_token, axis=0, keepdims=True)
        dx, dg_terms = _rms_bwd_math(x, gg, diff / D)
        dh_ref[...] = dx
        dhm_ref[...] = dx.astype(dhm_ref.dtype)
        dg_ref[...] += jnp.sum(dg_terms, axis=0, keepdims=True)

    blk = pl.BlockSpec((tb, D), lambda i: (i, 0))
    row = pl.BlockSpec((1, D), lambda i: (0, 0))
    return pl.pallas_call(
        body, grid=(T // tb,), in_specs=[blk, row, blk], out_specs=[blk, blk, row, pl.BlockSpec((1, 1), lambda i: (0, 0))],
        out_shape=[jax.ShapeDtypeStruct((T, D), F32), jax.ShapeDtypeStruct((T, D), MXU_DTYPE), jax.ShapeDtypeStruct((1, D), F32),
                   jax.ShapeDtypeStruct((1, 1), F32)],
        compiler_params=_params("arbitrary"), name=name,
    )(h, g, target)


def _sb_scores(qm, ks, later, tri, mask, need_log_beta=True):
    z = _dot_nt(qm, ks)
    sp = jnp.maximum(z, 0.0) + jnp.log(1.0 + jnp.exp(_neg_abs(z)))
    lb = z - sp if need_log_beta else None
    if mask is not None:
        sp = jnp.where(mask, sp, 0.0)
    after, total = _key_order_sums(sp, tri, later=True)
    w = jnp.exp((lb if need_log_beta else z) - (after + later))
    if mask is not None:
        w = jnp.where(mask, w, 0.0)
    return total, lb, w


def _sb_setup(q_ref, rows, inclusive=False):
    i, hsel = pl.program_id(1), pl.program_id(2)
    lane = lax.broadcasted_iota(jnp.int32, (rows, 2 * SB_DH), 1)
    mine = (lane >= SB_DH) == (hsel == 1)
    diag = (i * rows) // SB_KEYS
    t = i * rows + lax.broadcasted_iota(jnp.int32, (rows, SB_KEYS), 0)
    s = diag * SB_KEYS + lax.broadcasted_iota(jnp.int32, (rows, SB_KEYS), 1)
    a = lax.broadcasted_iota(jnp.int32, (2 * SB_SUB, SB_SUB), 0) % SB_SUB
    b = lax.broadcasted_iota(jnp.int32, (2 * SB_SUB, SB_SUB), 1)
    return i, hsel, mine, diag, s < t, _mx(a >= b if inclusive else a > b), _mx(a < b)


def _sb_keys(j, n=1):
    return pl.ds(pl.multiple_of(j * SB_KEYS, SB_KEYS), n * SB_KEYS)


def _sb_descend(n, step, carry, wide):
    pair = (lambda j, cr: step(j, 2, cr)) if wide else (lambda j, cr: step(j, 1, step(j + 1, 1, cr)))
    carry = lax.fori_loop(0, n // 2, lambda it, cr: pair(n - 2 - 2 * it, cr), carry)
    return lax.cond(n % 2 == 1, lambda cr: step(0, 1, cr), lambda cr: cr, carry)


def _sb_ascend(n, step, carry):
    odd = n % 2
    carry = lax.cond(odd == 1, lambda cr: step(0, 1, cr), lambda cr: cr, carry)
    return lax.fori_loop(0, n // 2, lambda it, cr: step(odd + 2 * it, 2, cr), carry)


def _sb_call(body, qkv, extra_in, out_blocks, out_dtype, scratch, rider, rows, *, name):
    T = qkv.shape[0]
    n_pairs = SB_HEADS // 2
    grid = (n_pairs, T // rows, 2)
    pair = lambda col0: pl.BlockSpec((rows, 2 * SB_DH), lambda p, i, h: (i, col0 + p))
    whole = lambda col0: pl.BlockSpec((T, 2 * SB_DH), lambda p, i, h: (0, col0 + p))
    in_specs = [pair(0), whole(n_pairs), whole(2 * n_pairs)] + [pair(0)] * len(extra_in)
    out_specs = [pair(0) if kind == "pair" else whole(0) for kind in out_blocks]
    n_in, n_out, n_r = len(in_specs), len(out_specs), 0 if rider is None else len(rider.arrays)

    def kernel_body(*refs):
        ins, r_in = refs[:n_in], refs[n_in:n_in + n_r]
        outs, r_out = refs[n_in + n_r:n_in + n_r + n_out], refs[n_in + n_r + n_out:n_in + 2 * n_r + n_out]
        rest = refs[n_in + 2 * n_r + n_out:]
        ids = [pl.program_id(a) for a in range(3)]
        if rider is not None:
            @pl.when((ids[0] == 0) & (ids[1] == 0) & (ids[2] == 0))
            def _():
                for cp in rider.make(r_in, r_out, *rest[len(scratch):]):
                    cp.start()

        body(ins, outs, rest[:len(scratch)])
        if rider is not None:
            @pl.when((ids[0] == grid[0] - 1) & (ids[1] == grid[1] - 1) & (ids[2] == grid[2] - 1))
            def _():
                for cp in rider.make(r_in, r_out, *rest[len(scratch):]):
                    cp.wait()

    res = pl.pallas_call(
        kernel_body, grid=grid, in_specs=in_specs + [HBM_SPEC] * n_r, out_specs=out_specs + [HBM_SPEC] * n_r,
        out_shape=[jax.ShapeDtypeStruct((T, SB_WIDTH), out_dtype)] * n_out + ([] if rider is None else rider.out_shapes),
        scratch_shapes=list(scratch) + ([] if rider is None else rider.scratch()),
        compiler_params=_params("arbitrary", "arbitrary", "arbitrary"), name=name,
    )(qkv, qkv, qkv, *extra_in, *([] if rider is None else rider.arrays))
    return res[:n_out], res[n_out:]


def _sb_fwd(qkv, rider=None, *, name):
    rows = min(SB_ROWS_FWD, qkv.shape[0])
    scale = SB_DH ** -0.5

    def body(ins, outs, _):
        (q_ref, k_ref, v_ref), (o_ref,) = ins, outs
        i, hsel, mine, diag, mask, tri, _ = _sb_setup(q_ref, rows, inclusive=True)
        qm = jnp.where(mine, q_ref[...], 0) * scale

        def tile(j, n, m, later, acc):
            total, _, w = _sb_scores(qm, k_ref[_sb_keys(j, n), :], later, tri, m, need_log_beta=False)
            return later + total, acc + _dot(w, v_ref[_sb_keys(j, n), :])

        carry = tile(diag, 1, mask, jnp.zeros((rows, 1), F32), jnp.zeros((rows, 2 * SB_DH), F32))
        _, acc = _sb_descend(diag, lambda j, n, cr: tile(j, n, None, *cr), carry, wide=False)
        res = jnp.where(mine, acc, 0.0).astype(o_ref.dtype)

        @pl.when(hsel == 0)
        def _():
            o_ref[...] = res

        @pl.when(hsel == 1)
        def _():
            o_ref[...] += res

    (o,), got = _sb_call(body, qkv, [], ["pair"], MXU_DTYPE, [], rider, rows, name=name)
    return o, got


def _sb_bwd(qkv, dmix, rider=None, *, name):
    T = qkv.shape[0]
    rows = min(SB_ROWS_BWD, T)
    scale = SB_DH ** -0.5

    def body(ins, outs, scratch):
        (q_ref, k_ref, v_ref, do_ref), (dq_ref, dk_ref, dv_ref), (da_ref, beta_ref) = ins, outs, scratch
        i, hsel, mine, diag, mask, tri, tri_before = _sb_setup(q_ref, rows)

        @pl.when((i == 0) & (hsel == 0))
        def _():
            dk_ref[...] = jnp.zeros_like(dk_ref)
            dv_ref[...] = jnp.zeros_like(dv_ref)

        qm = jnp.where(mine, q_ref[...], 0) * scale
        do_m = _mx(jnp.where(mine, do_ref[...], 0.0))

        def weights(j, n, m, later):
            total, lb, w = _sb_scores(qm, k_ref[_sb_keys(j, n), :], later, tri, m)
            da, beta = _dot_nt(do_m, v_ref[_sb_keys(j, n), :]) * w, jnp.exp(lb)
            for t in range(n):
                da_ref[j + t] = da[:, t * SB_KEYS:(t + 1) * SB_KEYS]
                beta_ref[j + t] = beta[:, t * SB_KEYS:(t + 1) * SB_KEYS]
            dv_ref[_sb_keys(j, n), :] += _dot_tn(w, do_m)
            return later + total

        later = weights(diag, 1, mask, jnp.zeros((rows, 1), F32))
        _sb_descend(diag, lambda j, n, c: weights(j, n, None, c), later, wide=True)

        def logits(j, n, m, before, dq):
            da = jnp.concatenate([da_ref[j + t] for t in range(n)], axis=1)
            beta = jnp.concatenate([beta_ref[j + t] for t in range(n)], axis=1)
            earlier, total = _key_order_sums(da, tri_before, later=False)
            dz = da - beta * (da + earlier + before)
            if m is not None:
                dz = jnp.where(m, dz, 0.0)
            dz = _mx(dz)
            dk_ref[_sb_keys(j, n), :] += _dot_tn(dz, qm)
            return before + total, dq + _dot(dz, k_ref[_sb_keys(j, n), :])

        carry = (jnp.zeros((rows, 1), F32), jnp.zeros((rows, 2 * SB_DH), F32))
        carry = _sb_ascend(diag, lambda j, n, cr: logits(j, n, None, *cr), carry)
        res = jnp.where(mine, logits(diag, 1, mask, *carry)[1] * scale, 0.0)

        @pl.when(hsel == 0)
        def _():
            dq_ref[...] = res

        @pl.when(hsel == 1)
        def _():
            dq_ref[...] += res

    n_tiles = T // SB_KEYS
    scratch = [pltpu.VMEM((n_tiles, rows, SB_KEYS), F32), pltpu.VMEM((n_tiles, rows, SB_KEYS), F32)]
    return _sb_call(body, qkv, [dmix], ["pair", "whole", "whole"], F32, scratch, rider, rows, name=name)


def _chunk_row(n):
    return lax.broadcasted_iota(jnp.int32, (n, HG_DH), 0) % HG_CHUNK


def _chunk_cumsum(x, row, reverse=False):
    n = x.shape[0]
    for sh in (1, 2, 4, 8):
        if reverse:
            x = x + jnp.where(row < HG_CHUNK - sh, pltpu.roll(x, n - sh, 0), 0.0)
        else:
            x = x + jnp.where(row >= sh, pltpu.roll(x, sh, 0), 0.0)
    return x


def _hg_lower_bound(logits_ref):
    lg = logits_ref[...]
    e = jnp.exp(lg - jnp.max(lg, axis=0, keepdims=True))
    return e[0:1, :] / jnp.sum(e, axis=0, keepdims=True)


def _hg_terms(fr, q, lb, row):
    sig = jax.nn.sigmoid(fr)
    f = lb + (1.0 - lb) * sig
    kk = 1.0 - f
    g = jnp.log(f)
    G = _chunk_cumsum(g, row)
    g_last = G + (_chunk_cumsum(g, row, reverse=True) - g)
    e_g, e_ng, e_lg = jnp.exp(G), jnp.exp(-G), jnp.exp(g_last - G)
    return dict(sig=sig, f=f, kk=kk, e_g=e_g, e_ng=e_ng, e_lg=e_lg, q_dec=q * e_g, k_intra=kk * e_ng,
                k_state=kk * e_lg, decay=jnp.exp(g_last))


def _hg_causal(n):
    t = lax.broadcasted_iota(jnp.int32, (n, n), 0)
    s = lax.broadcasted_iota(jnp.int32, (n, n), 1)
    return (s <= t) & (s // HG_CHUNK == t // HG_CHUNK)


def _chunks(a):
    return a.reshape(a.shape[0] // HG_CHUNK, HG_CHUNK, a.shape[1])


def _per_chunk(lhs, rhs, contract):
    return lax.dot_general(_mx(lhs), _mx(rhs), ((contract[:1], contract[1:]), ((0,), (0,))), preferred_element_type=F32)


def _hg_specs(T, tb, col0, order):
    return [pl.BlockSpec((tb, HG_WIDTH), functools.partial(lambda i, j: (order(i), j), j=col0 + j)) for j in range(4)]


def _hg_fwd(proj, logits, norm_g, *, name):
    T = proj.shape[0]
    tb = min(HG_TOKENS, T)
    nch = tb // HG_CHUNK

    def body(q_ref, f_ref, i_ref, gate_ref, lg_ref, ng_ref, out_ref, o_ref, s_ref, st_ref, inc_ref, dec_ref):
        @pl.when(pl.program_id(0) == 0)
        def _():
            st_ref[...] = jnp.zeros_like(st_ref)

        lb_all = _hg_lower_bound(lg_ref)
        row = _chunk_row(tb)
        causal = _hg_causal(tb)
        for hh in range(HG_HEADS):
            cols = slice(hh * HG_DH, (hh + 1) * HG_DH)
            t = _hg_terms(f_ref[:, cols], q_ref[:, cols], lb_all[:, cols], row)
            v = i_ref[:, cols]
            scores = jnp.where(causal, _dot_nt(t["q_dec"], t["k_intra"]), 0.0)
            o_intra = _dot(scores, v)
            inc_ref[...] = _per_chunk(_chunks(v), _chunks(t["k_state"]), (1, 1))
            dec_ref[...] = _chunks(t["decay"])

            def step(ci, st):
                s_ref[ci, hh] = st
                return st * dec_ref[ci][0:1, :] + inc_ref[ci]

            st_ref[hh] = lax.fori_loop(0, nch, step, st_ref[hh], unroll=4)
            o_inter = _per_chunk(_chunks(t["q_dec"]), s_ref[:, hh], (2, 2))
            o = o_intra + o_inter.reshape(tb, HG_DH)
            o_ref[:, cols] = o
            gate = gate_ref[:, cols]
            on = o * lax.rsqrt(jnp.mean(o * o, axis=-1, keepdims=True) + RMS_EPS) * ng_ref[:, cols]
            out_ref[:, cols] = (on * (gate * jax.nn.sigmoid(gate))).astype(out_ref.dtype)

    blk = pl.BlockSpec((tb, HG_WIDTH), lambda i: (i, 0))
    return pl.pallas_call(
        body, grid=(T // tb,),
        in_specs=_hg_specs(T, tb, 3, lambda i: i) + [pl.BlockSpec((3, HG_WIDTH), lambda i: (0, 0)), pl.BlockSpec((1, HG_WIDTH), lambda i: (0, 0))],
        out_specs=[blk, blk, pl.BlockSpec((nch, HG_HEADS, HG_DH, HG_DH), lambda i: (i, 0, 0, 0))],
        out_shape=[jax.ShapeDtypeStruct((T, HG_WIDTH), MXU_DTYPE), jax.ShapeDtypeStruct((T, HG_WIDTH), F32),
                   jax.ShapeDtypeStruct((T // HG_CHUNK, HG_HEADS, HG_DH, HG_DH), F32)],
        scratch_shapes=[pltpu.VMEM((HG_HEADS, HG_DH, HG_DH), F32), pltpu.VMEM((nch, HG_DH, HG_DH), F32),
                        pltpu.VMEM((nch, HG_CHUNK, HG_DH), F32)],
        compiler_params=_params("arbitrary"), name=name,
    )(proj, proj, proj, proj, logits, norm_g)


def _hg_bwd(proj, o_raw, states, dmix, logits, norm_g, *, name):
    T = proj.shape[0]
    tb = min(HG_TOKENS, T)
    nch = tb // HG_CHUNK
    nb = T // tb
    rev = lambda i: nb - 1 - i

    def body(q_ref, f_ref, i_ref, gate_ref, o_ref, s_ref, dout_ref, lg_ref, ng_ref, dp_ref, dlb_ref, dng_ref,
             dst_ref, inc_ref, dec_ref, after_ref):
        @pl.when(pl.program_id(0) == 0)
        def _():
            dst_ref[...] = jnp.zeros_like(dst_ref)
            dlb_ref[...] = jnp.zeros_like(dlb_ref)
            dng_ref[...] = jnp.zeros_like(dng_ref)

        lb_all = _hg_lower_bound(lg_ref)
        row = _chunk_row(tb)
        causal = _hg_causal(tb)
        for hh in range(HG_HEADS):
            cols = slice(hh * HG_DH, (hh + 1) * HG_DH)
            out_cols = lambda part: slice(part * HG_WIDTH + hh * HG_DH, part * HG_WIDTH + (hh + 1) * HG_DH)
            o, gate, dout, ng, lb = o_ref[:, cols], gate_ref[:, cols], dout_ref[:, cols], ng_ref[:, cols], lb_all[:, cols]
            sg = jax.nn.sigmoid(gate)
            r = lax.rsqrt(jnp.mean(o * o, axis=-1, keepdims=True) + RMS_EPS)
            oh = o * r
            dp_ref[:, out_cols(3)] = dout * (oh * ng) * (sg * (1.0 + gate * (1.0 - sg)))
            don = dout * (gate * sg)
            dng_ref[:, cols] += jnp.sum(don * oh, axis=0, keepdims=True)
            doh = don * ng
            do = r * (doh - oh * jnp.mean(doh * oh, axis=-1, keepdims=True))

            t = _hg_terms(f_ref[:, cols], q_ref[:, cols], lb, row)
            v = i_ref[:, cols]
            scores = jnp.where(causal, _dot_nt(t["q_dec"], t["k_intra"]), 0.0)
            dscores = jnp.where(causal, _dot_nt(do, v), 0.0)
            inc_ref[...] = _per_chunk(_chunks(do), _chunks(t["q_dec"]), (1, 1))
            dec_ref[...] = _chunks(t["decay"])

            def step(it, dst):
                ci = nch - 1 - it
                after_ref[ci] = dst
                return dst * dec_ref[ci][0:1, :] + inc_ref[ci]

            dst_ref[hh] = lax.fori_loop(0, nch, step, dst_ref[hh], unroll=4)
            st, dst = s_ref[:, hh], after_ref[...]
            dqd = _dot(dscores, t["k_intra"]) + _per_chunk(_chunks(do), st, (2, 1)).reshape(tb, HG_DH)
            dki = _dot_tn(dscores, t["q_dec"])
            dks = _per_chunk(_chunks(v), dst, (2, 1)).reshape(tb, HG_DH)
            dp_ref[:, out_cols(2)] = _dot_tn(scores, do) + _per_chunk(_chunks(t["k_state"]), dst, (2, 2)).reshape(tb, HG_DH)
            ddecay = jnp.broadcast_to(jnp.sum(st * dst, axis=1, keepdims=True), (nch, HG_CHUNK, HG_DH)).reshape(tb, HG_DH)
            dks_ks = dks * t["k_state"]
            d_glast = _chunk_cumsum(dks_ks, row) + ddecay * t["decay"]
            d_g = dqd * t["q_dec"] - dki * t["k_intra"] - dks_ks + jnp.where(row == HG_CHUNK - 1, d_glast, 0.0)
            df = _chunk_cumsum(d_g, row, reverse=True) / t["f"] - (dki * t["e_ng"] + dks * t["e_lg"])
            dp_ref[:, out_cols(0)] = dqd * t["e_g"]
            dp_ref[:, out_cols(1)] = df * (1.0 - lb) * t["sig"] * (1.0 - t["sig"])
            dlb_ref[:, cols] += jnp.sum(df * (1.0 - t["sig"]), axis=0, keepdims=True)

    blk = pl.BlockSpec((tb, HG_WIDTH), lambda i: (rev(i), 0))
    row_spec = pl.BlockSpec((1, HG_WIDTH), lambda i: (0, 0))
    return pl.pallas_call(
        body, grid=(nb,),
        in_specs=_hg_specs(T, tb, 3, rev) + [
            blk, pl.BlockSpec((nch, HG_HEADS, HG_DH, HG_DH), lambda i: (rev(i), 0, 0, 0)),
            pl.BlockSpec((tb, HG_WIDTH), lambda i: (rev(i), 1)), pl.BlockSpec((3, HG_WIDTH), lambda i: (0, 0)), row_spec],
        out_specs=[pl.BlockSpec((tb, 4 * HG_WIDTH), lambda i: (rev(i), 0)), row_spec, row_spec],
        out_shape=[jax.ShapeDtypeStruct((T, 4 * HG_WIDTH), F32), jax.ShapeDtypeStruct((1, HG_WIDTH), F32), jax.ShapeDtypeStruct((1, HG_WIDTH), F32)],
        scratch_shapes=[pltpu.VMEM((HG_HEADS, HG_DH, HG_DH), F32), pltpu.VMEM((nch, HG_DH, HG_DH), F32),
                        pltpu.VMEM((nch, HG_CHUNK, HG_DH), F32), pltpu.VMEM((nch, HG_DH, HG_DH), F32)],
        compiler_params=_params("arbitrary"), name=name,
    )(proj, proj, proj, proj, o_raw, states, dmix, logits, norm_g)


def _shifted_copies(sh_ref, n_rows):
    keep = n_rows + CONV_HALO - 8
    for b in range(1, 8):
        sh_ref[b, 0:keep, :] = sh_ref[0, b:b + keep, :]


def _tap_rows(sh_ref, offset, r0, lanes):
    start = pl.multiple_of(r0 + (offset - offset % 8), 8)
    return sh_ref[offset % 8, pl.ds(start, CONV_ROWS), lanes]


def _conv_fwd(p, w_dw, b_dw, ln_g, ln_b, *, name):
    T, D = p.shape[0], p.shape[1] // 2
    tb = _token_block(T)
    hpb = tb // CONV_HALO
    lane_step = 512

    def body(p1_ref, p2_ref, q1_ref, q2_ref, w_ref, bdw_ref, g_ref, b_ref, a_ref, y_ref, act_ref, sh_ref):
        i = pl.program_id(0)
        a = p1_ref[...] * jax.nn.sigmoid(p2_ref[...])
        sh_ref[0, 0:CONV_HALO, :] = jnp.where(i > 0, q1_ref[...] * jax.nn.sigmoid(q2_ref[...]), 0.0)
        sh_ref[0, CONV_HALO:, :] = a
        a_ref[...] = a
        _shifted_copies(sh_ref, tb)

        def chunk(ci, _):
            r0 = pl.multiple_of(ci * CONV_ROWS, CONV_ROWS)
            for l0 in range(0, D, lane_step):
                lanes = slice(l0, l0 + lane_step)
                acc = jnp.broadcast_to(bdw_ref[:, lanes], (CONV_ROWS, lane_step))
                for k in range(CONV_WIDTH):
                    acc = acc + _tap_rows(sh_ref, CONV_HALO - CONV_WIDTH + 1 + k, r0, lanes) * w_ref[k:k + 1, lanes]
                y_ref[pl.ds(r0, CONV_ROWS), lanes] = acc
            return 0

        lax.fori_loop(0, tb // CONV_ROWS, chunk, 0)
        y = y_ref[...]
        mu = jnp.mean(y, axis=-1, keepdims=True)
        yc = y - mu
        s = yc * lax.rsqrt(jnp.mean(yc * yc, axis=-1, keepdims=True) + LN_EPS) * g_ref[...] + b_ref[...]
        act_ref[...] = (s * jax.nn.sigmoid(s)).astype(act_ref.dtype)

    prev = lambda i: jnp.maximum(i * hpb - 1, 0)
    blk = pl.BlockSpec((tb, D), lambda i: (i, 0))
    row = pl.BlockSpec((1, D), lambda i: (0, 0))
    return pl.pallas_call(
        body, grid=(T // tb,),
        in_specs=[blk, pl.BlockSpec((tb, D), lambda i: (i, 1)), pl.BlockSpec((CONV_HALO, D), lambda i: (prev(i), 0)),
                  pl.BlockSpec((CONV_HALO, D), lambda i: (prev(i), 1)), pl.BlockSpec((CONV_HALO, D), lambda i: (0, 0)), row, row, row],
        out_specs=[blk, blk, blk],
        out_shape=[jax.ShapeDtypeStruct((T, D), F32), jax.ShapeDtypeStruct((T, D), F32), jax.ShapeDtypeStruct((T, D), MXU_DTYPE)],
        scratch_shapes=[pltpu.VMEM((8, tb + CONV_HALO, D), F32)],
        compiler_params=_params("parallel"), name=name,
    )(p, p, p, p, w_dw, b_dw, ln_g, ln_b)


def _conv_bwd_norm(dact, y, ln_g, ln_b, *, name):
    T, D = y.shape
    tb = _token_block(T)

    def body(da_ref, y_ref, g_ref, b_ref, dy_ref, dg_ref, db_ref, cs_ref):
        @pl.when(pl.program_id(0) == 0)
        def _():
            dg_ref[...] = jnp.zeros_like(dg_ref)
            db_ref[...] = jnp.zeros_like(db_ref)
            cs_ref[...] = jnp.zeros_like(cs_ref)

        y, g = y_ref[...], g_ref[...]
        yc = y - jnp.mean(y, axis=-1, keepdims=True)
        rs = lax.rsqrt(jnp.mean(yc * yc, axis=-1, keepdims=True) + LN_EPS)
        yn = yc * rs
        s = yn * g + b_ref[...]
        sg = jax.nn.sigmoid(s)
        ds = da_ref[...] * (sg * (1.0 + s * (1.0 - sg)))
        dg_ref[...] += jnp.sum(ds * yn, axis=0, keepdims=True)
        db_ref[...] += jnp.sum(ds, axis=0, keepdims=True)
        dyn = ds * g
        dy = rs * (dyn - jnp.mean(dyn, axis=-1, keepdims=True) - yn * jnp.mean(dyn * yn, axis=-1, keepdims=True))
        dy_ref[...] = dy
        cs_ref[...] += jnp.sum(dy, axis=0, keepdims=True)

    blk = pl.BlockSpec((tb, D), lambda i: (i, 0))
    row = pl.BlockSpec((1, D), lambda i: (0, 0))
    rs_ = jax.ShapeDtypeStruct((1, D), F32)
    return pl.pallas_call(
        body, grid=(T // tb,), in_specs=[blk, blk, row, row], out_specs=[blk, row, row, row],
        out_shape=[jax.ShapeDtypeStruct((T, D), F32), rs_, rs_, rs_], compiler_params=_params("arbitrary"), name=name,
    )(dact, y, ln_g, ln_b)


def _conv_bwd_taps(dy, a, p, w_dw, *, name):
    T, D = dy.shape
    tb = _token_block(T)
    hpb = tb // CONV_HALO
    last = T // CONV_HALO - 1
    nb = T // tb
    lane_step = 128
    groups = CONV_ROWS // 8

    def body(dy_ref, dyn_ref, a_ref, p1_ref, p2_ref, w_ref, dp_ref, dw_ref, cs_ref, sh_ref, da_ref):
        i = pl.program_id(0)

        @pl.when(i == 0)
        def _():
            dw_ref[...] = jnp.zeros_like(dw_ref)
            cs_ref[...] = jnp.zeros_like(cs_ref)

        sh_ref[0, 0:tb, :] = dy_ref[...]
        sh_ref[0, tb:, :] = jnp.where(i < nb - 1, dyn_ref[...], 0.0)
        _shifted_copies(sh_ref, tb)
        for l0 in range(0, D, lane_step):
            lanes = slice(l0, l0 + lane_step)
            for taps in (range(0, CONV_WIDTH // 2 + 1), range(CONV_WIDTH // 2 + 1, CONV_WIDTH)):
                def chunk(ci, sums, taps=taps):
                    r0 = pl.multiple_of(ci * CONV_ROWS, CONV_ROWS)
                    a_c = a_ref[pl.ds(r0, CONV_ROWS), lanes]
                    da = jnp.zeros((CONV_ROWS, lane_step), F32) if taps[0] == 0 else da_ref[pl.ds(r0, CONV_ROWS), lanes]
                    new = []
                    for n, k in enumerate(taps):
                        s_k = _tap_rows(sh_ref, CONV_WIDTH - 1 - k, r0, lanes)
                        da = da + s_k * w_ref[k:k + 1, lanes]
                        new.append(sums[n] + jnp.sum((s_k * a_c).reshape(groups, 8, lane_step), axis=0))
                    da_ref[pl.ds(r0, CONV_ROWS), lanes] = da
                    return tuple(new)

                sums = lax.fori_loop(0, tb // CONV_ROWS, chunk, tuple(jnp.zeros((8, lane_step), F32) for _ in taps))
                for n, k in enumerate(taps):
                    dw_ref[k:k + 1, lanes] += jnp.sum(sums[n], axis=0, keepdims=True)
        da = da_ref[...]
        p1 = p1_ref[...]
        sg = jax.nn.sigmoid(p2_ref[...])
        dp1 = da * sg
        dp2 = da * p1 * (sg * (1.0 - sg))
        dp_ref[:, 0:D] = dp1.astype(dp_ref.dtype)
        dp_ref[:, D:] = dp2.astype(dp_ref.dtype)
        cs_ref[:, 0:D] += jnp.sum(dp1, axis=0, keepdims=True)
        cs_ref[:, D:] += jnp.sum(dp2, axis=0, keepdims=True)

    blk = pl.BlockSpec((tb, D), lambda i: (i, 0))
    return pl.pallas_call(
        body, grid=(nb,),
        in_specs=[blk, pl.BlockSpec((CONV_HALO, D), lambda i: (jnp.minimum((i + 1) * hpb, last), 0)), blk, blk,
                  pl.BlockSpec((tb, D), lambda i: (i, 1)), pl.BlockSpec((CONV_HALO, D), lambda i: (0, 0))],
        out_specs=[pl.BlockSpec((tb, 2 * D), lambda i: (i, 0)), pl.BlockSpec((CONV_HALO, D), lambda i: (0, 0)), pl.BlockSpec((1, 2 * D), lambda i: (0, 0))],
        out_shape=[jax.ShapeDtypeStruct((T, 2 * D), MXU_DTYPE), jax.ShapeDtypeStruct((CONV_HALO, D), F32), jax.ShapeDtypeStruct((1, 2 * D), F32)],
        scratch_shapes=[pltpu.VMEM((8, tb + CONV_HALO, D), F32), pltpu.VMEM((tb, D), F32)],
        compiler_params=_params("arbitrary"), name=name,
    )(dy, dy, a, p, p, w_dw)


def _row_block(rows):
    for tr in (512, 256, 128, 64, 32, 16, 8):
        if rows % tr == 0:
            return tr
    return rows


def _sum_leading(x, *, name):
    n, R, C = x.shape
    tr = _row_block(R)

    def body(x_ref, o_ref):
        acc = x_ref[0]
        for j in range(1, n):
            acc = acc + x_ref[j]
        o_ref[...] = acc

    return pl.pallas_call(
        body, grid=(R // tr,), in_specs=[pl.BlockSpec((n, tr, C), lambda i: (0, i, 0))], out_specs=pl.BlockSpec((tr, C), lambda i: (i, 0)),
        out_shape=jax.ShapeDtypeStruct((R, C), x.dtype), compiler_params=_params("parallel"), name=name,
    )(x)


def _add_pair(x, y, *, name):
    n, R, C = x.shape
    tr = _row_block(R)

    def body(x_ref, y_ref, o_ref):
        o_ref[...] = x_ref[...] + y_ref[...]

    blk = pl.BlockSpec((1, tr, C), lambda j, i: (j, i, 0))
    return pl.pallas_call(
        body, grid=(n, R // tr), in_specs=[blk, blk], out_specs=blk,
        out_shape=jax.ShapeDtypeStruct((n, R, C), x.dtype), compiler_params=_params("parallel", "parallel"), name=name,
    )(x, y)


def _adamw(w, g, m, v, *, name):
    R, C = w.shape
    tr = _row_block(R)
    c1, c2 = 1.0 - ADAM_B1 ** ADAM_STEP, 1.0 - ADAM_B2 ** ADAM_STEP

    def body(w_ref, g_ref, m_ref, v_ref, d_ref, nm_ref, nv_ref):
        g_ = g_ref[...]
        nm = ADAM_B1 * m_ref[...] + (1.0 - ADAM_B1) * g_
        nv = ADAM_B2 * v_ref[...] + (1.0 - ADAM_B2) * (g_ * g_)
        d_ref[...] = -ADAM_LR * ((nm / c1) / (jnp.sqrt(nv / c2) + ADAM_EPS) + ADAM_WD * w_ref[...])
        nm_ref[...] = nm
        nv_ref[...] = nv

    blk = pl.BlockSpec((tr, C), lambda i: (i, 0))
    shp = jax.ShapeDtypeStruct((R, C), F32)
    return pl.pallas_call(
        body, grid=(R // tr,), in_specs=[blk] * 4, out_specs=[blk] * 3, out_shape=[shp] * 3,
        compiler_params=_params("parallel"), name=name,
    )(w, g, m, v)


def _small_reduce(packs, logits, *, name):
    n, R, C = packs.shape

    def body(p_ref, lg_ref, s_ref, dlg_ref):
        acc = p_ref[0]
        for j in range(1, n):
            acc = acc + p_ref[j]
        s_ref[...] = acc
        lg = lg_ref[...]
        e = jnp.exp(lg - jnp.max(lg, axis=0, keepdims=True))
        sm = e / jnp.sum(e, axis=0, keepdims=True)
        dlb = acc[5:6, HG_WIDTH:2 * HG_WIDTH]
        first = lax.broadcasted_iota(jnp.int32, sm.shape, 0) == 0
        dlg_ref[...] = sm[0:1, :] * (jnp.where(first, 1.0, 0.0) - sm) * dlb

    whole = lambda shape: pl.BlockSpec(shape, lambda: (0,) * len(shape))
    return pl.pallas_call(
        body, in_specs=[whole((n, R, C)), whole(logits.shape)], out_specs=[whole((R, C)), whole(logits.shape)],
        out_shape=[jax.ShapeDtypeStruct((R, C), F32), jax.ShapeDtypeStruct(logits.shape, F32)],
        compiler_params=pltpu.CompilerParams(vmem_limit_bytes=VMEM_LIMIT), name=name,
    )(packs, logits)


HBM_SPEC = pl.BlockSpec(memory_space=pl.ANY)


def _place():
    return lax.axis_index("x"), lax.axis_index("y"), lax.axis_index("c")


class _Copies:
    def __init__(self, arrays, out_shapes, n_copies, make, finish):
        self.arrays, self.out_shapes, self.n_copies, self.make, self.finish = list(arrays), list(out_shapes), n_copies, make, finish

    def scratch(self):
        return [pltpu.SemaphoreType.DMA((self.n_copies,)), pltpu.SemaphoreType.DMA((self.n_copies,))]

    def run(self, name):
        n = len(self.arrays)

        def body(*refs):
            copies = self.make(refs[:n], refs[n:2 * n], *refs[2 * n:])
            for cp in copies:
                cp.start()
            for cp in copies:
                cp.wait()

        outs = pl.pallas_call(body, in_specs=[HBM_SPEC] * n, out_specs=[HBM_SPEC] * n, out_shape=self.out_shapes,
                              scratch_shapes=self.scratch(), name=name)(*self.arrays)
        return self.finish(outs)


def _remote(src, dst, send_sems, recv_sems, k, peer):
    return pltpu.make_async_remote_copy(src_ref=src, dst_ref=dst, send_sem=send_sems.at[k], recv_sem=recv_sems.at[k],
                                        device_id=peer, device_id_type=MESH)


def _same_core_peers(x, y, c):
    return [(1 - x, y, c), (x, 1 - y, c), (1 - x, 1 - y, c)]


def _all_peers(x, y, c):
    flip = lambda v, b: 1 - v if b else v
    return [(flip(x, r & 4), flip(y, r & 2), flip(c, r & 1)) for r in range(1, 8)]


def _gather(arrays, peers_of, slot_of, n_slots):
    n_peers = len(peers_of(0, 0, 0))

    def make(ins, outs, send_sems, recv_sems):
        x, y, c = _place()
        slot = slot_of(x, y, c)
        return [_remote(ins[a], outs[a].at[slot], send_sems, recv_sems, a * n_peers + k, peer)
                for a in range(len(arrays)) for k, peer in enumerate(peers_of(x, y, c))]

    def finish(outs):
        slot = slot_of(*_place())
        return [lax.dynamic_update_index_in_dim(o, a, slot, 0) for o, a in zip(outs, arrays)]

    shapes = [jax.ShapeDtypeStruct((n_slots,) + a.shape, a.dtype) for a in arrays]
    return _Copies(arrays, shapes, len(arrays) * n_peers, make, finish)


def _gather_chips(arrays):
    return _gather(arrays, _same_core_peers, lambda x, y, c: 2 * x + y, N_CHIPS)


def _gather_all(arrays):
    return _gather(arrays, _all_peers, lambda x, y, c: 4 * x + 2 * y + c, N_DEV)


def _pair_swap(a):
    def make(ins, outs, send_sems, recv_sems):
        x, y, c = _place()
        return [_remote(ins[0].at[1 - c], outs[0], send_sems, recv_sems, 0, (x, y, 1 - c))]

    return _Copies([a], [jax.ShapeDtypeStruct(a.shape[1:], a.dtype)], 1, make, lambda outs: outs[0])


def _chip_scatter(p):
    def make(ins, outs, send_sems, recv_sems):
        x, y, c = _place()
        return [_remote(ins[0].at[2 * px + py], outs[0].at[2 * x + y], send_sems, recv_sems, k, (px, py, pc))
                for k, (px, py, pc) in enumerate(_same_core_peers(x, y, c))]

    def finish(outs):
        x, y, _ = _place()
        me = 2 * x + y
        return lax.dynamic_update_index_in_dim(outs[0], lax.dynamic_index_in_dim(p, me, 0, keepdims=False), me, 0)

    return _Copies([p], [jax.ShapeDtypeStruct(p.shape, p.dtype)], 3, make, finish)


def _owner_scatter(blocks):
    def make(ins, outs, send_sems, recv_sems):
        x, y, c = _place()
        return [_remote(ins[0].at[2 * px + py, pc], outs[0].at[4 * x + 2 * y + c], send_sems, recv_sems, k, (px, py, pc))
                for k, (px, py, pc) in enumerate(_all_peers(x, y, c))]

    def finish(outs):
        x, y, c = _place()
        mine = lax.dynamic_index_in_dim(lax.dynamic_index_in_dim(blocks, 2 * x + y, 0, keepdims=False), c, 0, keepdims=False)
        return lax.dynamic_update_index_in_dim(outs[0], mine, 4 * x + 2 * y + c, 0)

    return _Copies([blocks], [jax.ShapeDtypeStruct((N_DEV,) + blocks.shape[2:], blocks.dtype)], N_DEV - 1, make, finish)


def _pair_gather(q):
    def make(ins, outs, send_sems, recv_sems):
        x, y, c = _place()
        return [_remote(ins[0], outs[0].at[c], send_sems, recv_sems, 0, (x, y, 1 - c))]

    return _Copies([q], [jax.ShapeDtypeStruct((2,) + q.shape, q.dtype)], 1, make,
                   lambda outs: lax.dynamic_update_index_in_dim(outs[0], q, _place()[2], 0))


def _grad_blocks(dw, kind):
    if kind == "cols2d":
        K, N = dw.shape
        b = dw.reshape(2, K // 2, N_CHIPS, N // N_CHIPS).transpose(2, 0, 1, 3)
    elif kind == "rows2d":
        b = dw.reshape(N_CHIPS, 2, dw.shape[0] // 8, dw.shape[1])
    elif kind == "cols3d":
        L, K, N = dw.shape
        b = dw.reshape(L, K, N_CHIPS, N // N_CHIPS).transpose(2, 0, 1, 3)
    else:
        L, K, N = dw.shape
        b = dw.reshape(L, N_CHIPS, K // N_CHIPS, N).transpose(1, 0, 2, 3)
    return b.reshape(N_CHIPS, 2, -1, D_MODEL)


def _pad_rows(a, rows):
    return jnp.concatenate([a, jnp.zeros((rows - a.shape[0],) + a.shape[1:], a.dtype)], axis=0)


def _forward_backward(x, target, W, late_weights=None, early_grads=None, last_grads=None):
    row = lambda a: a.reshape(1, -1)
    relu2 = lambda acc: (jnp.square(jnp.maximum(acc, 0.0)),)
    normed = lambda h, g: h * lax.rsqrt(jnp.mean(h * h, axis=-1, keepdims=True) + RMS_EPS) * g

    def residual_norm(acc, res, g):
        h = res + acc
        return h, normed(h, g)

    def norm_bwd(du, h_blk, dres_blk, g):
        dx, dg_terms = _rms_bwd_math(h_blk, g, du)
        dh = dres_blk + dx
        return dh, dh, jnp.sum(dg_terms, axis=0, keepdims=True), jnp.sum(dh, axis=0, keepdims=True)

    def matmul_norm_bwd(dy, w, h_in, g, dres, rider=None, *, tk=1024, name):
        return _matmul(dy, w, mode="nt", out_dtypes=[F32, MXU_DTYPE], epilogue=norm_bwd, tiles=[h_in, dres], rows=[row(g)],
                       n_sums=2, rider=rider, tk=tk, name=name)

    G = {}

    u0 = _rmsnorm_fwd(x, row(W["norm_mix_g"][0]), name="norm_mix0")
    proj, qkv = _matmul(u0, W["w_in"], mode="nn", out_dtypes=[F32, MXU_DTYPE], epilogue=lambda acc: (acc, acc), tn=896, name="in_proj")
    o_sb, got = _sb_fwd(qkv, late_weights and late_weights[0], name="sb_fwd")
    if late_weights:
        W = {**W, **late_weights[1](got)}
    hg_out, hg_o, hg_states = _hg_fwd(proj, W["hg_lb_logits"], row(W["hg_norm_g"]), name="hg_fwd")
    mix = jnp.concatenate([o_sb, hg_out], axis=-1)
    h1, u1 = _matmul(mix, W["w_out"], mode="nn", out_dtypes=[F32, MXU_DTYPE], epilogue=residual_norm, tiles=[x],
                     rows=[row(W["norm_ffn_g"][0])], name="out_proj")
    r0 = _matmul(u1, W["w_ff1"][0], mode="nn", out_dtypes=[MXU_DTYPE], epilogue=relu2, name="ff1_0")
    h2, u2 = _matmul(r0, W["w_ff2"][0], mode="nn", out_dtypes=[F32, MXU_DTYPE], epilogue=residual_norm, tiles=[h1],
                     rows=[row(W["norm_mix_g"][1])], name="ff2_0")
    p = _matmul(u2, W["w_glu"], mode="nn", out_dtypes=[F32], epilogue=lambda acc, b: (acc + b,), rows=[row(W["b_glu"])], name="glu_proj")
    w_dw = _pad_rows(W["w_dw"], CONV_HALO)
    ca, cy, cact = _conv_fwd(p, w_dw, row(W["b_dw"]), row(W["ln_g"]), row(W["ln_b"]), name="conv_fwd")
    h3, u3 = _matmul(cact, W["w_pw"], mode="nn", out_dtypes=[F32, MXU_DTYPE], epilogue=lambda acc, res, b, g: residual_norm(acc + b, res, g),
                     tiles=[h2], rows=[row(W["b_pw"]), row(W["norm_ffn_g"][1])], name="pw_proj")
    r1 = _matmul(u3, W["w_ff1"][1], mode="nn", out_dtypes=[MXU_DTYPE], epilogue=relu2, name="ff1_1")
    h4 = _matmul(r1, W["w_ff2"][1], mode="nn", out_dtypes=[F32], epilogue=lambda acc, res: (res + acc,), tiles=[h3], name="ff2_1")

    dh4, dh4_m, G["final_norm_g"], loss = _loss_head(h4, row(W["final_norm_g"]), target, name="loss_head")

    def mlp_bwd(dh, dh_m, h_in, u, r, layer, tag):
        d_relu2 = lambda acc, r_blk: (acc * (2.0 * jnp.sqrt(r_blk.astype(F32))),)
        da = _matmul(dh_m, W["w_ff2"][layer], mode="nt", out_dtypes=[MXU_DTYPE], epilogue=d_relu2, tiles=[r], name="d_ff2_act" + tag)
        dw2 = _matmul(r, dh_m, mode="tn", out_dtypes=[F32], tk=WGRAD_TOKENS, name="d_ff2_w" + tag)
        dw1 = _matmul(u, da, mode="tn", out_dtypes=[F32], tk=WGRAD_TOKENS, name="d_ff1_w" + tag)
        dh_in, dh_in_m, dg, cs = matmul_norm_bwd(da, W["w_ff1"][layer], h_in, W["norm_ffn_g"][layer], dh, name="d_ff1_act" + tag)
        return dh_in, dh_in_m, dg, cs, dw1, dw2

    dh3, dh3_m, dg_ffn1, cs_h3, dw1_1, dw2_1 = mlp_bwd(dh4, dh4_m, h3, u3, r1, 1, "1")
    G["b_pw"] = cs_h3
    dact = _matmul(dh3_m, W["w_pw"], mode="nt", out_dtypes=[F32], name="d_pw_act")
    G["w_pw"] = _matmul(cact, dh3_m, mode="tn", out_dtypes=[F32], tk=WGRAD_TOKENS, name="d_pw_w")
    dy, G["ln_g"], G["ln_b"], G["b_dw"] = _conv_bwd_norm(dact, cy, row(W["ln_g"]), row(W["ln_b"]), name="d_conv_norm")
    dp, G["w_dw"], G["b_glu"] = _conv_bwd_taps(dy, ca, p, w_dw, name="d_conv_taps")
    G["w_glu"] = _matmul(u2, dp, mode="tn", out_dtypes=[F32], tk=WGRAD_TOKENS, name="d_glu_w")
    dh2, dh2_m, dg_mix1, _ = matmul_norm_bwd(dp, W["w_glu"], h2, W["norm_mix_g"][1], dh3, name="d_glu_act")
    dh1, dh1_m, dg_ffn0, _, dw1_0, dw2_0 = mlp_bwd(dh2, dh2_m, h1, u1, r0, 0, "0")
    G["w_ff1"], G["w_ff2"] = jnp.stack([dw1_0, dw1_1]), jnp.stack([dw2_0, dw2_1])
    G["norm_ffn_g"] = jnp.concatenate([dg_ffn0, dg_ffn1], axis=0)
    dmix = _matmul(dh1_m, W["w_out"], mode="nt", out_dtypes=[F32], name="d_out_act")
    G["w_out"] = _matmul(mix, dh1_m, mode="tn", out_dtypes=[F32], tk=WGRAD_TOKENS, name="d_out_w")
    riding = early_grads(G) if early_grads else None
    (dsq, dsk, dsv), got = _sb_bwd(qkv, dmix, riding, name="sb_bwd")
    d_hg, G["hg_lb"], G["hg_norm_g"] = _hg_bwd(proj, hg_o, hg_states, dmix, W["hg_lb_logits"], row(W["hg_norm_g"]), name="hg_bwd")
    dproj = jnp.concatenate([dsq, dsk, dsv, d_hg], axis=-1).astype(MXU_DTYPE)
    G["w_in"] = _matmul(u0, dproj, mode="tn", out_dtypes=[F32], tn=896, tk=WGRAD_TOKENS, name="d_in_w")
    last = last_grads(G) if last_grads else None
    res = matmul_norm_bwd(dproj, W["w_in"], x, W["norm_mix_g"][0], dh1, last, tk=896, name="d_in_act")
    (dx, _, dg_mix0, _), got_last = res if last_grads else (res, [])
    G["norm_mix_g"] = jnp.concatenate([dg_mix0, dg_mix1], axis=0)
    return loss, dx, G, [(riding, got), (last, got_last)]


BIG = (("w_out_ab", "w_out", "rows2d"), ("conv_w_glu", "w_glu", "cols2d"), ("conv_w_pw", "w_pw", "rows2d"),
       ("w_ff1", "w_ff1", "cols3d"), ("w_ff2", "w_ff2", "rows3d"), ("w_in_ab", "w_in", "cols2d"))
LATE = BIG[:-1]
SMALL_SHARDED = ("conv_b_glu", "conv_w_dw", "conv_b_dw", "conv_ln_g", "conv_ln_b", "conv_b_pw")
REPLICATED = ("norm_mix_g", "norm_ffn_g", "hg_lb_logits", "hg_norm_g", "final_norm_g")
ORDER = ("norm_mix_g", "norm_ffn_g", "w_in_ab", "w_out_ab", "hg_lb_logits", "hg_norm_g", "conv_w_glu", "conv_b_glu",
         "conv_w_dw", "conv_b_dw", "conv_ln_g", "conv_ln_b", "conv_w_pw", "conv_b_pw", "w_ff1", "w_ff2", "final_norm_g")


def _step(x, loss_target, w, m, v):
    D = D_MODEL
    x2, t2 = x.reshape(-1, D), loss_target.reshape(-1, D)
    chip = 2 * lax.axis_index("x") + lax.axis_index("y")
    c = lax.axis_index("c")

    small_in = jnp.concatenate([w["conv_b_glu"].reshape(2, SHARD), w["conv_w_dw"].reshape(CONV_WIDTH, SHARD)] +
                               [w[n].reshape(1, SHARD) for n in ("conv_b_dw", "conv_ln_g", "conv_ln_b", "conv_b_pw")], axis=0)
    g_in, gs = _gather_chips([w["w_in_ab"].astype(MXU_DTYPE), _pad_rows(small_in, SMALL_IN_ROWS)]).run("gather_first_weights")
    vec = lambda r0, r1: gs[:, r0:r1].transpose(1, 0, 2).reshape(r1 - r0, D)
    W = {
        "w_in": _ChipWeight(g_in[:, 0], "cols"),
        "b_glu": gs[:, 0:2].reshape(2 * D), "w_dw": vec(2, 33), "b_dw": vec(33, 34)[0], "ln_g": vec(34, 35)[0],
        "ln_b": vec(35, 36)[0], "b_pw": vec(36, 37)[0],
        "norm_mix_g": w["norm_mix_g"], "norm_ffn_g": w["norm_ffn_g"], "hg_lb_logits": w["hg_lb_logits"],
        "hg_norm_g": w["hg_norm_g"], "final_norm_g": w["final_norm_g"],
    }
    late = _gather_chips([w[n].astype(MXU_DTYPE) for n, _, _ in LATE])

    def assemble(got):
        gw = dict(zip([s for _, s, _ in LATE], late.finish(got)))
        layers = lambda g, along: [_ChipWeight(g, along, (layer,)) for layer in range(2)]
        return {"w_out": gw["w_out"].reshape(D, D), "w_glu": _ChipWeight(gw["w_glu"][:, 0], "cols"), "w_pw": gw["w_pw"].reshape(D, D),
                "w_ff1": layers(gw["w_ff1"], "cols"), "w_ff2": layers(gw["w_ff2"], "rows")}

    def early_grads(G):
        return _owner_scatter(jnp.concatenate([_grad_blocks(G[s], kind) for _, s, kind in LATE], axis=2))

    def last_grads(G):
        blocks = _grad_blocks(G["w_in"], "cols2d")
        from_pair = _pair_swap(blocks.transpose(1, 0, 2, 3)).run("grads_pair_swap_in")
        return _chip_scatter(_add_pair(lax.dynamic_index_in_dim(blocks, c, axis=1, keepdims=False), from_pair, name="grads_pair_add_in"))

    loss, dx, G, riders = _forward_backward(x2, t2, W, (late, assemble), early_grads, last_grads)
    halves = [_sum_leading(copies.finish(got), name="grads_add_" + tag) for (copies, got), tag in zip(riders, ("late", "in"))]
    half = jnp.concatenate(halves, axis=0)
    full = _pair_gather(half).run("grads_pair_gather")

    pack = jnp.concatenate([
        G["norm_mix_g"], G["norm_ffn_g"], G["final_norm_g"], jnp.concatenate([G["hg_norm_g"], G["hg_lb"]], axis=1),
        _pad_rows(jnp.broadcast_to(loss, (1, D)), 2), G["b_glu"].reshape(2, D), G["w_dw"], G["b_dw"], G["ln_g"], G["ln_b"], G["b_pw"],
    ], axis=0)
    pack = _pad_rows(pack, SMALL_ROWS)
    (packs,) = _gather_all([pack]).run("gather_small_grads")
    ssum, d_logits = _small_reduce(packs, w["hg_lb_logits"], name="reduce_small_grads")
    cut = lambda r0, r1: lax.dynamic_slice(ssum, (r0, chip * SHARD), (r1 - r0, SHARD))
    grads = {
        "norm_mix_g": ssum[0:2], "norm_ffn_g": ssum[2:4], "final_norm_g": ssum[4], "hg_norm_g": ssum[5, :HG_WIDTH].reshape(1, HG_HEADS, HG_DH),
        "hg_lb_logits": d_logits,
        "conv_b_glu": lax.dynamic_slice(ssum[8:10].reshape(1, 2 * D), (0, chip * 2 * SHARD), (1, 2 * SHARD)),
        "conv_w_dw": cut(10, 10 + CONV_WIDTH).reshape(1, CONV_WIDTH, SHARD),
        "conv_b_dw": cut(42, 43), "conv_ln_g": cut(43, 44), "conv_ln_b": cut(44, 45), "conv_b_pw": cut(45, 46),
    }
    loss_out = ssum[6, 0]

    off = 0
    for n, s, kind in BIG:
        shard = w[n].shape
        rows = w[n].size // (2 * D)
        grads[n] = full[:, off:off + rows].reshape(shard)
        off += rows

    delta, new_m, new_v = {}, {}, {}
    for n, _, _ in BIG:
        view = lambda a: a.reshape(-1, a.shape[-1])
        outs = _adamw(view(w[n]), view(grads[n]), view(m[n]), view(v[n]), name="adamw_" + n)
        delta[n], new_m[n], new_v[n] = (o.reshape(w[n].shape) for o in outs)
    small = SMALL_SHARDED + REPLICATED
    sizes = [w[n].size for n in small]
    total = sum(sizes)
    rows = -(-total // (8 * D)) * 8
    packed = lambda d: _pad_rows(jnp.concatenate([d[n].reshape(-1) for n in small]).reshape(-1, 128), rows * 8).reshape(rows, D)
    outs = _adamw(packed(w), packed(grads), packed(m), packed(v), name="adamw_small")
    off = 0
    for n, size in zip(small, sizes):
        delta[n], new_m[n], new_v[n] = (o.reshape(-1)[off:off + size].reshape(w[n].shape) for o in outs)
        off += size
    grads = {n: grads[n].reshape(w[n].shape) for n in ORDER}
    return (loss_out, dx.reshape(x.shape), *[grads[n] for n in ORDER], *[delta[n] for n in ORDER],
            *[new_m[n] for n in ORDER], *[new_v[n] for n in ORDER])


def kernel(x, norm_mix_g, norm_ffn_g, w_in_ab, w_out_ab, hg_lb_logits, hg_norm_g, conv_w_glu, conv_b_glu, conv_w_dw, conv_b_dw, conv_ln_g, conv_ln_b, conv_w_pw, conv_b_pw, w_ff1, w_ff2, final_norm_g, loss_target, m_norm_mix_g, m_norm_ffn_g, m_w_in_ab, m_w_out_ab, m_hg_lb_logits, m_hg_norm_g, m_conv_w_glu, m_conv_b_glu, m_conv_w_dw, m_conv_b_dw, m_conv_ln_g, m_conv_ln_b, m_conv_w_pw, m_conv_b_pw, m_w_ff1, m_w_ff2, m_final_norm_g, v_norm_mix_g, v_norm_ffn_g, v_w_in_ab, v_w_out_ab, v_hg_lb_logits, v_hg_norm_g, v_conv_w_glu, v_conv_b_glu, v_conv_w_dw, v_conv_b_dw, v_conv_ln_g, v_conv_ln_b, v_conv_w_pw, v_conv_b_pw, v_w_ff1, v_w_ff2, v_final_norm_g):
    args = locals()
    w = {n: args[n] for n in ORDER}
    m = {n: args["m_" + n] for n in ORDER}
    v = {n: args["v_" + n] for n in ORDER}
    return _step(x, loss_target, w, m, v)
```

```python
import functools

import jax
import jax.numpy as jnp
from jax import lax
from jax.experimental import pallas as pl
from jax.experimental.pallas import tpu as pltpu

F32 = jnp.float32
MXU_DTYPE = jnp.bfloat16
MESH = pl.DeviceIdType.MESH

D_MODEL = 1024
SB_HEADS, SB_DH, SB_WIDTH = 8, 64, 512
SB_KEYS = 512
SB_SUB = 256
SB_ROWS_FWD, SB_ROWS_BWD = 512, 256
HG_HEADS, HG_DH, HG_WIDTH = 4, 128, 512
HG_CHUNK = 16
HG_TOKENS = 256
CONV_WIDTH = 31
CONV_HALO = 32
CONV_ROWS = 32
RMS_EPS = 1e-6
LN_EPS = 1e-5
N_CHIPS = 4
N_DEV = 8
SHARD = D_MODEL // N_CHIPS
SMALL_IN_ROWS = 40
SMALL_ROWS = 48
WGRAD_TOKENS = 2048
VMEM_LIMIT = 56 * 1024 * 1024

ADAM_LR, ADAM_B1, ADAM_B2, ADAM_EPS, ADAM_WD, ADAM_STEP = 0.001, 0.9, 0.999, 1e-08, 0.01, 10


def _params(*sem):
    return pltpu.CompilerParams(dimension_semantics=sem, vmem_limit_bytes=VMEM_LIMIT)


def _mx(v):
    return v.astype(MXU_DTYPE)


def _dot(a, b):
    return jnp.dot(_mx(a), _mx(b), preferred_element_type=F32)


def _dot_nt(a, b):
    return lax.dot_general(_mx(a), _mx(b), (((1,), (1,)), ((), ())), preferred_element_type=F32)


def _dot_tn(a, b):
    return lax.dot_general(_mx(a), _mx(b), (((0,), (0,)), ((), ())), preferred_element_type=F32)


def _neg_abs(x):
    bits = lax.bitcast_convert_type(x, jnp.uint32) | jnp.uint32(0x80000000)
    return lax.bitcast_convert_type(bits, F32)


def _key_order_sums(v, tri2, later):
    hi = _mx(v)
    lo = _mx(v - hi.astype(F32))
    n = v.shape[1] // SB_SUB
    blocks = [slice(b * SB_SUB, (b + 1) * SB_SUB) for b in range(n)]
    totals = [jnp.sum(v[:, sl], axis=1, keepdims=True) for sl in blocks]
    far, running = [None] * n, None
    for b in (reversed(range(n)) if later else range(n)):
        far[b] = running
        running = totals[b] if running is None else running + totals[b]
    sums = []
    for b, sl in enumerate(blocks):
        inside = jnp.dot(jnp.concatenate([hi[:, sl], lo[:, sl]], axis=1), tri2, preferred_element_type=F32)
        sums.append(inside if far[b] is None else inside + far[b])
    return jnp.concatenate(sums, axis=1), running


class _ChipWeight:
    def __init__(self, parts, along, lead=()):
        self.parts, self.along, self.lead = parts, along, tuple(lead)
        r, c = parts.shape[-2:]
        self.shape = (r, N_CHIPS * c) if along == "cols" else (N_CHIPS * r, c)

    def _gathered_is_n(self, mode):
        return (self.along == "cols") == (mode in ("nn", "tn"))

    def tile(self, mode, tn, tk):
        r, c = self.parts.shape[-2:]
        part = c if self.along == "cols" else r
        return (part, tk) if self._gathered_is_n(mode) else (tn, part)

    def spec(self, mode, tn, tk):
        squeezed = (None,) * (1 + len(self.lead))
        lead, cols, by_n = self.lead, self.along == "cols", self._gathered_is_n(mode)
        block = (tn, tk) if mode == "nt" else (tk, tn)

        def index(i, j, k):
            chip, other = (j, k) if by_n else (k, j)
            return (chip,) + lead + ((other, 0) if cols else (0, other))

        return pl.BlockSpec(squeezed + block, index)


def _matmul(a, b, *, mode, out_dtypes, epilogue=None, tiles=(), rows=(), n_sums=0, rider=None, tm=1024, tn=1024, tk=1024, name):
    b_shape = b.shape
    if mode == "nn":
        (M, K), N = a.shape, b_shape[1]
    elif mode == "nt":
        (M, K), N = a.shape, b_shape[0]
    else:
        (K, M), N = a.shape, b_shape[1]
    if isinstance(b, _ChipWeight):
        tn, tk = b.tile(mode, tn, tk)
    tm, tn, tk = min(tm, M), min(tn, N), min(tk, K)
    assert M % tm == 0 and N % tn == 0 and K % tk == 0, (name, M, N, K)
    nk = K // tk
    a_spec = pl.BlockSpec((tk, tm), lambda i, j, k: (k, i)) if mode == "tn" else pl.BlockSpec((tm, tk), lambda i, j, k: (i, k))
    if isinstance(b, _ChipWeight):
        b_spec, b = b.spec(mode, tn, tk), b.parts
    else:
        b_spec = pl.BlockSpec((tn, tk), lambda i, j, k: (j, k)) if mode == "nt" else pl.BlockSpec((tk, tn), lambda i, j, k: (k, j))
    dims = {"nn": ((1,), (0,)), "nt": ((1,), (1,)), "tn": ((0,), (0,))}[mode]
    n_t, n_r, n_o = len(tiles), len(rows), len(out_dtypes)
    n_x = 0 if rider is None else len(rider.arrays)
    grid = (M // tm, N // tn, nk)
    if epilogue is None:
        epilogue = lambda acc: (acc,)

    def body(a_ref, b_ref, *rest):
        extra, x_in, rest = rest[:n_t + n_r], rest[n_t + n_r:n_t + n_r + n_x], rest[n_t + n_r + n_x:]
        outs, sums, x_out, acc_ref, sems = rest[:n_o], rest[n_o:n_o + n_sums], rest[n_o + n_sums:n_o + n_sums + n_x], rest[n_o + n_sums + n_x], rest[n_o + n_sums + n_x + 1:]
        i, j, k = (pl.program_id(d) for d in range(3))
        if rider is not None:
            @pl.when((i == 0) & (j == 0) & (k == 0))
            def _():
                for cp in rider.make(x_in, x_out, *sems):
                    cp.start()

        @pl.when(k == 0)
        def _():
            acc_ref[...] = jnp.zeros_like(acc_ref)

        acc_ref[...] += lax.dot_general(_mx(a_ref[...]), _mx(b_ref[...]), (dims, ((), ())), preferred_element_type=F32)

        @pl.when(k == nk - 1)
        def _():
            res = epilogue(acc_ref[...], *[e[...] for e in extra])
            for o_ref, r in zip(outs, res[:n_o]):
                o_ref[...] = r.astype(o_ref.dtype)
            for s_ref, r in zip(sums, res[n_o:]):
                @pl.when(i == 0)
                def _():
                    s_ref[...] = jnp.zeros_like(s_ref)

                s_ref[...] += r

        if rider is not None:
            @pl.when((i == grid[0] - 1) & (j == grid[1] - 1) & (k == grid[2] - 1))
            def _():
                for cp in rider.make(x_in, x_out, *sems):
                    cp.wait()

    tile_spec = pl.BlockSpec((tm, tn), lambda i, j, k: (i, j))
    row_spec = pl.BlockSpec((1, tn), lambda i, j, k: (0, j))
    ordered = n_sums > 0 or rider is not None
    outs = pl.pallas_call(
        body, grid=grid,
        in_specs=[a_spec, b_spec] + [tile_spec] * n_t + [row_spec] * n_r + [HBM_SPEC] * n_x,
        out_specs=[tile_spec] * n_o + [row_spec] * n_sums + [HBM_SPEC] * n_x,
        out_shape=[jax.ShapeDtypeStruct((M, N), dt) for dt in out_dtypes] + [jax.ShapeDtypeStruct((1, N), F32)] * n_sums
        + ([] if rider is None else rider.out_shapes),
        scratch_shapes=[pltpu.VMEM((tm, tn), F32)] + ([] if rider is None else rider.scratch()),
        compiler_params=_params(*(("arbitrary",) * 3 if ordered else ("parallel", "parallel", "arbitrary"))), name=name,
    )(a, b, *tiles, *rows, *([] if rider is None else rider.arrays))
    res = outs[0] if n_o + n_sums == 1 else outs[:n_o + n_sums]
    return res if rider is None else (res, outs[n_o + n_sums:])


def _token_block(T):
    return min(512, T)


def _rmsnorm_fwd(h, g, *, name):
    T, D = h.shape
    tb = _token_block(T)

    def body(h_ref, g_ref, u_ref):
        x = h_ref[...]
        r = lax.rsqrt(jnp.mean(x * x, axis=-1, keepdims=True) + RMS_EPS)
        u_ref[...] = (x * r * g_ref[...]).astype(u_ref.dtype)

    blk = pl.BlockSpec((tb, D), lambda i: (i, 0))
    return pl.pallas_call(
        body, grid=(T // tb,), in_specs=[blk, pl.BlockSpec((1, D), lambda i: (0, 0))], out_specs=blk,
        out_shape=jax.ShapeDtypeStruct((T, D), MXU_DTYPE), compiler_params=_params("parallel"), name=name,
    )(h, g)


def _rms_bwd_math(x, g, du):
    r = lax.rsqrt(jnp.mean(x * x, axis=-1, keepdims=True) + RMS_EPS)
    gd = g * du
    dx = r * gd - x * (r * r * r) * jnp.mean(gd * x, axis=-1, keepdims=True)
    return dx, du * x * r


def _loss_head(h, g, target, *, name):
    T, D = h.shape
    tb = _token_block(T)

    def body(h_ref, g_ref, t_ref, dh_ref, dhm_ref, dg_ref, loss_ref):
        @pl.when(pl.program_id(0) == 0)
        def _():
            dg_ref[...] = jnp.zeros_like(dg_ref)
            loss_ref[...] = jnp.zeros_like(loss_ref)

        x, gg = h_ref[...], g_ref[...]
        r = lax.rsqrt(jnp.mean(x * x, axis=-1, keepdims=True) + RMS_EPS)
        diff = x * r * gg - t_ref[...]
        per_token = jnp.mean(diff * diff, axis=-1, keepdims=True)
        loss_ref[...] += 0.5 * jnp.sum(per_token, axis=0, keepdims=True)
        dx, dg_terms = _rms_bwd_math(x, gg, diff / D)
        dh_ref[...] = dx
        dhm_ref[...] = dx.astype(dhm_ref.dtype)
        dg_ref[...] += jnp.sum(dg_terms, axis=0, keepdims=True)

    blk = pl.BlockSpec((tb, D), lambda i: (i, 0))
    row = pl.BlockSpec((1, D), lambda i: (0, 0))
    return pl.pallas_call(
        body, grid=(T // tb,), in_specs=[blk, row, blk], out_specs=[blk, blk, row, pl.BlockSpec((1, 1), lambda i: (0, 0))],
        out_shape=[jax.ShapeDtypeStruct((T, D), F32), jax.ShapeDtypeStruct((T, D), MXU_DTYPE), jax.ShapeDtypeStruct((1, D), F32),
                   jax.ShapeDtypeStruct((1, 1), F32)],
        compiler_params=_params("arbitrary"), name=name,
    )(h, g, target)


def _sb_scores(qm, ks, later, tri, mask, need_log_beta=True):
    z = _dot_nt(qm, ks)
    sp = jnp.maximum(z, 0.0) + jnp.log(1.0 + jnp.exp(_neg_abs(z)))
    lb = z - sp if need_log_beta else None
    if mask is not None:
        sp = jnp.where(mask, sp, 0.0)
    after, total = _key_order_sums(sp, tri, later=True)
    w = jnp.exp((lb if need_log_beta else z) - (after + later))
    if mask is not None:
        w = jnp.where(mask, w, 0.0)
    return total, lb, w


def _sb_setup(q_ref, rows, inclusive=False):
    i, hsel = pl.program_id(1), pl.program_id(2)
    lane = lax.broadcasted_iota(jnp.int32, (rows, 2 * SB_DH), 1)
    mine = (lane >= SB_DH) == (hsel == 1)
    diag = (i * rows) // SB_KEYS
    t = i * rows + lax.broadcasted_iota(jnp.int32, (rows, SB_KEYS), 0)
    s = diag * SB_KEYS + lax.broadcasted_iota(jnp.int32, (rows, SB_KEYS), 1)
    a = lax.broadcasted_iota(jnp.int32, (2 * SB_SUB, SB_SUB), 0) % SB_SUB
    b = lax.broadcasted_iota(jnp.int32, (2 * SB_SUB, SB_SUB), 1)
    return i, hsel, mine, diag, s < t, _mx(a >= b if inclusive else a > b), _mx(a < b)


def _sb_keys(j, n=1):
    return pl.ds(pl.multiple_of(j * SB_KEYS, SB_KEYS), n * SB_KEYS)


def _sb_descend(n, step, carry, wide):
    pair = (lambda j, cr: step(j, 2, cr)) if wide else (lambda j, cr: step(j, 1, step(j + 1, 1, cr)))
    carry = lax.fori_loop(0, n // 2, lambda it, cr: pair(n - 2 - 2 * it, cr), carry)
    return lax.cond(n % 2 == 1, lambda cr: step(0, 1, cr), lambda cr: cr, carry)


def _sb_ascend(n, step, carry):
    odd = n % 2
    carry = lax.cond(odd == 1, lambda cr: step(0, 1, cr), lambda cr: cr, carry)
    return lax.fori_loop(0, n // 2, lambda it, cr: step(odd + 2 * it, 2, cr), carry)


def _sb_call(body, qkv, extra_in, out_blocks, out_dtypes, scratch, rider, rows, *, name):
    T = qkv.shape[0]
    n_pairs = SB_HEADS // 2
    grid = (n_pairs, T // rows, 2)
    pair = lambda col0: pl.BlockSpec((rows, 2 * SB_DH), lambda p, i, h: (i, col0 + p))
    whole = lambda col0: pl.BlockSpec((T, 2 * SB_DH), lambda p, i, h: (0, col0 + p))
    in_specs = [pair(0), whole(n_pairs), whole(2 * n_pairs)] + [pair(0)] * len(extra_in)
    out_specs = [pair(0) if kind == "pair" else whole(0) for kind in out_blocks]
    n_in, n_out, n_r = len(in_specs), len(out_specs), 0 if rider is None else len(rider.arrays)

    def kernel_body(*refs):
        ins, r_in = refs[:n_in], refs[n_in:n_in + n_r]
        outs, r_out = refs[n_in + n_r:n_in + n_r + n_out], refs[n_in + n_r + n_out:n_in + 2 * n_r + n_out]
        rest = refs[n_in + 2 * n_r + n_out:]
        ids = [pl.program_id(a) for a in range(3)]
        if rider is not None:
            @pl.when((ids[0] == 0) & (ids[1] == 0) & (ids[2] == 0))
            def _():
                for cp in rider.make(r_in, r_out, *rest[len(scratch):]):
                    cp.start()

        body(ins, outs, rest[:len(scratch)])
        if rider is not None:
            @pl.when((ids[0] == grid[0] - 1) & (ids[1] == grid[1] - 1) & (ids[2] == grid[2] - 1))
            def _():
                for cp in rider.make(r_in, r_out, *rest[len(scratch):]):
                    cp.wait()

    res = pl.pallas_call(
        kernel_body, grid=grid, in_specs=in_specs + [HBM_SPEC] * n_r, out_specs=out_specs + [HBM_SPEC] * n_r,
        out_shape=[jax.ShapeDtypeStruct((T, SB_WIDTH), dt) for dt in out_dtypes] + ([] if rider is None else rider.out_shapes),
        scratch_shapes=list(scratch) + ([] if rider is None else rider.scratch()),
        compiler_params=_params("arbitrary", "arbitrary", "arbitrary"), name=name,
    )(qkv, qkv, qkv, *extra_in, *([] if rider is None else rider.arrays))
    return res[:n_out], res[n_out:]


def _sb_fwd(qkv, rider=None, *, name):
    rows = min(SB_ROWS_FWD, qkv.shape[0])
    scale = SB_DH ** -0.5

    def body(ins, outs, _):
        (q_ref, k_ref, v_ref), (o_ref,) = ins, outs
        i, hsel, mine, diag, mask, tri, _ = _sb_setup(q_ref, rows, inclusive=True)
        qm = jnp.where(mine, q_ref[...], 0) * scale

        def tile(j, n, m, later, acc):
            total, _, w = _sb_scores(qm, k_ref[_sb_keys(j, n), :], later, tri, m, need_log_beta=False)
            return later + total, acc + _dot(w, v_ref[_sb_keys(j, n), :])

        carry = tile(diag, 1, mask, jnp.zeros((rows, 1), F32), jnp.zeros((rows, 2 * SB_DH), F32))
        _, acc = _sb_descend(diag, lambda j, n, cr: tile(j, n, None, *cr), carry, wide=False)
        res = jnp.where(mine, acc, 0.0).astype(o_ref.dtype)

        @pl.when(hsel == 0)
        def _():
            o_ref[...] = res

        @pl.when(hsel == 1)
        def _():
            o_ref[...] += res

    (o,), got = _sb_call(body, qkv, [], ["pair"], [MXU_DTYPE], [], rider, rows, name=name)
    return o, got


def _sb_bwd(qkv, dmix, rider=None, *, name):
    T = qkv.shape[0]
    rows = min(SB_ROWS_BWD, T)
    scale = SB_DH ** -0.5

    def body(ins, outs, scratch):
        (q_ref, k_ref, v_ref, do_ref), (dq_ref, dk_ref, dv_ref), (da_ref, beta_ref) = ins, outs, scratch
        i, hsel, mine, diag, mask, tri, tri_before = _sb_setup(q_ref, rows)

        @pl.when((i == 0) & (hsel == 0))
        def _():
            dk_ref[...] = jnp.zeros_like(dk_ref)
            dv_ref[...] = jnp.zeros_like(dv_ref)

        qm = jnp.where(mine, q_ref[...], 0) * scale
        do_m = _mx(jnp.where(mine, do_ref[...], 0.0))

        def weights(j, n, m, later):
            total, lb, w = _sb_scores(qm, k_ref[_sb_keys(j, n), :], later, tri, m)
            da, beta = _dot_nt(do_m, v_ref[_sb_keys(j, n), :]) * w, jnp.exp(lb)
            for t in range(n):
                da_ref[j + t] = da[:, t * SB_KEYS:(t + 1) * SB_KEYS]
                beta_ref[j + t] = beta[:, t * SB_KEYS:(t + 1) * SB_KEYS]
            dv_ref[_sb_keys(j, n), :] += _dot_tn(w, do_m)
            return later + total

        later = weights(diag, 1, mask, jnp.zeros((rows, 1), F32))
        _sb_descend(diag, lambda j, n, c: weights(j, n, None, c), later, wide=True)

        def logits(j, n, m, before, dq):
            da = jnp.concatenate([da_ref[j + t] for t in range(n)], axis=1)
            beta = jnp.concatenate([beta_ref[j + t] for t in range(n)], axis=1)
            earlier, total = _key_order_sums(da, tri_before, later=False)
            dz = da - beta * (da + earlier + before)
            if m is not None:
                dz = jnp.where(m, dz, 0.0)
            dz = _mx(dz)
            dk_ref[_sb_keys(j, n), :] += _dot_tn(dz, qm)
            return before + total, dq + _dot(dz, k_ref[_sb_keys(j, n), :])

        carry = (jnp.zeros((rows, 1), F32), jnp.zeros((rows, 2 * SB_DH), F32))
        carry = _sb_ascend(diag, lambda j, n, cr: logits(j, n, None, *cr), carry)
        res = jnp.where(mine, logits(diag, 1, mask, *carry)[1] * scale, 0.0).astype(dq_ref.dtype)

        @pl.when(hsel == 0)
        def _():
            dq_ref[...] = res

        @pl.when(hsel == 1)
        def _():
            dq_ref[...] += res

    n_tiles = T // SB_KEYS
    scratch = [pltpu.VMEM((n_tiles, rows, SB_KEYS), F32), pltpu.VMEM((n_tiles, rows, SB_KEYS), F32)]
    return _sb_call(body, qkv, [dmix], ["pair", "whole", "whole"], [MXU_DTYPE, F32, F32], scratch, rider, rows, name=name)


def _chunk_row(n):
    return lax.broadcasted_iota(jnp.int32, (n, HG_DH), 0) % HG_CHUNK


def _chunk_cumsum(x, row, reverse=False):
    n = x.shape[0]
    for sh in (1, 2, 4, 8):
        if reverse:
            x = x + jnp.where(row < HG_CHUNK - sh, pltpu.roll(x, n - sh, 0), 0.0)
        else:
            x = x + jnp.where(row >= sh, pltpu.roll(x, sh, 0), 0.0)
    return x


def _hg_lower_bound(logits_ref):
    lg = logits_ref[...]
    e = jnp.exp(lg - jnp.max(lg, axis=0, keepdims=True))
    return e[0:1, :] / jnp.sum(e, axis=0, keepdims=True)


def _hg_terms(fr, q, lb, row):
    sig = jax.nn.sigmoid(fr)
    f = lb + (1.0 - lb) * sig
    kk = 1.0 - f
    g = jnp.log(f)
    G = _chunk_cumsum(g, row)
    g_last = G + (_chunk_cumsum(g, row, reverse=True) - g)
    e_g, e_ng, e_lg = jnp.exp(G), jnp.exp(-G), jnp.exp(g_last - G)
    return dict(sig=sig, f=f, kk=kk, e_g=e_g, e_ng=e_ng, e_lg=e_lg, q_dec=q * e_g, k_intra=kk * e_ng,
                k_state=kk * e_lg, decay=jnp.exp(g_last))


def _hg_causal(n):
    t = lax.broadcasted_iota(jnp.int32, (n, n), 0)
    s = lax.broadcasted_iota(jnp.int32, (n, n), 1)
    return (s <= t) & (s // HG_CHUNK == t // HG_CHUNK)


def _chunks(a):
    return a.reshape(a.shape[0] // HG_CHUNK, HG_CHUNK, a.shape[1])


def _per_chunk(lhs, rhs, contract):
    return lax.dot_general(_mx(lhs), _mx(rhs), ((contract[:1], contract[1:]), ((0,), (0,))), preferred_element_type=F32)


def _hg_specs(T, tb, col0, order):
    return [pl.BlockSpec((tb, HG_WIDTH), functools.partial(lambda i, j: (order(i), j), j=col0 + j)) for j in range(4)]


def _hg_fwd(proj, logits, norm_g, *, name):
    T = proj.shape[0]
    tb = min(HG_TOKENS, T)
    nch = tb // HG_CHUNK

    def body(q_ref, f_ref, i_ref, gate_ref, lg_ref, ng_ref, out_ref, o_ref, s_ref, st_ref, inc_ref, dec_ref):
        @pl.when(pl.program_id(0) == 0)
        def _():
            st_ref[...] = jnp.zeros_like(st_ref)

        lb_all = _hg_lower_bound(lg_ref)
        row = _chunk_row(tb)
        causal = _hg_causal(tb)
        for hh in range(HG_HEADS):
            cols = slice(hh * HG_DH, (hh + 1) * HG_DH)
            t = _hg_terms(f_ref[:, cols], q_ref[:, cols], lb_all[:, cols], row)
            v = i_ref[:, cols]
            scores = jnp.where(causal, _dot_nt(t["q_dec"], t["k_intra"]), 0.0)
            o_intra = _dot(scores, v)
            inc_ref[...] = _per_chunk(_chunks(v), _chunks(t["k_state"]), (1, 1))
            dec_ref[...] = _chunks(t["decay"])

            def step(ci, st):
                s_ref[ci, hh] = st
                return st * dec_ref[ci][0:1, :] + inc_ref[ci]

            st_ref[hh] = lax.fori_loop(0, nch, step, st_ref[hh], unroll=4)
            o_inter = _per_chunk(_chunks(t["q_dec"]), s_ref[:, hh], (2, 2))
            o = o_intra + o_inter.reshape(tb, HG_DH)
            o_ref[:, cols] = o
            gate = gate_ref[:, cols]
            on = o * lax.rsqrt(jnp.mean(o * o, axis=-1, keepdims=True) + RMS_EPS) * ng_ref[:, cols]
            out_ref[:, cols] = (on * (gate * jax.nn.sigmoid(gate))).astype(out_ref.dtype)

    blk = pl.BlockSpec((tb, HG_WIDTH), lambda i: (i, 0))
    return pl.pallas_call(
        body, grid=(T // tb,),
        in_specs=_hg_specs(T, tb, 3, lambda i: i) + [pl.BlockSpec((3, HG_WIDTH), lambda i: (0, 0)), pl.BlockSpec((1, HG_WIDTH), lambda i: (0, 0))],
        out_specs=[blk, blk, pl.BlockSpec((nch, HG_HEADS, HG_DH, HG_DH), lambda i: (i, 0, 0, 0))],
        out_shape=[jax.ShapeDtypeStruct((T, HG_WIDTH), MXU_DTYPE), jax.ShapeDtypeStruct((T, HG_WIDTH), F32),
                   jax.ShapeDtypeStruct((T // HG_CHUNK, HG_HEADS, HG_DH, HG_DH), F32)],
        scratch_shapes=[pltpu.VMEM((HG_HEADS, HG_DH, HG_DH), F32), pltpu.VMEM((nch, HG_DH, HG_DH), F32),
                        pltpu.VMEM((nch, HG_CHUNK, HG_DH), F32)],
        compiler_params=_params("arbitrary"), name=name,
    )(proj, proj, proj, proj, logits, norm_g)


def _hg_bwd(proj, o_raw, states, dmix, logits, norm_g, *, name):
    T = proj.shape[0]
    tb = min(HG_TOKENS, T)
    nch = tb // HG_CHUNK
    nb = T // tb
    rev = lambda i: nb - 1 - i

    def body(q_ref, f_ref, i_ref, gate_ref, o_ref, s_ref, dout_ref, lg_ref, ng_ref, dp_ref, dlb_ref, dng_ref,
             dst_ref, inc_ref, dec_ref, after_ref):
        @pl.when(pl.program_id(0) == 0)
        def _():
            dst_ref[...] = jnp.zeros_like(dst_ref)
            dlb_ref[...] = jnp.zeros_like(dlb_ref)
            dng_ref[...] = jnp.zeros_like(dng_ref)

        lb_all = _hg_lower_bound(lg_ref)
        row = _chunk_row(tb)
        causal = _hg_causal(tb)
        for hh in range(HG_HEADS):
            cols = slice(hh * HG_DH, (hh + 1) * HG_DH)
            out_cols = lambda part: slice(part * HG_WIDTH + hh * HG_DH, part * HG_WIDTH + (hh + 1) * HG_DH)
            o, gate, dout, ng, lb = o_ref[:, cols], gate_ref[:, cols], dout_ref[:, cols], ng_ref[:, cols], lb_all[:, cols]
            sg = jax.nn.sigmoid(gate)
            r = lax.rsqrt(jnp.mean(o * o, axis=-1, keepdims=True) + RMS_EPS)
            oh = o * r
            dp_ref[:, out_cols(3)] = (dout * (oh * ng) * (sg * (1.0 + gate * (1.0 - sg)))).astype(dp_ref.dtype)
            don = dout * (gate * sg)
            dng_ref[:, cols] += jnp.sum(don * oh, axis=0, keepdims=True)
            doh = don * ng
            do = r * (doh - oh * jnp.mean(doh * oh, axis=-1, keepdims=True))

            t = _hg_terms(f_ref[:, cols], q_ref[:, cols], lb, row)
            v = i_ref[:, cols]
            scores = jnp.where(causal, _dot_nt(t["q_dec"], t["k_intra"]), 0.0)
            dscores = jnp.where(causal, _dot_nt(do, v), 0.0)
            inc_ref[...] = _per_chunk(_chunks(do), _chunks(t["q_dec"]), (1, 1))
            dec_ref[...] = _chunks(t["decay"])

            def step(it, dst):
                ci = nch - 1 - it
                after_ref[ci] = dst
                return dst * dec_ref[ci][0:1, :] + inc_ref[ci]

            dst_ref[hh] = lax.fori_loop(0, nch, step, dst_ref[hh], unroll=4)
            st, dst = s_ref[:, hh], after_ref[...]
            dqd = _dot(dscores, t["k_intra"]) + _per_chunk(_chunks(do), st, (2, 1)).reshape(tb, HG_DH)
            dki = _dot_tn(dscores, t["q_dec"])
            dks = _per_chunk(_chunks(v), dst, (2, 1)).reshape(tb, HG_DH)
            dp_ref[:, out_cols(2)] = (_dot_tn(scores, do) + _per_chunk(_chunks(t["k_state"]), dst, (2, 2)).reshape(tb, HG_DH)).astype(dp_ref.dtype)
            ddecay = jnp.broadcast_to(jnp.sum(st * dst, axis=1, keepdims=True), (nch, HG_CHUNK, HG_DH)).reshape(tb, HG_DH)
            dks_ks = dks * t["k_state"]
            d_glast = _chunk_cumsum(dks_ks, row) + ddecay * t["decay"]
            d_g = dqd * t["q_dec"] - dki * t["k_intra"] - dks_ks + jnp.where(row == HG_CHUNK - 1, d_glast, 0.0)
            df = _chunk_cumsum(d_g, row, reverse=True) / t["f"] - (dki * t["e_ng"] + dks * t["e_lg"])
            dp_ref[:, out_cols(0)] = (dqd * t["e_g"]).astype(dp_ref.dtype)
            dp_ref[:, out_cols(1)] = (df * (1.0 - lb) * t["sig"] * (1.0 - t["sig"])).astype(dp_ref.dtype)
            dlb_ref[:, cols] += jnp.sum(df * (1.0 - t["sig"]), axis=0, keepdims=True)

    blk = pl.BlockSpec((tb, HG_WIDTH), lambda i: (rev(i), 0))
    row_spec = pl.BlockSpec((1, HG_WIDTH), lambda i: (0, 0))
    return pl.pallas_call(
        body, grid=(nb,),
        in_specs=_hg_specs(T, tb, 3, rev) + [
            blk, pl.BlockSpec((nch, HG_HEADS, HG_DH, HG_DH), lambda i: (rev(i), 0, 0, 0)),
            pl.BlockSpec((tb, HG_WIDTH), lambda i: (rev(i), 1)), pl.BlockSpec((3, HG_WIDTH), lambda i: (0, 0)), row_spec],
        out_specs=[pl.BlockSpec((tb, 4 * HG_WIDTH), lambda i: (rev(i), 0)), row_spec, row_spec],
        out_shape=[jax.ShapeDtypeStruct((T, 4 * HG_WIDTH), MXU_DTYPE), jax.ShapeDtypeStruct((1, HG_WIDTH), F32), jax.ShapeDtypeStruct((1, HG_WIDTH), F32)],
        scratch_shapes=[pltpu.VMEM((HG_HEADS, HG_DH, HG_DH), F32), pltpu.VMEM((nch, HG_DH, HG_DH), F32),
                        pltpu.VMEM((nch, HG_CHUNK, HG_DH), F32), pltpu.VMEM((nch, HG_DH, HG_DH), F32)],
        compiler_params=_params("arbitrary"), name=name,
    )(proj, proj, proj, proj, o_raw, states, dmix, logits, norm_g)


def _shifted_copies(sh_ref, n_rows):
    keep = n_rows + CONV_HALO - 8
    for b in range(1, 8):
        sh_ref[b, 0:keep, :] = sh_ref[0, b:b + keep, :]


def _tap_rows(sh_ref, offset, r0, lanes):
    start = pl.multiple_of(r0 + (offset - offset % 8), 8)
    return sh_ref[offset % 8, pl.ds(start, CONV_ROWS), lanes]


def _conv_fwd(p, w_dw, b_dw, ln_g, ln_b, *, name):
    T, D = p.shape[0], p.shape[1] // 2
    tb = _token_block(T)
    hpb = tb // CONV_HALO
    lane_step = 512

    def body(p1_ref, p2_ref, q1_ref, q2_ref, w_ref, bdw_ref, g_ref, b_ref, a_ref, y_ref, act_ref, sh_ref):
        i = pl.program_id(0)
        a = p1_ref[...] * jax.nn.sigmoid(p2_ref[...])
        sh_ref[0, 0:CONV_HALO, :] = jnp.where(i > 0, q1_ref[...] * jax.nn.sigmoid(q2_ref[...]), 0.0)
        sh_ref[0, CONV_HALO:, :] = a
        a_ref[...] = a
        _shifted_copies(sh_ref, tb)

        def chunk(ci, _):
            r0 = pl.multiple_of(ci * CONV_ROWS, CONV_ROWS)
            for l0 in range(0, D, lane_step):
                lanes = slice(l0, l0 + lane_step)
                acc = jnp.broadcast_to(bdw_ref[:, lanes], (CONV_ROWS, lane_step))
                for k in range(CONV_WIDTH):
                    acc = acc + _tap_rows(sh_ref, CONV_HALO - CONV_WIDTH + 1 + k, r0, lanes) * w_ref[k:k + 1, lanes]
                y_ref[pl.ds(r0, CONV_ROWS), lanes] = acc
            return 0

        lax.fori_loop(0, tb // CONV_ROWS, chunk, 0)
        y = y_ref[...]
        mu = jnp.mean(y, axis=-1, keepdims=True)
        yc = y - mu
        s = yc * lax.rsqrt(jnp.mean(yc * yc, axis=-1, keepdims=True) + LN_EPS) * g_ref[...] + b_ref[...]
        act_ref[...] = (s * jax.nn.sigmoid(s)).astype(act_ref.dtype)

    prev = lambda i: jnp.maximum(i * hpb - 1, 0)
    blk = pl.BlockSpec((tb, D), lambda i: (i, 0))
    row = pl.BlockSpec((1, D), lambda i: (0, 0))
    return pl.pallas_call(
        body, grid=(T // tb,),
        in_specs=[blk, pl.BlockSpec((tb, D), lambda i: (i, 1)), pl.BlockSpec((CONV_HALO, D), lambda i: (prev(i), 0)),
                  pl.BlockSpec((CONV_HALO, D), lambda i: (prev(i), 1)), pl.BlockSpec((CONV_HALO, D), lambda i: (0, 0)), row, row, row],
        out_specs=[blk, blk, blk],
        out_shape=[jax.ShapeDtypeStruct((T, D), F32), jax.ShapeDtypeStruct((T, D), F32), jax.ShapeDtypeStruct((T, D), MXU_DTYPE)],
        scratch_shapes=[pltpu.VMEM((8, tb + CONV_HALO, D), F32)],
        compiler_params=_params("parallel"), name=name,
    )(p, p, p, p, w_dw, b_dw, ln_g, ln_b)


def _conv_bwd_norm(dact, y, ln_g, ln_b, *, name):
    T, D = y.shape
    tb = _token_block(T)

    def body(da_ref, y_ref, g_ref, b_ref, dy_ref, dg_ref, db_ref, cs_ref):
        @pl.when(pl.program_id(0) == 0)
        def _():
            dg_ref[...] = jnp.zeros_like(dg_ref)
            db_ref[...] = jnp.zeros_like(db_ref)
            cs_ref[...] = jnp.zeros_like(cs_ref)

        y, g = y_ref[...], g_ref[...]
        yc = y - jnp.mean(y, axis=-1, keepdims=True)
        rs = lax.rsqrt(jnp.mean(yc * yc, axis=-1, keepdims=True) + LN_EPS)
        yn = yc * rs
        s = yn * g + b_ref[...]
        sg = jax.nn.sigmoid(s)
        ds = da_ref[...] * (sg * (1.0 + s * (1.0 - sg)))
        dg_ref[...] += jnp.sum(ds * yn, axis=0, keepdims=True)
        db_ref[...] += jnp.sum(ds, axis=0, keepdims=True)
        dyn = ds * g
        dy = rs * (dyn - jnp.mean(dyn, axis=-1, keepdims=True) - yn * jnp.mean(dyn * yn, axis=-1, keepdims=True))
        dy_ref[...] = dy
        cs_ref[...] += jnp.sum(dy, axis=0, keepdims=True)

    blk = pl.BlockSpec((tb, D), lambda i: (i, 0))
    row = pl.BlockSpec((1, D), lambda i: (0, 0))
    rs_ = jax.ShapeDtypeStruct((1, D), F32)
    return pl.pallas_call(
        body, grid=(T // tb,), in_specs=[blk, blk, row, row], out_specs=[blk, row, row, row],
        out_shape=[jax.ShapeDtypeStruct((T, D), F32), rs_, rs_, rs_], compiler_params=_params("arbitrary"), name=name,
    )(dact, y, ln_g, ln_b)


def _conv_bwd_taps(dy, a, p, w_dw, *, name):
    T, D = dy.shape
    tb = _token_block(T)
    hpb = tb // CONV_HALO
    last = T // CONV_HALO - 1
    nb = T // tb
    lane_step = 128
    groups = CONV_ROWS // 8

    def body(dy_ref, dyn_ref, a_ref, p1_ref, p2_ref, w_ref, dp_ref, dw_ref, cs_ref, sh_ref, da_ref):
        i = pl.program_id(0)

        @pl.when(i == 0)
        def _():
            dw_ref[...] = jnp.zeros_like(dw_ref)
            cs_ref[...] = jnp.zeros_like(cs_ref)

        sh_ref[0, 0:tb, :] = dy_ref[...]
        sh_ref[0, tb:, :] = jnp.where(i < nb - 1, dyn_ref[...], 0.0)
        _shifted_copies(sh_ref, tb)
        for l0 in range(0, D, lane_step):
            lanes = slice(l0, l0 + lane_step)
            for taps in (range(0, CONV_WIDTH // 2 + 1), range(CONV_WIDTH // 2 + 1, CONV_WIDTH)):
                def chunk(ci, sums, taps=taps):
                    r0 = pl.multiple_of(ci * CONV_ROWS, CONV_ROWS)
                    a_c = a_ref[pl.ds(r0, CONV_ROWS), lanes]
                    da = jnp.zeros((CONV_ROWS, lane_step), F32) if taps[0] == 0 else da_ref[pl.ds(r0, CONV_ROWS), lanes]
                    new = []
                    for n, k in enumerate(taps):
                        s_k = _tap_rows(sh_ref, CONV_WIDTH - 1 - k, r0, lanes)
                        da = da + s_k * w_ref[k:k + 1, lanes]
                        new.append(sums[n] + jnp.sum((s_k * a_c).reshape(groups, 8, lane_step), axis=0))
                    da_ref[pl.ds(r0, CONV_ROWS), lanes] = da
                    return tuple(new)

                sums = lax.fori_loop(0, tb // CONV_ROWS, chunk, tuple(jnp.zeros((8, lane_step), F32) for _ in taps))
                for n, k in enumerate(taps):
                    dw_ref[k:k + 1, lanes] += jnp.sum(sums[n], axis=0, keepdims=True)
        da = da_ref[...]
        p1 = p1_ref[...]
        sg = jax.nn.sigmoid(p2_ref[...])
        dp1 = da * sg
        dp2 = da * p1 * (sg * (1.0 - sg))
        dp_ref[:, 0:D] = dp1.astype(dp_ref.dtype)
        dp_ref[:, D:] = dp2.astype(dp_ref.dtype)
        cs_ref[:, 0:D] += jnp.sum(dp1, axis=0, keepdims=True)
        cs_ref[:, D:] += jnp.sum(dp2, axis=0, keepdims=True)

    blk = pl.BlockSpec((tb, D), lambda i: (i, 0))
    return pl.pallas_call(
        body, grid=(nb,),
        in_specs=[blk, pl.BlockSpec((CONV_HALO, D), lambda i: (jnp.minimum((i + 1) * hpb, last), 0)), blk, blk,
                  pl.BlockSpec((tb, D), lambda i: (i, 1)), pl.BlockSpec((CONV_HALO, D), lambda i: (0, 0))],
        out_specs=[pl.BlockSpec((tb, 2 * D), lambda i: (i, 0)), pl.BlockSpec((CONV_HALO, D), lambda i: (0, 0)), pl.BlockSpec((1, 2 * D), lambda i: (0, 0))],
        out_shape=[jax.ShapeDtypeStruct((T, 2 * D), MXU_DTYPE), jax.ShapeDtypeStruct((CONV_HALO, D), F32), jax.ShapeDtypeStruct((1, 2 * D), F32)],
        scratch_shapes=[pltpu.VMEM((8, tb + CONV_HALO, D), F32), pltpu.VMEM((tb, D), F32)],
        compiler_params=_params("arbitrary"), name=name,
    )(dy, dy, a, p, p, w_dw)


def _row_block(rows):
    for tr in (512, 256, 128, 64, 32, 16, 8):
        if rows % tr == 0:
            return tr
    return rows


def _sum_leading(x, *, name):
    n, R, C = x.shape
    tr = _row_block(R)

    def body(x_ref, o_ref):
        acc = x_ref[0]
        for j in range(1, n):
            acc = acc + x_ref[j]
        o_ref[...] = acc

    return pl.pallas_call(
        body, grid=(R // tr,), in_specs=[pl.BlockSpec((n, tr, C), lambda i: (0, i, 0))], out_specs=pl.BlockSpec((tr, C), lambda i: (i, 0)),
        out_shape=jax.ShapeDtypeStruct((R, C), x.dtype), compiler_params=_params("parallel"), name=name,
    )(x)


def _add_pair(x, y, *, name):
    n, R, C = x.shape
    tr = _row_block(R)

    def body(x_ref, y_ref, o_ref):
        o_ref[...] = x_ref[...] + y_ref[...]

    blk = pl.BlockSpec((1, tr, C), lambda j, i: (j, i, 0))
    return pl.pallas_call(
        body, grid=(n, R // tr), in_specs=[blk, blk], out_specs=blk,
        out_shape=jax.ShapeDtypeStruct((n, R, C), x.dtype), compiler_params=_params("parallel", "parallel"), name=name,
    )(x, y)


def _adamw(w, g, m, v, *, name):
    R, C = w.shape
    tr = _row_block(R)
    c1, c2 = 1.0 - ADAM_B1 ** ADAM_STEP, 1.0 - ADAM_B2 ** ADAM_STEP

    def body(w_ref, g_ref, m_ref, v_ref, d_ref, nm_ref, nv_ref):
        g_ = g_ref[...]
        nm = ADAM_B1 * m_ref[...] + (1.0 - ADAM_B1) * g_
        nv = ADAM_B2 * v_ref[...] + (1.0 - ADAM_B2) * (g_ * g_)
        d_ref[...] = -ADAM_LR * ((nm / c1) / (jnp.sqrt(nv / c2) + ADAM_EPS) + ADAM_WD * w_ref[...])
        nm_ref[...] = nm
        nv_ref[...] = nv

    blk = pl.BlockSpec((tr, C), lambda i: (i, 0))
    shp = jax.ShapeDtypeStruct((R, C), F32)
    return pl.pallas_call(
        body, grid=(R // tr,), in_specs=[blk] * 4, out_specs=[blk] * 3, out_shape=[shp] * 3,
        compiler_params=_params("parallel"), name=name,
    )(w, g, m, v)


def _small_reduce(packs, logits, *, name):
    n, R, C = packs.shape

    def body(p_ref, lg_ref, s_ref, dlg_ref):
        acc = p_ref[0]
        for j in range(1, n):
            acc = acc + p_ref[j]
        s_ref[...] = acc
        lg = lg_ref[...]
        e = jnp.exp(lg - jnp.max(lg, axis=0, keepdims=True))
        sm = e / jnp.sum(e, axis=0, keepdims=True)
        dlb = acc[5:6, HG_WIDTH:2 * HG_WIDTH]
        first = lax.broadcasted_iota(jnp.int32, sm.shape, 0) == 0
        dlg_ref[...] = sm[0:1, :] * (jnp.where(first, 1.0, 0.0) - sm) * dlb

    whole = lambda shape: pl.BlockSpec(shape, lambda: (0,) * len(shape))
    return pl.pallas_call(
        body, in_specs=[whole((n, R, C)), whole(logits.shape)], out_specs=[whole((R, C)), whole(logits.shape)],
        out_shape=[jax.ShapeDtypeStruct((R, C), F32), jax.ShapeDtypeStruct(logits.shape, F32)],
        compiler_params=pltpu.CompilerParams(vmem_limit_bytes=VMEM_LIMIT), name=name,
    )(packs, logits)


HBM_SPEC = pl.BlockSpec(memory_space=pl.ANY)


def _place():
    return lax.axis_index("x"), lax.axis_index("y"), lax.axis_index("c")


class _Copies:
    def __init__(self, arrays, out_shapes, n_copies, make, finish):
        self.arrays, self.out_shapes, self.n_copies, self.make, self.finish = list(arrays), list(out_shapes), n_copies, make, finish

    def scratch(self):
        return [pltpu.SemaphoreType.DMA((self.n_copies,)), pltpu.SemaphoreType.DMA((self.n_copies,))]

    def run(self, name):
        n = len(self.arrays)

        def body(*refs):
            copies = self.make(refs[:n], refs[n:2 * n], *refs[2 * n:])
            for cp in copies:
                cp.start()
            for cp in copies:
                cp.wait()

        outs = pl.pallas_call(body, in_specs=[HBM_SPEC] * n, out_specs=[HBM_SPEC] * n, out_shape=self.out_shapes,
                              scratch_shapes=self.scratch(), name=name)(*self.arrays)
        return self.finish(outs)


def _remote(src, dst, send_sems, recv_sems, k, peer):
    return pltpu.make_async_remote_copy(src_ref=src, dst_ref=dst, send_sem=send_sems.at[k], recv_sem=recv_sems.at[k],
                                        device_id=peer, device_id_type=MESH)


def _same_core_peers(x, y, c):
    return [(1 - x, y, c), (x, 1 - y, c), (1 - x, 1 - y, c)]


def _all_peers(x, y, c):
    flip = lambda v, b: 1 - v if b else v
    return [(flip(x, r & 4), flip(y, r & 2), flip(c, r & 1)) for r in range(1, 8)]


def _gather(arrays, peers_of, slot_of, n_slots):
    n_peers = len(peers_of(0, 0, 0))

    def make(ins, outs, send_sems, recv_sems):
        x, y, c = _place()
        slot = slot_of(x, y, c)
        return [_remote(ins[a], outs[a].at[slot], send_sems, recv_sems, a * n_peers + k, peer)
                for a in range(len(arrays)) for k, peer in enumerate(peers_of(x, y, c))]

    def finish(outs):
        slot = slot_of(*_place())
        return [lax.dynamic_update_index_in_dim(o, a, slot, 0) for o, a in zip(outs, arrays)]

    shapes = [jax.ShapeDtypeStruct((n_slots,) + a.shape, a.dtype) for a in arrays]
    return _Copies(arrays, shapes, len(arrays) * n_peers, make, finish)


def _gather_chips(arrays):
    return _gather(arrays, _same_core_peers, lambda x, y, c: 2 * x + y, N_CHIPS)


def _gather_all(arrays):
    return _gather(arrays, _all_peers, lambda x, y, c: 4 * x + 2 * y + c, N_DEV)


def _pair_swap(a):
    def make(ins, outs, send_sems, recv_sems):
        x, y, c = _place()
        return [_remote(ins[0].at[1 - c], outs[0], send_sems, recv_sems, 0, (x, y, 1 - c))]

    return _Copies([a], [jax.ShapeDtypeStruct(a.shape[1:], a.dtype)], 1, make, lambda outs: outs[0])


def _chip_scatter(p):
    def make(ins, outs, send_sems, recv_sems):
        x, y, c = _place()
        return [_remote(ins[0].at[2 * px + py], outs[0].at[2 * x + y], send_sems, recv_sems, k, (px, py, pc))
                for k, (px, py, pc) in enumerate(_same_core_peers(x, y, c))]

    def finish(outs):
        x, y, _ = _place()
        me = 2 * x + y
        return lax.dynamic_update_index_in_dim(outs[0], lax.dynamic_index_in_dim(p, me, 0, keepdims=False), me, 0)

    return _Copies([p], [jax.ShapeDtypeStruct(p.shape, p.dtype)], 3, make, finish)


def _owner_scatter(blocks):
    def make(ins, outs, send_sems, recv_sems):
        x, y, c = _place()
        return [_remote(ins[0].at[2 * px + py, pc], outs[0].at[4 * x + 2 * y + c], send_sems, recv_sems, k, (px, py, pc))
                for k, (px, py, pc) in enumerate(_all_peers(x, y, c))]

    def finish(outs):
        x, y, c = _place()
        mine = lax.dynamic_index_in_dim(lax.dynamic_index_in_dim(blocks, 2 * x + y, 0, keepdims=False), c, 0, keepdims=False)
        return lax.dynamic_update_index_in_dim(outs[0], mine, 4 * x + 2 * y + c, 0)

    return _Copies([blocks], [jax.ShapeDtypeStruct((N_DEV,) + blocks.shape[2:], blocks.dtype)], N_DEV - 1, make, finish)


def _pair_gather(q):
    def make(ins, outs, send_sems, recv_sems):
        x, y, c = _place()
        return [_remote(ins[0], outs[0].at[c], send_sems, recv_sems, 0, (x, y, 1 - c))]

    return _Copies([q], [jax.ShapeDtypeStruct((2,) + q.shape, q.dtype)], 1, make,
                   lambda outs: lax.dynamic_update_index_in_dim(outs[0], q, _place()[2], 0))


def _grad_blocks(dw, kind):
    if kind == "cols2d":
        K, N = dw.shape
        b = dw.reshape(2, K // 2, N_CHIPS, N // N_CHIPS).transpose(2, 0, 1, 3)
    elif kind == "rows2d":
        b = dw.reshape(N_CHIPS, 2, dw.shape[0] // 8, dw.shape[1])
    elif kind == "cols3d":
        L, K, N = dw.shape
        b = dw.reshape(L, K, N_CHIPS, N // N_CHIPS).transpose(2, 0, 1, 3)
    else:
        L, K, N = dw.shape
        b = dw.reshape(L, N_CHIPS, K // N_CHIPS, N).transpose(1, 0, 2, 3)
    return b.reshape(N_CHIPS, 2, -1, D_MODEL)


def _pad_rows(a, rows):
    return jnp.concatenate([a, jnp.zeros((rows - a.shape[0],) + a.shape[1:], a.dtype)], axis=0)


def _forward_backward(x, target, W, late_weights=None, early_grads=None, last_grads=None):
    row = lambda a: a.reshape(1, -1)
    relu2 = lambda acc: (jnp.square(jnp.maximum(acc, 0.0)),)
    normed = lambda h, g: h * lax.rsqrt(jnp.mean(h * h, axis=-1, keepdims=True) + RMS_EPS) * g

    def residual_norm(acc, res, g):
        h = res + acc
        return h, normed(h, g)

    def norm_bwd(du, h_blk, dres_blk, g):
        dx, dg_terms = _rms_bwd_math(h_blk, g, du)
        dh = dres_blk + dx
        return dh, dh, jnp.sum(dg_terms, axis=0, keepdims=True), jnp.sum(dh, axis=0, keepdims=True)

    def matmul_norm_bwd(dy, w, h_in, g, dres, rider=None, *, tk=1024, name):
        return _matmul(dy, w, mode="nt", out_dtypes=[F32, MXU_DTYPE], epilogue=norm_bwd, tiles=[h_in, dres], rows=[row(g)],
                       n_sums=2, rider=rider, tk=tk, name=name)

    G = {}

    u0 = _rmsnorm_fwd(x, row(W["norm_mix_g"][0]), name="norm_mix0")
    proj, qkv = _matmul(u0, W["w_in"], mode="nn", out_dtypes=[F32, MXU_DTYPE], epilogue=lambda acc: (acc, acc), tn=896, name="in_proj")
    o_sb, got = _sb_fwd(qkv, late_weights and late_weights[0], name="sb_fwd")
    if late_weights:
        W = {**W, **late_weights[1](got)}
    hg_out, hg_o, hg_states = _hg_fwd(proj, W["hg_lb_logits"], row(W["hg_norm_g"]), name="hg_fwd")
    mix = jnp.concatenate([o_sb, hg_out], axis=-1)
    h1, u1 = _matmul(mix, W["w_out"], mode="nn", out_dtypes=[F32, MXU_DTYPE], epilogue=residual_norm, tiles=[x],
                     rows=[row(W["norm_ffn_g"][0])], name="out_proj")
    r0 = _matmul(u1, W["w_ff1"][0], mode="nn", out_dtypes=[MXU_DTYPE], epilogue=relu2, name="ff1_0")
    h2, u2 = _matmul(r0, W["w_ff2"][0], mode="nn", out_dtypes=[F32, MXU_DTYPE], epilogue=residual_norm, tiles=[h1],
                     rows=[row(W["norm_mix_g"][1])], name="ff2_0")
    p = _matmul(u2, W["w_glu"], mode="nn", out_dtypes=[F32], epilogue=lambda acc, b: (acc + b,), rows=[row(W["b_glu"])], name="glu_proj")
    w_dw = _pad_rows(W["w_dw"], CONV_HALO)
    ca, cy, cact = _conv_fwd(p, w_dw, row(W["b_dw"]), row(W["ln_g"]), row(W["ln_b"]), name="conv_fwd")
    h3, u3 = _matmul(cact, W["w_pw"], mode="nn", out_dtypes=[F32, MXU_DTYPE], epilogue=lambda acc, res, b, g: residual_norm(acc + b, res, g),
                     tiles=[h2], rows=[row(W["b_pw"]), row(W["norm_ffn_g"][1])], name="pw_proj")
    r1 = _matmul(u3, W["w_ff1"][1], mode="nn", out_dtypes=[MXU_DTYPE], epilogue=relu2, name="ff1_1")
    h4 = _matmul(r1, W["w_ff2"][1], mode="nn", out_dtypes=[F32], epilogue=lambda acc, res: (res + acc,), tiles=[h3], name="ff2_1")

    dh4, dh4_m, G["final_norm_g"], loss = _loss_head(h4, row(W["final_norm_g"]), target, name="loss_head")

    def mlp_bwd(dh, dh_m, h_in, u, r, layer, tag):
        d_relu2 = lambda acc, r_blk: (acc * (2.0 * jnp.sqrt(r_blk.astype(F32))),)
        da = _matmul(dh_m, W["w_ff2"][layer], mode="nt", out_dtypes=[MXU_DTYPE], epilogue=d_relu2, tiles=[r], name="d_ff2_act" + tag)
        dw2 = _matmul(r, dh_m, mode="tn", out_dtypes=[F32], tk=WGRAD_TOKENS, name="d_ff2_w" + tag)
        dw1 = _matmul(u, da, mode="tn", out_dtypes=[F32], tk=WGRAD_TOKENS, name="d_ff1_w" + tag)
        dh_in, dh_in_m, dg, cs = matmul_norm_bwd(da, W["w_ff1"][layer], h_in, W["norm_ffn_g"][layer], dh, name="d_ff1_act" + tag)
        return dh_in, dh_in_m, dg, cs, dw1, dw2

    dh3, dh3_m, dg_ffn1, cs_h3, dw1_1, dw2_1 = mlp_bwd(dh4, dh4_m, h3, u3, r1, 1, "1")
    G["b_pw"] = cs_h3
    dact = _matmul(dh3_m, W["w_pw"], mode="nt", out_dtypes=[F32], name="d_pw_act")
    G["w_pw"] = _matmul(cact, dh3_m, mode="tn", out_dtypes=[F32], tk=WGRAD_TOKENS, name="d_pw_w")
    dy, G["ln_g"], G["ln_b"], G["b_dw"] = _conv_bwd_norm(dact, cy, row(W["ln_g"]), row(W["ln_b"]), name="d_conv_norm")
    dp, G["w_dw"], G["b_glu"] = _conv_bwd_taps(dy, ca, p, w_dw, name="d_conv_taps")
    G["w_glu"] = _matmul(u2, dp, mode="tn", out_dtypes=[F32], tk=WGRAD_TOKENS, name="d_glu_w")
    dh2, dh2_m, dg_mix1, _ = matmul_norm_bwd(dp, W["w_glu"], h2, W["norm_mix_g"][1], dh3, name="d_glu_act")
    dh1, dh1_m, dg_ffn0, _, dw1_0, dw2_0 = mlp_bwd(dh2, dh2_m, h1, u1, r0, 0, "0")
    G["w_ff1"], G["w_ff2"] = jnp.stack([dw1_0, dw1_1]), jnp.stack([dw2_0, dw2_1])
    G["norm_ffn_g"] = jnp.concatenate([dg_ffn0, dg_ffn1], axis=0)
    dmix = _matmul(dh1_m, W["w_out"], mode="nt", out_dtypes=[F32], name="d_out_act")
    G["w_out"] = _matmul(mix, dh1_m, mode="tn", out_dtypes=[F32], tk=WGRAD_TOKENS, name="d_out_w")
    riding = early_grads(G) if early_grads else None
    (dsq, dsk, dsv), got = _sb_bwd(qkv, dmix, riding, name="sb_bwd")
    d_hg, G["hg_lb"], G["hg_norm_g"] = _hg_bwd(proj, hg_o, hg_states, dmix, W["hg_lb_logits"], row(W["hg_norm_g"]), name="hg_bwd")
    dproj = jnp.concatenate([dsq, _mx(dsk), _mx(dsv), d_hg], axis=-1)
    G["w_in"] = _matmul(u0, dproj, mode="tn", out_dtypes=[F32], tn=896, tk=WGRAD_TOKENS, name="d_in_w")
    last = last_grads(G) if last_grads else None
    res = matmul_norm_bwd(dproj, W["w_in"], x, W["norm_mix_g"][0], dh1, last, tk=896, name="d_in_act")
    (dx, _, dg_mix0, _), got_last = res if last_grads else (res, [])
    G["norm_mix_g"] = jnp.concatenate([dg_mix0, dg_mix1], axis=0)
    return loss, dx, G, [(riding, got), (last, got_last)]


BIG = (("w_out_ab", "w_out", "rows2d"), ("conv_w_glu", "w_glu", "cols2d"), ("conv_w_pw", "w_pw", "rows2d"),
       ("w_ff1", "w_ff1", "cols3d"), ("w_ff2", "w_ff2", "rows3d"), ("w_in_ab", "w_in", "cols2d"))
LATE = BIG[:-1]
SMALL_SHARDED = ("conv_b_glu", "conv_w_dw", "conv_b_dw", "conv_ln_g", "conv_ln_b", "conv_b_pw")
REPLICATED = ("norm_mix_g", "norm_ffn_g", "hg_lb_logits", "hg_norm_g", "final_norm_g")
ORDER = ("norm_mix_g", "norm_ffn_g", "w_in_ab", "w_out_ab", "hg_lb_logits", "hg_norm_g", "conv_w_glu", "conv_b_glu",
         "conv_w_dw", "conv_b_dw", "conv_ln_g", "conv_ln_b", "conv_w_pw", "conv_b_pw", "w_ff1", "w_ff2", "final_norm_g")


def _step(x, loss_target, w, m, v):
    D = D_MODEL
    x2, t2 = x.reshape(-1, D), loss_target.reshape(-1, D)
    chip = 2 * lax.axis_index("x") + lax.axis_index("y")
    c = lax.axis_index("c")

    small_in = jnp.concatenate([w["conv_b_glu"].reshape(2, SHARD), w["conv_w_dw"].reshape(CONV_WIDTH, SHARD)] +
                               [w[n].reshape(1, SHARD) for n in ("conv_b_dw", "conv_ln_g", "conv_ln_b", "conv_b_pw")], axis=0)
    g_in, gs = _gather_chips([w["w_in_ab"].astype(MXU_DTYPE), _pad_rows(small_in, SMALL_IN_ROWS)]).run("gather_first_weights")
    vec = lambda r0, r1: gs[:, r0:r1].transpose(1, 0, 2).reshape(r1 - r0, D)
    W = {
        "w_in": _ChipWeight(g_in[:, 0], "cols"),
        "b_glu": gs[:, 0:2].reshape(2 * D), "w_dw": vec(2, 33), "b_dw": vec(33, 34)[0], "ln_g": vec(34, 35)[0],
        "ln_b": vec(35, 36)[0], "b_pw": vec(36, 37)[0],
        "norm_mix_g": w["norm_mix_g"], "norm_ffn_g": w["norm_ffn_g"], "hg_lb_logits": w["hg_lb_logits"],
        "hg_norm_g": w["hg_norm_g"], "final_norm_g": w["final_norm_g"],
    }
    late = _gather_chips([w[n].astype(MXU_DTYPE) for n, _, _ in LATE])

    def assemble(got):
        gw = dict(zip([s for _, s, _ in LATE], late.finish(got)))
        layers = lambda g, along: [_ChipWeight(g, along, (layer,)) for layer in range(2)]
        return {"w_out": gw["w_out"].reshape(D, D), "w_glu": _ChipWeight(gw["w_glu"][:, 0], "cols"), "w_pw": gw["w_pw"].reshape(D, D),
                "w_ff1": layers(gw["w_ff1"], "cols"), "w_ff2": layers(gw["w_ff2"], "rows")}

    def early_grads(G):
        return _owner_scatter(jnp.concatenate([_grad_blocks(G[s], kind) for _, s, kind in LATE], axis=2))

    def last_grads(G):
        blocks = _grad_blocks(G["w_in"], "cols2d")
        from_pair = _pair_swap(blocks.transpose(1, 0, 2, 3)).run("grads_pair_swap_in")
        return _chip_scatter(_add_pair(lax.dynamic_index_in_dim(blocks, c, axis=1, keepdims=False), from_pair, name="grads_pair_add_in"))

    loss, dx, G, riders = _forward_backward(x2, t2, W, (late, assemble), early_grads, last_grads)
    halves = [_sum_leading(copies.finish(got), name="grads_add_" + tag) for (copies, got), tag in zip(riders, ("late", "in"))]
    half = jnp.concatenate(halves, axis=0)
    full = _pair_gather(half).run("grads_pair_gather")

    pack = jnp.concatenate([
        G["norm_mix_g"], G["norm_ffn_g"], G["final_norm_g"], jnp.concatenate([G["hg_norm_g"], G["hg_lb"]], axis=1),
        _pad_rows(jnp.broadcast_to(loss, (1, D)), 2), G["b_glu"].reshape(2, D), G["w_dw"], G["b_dw"], G["ln_g"], G["ln_b"], G["b_pw"],
    ], axis=0)
    pack = _pad_rows(pack, SMALL_ROWS)
    (packs,) = _gather_all([pack]).run("gather_small_grads")
    ssum, d_logits = _small_reduce(packs, w["hg_lb_logits"], name="reduce_small_grads")
    cut = lambda r0, r1: lax.dynamic_slice(ssum, (r0, chip * SHARD), (r1 - r0, SHARD))
    grads = {
        "norm_mix_g": ssum[0:2], "norm_ffn_g": ssum[2:4], "final_norm_g": ssum[4], "hg_norm_g": ssum[5, :HG_WIDTH].reshape(1, HG_HEADS, HG_DH),
        "hg_lb_logits": d_logits,
        "conv_b_glu": lax.dynamic_slice(ssum[8:10].reshape(1, 2 * D), (0, chip * 2 * SHARD), (1, 2 * SHARD)),
        "conv_w_dw": cut(10, 10 + CONV_WIDTH).reshape(1, CONV_WIDTH, SHARD),
        "conv_b_dw": cut(42, 43), "conv_ln_g": cut(43, 44), "conv_ln_b": cut(44, 45), "conv_b_pw": cut(45, 46),
    }
    loss_out = ssum[6, 0]

    off = 0
    for n, s, kind in BIG:
        shard = w[n].shape
        rows = w[n].size // (2 * D)
        grads[n] = full[:, off:off + rows].reshape(shard)
        off += rows

    delta, new_m, new_v = {}, {}, {}
    for n, _, _ in BIG:
        view = lambda a: a.reshape(-1, a.shape[-1])
        outs = _adamw(view(w[n]), view(grads[n]), view(m[n]), view(v[n]), name="adamw_" + n)
        delta[n], new_m[n], new_v[n] = (o.reshape(w[n].shape) for o in outs)
    small = SMALL_SHARDED + REPLICATED
    sizes = [w[n].size for n in small]
    total = sum(sizes)
    rows = -(-total // (8 * D)) * 8
    packed = lambda d: _pad_rows(jnp.concatenate([d[n].reshape(-1) for n in small]).reshape(-1, 128), rows * 8).reshape(rows, D)
    outs = _adamw(packed(w), packed(grads), packed(m), packed(v), name="adamw_small")
    off = 0
    for n, size in zip(small, sizes):
        delta[n], new_m[n], new_v[n] = (o.reshape(-1)[off:off + size].reshape(w[n].shape) for o in outs)
        off += size
    grads = {n: grads[n].reshape(w[n].shape) for n in ORDER}
    return (loss_out, dx.reshape(x.shape), *[grads[n] for n in ORDER], *[delta[n] for n in ORDER],
            *[new_m[n] for n in ORDER], *[new_v[n] for n in ORDER])


def kernel(x, norm_mix_g, norm_ffn_g, w_in_ab, w_out_ab, hg_lb_logits, hg_norm_g, conv_w_glu, conv_b_glu, conv_w_dw, conv_b_dw, conv_ln_g, conv_ln_b, conv_w_pw, conv_b_pw, w_ff1, w_ff2, final_norm_g, loss_target, m_norm_mix_g, m_norm_ffn_g, m_w_in_ab, m_w_out_ab, m_hg_lb_logits, m_hg_norm_g, m_conv_w_glu, m_conv_b_glu, m_conv_w_dw, m_conv_b_dw, m_conv_ln_g, m_conv_ln_b, m_conv_w_pw, m_conv_b_pw, m_w_ff1, m_w_ff2, m_final_norm_g, v_norm_mix_g, v_norm_ffn_g, v_w_in_ab, v_w_out_ab, v_hg_lb_logits, v_hg_norm_g, v_conv_w_glu, v_conv_b_glu, v_conv_w_dw, v_conv_b_dw, v_conv_ln_g, v_conv_ln_b, v_conv_w_pw, v_conv_b_pw, v_w_ff1, v_w_ff2, v_final_norm_g):
    args = locals()
    w = {n: args[n] for n in ORDER}
    m = {n: args["m_" + n] for n in ORDER}
    v = {n: args["v_" + n] for n in ORDER}
    return _step(x, loss_target, w, m, v)
```

```python
import functools

import jax
import jax.numpy as jnp
from jax import lax
from jax.experimental import pallas as pl
from jax.experimental.pallas import tpu as pltpu

F32 = jnp.float32
MXU_DTYPE = jnp.bfloat16
MESH = pl.DeviceIdType.MESH

D_MODEL = 1024
SB_HEADS, SB_DH, SB_WIDTH = 8, 64, 512
SB_KEYS = 512
SB_SUB = 256
SB_ROWS_FWD, SB_ROWS_BWD = 512, 256
HG_HEADS, HG_DH, HG_WIDTH = 4, 128, 512
HG_CHUNK = 16
HG_TOKENS = 256
CONV_WIDTH = 31
CONV_HALO = 32
CONV_ROWS = 32
RMS_EPS = 1e-6
LN_EPS = 1e-5
N_CHIPS = 4
N_DEV = 8
SHARD = D_MODEL // N_CHIPS
SMALL_IN_ROWS = 40
SMALL_ROWS = 48
WGRAD_TOKENS = 2048
VMEM_LIMIT = 56 * 1024 * 1024

ADAM_LR, ADAM_B1, ADAM_B2, ADAM_EPS, ADAM_WD, ADAM_STEP = 0.001, 0.9, 0.999, 1e-08, 0.01, 10


def _params(*sem):
    return pltpu.CompilerParams(dimension_semantics=sem, vmem_limit_bytes=VMEM_LIMIT)


def _mx(v):
    return v.astype(MXU_DTYPE)


def _dot(a, b):
    return jnp.dot(_mx(a), _mx(b), preferred_element_type=F32)


def _dot_nt(a, b):
    return lax.dot_general(_mx(a), _mx(b), (((1,), (1,)), ((), ())), preferred_element_type=F32)


def _dot_tn(a, b):
    return lax.dot_general(_mx(a), _mx(b), (((0,), (0,)), ((), ())), preferred_element_type=F32)


def _neg_abs(x):
    bits = lax.bitcast_convert_type(x, jnp.uint32) | jnp.uint32(0x80000000)
    return lax.bitcast_convert_type(bits, F32)


def _key_order_sums(v, tri2, later):
    hi = _mx(v)
    lo = _mx(v - hi.astype(F32))
    n = v.shape[1] // SB_SUB
    blocks = [slice(b * SB_SUB, (b + 1) * SB_SUB) for b in range(n)]
    totals = [jnp.sum(v[:, sl], axis=1, keepdims=True) for sl in blocks]
    far, running = [None] * n, None
    for b in (reversed(range(n)) if later else range(n)):
        far[b] = running
        running = totals[b] if running is None else running + totals[b]
    sums = []
    for b, sl in enumerate(blocks):
        inside = jnp.dot(jnp.concatenate([hi[:, sl], lo[:, sl]], axis=1), tri2, preferred_element_type=F32)
        sums.append(inside if far[b] is None else inside + far[b])
    return jnp.concatenate(sums, axis=1), running


class _ChipWeight:
    def __init__(self, parts, along, lead=()):
        self.parts, self.along, self.lead = parts, along, tuple(lead)
        r, c = parts.shape[-2:]
        self.shape = (r, N_CHIPS * c) if along == "cols" else (N_CHIPS * r, c)

    def _gathered_is_n(self, mode):
        return (self.along == "cols") == (mode in ("nn", "tn"))

    def tile(self, mode, tn, tk):
        r, c = self.parts.shape[-2:]
        part = c if self.along == "cols" else r
        return (part, tk) if self._gathered_is_n(mode) else (tn, part)

    def spec(self, mode, tn, tk):
        squeezed = (None,) * (1 + len(self.lead))
        lead, cols, by_n = self.lead, self.along == "cols", self._gathered_is_n(mode)
        block = (tn, tk) if mode == "nt" else (tk, tn)

        def index(i, j, k):
            chip, other = (j, k) if by_n else (k, j)
            return (chip,) + lead + ((other, 0) if cols else (0, other))

        return pl.BlockSpec(squeezed + block, index)


def _matmul(a, b, *, mode, out_dtypes, epilogue=None, tiles=(), rows=(), n_sums=0, rider=None, tm=1024, tn=1024, tk=1024, name):
    b_shape = b.shape
    if mode == "nn":
        (M, K), N = a.shape, b_shape[1]
    elif mode == "nt":
        (M, K), N = a.shape, b_shape[0]
    else:
        (K, M), N = a.shape, b_shape[1]
    if isinstance(b, _ChipWeight):
        tn, tk = b.tile(mode, tn, tk)
    tm, tn, tk = min(tm, M), min(tn, N), min(tk, K)
    assert M % tm == 0 and N % tn == 0 and K % tk == 0, (name, M, N, K)
    nk = K // tk
    a_spec = pl.BlockSpec((tk, tm), lambda i, j, k: (k, i)) if mode == "tn" else pl.BlockSpec((tm, tk), lambda i, j, k: (i, k))
    if isinstance(b, _ChipWeight):
        b_spec, b = b.spec(mode, tn, tk), b.parts
    else:
        b_spec = pl.BlockSpec((tn, tk), lambda i, j, k: (j, k)) if mode == "nt" else pl.BlockSpec((tk, tn), lambda i, j, k: (k, j))
    dims = {"nn": ((1,), (0,)), "nt": ((1,), (1,)), "tn": ((0,), (0,))}[mode]
    n_t, n_r, n_o = len(tiles), len(rows), len(out_dtypes)
    n_x = 0 if rider is None else len(rider.arrays)
    grid = (M // tm, N // tn, nk)
    if epilogue is None:
        epilogue = lambda acc: (acc,)

    def body(a_ref, b_ref, *rest):
        extra, x_in, rest = rest[:n_t + n_r], rest[n_t + n_r:n_t + n_r + n_x], rest[n_t + n_r + n_x:]
        outs, sums, x_out, acc_ref, sems = rest[:n_o], rest[n_o:n_o + n_sums], rest[n_o + n_sums:n_o + n_sums + n_x], rest[n_o + n_sums + n_x], rest[n_o + n_sums + n_x + 1:]
        i, j, k = (pl.program_id(d) for d in range(3))
        if rider is not None:
            @pl.when((i == 0) & (j == 0) & (k == 0))
            def _():
                for cp in rider.make(x_in, x_out, *sems):
                    cp.start()

        @pl.when(k == 0)
        def _():
            acc_ref[...] = jnp.zeros_like(acc_ref)

        acc_ref[...] += lax.dot_general(_mx(a_ref[...]), _mx(b_ref[...]), (dims, ((), ())), preferred_element_type=F32)

        @pl.when(k == nk - 1)
        def _():
            res = epilogue(acc_ref[...], *[e[...] for e in extra])
            for o_ref, r in zip(outs, res[:n_o]):
                o_ref[...] = r.astype(o_ref.dtype)
            for s_ref, r in zip(sums, res[n_o:]):
                @pl.when(i == 0)
                def _():
                    s_ref[...] = jnp.zeros_like(s_ref)

                s_ref[...] += r

        if rider is not None:
            @pl.when((i == grid[0] - 1) & (j == grid[1] - 1) & (k == grid[2] - 1))
            def _():
                for cp in rider.make(x_in, x_out, *sems):
                    cp.wait()

    tile_spec = pl.BlockSpec((tm, tn), lambda i, j, k: (i, j))
    row_spec = pl.BlockSpec((1, tn), lambda i, j, k: (0, j))
    ordered = n_sums > 0 or rider is not None
    outs = pl.pallas_call(
        body, grid=grid,
        in_specs=[a_spec, b_spec] + [tile_spec] * n_t + [row_spec] * n_r + [HBM_SPEC] * n_x,
        out_specs=[tile_spec] * n_o + [row_spec] * n_sums + [HBM_SPEC] * n_x,
        out_shape=[jax.ShapeDtypeStruct((M, N), dt) for dt in out_dtypes] + [jax.ShapeDtypeStruct((1, N), F32)] * n_sums
        + ([] if rider is None else rider.out_shapes),
        scratch_shapes=[pltpu.VMEM((tm, tn), F32)] + ([] if rider is None else rider.scratch()),
        compiler_params=_params(*(("arbitrary",) * 3 if ordered else ("parallel", "parallel", "arbitrary"))), name=name,
    )(a, b, *tiles, *rows, *([] if rider is None else rider.arrays))
    res = outs[0] if n_o + n_sums == 1 else outs[:n_o + n_sums]
    return res if rider is None else (res, outs[n_o + n_sums:])


def _token_block(T):
    return min(512, T)


def _rmsnorm_fwd(h, g, *, name):
    T, D = h.shape
    tb = _token_block(T)

    def body(h_ref, g_ref, u_ref):
        x = h_ref[...]
        r = lax.rsqrt(jnp.mean(x * x, axis=-1, keepdims=True) + RMS_EPS)
        u_ref[...] = (x * r * g_ref[...]).astype(u_ref.dtype)

    blk = pl.BlockSpec((tb, D), lambda i: (i, 0))
    return pl.pallas_call(
        body, grid=(T // tb,), in_specs=[blk, pl.BlockSpec((1, D), lambda i: (0, 0))], out_specs=blk,
        out_shape=jax.ShapeDtypeStruct((T, D), MXU_DTYPE), compiler_params=_params("parallel"), name=name,
    )(h, g)


def _rms_bwd_math(x, g, du):
    r = lax.rsqrt(jnp.mean(x * x, axis=-1, keepdims=True) + RMS_EPS)
    gd = g * du
    dx = r * gd - x * (r * r * r) * jnp.mean(gd * x, axis=-1, keepdims=True)
    return dx, du * x * r


def _loss_head(h, g, target, *, name):
    T, D = h.shape
    tb = _token_block(T)

    def body(h_ref, g_ref, t_ref, dh_ref, dhm_ref, dg_ref, loss_ref):
        @pl.when(pl.program_id(0) == 0)
        def _():
            dg_ref[...] = jnp.zeros_like(dg_ref)
            loss_ref[...] = jnp.zeros_like(loss_ref)

        x, gg = h_ref[...], g_ref[...]
        r = lax.rsqrt(jnp.mean(x * x, axis=-1, keepdims=True) + RMS_EPS)
        diff = x * r * gg - t_ref[...]
        per_token = jnp.mean(diff * diff, axis=-1, keepdims=True)
        loss_ref[...] += 0.5 * jnp.sum(per_token, axis=0, keepdims=True)
        dx, dg_terms = _rms_bwd_math(x, gg, diff / D)
        dh_ref[...] = dx
        dhm_ref[...] = dx.astype(dhm_ref.dtype)
        dg_ref[...] += jnp.sum(dg_terms, axis=0, keepdims=True)

    blk = pl.BlockSpec((tb, D), lambda i: (i, 0))
    row = pl.BlockSpec((1, D), lambda i: (0, 0))
    return pl.pallas_call(
        body, grid=(T // tb,), in_specs=[blk, row, blk], out_specs=[blk, blk, row, pl.BlockSpec((1, 1), lambda i: (0, 0))],
        out_shape=[jax.ShapeDtypeStruct((T, D), F32), jax.ShapeDtypeStruct((T, D), MXU_DTYPE), jax.ShapeDtypeStruct((1, D), F32),
                   jax.ShapeDtypeStruct((1, 1), F32)],
        compiler_params=_params("arbitrary"), name=name,
    )(h, g, target)


def _sb_scores(qm, ks, later, tri, mask, need_log_beta=True):
    z = _dot_nt(qm, ks)
    sp = jnp.maximum(z, 0.0) + jnp.log(1.0 + jnp.exp(_neg_abs(z)))
    lb = z - sp if need_log_beta else None
    if mask is not None:
        sp = jnp.where(mask, sp, 0.0)
    after, total = _key_order_sums(sp, tri, later=True)
    w = jnp.exp((lb if need_log_beta else z) - (after + later))
    if mask is not None:
        w = jnp.where(mask, w, 0.0)
    return total, lb, w


def _sb_setup(q_ref, rows, inclusive=False):
    i, hsel = pl.program_id(1), pl.program_id(2)
    lane = lax.broadcasted_iota(jnp.int32, (rows, 2 * SB_DH), 1)
    mine = (lane >= SB_DH) == (hsel == 1)
    diag = (i * rows) // SB_KEYS
    t = i * rows + lax.broadcasted_iota(jnp.int32, (rows, SB_KEYS), 0)
    s = diag * SB_KEYS + lax.broadcasted_iota(jnp.int32, (rows, SB_KEYS), 1)
    a = lax.broadcasted_iota(jnp.int32, (2 * SB_SUB, SB_SUB), 0) % SB_SUB
    b = lax.broadcasted_iota(jnp.int32, (2 * SB_SUB, SB_SUB), 1)
    return i, hsel, mine, diag, s < t, _mx(a >= b if inclusive else a > b), _mx(a < b)


def _sb_keys(j, n=1, keys=None):
    return pl.ds(pl.multiple_of(j * SB_KEYS, SB_KEYS), n * SB_KEYS if keys is None else keys)


def _sb_descend(n, step, carry, wide):
    pair = (lambda j, cr: step(j, 2, cr)) if wide else (lambda j, cr: step(j, 1, step(j + 1, 1, cr)))
    carry = lax.fori_loop(0, n // 2, lambda it, cr: pair(n - 2 - 2 * it, cr), carry)
    return lax.cond(n % 2 == 1, lambda cr: step(0, 1, cr), lambda cr: cr, carry)


def _sb_ascend(n, step, carry):
    odd = n % 2
    carry = lax.cond(odd == 1, lambda cr: step(0, 1, cr), lambda cr: cr, carry)
    return lax.fori_loop(0, n // 2, lambda it, cr: step(odd + 2 * it, 2, cr), carry)


def _sb_call(body, qkv, extra_in, out_blocks, out_dtypes, scratch, rider, rows, *, name):
    T = qkv.shape[0]
    n_pairs = SB_HEADS // 2
    grid = (n_pairs, T // rows, 2)
    pair = lambda col0: pl.BlockSpec((rows, 2 * SB_DH), lambda p, i, h: (i, col0 + p))
    whole = lambda col0: pl.BlockSpec((T, 2 * SB_DH), lambda p, i, h: (0, col0 + p))
    in_specs = [pair(0), whole(n_pairs), whole(2 * n_pairs)] + [pair(0)] * len(extra_in)
    out_specs = [pair(0) if kind == "pair" else whole(0) for kind in out_blocks]
    n_in, n_out, n_r = len(in_specs), len(out_specs), 0 if rider is None else len(rider.arrays)

    def kernel_body(*refs):
        ins, r_in = refs[:n_in], refs[n_in:n_in + n_r]
        outs, r_out = refs[n_in + n_r:n_in + n_r + n_out], refs[n_in + n_r + n_out:n_in + 2 * n_r + n_out]
        rest = refs[n_in + 2 * n_r + n_out:]
        ids = [pl.program_id(a) for a in range(3)]
        if rider is not None:
            @pl.when((ids[0] == 0) & (ids[1] == 0) & (ids[2] == 0))
            def _():
                for cp in rider.make(r_in, r_out, *rest[len(scratch):]):
                    cp.start()

        body(ins, outs, rest[:len(scratch)])
        if rider is not None:
            @pl.when((ids[0] == grid[0] - 1) & (ids[1] == grid[1] - 1) & (ids[2] == grid[2] - 1))
            def _():
                for cp in rider.make(r_in, r_out, *rest[len(scratch):]):
                    cp.wait()

    res = pl.pallas_call(
        kernel_body, grid=grid, in_specs=in_specs + [HBM_SPEC] * n_r, out_specs=out_specs + [HBM_SPEC] * n_r,
        out_shape=[jax.ShapeDtypeStruct((T, SB_WIDTH), dt) for dt in out_dtypes] + ([] if rider is None else rider.out_shapes),
        scratch_shapes=list(scratch) + ([] if rider is None else rider.scratch()),
        compiler_params=_params("arbitrary", "arbitrary", "arbitrary"), name=name,
    )(qkv, qkv, qkv, *extra_in, *([] if rider is None else rider.arrays))
    return res[:n_out], res[n_out:]


def _sb_fwd(qkv, rider=None, *, name):
    rows = min(SB_ROWS_FWD, qkv.shape[0])
    scale = SB_DH ** -0.5

    def body(ins, outs, _):
        (q_ref, k_ref, v_ref), (o_ref,) = ins, outs
        i, hsel, mine, diag, mask, tri, _ = _sb_setup(q_ref, rows, inclusive=True)
        qm = jnp.where(mine, q_ref[...], 0) * scale

        def tile(j, n, m, later, acc):
            total, _, w = _sb_scores(qm, k_ref[_sb_keys(j, n), :], later, tri, m, need_log_beta=False)
            return later + total, acc + _dot(w, v_ref[_sb_keys(j, n), :])

        carry = tile(diag, 1, mask, jnp.zeros((rows, 1), F32), jnp.zeros((rows, 2 * SB_DH), F32))
        _, acc = _sb_descend(diag, lambda j, n, cr: tile(j, n, None, *cr), carry, wide=False)
        res = jnp.where(mine, acc, 0.0).astype(o_ref.dtype)

        @pl.when(hsel == 0)
        def _():
            o_ref[...] = res

        @pl.when(hsel == 1)
        def _():
            o_ref[...] += res

    (o,), got = _sb_call(body, qkv, [], ["pair"], [MXU_DTYPE], [], rider, rows, name=name)
    return o, got


def _sb_bwd(qkv, dmix, rider=None, *, name):
    T = qkv.shape[0]
    rows = min(SB_ROWS_BWD, T)
    scale = SB_DH ** -0.5

    def body(ins, outs, scratch):
        (q_ref, k_ref, v_ref, do_ref), (dq_ref, dk_ref, dv_ref), (da_ref, beta_ref) = ins, outs, scratch
        i, hsel, mine, diag, mask, tri, tri_before = _sb_setup(q_ref, rows)

        @pl.when((i == 0) & (hsel == 0))
        def _():
            dk_ref[...] = jnp.zeros_like(dk_ref)
            dv_ref[...] = jnp.zeros_like(dv_ref)

        qm = jnp.where(mine, q_ref[...], 0) * scale
        do_m = _mx(jnp.where(mine, do_ref[...], 0.0))

        def weights(j, n, m, later, keys=None):
            sl = _sb_keys(j, n, keys)
            total, lb, w = _sb_scores(qm, k_ref[sl, :], later, tri, m)
            da, beta = _dot_nt(do_m, v_ref[sl, :]) * w, jnp.exp(lb)
            if keys is None:
                for t in range(n):
                    da_ref[j + t] = da[:, t * SB_KEYS:(t + 1) * SB_KEYS]
                    beta_ref[j + t] = beta[:, t * SB_KEYS:(t + 1) * SB_KEYS]
            else:
                da_ref[j, :, 0:keys] = da
                beta_ref[j, :, 0:keys] = beta
            dv_ref[sl, :] += _dot_tn(w, do_m)
            return later + total

        short = (i * rows) % SB_KEYS + rows <= SB_SUB
        short_mask = mask[:, 0:SB_SUB]
        later = lax.cond(short, lambda z: weights(diag, 1, short_mask, z, SB_SUB), lambda z: weights(diag, 1, mask, z),
                         jnp.zeros((rows, 1), F32))
        _sb_descend(diag, lambda j, n, c: weights(j, n, None, c), later, wide=True)

        def logits(j, n, m, before, dq, keys=None):
            if keys is None:
                da = jnp.concatenate([da_ref[j + t] for t in range(n)], axis=1)
                beta = jnp.concatenate([beta_ref[j + t] for t in range(n)], axis=1)
            else:
                da, beta = da_ref[j, :, 0:keys], beta_ref[j, :, 0:keys]
            earlier, total = _key_order_sums(da, tri_before, later=False)
            dz = da - beta * (da + earlier + before)
            if m is not None:
                dz = jnp.where(m, dz, 0.0)
            dz = _mx(dz)
            dk_ref[_sb_keys(j, n, keys), :] += _dot_tn(dz, qm)
            return before + total, dq + _dot(dz, k_ref[_sb_keys(j, n, keys), :])

        carry = (jnp.zeros((rows, 1), F32), jnp.zeros((rows, 2 * SB_DH), F32))
        carry = _sb_ascend(diag, lambda j, n, cr: logits(j, n, None, *cr), carry)
        dq = lax.cond(short, lambda cr: logits(diag, 1, short_mask, *cr, keys=SB_SUB)[1], lambda cr: logits(diag, 1, mask, *cr)[1], carry)
        res = jnp.where(mine, dq * scale, 0.0).astype(dq_ref.dtype)

        @pl.when(hsel == 0)
        def _():
            dq_ref[...] = res

        @pl.when(hsel == 1)
        def _():
            dq_ref[...] += res

    n_tiles = T // SB_KEYS
    scratch = [pltpu.VMEM((n_tiles, rows, SB_KEYS), F32), pltpu.VMEM((n_tiles, rows, SB_KEYS), F32)]
    return _sb_call(body, qkv, [dmix], ["pair", "whole", "whole"], [MXU_DTYPE, F32, F32], scratch, rider, rows, name=name)


def _chunk_row(n):
    return lax.broadcasted_iota(jnp.int32, (n, HG_DH), 0) % HG_CHUNK


def _chunk_cumsum(x, row, reverse=False):
    n = x.shape[0]
    for sh in (1, 2, 4, 8):
        if reverse:
            x = x + jnp.where(row < HG_CHUNK - sh, pltpu.roll(x, n - sh, 0), 0.0)
        else:
            x = x + jnp.where(row >= sh, pltpu.roll(x, sh, 0), 0.0)
    return x


def _hg_lower_bound(logits_ref):
    lg = logits_ref[...]
    e = jnp.exp(lg - jnp.max(lg, axis=0, keepdims=True))
    return e[0:1, :] / jnp.sum(e, axis=0, keepdims=True)


def _hg_terms(fr, q, lb, row):
    sig = jax.nn.sigmoid(fr)
    f = lb + (1.0 - lb) * sig
    kk = 1.0 - f
    g = jnp.log(f)
    G = _chunk_cumsum(g, row)
    g_last = G + (_chunk_cumsum(g, row, reverse=True) - g)
    e_g, e_ng, e_lg = jnp.exp(G), jnp.exp(-G), jnp.exp(g_last - G)
    return dict(sig=sig, f=f, kk=kk, e_g=e_g, e_ng=e_ng, e_lg=e_lg, q_dec=q * e_g, k_intra=kk * e_ng,
                k_state=kk * e_lg, decay=jnp.exp(g_last))


def _hg_causal(n):
    t = lax.broadcasted_iota(jnp.int32, (n, n), 0)
    s = lax.broadcasted_iota(jnp.int32, (n, n), 1)
    return (s <= t) & (s // HG_CHUNK == t // HG_CHUNK)


def _chunks(a):
    return a.reshape(a.shape[0] // HG_CHUNK, HG_CHUNK, a.shape[1])


def _per_chunk(lhs, rhs, contract):
    return lax.dot_general(_mx(lhs), _mx(rhs), ((contract[:1], contract[1:]), ((0,), (0,))), preferred_element_type=F32)


def _hg_specs(T, tb, col0, order):
    return [pl.BlockSpec((tb, HG_WIDTH), functools.partial(lambda i, j: (order(i), j), j=col0 + j)) for j in range(4)]


def _hg_fwd(proj, logits, norm_g, *, name):
    T = proj.shape[0]
    tb = min(HG_TOKENS, T)
    nch = tb // HG_CHUNK

    def body(q_ref, f_ref, i_ref, gate_ref, lg_ref, ng_ref, out_ref, o_ref, s_ref, st_ref, inc_ref, dec_ref):
        @pl.when(pl.program_id(0) == 0)
        def _():
            st_ref[...] = jnp.zeros_like(st_ref)

        lb_all = _hg_lower_bound(lg_ref)
        row = _chunk_row(tb)
        causal = _hg_causal(tb)
        for hh in range(HG_HEADS):
            cols = slice(hh * HG_DH, (hh + 1) * HG_DH)
            t = _hg_terms(f_ref[:, cols], q_ref[:, cols], lb_all[:, cols], row)
            v = i_ref[:, cols]
            scores = jnp.where(causal, _dot_nt(t["q_dec"], t["k_intra"]), 0.0)
            o_intra = _dot(scores, v)
            inc_ref[...] = _per_chunk(_chunks(v), _chunks(t["k_state"]), (1, 1))
            dec_ref[...] = _chunks(t["decay"])

            def step(ci, st):
                s_ref[ci, hh] = st
                return st * dec_ref[ci][0:1, :] + inc_ref[ci]

            st_ref[hh] = lax.fori_loop(0, nch, step, st_ref[hh], unroll=4)
            o_inter = _per_chunk(_chunks(t["q_dec"]), s_ref[:, hh], (2, 2))
            o = o_intra + o_inter.reshape(tb, HG_DH)
            o_ref[:, cols] = o
            gate = gate_ref[:, cols]
            on = o * lax.rsqrt(jnp.mean(o * o, axis=-1, keepdims=True) + RMS_EPS) * ng_ref[:, cols]
            out_ref[:, cols] = (on * (gate * jax.nn.sigmoid(gate))).astype(out_ref.dtype)

    blk = pl.BlockSpec((tb, HG_WIDTH), lambda i: (i, 0))
    return pl.pallas_call(
        body, grid=(T // tb,),
        in_specs=_hg_specs(T, tb, 3, lambda i: i) + [pl.BlockSpec((3, HG_WIDTH), lambda i: (0, 0)), pl.BlockSpec((1, HG_WIDTH), lambda i: (0, 0))],
        out_specs=[blk, blk, pl.BlockSpec((nch, HG_HEADS, HG_DH, HG_DH), lambda i: (i, 0, 0, 0))],
        out_shape=[jax.ShapeDtypeStruct((T, HG_WIDTH), MXU_DTYPE), jax.ShapeDtypeStruct((T, HG_WIDTH), F32),
                   jax.ShapeDtypeStruct((T // HG_CHUNK, HG_HEADS, HG_DH, HG_DH), F32)],
        scratch_shapes=[pltpu.VMEM((HG_HEADS, HG_DH, HG_DH), F32), pltpu.VMEM((nch, HG_DH, HG_DH), F32),
                        pltpu.VMEM((nch, HG_CHUNK, HG_DH), F32)],
        compiler_params=_params("arbitrary"), name=name,
    )(proj, proj, proj, proj, logits, norm_g)


def _hg_bwd(proj, o_raw, states, dmix, logits, norm_g, *, name):
    T = proj.shape[0]
    tb = min(HG_TOKENS, T)
    nch = tb // HG_CHUNK
    nb = T // tb
    rev = lambda i: nb - 1 - i

    def body(q_ref, f_ref, i_ref, gate_ref, o_ref, s_ref, dout_ref, lg_ref, ng_ref, dp_ref, dlb_ref, dng_ref,
             dst_ref, inc_ref, dec_ref, after_ref):
        @pl.when(pl.program_id(0) == 0)
        def _():
            dst_ref[...] = jnp.zeros_like(dst_ref)
            dlb_ref[...] = jnp.zeros_like(dlb_ref)
            dng_ref[...] = jnp.zeros_like(dng_ref)

        lb_all = _hg_lower_bound(lg_ref)
        row = _chunk_row(tb)
        causal = _hg_causal(tb)
        for hh in range(HG_HEADS):
            cols = slice(hh * HG_DH, (hh + 1) * HG_DH)
            out_cols = lambda part: slice(part * HG_WIDTH + hh * HG_DH, part * HG_WIDTH + (hh + 1) * HG_DH)
            o, gate, dout, ng, lb = o_ref[:, cols], gate_ref[:, cols], dout_ref[:, cols], ng_ref[:, cols], lb_all[:, cols]
            sg = jax.nn.sigmoid(gate)
            r = lax.rsqrt(jnp.mean(o * o, axis=-1, keepdims=True) + RMS_EPS)
            oh = o * r
            dp_ref[:, out_cols(3)] = (dout * (oh * ng) * (sg * (1.0 + gate * (1.0 - sg)))).astype(dp_ref.dtype)
            don = dout * (gate * sg)
            dng_ref[:, cols] += jnp.sum(don * oh, axis=0, keepdims=True)
            doh = don * ng
            do = r * (doh - oh * jnp.mean(doh * oh, axis=-1, keepdims=True))

            t = _hg_terms(f_ref[:, cols], q_ref[:, cols], lb, row)
            v = i_ref[:, cols]
            scores = jnp.where(causal, _dot_nt(t["q_dec"], t["k_intra"]), 0.0)
            dscores = jnp.where(causal, _dot_nt(do, v), 0.0)
            inc_ref[...] = _per_chunk(_chunks(do), _chunks(t["q_dec"]), (1, 1))
            dec_ref[...] = _chunks(t["decay"])

            def step(it, dst):
                ci = nch - 1 - it
                after_ref[ci] = dst
                return dst * dec_ref[ci][0:1, :] + inc_ref[ci]

            dst_ref[hh] = lax.fori_loop(0, nch, step, dst_ref[hh], unroll=4)
            st, dst = s_ref[:, hh], after_ref[...]
            dqd = _dot(dscores, t["k_intra"]) + _per_chunk(_chunks(do), st, (2, 1)).reshape(tb, HG_DH)
            dki = _dot_tn(dscores, t["q_dec"])
            dks = _per_chunk(_chunks(v), dst, (2, 1)).reshape(tb, HG_DH)
            dp_ref[:, out_cols(2)] = (_dot_tn(scores, do) + _per_chunk(_chunks(t["k_state"]), dst, (2, 2)).reshape(tb, HG_DH)).astype(dp_ref.dtype)
            ddecay = jnp.broadcast_to(jnp.sum(st * dst, axis=1, keepdims=True), (nch, HG_CHUNK, HG_DH)).reshape(tb, HG_DH)
            dks_ks = dks * t["k_state"]
            d_glast = _chunk_cumsum(dks_ks, row) + ddecay * t["decay"]
            d_g = dqd * t["q_dec"] - dki * t["k_intra"] - dks_ks + jnp.where(row == HG_CHUNK - 1, d_glast, 0.0)
            df = _chunk_cumsum(d_g, row, reverse=True) / t["f"] - (dki * t["e_ng"] + dks * t["e_lg"])
            dp_ref[:, out_cols(0)] = (dqd * t["e_g"]).astype(dp_ref.dtype)
            dp_ref[:, out_cols(1)] = (df * (1.0 - lb) * t["sig"] * (1.0 - t["sig"])).astype(dp_ref.dtype)
            dlb_ref[:, cols] += jnp.sum(df * (1.0 - t["sig"]), axis=0, keepdims=True)

    blk = pl.BlockSpec((tb, HG_WIDTH), lambda i: (rev(i), 0))
    row_spec = pl.BlockSpec((1, HG_WIDTH), lambda i: (0, 0))
    return pl.pallas_call(
        body, grid=(nb,),
        in_specs=_hg_specs(T, tb, 3, rev) + [
            blk, pl.BlockSpec((nch, HG_HEADS, HG_DH, HG_DH), lambda i: (rev(i), 0, 0, 0)),
            pl.BlockSpec((tb, HG_WIDTH), lambda i: (rev(i), 1)), pl.BlockSpec((3, HG_WIDTH), lambda i: (0, 0)), row_spec],
        out_specs=[pl.BlockSpec((tb, 4 * HG_WIDTH), lambda i: (rev(i), 0)), row_spec, row_spec],
        out_shape=[jax.ShapeDtypeStruct((T, 4 * HG_WIDTH), MXU_DTYPE), jax.ShapeDtypeStruct((1, HG_WIDTH), F32), jax.ShapeDtypeStruct((1, HG_WIDTH), F32)],
        scratch_shapes=[pltpu.VMEM((HG_HEADS, HG_DH, HG_DH), F32), pltpu.VMEM((nch, HG_DH, HG_DH), F32),
                        pltpu.VMEM((nch, HG_CHUNK, HG_DH), F32), pltpu.VMEM((nch, HG_DH, HG_DH), F32)],
        compiler_params=_params("arbitrary"), name=name,
    )(proj, proj, proj, proj, o_raw, states, dmix, logits, norm_g)


def _shifted_copies(sh_ref, n_rows):
    keep = n_rows + CONV_HALO - 8
    for b in range(1, 8):
        sh_ref[b, 0:keep, :] = sh_ref[0, b:b + keep, :]


def _tap_rows(sh_ref, offset, r0, lanes):
    start = pl.multiple_of(r0 + (offset - offset % 8), 8)
    return sh_ref[offset % 8, pl.ds(start, CONV_ROWS), lanes]


def _conv_fwd(p, w_dw, b_dw, ln_g, ln_b, *, name):
    T, D = p.shape[0], p.shape[1] // 2
    tb = _token_block(T)
    hpb = tb // CONV_HALO
    lane_step = 512

    def body(p1_ref, p2_ref, q1_ref, q2_ref, w_ref, bdw_ref, g_ref, b_ref, a_ref, y_ref, act_ref, sh_ref):
        i = pl.program_id(0)
        a = p1_ref[...] * jax.nn.sigmoid(p2_ref[...])
        sh_ref[0, 0:CONV_HALO, :] = jnp.where(i > 0, q1_ref[...] * jax.nn.sigmoid(q2_ref[...]), 0.0)
        sh_ref[0, CONV_HALO:, :] = a
        a_ref[...] = a
        _shifted_copies(sh_ref, tb)

        def chunk(ci, _):
            r0 = pl.multiple_of(ci * CONV_ROWS, CONV_ROWS)
            for l0 in range(0, D, lane_step):
                lanes = slice(l0, l0 + lane_step)
                acc = jnp.broadcast_to(bdw_ref[:, lanes], (CONV_ROWS, lane_step))
                for k in range(CONV_WIDTH):
                    acc = acc + _tap_rows(sh_ref, CONV_HALO - CONV_WIDTH + 1 + k, r0, lanes) * w_ref[k:k + 1, lanes]
                y_ref[pl.ds(r0, CONV_ROWS), lanes] = acc
            return 0

        lax.fori_loop(0, tb // CONV_ROWS, chunk, 0)
        y = y_ref[...]
        mu = jnp.mean(y, axis=-1, keepdims=True)
        yc = y - mu
        s = yc * lax.rsqrt(jnp.mean(yc * yc, axis=-1, keepdims=True) + LN_EPS) * g_ref[...] + b_ref[...]
        act_ref[...] = (s * jax.nn.sigmoid(s)).astype(act_ref.dtype)

    prev = lambda i: jnp.maximum(i * hpb - 1, 0)
    blk = pl.BlockSpec((tb, D), lambda i: (i, 0))
    row = pl.BlockSpec((1, D), lambda i: (0, 0))
    return pl.pallas_call(
        body, grid=(T // tb,),
        in_specs=[blk, pl.BlockSpec((tb, D), lambda i: (i, 1)), pl.BlockSpec((CONV_HALO, D), lambda i: (prev(i), 0)),
                  pl.BlockSpec((CONV_HALO, D), lambda i: (prev(i), 1)), pl.BlockSpec((CONV_HALO, D), lambda i: (0, 0)), row, row, row],
        out_specs=[blk, blk, blk],
        out_shape=[jax.ShapeDtypeStruct((T, D), F32), jax.ShapeDtypeStruct((T, D), F32), jax.ShapeDtypeStruct((T, D), MXU_DTYPE)],
        scratch_shapes=[pltpu.VMEM((8, tb + CONV_HALO, D), F32)],
        compiler_params=_params("parallel"), name=name,
    )(p, p, p, p, w_dw, b_dw, ln_g, ln_b)


def _conv_bwd_norm(dact, y, ln_g, ln_b, *, name):
    T, D = y.shape
    tb = _token_block(T)

    def body(da_ref, y_ref, g_ref, b_ref, dy_ref, dg_ref, db_ref, cs_ref):
        @pl.when(pl.program_id(0) == 0)
        def _():
            dg_ref[...] = jnp.zeros_like(dg_ref)
            db_ref[...] = jnp.zeros_like(db_ref)
            cs_ref[...] = jnp.zeros_like(cs_ref)

        y, g = y_ref[...], g_ref[...]
        yc = y - jnp.mean(y, axis=-1, keepdims=True)
        rs = lax.rsqrt(jnp.mean(yc * yc, axis=-1, keepdims=True) + LN_EPS)
        yn = yc * rs
        s = yn * g + b_ref[...]
        sg = jax.nn.sigmoid(s)
        ds = da_ref[...] * (sg * (1.0 + s * (1.0 - sg)))
        dg_ref[...] += jnp.sum(ds * yn, axis=0, keepdims=True)
        db_ref[...] += jnp.sum(ds, axis=0, keepdims=True)
        dyn = ds * g
        dy = rs * (dyn - jnp.mean(dyn, axis=-1, keepdims=True) - yn * jnp.mean(dyn * yn, axis=-1, keepdims=True))
        dy_ref[...] = dy
        cs_ref[...] += jnp.sum(dy, axis=0, keepdims=True)

    blk = pl.BlockSpec((tb, D), lambda i: (i, 0))
    row = pl.BlockSpec((1, D), lambda i: (0, 0))
    rs_ = jax.ShapeDtypeStruct((1, D), F32)
    return pl.pallas_call(
        body, grid=(T // tb,), in_specs=[blk, blk, row, row], out_specs=[blk, row, row, row],
        out_shape=[jax.ShapeDtypeStruct((T, D), F32), rs_, rs_, rs_], compiler_params=_params("arbitrary"), name=name,
    )(dact, y, ln_g, ln_b)


def _conv_bwd_taps(dy, a, p, w_dw, *, name):
    T, D = dy.shape
    tb = _token_block(T)
    hpb = tb // CONV_HALO
    last = T // CONV_HALO - 1
    nb = T // tb
    lane_step = 128
    groups = CONV_ROWS // 8

    def body(dy_ref, dyn_ref, a_ref, p1_ref, p2_ref, w_ref, dp_ref, dw_ref, cs_ref, sh_ref, da_ref):
        i = pl.program_id(0)

        @pl.when(i == 0)
        def _():
            dw_ref[...] = jnp.zeros_like(dw_ref)
            cs_ref[...] = jnp.zeros_like(cs_ref)

        sh_ref[0, 0:tb, :] = dy_ref[...]
        sh_ref[0, tb:, :] = jnp.where(i < nb - 1, dyn_ref[...], 0.0)
        _shifted_copies(sh_ref, tb)
        for l0 in range(0, D, lane_step):
            lanes = slice(l0, l0 + lane_step)
            for taps in (range(0, CONV_WIDTH // 2 + 1), range(CONV_WIDTH // 2 + 1, CONV_WIDTH)):
                def chunk(ci, sums, taps=taps):
                    r0 = pl.multiple_of(ci * CONV_ROWS, CONV_ROWS)
                    a_c = a_ref[pl.ds(r0, CONV_ROWS), lanes]
                    da = jnp.zeros((CONV_ROWS, lane_step), F32) if taps[0] == 0 else da_ref[pl.ds(r0, CONV_ROWS), lanes]
                    new = []
                    for n, k in enumerate(taps):
                        s_k = _tap_rows(sh_ref, CONV_WIDTH - 1 - k, r0, lanes)
                        da = da + s_k * w_ref[k:k + 1, lanes]
                        new.append(sums[n] + jnp.sum((s_k * a_c).reshape(groups, 8, lane_step), axis=0))
                    da_ref[pl.ds(r0, CONV_ROWS), lanes] = da
                    return tuple(new)

                sums = lax.fori_loop(0, tb // CONV_ROWS, chunk, tuple(jnp.zeros((8, lane_step), F32) for _ in taps))
                for n, k in enumerate(taps):
                    dw_ref[k:k + 1, lanes] += jnp.sum(sums[n], axis=0, keepdims=True)
        da = da_ref[...]
        p1 = p1_ref[...]
        sg = jax.nn.sigmoid(p2_ref[...])
        dp1 = da * sg
        dp2 = da * p1 * (sg * (1.0 - sg))
        dp_ref[:, 0:D] = dp1.astype(dp_ref.dtype)
        dp_ref[:, D:] = dp2.astype(dp_ref.dtype)
        cs_ref[:, 0:D] += jnp.sum(dp1, axis=0, keepdims=True)
        cs_ref[:, D:] += jnp.sum(dp2, axis=0, keepdims=True)

    blk = pl.BlockSpec((tb, D), lambda i: (i, 0))
    return pl.pallas_call(
        body, grid=(nb,),
        in_specs=[blk, pl.BlockSpec((CONV_HALO, D), lambda i: (jnp.minimum((i + 1) * hpb, last), 0)), blk, blk,
                  pl.BlockSpec((tb, D), lambda i: (i, 1)), pl.BlockSpec((CONV_HALO, D), lambda i: (0, 0))],
        out_specs=[pl.BlockSpec((tb, 2 * D), lambda i: (i, 0)), pl.BlockSpec((CONV_HALO, D), lambda i: (0, 0)), pl.BlockSpec((1, 2 * D), lambda i: (0, 0))],
        out_shape=[jax.ShapeDtypeStruct((T, 2 * D), MXU_DTYPE), jax.ShapeDtypeStruct((CONV_HALO, D), F32), jax.ShapeDtypeStruct((1, 2 * D), F32)],
        scratch_shapes=[pltpu.VMEM((8, tb + CONV_HALO, D), F32), pltpu.VMEM((tb, D), F32)],
        compiler_params=_params("arbitrary"), name=name,
    )(dy, dy, a, p, p, w_dw)


def _row_block(rows):
    for tr in (512, 256, 128, 64, 32, 16, 8):
        if rows % tr == 0:
            return tr
    return rows


def _sum_leading(x, *, name):
    n, R, C = x.shape
    tr = _row_block(R)

    def body(x_ref, o_ref):
        acc = x_ref[0]
        for j in range(1, n):
            acc = acc + x_ref[j]
        o_ref[...] = acc

    return pl.pallas_call(
        body, grid=(R // tr,), in_specs=[pl.BlockSpec((n, tr, C), lambda i: (0, i, 0))], out_specs=pl.BlockSpec((tr, C), lambda i: (i, 0)),
        out_shape=jax.ShapeDtypeStruct((R, C), x.dtype), compiler_params=_params("parallel"), name=name,
    )(x)


def _add_pair(x, y, *, name):
    n, R, C = x.shape
    tr = _row_block(R)

    def body(x_ref, y_ref, o_ref):
        o_ref[...] = x_ref[...] + y_ref[...]

    blk = pl.BlockSpec((1, tr, C), lambda j, i: (j, i, 0))
    return pl.pallas_call(
        body, grid=(n, R // tr), in_specs=[blk, blk], out_specs=blk,
        out_shape=jax.ShapeDtypeStruct((n, R, C), x.dtype), compiler_params=_params("parallel", "parallel"), name=name,
    )(x, y)


def _adamw(w, g, m, v, *, name):
    R, C = w.shape
    tr = _row_block(R)
    c1, c2 = 1.0 - ADAM_B1 ** ADAM_STEP, 1.0 - ADAM_B2 ** ADAM_STEP

    def body(w_ref, g_ref, m_ref, v_ref, d_ref, nm_ref, nv_ref):
        g_ = g_ref[...]
        nm = ADAM_B1 * m_ref[...] + (1.0 - ADAM_B1) * g_
        nv = ADAM_B2 * v_ref[...] + (1.0 - ADAM_B2) * (g_ * g_)
        d_ref[...] = -ADAM_LR * ((nm / c1) / (jnp.sqrt(nv / c2) + ADAM_EPS) + ADAM_WD * w_ref[...])
        nm_ref[...] = nm
        nv_ref[...] = nv

    blk = pl.BlockSpec((tr, C), lambda i: (i, 0))
    shp = jax.ShapeDtypeStruct((R, C), F32)
    return pl.pallas_call(
        body, grid=(R // tr,), in_specs=[blk] * 4, out_specs=[blk] * 3, out_shape=[shp] * 3,
        compiler_params=_params("parallel"), name=name,
    )(w, g, m, v)


def _small_reduce(packs, logits, *, name):
    n, R, C = packs.shape

    def body(p_ref, lg_ref, s_ref, dlg_ref):
        acc = p_ref[0]
        for j in range(1, n):
            acc = acc + p_ref[j]
        s_ref[...] = acc
        lg = lg_ref[...]
        e = jnp.exp(lg - jnp.max(lg, axis=0, keepdims=True))
        sm = e / jnp.sum(e, axis=0, keepdims=True)
        dlb = acc[5:6, HG_WIDTH:2 * HG_WIDTH]
        first = lax.broadcasted_iota(jnp.int32, sm.shape, 0) == 0
        dlg_ref[...] = sm[0:1, :] * (jnp.where(first, 1.0, 0.0) - sm) * dlb

    whole = lambda shape: pl.BlockSpec(shape, lambda: (0,) * len(shape))
    return pl.pallas_call(
        body, in_specs=[whole((n, R, C)), whole(logits.shape)], out_specs=[whole((R, C)), whole(logits.shape)],
        out_shape=[jax.ShapeDtypeStruct((R, C), F32), jax.ShapeDtypeStruct(logits.shape, F32)],
        compiler_params=pltpu.CompilerParams(vmem_limit_bytes=VMEM_LIMIT), name=name,
    )(packs, logits)


HBM_SPEC = pl.BlockSpec(memory_space=pl.ANY)


def _place():
    return lax.axis_index("x"), lax.axis_index("y"), lax.axis_index("c")


class _Copies:
    def __init__(self, arrays, out_shapes, n_copies, make, finish):
        self.arrays, self.out_shapes, self.n_copies, self.make, self.finish = list(arrays), list(out_shapes), n_copies, make, finish

    def scratch(self):
        return [pltpu.SemaphoreType.DMA((self.n_copies,)), pltpu.SemaphoreType.DMA((self.n_copies,))]

    def run(self, name):
        n = len(self.arrays)

        def body(*refs):
            copies = self.make(refs[:n], refs[n:2 * n], *refs[2 * n:])
            for cp in copies:
                cp.start()
            for cp in copies:
                cp.wait()

        outs = pl.pallas_call(body, in_specs=[HBM_SPEC] * n, out_specs=[HBM_SPEC] * n, out_shape=self.out_shapes,
                              scratch_shapes=self.scratch(), name=name)(*self.arrays)
        return self.finish(outs)


def _remote(src, dst, send_sems, recv_sems, k, peer):
    return pltpu.make_async_remote_copy(src_ref=src, dst_ref=dst, send_sem=send_sems.at[k], recv_sem=recv_sems.at[k],
                                        device_id=peer, device_id_type=MESH)


def _same_core_peers(x, y, c):
    return [(1 - x, y, c), (x, 1 - y, c), (1 - x, 1 - y, c)]


def _all_peers(x, y, c):
    flip = lambda v, b: 1 - v if b else v
    return [(flip(x, r & 4), flip(y, r & 2), flip(c, r & 1)) for r in range(1, 8)]


def _gather(arrays, peers_of, slot_of, n_slots):
    n_peers = len(peers_of(0, 0, 0))

    def make(ins, outs, send_sems, recv_sems):
        x, y, c = _place()
        slot = slot_of(x, y, c)
        return [_remote(ins[a], outs[a].at[slot], send_sems, recv_sems, a * n_peers + k, peer)
                for a in range(len(arrays)) for k, peer in enumerate(peers_of(x, y, c))]

    def finish(outs):
        slot = slot_of(*_place())
        return [lax.dynamic_update_index_in_dim(o, a, slot, 0) for o, a in zip(outs, arrays)]

    shapes = [jax.ShapeDtypeStruct((n_slots,) + a.shape, a.dtype) for a in arrays]
    return _Copies(arrays, shapes, len(arrays) * n_peers, make, finish)


def _gather_chips(arrays):
    return _gather(arrays, _same_core_peers, lambda x, y, c: 2 * x + y, N_CHIPS)


def _gather_all(arrays):
    return _gather(arrays, _all_peers, lambda x, y, c: 4 * x + 2 * y + c, N_DEV)


def _pair_swap(a):
    def make(ins, outs, send_sems, recv_sems):
        x, y, c = _place()
        return [_remote(ins[0].at[1 - c], outs[0], send_sems, recv_sems, 0, (x, y, 1 - c))]

    return _Copies([a], [jax.ShapeDtypeStruct(a.shape[1:], a.dtype)], 1, make, lambda outs: outs[0])


def _chip_scatter(p):
    def make(ins, outs, send_sems, recv_sems):
        x, y, c = _place()
        return [_remote(ins[0].at[2 * px + py], outs[0].at[2 * x + y], send_sems, recv_sems, k, (px, py, pc))
                for k, (px, py, pc) in enumerate(_same_core_peers(x, y, c))]

    def finish(outs):
        x, y, _ = _place()
        me = 2 * x + y
        return lax.dynamic_update_index_in_dim(outs[0], lax.dynamic_index_in_dim(p, me, 0, keepdims=False), me, 0)

    return _Copies([p], [jax.ShapeDtypeStruct(p.shape, p.dtype)], 3, make, finish)


def _owner_scatter(blocks):
    def make(ins, outs, send_sems, recv_sems):
        x, y, c = _place()
        return [_remote(ins[0].at[2 * px + py, pc], outs[0].at[4 * x + 2 * y + c], send_sems, recv_sems, k, (px, py, pc))
                for k, (px, py, pc) in enumerate(_all_peers(x, y, c))]

    def finish(outs):
        x, y, c = _place()
        mine = lax.dynamic_index_in_dim(lax.dynamic_index_in_dim(blocks, 2 * x + y, 0, keepdims=False), c, 0, keepdims=False)
        return lax.dynamic_update_index_in_dim(outs[0], mine, 4 * x + 2 * y + c, 0)

    return _Copies([blocks], [jax.ShapeDtypeStruct((N_DEV,) + blocks.shape[2:], blocks.dtype)], N_DEV - 1, make, finish)


def _pair_gather(q):
    def make(ins, outs, send_sems, recv_sems):
        x, y, c = _place()
        return [_remote(ins[0], outs[0].at[c], send_sems, recv_sems, 0, (x, y, 1 - c))]

    return _Copies([q], [jax.ShapeDtypeStruct((2,) + q.shape, q.dtype)], 1, make,
                   lambda outs: lax.dynamic_update_index_in_dim(outs[0], q, _place()[2], 0))


def _grad_blocks(dw, kind):
    if kind == "cols2d":
        K, N = dw.shape
        b = dw.reshape(2, K // 2, N_CHIPS, N // N_CHIPS).transpose(2, 0, 1, 3)
    elif kind == "rows2d":
        b = dw.reshape(N_CHIPS, 2, dw.shape[0] // 8, dw.shape[1])
    elif kind == "cols3d":
        L, K, N = dw.shape
        b = dw.reshape(L, K, N_CHIPS, N // N_CHIPS).transpose(2, 0, 1, 3)
    else:
        L, K, N = dw.shape
        b = dw.reshape(L, N_CHIPS, K // N_CHIPS, N).transpose(1, 0, 2, 3)
    return b.reshape(N_CHIPS, 2, -1, D_MODEL)


def _pad_rows(a, rows):
    return jnp.concatenate([a, jnp.zeros((rows - a.shape[0],) + a.shape[1:], a.dtype)], axis=0)


def _forward_backward(x, target, W, late_weights=None, early_grads=None, last_grads=None):
    row = lambda a: a.reshape(1, -1)
    relu2 = lambda acc: (jnp.square(jnp.maximum(acc, 0.0)),)
    normed = lambda h, g: h * lax.rsqrt(jnp.mean(h * h, axis=-1, keepdims=True) + RMS_EPS) * g

    def residual_norm(acc, res, g):
        h = res + acc
        return h, normed(h, g)

    def norm_bwd(du, h_blk, dres_blk, g):
        dx, dg_terms = _rms_bwd_math(h_blk, g, du)
        dh = dres_blk + dx
        return dh, dh, jnp.sum(dg_terms, axis=0, keepdims=True), jnp.sum(dh, axis=0, keepdims=True)

    def matmul_norm_bwd(dy, w, h_in, g, dres, rider=None, *, tk=1024, name):
        return _matmul(dy, w, mode="nt", out_dtypes=[F32, MXU_DTYPE], epilogue=norm_bwd, tiles=[h_in, dres], rows=[row(g)],
                       n_sums=2, rider=rider, tk=tk, name=name)

    G = {}

    u0 = _rmsnorm_fwd(x, row(W["norm_mix_g"][0]), name="norm_mix0")
    proj, qkv = _matmul(u0, W["w_in"], mode="nn", out_dtypes=[F32, MXU_DTYPE], epilogue=lambda acc: (acc, acc), tn=896, name="in_proj")
    o_sb, got = _sb_fwd(qkv, late_weights and late_weights[0], name="sb_fwd")
    if late_weights:
        W = {**W, **late_weights[1](got)}
    hg_out, hg_o, hg_states = _hg_fwd(proj, W["hg_lb_logits"], row(W["hg_norm_g"]), name="hg_fwd")
    mix = jnp.concatenate([o_sb, hg_out], axis=-1)
    h1, u1 = _matmul(mix, W["w_out"], mode="nn", out_dtypes=[F32, MXU_DTYPE], epilogue=residual_norm, tiles=[x],
                     rows=[row(W["norm_ffn_g"][0])], name="out_proj")
    r0 = _matmul(u1, W["w_ff1"][0], mode="nn", out_dtypes=[MXU_DTYPE], epilogue=relu2, name="ff1_0")
    h2, u2 = _matmul(r0, W["w_ff2"][0], mode="nn", out_dtypes=[F32, MXU_DTYPE], epilogue=residual_norm, tiles=[h1],
                     rows=[row(W["norm_mix_g"][1])], name="ff2_0")
    p = _matmul(u2, W["w_glu"], mode="nn", out_dtypes=[F32], epilogue=lambda acc, b: (acc + b,), rows=[row(W["b_glu"])], name="glu_proj")
    w_dw = _pad_rows(W["w_dw"], CONV_HALO)
    ca, cy, cact = _conv_fwd(p, w_dw, row(W["b_dw"]), row(W["ln_g"]), row(W["ln_b"]), name="conv_fwd")
    h3, u3 = _matmul(cact, W["w_pw"], mode="nn", out_dtypes=[F32, MXU_DTYPE], epilogue=lambda acc, res, b, g: residual_norm(acc + b, res, g),
                     tiles=[h2], rows=[row(W["b_pw"]), row(W["norm_ffn_g"][1])], name="pw_proj")
    r1 = _matmul(u3, W["w_ff1"][1], mode="nn", out_dtypes=[MXU_DTYPE], epilogue=relu2, name="ff1_1")
    h4 = _matmul(r1, W["w_ff2"][1], mode="nn", out_dtypes=[F32], epilogue=lambda acc, res: (res + acc,), tiles=[h3], name="ff2_1")

    dh4, dh4_m, G["final_norm_g"], loss = _loss_head(h4, row(W["final_norm_g"]), target, name="loss_head")

    def mlp_bwd(dh, dh_m, h_in, u, r, layer, tag):
        d_relu2 = lambda acc, r_blk: (acc * (2.0 * jnp.sqrt(r_blk.astype(F32))),)
        da = _matmul(dh_m, W["w_ff2"][layer], mode="nt", out_dtypes=[MXU_DTYPE], epilogue=d_relu2, tiles=[r], name="d_ff2_act" + tag)
        dw2 = _matmul(r, dh_m, mode="tn", out_dtypes=[F32], tk=WGRAD_TOKENS, name="d_ff2_w" + tag)
        dw1 = _matmul(u, da, mode="tn", out_dtypes=[F32], tk=WGRAD_TOKENS, name="d_ff1_w" + tag)
        dh_in, dh_in_m, dg, cs = matmul_norm_bwd(da, W["w_ff1"][layer], h_in, W["norm_ffn_g"][layer], dh, name="d_ff1_act" + tag)
        return dh_in, dh_in_m, dg, cs, dw1, dw2

    dh3, dh3_m, dg_ffn1, cs_h3, dw1_1, dw2_1 = mlp_bwd(dh4, dh4_m, h3, u3, r1, 1, "1")
    G["b_pw"] = cs_h3
    dact = _matmul(dh3_m, W["w_pw"], mode="nt", out_dtypes=[F32], name="d_pw_act")
    G["w_pw"] = _matmul(cact, dh3_m, mode="tn", out_dtypes=[F32], tk=WGRAD_TOKENS, name="d_pw_w")
    dy, G["ln_g"], G["ln_b"], G["b_dw"] = _conv_bwd_norm(dact, cy, row(W["ln_g"]), row(W["ln_b"]), name="d_conv_norm")
    dp, G["w_dw"], G["b_glu"] = _conv_bwd_taps(dy, ca, p, w_dw, name="d_conv_taps")
    G["w_glu"] = _matmul(u2, dp, mode="tn", out_dtypes=[F32], tk=WGRAD_TOKENS, name="d_glu_w")
    dh2, dh2_m, dg_mix1, _ = matmul_norm_bwd(dp, W["w_glu"], h2, W["norm_mix_g"][1], dh3, name="d_glu_act")
    dh1, dh1_m, dg_ffn0, _, dw1_0, dw2_0 = mlp_bwd(dh2, dh2_m, h1, u1, r0, 0, "0")
    G["w_ff1"], G["w_ff2"] = jnp.stack([dw1_0, dw1_1]), jnp.stack([dw2_0, dw2_1])
    G["norm_ffn_g"] = jnp.concatenate([dg_ffn0, dg_ffn1], axis=0)
    dmix = _matmul(dh1_m, W["w_out"], mode="nt", out_dtypes=[F32], name="d_out_act")
    G["w_out"] = _matmul(mix, dh1_m, mode="tn", out_dtypes=[F32], tk=WGRAD_TOKENS, name="d_out_w")
    riding = early_grads(G) if early_grads else None
    (dsq, dsk, dsv), got = _sb_bwd(qkv, dmix, riding, name="sb_bwd")
    d_hg, G["hg_lb"], G["hg_norm_g"] = _hg_bwd(proj, hg_o, hg_states, dmix, W["hg_lb_logits"], row(W["hg_norm_g"]), name="hg_bwd")
    dproj = jnp.concatenate([dsq, _mx(dsk), _mx(dsv), d_hg], axis=-1)
    G["w_in"] = _matmul(u0, dproj, mode="tn", out_dtypes=[F32], tn=896, tk=WGRAD_TOKENS, name="d_in_w")
    last = last_grads(G) if last_grads else None
    res = matmul_norm_bwd(dproj, W["w_in"], x, W["norm_mix_g"][0], dh1, last, tk=896, name="d_in_act")
    (dx, _, dg_mix0, _), got_last = res if last_grads else (res, [])
    G["norm_mix_g"] = jnp.concatenate([dg_mix0, dg_mix1], axis=0)
    return loss, dx, G, [(riding, got), (last, got_last)]


BIG = (("w_out_ab", "w_out", "rows2d"), ("conv_w_glu", "w_glu", "cols2d"), ("conv_w_pw", "w_pw", "rows2d"),
       ("w_ff1", "w_ff1", "cols3d"), ("w_ff2", "w_ff2", "rows3d"), ("w_in_ab", "w_in", "cols2d"))
LATE = BIG[:-1]
SMALL_SHARDED = ("conv_b_glu", "conv_w_dw", "conv_b_dw", "conv_ln_g", "conv_ln_b", "conv_b_pw")
REPLICATED = ("norm_mix_g", "norm_ffn_g", "hg_lb_logits", "hg_norm_g", "final_norm_g")
ORDER = ("norm_mix_g", "norm_ffn_g", "w_in_ab", "w_out_ab", "hg_lb_logits", "hg_norm_g", "conv_w_glu", "conv_b_glu",
         "conv_w_dw", "conv_b_dw", "conv_ln_g", "conv_ln_b", "conv_w_pw", "conv_b_pw", "w_ff1", "w_ff2", "final_norm_g")


def _step(x, loss_target, w, m, v):
    D = D_MODEL
    x2, t2 = x.reshape(-1, D), loss_target.reshape(-1, D)
    chip = 2 * lax.axis_index("x") + lax.axis_index("y")
    c = lax.axis_index("c")

    small_in = jnp.concatenate([w["conv_b_glu"].reshape(2, SHARD), w["conv_w_dw"].reshape(CONV_WIDTH, SHARD)] +
                               [w[n].reshape(1, SHARD) for n in ("conv_b_dw", "conv_ln_g", "conv_ln_b", "conv_b_pw")], axis=0)
    g_in, gs = _gather_chips([w["w_in_ab"].astype(MXU_DTYPE), _pad_rows(small_in, SMALL_IN_ROWS)]).run("gather_first_weights")
    vec = lambda r0, r1: gs[:, r0:r1].transpose(1, 0, 2).reshape(r1 - r0, D)
    W = {
        "w_in": _ChipWeight(g_in[:, 0], "cols"),
        "b_glu": gs[:, 0:2].reshape(2 * D), "w_dw": vec(2, 33), "b_dw": vec(33, 34)[0], "ln_g": vec(34, 35)[0],
        "ln_b": vec(35, 36)[0], "b_pw": vec(36, 37)[0],
        "norm_mix_g": w["norm_mix_g"], "norm_ffn_g": w["norm_ffn_g"], "hg_lb_logits": w["hg_lb_logits"],
        "hg_norm_g": w["hg_norm_g"], "final_norm_g": w["final_norm_g"],
    }
    late = _gather_chips([w[n].astype(MXU_DTYPE) for n, _, _ in LATE])

    def assemble(got):
        gw = dict(zip([s for _, s, _ in LATE], late.finish(got)))
        layers = lambda g, along: [_ChipWeight(g, along, (layer,)) for layer in range(2)]
        return {"w_out": gw["w_out"].reshape(D, D), "w_glu": _ChipWeight(gw["w_glu"][:, 0], "cols"), "w_pw": gw["w_pw"].reshape(D, D),
                "w_ff1": layers(gw["w_ff1"], "cols"), "w_ff2": layers(gw["w_ff2"], "rows")}

    def early_grads(G):
        return _owner_scatter(jnp.concatenate([_grad_blocks(G[s], kind) for _, s, kind in LATE], axis=2))

    def last_grads(G):
        blocks = _grad_blocks(G["w_in"], "cols2d")
        from_pair = _pair_swap(blocks.transpose(1, 0, 2, 3)).run("grads_pair_swap_in")
        return _chip_scatter(_add_pair(lax.dynamic_index_in_dim(blocks, c, axis=1, keepdims=False), from_pair, name="grads_pair_add_in"))

    loss, dx, G, riders = _forward_backward(x2, t2, W, (late, assemble), early_grads, last_grads)
    halves = [_sum_leading(copies.finish(got), name="grads_add_" + tag) for (copies, got), tag in zip(riders, ("late", "in"))]
    half = jnp.concatenate(halves, axis=0)
    full = _pair_gather(half).run("grads_pair_gather")

    pack = jnp.concatenate([
        G["norm_mix_g"], G["norm_ffn_g"], G["final_norm_g"], jnp.concatenate([G["hg_norm_g"], G["hg_lb"]], axis=1),
        _pad_rows(jnp.broadcast_to(loss, (1, D)), 2), G["b_glu"].reshape(2, D), G["w_dw"], G["b_dw"], G["ln_g"], G["ln_b"], G["b_pw"],
    ], axis=0)
    pack = _pad_rows(pack, SMALL_ROWS)
    (packs,) = _gather_all([pack]).run("gather_small_grads")
    ssum, d_logits = _small_reduce(packs, w["hg_lb_logits"], name="reduce_small_grads")
    cut = lambda r0, r1: lax.dynamic_slice(ssum, (r0, chip * SHARD), (r1 - r0, SHARD))
    grads = {
        "norm_mix_g": ssum[0:2], "norm_ffn_g": ssum[2:4], "final_norm_g": ssum[4], "hg_norm_g": ssum[5, :HG_WIDTH].reshape(1, HG_HEADS, HG_DH),
        "hg_lb_logits": d_logits,
        "conv_b_glu": lax.dynamic_slice(ssum[8:10].reshape(1, 2 * D), (0, chip * 2 * SHARD), (1, 2 * SHARD)),
        "conv_w_dw": cut(10, 10 + CONV_WIDTH).reshape(1, CONV_WIDTH, SHARD),
        "conv_b_dw": cut(42, 43), "conv_ln_g": cut(43, 44), "conv_ln_b": cut(44, 45), "conv_b_pw": cut(45, 46),
    }
    loss_out = ssum[6, 0]

    off = 0
    for n, s, kind in BIG:
        shard = w[n].shape
        rows = w[n].size // (2 * D)
        grads[n] = full[:, off:off + rows].reshape(shard)
        off += rows

    delta, new_m, new_v = {}, {}, {}
    for n, _, _ in BIG:
        view = lambda a: a.reshape(-1, a.shape[-1])
        outs = _adamw(view(w[n]), view(grads[n]), view(m[n]), view(v[n]), name="adamw_" + n)
        delta[n], new_m[n], new_v[n] = (o.reshape(w[n].shape) for o in outs)
    small = SMALL_SHARDED + REPLICATED
    sizes = [w[n].size for n in small]
    total = sum(sizes)
    rows = -(-total // (8 * D)) * 8
    packed = lambda d: _pad_rows(jnp.concatenate([d[n].reshape(-1) for n in small]).reshape(-1, 128), rows * 8).reshape(rows, D)
    outs = _adamw(packed(w), packed(grads), packed(m), packed(v), name="adamw_small")
    off = 0
    for n, size in zip(small, sizes):
        delta[n], new_m[n], new_v[n] = (o.reshape(-1)[off:off + size].reshape(w[n].shape) for o in outs)
        off += size
    grads = {n: grads[n].reshape(w[n].shape) for n in ORDER}
    return (loss_out, dx.reshape(x.shape), *[grads[n] for n in ORDER], *[delta[n] for n in ORDER],
            *[new_m[n] for n in ORDER], *[new_v[n] for n in ORDER])


def kernel(x, norm_mix_g, norm_ffn_g, w_in_ab, w_out_ab, hg_lb_logits, hg_norm_g, conv_w_glu, conv_b_glu, conv_w_dw, conv_b_dw, conv_ln_g, conv_ln_b, conv_w_pw, conv_b_pw, w_ff1, w_ff2, final_norm_g, loss_target, m_norm_mix_g, m_norm_ffn_g, m_w_in_ab, m_w_out_ab, m_hg_lb_logits, m_hg_norm_g, m_conv_w_glu, m_conv_b_glu, m_conv_w_dw, m_conv_b_dw, m_conv_ln_g, m_conv_ln_b, m_conv_w_pw, m_conv_b_pw, m_w_ff1, m_w_ff2, m_final_norm_g, v_norm_mix_g, v_norm_ffn_g, v_w_in_ab, v_w_out_ab, v_hg_lb_logits, v_hg_norm_g, v_conv_w_glu, v_conv_b_glu, v_conv_w_dw, v_conv_b_dw, v_conv_ln_g, v_conv_ln_b, v_conv_w_pw, v_conv_b_pw, v_w_ff1, v_w_ff2, v_final_norm_g):
    args = locals()
    w = {n: args[n] for n in ORDER}
    m = {n: args["m_" + n] for n in ORDER}
    v = {n: args["v_" + n] for n in ORDER}
    return _step(x, loss_target, w, m, v)
```

```python
import functools

import jax
import jax.numpy as jnp
from jax import lax
from jax.experimental import pallas as pl
from jax.experimental.pallas import tpu as pltpu

F32 = jnp.float32
MXU_DTYPE = jnp.bfloat16
MESH = pl.DeviceIdType.MESH

D_MODEL = 1024
SB_HEADS, SB_DH, SB_WIDTH = 8, 64, 512
SB_KEYS = 512
SB_SUB = 256
SB_ROWS_FWD, SB_ROWS_BWD = 512, 256
HG_HEADS, HG_DH, HG_WIDTH = 4, 128, 512
HG_CHUNK = 16
HG_TOKENS = 256
CONV_WIDTH = 31
CONV_HALO = 32
CONV_ROWS = 32
RMS_EPS = 1e-6
LN_EPS = 1e-5
N_CHIPS = 4
N_DEV = 8
SHARD = D_MODEL // N_CHIPS
SMALL_IN_ROWS = 40
SMALL_ROWS = 48
WGRAD_TOKENS = 2048
VMEM_LIMIT = 56 * 1024 * 1024

ADAM_LR, ADAM_B1, ADAM_B2, ADAM_EPS, ADAM_WD, ADAM_STEP = 0.001, 0.9, 0.999, 1e-08, 0.01, 10


def _params(*sem):
    return pltpu.CompilerParams(dimension_semantics=sem, vmem_limit_bytes=VMEM_LIMIT)


def _mx(v):
    return v.astype(MXU_DTYPE)


def _dot(a, b):
    return jnp.dot(_mx(a), _mx(b), preferred_element_type=F32)


def _dot_nt(a, b):
    return lax.dot_general(_mx(a), _mx(b), (((1,), (1,)), ((), ())), preferred_element_type=F32)


def _dot_tn(a, b):
    return lax.dot_general(_mx(a), _mx(b), (((0,), (0,)), ((), ())), preferred_element_type=F32)


def _neg_abs(x):
    bits = lax.bitcast_convert_type(x, jnp.uint32) | jnp.uint32(0x80000000)
    return lax.bitcast_convert_type(bits, F32)


def _key_order_sums(v, tri2, later):
    hi = _mx(v)
    lo = _mx(v - hi.astype(F32))
    n = v.shape[1] // SB_SUB
    blocks = [slice(b * SB_SUB, (b + 1) * SB_SUB) for b in range(n)]
    totals = [jnp.sum(v[:, sl], axis=1, keepdims=True) for sl in blocks]
    far, running = [None] * n, None
    for b in (reversed(range(n)) if later else range(n)):
        far[b] = running
        running = totals[b] if running is None else running + totals[b]
    sums = []
    for b, sl in enumerate(blocks):
        inside = jnp.dot(jnp.concatenate([hi[:, sl], lo[:, sl]], axis=1), tri2, preferred_element_type=F32)
        sums.append(inside if far[b] is None else inside + far[b])
    return jnp.concatenate(sums, axis=1), running


class _ChipWeight:
    def __init__(self, parts, along, lead=()):
        self.parts, self.along, self.lead = parts, along, tuple(lead)
        r, c = parts.shape[-2:]
        self.shape = (r, N_CHIPS * c) if along == "cols" else (N_CHIPS * r, c)

    def _gathered_is_n(self, mode):
        return (self.along == "cols") == (mode in ("nn", "tn"))

    def tile(self, mode, tn, tk):
        r, c = self.parts.shape[-2:]
        part = c if self.along == "cols" else r
        return (part, tk) if self._gathered_is_n(mode) else (tn, part)

    def spec(self, mode, tn, tk):
        squeezed = (None,) * (1 + len(self.lead))
        lead, cols, by_n = self.lead, self.along == "cols", self._gathered_is_n(mode)
        block = (tn, tk) if mode == "nt" else (tk, tn)

        def index(i, j, k):
            chip, other = (j, k) if by_n else (k, j)
            return (chip,) + lead + ((other, 0) if cols else (0, other))

        return pl.BlockSpec(squeezed + block, index)


def _matmul(a, b, *, mode, out_dtypes, epilogue=None, tiles=(), rows=(), n_sums=0, rider=None, tm=1024, tn=1024, tk=1024, name):
    b_shape = b.shape
    if mode == "nn":
        (M, K), N = a.shape, b_shape[1]
    elif mode == "nt":
        (M, K), N = a.shape, b_shape[0]
    else:
        (K, M), N = a.shape, b_shape[1]
    if isinstance(b, _ChipWeight):
        tn, tk = b.tile(mode, tn, tk)
    tm, tn, tk = min(tm, M), min(tn, N), min(tk, K)
    assert M % tm == 0 and N % tn == 0 and K % tk == 0, (name, M, N, K)
    nk = K // tk
    a_spec = pl.BlockSpec((tk, tm), lambda i, j, k: (k, i)) if mode == "tn" else pl.BlockSpec((tm, tk), lambda i, j, k: (i, k))
    if isinstance(b, _ChipWeight):
        b_spec, b = b.spec(mode, tn, tk), b.parts
    else:
        b_spec = pl.BlockSpec((tn, tk), lambda i, j, k: (j, k)) if mode == "nt" else pl.BlockSpec((tk, tn), lambda i, j, k: (k, j))
    dims = {"nn": ((1,), (0,)), "nt": ((1,), (1,)), "tn": ((0,), (0,))}[mode]
    n_t, n_r, n_o = len(tiles), len(rows), len(out_dtypes)
    n_x = 0 if rider is None else len(rider.arrays)
    grid = (M // tm, N // tn, nk)
    if epilogue is None:
        epilogue = lambda acc: (acc,)

    def body(a_ref, b_ref, *rest):
        extra, x_in, rest = rest[:n_t + n_r], rest[n_t + n_r:n_t + n_r + n_x], rest[n_t + n_r + n_x:]
        outs, sums, x_out, acc_ref, sems = rest[:n_o], rest[n_o:n_o + n_sums], rest[n_o + n_sums:n_o + n_sums + n_x], rest[n_o + n_sums + n_x], rest[n_o + n_sums + n_x + 1:]
        i, j, k = (pl.program_id(d) for d in range(3))
        if rider is not None:
            @pl.when((i == 0) & (j == 0) & (k == 0))
            def _():
                for cp in rider.make(x_in, x_out, *sems):
                    cp.start()

        @pl.when(k == 0)
        def _():
            acc_ref[...] = jnp.zeros_like(acc_ref)

        acc_ref[...] += lax.dot_general(_mx(a_ref[...]), _mx(b_ref[...]), (dims, ((), ())), preferred_element_type=F32)

        @pl.when(k == nk - 1)
        def _():
            res = epilogue(acc_ref[...], *[e[...] for e in extra])
            for o_ref, r in zip(outs, res[:n_o]):
                o_ref[...] = r.astype(o_ref.dtype)
            for s_ref, r in zip(sums, res[n_o:]):
                @pl.when(i == 0)
                def _():
                    s_ref[...] = jnp.zeros_like(s_ref)

                s_ref[...] += r

        if rider is not None:
            @pl.when((i == grid[0] - 1) & (j == grid[1] - 1) & (k == grid[2] - 1))
            def _():
                for cp in rider.make(x_in, x_out, *sems):
                    cp.wait()

    tile_spec = pl.BlockSpec((tm, tn), lambda i, j, k: (i, j))
    row_spec = pl.BlockSpec((1, tn), lambda i, j, k: (0, j))
    ordered = n_sums > 0 or rider is not None
    outs = pl.pallas_call(
        body, grid=grid,
        in_specs=[a_spec, b_spec] + [tile_spec] * n_t + [row_spec] * n_r + [HBM_SPEC] * n_x,
        out_specs=[tile_spec] * n_o + [row_spec] * n_sums + [HBM_SPEC] * n_x,
        out_shape=[jax.ShapeDtypeStruct((M, N), dt) for dt in out_dtypes] + [jax.ShapeDtypeStruct((1, N), F32)] * n_sums
        + ([] if rider is None else rider.out_shapes),
        scratch_shapes=[pltpu.VMEM((tm, tn), F32)] + ([] if rider is None else rider.scratch()),
        compiler_params=_params(*(("arbitrary",) * 3 if ordered else ("parallel", "parallel", "arbitrary"))), name=name,
    )(a, b, *tiles, *rows, *([] if rider is None else rider.arrays))
    res = outs[0] if n_o + n_sums == 1 else outs[:n_o + n_sums]
    return res if rider is None else (res, outs[n_o + n_sums:])


def _token_block(T):
    return min(512, T)


def _rmsnorm_fwd(h, g, *, name):
    T, D = h.shape
    tb = _token_block(T)

    def body(h_ref, g_ref, u_ref):
        x = h_ref[...]
        r = lax.rsqrt(jnp.mean(x * x, axis=-1, keepdims=True) + RMS_EPS)
        u_ref[...] = (x * r * g_ref[...]).astype(u_ref.dtype)

    blk = pl.BlockSpec((tb, D), lambda i: (i, 0))
    return pl.pallas_call(
        body, grid=(T // tb,), in_specs=[blk, pl.BlockSpec((1, D), lambda i: (0, 0))], out_specs=blk,
        out_shape=jax.ShapeDtypeStruct((T, D), MXU_DTYPE), compiler_params=_params("parallel"), name=name,
    )(h, g)


def _rms_bwd_math(x, g, du):
    r = lax.rsqrt(jnp.mean(x * x, axis=-1, keepdims=True) + RMS_EPS)
    gd = g * du
    dx = r * gd - x * (r * r * r) * jnp.mean(gd * x, axis=-1, keepdims=True)
    return dx, du * x * r


def _loss_head(h, g, target, *, name):
    T, D = h.shape
    tb = _token_block(T)

    def body(h_ref, g_ref, t_ref, dh_ref, dhm_ref, dg_ref, loss_ref):
        @pl.when(pl.program_id(0) == 0)
        def _():
            dg_ref[...] = jnp.zeros_like(dg_ref)
            loss_ref[...] = jnp.zeros_like(loss_ref)

        x, gg = h_ref[...], g_ref[...]
        r = lax.rsqrt(jnp.mean(x * x, axis=-1, keepdims=True) + RMS_EPS)
        diff = x * r * gg - t_ref[...]
        per_token = jnp.mean(diff * diff, axis=-1, keepdims=True)
        loss_ref[...] += 0.5 * jnp.sum(per_token, axis=0, keepdims=True)
        dx, dg_terms = _rms_bwd_math(x, gg, diff / D)
        dh_ref[...] = dx
        dhm_ref[...] = dx.astype(dhm_ref.dtype)
        dg_ref[...] += jnp.sum(dg_terms, axis=0, keepdims=True)

    blk = pl.BlockSpec((tb, D), lambda i: (i, 0))
    row = pl.BlockSpec((1, D), lambda i: (0, 0))
    return pl.pallas_call(
        body, grid=(T // tb,), in_specs=[blk, row, blk], out_specs=[blk, blk, row, pl.BlockSpec((1, 1), lambda i: (0, 0))],
        out_shape=[jax.ShapeDtypeStruct((T, D), F32), jax.ShapeDtypeStruct((T, D), MXU_DTYPE), jax.ShapeDtypeStruct((1, D), F32),
                   jax.ShapeDtypeStruct((1, 1), F32)],
        compiler_params=_params("arbitrary"), name=name,
    )(h, g, target)


def _sb_scores(qm, ks, later, tri, mask, need_log_beta=True):
    z = _dot_nt(qm, ks)
    sp = jnp.maximum(z, 0.0) + jnp.log(1.0 + jnp.exp(_neg_abs(z)))
    lb = z - sp if need_log_beta else None
    if mask is not None:
        sp = jnp.where(mask, sp, 0.0)
    after, total = _key_order_sums(sp, tri, later=True)
    w = jnp.exp((lb if need_log_beta else z) - (after + later))
    if mask is not None:
        w = jnp.where(mask, w, 0.0)
    return total, lb, w


def _sb_setup(q_ref, rows, tri_ref, inclusive=False):
    i, hsel = pl.program_id(1), pl.program_id(2)
    lane = lax.broadcasted_iota(jnp.int32, (rows, 2 * SB_DH), 1)
    mine = (lane >= SB_DH) == (hsel == 1)
    diag = (i * rows) // SB_KEYS
    t = i * rows + lax.broadcasted_iota(jnp.int32, (rows, SB_KEYS), 0)
    s = diag * SB_KEYS + lax.broadcasted_iota(jnp.int32, (rows, SB_KEYS), 1)
    @pl.when((pl.program_id(0) == 0) & (i == 0) & (hsel == 0))
    def _():
        a = lax.broadcasted_iota(jnp.int32, (2 * SB_SUB, SB_SUB), 0) % SB_SUB
        b = lax.broadcasted_iota(jnp.int32, (2 * SB_SUB, SB_SUB), 1)
        tri_ref[0] = (a >= b if inclusive else a > b).astype(tri_ref.dtype)
        tri_ref[1] = (a < b).astype(tri_ref.dtype)

    return i, hsel, mine, diag, s < t, tri_ref[0], tri_ref[1]


_SB_TRI = pltpu.VMEM((2, 2 * SB_SUB, SB_SUB), MXU_DTYPE)


def _sb_keys(j, n=1, keys=None):
    return pl.ds(pl.multiple_of(j * SB_KEYS, SB_KEYS), n * SB_KEYS if keys is None else keys)


def _sb_descend(n, step, carry, wide):
    pair = (lambda j, cr: step(j, 2, cr)) if wide else (lambda j, cr: step(j, 1, step(j + 1, 1, cr)))
    carry = lax.fori_loop(0, n // 2, lambda it, cr: pair(n - 2 - 2 * it, cr), carry)
    return lax.cond(n % 2 == 1, lambda cr: step(0, 1, cr), lambda cr: cr, carry)


def _sb_ascend(n, step, carry):
    odd = n % 2
    carry = lax.cond(odd == 1, lambda cr: step(0, 1, cr), lambda cr: cr, carry)
    return lax.fori_loop(0, n // 2, lambda it, cr: step(odd + 2 * it, 2, cr), carry)


def _sb_call(body, qkv, extra_in, out_blocks, out_dtypes, scratch, rider, rows, *, name):
    T = qkv.shape[0]
    n_pairs = SB_HEADS // 2
    grid = (n_pairs, T // rows, 2)
    pair = lambda col0: pl.BlockSpec((rows, 2 * SB_DH), lambda p, i, h: (i, col0 + p))
    whole = lambda col0: pl.BlockSpec((T, 2 * SB_DH), lambda p, i, h: (0, col0 + p))
    in_specs = [pair(0), whole(n_pairs), whole(2 * n_pairs)] + [pair(0)] * len(extra_in)
    out_specs = [pair(0) if kind == "pair" else whole(0) for kind in out_blocks]
    n_in, n_out, n_r = len(in_specs), len(out_specs), 0 if rider is None else len(rider.arrays)

    def kernel_body(*refs):
        ins, r_in = refs[:n_in], refs[n_in:n_in + n_r]
        outs, r_out = refs[n_in + n_r:n_in + n_r + n_out], refs[n_in + n_r + n_out:n_in + 2 * n_r + n_out]
        rest = refs[n_in + 2 * n_r + n_out:]
        ids = [pl.program_id(a) for a in range(3)]
        if rider is not None:
            @pl.when((ids[0] == 0) & (ids[1] == 0) & (ids[2] == 0))
            def _():
                for cp in rider.make(r_in, r_out, *rest[len(scratch):]):
                    cp.start()

        body(ins, outs, rest[:len(scratch)])
        if rider is not None:
            @pl.when((ids[0] == grid[0] - 1) & (ids[1] == grid[1] - 1) & (ids[2] == grid[2] - 1))
            def _():
                for cp in rider.make(r_in, r_out, *rest[len(scratch):]):
                    cp.wait()

    res = pl.pallas_call(
        kernel_body, grid=grid, in_specs=in_specs + [HBM_SPEC] * n_r, out_specs=out_specs + [HBM_SPEC] * n_r,
        out_shape=[jax.ShapeDtypeStruct((T, SB_WIDTH), dt) for dt in out_dtypes] + ([] if rider is None else rider.out_shapes),
        scratch_shapes=list(scratch) + ([] if rider is None else rider.scratch()),
        compiler_params=_params("arbitrary", "arbitrary", "arbitrary"), name=name,
    )(qkv, qkv, qkv, *extra_in, *([] if rider is None else rider.arrays))
    return res[:n_out], res[n_out:]


def _sb_fwd(qkv, rider=None, *, name):
    rows = min(SB_ROWS_FWD, qkv.shape[0])
    scale = SB_DH ** -0.5

    def body(ins, outs, scratch):
        (q_ref, k_ref, v_ref), (o_ref,), (tri_ref,) = ins, outs, scratch
        i, hsel, mine, diag, mask, tri, _ = _sb_setup(q_ref, rows, tri_ref, inclusive=True)
        qm = jnp.where(mine, q_ref[...], 0) * scale

        def tile(j, n, m, later, acc):
            total, _, w = _sb_scores(qm, k_ref[_sb_keys(j, n), :], later, tri, m, need_log_beta=False)
            return later + total, acc + _dot(w, v_ref[_sb_keys(j, n), :])

        carry = tile(diag, 1, mask, jnp.zeros((rows, 1), F32), jnp.zeros((rows, 2 * SB_DH), F32))
        _, acc = _sb_descend(diag, lambda j, n, cr: tile(j, n, None, *cr), carry, wide=False)
        res = jnp.where(mine, acc, 0.0).astype(o_ref.dtype)

        @pl.when(hsel == 0)
        def _():
            o_ref[...] = res

        @pl.when(hsel == 1)
        def _():
            o_ref[...] += res

    (o,), got = _sb_call(body, qkv, [], ["pair"], [MXU_DTYPE], [_SB_TRI], rider, rows, name=name)
    return o, got


def _sb_bwd(qkv, dmix, rider=None, *, name):
    T = qkv.shape[0]
    rows = min(SB_ROWS_BWD, T)
    scale = SB_DH ** -0.5

    def body(ins, outs, scratch):
        (q_ref, k_ref, v_ref, do_ref), (dq_ref, dk_ref, dv_ref), (da_ref, beta_ref, tri_ref) = ins, outs, scratch
        i, hsel, mine, diag, mask, tri, tri_before = _sb_setup(q_ref, rows, tri_ref)

        @pl.when((i == 0) & (hsel == 0))
        def _():
            dk_ref[...] = jnp.zeros_like(dk_ref)
            dv_ref[...] = jnp.zeros_like(dv_ref)

        qm = jnp.where(mine, q_ref[...], 0) * scale
        do_m = _mx(jnp.where(mine, do_ref[...], 0.0))

        def weights(j, n, m, later, keys=None):
            sl = _sb_keys(j, n, keys)
            total, lb, w = _sb_scores(qm, k_ref[sl, :], later, tri, m)
            da, beta = _dot_nt(do_m, v_ref[sl, :]) * w, jnp.exp(lb)
            if keys is None:
                for t in range(n):
                    da_ref[j + t] = da[:, t * SB_KEYS:(t + 1) * SB_KEYS]
                    beta_ref[j + t] = beta[:, t * SB_KEYS:(t + 1) * SB_KEYS]
            else:
                da_ref[j, :, 0:keys] = da
                beta_ref[j, :, 0:keys] = beta
            dv_ref[sl, :] += _dot_tn(w, do_m)
            return later + total

        short = (i * rows) % SB_KEYS + rows <= SB_SUB
        short_mask = mask[:, 0:SB_SUB]
        later = lax.cond(short, lambda z: weights(diag, 1, short_mask, z, SB_SUB), lambda z: weights(diag, 1, mask, z),
                         jnp.zeros((rows, 1), F32))
        _sb_descend(diag, lambda j, n, c: weights(j, n, None, c), later, wide=True)

        def logits(j, n, m, before, dq, keys=None):
            if keys is None:
                da = jnp.concatenate([da_ref[j + t] for t in range(n)], axis=1)
                beta = jnp.concatenate([beta_ref[j + t] for t in range(n)], axis=1)
            else:
                da, beta = da_ref[j, :, 0:keys], beta_ref[j, :, 0:keys]
            earlier, total = _key_order_sums(da, tri_before, later=False)
            dz = da - beta * (da + earlier + before)
            if m is not None:
                dz = jnp.where(m, dz, 0.0)
            dz = _mx(dz)
            dk_ref[_sb_keys(j, n, keys), :] += _dot_tn(dz, qm)
            return before + total, dq + _dot(dz, k_ref[_sb_keys(j, n, keys), :])

        carry = (jnp.zeros((rows, 1), F32), jnp.zeros((rows, 2 * SB_DH), F32))
        carry = _sb_ascend(diag, lambda j, n, cr: logits(j, n, None, *cr), carry)
        dq = lax.cond(short, lambda cr: logits(diag, 1, short_mask, *cr, keys=SB_SUB)[1], lambda cr: logits(diag, 1, mask, *cr)[1], carry)
        res = jnp.where(mine, dq * scale, 0.0).astype(dq_ref.dtype)

        @pl.when(hsel == 0)
        def _():
            dq_ref[...] = res

        @pl.when(hsel == 1)
        def _():
            dq_ref[...] += res

    n_tiles = T // SB_KEYS
    scratch = [pltpu.VMEM((n_tiles, rows, SB_KEYS), F32), pltpu.VMEM((n_tiles, rows, SB_KEYS), F32), _SB_TRI]
    return _sb_call(body, qkv, [dmix], ["pair", "whole", "whole"], [MXU_DTYPE, F32, F32], scratch, rider, rows, name=name)


def _chunk_row(n):
    return lax.broadcasted_iota(jnp.int32, (n, HG_DH), 0) % HG_CHUNK


def _chunk_cumsum(x, row, reverse=False):
    n = x.shape[0]
    for sh in (1, 2, 4, 8):
        if reverse:
            x = x + jnp.where(row < HG_CHUNK - sh, pltpu.roll(x, n - sh, 0), 0.0)
        else:
            x = x + jnp.where(row >= sh, pltpu.roll(x, sh, 0), 0.0)
    return x


def _hg_lower_bound(logits_ref):
    lg = logits_ref[...]
    e = jnp.exp(lg - jnp.max(lg, axis=0, keepdims=True))
    return e[0:1, :] / jnp.sum(e, axis=0, keepdims=True)


def _hg_terms(fr, q, lb, row):
    sig = jax.nn.sigmoid(fr)
    f = lb + (1.0 - lb) * sig
    kk = 1.0 - f
    g = jnp.log(f)
    G = _chunk_cumsum(g, row)
    g_last = G + (_chunk_cumsum(g, row, reverse=True) - g)
    e_g, e_ng, e_lg = jnp.exp(G), jnp.exp(-G), jnp.exp(g_last - G)
    return dict(sig=sig, f=f, kk=kk, e_g=e_g, e_ng=e_ng, e_lg=e_lg, q_dec=q * e_g, k_intra=kk * e_ng,
                k_state=kk * e_lg, decay=jnp.exp(g_last))


def _hg_causal(n):
    t = lax.broadcasted_iota(jnp.int32, (n, n), 0)
    s = lax.broadcasted_iota(jnp.int32, (n, n), 1)
    return (s <= t) & (s // HG_CHUNK == t // HG_CHUNK)


def _chunks(a):
    return a.reshape(a.shape[0] // HG_CHUNK, HG_CHUNK, a.shape[1])


def _per_chunk(lhs, rhs, contract):
    return lax.dot_general(_mx(lhs), _mx(rhs), ((contract[:1], contract[1:]), ((0,), (0,))), preferred_element_type=F32)


def _hg_specs(T, tb, col0, order):
    return [pl.BlockSpec((tb, HG_WIDTH), functools.partial(lambda i, j: (order(i), j), j=col0 + j)) for j in range(4)]


def _hg_fwd(proj, logits, norm_g, *, name):
    T = proj.shape[0]
    tb = min(HG_TOKENS, T)
    nch = tb // HG_CHUNK

    def body(q_ref, f_ref, i_ref, gate_ref, lg_ref, ng_ref, out_ref, o_ref, s_ref, st_ref, inc_ref, dec_ref):
        @pl.when(pl.program_id(0) == 0)
        def _():
            st_ref[...] = jnp.zeros_like(st_ref)

        lb_all = _hg_lower_bound(lg_ref)
        row = _chunk_row(tb)
        causal = _hg_causal(tb)
        for hh in range(HG_HEADS):
            cols = slice(hh * HG_DH, (hh + 1) * HG_DH)
            t = _hg_terms(f_ref[:, cols], q_ref[:, cols], lb_all[:, cols], row)
            v = i_ref[:, cols]
            scores = jnp.where(causal, _dot_nt(t["q_dec"], t["k_intra"]), 0.0)
            o_intra = _dot(scores, v)
            inc_ref[...] = _per_chunk(_chunks(v), _chunks(t["k_state"]), (1, 1))
            dec_ref[...] = _chunks(t["decay"])

            def step(ci, st):
                s_ref[ci, hh] = st
                return st * dec_ref[ci][0:1, :] + inc_ref[ci]

            st_ref[hh] = lax.fori_loop(0, nch, step, st_ref[hh], unroll=4)
            o_inter = _per_chunk(_chunks(t["q_dec"]), s_ref[:, hh], (2, 2))
            o = o_intra + o_inter.reshape(tb, HG_DH)
            o_ref[:, cols] = o
            gate = gate_ref[:, cols]
            on = o * lax.rsqrt(jnp.mean(o * o, axis=-1, keepdims=True) + RMS_EPS) * ng_ref[:, cols]
            out_ref[:, cols] = (on * (gate * jax.nn.sigmoid(gate))).astype(out_ref.dtype)

    blk = pl.BlockSpec((tb, HG_WIDTH), lambda i: (i, 0))
    return pl.pallas_call(
        body, grid=(T // tb,),
        in_specs=_hg_specs(T, tb, 3, lambda i: i) + [pl.BlockSpec((3, HG_WIDTH), lambda i: (0, 0)), pl.BlockSpec((1, HG_WIDTH), lambda i: (0, 0))],
        out_specs=[blk, blk, pl.BlockSpec((nch, HG_HEADS, HG_DH, HG_DH), lambda i: (i, 0, 0, 0))],
        out_shape=[jax.ShapeDtypeStruct((T, HG_WIDTH), MXU_DTYPE), jax.ShapeDtypeStruct((T, HG_WIDTH), F32),
                   jax.ShapeDtypeStruct((T // HG_CHUNK, HG_HEADS, HG_DH, HG_DH), F32)],
        scratch_shapes=[pltpu.VMEM((HG_HEADS, HG_DH, HG_DH), F32), pltpu.VMEM((nch, HG_DH, HG_DH), F32),
                        pltpu.VMEM((nch, HG_CHUNK, HG_DH), F32)],
        compiler_params=_params("arbitrary"), name=name,
    )(proj, proj, proj, proj, logits, norm_g)


def _hg_bwd(proj, o_raw, states, dmix, logits, norm_g, *, name):
    T = proj.shape[0]
    tb = min(HG_TOKENS, T)
    nch = tb // HG_CHUNK
    nb = T // tb
    rev = lambda i: nb - 1 - i

    def body(q_ref, f_ref, i_ref, gate_ref, o_ref, s_ref, dout_ref, lg_ref, ng_ref, dp_ref, dlb_ref, dng_ref,
             dst_ref, inc_ref, dec_ref, after_ref):
        @pl.when(pl.program_id(0) == 0)
        def _():
            dst_ref[...] = jnp.zeros_like(dst_ref)
            dlb_ref[...] = jnp.zeros_like(dlb_ref)
            dng_ref[...] = jnp.zeros_like(dng_ref)

        lb_all = _hg_lower_bound(lg_ref)
        row = _chunk_row(tb)
        causal = _hg_causal(tb)
        for hh in range(HG_HEADS):
            cols = slice(hh * HG_DH, (hh + 1) * HG_DH)
            out_cols = lambda part: slice(part * HG_WIDTH + hh * HG_DH, part * HG_WIDTH + (hh + 1) * HG_DH)
            o, gate, dout, ng, lb = o_ref[:, cols], gate_ref[:, cols], dout_ref[:, cols], ng_ref[:, cols], lb_all[:, cols]
            sg = jax.nn.sigmoid(gate)
            r = lax.rsqrt(jnp.mean(o * o, axis=-1, keepdims=True) + RMS_EPS)
            oh = o * r
            dp_ref[:, out_cols(3)] = (dout * (oh * ng) * (sg * (1.0 + gate * (1.0 - sg)))).astype(dp_ref.dtype)
            don = dout * (gate * sg)
            dng_ref[:, cols] += jnp.sum(don * oh, axis=0, keepdims=True)
            doh = don * ng
            do = r * (doh - oh * jnp.mean(doh * oh, axis=-1, keepdims=True))

            t = _hg_terms(f_ref[:, cols], q_ref[:, cols], lb, row)
            v = i_ref[:, cols]
            scores = jnp.where(causal, _dot_nt(t["q_dec"], t["k_intra"]), 0.0)
            dscores = jnp.where(causal, _dot_nt(do, v), 0.0)
            inc_ref[...] = _per_chunk(_chunks(do), _chunks(t["q_dec"]), (1, 1))
            dec_ref[...] = _chunks(t["decay"])

            def step(it, dst):
                ci = nch - 1 - it
                after_ref[ci] = dst
                return dst * dec_ref[ci][0:1, :] + inc_ref[ci]

            dst_ref[hh] = lax.fori_loop(0, nch, step, dst_ref[hh], unroll=4)
            st, dst = s_ref[:, hh], after_ref[...]
            dqd = _dot(dscores, t["k_intra"]) + _per_chunk(_chunks(do), st, (2, 1)).reshape(tb, HG_DH)
            dki = _dot_tn(dscores, t["q_dec"])
            dks = _per_chunk(_chunks(v), dst, (2, 1)).reshape(tb, HG_DH)
            dp_ref[:, out_cols(2)] = (_dot_tn(scores, do) + _per_chunk(_chunks(t["k_state"]), dst, (2, 2)).reshape(tb, HG_DH)).astype(dp_ref.dtype)
            ddecay = jnp.broadcast_to(jnp.sum(st * dst, axis=1, keepdims=True), (nch, HG_CHUNK, HG_DH)).reshape(tb, HG_DH)
            dks_ks = dks * t["k_state"]
            d_glast = _chunk_cumsum(dks_ks, row) + ddecay * t["decay"]
            d_g = dqd * t["q_dec"] - dki * t["k_intra"] - dks_ks + jnp.where(row == HG_CHUNK - 1, d_glast, 0.0)
            df = _chunk_cumsum(d_g, row, reverse=True) / t["f"] - (dki * t["e_ng"] + dks * t["e_lg"])
            dp_ref[:, out_cols(0)] = (dqd * t["e_g"]).astype(dp_ref.dtype)
            dp_ref[:, out_cols(1)] = (df * (1.0 - lb) * t["sig"] * (1.0 - t["sig"])).astype(dp_ref.dtype)
            dlb_ref[:, cols] += jnp.sum(df * (1.0 - t["sig"]), axis=0, keepdims=True)

    blk = pl.BlockSpec((tb, HG_WIDTH), lambda i: (rev(i), 0))
    row_spec = pl.BlockSpec((1, HG_WIDTH), lambda i: (0, 0))
    return pl.pallas_call(
        body, grid=(nb,),
        in_specs=_hg_specs(T, tb, 3, rev) + [
            blk, pl.BlockSpec((nch, HG_HEADS, HG_DH, HG_DH), lambda i: (rev(i), 0, 0, 0)),
            pl.BlockSpec((tb, HG_WIDTH), lambda i: (rev(i), 1)), pl.BlockSpec((3, HG_WIDTH), lambda i: (0, 0)), row_spec],
        out_specs=[pl.BlockSpec((tb, 4 * HG_WIDTH), lambda i: (rev(i), 0)), row_spec, row_spec],
        out_shape=[jax.ShapeDtypeStruct((T, 4 * HG_WIDTH), MXU_DTYPE), jax.ShapeDtypeStruct((1, HG_WIDTH), F32), jax.ShapeDtypeStruct((1, HG_WIDTH), F32)],
        scratch_shapes=[pltpu.VMEM((HG_HEADS, HG_DH, HG_DH), F32), pltpu.VMEM((nch, HG_DH, HG_DH), F32),
                        pltpu.VMEM((nch, HG_CHUNK, HG_DH), F32), pltpu.VMEM((nch, HG_DH, HG_DH), F32)],
        compiler_params=_params("arbitrary"), name=name,
    )(proj, proj, proj, proj, o_raw, states, dmix, logits, norm_g)


def _shifted_copies(sh_ref, n_rows):
    keep = n_rows + CONV_HALO - 8
    for b in range(1, 8):
        sh_ref[b, 0:keep, :] = sh_ref[0, b:b + keep, :]


def _tap_rows(sh_ref, offset, r0, lanes):
    start = pl.multiple_of(r0 + (offset - offset % 8), 8)
    return sh_ref[offset % 8, pl.ds(start, CONV_ROWS), lanes]


def _conv_fwd(p, w_dw, b_dw, ln_g, ln_b, *, name):
    T, D = p.shape[0], p.shape[1] // 2
    tb = _token_block(T)
    hpb = tb // CONV_HALO
    lane_step = 512

    def body(p1_ref, p2_ref, q1_ref, q2_ref, w_ref, bdw_ref, g_ref, b_ref, a_ref, y_ref, act_ref, sh_ref):
        i = pl.program_id(0)
        a = p1_ref[...] * jax.nn.sigmoid(p2_ref[...])
        sh_ref[0, 0:CONV_HALO, :] = jnp.where(i > 0, q1_ref[...] * jax.nn.sigmoid(q2_ref[...]), 0.0)
        sh_ref[0, CONV_HALO:, :] = a
        a_ref[...] = a
        _shifted_copies(sh_ref, tb)

        def chunk(ci, _):
            r0 = pl.multiple_of(ci * CONV_ROWS, CONV_ROWS)
            for l0 in range(0, D, lane_step):
                lanes = slice(l0, l0 + lane_step)
                acc = jnp.broadcast_to(bdw_ref[:, lanes], (CONV_ROWS, lane_step))
                for k in range(CONV_WIDTH):
                    acc = acc + _tap_rows(sh_ref, CONV_HALO - CONV_WIDTH + 1 + k, r0, lanes) * w_ref[k:k + 1, lanes]
                y_ref[pl.ds(r0, CONV_ROWS), lanes] = acc
            return 0

        lax.fori_loop(0, tb // CONV_ROWS, chunk, 0)
        y = y_ref[...]
        mu = jnp.mean(y, axis=-1, keepdims=True)
        yc = y - mu
        s = yc * lax.rsqrt(jnp.mean(yc * yc, axis=-1, keepdims=True) + LN_EPS) * g_ref[...] + b_ref[...]
        act_ref[...] = (s * jax.nn.sigmoid(s)).astype(act_ref.dtype)

    prev = lambda i: jnp.maximum(i * hpb - 1, 0)
    blk = pl.BlockSpec((tb, D), lambda i: (i, 0))
    row = pl.BlockSpec((1, D), lambda i: (0, 0))
    return pl.pallas_call(
        body, grid=(T // tb,),
        in_specs=[blk, pl.BlockSpec((tb, D), lambda i: (i, 1)), pl.BlockSpec((CONV_HALO, D), lambda i: (prev(i), 0)),
                  pl.BlockSpec((CONV_HALO, D), lambda i: (prev(i), 1)), pl.BlockSpec((CONV_HALO, D), lambda i: (0, 0)), row, row, row],
        out_specs=[blk, blk, blk],
        out_shape=[jax.ShapeDtypeStruct((T, D), F32), jax.ShapeDtypeStruct((T, D), F32), jax.ShapeDtypeStruct((T, D), MXU_DTYPE)],
        scratch_shapes=[pltpu.VMEM((8, tb + CONV_HALO, D), F32)],
        compiler_params=_params("parallel"), name=name,
    )(p, p, p, p, w_dw, b_dw, ln_g, ln_b)


def _conv_bwd_norm(dact, y, ln_g, ln_b, *, name):
    T, D = y.shape
    tb = _token_block(T)

    def body(da_ref, y_ref, g_ref, b_ref, dy_ref, dg_ref, db_ref, cs_ref):
        @pl.when(pl.program_id(0) == 0)
        def _():
            dg_ref[...] = jnp.zeros_like(dg_ref)
            db_ref[...] = jnp.zeros_like(db_ref)
            cs_ref[...] = jnp.zeros_like(cs_ref)

        y, g = y_ref[...], g_ref[...]
        yc = y - jnp.mean(y, axis=-1, keepdims=True)
        rs = lax.rsqrt(jnp.mean(yc * yc, axis=-1, keepdims=True) + LN_EPS)
        yn = yc * rs
        s = yn * g + b_ref[...]
        sg = jax.nn.sigmoid(s)
        ds = da_ref[...] * (sg * (1.0 + s * (1.0 - sg)))
        dg_ref[...] += jnp.sum(ds * yn, axis=0, keepdims=True)
        db_ref[...] += jnp.sum(ds, axis=0, keepdims=True)
        dyn = ds * g
        dy = rs * (dyn - jnp.mean(dyn, axis=-1, keepdims=True) - yn * jnp.mean(dyn * yn, axis=-1, keepdims=True))
        dy_ref[...] = dy
        cs_ref[...] += jnp.sum(dy, axis=0, keepdims=True)

    blk = pl.BlockSpec((tb, D), lambda i: (i, 0))
    row = pl.BlockSpec((1, D), lambda i: (0, 0))
    rs_ = jax.ShapeDtypeStruct((1, D), F32)
    return pl.pallas_call(
        body, grid=(T // tb,), in_specs=[blk, blk, row, row], out_specs=[blk, row, row, row],
        out_shape=[jax.ShapeDtypeStruct((T, D), F32), rs_, rs_, rs_], compiler_params=_params("arbitrary"), name=name,
    )(dact, y, ln_g, ln_b)


def _conv_bwd_taps(dy, a, p, w_dw, *, name):
    T, D = dy.shape
    tb = _token_block(T)
    hpb = tb // CONV_HALO
    last = T // CONV_HALO - 1
    nb = T // tb
    lane_step = 128
    groups = CONV_ROWS // 8

    def body(dy_ref, dyn_ref, a_ref, p1_ref, p2_ref, w_ref, dp_ref, dw_ref, cs_ref, sh_ref, da_ref):
        i = pl.program_id(0)

        @pl.when(i == 0)
        def _():
            dw_ref[...] = jnp.zeros_like(dw_ref)
            cs_ref[...] = jnp.zeros_like(cs_ref)

        sh_ref[0, 0:tb, :] = dy_ref[...]
        sh_ref[0, tb:, :] = jnp.where(i < nb - 1, dyn_ref[...], 0.0)
        _shifted_copies(sh_ref, tb)
        for l0 in range(0, D, lane_step):
            lanes = slice(l0, l0 + lane_step)
            for taps in (range(0, CONV_WIDTH // 2 + 1), range(CONV_WIDTH // 2 + 1, CONV_WIDTH)):
                def chunk(ci, sums, taps=taps):
                    r0 = pl.multiple_of(ci * CONV_ROWS, CONV_ROWS)
                    a_c = a_ref[pl.ds(r0, CONV_ROWS), lanes]
                    da = jnp.zeros((CONV_ROWS, lane_step), F32) if taps[0] == 0 else da_ref[pl.ds(r0, CONV_ROWS), lanes]
                    new = []
                    for n, k in enumerate(taps):
                        s_k = _tap_rows(sh_ref, CONV_WIDTH - 1 - k, r0, lanes)
                        da = da + s_k * w_ref[k:k + 1, lanes]
                        new.append(sums[n] + jnp.sum((s_k * a_c).reshape(groups, 8, lane_step), axis=0))
                    da_ref[pl.ds(r0, CONV_ROWS), lanes] = da
                    return tuple(new)

                sums = lax.fori_loop(0, tb // CONV_ROWS, chunk, tuple(jnp.zeros((8, lane_step), F32) for _ in taps))
                for n, k in enumerate(taps):
                    dw_ref[k:k + 1, lanes] += jnp.sum(sums[n], axis=0, keepdims=True)
        da = da_ref[...]
        p1 = p1_ref[...]
        sg = jax.nn.sigmoid(p2_ref[...])
        dp1 = da * sg
        dp2 = da * p1 * (sg * (1.0 - sg))
        dp_ref[:, 0:D] = dp1.astype(dp_ref.dtype)
        dp_ref[:, D:] = dp2.astype(dp_ref.dtype)
        cs_ref[:, 0:D] += jnp.sum(dp1, axis=0, keepdims=True)
        cs_ref[:, D:] += jnp.sum(dp2, axis=0, keepdims=True)

    blk = pl.BlockSpec((tb, D), lambda i: (i, 0))
    return pl.pallas_call(
        body, grid=(nb,),
        in_specs=[blk, pl.BlockSpec((CONV_HALO, D), lambda i: (jnp.minimum((i + 1) * hpb, last), 0)), blk, blk,
                  pl.BlockSpec((tb, D), lambda i: (i, 1)), pl.BlockSpec((CONV_HALO, D), lambda i: (0, 0))],
        out_specs=[pl.BlockSpec((tb, 2 * D), lambda i: (i, 0)), pl.BlockSpec((CONV_HALO, D), lambda i: (0, 0)), pl.BlockSpec((1, 2 * D), lambda i: (0, 0))],
        out_shape=[jax.ShapeDtypeStruct((T, 2 * D), MXU_DTYPE), jax.ShapeDtypeStruct((CONV_HALO, D), F32), jax.ShapeDtypeStruct((1, 2 * D), F32)],
        scratch_shapes=[pltpu.VMEM((8, tb + CONV_HALO, D), F32), pltpu.VMEM((tb, D), F32)],
        compiler_params=_params("arbitrary"), name=name,
    )(dy, dy, a, p, p, w_dw)


def _row_block(rows):
    for tr in (512, 256, 128, 64, 32, 16, 8):
        if rows % tr == 0:
            return tr
    return rows


def _sum_leading(x, *, name):
    n, R, C = x.shape
    tr = _row_block(R)

    def body(x_ref, o_ref):
        acc = x_ref[0]
        for j in range(1, n):
            acc = acc + x_ref[j]
        o_ref[...] = acc

    return pl.pallas_call(
        body, grid=(R // tr,), in_specs=[pl.BlockSpec((n, tr, C), lambda i: (0, i, 0))], out_specs=pl.BlockSpec((tr, C), lambda i: (i, 0)),
        out_shape=jax.ShapeDtypeStruct((R, C), x.dtype), compiler_params=_params("parallel"), name=name,
    )(x)


def _add_pair(x, y, *, name):
    n, R, C = x.shape
    tr = _row_block(R)

    def body(x_ref, y_ref, o_ref):
        o_ref[...] = x_ref[...] + y_ref[...]

    blk = pl.BlockSpec((1, tr, C), lambda j, i: (j, i, 0))
    return pl.pallas_call(
        body, grid=(n, R // tr), in_specs=[blk, blk], out_specs=blk,
        out_shape=jax.ShapeDtypeStruct((n, R, C), x.dtype), compiler_params=_params("parallel", "parallel"), name=name,
    )(x, y)


def _adamw(w, g, m, v, *, name):
    R, C = w.shape
    tr = _row_block(R)
    c1, c2 = 1.0 - ADAM_B1 ** ADAM_STEP, 1.0 - ADAM_B2 ** ADAM_STEP

    def body(w_ref, g_ref, m_ref, v_ref, d_ref, nm_ref, nv_ref):
        g_ = g_ref[...]
        nm = ADAM_B1 * m_ref[...] + (1.0 - ADAM_B1) * g_
        nv = ADAM_B2 * v_ref[...] + (1.0 - ADAM_B2) * (g_ * g_)
        d_ref[...] = -ADAM_LR * ((nm / c1) / (jnp.sqrt(nv / c2) + ADAM_EPS) + ADAM_WD * w_ref[...])
        nm_ref[...] = nm
        nv_ref[...] = nv

    blk = pl.BlockSpec((tr, C), lambda i: (i, 0))
    shp = jax.ShapeDtypeStruct((R, C), F32)
    return pl.pallas_call(
        body, grid=(R // tr,), in_specs=[blk] * 4, out_specs=[blk] * 3, out_shape=[shp] * 3,
        compiler_params=_params("parallel"), name=name,
    )(w, g, m, v)


def _small_reduce(packs, logits, *, name):
    n, R, C = packs.shape

    def body(p_ref, lg_ref, s_ref, dlg_ref):
        acc = p_ref[0]
        for j in range(1, n):
            acc = acc + p_ref[j]
        s_ref[...] = acc
        lg = lg_ref[...]
        e = jnp.exp(lg - jnp.max(lg, axis=0, keepdims=True))
        sm = e / jnp.sum(e, axis=0, keepdims=True)
        dlb = acc[5:6, HG_WIDTH:2 * HG_WIDTH]
        first = lax.broadcasted_iota(jnp.int32, sm.shape, 0) == 0
        dlg_ref[...] = sm[0:1, :] * (jnp.where(first, 1.0, 0.0) - sm) * dlb

    whole = lambda shape: pl.BlockSpec(shape, lambda: (0,) * len(shape))
    return pl.pallas_call(
        body, in_specs=[whole((n, R, C)), whole(logits.shape)], out_specs=[whole((R, C)), whole(logits.shape)],
        out_shape=[jax.ShapeDtypeStruct((R, C), F32), jax.ShapeDtypeStruct(logits.shape, F32)],
        compiler_params=pltpu.CompilerParams(vmem_limit_bytes=VMEM_LIMIT), name=name,
    )(packs, logits)


HBM_SPEC = pl.BlockSpec(memory_space=pl.ANY)


def _place():
    return lax.axis_index("x"), lax.axis_index("y"), lax.axis_index("c")


class _Copies:
    def __init__(self, arrays, out_shapes, n_copies, make, finish):
        self.arrays, self.out_shapes, self.n_copies, self.make, self.finish = list(arrays), list(out_shapes), n_copies, make, finish

    def scratch(self):
        return [pltpu.SemaphoreType.DMA((self.n_copies,)), pltpu.SemaphoreType.DMA((self.n_copies,))]

    def run(self, name):
        n = len(self.arrays)

        def body(*refs):
            copies = self.make(refs[:n], refs[n:2 * n], *refs[2 * n:])
            for cp in copies:
                cp.start()
            for cp in copies:
                cp.wait()

        outs = pl.pallas_call(body, in_specs=[HBM_SPEC] * n, out_specs=[HBM_SPEC] * n, out_shape=self.out_shapes,
                              scratch_shapes=self.scratch(), name=name)(*self.arrays)
        return self.finish(outs)


def _remote(src, dst, send_sems, recv_sems, k, peer):
    return pltpu.make_async_remote_copy(src_ref=src, dst_ref=dst, send_sem=send_sems.at[k], recv_sem=recv_sems.at[k],
                                        device_id=peer, device_id_type=MESH)


def _same_core_peers(x, y, c):
    return [(1 - x, y, c), (x, 1 - y, c), (1 - x, 1 - y, c)]


def _all_peers(x, y, c):
    flip = lambda v, b: 1 - v if b else v
    return [(flip(x, r & 4), flip(y, r & 2), flip(c, r & 1)) for r in range(1, 8)]


def _gather(arrays, peers_of, slot_of, n_slots):
    n_peers = len(peers_of(0, 0, 0))

    def make(ins, outs, send_sems, recv_sems):
        x, y, c = _place()
        slot = slot_of(x, y, c)
        return [_remote(ins[a], outs[a].at[slot], send_sems, recv_sems, a * n_peers + k, peer)
                for a in range(len(arrays)) for k, peer in enumerate(peers_of(x, y, c))]

    def finish(outs):
        slot = slot_of(*_place())
        return [lax.dynamic_update_index_in_dim(o, a, slot, 0) for o, a in zip(outs, arrays)]

    shapes = [jax.ShapeDtypeStruct((n_slots,) + a.shape, a.dtype) for a in arrays]
    return _Copies(arrays, shapes, len(arrays) * n_peers, make, finish)


def _gather_chips(arrays):
    return _gather(arrays, _same_core_peers, lambda x, y, c: 2 * x + y, N_CHIPS)


def _gather_all(arrays):
    return _gather(arrays, _all_peers, lambda x, y, c: 4 * x + 2 * y + c, N_DEV)


def _pair_swap(a):
    def make(ins, outs, send_sems, recv_sems):
        x, y, c = _place()
        return [_remote(ins[0].at[1 - c], outs[0], send_sems, recv_sems, 0, (x, y, 1 - c))]

    return _Copies([a], [jax.ShapeDtypeStruct(a.shape[1:], a.dtype)], 1, make, lambda outs: outs[0])


def _chip_scatter(p):
    def make(ins, outs, send_sems, recv_sems):
        x, y, c = _place()
        return [_remote(ins[0].at[2 * px + py], outs[0].at[2 * x + y], send_sems, recv_sems, k, (px, py, pc))
                for k, (px, py, pc) in enumerate(_same_core_peers(x, y, c))]

    def finish(outs):
        x, y, _ = _place()
        me = 2 * x + y
        return lax.dynamic_update_index_in_dim(outs[0], lax.dynamic_index_in_dim(p, me, 0, keepdims=False), me, 0)

    return _Copies([p], [jax.ShapeDtypeStruct(p.shape, p.dtype)], 3, make, finish)


def _owner_scatter(blocks):
    def make(ins, outs, send_sems, recv_sems):
        x, y, c = _place()
        return [_remote(ins[0].at[2 * px + py, pc], outs[0].at[4 * x + 2 * y + c], send_sems, recv_sems, k, (px, py, pc))
                for k, (px, py, pc) in enumerate(_all_peers(x, y, c))]

    def finish(outs):
        x, y, c = _place()
        mine = lax.dynamic_index_in_dim(lax.dynamic_index_in_dim(blocks, 2 * x + y, 0, keepdims=False), c, 0, keepdims=False)
        return lax.dynamic_update_index_in_dim(outs[0], mine, 4 * x + 2 * y + c, 0)

    return _Copies([blocks], [jax.ShapeDtypeStruct((N_DEV,) + blocks.shape[2:], blocks.dtype)], N_DEV - 1, make, finish)


def _pair_gather(q):
    def make(ins, outs, send_sems, recv_sems):
        x, y, c = _place()
        return [_remote(ins[0], outs[0].at[c], send_sems, recv_sems, 0, (x, y, 1 - c))]

    return _Copies([q], [jax.ShapeDtypeStruct((2,) + q.shape, q.dtype)], 1, make,
                   lambda outs: lax.dynamic_update_index_in_dim(outs[0], q, _place()[2], 0))


def _grad_blocks(dw, kind):
    if kind == "cols2d":
        K, N = dw.shape
        b = dw.reshape(2, K // 2, N_CHIPS, N // N_CHIPS).transpose(2, 0, 1, 3)
    elif kind == "rows2d":
        b = dw.reshape(N_CHIPS, 2, dw.shape[0] // 8, dw.shape[1])
    elif kind == "cols3d":
        L, K, N = dw.shape
        b = dw.reshape(L, K, N_CHIPS, N // N_CHIPS).transpose(2, 0, 1, 3)
    else:
        L, K, N = dw.shape
        b = dw.reshape(L, N_CHIPS, K // N_CHIPS, N).transpose(1, 0, 2, 3)
    return b.reshape(N_CHIPS, 2, -1, D_MODEL)


def _pad_rows(a, rows):
    return jnp.concatenate([a, jnp.zeros((rows - a.shape[0],) + a.shape[1:], a.dtype)], axis=0)


def _forward_backward(x, target, W, late_weights=None, early_grads=None, last_grads=None):
    row = lambda a: a.reshape(1, -1)
    relu2 = lambda acc: (jnp.square(jnp.maximum(acc, 0.0)),)
    normed = lambda h, g: h * lax.rsqrt(jnp.mean(h * h, axis=-1, keepdims=True) + RMS_EPS) * g

    def residual_norm(acc, res, g):
        h = res + acc
        return h, normed(h, g)

    def norm_bwd(du, h_blk, dres_blk, g):
        dx, dg_terms = _rms_bwd_math(h_blk, g, du)
        dh = dres_blk + dx
        return dh, dh, jnp.sum(dg_terms, axis=0, keepdims=True), jnp.sum(dh, axis=0, keepdims=True)

    def matmul_norm_bwd(dy, w, h_in, g, dres, rider=None, *, tk=1024, name):
        return _matmul(dy, w, mode="nt", out_dtypes=[F32, MXU_DTYPE], epilogue=norm_bwd, tiles=[h_in, dres], rows=[row(g)],
                       n_sums=2, rider=rider, tk=tk, name=name)

    G = {}

    u0 = _rmsnorm_fwd(x, row(W["norm_mix_g"][0]), name="norm_mix0")
    proj, qkv = _matmul(u0, W["w_in"], mode="nn", out_dtypes=[F32, MXU_DTYPE], epilogue=lambda acc: (acc, acc), tn=896, name="in_proj")
    o_sb, got = _sb_fwd(qkv, late_weights and late_weights[0], name="sb_fwd")
    if late_weights:
        W = {**W, **late_weights[1](got)}
    hg_out, hg_o, hg_states = _hg_fwd(proj, W["hg_lb_logits"], row(W["hg_norm_g"]), name="hg_fwd")
    mix = jnp.concatenate([o_sb, hg_out], axis=-1)
    h1, u1 = _matmul(mix, W["w_out"], mode="nn", out_dtypes=[F32, MXU_DTYPE], epilogue=residual_norm, tiles=[x],
                     rows=[row(W["norm_ffn_g"][0])], name="out_proj")
    r0 = _matmul(u1, W["w_ff1"][0], mode="nn", out_dtypes=[MXU_DTYPE], epilogue=relu2, name="ff1_0")
    h2, u2 = _matmul(r0, W["w_ff2"][0], mode="nn", out_dtypes=[F32, MXU_DTYPE], epilogue=residual_norm, tiles=[h1],
                     rows=[row(W["norm_mix_g"][1])], name="ff2_0")
    p = _matmul(u2, W["w_glu"], mode="nn", out_dtypes=[F32], epilogue=lambda acc, b: (acc + b,), rows=[row(W["b_glu"])], name="glu_proj")
    w_dw = _pad_rows(W["w_dw"], CONV_HALO)
    ca, cy, cact = _conv_fwd(p, w_dw, row(W["b_dw"]), row(W["ln_g"]), row(W["ln_b"]), name="conv_fwd")
    h3, u3 = _matmul(cact, W["w_pw"], mode="nn", out_dtypes=[F32, MXU_DTYPE], epilogue=lambda acc, res, b, g: residual_norm(acc + b, res, g),
                     tiles=[h2], rows=[row(W["b_pw"]), row(W["norm_ffn_g"][1])], name="pw_proj")
    r1 = _matmul(u3, W["w_ff1"][1], mode="nn", out_dtypes=[MXU_DTYPE], epilogue=relu2, name="ff1_1")
    h4 = _matmul(r1, W["w_ff2"][1], mode="nn", out_dtypes=[F32], epilogue=lambda acc, res: (res + acc,), tiles=[h3], name="ff2_1")

    dh4, dh4_m, G["final_norm_g"], loss = _loss_head(h4, row(W["final_norm_g"]), target, name="loss_head")

    def mlp_bwd(dh, dh_m, h_in, u, r, layer, tag):
        d_relu2 = lambda acc, r_blk: (acc * (2.0 * jnp.sqrt(r_blk.astype(F32))),)
        da = _matmul(dh_m, W["w_ff2"][layer], mode="nt", out_dtypes=[MXU_DTYPE], epilogue=d_relu2, tiles=[r], name="d_ff2_act" + tag)
        dw2 = _matmul(r, dh_m, mode="tn", out_dtypes=[F32], tk=WGRAD_TOKENS, name="d_ff2_w" + tag)
        dw1 = _matmul(u, da, mode="tn", out_dtypes=[F32], tk=WGRAD_TOKENS, name="d_ff1_w" + tag)
        dh_in, dh_in_m, dg, cs = matmul_norm_bwd(da, W["w_ff1"][layer], h_in, W["norm_ffn_g"][layer], dh, name="d_ff1_act" + tag)
        return dh_in, dh_in_m, dg, cs, dw1, dw2

    dh3, dh3_m, dg_ffn1, cs_h3, dw1_1, dw2_1 = mlp_bwd(dh4, dh4_m, h3, u3, r1, 1, "1")
    G["b_pw"] = cs_h3
    dact = _matmul(dh3_m, W["w_pw"], mode="nt", out_dtypes=[F32], name="d_pw_act")
    G["w_pw"] = _matmul(cact, dh3_m, mode="tn", out_dtypes=[F32], tk=WGRAD_TOKENS, name="d_pw_w")
    dy, G["ln_g"], G["ln_b"], G["b_dw"] = _conv_bwd_norm(dact, cy, row(W["ln_g"]), row(W["ln_b"]), name="d_conv_norm")
    dp, G["w_dw"], G["b_glu"] = _conv_bwd_taps(dy, ca, p, w_dw, name="d_conv_taps")
    G["w_glu"] = _matmul(u2, dp, mode="tn", out_dtypes=[F32], tk=WGRAD_TOKENS, name="d_glu_w")
    dh2, dh2_m, dg_mix1, _ = matmul_norm_bwd(dp, W["w_glu"], h2, W["norm_mix_g"][1], dh3, name="d_glu_act")
    dh1, dh1_m, dg_ffn0, _, dw1_0, dw2_0 = mlp_bwd(dh2, dh2_m, h1, u1, r0, 0, "0")
    G["w_ff1"], G["w_ff2"] = jnp.stack([dw1_0, dw1_1]), jnp.stack([dw2_0, dw2_1])
    G["norm_ffn_g"] = jnp.concatenate([dg_ffn0, dg_ffn1], axis=0)
    dmix = _matmul(dh1_m, W["w_out"], mode="nt", out_dtypes=[F32], name="d_out_act")
    G["w_out"] = _matmul(mix, dh1_m, mode="tn", out_dtypes=[F32], tk=WGRAD_TOKENS, name="d_out_w")
    riding = early_grads(G) if early_grads else None
    (dsq, dsk, dsv), got = _sb_bwd(qkv, dmix, riding, name="sb_bwd")
    d_hg, G["hg_lb"], G["hg_norm_g"] = _hg_bwd(proj, hg_o, hg_states, dmix, W["hg_lb_logits"], row(W["hg_norm_g"]), name="hg_bwd")
    dproj = jnp.concatenate([dsq, _mx(dsk), _mx(dsv), d_hg], axis=-1)
    G["w_in"] = _matmul(u0, dproj, mode="tn", out_dtypes=[F32], tn=896, tk=WGRAD_TOKENS, name="d_in_w")
    last = last_grads(G) if last_grads else None
    res = matmul_norm_bwd(dproj, W["w_in"], x, W["norm_mix_g"][0], dh1, last, tk=896, name="d_in_act")
    (dx, _, dg_mix0, _), got_last = res if last_grads else (res, [])
    G["norm_mix_g"] = jnp.concatenate([dg_mix0, dg_mix1], axis=0)
    return loss, dx, G, [(riding, got), (last, got_last)]


BIG = (("w_out_ab", "w_out", "rows2d"), ("conv_w_glu", "w_glu", "cols2d"), ("conv_w_pw", "w_pw", "rows2d"),
       ("w_ff1", "w_ff1", "cols3d"), ("w_ff2", "w_ff2", "rows3d"), ("w_in_ab", "w_in", "cols2d"))
LATE = BIG[:-1]
SMALL_SHARDED = ("conv_b_glu", "conv_w_dw", "conv_b_dw", "conv_ln_g", "conv_ln_b", "conv_b_pw")
REPLICATED = ("norm_mix_g", "norm_ffn_g", "hg_lb_logits", "hg_norm_g", "final_norm_g")
ORDER = ("norm_mix_g", "norm_ffn_g", "w_in_ab", "w_out_ab", "hg_lb_logits", "hg_norm_g", "conv_w_glu", "conv_b_glu",
         "conv_w_dw", "conv_b_dw", "conv_ln_g", "conv_ln_b", "conv_w_pw", "conv_b_pw", "w_ff1", "w_ff2", "final_norm_g")


def _step(x, loss_target, w, m, v):
    D = D_MODEL
    x2, t2 = x.reshape(-1, D), loss_target.reshape(-1, D)
    chip = 2 * lax.axis_index("x") + lax.axis_index("y")
    c = lax.axis_index("c")

    small_in = jnp.concatenate([w["conv_b_glu"].reshape(2, SHARD), w["conv_w_dw"].reshape(CONV_WIDTH, SHARD)] +
                               [w[n].reshape(1, SHARD) for n in ("conv_b_dw", "conv_ln_g", "conv_ln_b", "conv_b_pw")], axis=0)
    g_in, gs = _gather_chips([w["w_in_ab"].astype(MXU_DTYPE), _pad_rows(small_in, SMALL_IN_ROWS)]).run("gather_first_weights")
    vec = lambda r0, r1: gs[:, r0:r1].transpose(1, 0, 2).reshape(r1 - r0, D)
    W = {
        "w_in": _ChipWeight(g_in[:, 0], "cols"),
        "b_glu": gs[:, 0:2].reshape(2 * D), "w_dw": vec(2, 33), "b_dw": vec(33, 34)[0], "ln_g": vec(34, 35)[0],
        "ln_b": vec(35, 36)[0], "b_pw": vec(36, 37)[0],
        "norm_mix_g": w["norm_mix_g"], "norm_ffn_g": w["norm_ffn_g"], "hg_lb_logits": w["hg_lb_logits"],
        "hg_norm_g": w["hg_norm_g"], "final_norm_g": w["final_norm_g"],
    }
    late = _gather_chips([w[n].astype(MXU_DTYPE) for n, _, _ in LATE])

    def assemble(got):
        gw = dict(zip([s for _, s, _ in LATE], late.finish(got)))
        layers = lambda g, along: [_ChipWeight(g, along, (layer,)) for layer in range(2)]
        return {"w_out": gw["w_out"].reshape(D, D), "w_glu": _ChipWeight(gw["w_glu"][:, 0], "cols"), "w_pw": gw["w_pw"].reshape(D, D),
                "w_ff1": layers(gw["w_ff1"], "cols"), "w_ff2": layers(gw["w_ff2"], "rows")}

    def early_grads(G):
        return _owner_scatter(jnp.concatenate([_grad_blocks(G[s], kind) for _, s, kind in LATE], axis=2))

    def last_grads(G):
        blocks = _grad_blocks(G["w_in"], "cols2d")
        from_pair = _pair_swap(blocks.transpose(1, 0, 2, 3)).run("grads_pair_swap_in")
        return _chip_scatter(_add_pair(lax.dynamic_index_in_dim(blocks, c, axis=1, keepdims=False), from_pair, name="grads_pair_add_in"))

    loss, dx, G, riders = _forward_backward(x2, t2, W, (late, assemble), early_grads, last_grads)
    halves = [_sum_leading(copies.finish(got), name="grads_add_" + tag) for (copies, got), tag in zip(riders, ("late", "in"))]
    half = jnp.concatenate(halves, axis=0)
    full = _pair_gather(half).run("grads_pair_gather")

    pack = jnp.concatenate([
        G["norm_mix_g"], G["norm_ffn_g"], G["final_norm_g"], jnp.concatenate([G["hg_norm_g"], G["hg_lb"]], axis=1),
        _pad_rows(jnp.broadcast_to(loss, (1, D)), 2), G["b_glu"].reshape(2, D), G["w_dw"], G["b_dw"], G["ln_g"], G["ln_b"], G["b_pw"],
    ], axis=0)
    pack = _pad_rows(pack, SMALL_ROWS)
    (packs,) = _gather_all([pack]).run("gather_small_grads")
    ssum, d_logits = _small_reduce(packs, w["hg_lb_logits"], name="reduce_small_grads")
    cut = lambda r0, r1: lax.dynamic_slice(ssum, (r0, chip * SHARD), (r1 - r0, SHARD))
    grads = {
        "norm_mix_g": ssum[0:2], "norm_ffn_g": ssum[2:4], "final_norm_g": ssum[4], "hg_norm_g": ssum[5, :HG_WIDTH].reshape(1, HG_HEADS, HG_DH),
        "hg_lb_logits": d_logits,
        "conv_b_glu": lax.dynamic_slice(ssum[8:10].reshape(1, 2 * D), (0, chip * 2 * SHARD), (1, 2 * SHARD)),
        "conv_w_dw": cut(10, 10 + CONV_WIDTH).reshape(1, CONV_WIDTH, SHARD),
        "conv_b_dw": cut(42, 43), "conv_ln_g": cut(43, 44), "conv_ln_b": cut(44, 45), "conv_b_pw": cut(45, 46),
    }
    loss_out = ssum[6, 0]

    off = 0
    for n, s, kind in BIG:
        shard = w[n].shape
        rows = w[n].size // (2 * D)
        grads[n] = full[:, off:off + rows].reshape(shard)
        off += rows

    delta, new_m, new_v = {}, {}, {}
    for n, _, _ in BIG:
        view = lambda a: a.reshape(-1, a.shape[-1])
        outs = _adamw(view(w[n]), view(grads[n]), view(m[n]), view(v[n]), name="adamw_" + n)
        delta[n], new_m[n], new_v[n] = (o.reshape(w[n].shape) for o in outs)
    small = SMALL_SHARDED + REPLICATED
    sizes = [w[n].size for n in small]
    total = sum(sizes)
    rows = -(-total // (8 * D)) * 8
    packed = lambda d: _pad_rows(jnp.concatenate([d[n].reshape(-1) for n in small]).reshape(-1, 128), rows * 8).reshape(rows, D)
    outs = _adamw(packed(w), packed(grads), packed(m), packed(v), name="adamw_small")
    off = 0
    for n, size in zip(small, sizes):
        delta[n], new_m[n], new_v[n] = (o.reshape(-1)[off:off + size].reshape(w[n].shape) for o in outs)
        off += size
    grads = {n: grads[n].reshape(w[n].shape) for n in ORDER}
    return (loss_out, dx.reshape(x.shape), *[grads[n] for n in ORDER], *[delta[n] for n in ORDER],
            *[new_m[n] for n in ORDER], *[new_v[n] for n in ORDER])


def kernel(x, norm_mix_g, norm_ffn_g, w_in_ab, w_out_ab, hg_lb_logits, hg_norm_g, conv_w_glu, conv_b_glu, conv_w_dw, conv_b_dw, conv_ln_g, conv_ln_b, conv_w_pw, conv_b_pw, w_ff1, w_ff2, final_norm_g, loss_target, m_norm_mix_g, m_norm_ffn_g, m_w_in_ab, m_w_out_ab, m_hg_lb_logits, m_hg_norm_g, m_conv_w_glu, m_conv_b_glu, m_conv_w_dw, m_conv_b_dw, m_conv_ln_g, m_conv_ln_b, m_conv_w_pw, m_conv_b_pw, m_w_ff1, m_w_ff2, m_final_norm_g, v_norm_mix_g, v_norm_ffn_g, v_w_in_ab, v_w_out_ab, v_hg_lb_logits, v_hg_norm_g, v_conv_w_glu, v_conv_b_glu, v_conv_w_dw, v_conv_b_dw, v_conv_ln_g, v_conv_ln_b, v_conv_w_pw, v_conv_b_pw, v_w_ff1, v_w_ff2, v_final_norm_g):
    args = locals()
    w = {n: args[n] for n in ORDER}
    m = {n: args["m_" + n] for n in ORDER}
    v = {n: args["v_" + n] for n in ORDER}
    return _step(x, loss_target, w, m, v)
```

```python
import functools

import jax
import jax.numpy as jnp
from jax import lax
from jax.experimental import pallas as pl
from jax.experimental.pallas import tpu as pltpu

F32 = jnp.float32
MXU_DTYPE = jnp.bfloat16
MESH = pl.DeviceIdType.MESH

D_MODEL = 1024
SB_HEADS, SB_DH, SB_WIDTH = 8, 64, 512
SB_KEYS = 512
SB_SUB = 256
SB_ROWS_FWD, SB_ROWS_BWD = 512, 256
HG_HEADS, HG_DH, HG_WIDTH = 4, 128, 512
HG_CHUNK = 16
HG_TOKENS = 256
CONV_WIDTH = 31
CONV_HALO = 32
CONV_ROWS = 32
RMS_EPS = 1e-6
LN_EPS = 1e-5
N_CHIPS = 4
N_DEV = 8
SHARD = D_MODEL // N_CHIPS
SMALL_IN_ROWS = 40
SMALL_ROWS = 48
WGRAD_TOKENS = 4096
VMEM_LIMIT = 56 * 1024 * 1024

ADAM_LR, ADAM_B1, ADAM_B2, ADAM_EPS, ADAM_WD, ADAM_STEP = 0.001, 0.9, 0.999, 1e-08, 0.01, 10


def _params(*sem):
    return pltpu.CompilerParams(dimension_semantics=sem, vmem_limit_bytes=VMEM_LIMIT)


def _mx(v):
    return v.astype(MXU_DTYPE)


def _dot(a, b):
    return jnp.dot(_mx(a), _mx(b), preferred_element_type=F32)


def _dot_nt(a, b):
    return lax.dot_general(_mx(a), _mx(b), (((1,), (1,)), ((), ())), preferred_element_type=F32)


def _dot_tn(a, b):
    return lax.dot_general(_mx(a), _mx(b), (((0,), (0,)), ((), ())), preferred_element_type=F32)


def _neg_abs(x):
    bits = lax.bitcast_convert_type(x, jnp.uint32) | jnp.uint32(0x80000000)
    return lax.bitcast_convert_type(bits, F32)


def _key_order_sums(v, tri2, later):
    hi = _mx(v)
    lo = _mx(v - hi.astype(F32))
    n = v.shape[1] // SB_SUB
    blocks = [slice(b * SB_SUB, (b + 1) * SB_SUB) for b in range(n)]
    totals = [jnp.sum(v[:, sl], axis=1, keepdims=True) for sl in blocks]
    far, running = [None] * n, None
    for b in (reversed(range(n)) if later else range(n)):
        far[b] = running
        running = totals[b] if running is None else running + totals[b]
    sums = []
    for b, sl in enumerate(blocks):
        inside = jnp.dot(jnp.concatenate([hi[:, sl], lo[:, sl]], axis=1), tri2, preferred_element_type=F32)
        sums.append(inside if far[b] is None else inside + far[b])
    return jnp.concatenate(sums, axis=1), running


class _ChipWeight:
    def __init__(self, parts, along, lead=()):
        self.parts, self.along, self.lead = parts, along, tuple(lead)
        r, c = parts.shape[-2:]
        self.shape = (r, N_CHIPS * c) if along == "cols" else (N_CHIPS * r, c)

    def _gathered_is_n(self, mode):
        return (self.along == "cols") == (mode in ("nn", "tn"))

    def tile(self, mode, tn, tk):
        r, c = self.parts.shape[-2:]
        part = c if self.along == "cols" else r
        return (part, tk) if self._gathered_is_n(mode) else (tn, part)

    def spec(self, mode, tn, tk):
        squeezed = (None,) * (1 + len(self.lead))
        lead, cols, by_n = self.lead, self.along == "cols", self._gathered_is_n(mode)
        block = (tn, tk) if mode == "nt" else (tk, tn)

        def index(i, j, k):
            chip, other = (j, k) if by_n else (k, j)
            return (chip,) + lead + ((other, 0) if cols else (0, other))

        return pl.BlockSpec(squeezed + block, index)


def _matmul(a, b, *, mode, out_dtypes, epilogue=None, tiles=(), rows=(), n_sums=0, rider=None, tm=1024, tn=1024, tk=1024, name):
    b_shape = b.shape
    if mode == "nn":
        (M, K), N = a.shape, b_shape[1]
    elif mode == "nt":
        (M, K), N = a.shape, b_shape[0]
    else:
        (K, M), N = a.shape, b_shape[1]
    if isinstance(b, _ChipWeight):
        tn, tk = b.tile(mode, tn, tk)
    tm, tn, tk = min(tm, M), min(tn, N), min(tk, K)
    assert M % tm == 0 and N % tn == 0 and K % tk == 0, (name, M, N, K)
    nk = K // tk
    a_spec = pl.BlockSpec((tk, tm), lambda i, j, k: (k, i)) if mode == "tn" else pl.BlockSpec((tm, tk), lambda i, j, k: (i, k))
    if isinstance(b, _ChipWeight):
        b_spec, b = b.spec(mode, tn, tk), b.parts
    else:
        b_spec = pl.BlockSpec((tn, tk), lambda i, j, k: (j, k)) if mode == "nt" else pl.BlockSpec((tk, tn), lambda i, j, k: (k, j))
    dims = {"nn": ((1,), (0,)), "nt": ((1,), (1,)), "tn": ((0,), (0,))}[mode]
    n_t, n_r, n_o = len(tiles), len(rows), len(out_dtypes)
    n_x = 0 if rider is None else len(rider.arrays)
    grid = (M // tm, N // tn, nk)
    if epilogue is None:
        epilogue = lambda acc: (acc,)

    def body(a_ref, b_ref, *rest):
        extra, x_in, rest = rest[:n_t + n_r], rest[n_t + n_r:n_t + n_r + n_x], rest[n_t + n_r + n_x:]
        outs, sums, x_out, acc_ref, sems = rest[:n_o], rest[n_o:n_o + n_sums], rest[n_o + n_sums:n_o + n_sums + n_x], rest[n_o + n_sums + n_x], rest[n_o + n_sums + n_x + 1:]
        i, j, k = (pl.program_id(d) for d in range(3))
        if rider is not None:
            @pl.when((i == 0) & (j == 0) & (k == 0))
            def _():
                for cp in rider.make(x_in, x_out, *sems):
                    cp.start()

        @pl.when(k == 0)
        def _():
            acc_ref[...] = jnp.zeros_like(acc_ref)

        acc_ref[...] += lax.dot_general(_mx(a_ref[...]), _mx(b_ref[...]), (dims, ((), ())), preferred_element_type=F32)

        @pl.when(k == nk - 1)
        def _():
            res = epilogue(acc_ref[...], *[e[...] for e in extra])
            for o_ref, r in zip(outs, res[:n_o]):
                o_ref[...] = r.astype(o_ref.dtype)
            for s_ref, r in zip(sums, res[n_o:]):
                @pl.when(i == 0)
                def _():
                    s_ref[...] = jnp.zeros_like(s_ref)

                s_ref[...] += r

        if rider is not None:
            @pl.when((i == grid[0] - 1) & (j == grid[1] - 1) & (k == grid[2] - 1))
            def _():
                for cp in rider.make(x_in, x_out, *sems):
                    cp.wait()

    tile_spec = pl.BlockSpec((tm, tn), lambda i, j, k: (i, j))
    row_spec = pl.BlockSpec((1, tn), lambda i, j, k: (0, j))
    ordered = n_sums > 0 or rider is not None
    outs = pl.pallas_call(
        body, grid=grid,
        in_specs=[a_spec, b_spec] + [tile_spec] * n_t + [row_spec] * n_r + [HBM_SPEC] * n_x,
        out_specs=[tile_spec] * n_o + [row_spec] * n_sums + [HBM_SPEC] * n_x,
        out_shape=[jax.ShapeDtypeStruct((M, N), dt) for dt in out_dtypes] + [jax.ShapeDtypeStruct((1, N), F32)] * n_sums
        + ([] if rider is None else rider.out_shapes),
        scratch_shapes=[pltpu.VMEM((tm, tn), F32)] + ([] if rider is None else rider.scratch()),
        compiler_params=_params(*(("arbitrary",) * 3 if ordered else ("parallel", "parallel", "arbitrary"))), name=name,
    )(a, b, *tiles, *rows, *([] if rider is None else rider.arrays))
    res = outs[0] if n_o + n_sums == 1 else outs[:n_o + n_sums]
    return res if rider is None else (res, outs[n_o + n_sums:])


def _token_block(T):
    return min(512, T)


def _rmsnorm_fwd(h, g, *, name):
    T, D = h.shape
    tb = _token_block(T)

    def body(h_ref, g_ref, u_ref):
        x = h_ref[...]
        r = lax.rsqrt(jnp.mean(x * x, axis=-1, keepdims=True) + RMS_EPS)
        u_ref[...] = (x * r * g_ref[...]).astype(u_ref.dtype)

    blk = pl.BlockSpec((tb, D), lambda i: (i, 0))
    return pl.pallas_call(
        body, grid=(T // tb,), in_specs=[blk, pl.BlockSpec((1, D), lambda i: (0, 0))], out_specs=blk,
        out_shape=jax.ShapeDtypeStruct((T, D), MXU_DTYPE), compiler_params=_params("parallel"), name=name,
    )(h, g)


def _rms_bwd_math(x, g, du):
    r = lax.rsqrt(jnp.mean(x * x, axis=-1, keepdims=True) + RMS_EPS)
    gd = g * du
    dx = r * gd - x * (r * r * r) * jnp.mean(gd * x, axis=-1, keepdims=True)
    return dx, du * x * r


def _loss_head(h, g, target, *, name):
    T, D = h.shape
    tb = _token_block(T)

    def body(h_ref, g_ref, t_ref, dh_ref, dhm_ref, dg_ref, loss_ref):
        @pl.when(pl.program_id(0) == 0)
        def _():
            dg_ref[...] = jnp.zeros_like(dg_ref)
            loss_ref[...] = jnp.zeros_like(loss_ref)

        x, gg = h_ref[...], g_ref[...]
        r = lax.rsqrt(jnp.mean(x * x, axis=-1, keepdims=True) + RMS_EPS)
        diff = x * r * gg - t_ref[...]
        per_token = jnp.mean(diff * diff, axis=-1, keepdims=True)
        loss_ref[...] += 0.5 * jnp.sum(per_token, axis=0, keepdims=True)
        dx, dg_terms = _rms_bwd_math(x, gg, diff / D)
        dh_ref[...] = dx
        dhm_ref[...] = dx.astype(dhm_ref.dtype)
        dg_ref[...] += jnp.sum(dg_terms, axis=0, keepdims=True)

    blk = pl.BlockSpec((tb, D), lambda i: (i, 0))
    row = pl.BlockSpec((1, D), lambda i: (0, 0))
    return pl.pallas_call(
        body, grid=(T // tb,), in_specs=[blk, row, blk], out_specs=[blk, blk, row, pl.BlockSpec((1, 1), lambda i: (0, 0))],
        out_shape=[jax.ShapeDtypeStruct((T, D), F32), jax.ShapeDtypeStruct((T, D), MXU_DTYPE), jax.ShapeDtypeStruct((1, D), F32),
                   jax.ShapeDtypeStruct((1, 1), F32)],
        compiler_params=_params("arbitrary"), name=name,
    )(h, g, target)


def _sb_scores(qm, ks, later, tri, mask, need_log_beta=True):
    z = _dot_nt(qm, ks)
    sp = jnp.maximum(z, 0.0) + jnp.log(1.0 + jnp.exp(_neg_abs(z)))
    lb = z - sp if need_log_beta else None
    if mask is not None:
        sp = jnp.where(mask, sp, 0.0)
    after, total = _key_order_sums(sp, tri, later=True)
    w = jnp.exp((lb if need_log_beta else z) - (after + later))
    if mask is not None:
        w = jnp.where(mask, w, 0.0)
    return total, lb, w


def _sb_setup(q_ref, rows, tri_ref, inclusive=False):
    i, hsel = pl.program_id(1), pl.program_id(2)
    lane = lax.broadcasted_iota(jnp.int32, (rows, 2 * SB_DH), 1)
    mine = (lane >= SB_DH) == (hsel == 1)
    diag = (i * rows) // SB_KEYS
    t = i * rows + lax.broadcasted_iota(jnp.int32, (rows, SB_KEYS), 0)
    s = diag * SB_KEYS + lax.broadcasted_iota(jnp.int32, (rows, SB_KEYS), 1)
    @pl.when((pl.program_id(0) == 0) & (i == 0) & (hsel == 0))
    def _():
        a = lax.broadcasted_iota(jnp.int32, (2 * SB_SUB, SB_SUB), 0) % SB_SUB
        b = lax.broadcasted_iota(jnp.int32, (2 * SB_SUB, SB_SUB), 1)
        tri_ref[0] = (a >= b if inclusive else a > b).astype(tri_ref.dtype)
        tri_ref[1] = (a < b).astype(tri_ref.dtype)

    return i, hsel, mine, diag, s < t, tri_ref[0], tri_ref[1]


_SB_TRI = pltpu.VMEM((2, 2 * SB_SUB, SB_SUB), MXU_DTYPE)


def _sb_keys(j, n=1, keys=None):
    return pl.ds(pl.multiple_of(j * SB_KEYS, SB_KEYS), n * SB_KEYS if keys is None else keys)


def _sb_descend(n, step, carry, wide):
    pair = (lambda j, cr: step(j, 2, cr)) if wide else (lambda j, cr: step(j, 1, step(j + 1, 1, cr)))
    carry = lax.fori_loop(0, n // 2, lambda it, cr: pair(n - 2 - 2 * it, cr), carry)
    return lax.cond(n % 2 == 1, lambda cr: step(0, 1, cr), lambda cr: cr, carry)


def _sb_ascend(n, step, carry):
    odd = n % 2
    carry = lax.cond(odd == 1, lambda cr: step(0, 1, cr), lambda cr: cr, carry)
    return lax.fori_loop(0, n // 2, lambda it, cr: step(odd + 2 * it, 2, cr), carry)


def _sb_call(body, qkv, extra_in, out_blocks, out_dtypes, scratch, rider, rows, *, name):
    T = qkv.shape[0]
    n_pairs = SB_HEADS // 2
    grid = (n_pairs, T // rows, 2)
    pair = lambda col0: pl.BlockSpec((rows, 2 * SB_DH), lambda p, i, h: (i, col0 + p))
    whole = lambda col0: pl.BlockSpec((T, 2 * SB_DH), lambda p, i, h: (0, col0 + p))
    in_specs = [pair(0), whole(n_pairs), whole(2 * n_pairs)] + [pair(0)] * len(extra_in)
    out_specs = [pair(0) if kind == "pair" else whole(0) for kind in out_blocks]
    n_in, n_out, n_r = len(in_specs), len(out_specs), 0 if rider is None else len(rider.arrays)

    def kernel_body(*refs):
        ins, r_in = refs[:n_in], refs[n_in:n_in + n_r]
        outs, r_out = refs[n_in + n_r:n_in + n_r + n_out], refs[n_in + n_r + n_out:n_in + 2 * n_r + n_out]
        rest = refs[n_in + 2 * n_r + n_out:]
        ids = [pl.program_id(a) for a in range(3)]
        if rider is not None:
            @pl.when((ids[0] == 0) & (ids[1] == 0) & (ids[2] == 0))
            def _():
                for cp in rider.make(r_in, r_out, *rest[len(scratch):]):
                    cp.start()

        body(ins, outs, rest[:len(scratch)])
        if rider is not None:
            @pl.when((ids[0] == grid[0] - 1) & (ids[1] == grid[1] - 1) & (ids[2] == grid[2] - 1))
            def _():
                for cp in rider.make(r_in, r_out, *rest[len(scratch):]):
                    cp.wait()

    res = pl.pallas_call(
        kernel_body, grid=grid, in_specs=in_specs + [HBM_SPEC] * n_r, out_specs=out_specs + [HBM_SPEC] * n_r,
        out_shape=[jax.ShapeDtypeStruct((T, SB_WIDTH), dt) for dt in out_dtypes] + ([] if rider is None else rider.out_shapes),
        scratch_shapes=list(scratch) + ([] if rider is None else rider.scratch()),
        compiler_params=_params("arbitrary", "arbitrary", "arbitrary"), name=name,
    )(qkv, qkv, qkv, *extra_in, *([] if rider is None else rider.arrays))
    return res[:n_out], res[n_out:]


def _sb_fwd(qkv, rider=None, *, name):
    rows = min(SB_ROWS_FWD, qkv.shape[0])
    scale = SB_DH ** -0.5

    def body(ins, outs, scratch):
        (q_ref, k_ref, v_ref), (o_ref,), (tri_ref,) = ins, outs, scratch
        i, hsel, mine, diag, mask, tri, _ = _sb_setup(q_ref, rows, tri_ref, inclusive=True)
        qm = jnp.where(mine, q_ref[...], 0) * scale

        def tile(j, n, m, later, acc):
            total, _, w = _sb_scores(qm, k_ref[_sb_keys(j, n), :], later, tri, m, need_log_beta=False)
            return later + total, acc + _dot(w, v_ref[_sb_keys(j, n), :])

        carry = tile(diag, 1, mask, jnp.zeros((rows, 1), F32), jnp.zeros((rows, 2 * SB_DH), F32))
        _, acc = _sb_descend(diag, lambda j, n, cr: tile(j, n, None, *cr), carry, wide=False)
        res = jnp.where(mine, acc, 0.0).astype(o_ref.dtype)

        @pl.when(hsel == 0)
        def _():
            o_ref[...] = res

        @pl.when(hsel == 1)
        def _():
            o_ref[...] += res

    (o,), got = _sb_call(body, qkv, [], ["pair"], [MXU_DTYPE], [_SB_TRI], rider, rows, name=name)
    return o, got


def _sb_bwd(qkv, dmix, rider=None, *, name):
    T = qkv.shape[0]
    rows = min(SB_ROWS_BWD, T)
    scale = SB_DH ** -0.5

    def body(ins, outs, scratch):
        (q_ref, k_ref, v_ref, do_ref), (dq_ref, dk_ref, dv_ref), (da_ref, beta_ref, tri_ref) = ins, outs, scratch
        i, hsel, mine, diag, mask, tri, tri_before = _sb_setup(q_ref, rows, tri_ref)

        @pl.when((i == 0) & (hsel == 0))
        def _():
            dk_ref[...] = jnp.zeros_like(dk_ref)
            dv_ref[...] = jnp.zeros_like(dv_ref)

        qm = jnp.where(mine, q_ref[...], 0) * scale
        do_m = _mx(jnp.where(mine, do_ref[...], 0.0))

        def weights(j, n, m, later, keys=None):
            sl = _sb_keys(j, n, keys)
            total, lb, w = _sb_scores(qm, k_ref[sl, :], later, tri, m)
            da, beta = _dot_nt(do_m, v_ref[sl, :]) * w, jnp.exp(lb)
            if keys is None:
                for t in range(n):
                    da_ref[j + t] = da[:, t * SB_KEYS:(t + 1) * SB_KEYS]
                    beta_ref[j + t] = beta[:, t * SB_KEYS:(t + 1) * SB_KEYS]
            else:
                da_ref[j, :, 0:keys] = da
                beta_ref[j, :, 0:keys] = beta
            dv_ref[sl, :] += _dot_tn(w, do_m)
            return later + total

        short = (i * rows) % SB_KEYS + rows <= SB_SUB
        short_mask = mask[:, 0:SB_SUB]
        later = lax.cond(short, lambda z: weights(diag, 1, short_mask, z, SB_SUB), lambda z: weights(diag, 1, mask, z),
                         jnp.zeros((rows, 1), F32))
        _sb_descend(diag, lambda j, n, c: weights(j, n, None, c), later, wide=True)

        def logits(j, n, m, before, dq, keys=None):
            if keys is None:
                da = jnp.concatenate([da_ref[j + t] for t in range(n)], axis=1)
                beta = jnp.concatenate([beta_ref[j + t] for t in range(n)], axis=1)
            else:
                da, beta = da_ref[j, :, 0:keys], beta_ref[j, :, 0:keys]
            earlier, total = _key_order_sums(da, tri_before, later=False)
            dz = da - beta * (da + earlier + before)
            if m is not None:
                dz = jnp.where(m, dz, 0.0)
            dz = _mx(dz)
            dk_ref[_sb_keys(j, n, keys), :] += _dot_tn(dz, qm)
            return before + total, dq + _dot(dz, k_ref[_sb_keys(j, n, keys), :])

        carry = (jnp.zeros((rows, 1), F32), jnp.zeros((rows, 2 * SB_DH), F32))
        carry = _sb_ascend(diag, lambda j, n, cr: logits(j, n, None, *cr), carry)
        dq = lax.cond(short, lambda cr: logits(diag, 1, short_mask, *cr, keys=SB_SUB)[1], lambda cr: logits(diag, 1, mask, *cr)[1], carry)
        res = jnp.where(mine, dq * scale, 0.0).astype(dq_ref.dtype)

        @pl.when(hsel == 0)
        def _():
            dq_ref[...] = res

        @pl.when(hsel == 1)
        def _():
            dq_ref[...] += res

    n_tiles = T // SB_KEYS
    scratch = [pltpu.VMEM((n_tiles, rows, SB_KEYS), F32), pltpu.VMEM((n_tiles, rows, SB_KEYS), F32), _SB_TRI]
    return _sb_call(body, qkv, [dmix], ["pair", "whole", "whole"], [MXU_DTYPE, F32, F32], scratch, rider, rows, name=name)


def _chunk_row(n):
    return lax.broadcasted_iota(jnp.int32, (n, HG_DH), 0) % HG_CHUNK


def _chunk_cumsum(x, row, reverse=False):
    n = x.shape[0]
    for sh in (1, 2, 4, 8):
        if reverse:
            x = x + jnp.where(row < HG_CHUNK - sh, pltpu.roll(x, n - sh, 0), 0.0)
        else:
            x = x + jnp.where(row >= sh, pltpu.roll(x, sh, 0), 0.0)
    return x


def _hg_lower_bound(logits_ref):
    lg = logits_ref[...]
    e = jnp.exp(lg - jnp.max(lg, axis=0, keepdims=True))
    return e[0:1, :] / jnp.sum(e, axis=0, keepdims=True)


def _hg_terms(fr, q, lb, row):
    sig = jax.nn.sigmoid(fr)
    f = lb + (1.0 - lb) * sig
    kk = 1.0 - f
    g = jnp.log(f)
    G = _chunk_cumsum(g, row)
    g_last = G + (_chunk_cumsum(g, row, reverse=True) - g)
    e_g, e_ng, e_lg = jnp.exp(G), jnp.exp(-G), jnp.exp(g_last - G)
    return dict(sig=sig, f=f, kk=kk, e_g=e_g, e_ng=e_ng, e_lg=e_lg, q_dec=q * e_g, k_intra=kk * e_ng,
                k_state=kk * e_lg, decay=jnp.exp(g_last))


def _hg_causal(n):
    t = lax.broadcasted_iota(jnp.int32, (n, n), 0)
    s = lax.broadcasted_iota(jnp.int32, (n, n), 1)
    return (s <= t) & (s // HG_CHUNK == t // HG_CHUNK)


def _chunks(a):
    return a.reshape(a.shape[0] // HG_CHUNK, HG_CHUNK, a.shape[1])


def _per_chunk(lhs, rhs, contract):
    return lax.dot_general(_mx(lhs), _mx(rhs), ((contract[:1], contract[1:]), ((0,), (0,))), preferred_element_type=F32)


def _hg_specs(T, tb, col0, order):
    return [pl.BlockSpec((tb, HG_WIDTH), functools.partial(lambda i, j: (order(i), j), j=col0 + j)) for j in range(4)]


def _hg_fwd(proj, logits, norm_g, *, name):
    T = proj.shape[0]
    tb = min(HG_TOKENS, T)
    nch = tb // HG_CHUNK

    def body(q_ref, f_ref, i_ref, gate_ref, lg_ref, ng_ref, out_ref, o_ref, s_ref, st_ref, inc_ref, dec_ref):
        @pl.when(pl.program_id(0) == 0)
        def _():
            st_ref[...] = jnp.zeros_like(st_ref)

        lb_all = _hg_lower_bound(lg_ref)
        row = _chunk_row(tb)
        causal = _hg_causal(tb)
        for hh in range(HG_HEADS):
            cols = slice(hh * HG_DH, (hh + 1) * HG_DH)
            t = _hg_terms(f_ref[:, cols], q_ref[:, cols], lb_all[:, cols], row)
            v = i_ref[:, cols]
            scores = jnp.where(causal, _dot_nt(t["q_dec"], t["k_intra"]), 0.0)
            o_intra = _dot(scores, v)
            inc_ref[...] = _per_chunk(_chunks(v), _chunks(t["k_state"]), (1, 1))
            dec_ref[...] = _chunks(t["decay"])

            def step(ci, st):
                s_ref[ci, hh] = st
                return st * dec_ref[ci][0:1, :] + inc_ref[ci]

            st_ref[hh] = lax.fori_loop(0, nch, step, st_ref[hh], unroll=4)
            o_inter = _per_chunk(_chunks(t["q_dec"]), s_ref[:, hh], (2, 2))
            o = o_intra + o_inter.reshape(tb, HG_DH)
            o_ref[:, cols] = o
            gate = gate_ref[:, cols]
            on = o * lax.rsqrt(jnp.mean(o * o, axis=-1, keepdims=True) + RMS_EPS) * ng_ref[:, cols]
            out_ref[:, cols] = (on * (gate * jax.nn.sigmoid(gate))).astype(out_ref.dtype)

    blk = pl.BlockSpec((tb, HG_WIDTH), lambda i: (i, 0))
    return pl.pallas_call(
        body, grid=(T // tb,),
        in_specs=_hg_specs(T, tb, 3, lambda i: i) + [pl.BlockSpec((3, HG_WIDTH), lambda i: (0, 0)), pl.BlockSpec((1, HG_WIDTH), lambda i: (0, 0))],
        out_specs=[blk, blk, pl.BlockSpec((nch, HG_HEADS, HG_DH, HG_DH), lambda i: (i, 0, 0, 0))],
        out_shape=[jax.ShapeDtypeStruct((T, HG_WIDTH), MXU_DTYPE), jax.ShapeDtypeStruct((T, HG_WIDTH), F32),
                   jax.ShapeDtypeStruct((T // HG_CHUNK, HG_HEADS, HG_DH, HG_DH), F32)],
        scratch_shapes=[pltpu.VMEM((HG_HEADS, HG_DH, HG_DH), F32), pltpu.VMEM((nch, HG_DH, HG_DH), F32),
                        pltpu.VMEM((nch, HG_CHUNK, HG_DH), F32)],
        compiler_params=_params("arbitrary"), name=name,
    )(proj, proj, proj, proj, logits, norm_g)


def _hg_bwd(proj, o_raw, states, dmix, logits, norm_g, *, name):
    T = proj.shape[0]
    tb = min(HG_TOKENS, T)
    nch = tb // HG_CHUNK
    nb = T // tb
    rev = lambda i: nb - 1 - i

    def body(q_ref, f_ref, i_ref, gate_ref, o_ref, s_ref, dout_ref, lg_ref, ng_ref, dp_ref, dlb_ref, dng_ref,
             dst_ref, inc_ref, dec_ref, after_ref):
        @pl.when(pl.program_id(0) == 0)
        def _():
            dst_ref[...] = jnp.zeros_like(dst_ref)
            dlb_ref[...] = jnp.zeros_like(dlb_ref)
            dng_ref[...] = jnp.zeros_like(dng_ref)

        lb_all = _hg_lower_bound(lg_ref)
        row = _chunk_row(tb)
        causal = _hg_causal(tb)
        for hh in range(HG_HEADS):
            cols = slice(hh * HG_DH, (hh + 1) * HG_DH)
            out_cols = lambda part: slice(part * HG_WIDTH + hh * HG_DH, part * HG_WIDTH + (hh + 1) * HG_DH)
            o, gate, dout, ng, lb = o_ref[:, cols], gate_ref[:, cols], dout_ref[:, cols], ng_ref[:, cols], lb_all[:, cols]
            sg = jax.nn.sigmoid(gate)
            r = lax.rsqrt(jnp.mean(o * o, axis=-1, keepdims=True) + RMS_EPS)
            oh = o * r
            dp_ref[:, out_cols(3)] = (dout * (oh * ng) * (sg * (1.0 + gate * (1.0 - sg)))).astype(dp_ref.dtype)
            don = dout * (gate * sg)
            dng_ref[:, cols] += jnp.sum(don * oh, axis=0, keepdims=True)
            doh = don * ng
            do = r * (doh - oh * jnp.mean(doh * oh, axis=-1, keepdims=True))

            t = _hg_terms(f_ref[:, cols], q_ref[:, cols], lb, row)
            v = i_ref[:, cols]
            scores = jnp.where(causal, _dot_nt(t["q_dec"], t["k_intra"]), 0.0)
            dscores = jnp.where(causal, _dot_nt(do, v), 0.0)
            inc_ref[...] = _per_chunk(_chunks(do), _chunks(t["q_dec"]), (1, 1))
            dec_ref[...] = _chunks(t["decay"])

            def step(it, dst):
                ci = nch - 1 - it
                after_ref[ci] = dst
                return dst * dec_ref[ci][0:1, :] + inc_ref[ci]

            dst_ref[hh] = lax.fori_loop(0, nch, step, dst_ref[hh], unroll=4)
            st, dst = s_ref[:, hh], after_ref[...]
            dqd = _dot(dscores, t["k_intra"]) + _per_chunk(_chunks(do), st, (2, 1)).reshape(tb, HG_DH)
            dki = _dot_tn(dscores, t["q_dec"])
            dks = _per_chunk(_chunks(v), dst, (2, 1)).reshape(tb, HG_DH)
            dp_ref[:, out_cols(2)] = (_dot_tn(scores, do) + _per_chunk(_chunks(t["k_state"]), dst, (2, 2)).reshape(tb, HG_DH)).astype(dp_ref.dtype)
            ddecay = jnp.broadcast_to(jnp.sum(st * dst, axis=1, keepdims=True), (nch, HG_CHUNK, HG_DH)).reshape(tb, HG_DH)
            dks_ks = dks * t["k_state"]
            d_glast = _chunk_cumsum(dks_ks, row) + ddecay * t["decay"]
            d_g = dqd * t["q_dec"] - dki * t["k_intra"] - dks_ks + jnp.where(row == HG_CHUNK - 1, d_glast, 0.0)
            df = _chunk_cumsum(d_g, row, reverse=True) / t["f"] - (dki * t["e_ng"] + dks * t["e_lg"])
            dp_ref[:, out_cols(0)] = (dqd * t["e_g"]).astype(dp_ref.dtype)
            dp_ref[:, out_cols(1)] = (df * (1.0 - lb) * t["sig"] * (1.0 - t["sig"])).astype(dp_ref.dtype)
            dlb_ref[:, cols] += jnp.sum(df * (1.0 - t["sig"]), axis=0, keepdims=True)

    blk = pl.BlockSpec((tb, HG_WIDTH), lambda i: (rev(i), 0))
    row_spec = pl.BlockSpec((1, HG_WIDTH), lambda i: (0, 0))
    return pl.pallas_call(
        body, grid=(nb,),
        in_specs=_hg_specs(T, tb, 3, rev) + [
            blk, pl.BlockSpec((nch, HG_HEADS, HG_DH, HG_DH), lambda i: (rev(i), 0, 0, 0)),
            pl.BlockSpec((tb, HG_WIDTH), lambda i: (rev(i), 1)), pl.BlockSpec((3, HG_WIDTH), lambda i: (0, 0)), row_spec],
        out_specs=[pl.BlockSpec((tb, 4 * HG_WIDTH), lambda i: (rev(i), 0)), row_spec, row_spec],
        out_shape=[jax.ShapeDtypeStruct((T, 4 * HG_WIDTH), MXU_DTYPE), jax.ShapeDtypeStruct((1, HG_WIDTH), F32), jax.ShapeDtypeStruct((1, HG_WIDTH), F32)],
        scratch_shapes=[pltpu.VMEM((HG_HEADS, HG_DH, HG_DH), F32), pltpu.VMEM((nch, HG_DH, HG_DH), F32),
                        pltpu.VMEM((nch, HG_CHUNK, HG_DH), F32), pltpu.VMEM((nch, HG_DH, HG_DH), F32)],
        compiler_params=_params("arbitrary"), name=name,
    )(proj, proj, proj, proj, o_raw, states, dmix, logits, norm_g)


def _shifted_copies(sh_ref, n_rows):
    keep = n_rows + CONV_HALO - 8
    for b in range(1, 8):
        sh_ref[b, 0:keep, :] = sh_ref[0, b:b + keep, :]


def _tap_rows(sh_ref, offset, r0, lanes):
    start = pl.multiple_of(r0 + (offset - offset % 8), 8)
    return sh_ref[offset % 8, pl.ds(start, CONV_ROWS), lanes]


def _conv_fwd(p, w_dw, b_dw, ln_g, ln_b, *, name):
    T, D = p.shape[0], p.shape[1] // 2
    tb = _token_block(T)
    hpb = tb // CONV_HALO
    lane_step = 512

    def body(p1_ref, p2_ref, q1_ref, q2_ref, w_ref, bdw_ref, g_ref, b_ref, a_ref, y_ref, act_ref, sh_ref):
        i = pl.program_id(0)
        a = p1_ref[...] * jax.nn.sigmoid(p2_ref[...])
        sh_ref[0, 0:CONV_HALO, :] = jnp.where(i > 0, q1_ref[...] * jax.nn.sigmoid(q2_ref[...]), 0.0)
        sh_ref[0, CONV_HALO:, :] = a
        a_ref[...] = a
        _shifted_copies(sh_ref, tb)

        def chunk(ci, _):
            r0 = pl.multiple_of(ci * CONV_ROWS, CONV_ROWS)
            for l0 in range(0, D, lane_step):
                lanes = slice(l0, l0 + lane_step)
                acc = jnp.broadcast_to(bdw_ref[:, lanes], (CONV_ROWS, lane_step))
                for k in range(CONV_WIDTH):
                    acc = acc + _tap_rows(sh_ref, CONV_HALO - CONV_WIDTH + 1 + k, r0, lanes) * w_ref[k:k + 1, lanes]
                y_ref[pl.ds(r0, CONV_ROWS), lanes] = acc
            return 0

        lax.fori_loop(0, tb // CONV_ROWS, chunk, 0)
        y = y_ref[...]
        mu = jnp.mean(y, axis=-1, keepdims=True)
        yc = y - mu
        s = yc * lax.rsqrt(jnp.mean(yc * yc, axis=-1, keepdims=True) + LN_EPS) * g_ref[...] + b_ref[...]
        act_ref[...] = (s * jax.nn.sigmoid(s)).astype(act_ref.dtype)

    prev = lambda i: jnp.maximum(i * hpb - 1, 0)
    blk = pl.BlockSpec((tb, D), lambda i: (i, 0))
    row = pl.BlockSpec((1, D), lambda i: (0, 0))
    return pl.pallas_call(
        body, grid=(T // tb,),
        in_specs=[blk, pl.BlockSpec((tb, D), lambda i: (i, 1)), pl.BlockSpec((CONV_HALO, D), lambda i: (prev(i), 0)),
                  pl.BlockSpec((CONV_HALO, D), lambda i: (prev(i), 1)), pl.BlockSpec((CONV_HALO, D), lambda i: (0, 0)), row, row, row],
        out_specs=[blk, blk, blk],
        out_shape=[jax.ShapeDtypeStruct((T, D), F32), jax.ShapeDtypeStruct((T, D), F32), jax.ShapeDtypeStruct((T, D), MXU_DTYPE)],
        scratch_shapes=[pltpu.VMEM((8, tb + CONV_HALO, D), F32)],
        compiler_params=_params("parallel"), name=name,
    )(p, p, p, p, w_dw, b_dw, ln_g, ln_b)


def _conv_bwd_norm(dact, y, ln_g, ln_b, *, name):
    T, D = y.shape
    tb = _token_block(T)

    def body(da_ref, y_ref, g_ref, b_ref, dy_ref, dg_ref, db_ref, cs_ref):
        @pl.when(pl.program_id(0) == 0)
        def _():
            dg_ref[...] = jnp.zeros_like(dg_ref)
            db_ref[...] = jnp.zeros_like(db_ref)
            cs_ref[...] = jnp.zeros_like(cs_ref)

        y, g = y_ref[...], g_ref[...]
        yc = y - jnp.mean(y, axis=-1, keepdims=True)
        rs = lax.rsqrt(jnp.mean(yc * yc, axis=-1, keepdims=True) + LN_EPS)
        yn = yc * rs
        s = yn * g + b_ref[...]
        sg = jax.nn.sigmoid(s)
        ds = da_ref[...] * (sg * (1.0 + s * (1.0 - sg)))
        dg_ref[...] += jnp.sum(ds * yn, axis=0, keepdims=True)
        db_ref[...] += jnp.sum(ds, axis=0, keepdims=True)
        dyn = ds * g
        dy = rs * (dyn - jnp.mean(dyn, axis=-1, keepdims=True) - yn * jnp.mean(dyn * yn, axis=-1, keepdims=True))
        dy_ref[...] = dy
        cs_ref[...] += jnp.sum(dy, axis=0, keepdims=True)

    blk = pl.BlockSpec((tb, D), lambda i: (i, 0))
    row = pl.BlockSpec((1, D), lambda i: (0, 0))
    rs_ = jax.ShapeDtypeStruct((1, D), F32)
    return pl.pallas_call(
        body, grid=(T // tb,), in_specs=[blk, blk, row, row], out_specs=[blk, row, row, row],
        out_shape=[jax.ShapeDtypeStruct((T, D), F32), rs_, rs_, rs_], compiler_params=_params("arbitrary"), name=name,
    )(dact, y, ln_g, ln_b)


def _conv_bwd_taps(dy, a, p, w_dw, *, name):
    T, D = dy.shape
    tb = _token_block(T)
    hpb = tb // CONV_HALO
    last = T // CONV_HALO - 1
    nb = T // tb
    lane_step = 128
    groups = CONV_ROWS // 8

    def body(dy_ref, dyn_ref, a_ref, p1_ref, p2_ref, w_ref, dp_ref, dw_ref, cs_ref, sh_ref, da_ref):
        i = pl.program_id(0)

        @pl.when(i == 0)
        def _():
            dw_ref[...] = jnp.zeros_like(dw_ref)
            cs_ref[...] = jnp.zeros_like(cs_ref)

        sh_ref[0, 0:tb, :] = dy_ref[...]
        sh_ref[0, tb:, :] = jnp.where(i < nb - 1, dyn_ref[...], 0.0)
        _shifted_copies(sh_ref, tb)
        for l0 in range(0, D, lane_step):
            lanes = slice(l0, l0 + lane_step)
            for taps in (range(0, CONV_WIDTH // 2 + 1), range(CONV_WIDTH // 2 + 1, CONV_WIDTH)):
                def chunk(ci, sums, taps=taps):
                    r0 = pl.multiple_of(ci * CONV_ROWS, CONV_ROWS)
                    a_c = a_ref[pl.ds(r0, CONV_ROWS), lanes]
                    da = jnp.zeros((CONV_ROWS, lane_step), F32) if taps[0] == 0 else da_ref[pl.ds(r0, CONV_ROWS), lanes]
                    new = []
                    for n, k in enumerate(taps):
                        s_k = _tap_rows(sh_ref, CONV_WIDTH - 1 - k, r0, lanes)
                        da = da + s_k * w_ref[k:k + 1, lanes]
                        new.append(sums[n] + jnp.sum((s_k * a_c).reshape(groups, 8, lane_step), axis=0))
                    da_ref[pl.ds(r0, CONV_ROWS), lanes] = da
                    return tuple(new)

                sums = lax.fori_loop(0, tb // CONV_ROWS, chunk, tuple(jnp.zeros((8, lane_step), F32) for _ in taps))
                for n, k in enumerate(taps):
                    dw_ref[k:k + 1, lanes] += jnp.sum(sums[n], axis=0, keepdims=True)
        da = da_ref[...]
        p1 = p1_ref[...]
        sg = jax.nn.sigmoid(p2_ref[...])
        dp1 = da * sg
        dp2 = da * p1 * (sg * (1.0 - sg))
        dp_ref[:, 0:D] = dp1.astype(dp_ref.dtype)
        dp_ref[:, D:] = dp2.astype(dp_ref.dtype)
        cs_ref[:, 0:D] += jnp.sum(dp1, axis=0, keepdims=True)
        cs_ref[:, D:] += jnp.sum(dp2, axis=0, keepdims=True)

    blk = pl.BlockSpec((tb, D), lambda i: (i, 0))
    return pl.pallas_call(
        body, grid=(nb,),
        in_specs=[blk, pl.BlockSpec((CONV_HALO, D), lambda i: (jnp.minimum((i + 1) * hpb, last), 0)), blk, blk,
                  pl.BlockSpec((tb, D), lambda i: (i, 1)), pl.BlockSpec((CONV_HALO, D), lambda i: (0, 0))],
        out_specs=[pl.BlockSpec((tb, 2 * D), lambda i: (i, 0)), pl.BlockSpec((CONV_HALO, D), lambda i: (0, 0)), pl.BlockSpec((1, 2 * D), lambda i: (0, 0))],
        out_shape=[jax.ShapeDtypeStruct((T, 2 * D), MXU_DTYPE), jax.ShapeDtypeStruct((CONV_HALO, D), F32), jax.ShapeDtypeStruct((1, 2 * D), F32)],
        scratch_shapes=[pltpu.VMEM((8, tb + CONV_HALO, D), F32), pltpu.VMEM((tb, D), F32)],
        compiler_params=_params("arbitrary"), name=name,
    )(dy, dy, a, p, p, w_dw)


def _row_block(rows):
    for tr in (512, 256, 128, 64, 32, 16, 8):
        if rows % tr == 0:
            return tr
    return rows


def _sum_leading(x, *, name):
    n, R, C = x.shape
    tr = _row_block(R)

    def body(x_ref, o_ref):
        acc = x_ref[0]
        for j in range(1, n):
            acc = acc + x_ref[j]
        o_ref[...] = acc

    return pl.pallas_call(
        body, grid=(R // tr,), in_specs=[pl.BlockSpec((n, tr, C), lambda i: (0, i, 0))], out_specs=pl.BlockSpec((tr, C), lambda i: (i, 0)),
        out_shape=jax.ShapeDtypeStruct((R, C), x.dtype), compiler_params=_params("parallel"), name=name,
    )(x)


def _add_pair(x, y, *, name):
    n, R, C = x.shape
    tr = _row_block(R)

    def body(x_ref, y_ref, o_ref):
        o_ref[...] = x_ref[...] + y_ref[...]

    blk = pl.BlockSpec((1, tr, C), lambda j, i: (j, i, 0))
    return pl.pallas_call(
        body, grid=(n, R // tr), in_specs=[blk, blk], out_specs=blk,
        out_shape=jax.ShapeDtypeStruct((n, R, C), x.dtype), compiler_params=_params("parallel", "parallel"), name=name,
    )(x, y)


def _adamw(w, g, m, v, *, name):
    R, C = w.shape
    tr = _row_block(R)
    c1, c2 = 1.0 - ADAM_B1 ** ADAM_STEP, 1.0 - ADAM_B2 ** ADAM_STEP

    def body(w_ref, g_ref, m_ref, v_ref, d_ref, nm_ref, nv_ref):
        g_ = g_ref[...]
        nm = ADAM_B1 * m_ref[...] + (1.0 - ADAM_B1) * g_
        nv = ADAM_B2 * v_ref[...] + (1.0 - ADAM_B2) * (g_ * g_)
        d_ref[...] = -ADAM_LR * ((nm / c1) / (jnp.sqrt(nv / c2) + ADAM_EPS) + ADAM_WD * w_ref[...])
        nm_ref[...] = nm
        nv_ref[...] = nv

    blk = pl.BlockSpec((tr, C), lambda i: (i, 0))
    shp = jax.ShapeDtypeStruct((R, C), F32)
    return pl.pallas_call(
        body, grid=(R // tr,), in_specs=[blk] * 4, out_specs=[blk] * 3, out_shape=[shp] * 3,
        compiler_params=_params("parallel"), name=name,
    )(w, g, m, v)


def _small_reduce(packs, logits, *, name):
    n, R, C = packs.shape

    def body(p_ref, lg_ref, s_ref, dlg_ref):
        acc = p_ref[0]
        for j in range(1, n):
            acc = acc + p_ref[j]
        s_ref[...] = acc
        lg = lg_ref[...]
        e = jnp.exp(lg - jnp.max(lg, axis=0, keepdims=True))
        sm = e / jnp.sum(e, axis=0, keepdims=True)
        dlb = acc[5:6, HG_WIDTH:2 * HG_WIDTH]
        first = lax.broadcasted_iota(jnp.int32, sm.shape, 0) == 0
        dlg_ref[...] = sm[0:1, :] * (jnp.where(first, 1.0, 0.0) - sm) * dlb

    whole = lambda shape: pl.BlockSpec(shape, lambda: (0,) * len(shape))
    return pl.pallas_call(
        body, in_specs=[whole((n, R, C)), whole(logits.shape)], out_specs=[whole((R, C)), whole(logits.shape)],
        out_shape=[jax.ShapeDtypeStruct((R, C), F32), jax.ShapeDtypeStruct(logits.shape, F32)],
        compiler_params=pltpu.CompilerParams(vmem_limit_bytes=VMEM_LIMIT), name=name,
    )(packs, logits)


HBM_SPEC = pl.BlockSpec(memory_space=pl.ANY)


def _place():
    return lax.axis_index("x"), lax.axis_index("y"), lax.axis_index("c")


class _Copies:
    def __init__(self, arrays, out_shapes, n_copies, make, finish):
        self.arrays, self.out_shapes, self.n_copies, self.make, self.finish = list(arrays), list(out_shapes), n_copies, make, finish

    def scratch(self):
        return [pltpu.SemaphoreType.DMA((self.n_copies,)), pltpu.SemaphoreType.DMA((self.n_copies,))]

    def run(self, name):
        n = len(self.arrays)

        def body(*refs):
            copies = self.make(refs[:n], refs[n:2 * n], *refs[2 * n:])
            for cp in copies:
                cp.start()
            for cp in copies:
                cp.wait()

        outs = pl.pallas_call(body, in_specs=[HBM_SPEC] * n, out_specs=[HBM_SPEC] * n, out_shape=self.out_shapes,
                              scratch_shapes=self.scratch(), name=name)(*self.arrays)
        return self.finish(outs)


def _remote(src, dst, send_sems, recv_sems, k, peer):
    return pltpu.make_async_remote_copy(src_ref=src, dst_ref=dst, send_sem=send_sems.at[k], recv_sem=recv_sems.at[k],
                                        device_id=peer, device_id_type=MESH)


def _same_core_peers(x, y, c):
    return [(1 - x, y, c), (x, 1 - y, c), (1 - x, 1 - y, c)]


def _all_peers(x, y, c):
    flip = lambda v, b: 1 - v if b else v
    return [(flip(x, r & 4), flip(y, r & 2), flip(c, r & 1)) for r in range(1, 8)]


def _gather(arrays, peers_of, slot_of, n_slots):
    n_peers = len(peers_of(0, 0, 0))

    def make(ins, outs, send_sems, recv_sems):
        x, y, c = _place()
        slot = slot_of(x, y, c)
        return [_remote(ins[a], outs[a].at[slot], send_sems, recv_sems, a * n_peers + k, peer)
                for a in range(len(arrays)) for k, peer in enumerate(peers_of(x, y, c))]

    def finish(outs):
        slot = slot_of(*_place())
        return [lax.dynamic_update_index_in_dim(o, a, slot, 0) for o, a in zip(outs, arrays)]

    shapes = [jax.ShapeDtypeStruct((n_slots,) + a.shape, a.dtype) for a in arrays]
    return _Copies(arrays, shapes, len(arrays) * n_peers, make, finish)


def _gather_chips(arrays):
    return _gather(arrays, _same_core_peers, lambda x, y, c: 2 * x + y, N_CHIPS)


def _gather_all(arrays):
    return _gather(arrays, _all_peers, lambda x, y, c: 4 * x + 2 * y + c, N_DEV)


def _pair_swap(a):
    def make(ins, outs, send_sems, recv_sems):
        x, y, c = _place()
        return [_remote(ins[0].at[1 - c], outs[0], send_sems, recv_sems, 0, (x, y, 1 - c))]

    return _Copies([a], [jax.ShapeDtypeStruct(a.shape[1:], a.dtype)], 1, make, lambda outs: outs[0])


def _chip_scatter(p):
    def make(ins, outs, send_sems, recv_sems):
        x, y, c = _place()
        return [_remote(ins[0].at[2 * px + py], outs[0].at[2 * x + y], send_sems, recv_sems, k, (px, py, pc))
                for k, (px, py, pc) in enumerate(_same_core_peers(x, y, c))]

    def finish(outs):
        x, y, _ = _place()
        me = 2 * x + y
        return lax.dynamic_update_index_in_dim(outs[0], lax.dynamic_index_in_dim(p, me, 0, keepdims=False), me, 0)

    return _Copies([p], [jax.ShapeDtypeStruct(p.shape, p.dtype)], 3, make, finish)


def _owner_scatter(blocks):
    def make(ins, outs, send_sems, recv_sems):
        x, y, c = _place()
        return [_remote(ins[0].at[2 * px + py, pc], outs[0].at[4 * x + 2 * y + c], send_sems, recv_sems, k, (px, py, pc))
                for k, (px, py, pc) in enumerate(_all_peers(x, y, c))]

    def finish(outs):
        x, y, c = _place()
        mine = lax.dynamic_index_in_dim(lax.dynamic_index_in_dim(blocks, 2 * x + y, 0, keepdims=False), c, 0, keepdims=False)
        return lax.dynamic_update_index_in_dim(outs[0], mine, 4 * x + 2 * y + c, 0)

    return _Copies([blocks], [jax.ShapeDtypeStruct((N_DEV,) + blocks.shape[2:], blocks.dtype)], N_DEV - 1, make, finish)


def _pair_gather(q):
    def make(ins, outs, send_sems, recv_sems):
        x, y, c = _place()
        return [_remote(ins[0], outs[0].at[c], send_sems, recv_sems, 0, (x, y, 1 - c))]

    return _Copies([q], [jax.ShapeDtypeStruct((2,) + q.shape, q.dtype)], 1, make,
                   lambda outs: lax.dynamic_update_index_in_dim(outs[0], q, _place()[2], 0))


def _grad_blocks(dw, kind):
    if kind == "cols2d":
        K, N = dw.shape
        b = dw.reshape(2, K // 2, N_CHIPS, N // N_CHIPS).transpose(2, 0, 1, 3)
    elif kind == "rows2d":
        b = dw.reshape(N_CHIPS, 2, dw.shape[0] // 8, dw.shape[1])
    elif kind == "cols3d":
        L, K, N = dw.shape
        b = dw.reshape(L, K, N_CHIPS, N // N_CHIPS).transpose(2, 0, 1, 3)
    else:
        L, K, N = dw.shape
        b = dw.reshape(L, N_CHIPS, K // N_CHIPS, N).transpose(1, 0, 2, 3)
    return b.reshape(N_CHIPS, 2, -1, D_MODEL)


def _pad_rows(a, rows):
    return jnp.concatenate([a, jnp.zeros((rows - a.shape[0],) + a.shape[1:], a.dtype)], axis=0)


def _forward_backward(x, target, W, late_weights=None, early_grads=None, last_grads=None):
    row = lambda a: a.reshape(1, -1)
    relu2 = lambda acc: (jnp.square(jnp.maximum(acc, 0.0)),)
    normed = lambda h, g: h * lax.rsqrt(jnp.mean(h * h, axis=-1, keepdims=True) + RMS_EPS) * g

    def residual_norm(acc, res, g):
        h = res + acc
        return h, normed(h, g)

    def norm_bwd(du, h_blk, dres_blk, g):
        dx, dg_terms = _rms_bwd_math(h_blk, g, du)
        dh = dres_blk + dx
        return dh, dh, jnp.sum(dg_terms, axis=0, keepdims=True), jnp.sum(dh, axis=0, keepdims=True)

    def matmul_norm_bwd(dy, w, h_in, g, dres, rider=None, *, tk=1024, name):
        return _matmul(dy, w, mode="nt", out_dtypes=[F32, MXU_DTYPE], epilogue=norm_bwd, tiles=[h_in, dres], rows=[row(g)],
                       n_sums=2, rider=rider, tk=tk, name=name)

    G = {}

    u0 = _rmsnorm_fwd(x, row(W["norm_mix_g"][0]), name="norm_mix0")
    proj, qkv = _matmul(u0, W["w_in"], mode="nn", out_dtypes=[F32, MXU_DTYPE], epilogue=lambda acc: (acc, acc), tn=896, name="in_proj")
    o_sb, got = _sb_fwd(qkv, late_weights and late_weights[0], name="sb_fwd")
    if late_weights:
        W = {**W, **late_weights[1](got)}
    hg_out, hg_o, hg_states = _hg_fwd(proj, W["hg_lb_logits"], row(W["hg_norm_g"]), name="hg_fwd")
    mix = jnp.concatenate([o_sb, hg_out], axis=-1)
    h1, u1 = _matmul(mix, W["w_out"], mode="nn", out_dtypes=[F32, MXU_DTYPE], epilogue=residual_norm, tiles=[x],
                     rows=[row(W["norm_ffn_g"][0])], name="out_proj")
    r0 = _matmul(u1, W["w_ff1"][0], mode="nn", out_dtypes=[MXU_DTYPE], epilogue=relu2, name="ff1_0")
    h2, u2 = _matmul(r0, W["w_ff2"][0], mode="nn", out_dtypes=[F32, MXU_DTYPE], epilogue=residual_norm, tiles=[h1],
                     rows=[row(W["norm_mix_g"][1])], name="ff2_0")
    p = _matmul(u2, W["w_glu"], mode="nn", out_dtypes=[F32], epilogue=lambda acc, b: (acc + b,), rows=[row(W["b_glu"])], name="glu_proj")
    w_dw = _pad_rows(W["w_dw"], CONV_HALO)
    ca, cy, cact = _conv_fwd(p, w_dw, row(W["b_dw"]), row(W["ln_g"]), row(W["ln_b"]), name="conv_fwd")
    h3, u3 = _matmul(cact, W["w_pw"], mode="nn", out_dtypes=[F32, MXU_DTYPE], epilogue=lambda acc, res, b, g: residual_norm(acc + b, res, g),
                     tiles=[h2], rows=[row(W["b_pw"]), row(W["norm_ffn_g"][1])], name="pw_proj")
    r1 = _matmul(u3, W["w_ff1"][1], mode="nn", out_dtypes=[MXU_DTYPE], epilogue=relu2, name="ff1_1")
    h4 = _matmul(r1, W["w_ff2"][1], mode="nn", out_dtypes=[F32], epilogue=lambda acc, res: (res + acc,), tiles=[h3], name="ff2_1")

    dh4, dh4_m, G["final_norm_g"], loss = _loss_head(h4, row(W["final_norm_g"]), target, name="loss_head")

    def mlp_bwd(dh, dh_m, h_in, u, r, layer, tag):
        d_relu2 = lambda acc, r_blk: (acc * (2.0 * jnp.sqrt(r_blk.astype(F32))),)
        da = _matmul(dh_m, W["w_ff2"][layer], mode="nt", out_dtypes=[MXU_DTYPE], epilogue=d_relu2, tiles=[r], name="d_ff2_act" + tag)
        dw2 = _matmul(r, dh_m, mode="tn", out_dtypes=[F32], tk=WGRAD_TOKENS, name="d_ff2_w" + tag)
        dw1 = _matmul(u, da, mode="tn", out_dtypes=[F32], tk=WGRAD_TOKENS, name="d_ff1_w" + tag)
        dh_in, dh_in_m, dg, cs = matmul_norm_bwd(da, W["w_ff1"][layer], h_in, W["norm_ffn_g"][layer], dh, name="d_ff1_act" + tag)
        return dh_in, dh_in_m, dg, cs, dw1, dw2

    dh3, dh3_m, dg_ffn1, cs_h3, dw1_1, dw2_1 = mlp_bwd(dh4, dh4_m, h3, u3, r1, 1, "1")
    G["b_pw"] = cs_h3
    dact = _matmul(dh3_m, W["w_pw"], mode="nt", out_dtypes=[F32], name="d_pw_act")
    G["w_pw"] = _matmul(cact, dh3_m, mode="tn", out_dtypes=[F32], tk=WGRAD_TOKENS, name="d_pw_w")
    dy, G["ln_g"], G["ln_b"], G["b_dw"] = _conv_bwd_norm(dact, cy, row(W["ln_g"]), row(W["ln_b"]), name="d_conv_norm")
    dp, G["w_dw"], G["b_glu"] = _conv_bwd_taps(dy, ca, p, w_dw, name="d_conv_taps")
    G["w_glu"] = _matmul(u2, dp, mode="tn", out_dtypes=[F32], tk=WGRAD_TOKENS, name="d_glu_w")
    dh2, dh2_m, dg_mix1, _ = matmul_norm_bwd(dp, W["w_glu"], h2, W["norm_mix_g"][1], dh3, name="d_glu_act")
    dh1, dh1_m, dg_ffn0, _, dw1_0, dw2_0 = mlp_bwd(dh2, dh2_m, h1, u1, r0, 0, "0")
    G["w_ff1"], G["w_ff2"] = jnp.stack([dw1_0, dw1_1]), jnp.stack([dw2_0, dw2_1])
    G["norm_ffn_g"] = jnp.concatenate([dg_ffn0, dg_ffn1], axis=0)
    dmix = _matmul(dh1_m, W["w_out"], mode="nt", out_dtypes=[F32], name="d_out_act")
    G["w_out"] = _matmul(mix, dh1_m, mode="tn", out_dtypes=[F32], tk=WGRAD_TOKENS, name="d_out_w")
    riding = early_grads(G) if early_grads else None
    (dsq, dsk, dsv), got = _sb_bwd(qkv, dmix, riding, name="sb_bwd")
    d_hg, G["hg_lb"], G["hg_norm_g"] = _hg_bwd(proj, hg_o, hg_states, dmix, W["hg_lb_logits"], row(W["hg_norm_g"]), name="hg_bwd")
    dproj = jnp.concatenate([dsq, _mx(dsk), _mx(dsv), d_hg], axis=-1)
    G["w_in"] = _matmul(u0, dproj, mode="tn", out_dtypes=[F32], tn=896, tk=WGRAD_TOKENS, name="d_in_w")
    last = last_grads(G) if last_grads else None
    res = matmul_norm_bwd(dproj, W["w_in"], x, W["norm_mix_g"][0], dh1, last, tk=896, name="d_in_act")
    (dx, _, dg_mix0, _), got_last = res if last_grads else (res, [])
    G["norm_mix_g"] = jnp.concatenate([dg_mix0, dg_mix1], axis=0)
    return loss, dx, G, [(riding, got), (last, got_last)]


BIG = (("w_out_ab", "w_out", "rows2d"), ("conv_w_glu", "w_glu", "cols2d"), ("conv_w_pw", "w_pw", "rows2d"),
       ("w_ff1", "w_ff1", "cols3d"), ("w_ff2", "w_ff2", "rows3d"), ("w_in_ab", "w_in", "cols2d"))
LATE = BIG[:-1]
SMALL_SHARDED = ("conv_b_glu", "conv_w_dw", "conv_b_dw", "conv_ln_g", "conv_ln_b", "conv_b_pw")
REPLICATED = ("norm_mix_g", "norm_ffn_g", "hg_lb_logits", "hg_norm_g", "final_norm_g")
ORDER = ("norm_mix_g", "norm_ffn_g", "w_in_ab", "w_out_ab", "hg_lb_logits", "hg_norm_g", "conv_w_glu", "conv_b_glu",
         "conv_w_dw", "conv_b_dw", "conv_ln_g", "conv_ln_b", "conv_w_pw", "conv_b_pw", "w_ff1", "w_ff2", "final_norm_g")


def _step(x, loss_target, w, m, v):
    D = D_MODEL
    x2, t2 = x.reshape(-1, D), loss_target.reshape(-1, D)
    chip = 2 * lax.axis_index("x") + lax.axis_index("y")
    c = lax.axis_index("c")

    small_in = jnp.concatenate([w["conv_b_glu"].reshape(2, SHARD), w["conv_w_dw"].reshape(CONV_WIDTH, SHARD)] +
                               [w[n].reshape(1, SHARD) for n in ("conv_b_dw", "conv_ln_g", "conv_ln_b", "conv_b_pw")], axis=0)
    g_in, gs = _gather_chips([w["w_in_ab"].astype(MXU_DTYPE), _pad_rows(small_in, SMALL_IN_ROWS)]).run("gather_first_weights")
    vec = lambda r0, r1: gs[:, r0:r1].transpose(1, 0, 2).reshape(r1 - r0, D)
    W = {
        "w_in": _ChipWeight(g_in[:, 0], "cols"),
        "b_glu": gs[:, 0:2].reshape(2 * D), "w_dw": vec(2, 33), "b_dw": vec(33, 34)[0], "ln_g": vec(34, 35)[0],
        "ln_b": vec(35, 36)[0], "b_pw": vec(36, 37)[0],
        "norm_mix_g": w["norm_mix_g"], "norm_ffn_g": w["norm_ffn_g"], "hg_lb_logits": w["hg_lb_logits"],
        "hg_norm_g": w["hg_norm_g"], "final_norm_g": w["final_norm_g"],
    }
    late = _gather_chips([w[n].astype(MXU_DTYPE) for n, _, _ in LATE])

    def assemble(got):
        gw = dict(zip([s for _, s, _ in LATE], late.finish(got)))
        layers = lambda g, along: [_ChipWeight(g, along, (layer,)) for layer in range(2)]
        return {"w_out": gw["w_out"].reshape(D, D), "w_glu": _ChipWeight(gw["w_glu"][:, 0], "cols"), "w_pw": gw["w_pw"].reshape(D, D),
                "w_ff1": layers(gw["w_ff1"], "cols"), "w_ff2": layers(gw["w_ff2"], "rows")}

    def early_grads(G):
        return _owner_scatter(jnp.concatenate([_grad_blocks(G[s], kind) for _, s, kind in LATE], axis=2))

    def last_grads(G):
        blocks = _grad_blocks(G["w_in"], "cols2d")
        from_pair = _pair_swap(blocks.transpose(1, 0, 2, 3)).run("grads_pair_swap_in")
        return _chip_scatter(_add_pair(lax.dynamic_index_in_dim(blocks, c, axis=1, keepdims=False), from_pair, name="grads_pair_add_in"))

    loss, dx, G, riders = _forward_backward(x2, t2, W, (late, assemble), early_grads, last_grads)
    halves = [_sum_leading(copies.finish(got), name="grads_add_" + tag) for (copies, got), tag in zip(riders, ("late", "in"))]
    half = jnp.concatenate(halves, axis=0)
    full = _pair_gather(half).run("grads_pair_gather")

    pack = jnp.concatenate([
        G["norm_mix_g"], G["norm_ffn_g"], G["final_norm_g"], jnp.concatenate([G["hg_norm_g"], G["hg_lb"]], axis=1),
        _pad_rows(jnp.broadcast_to(loss, (1, D)), 2), G["b_glu"].reshape(2, D), G["w_dw"], G["b_dw"], G["ln_g"], G["ln_b"], G["b_pw"],
    ], axis=0)
    pack = _pad_rows(pack, SMALL_ROWS)
    (packs,) = _gather_all([pack]).run("gather_small_grads")
    ssum, d_logits = _small_reduce(packs, w["hg_lb_logits"], name="reduce_small_grads")
    cut = lambda r0, r1: lax.dynamic_slice(ssum, (r0, chip * SHARD), (r1 - r0, SHARD))
    grads = {
        "norm_mix_g": ssum[0:2], "norm_ffn_g": ssum[2:4], "final_norm_g": ssum[4], "hg_norm_g": ssum[5, :HG_WIDTH].reshape(1, HG_HEADS, HG_DH),
        "hg_lb_logits": d_logits,
        "conv_b_glu": lax.dynamic_slice(ssum[8:10].reshape(1, 2 * D), (0, chip * 2 * SHARD), (1, 2 * SHARD)),
        "conv_w_dw": cut(10, 10 + CONV_WIDTH).reshape(1, CONV_WIDTH, SHARD),
        "conv_b_dw": cut(42, 43), "conv_ln_g": cut(43, 44), "conv_ln_b": cut(44, 45), "conv_b_pw": cut(45, 46),
    }
    loss_out = ssum[6, 0]

    off = 0
    for n, s, kind in BIG:
        shard = w[n].shape
        rows = w[n].size // (2 * D)
        grads[n] = full[:, off:off + rows].reshape(shard)
        off += rows

    delta, new_m, new_v = {}, {}, {}
    for n, _, _ in BIG:
        view = lambda a: a.reshape(-1, a.shape[-1])
        outs = _adamw(view(w[n]), view(grads[n]), view(m[n]), view(v[n]), name="adamw_" + n)
        delta[n], new_m[n], new_v[n] = (o.reshape(w[n].shape) for o in outs)
    small = SMALL_SHARDED + REPLICATED
    sizes = [w[n].size for n in small]
    total = sum(sizes)
    rows = -(-total // (8 * D)) * 8
    packed = lambda d: _pad_rows(jnp.concatenate([d[n].reshape(-1) for n in small]).reshape(-1, 128), rows * 8).reshape(rows, D)
    outs = _adamw(packed(w), packed(grads), packed(m), packed(v), name="adamw_small")
    off = 0
    for n, size in zip(small, sizes):
        delta[n], new_m[n], new_v[n] = (o.reshape(-1)[off:off + size].reshape(w[n].shape) for o in outs)
        off += size
    grads = {n: grads[n].reshape(w[n].shape) for n in ORDER}
    return (loss_out, dx.reshape(x.shape), *[grads[n] for n in ORDER], *[delta[n] for n in ORDER],
            *[new_m[n] for n in ORDER], *[new_v[n] for n in ORDER])


def kernel(x, norm_mix_g, norm_ffn_g, w_in_ab, w_out_ab, hg_lb_logits, hg_norm_g, conv_w_glu, conv_b_glu, conv_w_dw, conv_b_dw, conv_ln_g, conv_ln_b, conv_w_pw, conv_b_pw, w_ff1, w_ff2, final_norm_g, loss_target, m_norm_mix_g, m_norm_ffn_g, m_w_in_ab, m_w_out_ab, m_hg_lb_logits, m_hg_norm_g, m_conv_w_glu, m_conv_b_glu, m_conv_w_dw, m_conv_b_dw, m_conv_ln_g, m_conv_ln_b, m_conv_w_pw, m_conv_b_pw, m_w_ff1, m_w_ff2, m_final_norm_g, v_norm_mix_g, v_norm_ffn_g, v_w_in_ab, v_w_out_ab, v_hg_lb_logits, v_hg_norm_g, v_conv_w_glu, v_conv_b_glu, v_conv_w_dw, v_conv_b_dw, v_conv_ln_g, v_conv_ln_b, v_conv_w_pw, v_conv_b_pw, v_w_ff1, v_w_ff2, v_final_norm_g):
    args = locals()
    w = {n: args[n] for n in ORDER}
    m = {n: args["m_" + n] for n in ORDER}
    v = {n: args["v_" + n] for n in ORDER}
    return _step(x, loss_target, w, m, v)
```

```python
import functools

import jax
import jax.numpy as jnp
from jax import lax
from jax.experimental import pallas as pl
from jax.experimental.pallas import tpu as pltpu

F32 = jnp.float32
MXU_DTYPE = jnp.bfloat16
MESH = pl.DeviceIdType.MESH

D_MODEL = 1024
SB_HEADS, SB_DH, SB_WIDTH = 8, 64, 512
SB_KEYS = 512
SB_SUB = 256
SB_ROWS_FWD, SB_ROWS_BWD = 512, 256
HG_HEADS, HG_DH, HG_WIDTH = 4, 128, 512
HG_CHUNK = 16
HG_TOKENS = 256
CONV_WIDTH = 31
CONV_HALO = 32
CONV_ROWS = 32
RMS_EPS = 1e-6
LN_EPS = 1e-5
N_CHIPS = 4
N_DEV = 8
SHARD = D_MODEL // N_CHIPS
SMALL_IN_ROWS = 40
SMALL_ROWS = 48
WGRAD_TOKENS = 2048
VMEM_LIMIT = 56 * 1024 * 1024

ADAM_LR, ADAM_B1, ADAM_B2, ADAM_EPS, ADAM_WD, ADAM_STEP = 0.001, 0.9, 0.999, 1e-08, 0.01, 10


def _params(*sem):
    return pltpu.CompilerParams(dimension_semantics=sem, vmem_limit_bytes=VMEM_LIMIT)


def _mx(v):
    return v.astype(MXU_DTYPE)


def _dot(a, b):
    return jnp.dot(_mx(a), _mx(b), preferred_element_type=F32)


def _dot_nt(a, b):
    return lax.dot_general(_mx(a), _mx(b), (((1,), (1,)), ((), ())), preferred_element_type=F32)


def _dot_tn(a, b):
    return lax.dot_general(_mx(a), _mx(b), (((0,), (0,)), ((), ())), preferred_element_type=F32)


def _neg_abs(x):
    bits = lax.bitcast_convert_type(x, jnp.uint32) | jnp.uint32(0x80000000)
    return lax.bitcast_convert_type(bits, F32)


def _key_order_sums(v, tri2, later):
    hi = _mx(v)
    lo = _mx(v - hi.astype(F32))
    n = v.shape[1] // SB_SUB
    blocks = [slice(b * SB_SUB, (b + 1) * SB_SUB) for b in range(n)]
    totals = [jnp.sum(v[:, sl], axis=1, keepdims=True) for sl in blocks]
    far, running = [None] * n, None
    for b in (reversed(range(n)) if later else range(n)):
        far[b] = running
        running = totals[b] if running is None else running + totals[b]
    sums = []
    for b, sl in enumerate(blocks):
        inside = jnp.dot(jnp.concatenate([hi[:, sl], lo[:, sl]], axis=1), tri2, preferred_element_type=F32)
        sums.append(inside if far[b] is None else inside + far[b])
    return jnp.concatenate(sums, axis=1), running


class _ChipWeight:
    def __init__(self, parts, along, lead=()):
        self.parts, self.along, self.lead = parts, along, tuple(lead)
        r, c = parts.shape[-2:]
        self.shape = (r, N_CHIPS * c) if along == "cols" else (N_CHIPS * r, c)

    def _gathered_is_n(self, mode):
        return (self.along == "cols") == (mode in ("nn", "tn"))

    def tile(self, mode, tn, tk):
        r, c = self.parts.shape[-2:]
        part = c if self.along == "cols" else r
        return (part, tk) if self._gathered_is_n(mode) else (tn, part)

    def spec(self, mode, tn, tk):
        squeezed = (None,) * (1 + len(self.lead))
        lead, cols, by_n = self.lead, self.along == "cols", self._gathered_is_n(mode)
        block = (tn, tk) if mode == "nt" else (tk, tn)

        def index(i, j, k):
            chip, other = (j, k) if by_n else (k, j)
            return (chip,) + lead + ((other, 0) if cols else (0, other))

        return pl.BlockSpec(squeezed + block, index)


def _matmul(a, b, *, mode, out_dtypes, epilogue=None, tiles=(), rows=(), n_sums=0, rider=None, tm=1024, tn=1024, tk=1024, name):
    b_shape = b.shape
    if mode == "nn":
        (M, K), N = a.shape, b_shape[1]
    elif mode == "nt":
        (M, K), N = a.shape, b_shape[0]
    else:
        (K, M), N = a.shape, b_shape[1]
    if isinstance(b, _ChipWeight):
        tn, tk = b.tile(mode, tn, tk)
    tm, tn, tk = min(tm, M), min(tn, N), min(tk, K)
    assert M % tm == 0 and N % tn == 0 and K % tk == 0, (name, M, N, K)
    nk = K // tk
    a_spec = pl.BlockSpec((tk, tm), lambda i, j, k: (k, i)) if mode == "tn" else pl.BlockSpec((tm, tk), lambda i, j, k: (i, k))
    if isinstance(b, _ChipWeight):
        b_spec, b = b.spec(mode, tn, tk), b.parts
    else:
        b_spec = pl.BlockSpec((tn, tk), lambda i, j, k: (j, k)) if mode == "nt" else pl.BlockSpec((tk, tn), lambda i, j, k: (k, j))
    dims = {"nn": ((1,), (0,)), "nt": ((1,), (1,)), "tn": ((0,), (0,))}[mode]
    n_t, n_r, n_o = len(tiles), len(rows), len(out_dtypes)
    n_x = 0 if rider is None else len(rider.arrays)
    grid = (M // tm, N // tn, nk)
    if epilogue is None:
        epilogue = lambda acc: (acc,)

    def body(a_ref, b_ref, *rest):
        extra, x_in, rest = rest[:n_t + n_r], rest[n_t + n_r:n_t + n_r + n_x], rest[n_t + n_r + n_x:]
        outs, sums, x_out, acc_ref, sems = rest[:n_o], rest[n_o:n_o + n_sums], rest[n_o + n_sums:n_o + n_sums + n_x], rest[n_o + n_sums + n_x], rest[n_o + n_sums + n_x + 1:]
        i, j, k = (pl.program_id(d) for d in range(3))
        if rider is not None:
            @pl.when((i == 0) & (j == 0) & (k == 0))
            def _():
                for cp in rider.make(x_in, x_out, *sems):
                    cp.start()

        @pl.when(k == 0)
        def _():
            acc_ref[...] = jnp.zeros_like(acc_ref)

        acc_ref[...] += lax.dot_general(_mx(a_ref[...]), _mx(b_ref[...]), (dims, ((), ())), preferred_element_type=F32)

        @pl.when(k == nk - 1)
        def _():
            res = epilogue(acc_ref[...], *[e[...] for e in extra])
            for o_ref, r in zip(outs, res[:n_o]):
                o_ref[...] = r.astype(o_ref.dtype)
            for s_ref, r in zip(sums, res[n_o:]):
                @pl.when(i == 0)
                def _():
                    s_ref[...] = jnp.zeros_like(s_ref)

                s_ref[...] += r

        if rider is not None:
            @pl.when((i == grid[0] - 1) & (j == grid[1] - 1) & (k == grid[2] - 1))
            def _():
                for cp in rider.make(x_in, x_out, *sems):
                    cp.wait()

    tile_spec = pl.BlockSpec((tm, tn), lambda i, j, k: (i, j))
    row_spec = pl.BlockSpec((1, tn), lambda i, j, k: (0, j))
    ordered = n_sums > 0 or rider is not None
    outs = pl.pallas_call(
        body, grid=grid,
        in_specs=[a_spec, b_spec] + [tile_spec] * n_t + [row_spec] * n_r + [HBM_SPEC] * n_x,
        out_specs=[tile_spec] * n_o + [row_spec] * n_sums + [HBM_SPEC] * n_x,
        out_shape=[jax.ShapeDtypeStruct((M, N), dt) for dt in out_dtypes] + [jax.ShapeDtypeStruct((1, N), F32)] * n_sums
        + ([] if rider is None else rider.out_shapes),
        scratch_shapes=[pltpu.VMEM((tm, tn), F32)] + ([] if rider is None else rider.scratch()),
        compiler_params=_params(*(("arbitrary",) * 3 if ordered else ("parallel", "parallel", "arbitrary"))), name=name,
    )(a, b, *tiles, *rows, *([] if rider is None else rider.arrays))
    res = outs[0] if n_o + n_sums == 1 else outs[:n_o + n_sums]
    return res if rider is None else (res, outs[n_o + n_sums:])


def _token_block(T):
    return min(512, T)


def _rmsnorm_fwd(h, g, *, name):
    T, D = h.shape
    tb = _token_block(T)

    def body(h_ref, g_ref, u_ref):
        x = h_ref[...]
        r = lax.rsqrt(jnp.mean(x * x, axis=-1, keepdims=True) + RMS_EPS)
        u_ref[...] = (x * r * g_ref[...]).astype(u_ref.dtype)

    blk = pl.BlockSpec((tb, D), lambda i: (i, 0))
    return pl.pallas_call(
        body, grid=(T // tb,), in_specs=[blk, pl.BlockSpec((1, D), lambda i: (0, 0))], out_specs=blk,
        out_shape=jax.ShapeDtypeStruct((T, D), MXU_DTYPE), compiler_params=_params("parallel"), name=name,
    )(h, g)


def _rms_bwd_math(x, g, du):
    r = lax.rsqrt(jnp.mean(x * x, axis=-1, keepdims=True) + RMS_EPS)
    gd = g * du
    dx = r * gd - x * (r * r * r) * jnp.mean(gd * x, axis=-1, keepdims=True)
    return dx, du * x * r


def _loss_head(h, g, target, *, name):
    T, D = h.shape
    tb = _token_block(T)

    def body(h_ref, g_ref, t_ref, dh_ref, dhm_ref, dg_ref, loss_ref):
        @pl.when(pl.program_id(0) == 0)
        def _():
            dg_ref[...] = jnp.zeros_like(dg_ref)
            loss_ref[...] = jnp.zeros_like(loss_ref)

        x, gg = h_ref[...], g_ref[...]
        r = lax.rsqrt(jnp.mean(x * x, axis=-1, keepdims=True) + RMS_EPS)
        diff = x * r * gg - t_ref[...]
        per_token = jnp.mean(diff * diff, axis=-1, keepdims=True)
        loss_ref[...] += 0.5 * jnp.sum(per_token, axis=0, keepdims=True)
        dx, dg_terms = _rms_bwd_math(x, gg, diff / D)
        dh_ref[...] = dx
        dhm_ref[...] = dx.astype(dhm_ref.dtype)
        dg_ref[...] += jnp.sum(dg_terms, axis=0, keepdims=True)

    blk = pl.BlockSpec((tb, D), lambda i: (i, 0))
    row = pl.BlockSpec((1, D), lambda i: (0, 0))
    return pl.pallas_call(
        body, grid=(T // tb,), in_specs=[blk, row, blk], out_specs=[blk, blk, row, pl.BlockSpec((1, 1), lambda i: (0, 0))],
        out_shape=[jax.ShapeDtypeStruct((T, D), F32), jax.ShapeDtypeStruct((T, D), MXU_DTYPE), jax.ShapeDtypeStruct((1, D), F32),
                   jax.ShapeDtypeStruct((1, 1), F32)],
        compiler_params=_params("arbitrary"), name=name,
    )(h, g, target)


def _sb_scores(qm, ks, later, tri, mask, need_log_beta=True):
    z = _dot_nt(qm, ks)
    sp = jnp.maximum(z, 0.0) + jnp.log(1.0 + jnp.exp(_neg_abs(z)))
    lb = z - sp if need_log_beta else None
    if mask is not None:
        sp = jnp.where(mask, sp, 0.0)
    after, total = _key_order_sums(sp, tri, later=True)
    w = jnp.exp((lb if need_log_beta else z) - (after + later))
    if mask is not None:
        w = jnp.where(mask, w, 0.0)
    return total, lb, w


def _sb_setup(q_ref, rows, tri_ref, inclusive=False):
    i, hsel = pl.program_id(1), pl.program_id(2)
    lane = lax.broadcasted_iota(jnp.int32, (rows, 2 * SB_DH), 1)
    mine = (lane >= SB_DH) == (hsel == 1)
    diag = (i * rows) // SB_KEYS
    t = i * rows + lax.broadcasted_iota(jnp.int32, (rows, SB_KEYS), 0)
    s = diag * SB_KEYS + lax.broadcasted_iota(jnp.int32, (rows, SB_KEYS), 1)
    @pl.when((pl.program_id(0) == 0) & (i == 0) & (hsel == 0))
    def _():
        a = lax.broadcasted_iota(jnp.int32, (2 * SB_SUB, SB_SUB), 0) % SB_SUB
        b = lax.broadcasted_iota(jnp.int32, (2 * SB_SUB, SB_SUB), 1)
        tri_ref[0] = (a >= b if inclusive else a > b).astype(tri_ref.dtype)
        tri_ref[1] = (a < b).astype(tri_ref.dtype)

    return i, hsel, mine, diag, s < t, tri_ref[0], tri_ref[1]


_SB_TRI = pltpu.VMEM((2, 2 * SB_SUB, SB_SUB), MXU_DTYPE)


def _sb_keys(j, n=1, keys=None):
    return pl.ds(pl.multiple_of(j * SB_KEYS, SB_KEYS), n * SB_KEYS if keys is None else keys)


def _sb_descend(n, step, carry, wide):
    pair = (lambda j, cr: step(j, 2, cr)) if wide else (lambda j, cr: step(j, 1, step(j + 1, 1, cr)))
    carry = lax.fori_loop(0, n // 2, lambda it, cr: pair(n - 2 - 2 * it, cr), carry)
    return lax.cond(n % 2 == 1, lambda cr: step(0, 1, cr), lambda cr: cr, carry)


def _sb_ascend(n, step, carry):
    odd = n % 2
    carry = lax.cond(odd == 1, lambda cr: step(0, 1, cr), lambda cr: cr, carry)
    return lax.fori_loop(0, n // 2, lambda it, cr: step(odd + 2 * it, 2, cr), carry)


def _sb_call(body, qkv, extra_in, out_blocks, out_dtypes, scratch, rider, rows, *, name):
    T = qkv.shape[0]
    n_pairs = SB_HEADS // 2
    grid = (n_pairs, T // rows, 2)
    pair = lambda col0: pl.BlockSpec((rows, 2 * SB_DH), lambda p, i, h: (i, col0 + p))
    whole = lambda col0: pl.BlockSpec((T, 2 * SB_DH), lambda p, i, h: (0, col0 + p))
    in_specs = [pair(0), whole(n_pairs), whole(2 * n_pairs)] + [pair(0)] * len(extra_in)
    out_specs = [pair(0) if kind == "pair" else whole(0) for kind in out_blocks]
    n_in, n_out, n_r = len(in_specs), len(out_specs), 0 if rider is None else len(rider.arrays)

    def kernel_body(*refs):
        ins, r_in = refs[:n_in], refs[n_in:n_in + n_r]
        outs, r_out = refs[n_in + n_r:n_in + n_r + n_out], refs[n_in + n_r + n_out:n_in + 2 * n_r + n_out]
        rest = refs[n_in + 2 * n_r + n_out:]
        ids = [pl.program_id(a) for a in range(3)]
        if rider is not None:
            @pl.when((ids[0] == 0) & (ids[1] == 0) & (ids[2] == 0))
            def _():
                for cp in rider.make(r_in, r_out, *rest[len(scratch):]):
                    cp.start()

        body(ins, outs, rest[:len(scratch)])
        if rider is not None:
            @pl.when((ids[0] == grid[0] - 1) & (ids[1] == grid[1] - 1) & (ids[2] == grid[2] - 1))
            def _():
                for cp in rider.make(r_in, r_out, *rest[len(scratch):]):
                    cp.wait()

    res = pl.pallas_call(
        kernel_body, grid=grid, in_specs=in_specs + [HBM_SPEC] * n_r, out_specs=out_specs + [HBM_SPEC] * n_r,
        out_shape=[jax.ShapeDtypeStruct((T, SB_WIDTH), dt) for dt in out_dtypes] + ([] if rider is None else rider.out_shapes),
        scratch_shapes=list(scratch) + ([] if rider is None else rider.scratch()),
        compiler_params=_params("arbitrary", "arbitrary", "arbitrary"), name=name,
    )(qkv, qkv, qkv, *extra_in, *([] if rider is None else rider.arrays))
    return res[:n_out], res[n_out:]


def _sb_fwd(qkv, rider=None, *, name):
    rows = min(SB_ROWS_FWD, qkv.shape[0])
    scale = SB_DH ** -0.5

    def body(ins, outs, scratch):
        (q_ref, k_ref, v_ref), (o_ref,), (tri_ref,) = ins, outs, scratch
        i, hsel, mine, diag, mask, tri, _ = _sb_setup(q_ref, rows, tri_ref, inclusive=True)
        qm = jnp.where(mine, q_ref[...], 0) * scale

        def tile(j, n, m, later, acc):
            total, _, w = _sb_scores(qm, k_ref[_sb_keys(j, n), :], later, tri, m, need_log_beta=False)
            return later + total, acc + _dot(w, v_ref[_sb_keys(j, n), :])

        carry = tile(diag, 1, mask, jnp.zeros((rows, 1), F32), jnp.zeros((rows, 2 * SB_DH), F32))
        _, acc = _sb_descend(diag, lambda j, n, cr: tile(j, n, None, *cr), carry, wide=False)
        res = jnp.where(mine, acc, 0.0).astype(o_ref.dtype)

        @pl.when(hsel == 0)
        def _():
            o_ref[...] = res

        @pl.when(hsel == 1)
        def _():
            o_ref[...] += res

    (o,), got = _sb_call(body, qkv, [], ["pair"], [MXU_DTYPE], [_SB_TRI], rider, rows, name=name)
    return o, got


def _sb_bwd(qkv, dmix, rider=None, *, name):
    T = qkv.shape[0]
    rows = min(SB_ROWS_BWD, T)
    scale = SB_DH ** -0.5

    def body(ins, outs, scratch):
        (q_ref, k_ref, v_ref, do_ref), (dq_ref, dk_ref, dv_ref), (da_ref, beta_ref, tri_ref) = ins, outs, scratch
        i, hsel, mine, diag, mask, tri, tri_before = _sb_setup(q_ref, rows, tri_ref)

        @pl.when((i == 0) & (hsel == 0))
        def _():
            dk_ref[...] = jnp.zeros_like(dk_ref)
            dv_ref[...] = jnp.zeros_like(dv_ref)

        qm = jnp.where(mine, q_ref[...], 0) * scale
        do_m = _mx(jnp.where(mine, do_ref[...], 0.0))

        def weights(j, n, m, later, keys=None):
            sl = _sb_keys(j, n, keys)
            total, lb, w = _sb_scores(qm, k_ref[sl, :], later, tri, m)
            da, beta = _dot_nt(do_m, v_ref[sl, :]) * w, jnp.exp(lb)
            if keys is None:
                for t in range(n):
                    da_ref[j + t] = da[:, t * SB_KEYS:(t + 1) * SB_KEYS]
                    beta_ref[j + t] = beta[:, t * SB_KEYS:(t + 1) * SB_KEYS]
            else:
                da_ref[j, :, 0:keys] = da
                beta_ref[j, :, 0:keys] = beta
            dv_ref[sl, :] += _dot_tn(w, do_m)
            return later + total

        short = (i * rows) % SB_KEYS + rows <= SB_SUB
        short_mask = mask[:, 0:SB_SUB]
        later = lax.cond(short, lambda z: weights(diag, 1, short_mask, z, SB_SUB), lambda z: weights(diag, 1, mask, z),
                         jnp.zeros((rows, 1), F32))
        _sb_descend(diag, lambda j, n, c: weights(j, n, None, c), later, wide=True)

        def logits(j, n, m, before, dq, keys=None):
            if keys is None:
                da = jnp.concatenate([da_ref[j + t] for t in range(n)], axis=1)
                beta = jnp.concatenate([beta_ref[j + t] for t in range(n)], axis=1)
            else:
                da, beta = da_ref[j, :, 0:keys], beta_ref[j, :, 0:keys]
            earlier, total = _key_order_sums(da, tri_before, later=False)
            dz = da - beta * (da + earlier + before)
            if m is not None:
                dz = jnp.where(m, dz, 0.0)
            dz = _mx(dz)
            dk_ref[_sb_keys(j, n, keys), :] += _dot_tn(dz, qm)
            return before + total, dq + _dot(dz, k_ref[_sb_keys(j, n, keys), :])

        carry = (jnp.zeros((rows, 1), F32), jnp.zeros((rows, 2 * SB_DH), F32))
        carry = _sb_ascend(diag, lambda j, n, cr: logits(j, n, None, *cr), carry)
        dq = lax.cond(short, lambda cr: logits(diag, 1, short_mask, *cr, keys=SB_SUB)[1], lambda cr: logits(diag, 1, mask, *cr)[1], carry)
        res = jnp.where(mine, dq * scale, 0.0).astype(dq_ref.dtype)

        @pl.when(hsel == 0)
        def _():
            dq_ref[...] = res

        @pl.when(hsel == 1)
        def _():
            dq_ref[...] += res

    n_tiles = T // SB_KEYS
    scratch = [pltpu.VMEM((n_tiles, rows, SB_KEYS), F32), pltpu.VMEM((n_tiles, rows, SB_KEYS), F32), _SB_TRI]
    return _sb_call(body, qkv, [dmix], ["pair", "whole", "whole"], [MXU_DTYPE, F32, F32], scratch, rider, rows, name=name)


def _chunk_row(n):
    return lax.broadcasted_iota(jnp.int32, (n, HG_DH), 0) % HG_CHUNK


def _chunk_cumsum(x, row, reverse=False):
    n = x.shape[0]
    for sh in (1, 2, 4, 8):
        if reverse:
            x = x + jnp.where(row < HG_CHUNK - sh, pltpu.roll(x, n - sh, 0), 0.0)
        else:
            x = x + jnp.where(row >= sh, pltpu.roll(x, sh, 0), 0.0)
    return x


def _hg_lower_bound(logits_ref):
    lg = logits_ref[...]
    e = jnp.exp(lg - jnp.max(lg, axis=0, keepdims=True))
    return e[0:1, :] / jnp.sum(e, axis=0, keepdims=True)


def _hg_terms(fr, q, lb, row):
    sig = jax.nn.sigmoid(fr)
    f = lb + (1.0 - lb) * sig
    kk = 1.0 - f
    g = jnp.log(f)
    G = _chunk_cumsum(g, row)
    g_last = G + (_chunk_cumsum(g, row, reverse=True) - g)
    e_g, e_ng, e_lg = jnp.exp(G), jnp.exp(-G), jnp.exp(g_last - G)
    return dict(sig=sig, f=f, kk=kk, e_g=e_g, e_ng=e_ng, e_lg=e_lg, q_dec=q * e_g, k_intra=kk * e_ng,
                k_state=kk * e_lg, decay=jnp.exp(g_last))


def _hg_causal(n):
    t = lax.broadcasted_iota(jnp.int32, (n, n), 0)
    s = lax.broadcasted_iota(jnp.int32, (n, n), 1)
    return (s <= t) & (s // HG_CHUNK == t // HG_CHUNK)


def _chunks(a):
    return a.reshape(a.shape[0] // HG_CHUNK, HG_CHUNK, a.shape[1])


def _per_chunk(lhs, rhs, contract):
    return lax.dot_general(_mx(lhs), _mx(rhs), ((contract[:1], contract[1:]), ((0,), (0,))), preferred_element_type=F32)


def _hg_specs(T, tb, col0, order):
    return [pl.BlockSpec((tb, HG_WIDTH), functools.partial(lambda i, j: (order(i), j), j=col0 + j)) for j in range(4)]


def _hg_fwd(proj, logits, norm_g, *, name):
    T = proj.shape[0]
    tb = min(HG_TOKENS, T)
    nch = tb // HG_CHUNK

    def body(q_ref, f_ref, i_ref, gate_ref, lg_ref, ng_ref, out_ref, o_ref, s_ref, st_ref, inc_ref, dec_ref):
        @pl.when(pl.program_id(0) == 0)
        def _():
            st_ref[...] = jnp.zeros_like(st_ref)

        lb_all = _hg_lower_bound(lg_ref)
        row = _chunk_row(tb)
        causal = _hg_causal(tb)
        for hh in range(HG_HEADS):
            cols = slice(hh * HG_DH, (hh + 1) * HG_DH)
            t = _hg_terms(f_ref[:, cols], q_ref[:, cols], lb_all[:, cols], row)
            v = i_ref[:, cols]
            scores = jnp.where(causal, _dot_nt(t["q_dec"], t["k_intra"]), 0.0)
            o_intra = _dot(scores, v)
            inc_ref[...] = _per_chunk(_chunks(v), _chunks(t["k_state"]), (1, 1))
            dec_ref[...] = _chunks(t["decay"])

            def step(ci, st):
                s_ref[ci, hh] = st
                return st * dec_ref[ci][0:1, :] + inc_ref[ci]

            st_ref[hh] = lax.fori_loop(0, nch, step, st_ref[hh], unroll=4)
            o_inter = _per_chunk(_chunks(t["q_dec"]), s_ref[:, hh], (2, 2))
            o = o_intra + o_inter.reshape(tb, HG_DH)
            o_ref[:, cols] = o
            gate = gate_ref[:, cols]
            on = o * lax.rsqrt(jnp.mean(o * o, axis=-1, keepdims=True) + RMS_EPS) * ng_ref[:, cols]
            out_ref[:, cols] = (on * (gate * jax.nn.sigmoid(gate))).astype(out_ref.dtype)

    blk = pl.BlockSpec((tb, HG_WIDTH), lambda i: (i, 0))
    return pl.pallas_call(
        body, grid=(T // tb,),
        in_specs=_hg_specs(T, tb, 3, lambda i: i) + [pl.BlockSpec((3, HG_WIDTH), lambda i: (0, 0)), pl.BlockSpec((1, HG_WIDTH), lambda i: (0, 0))],
        out_specs=[blk, blk, pl.BlockSpec((nch, HG_HEADS, HG_DH, HG_DH), lambda i: (i, 0, 0, 0))],
        out_shape=[jax.ShapeDtypeStruct((T, HG_WIDTH), MXU_DTYPE), jax.ShapeDtypeStruct((T, HG_WIDTH), F32),
                   jax.ShapeDtypeStruct((T // HG_CHUNK, HG_HEADS, HG_DH, HG_DH), F32)],
        scratch_shapes=[pltpu.VMEM((HG_HEADS, HG_DH, HG_DH), F32), pltpu.VMEM((nch, HG_DH, HG_DH), F32),
                        pltpu.VMEM((nch, HG_CHUNK, HG_DH), F32)],
        compiler_params=_params("arbitrary"), name=name,
    )(proj, proj, proj, proj, logits, norm_g)


def _hg_bwd(proj, o_raw, states, dmix, logits, norm_g, *, name):
    T = proj.shape[0]
    tb = min(HG_TOKENS, T)
    nch = tb // HG_CHUNK
    nb = T // tb
    rev = lambda i: nb - 1 - i

    def body(q_ref, f_ref, i_ref, gate_ref, o_ref, s_ref, dout_ref, lg_ref, ng_ref, dp_ref, dlb_ref, dng_ref,
             dst_ref, inc_ref, dec_ref, after_ref):
        @pl.when(pl.program_id(0) == 0)
        def _():
            dst_ref[...] = jnp.zeros_like(dst_ref)
            dlb_ref[...] = jnp.zeros_like(dlb_ref)
            dng_ref[...] = jnp.zeros_like(dng_ref)

        lb_all = _hg_lower_bound(lg_ref)
        row = _chunk_row(tb)
        causal = _hg_causal(tb)
        for hh in range(HG_HEADS):
            cols = slice(hh * HG_DH, (hh + 1) * HG_DH)
            out_cols = lambda part: slice(part * HG_WIDTH + hh * HG_DH, part * HG_WIDTH + (hh + 1) * HG_DH)
            o, gate, dout, ng, lb = o_ref[:, cols], gate_ref[:, cols], dout_ref[:, cols], ng_ref[:, cols], lb_all[:, cols]
            sg = jax.nn.sigmoid(gate)
            r = lax.rsqrt(jnp.mean(o * o, axis=-1, keepdims=True) + RMS_EPS)
            oh = o * r
            dp_ref[:, out_cols(3)] = (dout * (oh * ng) * (sg * (1.0 + gate * (1.0 - sg)))).astype(dp_ref.dtype)
            don = dout * (gate * sg)
            dng_ref[:, cols] += jnp.sum(don * oh, axis=0, keepdims=True)
            doh = don * ng
            do = r * (doh - oh * jnp.mean(doh * oh, axis=-1, keepdims=True))

            t = _hg_terms(f_ref[:, cols], q_ref[:, cols], lb, row)
            v = i_ref[:, cols]
            scores = jnp.where(causal, _dot_nt(t["q_dec"], t["k_intra"]), 0.0)
            dscores = jnp.where(causal, _dot_nt(do, v), 0.0)
            inc_ref[...] = _per_chunk(_chunks(do), _chunks(t["q_dec"]), (1, 1))
            dec_ref[...] = _chunks(t["decay"])

            def step(it, dst):
                ci = nch - 1 - it
                after_ref[ci] = dst
                return dst * dec_ref[ci][0:1, :] + inc_ref[ci]

            dst_ref[hh] = lax.fori_loop(0, nch, step, dst_ref[hh], unroll=4)
            st, dst = s_ref[:, hh], after_ref[...]
            dqd = _dot(dscores, t["k_intra"]) + _per_chunk(_chunks(do), st, (2, 1)).reshape(tb, HG_DH)
            dki = _dot_tn(dscores, t["q_dec"])
            dks = _per_chunk(_chunks(v), dst, (2, 1)).reshape(tb, HG_DH)
            dp_ref[:, out_cols(2)] = (_dot_tn(scores, do) + _per_chunk(_chunks(t["k_state"]), dst, (2, 2)).reshape(tb, HG_DH)).astype(dp_ref.dtype)
            ddecay = jnp.broadcast_to(jnp.sum(st * dst, axis=1, keepdims=True), (nch, HG_CHUNK, HG_DH)).reshape(tb, HG_DH)
            dks_ks = dks * t["k_state"]
            d_glast = _chunk_cumsum(dks_ks, row) + ddecay * t["decay"]
            d_g = dqd * t["q_dec"] - dki * t["k_intra"] - dks_ks + jnp.where(row == HG_CHUNK - 1, d_glast, 0.0)
            df = _chunk_cumsum(d_g, row, reverse=True) / t["f"] - (dki * t["e_ng"] + dks * t["e_lg"])
            dp_ref[:, out_cols(0)] = (dqd * t["e_g"]).astype(dp_ref.dtype)
            dp_ref[:, out_cols(1)] = (df * (1.0 - lb) * t["sig"] * (1.0 - t["sig"])).astype(dp_ref.dtype)
            dlb_ref[:, cols] += jnp.sum(df * (1.0 - t["sig"]), axis=0, keepdims=True)

    blk = pl.BlockSpec((tb, HG_WIDTH), lambda i: (rev(i), 0))
    row_spec = pl.BlockSpec((1, HG_WIDTH), lambda i: (0, 0))
    return pl.pallas_call(
        body, grid=(nb,),
        in_specs=_hg_specs(T, tb, 3, rev) + [
            blk, pl.BlockSpec((nch, HG_HEADS, HG_DH, HG_DH), lambda i: (rev(i), 0, 0, 0)),
            pl.BlockSpec((tb, HG_WIDTH), lambda i: (rev(i), 1)), pl.BlockSpec((3, HG_WIDTH), lambda i: (0, 0)), row_spec],
        out_specs=[pl.BlockSpec((tb, 4 * HG_WIDTH), lambda i: (rev(i), 0)), row_spec, row_spec],
        out_shape=[jax.ShapeDtypeStruct((T, 4 * HG_WIDTH), MXU_DTYPE), jax.ShapeDtypeStruct((1, HG_WIDTH), F32), jax.ShapeDtypeStruct((1, HG_WIDTH), F32)],
        scratch_shapes=[pltpu.VMEM((HG_HEADS, HG_DH, HG_DH), F32), pltpu.VMEM((nch, HG_DH, HG_DH), F32),
                        pltpu.VMEM((nch, HG_CHUNK, HG_DH), F32), pltpu.VMEM((nch, HG_DH, HG_DH), F32)],
        compiler_params=_params("arbitrary"), name=name,
    )(proj, proj, proj, proj, o_raw, states, dmix, logits, norm_g)


def _shifted_copies(sh_ref, n_rows):
    keep = n_rows + CONV_HALO - 8
    for b in range(1, 8):
        sh_ref[b, 0:keep, :] = sh_ref[0, b:b + keep, :]


def _tap_rows(sh_ref, offset, r0, lanes):
    start = pl.multiple_of(r0 + (offset - offset % 8), 8)
    return sh_ref[offset % 8, pl.ds(start, CONV_ROWS), lanes]


def _conv_fwd(p, w_dw, b_dw, ln_g, ln_b, *, name):
    T, D = p.shape[0], p.shape[1] // 2
    tb = _token_block(T)
    hpb = tb // CONV_HALO
    lane_step = 512

    def body(p1_ref, p2_ref, q1_ref, q2_ref, w_ref, bdw_ref, g_ref, b_ref, a_ref, y_ref, act_ref, sh_ref):
        i = pl.program_id(0)
        a = p1_ref[...] * jax.nn.sigmoid(p2_ref[...])
        sh_ref[0, 0:CONV_HALO, :] = jnp.where(i > 0, q1_ref[...] * jax.nn.sigmoid(q2_ref[...]), 0.0)
        sh_ref[0, CONV_HALO:, :] = a
        a_ref[...] = a
        _shifted_copies(sh_ref, tb)

        def chunk(ci, _):
            r0 = pl.multiple_of(ci * CONV_ROWS, CONV_ROWS)
            for l0 in range(0, D, lane_step):
                lanes = slice(l0, l0 + lane_step)
                acc = jnp.broadcast_to(bdw_ref[:, lanes], (CONV_ROWS, lane_step))
                for k in range(CONV_WIDTH):
                    acc = acc + _tap_rows(sh_ref, CONV_HALO - CONV_WIDTH + 1 + k, r0, lanes) * w_ref[k:k + 1, lanes]
                y_ref[pl.ds(r0, CONV_ROWS), lanes] = acc
            return 0

        lax.fori_loop(0, tb // CONV_ROWS, chunk, 0)
        y = y_ref[...]
        mu = jnp.mean(y, axis=-1, keepdims=True)
        yc = y - mu
        s = yc * lax.rsqrt(jnp.mean(yc * yc, axis=-1, keepdims=True) + LN_EPS) * g_ref[...] + b_ref[...]
        act_ref[...] = (s * jax.nn.sigmoid(s)).astype(act_ref.dtype)

    prev = lambda i: jnp.maximum(i * hpb - 1, 0)
    blk = pl.BlockSpec((tb, D), lambda i: (i, 0))
    row = pl.BlockSpec((1, D), lambda i: (0, 0))
    return pl.pallas_call(
        body, grid=(T // tb,),
        in_specs=[blk, pl.BlockSpec((tb, D), lambda i: (i, 1)), pl.BlockSpec((CONV_HALO, D), lambda i: (prev(i), 0)),
                  pl.BlockSpec((CONV_HALO, D), lambda i: (prev(i), 1)), pl.BlockSpec((CONV_HALO, D), lambda i: (0, 0)), row, row, row],
        out_specs=[blk, blk, blk],
        out_shape=[jax.ShapeDtypeStruct((T, D), F32), jax.ShapeDtypeStruct((T, D), F32), jax.ShapeDtypeStruct((T, D), MXU_DTYPE)],
        scratch_shapes=[pltpu.VMEM((8, tb + CONV_HALO, D), F32)],
        compiler_params=_params("parallel"), name=name,
    )(p, p, p, p, w_dw, b_dw, ln_g, ln_b)


def _conv_bwd_norm(dact, y, ln_g, ln_b, *, name):
    T, D = y.shape
    tb = _token_block(T)

    def body(da_ref, y_ref, g_ref, b_ref, dy_ref, dg_ref, db_ref, cs_ref):
        @pl.when(pl.program_id(0) == 0)
        def _():
            dg_ref[...] = jnp.zeros_like(dg_ref)
            db_ref[...] = jnp.zeros_like(db_ref)
            cs_ref[...] = jnp.zeros_like(cs_ref)

        y, g = y_ref[...], g_ref[...]
        yc = y - jnp.mean(y, axis=-1, keepdims=True)
        rs = lax.rsqrt(jnp.mean(yc * yc, axis=-1, keepdims=True) + LN_EPS)
        yn = yc * rs
        s = yn * g + b_ref[...]
        sg = jax.nn.sigmoid(s)
        ds = da_ref[...] * (sg * (1.0 + s * (1.0 - sg)))
        dg_ref[...] += jnp.sum(ds * yn, axis=0, keepdims=True)
        db_ref[...] += jnp.sum(ds, axis=0, keepdims=True)
        dyn = ds * g
        dy = rs * (dyn - jnp.mean(dyn, axis=-1, keepdims=True) - yn * jnp.mean(dyn * yn, axis=-1, keepdims=True))
        dy_ref[...] = dy
        cs_ref[...] += jnp.sum(dy, axis=0, keepdims=True)

    blk = pl.BlockSpec((tb, D), lambda i: (i, 0))
    row = pl.BlockSpec((1, D), lambda i: (0, 0))
    rs_ = jax.ShapeDtypeStruct((1, D), F32)
    return pl.pallas_call(
        body, grid=(T // tb,), in_specs=[blk, blk, row, row], out_specs=[blk, row, row, row],
        out_shape=[jax.ShapeDtypeStruct((T, D), F32), rs_, rs_, rs_], compiler_params=_params("arbitrary"), name=name,
    )(dact, y, ln_g, ln_b)


def _conv_bwd_taps(dy, a, p, w_dw, *, name):
    T, D = dy.shape
    tb = _token_block(T)
    hpb = tb // CONV_HALO
    last = T // CONV_HALO - 1
    nb = T // tb
    lane_step = 128
    groups = CONV_ROWS // 8

    def body(dy_ref, dyn_ref, a_ref, p1_ref, p2_ref, w_ref, dp_ref, dw_ref, cs_ref, sh_ref, da_ref):
        i = pl.program_id(0)

        @pl.when(i == 0)
        def _():
            dw_ref[...] = jnp.zeros_like(dw_ref)
            cs_ref[...] = jnp.zeros_like(cs_ref)

        sh_ref[0, 0:tb, :] = dy_ref[...]
        sh_ref[0, tb:, :] = jnp.where(i < nb - 1, dyn_ref[...], 0.0)
        _shifted_copies(sh_ref, tb)
        for l0 in range(0, D, lane_step):
            lanes = slice(l0, l0 + lane_step)
            for taps in (range(0, CONV_WIDTH // 2 + 1), range(CONV_WIDTH // 2 + 1, CONV_WIDTH)):
                def chunk(ci, sums, taps=taps):
                    r0 = pl.multiple_of(ci * CONV_ROWS, CONV_ROWS)
                    a_c = a_ref[pl.ds(r0, CONV_ROWS), lanes]
                    da = jnp.zeros((CONV_ROWS, lane_step), F32) if taps[0] == 0 else da_ref[pl.ds(r0, CONV_ROWS), lanes]
                    new = []
                    for n, k in enumerate(taps):
                        s_k = _tap_rows(sh_ref, CONV_WIDTH - 1 - k, r0, lanes)
                        da = da + s_k * w_ref[k:k + 1, lanes]
                        new.append(sums[n] + jnp.sum((s_k * a_c).reshape(groups, 8, lane_step), axis=0))
                    da_ref[pl.ds(r0, CONV_ROWS), lanes] = da
                    return tuple(new)

                sums = lax.fori_loop(0, tb // CONV_ROWS, chunk, tuple(jnp.zeros((8, lane_step), F32) for _ in taps))
                for n, k in enumerate(taps):
                    dw_ref[k:k + 1, lanes] += jnp.sum(sums[n], axis=0, keepdims=True)
        da = da_ref[...]
        p1 = p1_ref[...]
        sg = jax.nn.sigmoid(p2_ref[...])
        dp1 = da * sg
        dp2 = da * p1 * (sg * (1.0 - sg))
        dp_ref[:, 0:D] = dp1.astype(dp_ref.dtype)
        dp_ref[:, D:] = dp2.astype(dp_ref.dtype)
        cs_ref[:, 0:D] += jnp.sum(dp1, axis=0, keepdims=True)
        cs_ref[:, D:] += jnp.sum(dp2, axis=0, keepdims=True)

    blk = pl.BlockSpec((tb, D), lambda i: (i, 0))
    return pl.pallas_call(
        body, grid=(nb,),
        in_specs=[blk, pl.BlockSpec((CONV_HALO, D), lambda i: (jnp.minimum((i + 1) * hpb, last), 0)), blk, blk,
                  pl.BlockSpec((tb, D), lambda i: (i, 1)), pl.BlockSpec((CONV_HALO, D), lambda i: (0, 0))],
        out_specs=[pl.BlockSpec((tb, 2 * D), lambda i: (i, 0)), pl.BlockSpec((CONV_HALO, D), lambda i: (0, 0)), pl.BlockSpec((1, 2 * D), lambda i: (0, 0))],
        out_shape=[jax.ShapeDtypeStruct((T, 2 * D), MXU_DTYPE), jax.ShapeDtypeStruct((CONV_HALO, D), F32), jax.ShapeDtypeStruct((1, 2 * D), F32)],
        scratch_shapes=[pltpu.VMEM((8, tb + CONV_HALO, D), F32), pltpu.VMEM((tb, D), F32)],
        compiler_params=_params("arbitrary"), name=name,
    )(dy, dy, a, p, p, w_dw)


def _row_block(rows):
    for tr in (512, 256, 128, 64, 32, 16, 8):
        if rows % tr == 0:
            return tr
    return rows


def _sum_leading(x, *, name):
    n, R, C = x.shape
    tr = _row_block(R)

    def body(x_ref, o_ref):
        acc = x_ref[0]
        for j in range(1, n):
            acc = acc + x_ref[j]
        o_ref[...] = acc

    return pl.pallas_call(
        body, grid=(R // tr,), in_specs=[pl.BlockSpec((n, tr, C), lambda i: (0, i, 0))], out_specs=pl.BlockSpec((tr, C), lambda i: (i, 0)),
        out_shape=jax.ShapeDtypeStruct((R, C), x.dtype), compiler_params=_params("parallel"), name=name,
    )(x)


def _add_pair(x, y, *, name):
    n, R, C = x.shape
    tr = _row_block(R)

    def body(x_ref, y_ref, o_ref):
        o_ref[...] = x_ref[...] + y_ref[...]

    blk = pl.BlockSpec((1, tr, C), lambda j, i: (j, i, 0))
    return pl.pallas_call(
        body, grid=(n, R // tr), in_specs=[blk, blk], out_specs=blk,
        out_shape=jax.ShapeDtypeStruct((n, R, C), x.dtype), compiler_params=_params("parallel", "parallel"), name=name,
    )(x, y)


def _adamw(w, g, m, v, *, name):
    R, C = w.shape
    tr = _row_block(R)
    c1, c2 = 1.0 - ADAM_B1 ** ADAM_STEP, 1.0 - ADAM_B2 ** ADAM_STEP

    def body(w_ref, g_ref, m_ref, v_ref, d_ref, nm_ref, nv_ref):
        g_ = g_ref[...]
        nm = ADAM_B1 * m_ref[...] + (1.0 - ADAM_B1) * g_
        nv = ADAM_B2 * v_ref[...] + (1.0 - ADAM_B2) * (g_ * g_)
        d_ref[...] = -ADAM_LR * ((nm / c1) / (jnp.sqrt(nv / c2) + ADAM_EPS) + ADAM_WD * w_ref[...])
        nm_ref[...] = nm
        nv_ref[...] = nv

    blk = pl.BlockSpec((tr, C), lambda i: (i, 0))
    shp = jax.ShapeDtypeStruct((R, C), F32)
    return pl.pallas_call(
        body, grid=(R // tr,), in_specs=[blk] * 4, out_specs=[blk] * 3, out_shape=[shp] * 3,
        compiler_params=_params("parallel"), name=name,
    )(w, g, m, v)


def _small_reduce(packs, logits, *, name):
    n, R, C = packs.shape

    def body(p_ref, lg_ref, s_ref, dlg_ref):
        acc = p_ref[0]
        for j in range(1, n):
            acc = acc + p_ref[j]
        s_ref[...] = acc
        lg = lg_ref[...]
        e = jnp.exp(lg - jnp.max(lg, axis=0, keepdims=True))
        sm = e / jnp.sum(e, axis=0, keepdims=True)
        dlb = acc[5:6, HG_WIDTH:2 * HG_WIDTH]
        first = lax.broadcasted_iota(jnp.int32, sm.shape, 0) == 0
        dlg_ref[...] = sm[0:1, :] * (jnp.where(first, 1.0, 0.0) - sm) * dlb

    whole = lambda shape: pl.BlockSpec(shape, lambda: (0,) * len(shape))
    return pl.pallas_call(
        body, in_specs=[whole((n, R, C)), whole(logits.shape)], out_specs=[whole((R, C)), whole(logits.shape)],
        out_shape=[jax.ShapeDtypeStruct((R, C), F32), jax.ShapeDtypeStruct(logits.shape, F32)],
        compiler_params=pltpu.CompilerParams(vmem_limit_bytes=VMEM_LIMIT), name=name,
    )(packs, logits)


HBM_SPEC = pl.BlockSpec(memory_space=pl.ANY)


def _place():
    return lax.axis_index("x"), lax.axis_index("y"), lax.axis_index("c")


class _Copies:
    def __init__(self, arrays, out_shapes, n_copies, make, finish):
        self.arrays, self.out_shapes, self.n_copies, self.make, self.finish = list(arrays), list(out_shapes), n_copies, make, finish

    def scratch(self):
        return [pltpu.SemaphoreType.DMA((self.n_copies,)), pltpu.SemaphoreType.DMA((self.n_copies,))]

    def run(self, name):
        n = len(self.arrays)

        def body(*refs):
            copies = self.make(refs[:n], refs[n:2 * n], *refs[2 * n:])
            for cp in copies:
                cp.start()
            for cp in copies:
                cp.wait()

        outs = pl.pallas_call(body, in_specs=[HBM_SPEC] * n, out_specs=[HBM_SPEC] * n, out_shape=self.out_shapes,
                              scratch_shapes=self.scratch(), name=name)(*self.arrays)
        return self.finish(outs)


def _remote(src, dst, send_sems, recv_sems, k, peer):
    return pltpu.make_async_remote_copy(src_ref=src, dst_ref=dst, send_sem=send_sems.at[k], recv_sem=recv_sems.at[k],
                                        device_id=peer, device_id_type=MESH)


def _same_core_peers(x, y, c):
    return [(1 - x, y, c), (x, 1 - y, c), (1 - x, 1 - y, c)]


def _all_peers(x, y, c):
    flip = lambda v, b: 1 - v if b else v
    return [(flip(x, r & 4), flip(y, r & 2), flip(c, r & 1)) for r in range(1, 8)]


def _gather(arrays, peers_of, slot_of, n_slots):
    n_peers = len(peers_of(0, 0, 0))

    def make(ins, outs, send_sems, recv_sems):
        x, y, c = _place()
        slot = slot_of(x, y, c)
        return [_remote(ins[a], outs[a].at[slot], send_sems, recv_sems, a * n_peers + k, peer)
                for a in range(len(arrays)) for k, peer in enumerate(peers_of(x, y, c))]

    def finish(outs):
        slot = slot_of(*_place())
        return [lax.dynamic_update_index_in_dim(o, a, slot, 0) for o, a in zip(outs, arrays)]

    shapes = [jax.ShapeDtypeStruct((n_slots,) + a.shape, a.dtype) for a in arrays]
    return _Copies(arrays, shapes, len(arrays) * n_peers, make, finish)


def _gather_chips(arrays):
    return _gather(arrays, _same_core_peers, lambda x, y, c: 2 * x + y, N_CHIPS)


def _gather_chips_two_level(arrays, *, name):
    n = len(arrays)

    def body(*refs):
        ins, outs, (send_sems, recv_sems) = refs[:n], refs[n:2 * n], refs[2 * n:]
        x, y, c = _place()
        chips = [(1 - x, y), (x, 1 - y), (1 - x, 1 - y)]
        over_ici = [[_remote(ins[a].at[c], outs[a].at[2 * x + y, c], send_sems, recv_sems, 6 * a + k, (px, py, c))
                     for k, (px, py) in enumerate(chips)] for a in range(n)]
        passed_on = [[_remote(outs[a].at[2 * px + py, c], outs[a].at[2 * px + py, c], send_sems, recv_sems, 6 * a + 3 + k, (x, y, 1 - c))
                      for k, (px, py) in enumerate(chips)] for a in range(n)]
        for a in range(n):
            for cp in over_ici[a]:
                cp.start()
        for a in range(n):
            for k in range(3):
                over_ici[a][k].wait_recv()
                passed_on[a][k].start()
        for a in range(n):
            for k in range(3):
                passed_on[a][k].wait_recv()
        for a in range(n):
            for cp in over_ici[a] + passed_on[a]:
                cp.wait_send()

    outs = pl.pallas_call(
        body, in_specs=[HBM_SPEC] * n, out_specs=[HBM_SPEC] * n,
        out_shape=[jax.ShapeDtypeStruct((N_CHIPS,) + a.shape, a.dtype) for a in arrays],
        scratch_shapes=[pltpu.SemaphoreType.DMA((6 * n,)), pltpu.SemaphoreType.DMA((6 * n,))], name=name,
    )(*arrays)
    x, y, _ = _place()
    return [lax.dynamic_update_index_in_dim(o, a, 2 * x + y, 0) for o, a in zip(outs, arrays)]


def _gather_all(arrays):
    return _gather(arrays, _all_peers, lambda x, y, c: 4 * x + 2 * y + c, N_DEV)


def _pair_swap(a):
    def make(ins, outs, send_sems, recv_sems):
        x, y, c = _place()
        return [_remote(ins[0].at[1 - c], outs[0], send_sems, recv_sems, 0, (x, y, 1 - c))]

    return _Copies([a], [jax.ShapeDtypeStruct(a.shape[1:], a.dtype)], 1, make, lambda outs: outs[0])


def _chip_scatter(p):
    def make(ins, outs, send_sems, recv_sems):
        x, y, c = _place()
        return [_remote(ins[0].at[2 * px + py], outs[0].at[2 * x + y], send_sems, recv_sems, k, (px, py, pc))
                for k, (px, py, pc) in enumerate(_same_core_peers(x, y, c))]

    def finish(outs):
        x, y, _ = _place()
        me = 2 * x + y
        return lax.dynamic_update_index_in_dim(outs[0], lax.dynamic_index_in_dim(p, me, 0, keepdims=False), me, 0)

    return _Copies([p], [jax.ShapeDtypeStruct(p.shape, p.dtype)], 3, make, finish)


def _owner_scatter(blocks):
    def make(ins, outs, send_sems, recv_sems):
        x, y, c = _place()
        return [_remote(ins[0].at[2 * px + py, pc], outs[0].at[4 * x + 2 * y + c], send_sems, recv_sems, k, (px, py, pc))
                for k, (px, py, pc) in enumerate(_all_peers(x, y, c))]

    def finish(outs):
        x, y, c = _place()
        mine = lax.dynamic_index_in_dim(lax.dynamic_index_in_dim(blocks, 2 * x + y, 0, keepdims=False), c, 0, keepdims=False)
        return lax.dynamic_update_index_in_dim(outs[0], mine, 4 * x + 2 * y + c, 0)

    return _Copies([blocks], [jax.ShapeDtypeStruct((N_DEV,) + blocks.shape[2:], blocks.dtype)], N_DEV - 1, make, finish)


def _pair_gather(q):
    def make(ins, outs, send_sems, recv_sems):
        x, y, c = _place()
        return [_remote(ins[0], outs[0].at[c], send_sems, recv_sems, 0, (x, y, 1 - c))]

    return _Copies([q], [jax.ShapeDtypeStruct((2,) + q.shape, q.dtype)], 1, make,
                   lambda outs: lax.dynamic_update_index_in_dim(outs[0], q, _place()[2], 0))


def _grad_blocks(dw, kind):
    if kind == "cols2d":
        K, N = dw.shape
        b = dw.reshape(2, K // 2, N_CHIPS, N // N_CHIPS).transpose(2, 0, 1, 3)
    elif kind == "rows2d":
        b = dw.reshape(N_CHIPS, 2, dw.shape[0] // 8, dw.shape[1])
    elif kind == "cols3d":
        L, K, N = dw.shape
        b = dw.reshape(L, K, N_CHIPS, N // N_CHIPS).transpose(2, 0, 1, 3)
    else:
        L, K, N = dw.shape
        b = dw.reshape(L, N_CHIPS, K // N_CHIPS, N).transpose(1, 0, 2, 3)
    return b.reshape(N_CHIPS, 2, -1, D_MODEL)


def _pad_rows(a, rows):
    return jnp.concatenate([a, jnp.zeros((rows - a.shape[0],) + a.shape[1:], a.dtype)], axis=0)


def _forward_backward(x, target, W, late_weights=None, early_grads=None, last_grads=None):
    row = lambda a: a.reshape(1, -1)
    relu2 = lambda acc: (jnp.square(jnp.maximum(acc, 0.0)),)
    normed = lambda h, g: h * lax.rsqrt(jnp.mean(h * h, axis=-1, keepdims=True) + RMS_EPS) * g

    def residual_norm(acc, res, g):
        h = res + acc
        return h, normed(h, g)

    def norm_bwd(du, h_blk, dres_blk, g):
        dx, dg_terms = _rms_bwd_math(h_blk, g, du)
        dh = dres_blk + dx
        return dh, dh, jnp.sum(dg_terms, axis=0, keepdims=True), jnp.sum(dh, axis=0, keepdims=True)

    def matmul_norm_bwd(dy, w, h_in, g, dres, rider=None, *, tk=1024, name):
        return _matmul(dy, w, mode="nt", out_dtypes=[F32, MXU_DTYPE], epilogue=norm_bwd, tiles=[h_in, dres], rows=[row(g)],
                       n_sums=2, rider=rider, tk=tk, name=name)

    G = {}

    u0 = _rmsnorm_fwd(x, row(W["norm_mix_g"][0]), name="norm_mix0")
    proj, qkv = _matmul(u0, W["w_in"], mode="nn", out_dtypes=[F32, MXU_DTYPE], epilogue=lambda acc: (acc, acc), tn=896, name="in_proj")
    o_sb, got = _sb_fwd(qkv, late_weights and late_weights[0], name="sb_fwd")
    if late_weights:
        W = {**W, **late_weights[1](got)}
    hg_out, hg_o, hg_states = _hg_fwd(proj, W["hg_lb_logits"], row(W["hg_norm_g"]), name="hg_fwd")
    mix = jnp.concatenate([o_sb, hg_out], axis=-1)
    h1, u1 = _matmul(mix, W["w_out"], mode="nn", out_dtypes=[F32, MXU_DTYPE], epilogue=residual_norm, tiles=[x],
                     rows=[row(W["norm_ffn_g"][0])], name="out_proj")
    r0 = _matmul(u1, W["w_ff1"][0], mode="nn", out_dtypes=[MXU_DTYPE], epilogue=relu2, name="ff1_0")
    h2, u2 = _matmul(r0, W["w_ff2"][0], mode="nn", out_dtypes=[F32, MXU_DTYPE], epilogue=residual_norm, tiles=[h1],
                     rows=[row(W["norm_mix_g"][1])], name="ff2_0")
    p = _matmul(u2, W["w_glu"], mode="nn", out_dtypes=[F32], epilogue=lambda acc, b: (acc + b,), rows=[row(W["b_glu"])], name="glu_proj")
    w_dw = _pad_rows(W["w_dw"], CONV_HALO)
    ca, cy, cact = _conv_fwd(p, w_dw, row(W["b_dw"]), row(W["ln_g"]), row(W["ln_b"]), name="conv_fwd")
    h3, u3 = _matmul(cact, W["w_pw"], mode="nn", out_dtypes=[F32, MXU_DTYPE], epilogue=lambda acc, res, b, g: residual_norm(acc + b, res, g),
                     tiles=[h2], rows=[row(W["b_pw"]), row(W["norm_ffn_g"][1])], name="pw_proj")
    r1 = _matmul(u3, W["w_ff1"][1], mode="nn", out_dtypes=[MXU_DTYPE], epilogue=relu2, name="ff1_1")
    h4 = _matmul(r1, W["w_ff2"][1], mode="nn", out_dtypes=[F32], epilogue=lambda acc, res: (res + acc,), tiles=[h3], name="ff2_1")

    dh4, dh4_m, G["final_norm_g"], loss = _loss_head(h4, row(W["final_norm_g"]), target, name="loss_head")

    def mlp_bwd(dh, dh_m, h_in, u, r, layer, tag):
        d_relu2 = lambda acc, r_blk: (acc * (2.0 * jnp.sqrt(r_blk.astype(F32))),)
        da = _matmul(dh_m, W["w_ff2"][layer], mode="nt", out_dtypes=[MXU_DTYPE], epilogue=d_relu2, tiles=[r], name="d_ff2_act" + tag)
        dw2 = _matmul(r, dh_m, mode="tn", out_dtypes=[F32], tk=WGRAD_TOKENS, name="d_ff2_w" + tag)
        dw1 = _matmul(u, da, mode="tn", out_dtypes=[F32], tk=WGRAD_TOKENS, name="d_ff1_w" + tag)
        dh_in, dh_in_m, dg, cs = matmul_norm_bwd(da, W["w_ff1"][layer], h_in, W["norm_ffn_g"][layer], dh, name="d_ff1_act" + tag)
        return dh_in, dh_in_m, dg, cs, dw1, dw2

    dh3, dh3_m, dg_ffn1, cs_h3, dw1_1, dw2_1 = mlp_bwd(dh4, dh4_m, h3, u3, r1, 1, "1")
    G["b_pw"] = cs_h3
    dact = _matmul(dh3_m, W["w_pw"], mode="nt", out_dtypes=[F32], name="d_pw_act")
    G["w_pw"] = _matmul(cact, dh3_m, mode="tn", out_dtypes=[F32], tk=WGRAD_TOKENS, name="d_pw_w")
    dy, G["ln_g"], G["ln_b"], G["b_dw"] = _conv_bwd_norm(dact, cy, row(W["ln_g"]), row(W["ln_b"]), name="d_conv_norm")
    dp, G["w_dw"], G["b_glu"] = _conv_bwd_taps(dy, ca, p, w_dw, name="d_conv_taps")
    G["w_glu"] = _matmul(u2, dp, mode="tn", out_dtypes=[F32], tk=WGRAD_TOKENS, name="d_glu_w")
    dh2, dh2_m, dg_mix1, _ = matmul_norm_bwd(dp, W["w_glu"], h2, W["norm_mix_g"][1], dh3, name="d_glu_act")
    dh1, dh1_m, dg_ffn0, _, dw1_0, dw2_0 = mlp_bwd(dh2, dh2_m, h1, u1, r0, 0, "0")
    G["w_ff1"], G["w_ff2"] = jnp.stack([dw1_0, dw1_1]), jnp.stack([dw2_0, dw2_1])
    G["norm_ffn_g"] = jnp.concatenate([dg_ffn0, dg_ffn1], axis=0)
    dmix = _matmul(dh1_m, W["w_out"], mode="nt", out_dtypes=[F32], name="d_out_act")
    G["w_out"] = _matmul(mix, dh1_m, mode="tn", out_dtypes=[F32], tk=WGRAD_TOKENS, name="d_out_w")
    riding = early_grads(G) if early_grads else None
    (dsq, dsk, dsv), got = _sb_bwd(qkv, dmix, riding, name="sb_bwd")
    d_hg, G["hg_lb"], G["hg_norm_g"] = _hg_bwd(proj, hg_o, hg_states, dmix, W["hg_lb_logits"], row(W["hg_norm_g"]), name="hg_bwd")
    dproj = jnp.concatenate([dsq, _mx(dsk), _mx(dsv), d_hg], axis=-1)
    G["w_in"] = _matmul(u0, dproj, mode="tn", out_dtypes=[F32], tn=896, tk=WGRAD_TOKENS, name="d_in_w")
    last = last_grads(G) if last_grads else None
    res = matmul_norm_bwd(dproj, W["w_in"], x, W["norm_mix_g"][0], dh1, last, tk=896, name="d_in_act")
    (dx, _, dg_mix0, _), got_last = res if last_grads else (res, [])
    G["norm_mix_g"] = jnp.concatenate([dg_mix0, dg_mix1], axis=0)
    return loss, dx, G, [(riding, got), (last, got_last)]


BIG = (("w_out_ab", "w_out", "rows2d"), ("conv_w_glu", "w_glu", "cols2d"), ("conv_w_pw", "w_pw", "rows2d"),
       ("w_ff1", "w_ff1", "cols3d"), ("w_ff2", "w_ff2", "rows3d"), ("w_in_ab", "w_in", "cols2d"))
LATE = BIG[:-1]
SMALL_SHARDED = ("conv_b_glu", "conv_w_dw", "conv_b_dw", "conv_ln_g", "conv_ln_b", "conv_b_pw")
REPLICATED = ("norm_mix_g", "norm_ffn_g", "hg_lb_logits", "hg_norm_g", "final_norm_g")
ORDER = ("norm_mix_g", "norm_ffn_g", "w_in_ab", "w_out_ab", "hg_lb_logits", "hg_norm_g", "conv_w_glu", "conv_b_glu",
         "conv_w_dw", "conv_b_dw", "conv_ln_g", "conv_ln_b", "conv_w_pw", "conv_b_pw", "w_ff1", "w_ff2", "final_norm_g")


def _step(x, loss_target, w, m, v):
    D = D_MODEL
    x2, t2 = x.reshape(-1, D), loss_target.reshape(-1, D)
    chip = 2 * lax.axis_index("x") + lax.axis_index("y")
    c = lax.axis_index("c")

    small_in = jnp.concatenate([w["conv_b_glu"].reshape(2, SHARD), w["conv_w_dw"].reshape(CONV_WIDTH, SHARD)] +
                               [w[n].reshape(1, SHARD) for n in ("conv_b_dw", "conv_ln_g", "conv_ln_b", "conv_b_pw")], axis=0)
    halves = lambda a: a.reshape((2, a.shape[-2] // 2, a.shape[-1]))
    g_in, gs = _gather_chips_two_level([halves(w["w_in_ab"].astype(MXU_DTYPE)), halves(_pad_rows(small_in, SMALL_IN_ROWS))],
                                       name="gather_first_weights")
    g_in, gs = g_in.reshape((N_CHIPS,) + w["w_in_ab"].shape), gs.reshape(N_CHIPS, SMALL_IN_ROWS, SHARD)
    vec = lambda r0, r1: gs[:, r0:r1].transpose(1, 0, 2).reshape(r1 - r0, D)
    W = {
        "w_in": _ChipWeight(g_in[:, 0], "cols"),
        "b_glu": gs[:, 0:2].reshape(2 * D), "w_dw": vec(2, 33), "b_dw": vec(33, 34)[0], "ln_g": vec(34, 35)[0],
        "ln_b": vec(35, 36)[0], "b_pw": vec(36, 37)[0],
        "norm_mix_g": w["norm_mix_g"], "norm_ffn_g": w["norm_ffn_g"], "hg_lb_logits": w["hg_lb_logits"],
        "hg_norm_g": w["hg_norm_g"], "final_norm_g": w["final_norm_g"],
    }
    late = _gather_chips([w[n].astype(MXU_DTYPE) for n, _, _ in LATE])

    def assemble(got):
        gw = dict(zip([s for _, s, _ in LATE], late.finish(got)))
        layers = lambda g, along: [_ChipWeight(g, along, (layer,)) for layer in range(2)]
        return {"w_out": gw["w_out"].reshape(D, D), "w_glu": _ChipWeight(gw["w_glu"][:, 0], "cols"), "w_pw": gw["w_pw"].reshape(D, D),
                "w_ff1": layers(gw["w_ff1"], "cols"), "w_ff2": layers(gw["w_ff2"], "rows")}

    def early_grads(G):
        return _owner_scatter(jnp.concatenate([_grad_blocks(G[s], kind) for _, s, kind in LATE], axis=2))

    def last_grads(G):
        blocks = _grad_blocks(G["w_in"], "cols2d")
        from_pair = _pair_swap(blocks.transpose(1, 0, 2, 3)).run("grads_pair_swap_in")
        return _chip_scatter(_add_pair(lax.dynamic_index_in_dim(blocks, c, axis=1, keepdims=False), from_pair, name="grads_pair_add_in"))

    loss, dx, G, riders = _forward_backward(x2, t2, W, (late, assemble), early_grads, last_grads)
    halves = [_sum_leading(copies.finish(got), name="grads_add_" + tag) for (copies, got), tag in zip(riders, ("late", "in"))]
    half = jnp.concatenate(halves, axis=0)
    full = _pair_gather(half).run("grads_pair_gather")

    pack = jnp.concatenate([
        G["norm_mix_g"], G["norm_ffn_g"], G["final_norm_g"], jnp.concatenate([G["hg_norm_g"], G["hg_lb"]], axis=1),
        _pad_rows(jnp.broadcast_to(loss, (1, D)), 2), G["b_glu"].reshape(2, D), G["w_dw"], G["b_dw"], G["ln_g"], G["ln_b"], G["b_pw"],
    ], axis=0)
    pack = _pad_rows(pack, SMALL_ROWS)
    (packs,) = _gather_all([pack]).run("gather_small_grads")
    ssum, d_logits = _small_reduce(packs, w["hg_lb_logits"], name="reduce_small_grads")
    cut = lambda r0, r1: lax.dynamic_slice(ssum, (r0, chip * SHARD), (r1 - r0, SHARD))
    grads = {
        "norm_mix_g": ssum[0:2], "norm_ffn_g": ssum[2:4], "final_norm_g": ssum[4], "hg_norm_g": ssum[5, :HG_WIDTH].reshape(1, HG_HEADS, HG_DH),
        "hg_lb_logits": d_logits,
        "conv_b_glu": lax.dynamic_slice(ssum[8:10].reshape(1, 2 * D), (0, chip * 2 * SHARD), (1, 2 * SHARD)),
        "conv_w_dw": cut(10, 10 + CONV_WIDTH).reshape(1, CONV_WIDTH, SHARD),
        "conv_b_dw": cut(42, 43), "conv_ln_g": cut(43, 44), "conv_ln_b": cut(44, 45), "conv_b_pw": cut(45, 46),
    }
    loss_out = ssum[6, 0]

    off = 0
    for n, s, kind in BIG:
        shard = w[n].shape
        rows = w[n].size // (2 * D)
        grads[n] = full[:, off:off + rows].reshape(shard)
        off += rows

    delta, new_m, new_v = {}, {}, {}
    for n, _, _ in BIG:
        view = lambda a: a.reshape(-1, a.shape[-1])
        outs = _adamw(view(w[n]), view(grads[n]), view(m[n]), view(v[n]), name="adamw_" + n)
        delta[n], new_m[n], new_v[n] = (o.reshape(w[n].shape) for o in outs)
    small = SMALL_SHARDED + REPLICATED
    sizes = [w[n].size for n in small]
    total = sum(sizes)
    rows = -(-total // (8 * D)) * 8
    packed = lambda d: _pad_rows(jnp.concatenate([d[n].reshape(-1) for n in small]).reshape(-1, 128), rows * 8).reshape(rows, D)
    outs = _adamw(packed(w), packed(grads), packed(m), packed(v), name="adamw_small")
    off = 0
    for n, size in zip(small, sizes):
        delta[n], new_m[n], new_v[n] = (o.reshape(-1)[off:off + size].reshape(w[n].shape) for o in outs)
        off += size
    grads = {n: grads[n].reshape(w[n].shape) for n in ORDER}
    return (loss_out, dx.reshape(x.shape), *[grads[n] for n in ORDER], *[delta[n] for n in ORDER],
            *[new_m[n] for n in ORDER], *[new_v[n] for n in ORDER])


def kernel(x, norm_mix_g, norm_ffn_g, w_in_ab, w_out_ab, hg_lb_logits, hg_norm_g, conv_w_glu, conv_b_glu, conv_w_dw, conv_b_dw, conv_ln_g, conv_ln_b, conv_w_pw, conv_b_pw, w_ff1, w_ff2, final_norm_g, loss_target, m_norm_mix_g, m_norm_ffn_g, m_w_in_ab, m_w_out_ab, m_hg_lb_logits, m_hg_norm_g, m_conv_w_glu, m_conv_b_glu, m_conv_w_dw, m_conv_b_dw, m_conv_ln_g, m_conv_ln_b, m_conv_w_pw, m_conv_b_pw, m_w_ff1, m_w_ff2, m_final_norm_g, v_norm_mix_g, v_norm_ffn_g, v_w_in_ab, v_w_out_ab, v_hg_lb_logits, v_hg_norm_g, v_conv_w_glu, v_conv_b_glu, v_conv_w_dw, v_conv_b_dw, v_conv_ln_g, v_conv_ln_b, v_conv_w_pw, v_conv_b_pw, v_w_ff1, v_w_ff2, v_final_norm_g):
    args = locals()
    w = {n: args[n] for n in ORDER}
    m = {n: args["m_" + n] for n in ORDER}
    v = {n: args["v_" + n] for n in ORDER}
    return _step(x, loss_target, w, m, v)
```
